```python
import jax, jax.numpy as jnp
from jax import lax
import numpy as np

D_MODEL = 1024
BATCH = 16
SEQ = 2048
DEPTH = 1
DEC_BATCH = 16
DEC_SEQ = 64
PAST_LEN = 4096

CHUNK = 64
N_META = 16
EPS = 1e-6
D_POOL = D_MODEL // 2
POOL_WINDOWS = (2, 4, 8, 16)
N_POOL_GROUPS = len(POOL_WINDOWS)
POOL_GROUP_DIM = D_POOL // N_POOL_GROUPS
POOL_PAD = max(POOL_WINDOWS) - 1
GLA_HEADS = 4
GLA_DK = D_MODEL // 16
GLA_DV = D_MODEL // 8
D_QK = GLA_HEADS * GLA_DK
D_V = GLA_HEADS * GLA_DV
GATE_RANK = 16
GATE_TAU = 16.0
D_MIX = D_POOL + D_V
IN_SPLITS = (D_POOL, D_POOL + D_QK, D_POOL + 2 * D_QK, D_POOL + 2 * D_QK + D_V, D_POOL + 2 * D_QK + 2 * D_V)
D_IN = D_POOL + 2 * D_QK + 2 * D_V + GATE_RANK
N_GROUPS = 4
EXPERTS_PER_GROUP = 8
N_EXPERTS = N_GROUPS * EXPERTS_PER_GROUP
D_EXPERT = D_MODEL // 2
MOE_BLOCK = 128

kernel_name = 'hybrid_pool_gla_hiermoe_stream_step'


def rmsnorm(x, g):
    xf = x.astype(jnp.float32)
    y = xf * lax.rsqrt(jnp.mean(xf * xf, axis=-1, keepdims=True) + EPS)
    return y * g.astype(jnp.float32)


def pool_mix(u, prefix, n_valid_prefix, w_pool, pool_scale):
    B, L, _ = u.shape
    ext = jnp.concatenate([prefix.astype(u.dtype), u], axis=1)
    cs = jnp.cumsum(ext.astype(jnp.float32), axis=1)
    cs = jnp.pad(cs, ((0, 0), (1, 0), (0, 0)))
    t = jnp.arange(L)
    outs = []
    for g, w in enumerate(POOL_WINDOWS):
        sl = slice(g * POOL_GROUP_DIM, (g + 1) * POOL_GROUP_DIM)
        s = cs[:, POOL_PAD + 1:POOL_PAD + 1 + L, sl] - cs[:, POOL_PAD + 1 - w:POOL_PAD + 1 - w + L, sl]
        cnt = jnp.minimum(t + n_valid_prefix + 1, w).astype(jnp.float32)
        outs.append(s / cnt[None, :, None])
    pooled = jnp.concatenate(outs, axis=-1) - u.astype(jnp.float32)
    pooled = pooled.reshape(B, L, N_POOL_GROUPS, POOL_GROUP_DIM)
    y = jnp.einsum('blgc,gcd->blgd', pooled, w_pool.astype(jnp.float32)).reshape(B, L, D_POOL)
    return y * pool_scale.astype(jnp.float32), ext[:, -POOL_PAD:]


def gla_chunked(q, k, v, log_a, s0, chunk):
    B, L, H, DK = q.shape
    DV = v.shape[-1]
    N = L // chunk
    q = q.reshape(B, N, chunk, H, DK)
    k = k.reshape(B, N, chunk, H, DK)
    v = v.reshape(B, N, chunk, H, DV)
    b = jnp.cumsum(log_a.reshape(B, N, chunk, H, DK), axis=2)
    b_last = b[:, :, -1]
    q_i = q * jnp.exp(b)
    k_i = k * jnp.exp(-b)
    scores = jnp.einsum('bnchk,bnshk->bnhcs', q_i, k_i)
    mask = jnp.tril(jnp.ones((chunk, chunk), dtype=bool))
    scores = jnp.where(mask, scores, 0.0)
    o_intra = jnp.einsum('bnhcs,bnshv->bnchv', scores, v)
    k_dec = k * jnp.exp(b_last[:, :, None] - b)
    kv = jnp.einsum('bnchk,bnchv->bnhkv', k_dec, v)
    decay = jnp.exp(b_last)

    def step(S, inp):
        dec, kvn = inp
        return dec[..., None] * S + kvn, S

    s_fin, s_in = lax.scan(step, s0, (jnp.moveaxis(decay, 1, 0), jnp.moveaxis(kv, 1, 0)))
    s_in = jnp.moveaxis(s_in, 0, 1)
    o_inter = jnp.einsum('bnchk,bnhkv->bnchv', q_i, s_in)
    return (o_intra + o_inter).reshape(B, L, H, DV), s_fin


def token_mixer(xn, pool_prefix, n_valid_prefix, s0, chunk, lead_pad,
                w_in, w_gate_up, b_gate, w_pool, pool_scale, gla_norm, w_out):
    B, L, _ = xn.shape
    f32 = jnp.float32
    proj = xn @ w_in
    u, q, k, v, r, z = jnp.split(proj, IN_SPLITS, axis=-1)
    pool_y, pool_state = pool_mix(u, pool_prefix, n_valid_prefix, w_pool, pool_scale)
    log_a = jax.nn.log_sigmoid(z.astype(f32) @ w_gate_up.astype(f32) + b_gate.astype(f32)) / GATE_TAU
    qh = q.astype(f32).reshape(B, L, GLA_HEADS, GLA_DK) * (GLA_DK ** -0.5)
    kh = k.astype(f32).reshape(B, L, GLA_HEADS, GLA_DK)
    vh = v.astype(f32).reshape(B, L, GLA_HEADS, GLA_DV)
    ah = log_a.reshape(B, L, GLA_HEADS, GLA_DK)
    if lead_pad:
        padw = ((0, 0), (lead_pad, 0), (0, 0), (0, 0))
        qh, kh, vh, ah = jnp.pad(qh, padw), jnp.pad(kh, padw), jnp.pad(vh, padw), jnp.pad(ah, padw)
    o, s_new = gla_chunked(qh, kh, vh, ah, s0.astype(f32), chunk)
    o = o[:, lead_pad:]
    o = o * lax.rsqrt(jnp.mean(o * o, axis=-1, keepdims=True) + EPS) * gla_norm.astype(f32).reshape(GLA_HEADS, GLA_DV)
    o = o.reshape(B, L, D_V) * jax.nn.silu(r.astype(f32))
    mix = jnp.concatenate([pool_y, o], axis=-1).astype(xn.dtype) @ w_out
    return mix, pool_state, s_new


def hier_moe(x, w_rg, b_rg, w_re, b_re, w_eg, w_eu, w_ed):
    T, D = x.shape
    xf = x.astype(jnp.float32)
    p_group = jax.nn.softmax(xf @ w_rg.astype(jnp.float32) + b_rg.astype(jnp.float32), axis=-1)
    g = jnp.argmax(p_group, axis=-1)
    p_g = jnp.max(p_group, axis=-1, keepdims=True)
    le = (xf @ w_re.astype(jnp.float32) + b_re.astype(jnp.float32)).reshape(T, N_GROUPS, EXPERTS_PER_GROUP)
    le_sel = jnp.einsum('tge,tg->te', le, jax.nn.one_hot(g, N_GROUPS, dtype=jnp.float32))
    top_v, top_i = lax.top_k(jax.nn.softmax(le_sel, axis=-1), 2)
    gates = p_g * top_v / jnp.sum(top_v, axis=-1, keepdims=True)
    expert = g[:, None].astype(jnp.int32) * EXPERTS_PER_GROUP + top_i.astype(jnp.int32)
    A = 2 * T
    flat_e = expert.reshape(-1)
    flat_gate = gates.reshape(-1)
    order = jnp.argsort(flat_e)
    sorted_e = flat_e[order]
    counts = jnp.bincount(flat_e, length=N_EXPERTS)
    padded = (counts + MOE_BLOCK - 1) // MOE_BLOCK * MOE_BLOCK
    start = jnp.cumsum(counts) - counts
    ends = jnp.cumsum(padded)
    pstart = ends - padded
    dest = pstart[sorted_e] + jnp.arange(A) - start[sorted_e]
    R = (A + N_EXPERTS * (MOE_BLOCK - 1) + MOE_BLOCK - 1) // MOE_BLOCK * MOE_BLOCK
    NB = R // MOE_BLOCK
    slot_tok = jnp.full((R,), T, jnp.int32).at[dest].set((order // 2).astype(jnp.int32))
    slot_gate = jnp.zeros((R,), jnp.float32).at[dest].set(flat_gate[order])
    block_e = jnp.clip(jnp.searchsorted(ends, jnp.arange(NB) * MOE_BLOCK, side='right'), 0, N_EXPERTS - 1)
    x_pad = jnp.concatenate([x, jnp.zeros((1, D), x.dtype)], axis=0)
    xs = x_pad[slot_tok].reshape(NB, MOE_BLOCK, D)

    def expert_block(args):
        xb, e = args
        h = jax.nn.silu(xb @ w_eg[e]) * (xb @ w_eu[e])
        return h @ w_ed[e]

    ys = lax.map(expert_block, (xs, block_e)).reshape(R, D)
    ys = ys.astype(jnp.float32) * slot_gate[:, None]
    out = jax.ops.segment_sum(ys, slot_tok, num_segments=T + 1)[:T]
    return out.astype(x.dtype)


def encoder_layer(x, pool_prefix, n_valid_prefix, s0, chunk, lead_pad,
                  norm_mix, w_in, w_gate_up, b_gate, w_pool, pool_scale, gla_norm, w_out,
                  norm_ffn, w_rg, b_rg, w_re, b_re, w_eg, w_eu, w_ed):
    B, L, D = x.shape
    xn = rmsnorm(x, norm_mix).astype(x.dtype)
    mix, pool_state, s_new = token_mixer(xn, pool_prefix, n_valid_prefix, s0, chunk, lead_pad,
                                         w_in, w_gate_up, b_gate, w_pool, pool_scale, gla_norm, w_out)
    h = x + mix.astype(x.dtype)
    hn = rmsnorm(h, norm_ffn).astype(x.dtype).reshape(B * L, D)
    h = h + hier_moe(hn, w_rg, b_rg, w_re, b_re, w_eg, w_eu, w_ed).reshape(B, L, D)
    return h, pool_state, s_new


def setup_inputs(seed: int = 0) -> dict:
    key = jax.random.key(seed)
    ks = jax.random.split(key, 24)
    n = lambda i, shape: jax.random.normal(ks[i], shape, jnp.float32)
    return {
        'x_prompt': n(0, (BATCH, SEQ, D_MODEL)),
        'x_sample': n(1, (DEC_BATCH, DEC_SEQ, D_MODEL)),
        'state_pool': n(2, (DEPTH, DEC_BATCH, POOL_PAD, D_POOL)),
        'state_gla': n(3, (DEPTH, DEC_BATCH, GLA_HEADS, GLA_DK, GLA_DV)),
        'meta_tokens': n(4, (N_META, D_MODEL)),
        'norm_mix': 1.0 + 0.02 * n(5, (DEPTH, D_MODEL)),
        'w_in': n(6, (DEPTH, D_MODEL, D_IN)) * D_MODEL ** -0.5,
        'w_gate_up': n(7, (DEPTH, GATE_RANK, D_QK)) * GATE_RANK ** -0.5,
        'b_gate': 0.1 * n(8, (DEPTH, D_QK)),
        'w_pool': n(9, (DEPTH, N_POOL_GROUPS, POOL_GROUP_DIM, POOL_GROUP_DIM)) * POOL_GROUP_DIM ** -0.5,
        'pool_scale': 1.0 + 0.02 * n(10, (DEPTH, D_POOL)),
        'gla_norm': 1.0 + 0.02 * n(11, (DEPTH, D_V)),
        'w_out': n(12, (DEPTH, D_MIX, D_MODEL)) * D_MIX ** -0.5,
        'norm_ffn': 1.0 + 0.02 * n(13, (DEPTH, D_MODEL)),
        'w_router_group': n(14, (DEPTH, D_MODEL, N_GROUPS)) * D_MODEL ** -0.5,
        'b_router_group': 0.01 * n(15, (DEPTH, N_GROUPS)),
        'w_router_expert': n(16, (DEPTH, D_MODEL, N_EXPERTS)) * D_MODEL ** -0.5,
        'b_router_expert': 0.01 * n(17, (DEPTH, N_EXPERTS)),
        'w_expert_gate': n(18, (DEPTH, N_EXPERTS, D_MODEL, D_EXPERT)) * D_MODEL ** -0.5,
        'w_expert_up': n(19, (DEPTH, N_EXPERTS, D_MODEL, D_EXPERT)) * D_MODEL ** -0.5,
        'w_expert_down': n(20, (DEPTH, N_EXPERTS, D_EXPERT, D_MODEL)) * D_EXPERT ** -0.5,
        'norm_final': 1.0 + 0.02 * n(21, (D_MODEL,)),
    }


def reference(x_prompt, x_sample, state_pool, state_gla, meta_tokens, norm_mix, w_in, w_gate_up, b_gate,
              w_pool, pool_scale, gla_norm, w_out, norm_ffn, w_router_group, b_router_group,
              w_router_expert, b_router_expert, w_expert_gate, w_expert_up, w_expert_down, norm_final):
    B = x_prompt.shape[0]
    meta = jnp.broadcast_to(meta_tokens.astype(x_prompt.dtype)[None], (B, N_META, D_MODEL))
    hp = jnp.concatenate([meta, x_prompt], axis=1)
    hs = x_sample
    zero_pool = jnp.zeros((B, POOL_PAD, D_POOL), x_prompt.dtype)
    zero_gla = jnp.zeros((B, GLA_HEADS, GLA_DK, GLA_DV), jnp.float32)
    pool_p, gla_p, pool_s, gla_s = [], [], [], []
    for l in range(DEPTH):
        lw = (norm_mix[l], w_in[l], w_gate_up[l], b_gate[l], w_pool[l], pool_scale[l], gla_norm[l], w_out[l],
              norm_ffn[l], w_router_group[l], b_router_group[l], w_router_expert[l], b_router_expert[l],
              w_expert_gate[l], w_expert_up[l], w_expert_down[l])
        hp, sp, gp = encoder_layer(hp, zero_pool, 0, zero_gla, CHUNK, CHUNK - N_META, *lw)
        hs, ss, gs = encoder_layer(hs, state_pool[l], POOL_PAD, state_gla[l], hs.shape[1], 0, *lw)
        pool_p.append(sp)
        gla_p.append(gp)
        pool_s.append(ss)
        gla_s.append(gs)
    y_prompt = rmsnorm(hp[:, N_META:], norm_final).astype(x_prompt.dtype)
    y_sample = rmsnorm(hs, norm_final).astype(x_sample.dtype)
    return (y_prompt, y_sample, jnp.stack(pool_p), jnp.stack(gla_p), jnp.stack(pool_s), jnp.stack(gla_s))
```

```python
import functools

import jax
import jax.numpy as jnp
from jax import lax
from jax.experimental import pallas as pl
from jax.experimental.pallas import tpu as pltpu

F32 = jnp.float32
BF16 = jnp.bfloat16
U32 = jnp.uint32
I32 = jnp.int32

D_MODEL = 1024
N_META = 16
CHUNK = 64
EPS = 1e-6
D_POOL = 512
POOL_WINDOWS = (2, 4, 8, 16)
POOL_GROUP_DIM = 128
POOL_PAD = 15
POOL_ROWS = 16
GLA_HEADS = 4
GLA_DK = 64
GLA_DV = 128
D_QK = 256
D_V = 512
GATE_RANK = 16
GATE_TAU = 16.0
D_MAIN = D_POOL + 2 * D_QK + 2 * D_V
N_GROUPS = 4
EXPERTS_PER_GROUP = 8
N_EXPERTS = 32
D_EXPERT = 512

LANES = 128
HALF = D_MODEL // 2
MIX_TB = 256
MOE_BLK = 256
DISPATCH_TB = 1024
COMBINE_TB = 256
VMEM_LIMIT = 48 * 1024 * 1024


def _rms(x, g):
    return x * lax.rsqrt(jnp.mean(x * x, axis=-1, keepdims=True) + EPS) * g


def _dot(a, b):
    return jnp.dot(a, b, preferred_element_type=F32)


def _dot_nt(a, b):
    return lax.dot_general(a, b, (((1,), (1,)), ((), ())), preferred_element_type=F32)


def _dot_tn(a, b):
    return lax.dot_general(a, b, (((0,), (0,)), ((), ())), preferred_element_type=F32)


def _mixer_kernel(x_ref, pool0_ref, st0_ref, cnt0_ref, tril_ref, slow_ref,
                  nmix_ref, wmain_ref, wz_ref, wgu_ref, bgate_ref, wpool_ref, pscale_ref,
                  gnorm_ref, wout_ref, nffn_ref, wr_ref, br_ref,
                  h_ref, hn_ref, route_ref, pool_out_ref, st_out_ref, cnt_out_ref,
                  ext_ref, st_ref, o_ref, cnt_ref, *, tb, lead_pad):
    b = pl.program_id(0)
    j = pl.program_id(1)
    nj = pl.num_programs(1)

    @pl.when(j == 0)
    def _():
        ext_ref[0:POOL_ROWS, :] = pool0_ref[0]
        st_ref[...] = st0_ref[0]

    @pl.when((b == 0) & (j == 0))
    def _():
        cnt_ref[...] = cnt0_ref[...]

    x = x_ref[...]
    xn = _rms(x, nmix_ref[...]).astype(BF16)
    proj = _dot(xn, wmain_ref[...])
    z = _dot(xn, wz_ref[...])
    u = proj[:, 0:D_POOL]
    q = proj[:, D_POOL:D_POOL + D_QK]
    k = proj[:, D_POOL + D_QK:D_POOL + 2 * D_QK]
    v = proj[:, D_POOL + 2 * D_QK:D_POOL + 2 * D_QK + D_V]
    r = proj[:, D_POOL + 2 * D_QK + D_V:D_MAIN]

    row = lax.broadcasted_iota(I32, (tb, 1), 0)

    ext_ref[POOL_ROWS:POOL_ROWS + tb, :] = u
    ext = ext_ref[...]
    pys = []
    for g, w in enumerate(POOL_WINDOWS):
        sl = slice(g * POOL_GROUP_DIM, (g + 1) * POOL_GROUP_DIM)
        acc = ext[:, sl]
        for s in range(g + 1):
            acc = acc + pltpu.roll(acc, 1 << s, axis=0)
        win = acc[POOL_ROWS:, :]
        if lead_pad:
            cnt = jnp.clip(row - lead_pad + 1, 1, w).astype(F32)
            pooled = win / cnt - u[:, sl]
        else:
            pooled = win * (1.0 / w) - u[:, sl]
        pys.append(_dot(pooled.astype(BF16), wpool_ref[g]))
    pool_y = jnp.concatenate(pys, axis=-1) * pscale_ref[...]
    ext_ref[0:POOL_ROWS, :] = ext_ref[tb:tb + POOL_ROWS, :]

    gpre = _dot(z.astype(BF16), wgu_ref[...]) + bgate_ref[...]
    log_a = jax.nn.log_sigmoid(gpre) * (1.0 / GATE_TAU)
    if lead_pad:
        log_a = jnp.where(row >= lead_pad, log_a, 0.0)
    a_hi = log_a.astype(BF16)
    a_lo = (log_a - a_hi.astype(F32)).astype(BF16)
    tril = tril_ref[...]
    bcum = _dot(tril, a_hi) + _dot(tril, a_lo)
    eb = jnp.exp(bcum)
    qi = q * (GLA_DK ** -0.5) * eb
    ki = k * jnp.exp(-bcum)

    rr = lax.broadcasted_iota(I32, (GLA_HEADS * CHUNK, D_QK), 0)
    cc = lax.broadcasted_iota(I32, (GLA_HEADS * CHUNK, D_QK), 1)
    kbd_mask = (rr // CHUNK) == (cc // GLA_DK)
    rr = lax.broadcasted_iota(I32, (GLA_HEADS * CHUNK, D_V), 0)
    cc = lax.broadcasted_iota(I32, (GLA_HEADS * CHUNK, D_V), 1)
    vbd_mask = (rr // CHUNK) == (cc // GLA_DV)
    rr = lax.broadcasted_iota(I32, (D_V, D_QK), 0)
    cc = lax.broadcasted_iota(I32, (D_V, D_QK), 1)
    st_mask = (rr // GLA_DV) == (cc // GLA_DK)
    rr = lax.broadcasted_iota(I32, (CHUNK, GLA_HEADS * CHUNK), 0)
    cc = lax.broadcasted_iota(I32, (CHUNK, GLA_HEADS * CHUNK), 1)
    causal = (cc % CHUNK) <= rr

    for c in range(tb // CHUNK):
        rows = slice(c * CHUNK, (c + 1) * CHUNK)
        eb_last = eb[(c + 1) * CHUNK - 1:(c + 1) * CHUNK, :]
        qi_c = qi[rows].astype(BF16)
        ki_c = ki[rows]
        kd_c = (ki_c * eb_last).astype(BF16)
        v_c = v[rows]
        kbd = jnp.where(kbd_mask, jnp.concatenate([ki_c] * GLA_HEADS, axis=0), 0.0).astype(BF16)
        scores = _dot_nt(qi_c, kbd)
        p = jnp.where(causal, scores, 0.0).astype(BF16)
        vbd = jnp.where(vbd_mask, jnp.concatenate([v_c] * GLA_HEADS, axis=0), 0.0).astype(BF16)
        st = st_ref[...]
        o_ref[rows, :] = _dot(p, vbd) + _dot_nt(qi_c, st.astype(BF16))
        kvt = _dot_tn(v_c.astype(BF16), kd_c)
        st_ref[...] = st * eb_last + jnp.where(st_mask, kvt, 0.0)

    o = o_ref[...]
    ons = []
    for hh in range(GLA_HEADS):
        oh = o[:, hh * GLA_DV:(hh + 1) * GLA_DV]
        ons.append(oh * lax.rsqrt(jnp.mean(oh * oh, axis=-1, keepdims=True) + EPS))
    og = jnp.concatenate(ons, axis=-1) * gnorm_ref[...] * (r * jax.nn.sigmoid(r))
    mix = _dot(jnp.concatenate([pool_y, og], axis=-1).astype(BF16), wout_ref[...])

    h = x + mix
    h_ref[...] = h
    hn = _rms(h, nffn_ref[...]).astype(BF16)
    lo = lax.bitcast_convert_type(hn[:, :HALF].astype(F32), U32) >> 16
    hi = lax.bitcast_convert_type(hn[:, HALF:].astype(F32), U32) & jnp.uint32(0xFFFF0000)
    hn_ref[...] = lo | hi

    logits = _dot(hn, wr_ref[...]) + br_ref[...]
    lane = lax.broadcasted_iota(I32, (tb, LANES), 1)
    lanef = lane.astype(F32)
    neg = jnp.float32(-jnp.inf)
    big = jnp.float32(LANES)
    is_g = lane < N_GROUPS
    gl = jnp.where(is_g, logits, neg)
    gmax = jnp.max(gl, axis=-1, keepdims=True)
    gsum = jnp.sum(jnp.where(is_g, jnp.exp(logits - gmax), 0.0), axis=-1, keepdims=True)
    p_g = 1.0 / gsum
    gidx = jnp.min(jnp.where(is_g & (logits == gmax), lanef, big), axis=-1, keepdims=True)
    e_lo = N_GROUPS + EXPERTS_PER_GROUP * gidx
    is_e = (lanef >= e_lo) & (lanef < e_lo + EXPERTS_PER_GROUP)
    el = jnp.where(is_e, logits, neg)
    m1 = jnp.max(el, axis=-1, keepdims=True)
    i1 = jnp.min(jnp.where(is_e & (logits == m1), lanef, big), axis=-1, keepdims=True)
    is_e2 = is_e & (lanef != i1)
    el2 = jnp.where(is_e2, logits, neg)
    m2 = jnp.max(el2, axis=-1, keepdims=True)
    i2 = jnp.min(jnp.where(is_e2 & (logits == m2), lanef, big), axis=-1, keepdims=True)
    t2 = jnp.exp(m2 - m1)
    den = 1.0 + t2
    g1 = p_g / den
    g2 = p_g * t2 / den
    e1 = i1 - N_GROUPS
    e2 = i2 - N_GROUPS

    oh1 = lanef == e1
    oh2 = lanef == e2
    both = jnp.where(oh1 | oh2, 1.0, 0.0)
    before = _dot(slow_ref[...], both.astype(BF16)) + cnt_ref[...]
    pos1 = jnp.sum(jnp.where(oh1, before, 0.0), axis=-1, keepdims=True)
    pos2 = jnp.sum(jnp.where(oh2, before, 0.0), axis=-1, keepdims=True)
    cnt_new = cnt_ref[...] + jnp.sum(both, axis=0, keepdims=True)
    cnt_ref[...] = cnt_new
    cnt_out_ref[...] = cnt_new

    route = jnp.where(lane == 0, e1, 0.0)
    for idx, val in ((1, e2), (2, g1), (3, g2), (4, pos1), (5, pos2)):
        route = jnp.where(lane == idx, val, route)
    route_ref[...] = route

    @pl.when(j == nj - 1)
    def _():
        pool_out_ref[0] = ext_ref[0:POOL_ROWS, :]
        st_out_ref[0] = st_ref[...]


def _mixer(x2d, pool0, st0, cnt0, weights, *, batch, seq, tb, lead_pad):
    nj = seq // tb
    total_rows = batch * seq
    shared = pool0.shape[0] == 1
    st_idx = (lambda b, j: (0, 0, 0)) if shared else (lambda b, j: (b, 0, 0))
    const2 = lambda b, j: (0, 0)
    tok = lambda b, j: (b * nj + j, 0)
    tok_out = tok

    ii = jnp.arange(tb)
    tril = ((ii[:, None] >= ii[None, :]) & (ii[:, None] // CHUNK == ii[None, :] // CHUNK)).astype(BF16)
    slow = (ii[:, None] > ii[None, :]).astype(BF16)

    in_specs = [
        pl.BlockSpec((tb, D_MODEL), tok),
        pl.BlockSpec((1, POOL_ROWS, D_POOL), st_idx),
        pl.BlockSpec((1, D_V, D_QK), st_idx),
        pl.BlockSpec((1, LANES), const2),
        pl.BlockSpec((tb, tb), const2),
        pl.BlockSpec((tb, tb), const2),
    ]
    for wgt in weights:
        in_specs.append(pl.BlockSpec(wgt.shape, (lambda b, j, n=wgt.ndim: (0,) * n)))
    args = [x2d, pool0, st0, cnt0, tril, slow, *weights]

    out_shape = [
        jax.ShapeDtypeStruct((total_rows, D_MODEL), F32),
        jax.ShapeDtypeStruct((total_rows, HALF), U32),
        jax.ShapeDtypeStruct((total_rows, LANES), F32),
        jax.ShapeDtypeStruct((batch, POOL_ROWS, D_POOL), F32),
        jax.ShapeDtypeStruct((batch, D_V, D_QK), F32),
        jax.ShapeDtypeStruct((1, LANES), F32),
    ]
    out_specs = [
        pl.BlockSpec((tb, D_MODEL), tok_out),
        pl.BlockSpec((tb, HALF), tok_out),
        pl.BlockSpec((tb, LANES), tok_out),
        pl.BlockSpec((1, POOL_ROWS, D_POOL), lambda b, j: (b, 0, 0)),
        pl.BlockSpec((1, D_V, D_QK), lambda b, j: (b, 0, 0)),
        pl.BlockSpec((1, LANES), const2),
    ]
    scratch = [
        pltpu.VMEM((POOL_ROWS + tb, D_POOL), F32),
        pltpu.VMEM((D_V, D_QK), F32),
        pltpu.VMEM((tb, D_V), F32),
        pltpu.VMEM((1, LANES), F32),
    ]
    return pl.pallas_call(
        functools.partial(_mixer_kernel, tb=tb, lead_pad=lead_pad),
        grid=(batch, nj),
        in_specs=in_specs,
        out_specs=out_specs,
        out_shape=out_shape,
        scratch_shapes=scratch,
        compiler_params=pltpu.CompilerParams(
            dimension_semantics=("arbitrary", "arbitrary"), vmem_limit_bytes=VMEM_LIMIT),
        name=f"mixer_tb{tb}_pad{lead_pad}",
    )(*args)


def _dispatch_kernel(dest_ref, hn_ref, xs_in_ref, xs_ref, sem, *, tb):
    del xs_in_ref
    i = pl.program_id(0)

    def row_copy(t, d):
        return pltpu.make_async_copy(hn_ref.at[pl.ds(t, 1)], xs_ref.at[pl.ds(d, 1)], sem)

    def start(rr, carry):
        t = i * tb + rr
        row_copy(t, dest_ref[0, 0, 2 * rr]).start()
        row_copy(t, dest_ref[0, 0, 2 * rr + 1]).start()
        return carry

    lax.fori_loop(0, tb, start, 0)

    def wait(rr, carry):
        row_copy(0, 0).wait()
        row_copy(0, 0).wait()
        return carry

    lax.fori_loop(0, tb, wait, 0)


def _dispatch(dest, hn, xs0):
    n_tok = hn.shape[0]
    n_rows = xs0.shape[0]
    tb = DISPATCH_TB
    dest3 = dest.reshape(n_tok // tb, 1, 2 * tb)
    return pl.pallas_call(
        functools.partial(_dispatch_kernel, tb=tb),
        grid=(n_tok // tb,),
        in_specs=[
            pl.BlockSpec((1, 1, 2 * tb), lambda i: (i, 0, 0), memory_space=pltpu.SMEM),
            pl.BlockSpec(memory_space=pl.ANY),
            pl.BlockSpec(memory_space=pl.ANY),
        ],
        out_specs=pl.BlockSpec(memory_space=pl.ANY),
        out_shape=jax.ShapeDtypeStruct((n_rows, HALF), U32),
        scratch_shapes=[pltpu.SemaphoreType.DMA],
        input_output_aliases={2: 0},
        compiler_params=pltpu.CompilerParams(dimension_semantics=("arbitrary",)),
        name="moe_dispatch",
    )(dest3, hn, xs0)


def _expert_kernel(be_ref, nb_ref, xs_ref, wg_ref, wu_ref, wd_ref, ys_ref, wgu_s, wd_s):
    i = pl.program_id(0)
    live = i < nb_ref[0]
    prev = be_ref[jnp.maximum(i - 1, 0)]
    fresh = (i == 0) | (be_ref[i] != prev)

    @pl.when(live & fresh)
    def _():
        wgu_s[:, 0:D_EXPERT] = wg_ref[0].astype(BF16)
        wgu_s[:, D_EXPERT:2 * D_EXPERT] = wu_ref[0].astype(BF16)
        wd_s[...] = wd_ref[0].astype(BF16)

    @pl.when(live)
    def _():
        x32 = xs_ref[...]
        lo = lax.bitcast_convert_type(x32 << 16, F32).astype(BF16)
        hi = lax.bitcast_convert_type(x32 & jnp.uint32(0xFFFF0000), F32).astype(BF16)
        xb = jnp.concatenate([lo, hi], axis=-1)
        gu = _dot(xb, wgu_s[...])
        gate = gu[:, 0:D_EXPERT]
        hmid = gate * jax.nn.sigmoid(gate) * gu[:, D_EXPERT:]
        ys_ref[...] = _dot(hmid.astype(BF16), wd_s[...])

    @pl.when(jnp.logical_not(live))
    def _():
        ys_ref[...] = jnp.zeros_like(ys_ref)


def _experts(block_e, nb, xs, w_eg, w_eu, w_ed):
    n_blocks = xs.shape[0] // MOE_BLK
    row_idx = lambda i, be, nb: (jnp.minimum(i, nb[0] - 1), 0)
    w_idx = lambda i, be, nb: (be[i], 0, 0)
    grid_spec = pltpu.PrefetchScalarGridSpec(
        num_scalar_prefetch=2,
        grid=(n_blocks,),
        in_specs=[
            pl.BlockSpec((MOE_BLK, HALF), row_idx),
            pl.BlockSpec((1, D_MODEL, D_EXPERT), w_idx),
            pl.BlockSpec((1, D_MODEL, D_EXPERT), w_idx),
            pl.BlockSpec((1, D_EXPERT, D_MODEL), w_idx),
        ],
        out_specs=pl.BlockSpec((MOE_BLK, D_MODEL), lambda i, be, nb: (i, 0)),
        scratch_shapes=[
            pltpu.VMEM((D_MODEL, 2 * D_EXPERT), BF16),
            pltpu.VMEM((D_EXPERT, D_MODEL), BF16),
        ],
    )
    return pl.pallas_call(
        _expert_kernel,
        grid_spec=grid_spec,
        out_shape=jax.ShapeDtypeStruct((xs.shape[0], D_MODEL), F32),
        compiler_params=pltpu.CompilerParams(
            dimension_semantics=("arbitrary",), vmem_limit_bytes=VMEM_LIMIT),
        name="moe_experts",
    )(block_e, nb, xs, w_eg, w_eu, w_ed)


def _combine_kernel(dest_ref, h_ref, route_ref, nfin_ref, ys_ref, y_ref, buf1, buf2, sem, *, tb):
    def row_copy(d, buf, rr):
        return pltpu.make_async_copy(ys_ref.at[pl.ds(d, 1)], buf.at[pl.ds(rr, 1)], sem)

    def start(rr, carry):
        row_copy(dest_ref[0, 0, 2 * rr], buf1, rr).start()
        row_copy(dest_ref[0, 0, 2 * rr + 1], buf2, rr).start()
        return carry

    lax.fori_loop(0, tb, start, 0)

    def wait(rr, carry):
        row_copy(0, buf1, 0).wait()
        row_copy(0, buf2, 0).wait()
        return carry

    lax.fori_loop(0, tb, wait, 0)

    route = route_ref[...]
    out = h_ref[...] + (buf1[...] * route[:, 2:3] + buf2[...] * route[:, 3:4])
    y_ref[...] = _rms(out, nfin_ref[...])


def _combine(dest, h, route, norm_final, ys):
    n_rows = h.shape[0]
    tb = COMBINE_TB
    dest3 = dest.reshape(n_rows // tb, 1, 2 * tb)
    return pl.pallas_call(
        functools.partial(_combine_kernel, tb=tb),
        grid=(n_rows // tb,),
        in_specs=[
            pl.BlockSpec((1, 1, 2 * tb), lambda i: (i, 0, 0), memory_space=pltpu.SMEM),
            pl.BlockSpec((tb, D_MODEL), lambda i: (i, 0)),
            pl.BlockSpec((tb, LANES), lambda i: (i, 0)),
            pl.BlockSpec((1, D_MODEL), lambda i: (0, 0)),
            pl.BlockSpec(memory_space=pl.ANY),
        ],
        out_specs=pl.BlockSpec((tb, D_MODEL), lambda i: (i, 0)),
        out_shape=jax.ShapeDtypeStruct((n_rows, D_MODEL), F32),
        scratch_shapes=[
            pltpu.VMEM((tb, D_MODEL), F32),
            pltpu.VMEM((tb, D_MODEL), F32),
            pltpu.SemaphoreType.DMA,
        ],
        compiler_params=pltpu.CompilerParams(
            dimension_semantics=("arbitrary",), vmem_limit_bytes=VMEM_LIMIT),
        name="moe_combine",
    )(dest3, h, route, norm_final, ys)


def _state_to_blockdiag(s):
    eye = jnp.eye(GLA_HEADS, dtype=s.dtype)
    bd = jnp.einsum('bhkv,hg->bhvgk', s, eye)
    return bd.reshape(s.shape[0], D_V, D_QK)


def _state_from_blockdiag(st):
    st5 = st.reshape(st.shape[0], GLA_HEADS, GLA_DV, GLA_HEADS, GLA_DK)
    blocks = [st5[:, hh, :, hh, :] for hh in range(GLA_HEADS)]
    return jnp.stack(blocks, axis=1).transpose(0, 1, 3, 2)


def kernel(x_prompt, x_sample, state_pool, state_gla, meta_tokens, norm_mix, w_in, w_gate_up, b_gate, w_pool, pool_scale, gla_norm, w_out, norm_ffn, w_router_group, b_router_group, w_router_expert, b_router_expert, w_expert_gate, w_expert_up, w_expert_down, norm_final):
    assert w_in.shape[0] == 1, "one encoder layer"
    batch, seq, _ = x_prompt.shape
    dec_batch, dec_seq, _ = x_sample.shape
    assert seq % MIX_TB == 0 and dec_seq == CHUNK and N_META <= CHUNK
    t_prompt = batch * seq
    t_sample = dec_batch * dec_seq
    t_all = t_prompt + t_sample
    assert t_prompt % DISPATCH_TB == 0 and t_sample % DISPATCH_TB == 0
    assert t_prompt % COMBINE_TB == 0 and t_sample % COMBINE_TB == 0

    w_in0 = w_in[0]
    zpad = LANES - GATE_RANK
    nr = N_GROUPS + N_EXPERTS
    weights = (
        norm_mix[0][None, :],
        w_in0[:, :D_MAIN].astype(BF16),
        jnp.pad(w_in0[:, D_MAIN:], ((0, 0), (0, zpad))).astype(BF16),
        jnp.pad(w_gate_up[0], ((0, zpad), (0, 0))).astype(BF16),
        b_gate[0][None, :],
        w_pool[0].astype(BF16),
        pool_scale[0][None, :],
        gla_norm[0][None, :],
        w_out[0].astype(BF16),
        norm_ffn[0][None, :],
        jnp.pad(jnp.concatenate([w_router_group[0], w_router_expert[0]], axis=1),
                ((0, 0), (0, LANES - nr))).astype(BF16),
        jnp.pad(jnp.concatenate([b_router_group[0], b_router_expert[0]]), (0, LANES - nr))[None, :],
    )

    zero_cnt = jnp.zeros((1, LANES), F32)
    x_meta = jnp.pad(meta_tokens.astype(F32), ((CHUNK - N_META, 0), (0, 0)))
    meta = _mixer(x_meta, jnp.zeros((1, POOL_ROWS, D_POOL), F32), jnp.zeros((1, D_V, D_QK), F32), zero_cnt,
                  weights, batch=1, seq=CHUNK, tb=CHUNK, lead_pad=CHUNK - N_META)
    prm = _mixer(x_prompt.reshape(t_prompt, D_MODEL), meta[3], meta[4], zero_cnt, weights,
                 batch=batch, seq=seq, tb=MIX_TB, lead_pad=0)
    pool_s0 = jnp.pad(state_pool[0], ((0, 0), (POOL_ROWS - POOL_PAD, 0), (0, 0)))
    smp = _mixer(x_sample.reshape(t_sample, D_MODEL), pool_s0, _state_to_blockdiag(state_gla[0]),
                 prm[5], weights, batch=dec_batch, seq=dec_seq, tb=CHUNK, lead_pad=0)

    counts = smp[5][0, :N_EXPERTS].astype(I32)
    padded = (counts + MOE_BLK - 1) // MOE_BLK * MOE_BLK
    ends = jnp.cumsum(padded)
    pstart = ends - padded
    n_blocks = (2 * t_all + N_EXPERTS * (MOE_BLK - 1) + MOE_BLK - 1) // MOE_BLK
    nb = (ends[-1] // MOE_BLK).astype(I32)
    blk_ids = jnp.minimum(jnp.arange(n_blocks, dtype=I32), nb - 1)
    block_e = jnp.clip(jnp.searchsorted(ends, blk_ids * MOE_BLK, side='right'), 0, N_EXPERTS - 1).astype(I32)

    def dest_rows(route):
        return pstart[route[:, 0:2].astype(I32)] + route[:, 4:6].astype(I32)

    dest_p, dest_s = dest_rows(prm[2]), dest_rows(smp[2])
    xs = jnp.zeros((n_blocks * MOE_BLK, HALF), U32)
    xs = _dispatch(dest_p, prm[1], xs)
    xs = _dispatch(dest_s, smp[1], xs)
    ys = _experts(block_e, nb[None], xs, w_expert_gate[0], w_expert_up[0], w_expert_down[0])
    nfin = norm_final[None, :]
    y_prompt = _combine(dest_p, prm[0], prm[2], nfin, ys).reshape(batch, seq, D_MODEL)
    y_sample = _combine(dest_s, smp[0], smp[2], nfin, ys).reshape(dec_batch, dec_seq, D_MODEL)
    new_pool_prompt = prm[3][:, POOL_ROWS - POOL_PAD:][None]
    new_gla_prompt = _state_from_blockdiag(prm[4])[None]
    new_pool_sample = smp[3][:, POOL_ROWS - POOL_PAD:][None]
    new_gla_sample = _state_from_blockdiag(smp[4])[None]
    return (y_prompt, y_sample, new_pool_prompt, new_gla_prompt, new_pool_sample, new_gla_sample)
```

```python
import functools

import jax
import jax.numpy as jnp
from jax import lax
from jax.experimental import pallas as pl
from jax.experimental.pallas import tpu as pltpu

F32 = jnp.float32
BF16 = jnp.bfloat16
U32 = jnp.uint32
I32 = jnp.int32

D_MODEL = 1024
N_META = 16
CHUNK = 64
EPS = 1e-6
D_POOL = 512
POOL_WINDOWS = (2, 4, 8, 16)
POOL_GROUP_DIM = 128
POOL_PAD = 15
POOL_ROWS = 16
GLA_HEADS = 4
GLA_DK = 64
GLA_DV = 128
D_QK = 256
D_V = 512
GATE_RANK = 16
GATE_TAU = 16.0
D_MAIN = D_POOL + 2 * D_QK + 2 * D_V
N_GROUPS = 4
EXPERTS_PER_GROUP = 8
N_EXPERTS = 32
D_EXPERT = 512

LANES = 128
HALF = D_MODEL // 2
MIX_TB = 256
MOE_BLK = 256
DISPATCH_TB = 1024
COMBINE_TB = 256
VMEM_LIMIT = 48 * 1024 * 1024


def _rms(x, g):
    return x * lax.rsqrt(jnp.mean(x * x, axis=-1, keepdims=True) + EPS) * g


def _dot(a, b):
    return jnp.dot(a, b, preferred_element_type=F32)


def _dot_nt(a, b):
    return lax.dot_general(a, b, (((1,), (1,)), ((), ())), preferred_element_type=F32)


def _dot_tn(a, b):
    return lax.dot_general(a, b, (((0,), (0,)), ((), ())), preferred_element_type=F32)


def _mixer_kernel(x_ref, pool0_ref, st0_ref, cnt0_ref, tril_ref, slow_ref,
                  nmix_ref, wmain_ref, wz_ref, wgu_ref, bgate_ref, wpool_ref, pscale_ref,
                  gnorm_ref, wout_ref, nffn_ref, wr_ref, br_ref,
                  h_ref, hn_ref, route_ref, pool_out_ref, st_out_ref, cnt_out_ref,
                  ext_ref, st_ref, o_ref, cnt_ref, *, tb, lead_pad):
    b = pl.program_id(0)
    j = pl.program_id(1)
    nj = pl.num_programs(1)

    @pl.when(j == 0)
    def _():
        ext_ref[0:POOL_ROWS, :] = pool0_ref[0]
        st_ref[...] = st0_ref[0]

    @pl.when((b == 0) & (j == 0))
    def _():
        cnt_ref[...] = cnt0_ref[...]

    x = x_ref[...]
    xn = _rms(x, nmix_ref[...]).astype(BF16)
    proj = _dot(xn, wmain_ref[...])
    z = _dot(xn, wz_ref[...])
    u = proj[:, 0:D_POOL]
    q = proj[:, D_POOL:D_POOL + D_QK]
    k = proj[:, D_POOL + D_QK:D_POOL + 2 * D_QK]
    v = proj[:, D_POOL + 2 * D_QK:D_POOL + 2 * D_QK + D_V]
    r = proj[:, D_POOL + 2 * D_QK + D_V:D_MAIN]

    row = lax.broadcasted_iota(I32, (tb, 1), 0)

    ext_ref[POOL_ROWS:POOL_ROWS + tb, :] = u
    ext = ext_ref[...]
    pys = []
    for g, w in enumerate(POOL_WINDOWS):
        sl = slice(g * POOL_GROUP_DIM, (g + 1) * POOL_GROUP_DIM)
        acc = ext[:, sl]
        for s in range(g + 1):
            acc = acc + pltpu.roll(acc, 1 << s, axis=0)
        win = acc[POOL_ROWS:, :]
        if lead_pad:
            cnt = jnp.clip(row - lead_pad + 1, 1, w).astype(F32)
            pooled = win / cnt - u[:, sl]
        else:
            pooled = win * (1.0 / w) - u[:, sl]
        pys.append(_dot(pooled.astype(BF16), wpool_ref[g]))
    pool_y = jnp.concatenate(pys, axis=-1) * pscale_ref[...]
    ext_ref[0:POOL_ROWS, :] = ext_ref[tb:tb + POOL_ROWS, :]

    gpre = _dot(z.astype(BF16), wgu_ref[...]) + bgate_ref[...]
    log_a = jax.nn.log_sigmoid(gpre) * (1.0 / GATE_TAU)
    if lead_pad:
        log_a = jnp.where(row >= lead_pad, log_a, 0.0)
    a_hi = log_a.astype(BF16)
    a_lo = (log_a - a_hi.astype(F32)).astype(BF16)
    tril = tril_ref[...]
    bcum = _dot(tril, a_hi) + _dot(tril, a_lo)
    eb = jnp.exp(bcum)
    qi = q * (GLA_DK ** -0.5) * eb
    ki = k * jnp.exp(-bcum)

    rr = lax.broadcasted_iota(I32, (GLA_HEADS * CHUNK, D_QK), 0)
    cc = lax.broadcasted_iota(I32, (GLA_HEADS * CHUNK, D_QK), 1)
    kbd_mask = (rr // CHUNK) == (cc // GLA_DK)
    rr = lax.broadcasted_iota(I32, (GLA_HEADS * CHUNK, D_V), 0)
    cc = lax.broadcasted_iota(I32, (GLA_HEADS * CHUNK, D_V), 1)
    vbd_mask = (rr // CHUNK) == (cc // GLA_DV)
    rr = lax.broadcasted_iota(I32, (D_V, D_QK), 0)
    cc = lax.broadcasted_iota(I32, (D_V, D_QK), 1)
    st_mask = (rr // GLA_DV) == (cc // GLA_DK)
    rr = lax.broadcasted_iota(I32, (CHUNK, GLA_HEADS * CHUNK), 0)
    cc = lax.broadcasted_iota(I32, (CHUNK, GLA_HEADS * CHUNK), 1)
    causal = (cc % CHUNK) <= rr

    for c in range(tb // CHUNK):
        rows = slice(c * CHUNK, (c + 1) * CHUNK)
        eb_last = eb[(c + 1) * CHUNK - 1:(c + 1) * CHUNK, :]
        qi_c = qi[rows].astype(BF16)
        ki_c = ki[rows]
        kd_c = (ki_c * eb_last).astype(BF16)
        v_c = v[rows]
        kbd = jnp.where(kbd_mask, jnp.concatenate([ki_c] * GLA_HEADS, axis=0), 0.0).astype(BF16)
        scores = _dot_nt(qi_c, kbd)
        p = jnp.where(causal, scores, 0.0).astype(BF16)
        vbd = jnp.where(vbd_mask, jnp.concatenate([v_c] * GLA_HEADS, axis=0), 0.0).astype(BF16)
        st = st_ref[...]
        o_ref[rows, :] = _dot(p, vbd) + _dot_nt(qi_c, st.astype(BF16))
        kvt = _dot_tn(v_c.astype(BF16), kd_c)
        st_ref[...] = st * eb_last + jnp.where(st_mask, kvt, 0.0)

    o = o_ref[...]
    ons = []
    for hh in range(GLA_HEADS):
        oh = o[:, hh * GLA_DV:(hh + 1) * GLA_DV]
        ons.append(oh * lax.rsqrt(jnp.mean(oh * oh, axis=-1, keepdims=True) + EPS))
    og = jnp.concatenate(ons, axis=-1) * gnorm_ref[...] * (r * jax.nn.sigmoid(r))
    mix = _dot(jnp.concatenate([pool_y, og], axis=-1).astype(BF16), wout_ref[...])

    h = x + mix
    h_ref[...] = h
    hn = _rms(h, nffn_ref[...]).astype(BF16)
    lo = lax.bitcast_convert_type(hn[:, :HALF].astype(F32), U32) >> 16
    hi = lax.bitcast_convert_type(hn[:, HALF:].astype(F32), U32) & jnp.uint32(0xFFFF0000)
    hn_ref[...] = lo | hi

    logits = _dot(hn, wr_ref[...]) + br_ref[...]
    lane = lax.broadcasted_iota(I32, (tb, LANES), 1)
    lanef = lane.astype(F32)
    neg = jnp.float32(-jnp.inf)
    big = jnp.float32(LANES)
    is_g = lane < N_GROUPS
    gl = jnp.where(is_g, logits, neg)
    gmax = jnp.max(gl, axis=-1, keepdims=True)
    gsum = jnp.sum(jnp.where(is_g, jnp.exp(logits - gmax), 0.0), axis=-1, keepdims=True)
    p_g = 1.0 / gsum
    gidx = jnp.min(jnp.where(is_g & (logits == gmax), lanef, big), axis=-1, keepdims=True)
    e_lo = N_GROUPS + EXPERTS_PER_GROUP * gidx
    is_e = (lanef >= e_lo) & (lanef < e_lo + EXPERTS_PER_GROUP)
    el = jnp.where(is_e, logits, neg)
    m1 = jnp.max(el, axis=-1, keepdims=True)
    i1 = jnp.min(jnp.where(is_e & (logits == m1), lanef, big), axis=-1, keepdims=True)
    is_e2 = is_e & (lanef != i1)
    el2 = jnp.where(is_e2, logits, neg)
    m2 = jnp.max(el2, axis=-1, keepdims=True)
    i2 = jnp.min(jnp.where(is_e2 & (logits == m2), lanef, big), axis=-1, keepdims=True)
    t2 = jnp.exp(m2 - m1)
    den = 1.0 + t2
    g1 = p_g / den
    g2 = p_g * t2 / den
    e1 = i1 - N_GROUPS
    e2 = i2 - N_GROUPS

    oh1 = lanef == e1
    oh2 = lanef == e2
    both = jnp.where(oh1 | oh2, 1.0, 0.0)
    before = _dot(slow_ref[...], both.astype(BF16)) + cnt_ref[...]
    pos1 = jnp.sum(jnp.where(oh1, before, 0.0), axis=-1, keepdims=True)
    pos2 = jnp.sum(jnp.where(oh2, before, 0.0), axis=-1, keepdims=True)
    cnt_new = cnt_ref[...] + jnp.sum(both, axis=0, keepdims=True)
    cnt_ref[...] = cnt_new
    cnt_out_ref[...] = cnt_new

    route = jnp.where(lane == 0, e1, 0.0)
    for idx, val in ((1, e2), (2, g1), (3, g2), (4, pos1), (5, pos2)):
        route = jnp.where(lane == idx, val, route)
    route_ref[...] = route

    @pl.when(j == nj - 1)
    def _():
        pool_out_ref[0] = ext_ref[0:POOL_ROWS, :]
        st_out_ref[0] = st_ref[...]


def _mixer(x2d, pool0, st0, cnt0, weights, *, batch, seq, tb, lead_pad):
    nj = seq // tb
    total_rows = batch * seq
    shared = pool0.shape[0] == 1
    st_idx = (lambda b, j: (0, 0, 0)) if shared else (lambda b, j: (b, 0, 0))
    const2 = lambda b, j: (0, 0)
    tok = lambda b, j: (b * nj + j, 0)
    tok_out = tok

    ii = jnp.arange(tb)
    tril = ((ii[:, None] >= ii[None, :]) & (ii[:, None] // CHUNK == ii[None, :] // CHUNK)).astype(BF16)
    slow = (ii[:, None] > ii[None, :]).astype(BF16)

    in_specs = [
        pl.BlockSpec((tb, D_MODEL), tok),
        pl.BlockSpec((1, POOL_ROWS, D_POOL), st_idx),
        pl.BlockSpec((1, D_V, D_QK), st_idx),
        pl.BlockSpec((1, LANES), const2),
        pl.BlockSpec((tb, tb), const2),
        pl.BlockSpec((tb, tb), const2),
    ]
    for wgt in weights:
        in_specs.append(pl.BlockSpec(wgt.shape, (lambda b, j, n=wgt.ndim: (0,) * n)))
    args = [x2d, pool0, st0, cnt0, tril, slow, *weights]

    out_shape = [
        jax.ShapeDtypeStruct((total_rows, D_MODEL), F32),
        jax.ShapeDtypeStruct((total_rows, HALF), U32),
        jax.ShapeDtypeStruct((total_rows, LANES), F32),
        jax.ShapeDtypeStruct((batch, POOL_ROWS, D_POOL), F32),
        jax.ShapeDtypeStruct((batch, D_V, D_QK), F32),
        jax.ShapeDtypeStruct((1, LANES), F32),
    ]
    out_specs = [
        pl.BlockSpec((tb, D_MODEL), tok_out),
        pl.BlockSpec((tb, HALF), tok_out),
        pl.BlockSpec((tb, LANES), tok_out),
        pl.BlockSpec((1, POOL_ROWS, D_POOL), lambda b, j: (b, 0, 0)),
        pl.BlockSpec((1, D_V, D_QK), lambda b, j: (b, 0, 0)),
        pl.BlockSpec((1, LANES), const2),
    ]
    scratch = [
        pltpu.VMEM((POOL_ROWS + tb, D_POOL), F32),
        pltpu.VMEM((D_V, D_QK), F32),
        pltpu.VMEM((tb, D_V), F32),
        pltpu.VMEM((1, LANES), F32),
    ]
    return pl.pallas_call(
        functools.partial(_mixer_kernel, tb=tb, lead_pad=lead_pad),
        grid=(batch, nj),
        in_specs=in_specs,
        out_specs=out_specs,
        out_shape=out_shape,
        scratch_shapes=scratch,
        compiler_params=pltpu.CompilerParams(
            dimension_semantics=("arbitrary", "arbitrary"), vmem_limit_bytes=VMEM_LIMIT),
        name=f"mixer_tb{tb}_pad{lead_pad}",
    )(*args)


def _dispatch_kernel(dest_ref, hn_ref, xs_in_ref, xs_ref, sem, *, tb):
    del xs_in_ref

    def row_copy(t, d):
        return pltpu.make_async_copy(hn_ref.at[pl.ds(t, 1)], xs_ref.at[pl.ds(d, 1)], sem)

    def start(rr, carry):
        row_copy(rr, dest_ref[0, 0, 2 * rr]).start()
        row_copy(rr, dest_ref[0, 0, 2 * rr + 1]).start()
        return carry

    lax.fori_loop(0, tb, start, 0)

    def wait(rr, carry):
        row_copy(0, 0).wait()
        row_copy(0, 0).wait()
        return carry

    lax.fori_loop(0, tb, wait, 0)


def _dispatch(dest, hn, xs0):
    n_tok = hn.shape[0]
    n_rows = xs0.shape[0]
    tb = DISPATCH_TB
    dest3 = dest.reshape(n_tok // tb, 1, 2 * tb)
    return pl.pallas_call(
        functools.partial(_dispatch_kernel, tb=tb),
        grid=(n_tok // tb,),
        in_specs=[
            pl.BlockSpec((1, 1, 2 * tb), lambda i: (i, 0, 0), memory_space=pltpu.SMEM),
            pl.BlockSpec((tb, HALF), lambda i: (i, 0)),
            pl.BlockSpec(memory_space=pl.ANY),
        ],
        out_specs=pl.BlockSpec(memory_space=pl.ANY),
        out_shape=jax.ShapeDtypeStruct((n_rows, HALF), U32),
        scratch_shapes=[pltpu.SemaphoreType.DMA],
        input_output_aliases={2: 0},
        compiler_params=pltpu.CompilerParams(dimension_semantics=("arbitrary",)),
        name="moe_dispatch",
    )(dest3, hn, xs0)


def _expert_kernel(be_ref, nb_ref, xs_ref, wg_ref, wu_ref, wd_ref, ys_ref, wgu_s, wd_s):
    i = pl.program_id(0)
    live = i < nb_ref[0]
    prev = be_ref[jnp.maximum(i - 1, 0)]
    fresh = (i == 0) | (be_ref[i] != prev)

    @pl.when(live & fresh)
    def _():
        wgu_s[:, 0:D_EXPERT] = wg_ref[0].astype(BF16)
        wgu_s[:, D_EXPERT:2 * D_EXPERT] = wu_ref[0].astype(BF16)
        wd_s[...] = wd_ref[0].astype(BF16)

    @pl.when(live)
    def _():
        x32 = xs_ref[...]
        lo = lax.bitcast_convert_type(x32 << 16, F32).astype(BF16)
        hi = lax.bitcast_convert_type(x32 & jnp.uint32(0xFFFF0000), F32).astype(BF16)
        xb = jnp.concatenate([lo, hi], axis=-1)
        gu = _dot(xb, wgu_s[...])
        gate = gu[:, 0:D_EXPERT]
        hmid = gate * jax.nn.sigmoid(gate) * gu[:, D_EXPERT:]
        ys_ref[...] = _dot(hmid.astype(BF16), wd_s[...])

    @pl.when(jnp.logical_not(live))
    def _():
        ys_ref[...] = jnp.zeros_like(ys_ref)


def _experts(block_e, nb, xs, w_eg, w_eu, w_ed):
    n_blocks = xs.shape[0] // MOE_BLK
    row_idx = lambda i, be, nb: (jnp.minimum(i, nb[0] - 1), 0)
    w_idx = lambda i, be, nb: (be[i], 0, 0)
    grid_spec = pltpu.PrefetchScalarGridSpec(
        num_scalar_prefetch=2,
        grid=(n_blocks,),
        in_specs=[
            pl.BlockSpec((MOE_BLK, HALF), row_idx),
            pl.BlockSpec((1, D_MODEL, D_EXPERT), w_idx),
            pl.BlockSpec((1, D_MODEL, D_EXPERT), w_idx),
            pl.BlockSpec((1, D_EXPERT, D_MODEL), w_idx),
        ],
        out_specs=pl.BlockSpec((MOE_BLK, D_MODEL), lambda i, be, nb: (i, 0)),
        scratch_shapes=[
            pltpu.VMEM((D_MODEL, 2 * D_EXPERT), BF16),
            pltpu.VMEM((D_EXPERT, D_MODEL), BF16),
        ],
    )
    return pl.pallas_call(
        _expert_kernel,
        grid_spec=grid_spec,
        out_shape=jax.ShapeDtypeStruct((xs.shape[0], D_MODEL), F32),
        compiler_params=pltpu.CompilerParams(
            dimension_semantics=("arbitrary",), vmem_limit_bytes=VMEM_LIMIT),
        name="moe_experts",
    )(block_e, nb, xs, w_eg, w_eu, w_ed)


def _combine_kernel(dest_ref, h_ref, route_ref, nfin_ref, ys_ref, y_ref, buf1, buf2, sem, *, tb):
    def row_copy(d, buf, rr):
        return pltpu.make_async_copy(ys_ref.at[pl.ds(d, 1)], buf.at[pl.ds(rr, 1)], sem)

    def start(rr, carry):
        row_copy(dest_ref[0, 0, 2 * rr], buf1, rr).start()
        row_copy(dest_ref[0, 0, 2 * rr + 1], buf2, rr).start()
        return carry

    lax.fori_loop(0, tb, start, 0)

    def wait(rr, carry):
        row_copy(0, buf1, 0).wait()
        row_copy(0, buf2, 0).wait()
        return carry

    lax.fori_loop(0, tb, wait, 0)

    route = route_ref[...]
    out = h_ref[...] + (buf1[...] * route[:, 2:3] + buf2[...] * route[:, 3:4])
    y_ref[...] = _rms(out, nfin_ref[...])


def _combine(dest, h, route, norm_final, ys):
    n_rows = h.shape[0]
    tb = COMBINE_TB
    dest3 = dest.reshape(n_rows // tb, 1, 2 * tb)
    return pl.pallas_call(
        functools.partial(_combine_kernel, tb=tb),
        grid=(n_rows // tb,),
        in_specs=[
            pl.BlockSpec((1, 1, 2 * tb), lambda i: (i, 0, 0), memory_space=pltpu.SMEM),
            pl.BlockSpec((tb, D_MODEL), lambda i: (i, 0)),
            pl.BlockSpec((tb, LANES), lambda i: (i, 0)),
            pl.BlockSpec((1, D_MODEL), lambda i: (0, 0)),
            pl.BlockSpec(memory_space=pl.ANY),
        ],
        out_specs=pl.BlockSpec((tb, D_MODEL), lambda i: (i, 0)),
        out_shape=jax.ShapeDtypeStruct((n_rows, D_MODEL), F32),
        scratch_shapes=[
            pltpu.VMEM((tb, D_MODEL), F32),
            pltpu.VMEM((tb, D_MODEL), F32),
            pltpu.SemaphoreType.DMA,
        ],
        compiler_params=pltpu.CompilerParams(
            dimension_semantics=("arbitrary",), vmem_limit_bytes=VMEM_LIMIT),
        name="moe_combine",
    )(dest3, h, route, norm_final, ys)


def _state_to_blockdiag(s):
    eye = jnp.eye(GLA_HEADS, dtype=s.dtype)
    bd = jnp.einsum('bhkv,hg->bhvgk', s, eye)
    return bd.reshape(s.shape[0], D_V, D_QK)


def _state_from_blockdiag(st):
    st5 = st.reshape(st.shape[0], GLA_HEADS, GLA_DV, GLA_HEADS, GLA_DK)
    blocks = [st5[:, hh, :, hh, :] for hh in range(GLA_HEADS)]
    return jnp.stack(blocks, axis=1).transpose(0, 1, 3, 2)


def kernel(x_prompt, x_sample, state_pool, state_gla, meta_tokens, norm_mix, w_in, w_gate_up, b_gate, w_pool, pool_scale, gla_norm, w_out, norm_ffn, w_router_group, b_router_group, w_router_expert, b_router_expert, w_expert_gate, w_expert_up, w_expert_down, norm_final):
    assert w_in.shape[0] == 1, "one encoder layer"
    batch, seq, _ = x_prompt.shape
    dec_batch, dec_seq, _ = x_sample.shape
    assert seq % MIX_TB == 0 and dec_seq == CHUNK and N_META <= CHUNK
    t_prompt = batch * seq
    t_sample = dec_batch * dec_seq
    t_all = t_prompt + t_sample
    assert t_prompt % DISPATCH_TB == 0 and t_sample % DISPATCH_TB == 0
    assert t_prompt % COMBINE_TB == 0 and t_sample % COMBINE_TB == 0

    w_in0 = w_in[0]
    zpad = LANES - GATE_RANK
    nr = N_GROUPS + N_EXPERTS
    weights = (
        norm_mix[0][None, :],
        w_in0[:, :D_MAIN].astype(BF16),
        jnp.pad(w_in0[:, D_MAIN:], ((0, 0), (0, zpad))).astype(BF16),
        jnp.pad(w_gate_up[0], ((0, zpad), (0, 0))).astype(BF16),
        b_gate[0][None, :],
        w_pool[0].astype(BF16),
        pool_scale[0][None, :],
        gla_norm[0][None, :],
        w_out[0].astype(BF16),
        norm_ffn[0][None, :],
        jnp.pad(jnp.concatenate([w_router_group[0], w_router_expert[0]], axis=1),
                ((0, 0), (0, LANES - nr))).astype(BF16),
        jnp.pad(jnp.concatenate([b_router_group[0], b_router_expert[0]]), (0, LANES - nr))[None, :],
    )

    zero_cnt = jnp.zeros((1, LANES), F32)
    x_meta = jnp.pad(meta_tokens.astype(F32), ((CHUNK - N_META, 0), (0, 0)))
    meta = _mixer(x_meta, jnp.zeros((1, POOL_ROWS, D_POOL), F32), jnp.zeros((1, D_V, D_QK), F32), zero_cnt,
                  weights, batch=1, seq=CHUNK, tb=CHUNK, lead_pad=CHUNK - N_META)
    prm = _mixer(x_prompt.reshape(t_prompt, D_MODEL), meta[3], meta[4], zero_cnt, weights,
                 batch=batch, seq=seq, tb=MIX_TB, lead_pad=0)
    pool_s0 = jnp.pad(state_pool[0], ((0, 0), (POOL_ROWS - POOL_PAD, 0), (0, 0)))
    smp = _mixer(x_sample.reshape(t_sample, D_MODEL), pool_s0, _state_to_blockdiag(state_gla[0]),
                 prm[5], weights, batch=dec_batch, seq=dec_seq, tb=CHUNK, lead_pad=0)

    counts = smp[5][0, :N_EXPERTS].astype(I32)
    padded = (counts + MOE_BLK - 1) // MOE_BLK * MOE_BLK
    ends = jnp.cumsum(padded)
    pstart = ends - padded
    n_blocks = (2 * t_all + N_EXPERTS * (MOE_BLK - 1) + MOE_BLK - 1) // MOE_BLK
    nb = (ends[-1] // MOE_BLK).astype(I32)
    blk_ids = jnp.minimum(jnp.arange(n_blocks, dtype=I32), nb - 1)
    block_e = jnp.sum((ends[None, :] <= (blk_ids * MOE_BLK)[:, None]).astype(I32), axis=1)
    block_e = jnp.minimum(block_e, N_EXPERTS - 1)

    def dest_rows(route):
        onehot = route[:, 0:2].astype(I32)[..., None] == jnp.arange(N_EXPERTS, dtype=I32)
        return jnp.sum(jnp.where(onehot, pstart, 0), axis=-1) + route[:, 4:6].astype(I32)

    dest_p, dest_s = dest_rows(prm[2]), dest_rows(smp[2])
    xs = jnp.zeros((n_blocks * MOE_BLK, HALF), U32)
    xs = _dispatch(dest_p, prm[1], xs)
    xs = _dispatch(dest_s, smp[1], xs)
    ys = _experts(block_e, nb[None], xs, w_expert_gate[0], w_expert_up[0], w_expert_down[0])
    nfin = norm_final[None, :]
    y_prompt = _combine(dest_p, prm[0], prm[2], nfin, ys).reshape(batch, seq, D_MODEL)
    y_sample = _combine(dest_s, smp[0], smp[2], nfin, ys).reshape(dec_batch, dec_seq, D_MODEL)
    new_pool_prompt = prm[3][:, POOL_ROWS - POOL_PAD:][None]
    new_gla_prompt = _state_from_blockdiag(prm[4])[None]
    new_pool_sample = smp[3][:, POOL_ROWS - POOL_PAD:][None]
    new_gla_sample = _state_from_blockdiag(smp[4])[None]
    return (y_prompt, y_sample, new_pool_prompt, new_gla_prompt, new_pool_sample, new_gla_sample)
```

```python
import functools

import jax
import jax.numpy as jnp
from jax import lax
from jax.experimental import pallas as pl
from jax.experimental.pallas import tpu as pltpu
from jax.experimental.pallas import tpu_sc as plsc

F32 = jnp.float32
BF16 = jnp.bfloat16
U32 = jnp.uint32
I32 = jnp.int32

D_MODEL = 1024
N_META = 16
CHUNK = 64
EPS = 1e-6
D_POOL = 512
POOL_WINDOWS = (2, 4, 8, 16)
POOL_GROUP_DIM = 128
POOL_PAD = 15
POOL_ROWS = 16
GLA_HEADS = 4
GLA_DK = 64
GLA_DV = 128
D_QK = 256
D_V = 512
GATE_RANK = 16
GATE_TAU = 16.0
D_MAIN = D_POOL + 2 * D_QK + 2 * D_V
N_GROUPS = 4
EXPERTS_PER_GROUP = 8
N_EXPERTS = 32
D_EXPERT = 512

LANES = 128
HALF = D_MODEL // 2
HN_PLANES = HALF // LANES
Y_PLANES = D_MODEL // LANES
ROUTE_ROWS = 8
MIX_TB = 256
MOE_BLK = 256
COMBINE_TB = 256
VMEM_LIMIT = 48 * 1024 * 1024
SC_CORES = 2
SC_SUBCORES = 16
SC_WORKERS = SC_CORES * SC_SUBCORES
SC_ROWS = 128


def _rms(x, g):
    return x * lax.rsqrt(jnp.mean(x * x, axis=-1, keepdims=True) + EPS) * g


def _dot(a, b):
    return jnp.dot(a, b, preferred_element_type=F32)


def _dot_nt(a, b):
    return lax.dot_general(a, b, (((1,), (1,)), ((), ())), preferred_element_type=F32)


def _dot_tn(a, b):
    return lax.dot_general(a, b, (((0,), (0,)), ((), ())), preferred_element_type=F32)


def _mixer_kernel(x_ref, pool0_ref, st0_ref, cnt0_ref, tril_ref, slow_ref,
                  nmix_ref, wmain_ref, wz_ref, wgu_ref, bgate_ref, wpool_ref, pscale_ref,
                  gnorm_ref, wout_ref, nffn_ref, wr_ref, br_ref,
                  h_ref, hn_ref, route_ref, route_t_ref, pool_out_ref, st_out_ref, cnt_out_ref,
                  ext_ref, st_ref, o_ref, cnt_ref, *, tb, lead_pad):
    b = pl.program_id(0)
    j = pl.program_id(1)
    nj = pl.num_programs(1)

    @pl.when(j == 0)
    def _():
        ext_ref[0:POOL_ROWS, :] = pool0_ref[0]
        st_ref[...] = st0_ref[0]

    @pl.when((b == 0) & (j == 0))
    def _():
        cnt_ref[...] = cnt0_ref[...]

    x = x_ref[...]
    xn = _rms(x, nmix_ref[...]).astype(BF16)
    proj = _dot(xn, wmain_ref[...])
    z = _dot(xn, wz_ref[...])
    u = proj[:, 0:D_POOL]
    q = proj[:, D_POOL:D_POOL + D_QK]
    k = proj[:, D_POOL + D_QK:D_POOL + 2 * D_QK]
    v = proj[:, D_POOL + 2 * D_QK:D_POOL + 2 * D_QK + D_V]
    r = proj[:, D_POOL + 2 * D_QK + D_V:D_MAIN]

    row = lax.broadcasted_iota(I32, (tb, 1), 0)

    ext_ref[POOL_ROWS:POOL_ROWS + tb, :] = u
    ext = ext_ref[...]
    pys = []
    for g, w in enumerate(POOL_WINDOWS):
        sl = slice(g * POOL_GROUP_DIM, (g + 1) * POOL_GROUP_DIM)
        acc = ext[:, sl]
        for s in range(g + 1):
            acc = acc + pltpu.roll(acc, 1 << s, axis=0)
        win = acc[POOL_ROWS:, :]
        if lead_pad:
            cnt = jnp.clip(row - lead_pad + 1, 1, w).astype(F32)
            pooled = win / cnt - u[:, sl]
        else:
            pooled = win * (1.0 / w) - u[:, sl]
        pys.append(_dot(pooled.astype(BF16), wpool_ref[g]))
    pool_y = jnp.concatenate(pys, axis=-1) * pscale_ref[...]
    ext_ref[0:POOL_ROWS, :] = ext_ref[tb:tb + POOL_ROWS, :]

    gpre = _dot(z.astype(BF16), wgu_ref[...]) + bgate_ref[...]
    log_a = jax.nn.log_sigmoid(gpre) * (1.0 / GATE_TAU)
    if lead_pad:
        log_a = jnp.where(row >= lead_pad, log_a, 0.0)
    a_hi = log_a.astype(BF16)
    a_lo = (log_a - a_hi.astype(F32)).astype(BF16)
    tril = tril_ref[...]
    bcum = _dot(tril, a_hi) + _dot(tril, a_lo)
    eb = jnp.exp(bcum)
    qi = q * (GLA_DK ** -0.5) * eb
    ki = k * jnp.exp(-bcum)

    rr = lax.broadcasted_iota(I32, (GLA_HEADS * CHUNK, D_QK), 0)
    cc = lax.broadcasted_iota(I32, (GLA_HEADS * CHUNK, D_QK), 1)
    kbd_mask = (rr // CHUNK) == (cc // GLA_DK)
    rr = lax.broadcasted_iota(I32, (GLA_HEADS * CHUNK, D_V), 0)
    cc = lax.broadcasted_iota(I32, (GLA_HEADS * CHUNK, D_V), 1)
    vbd_mask = (rr // CHUNK) == (cc // GLA_DV)
    rr = lax.broadcasted_iota(I32, (D_V, D_QK), 0)
    cc = lax.broadcasted_iota(I32, (D_V, D_QK), 1)
    st_mask = (rr // GLA_DV) == (cc // GLA_DK)
    rr = lax.broadcasted_iota(I32, (CHUNK, GLA_HEADS * CHUNK), 0)
    cc = lax.broadcasted_iota(I32, (CHUNK, GLA_HEADS * CHUNK), 1)
    causal = (cc % CHUNK) <= rr

    for c in range(tb // CHUNK):
        rows = slice(c * CHUNK, (c + 1) * CHUNK)
        eb_last = eb[(c + 1) * CHUNK - 1:(c + 1) * CHUNK, :]
        qi_c = qi[rows].astype(BF16)
        ki_c = ki[rows]
        kd_c = (ki_c * eb_last).astype(BF16)
        v_c = v[rows]
        kbd = jnp.where(kbd_mask, jnp.concatenate([ki_c] * GLA_HEADS, axis=0), 0.0).astype(BF16)
        scores = _dot_nt(qi_c, kbd)
        p = jnp.where(causal, scores, 0.0).astype(BF16)
        vbd = jnp.where(vbd_mask, jnp.concatenate([v_c] * GLA_HEADS, axis=0), 0.0).astype(BF16)
        st = st_ref[...]
        o_ref[rows, :] = _dot(p, vbd) + _dot_nt(qi_c, st.astype(BF16))
        kvt = _dot_tn(v_c.astype(BF16), kd_c)
        st_ref[...] = st * eb_last + jnp.where(st_mask, kvt, 0.0)

    o = o_ref[...]
    ons = []
    for hh in range(GLA_HEADS):
        oh = o[:, hh * GLA_DV:(hh + 1) * GLA_DV]
        ons.append(oh * lax.rsqrt(jnp.mean(oh * oh, axis=-1, keepdims=True) + EPS))
    og = jnp.concatenate(ons, axis=-1) * gnorm_ref[...] * (r * jax.nn.sigmoid(r))
    mix = _dot(jnp.concatenate([pool_y, og], axis=-1).astype(BF16), wout_ref[...])

    h = x + mix
    h_ref[...] = h
    hn = _rms(h, nffn_ref[...]).astype(BF16)
    lo = lax.bitcast_convert_type(hn[:, :HALF].astype(F32), U32) >> 16
    hi = lax.bitcast_convert_type(hn[:, HALF:].astype(F32), U32) & jnp.uint32(0xFFFF0000)
    packed = lax.bitcast_convert_type(lo | hi, I32)
    for p in range(HN_PLANES):
        hn_ref[p] = packed[:, p * LANES:(p + 1) * LANES]

    logits = _dot(hn, wr_ref[...]) + br_ref[...]
    lane = lax.broadcasted_iota(I32, (tb, LANES), 1)
    lanef = lane.astype(F32)
    neg = jnp.float32(-jnp.inf)
    big = jnp.float32(LANES)
    is_g = lane < N_GROUPS
    gl = jnp.where(is_g, logits, neg)
    gmax = jnp.max(gl, axis=-1, keepdims=True)
    gsum = jnp.sum(jnp.where(is_g, jnp.exp(logits - gmax), 0.0), axis=-1, keepdims=True)
    p_g = 1.0 / gsum
    gidx = jnp.min(jnp.where(is_g & (logits == gmax), lanef, big), axis=-1, keepdims=True)
    e_lo = N_GROUPS + EXPERTS_PER_GROUP * gidx
    is_e = (lanef >= e_lo) & (lanef < e_lo + EXPERTS_PER_GROUP)
    el = jnp.where(is_e, logits, neg)
    m1 = jnp.max(el, axis=-1, keepdims=True)
    i1 = jnp.min(jnp.where(is_e & (logits == m1), lanef, big), axis=-1, keepdims=True)
    is_e2 = is_e & (lanef != i1)
    el2 = jnp.where(is_e2, logits, neg)
    m2 = jnp.max(el2, axis=-1, keepdims=True)
    i2 = jnp.min(jnp.where(is_e2 & (logits == m2), lanef, big), axis=-1, keepdims=True)
    t2 = jnp.exp(m2 - m1)
    den = 1.0 + t2
    g1 = p_g / den
    g2 = p_g * t2 / den
    e1 = i1 - N_GROUPS
    e2 = i2 - N_GROUPS

    oh1 = lanef == e1
    oh2 = lanef == e2
    both = jnp.where(oh1 | oh2, 1.0, 0.0)
    before = _dot(slow_ref[...], both.astype(BF16)) + cnt_ref[...]
    pos1 = jnp.sum(jnp.where(oh1, before, 0.0), axis=-1, keepdims=True)
    pos2 = jnp.sum(jnp.where(oh2, before, 0.0), axis=-1, keepdims=True)
    cnt_new = cnt_ref[...] + jnp.sum(both, axis=0, keepdims=True)
    cnt_ref[...] = cnt_new
    cnt_out_ref[...] = cnt_new

    route = jnp.where(lane == 0, e1, 0.0)
    for idx, val in ((1, e2), (2, g1), (3, g2), (4, pos1), (5, pos2)):
        route = jnp.where(lane == idx, val, route)
    route_ref[...] = route
    route_t_ref[0] = route.T[0:ROUTE_ROWS, :]

    @pl.when(j == nj - 1)
    def _():
        pool_out_ref[0] = ext_ref[0:POOL_ROWS, :]
        st_out_ref[0] = st_ref[...]


def _mixer(x2d, pool0, st0, cnt0, weights, *, batch, seq, tb, lead_pad):
    nj = seq // tb
    total_rows = batch * seq
    shared = pool0.shape[0] == 1
    st_idx = (lambda b, j: (0, 0, 0)) if shared else (lambda b, j: (b, 0, 0))
    const2 = lambda b, j: (0, 0)
    tok = lambda b, j: (b * nj + j, 0)
    tok_out = tok

    ii = jnp.arange(tb)
    tril = ((ii[:, None] >= ii[None, :]) & (ii[:, None] // CHUNK == ii[None, :] // CHUNK)).astype(BF16)
    slow = (ii[:, None] > ii[None, :]).astype(BF16)

    in_specs = [
        pl.BlockSpec((tb, D_MODEL), tok),
        pl.BlockSpec((1, POOL_ROWS, D_POOL), st_idx),
        pl.BlockSpec((1, D_V, D_QK), st_idx),
        pl.BlockSpec((1, LANES), const2),
        pl.BlockSpec((tb, tb), const2),
        pl.BlockSpec((tb, tb), const2),
    ]
    for wgt in weights:
        in_specs.append(pl.BlockSpec(wgt.shape, (lambda b, j, n=wgt.ndim: (0,) * n)))
    args = [x2d, pool0, st0, cnt0, tril, slow, *weights]

    out_shape = [
        jax.ShapeDtypeStruct((total_rows, D_MODEL), F32),
        jax.ShapeDtypeStruct((HN_PLANES, total_rows, LANES), I32),
        jax.ShapeDtypeStruct((total_rows, LANES), F32),
        jax.ShapeDtypeStruct((batch * nj, ROUTE_ROWS, tb), F32),
        jax.ShapeDtypeStruct((batch, POOL_ROWS, D_POOL), F32),
        jax.ShapeDtypeStruct((batch, D_V, D_QK), F32),
        jax.ShapeDtypeStruct((1, LANES), F32),
    ]
    out_specs = [
        pl.BlockSpec((tb, D_MODEL), tok_out),
        pl.BlockSpec((HN_PLANES, tb, LANES), lambda b, j: (0, b * nj + j, 0)),
        pl.BlockSpec((tb, LANES), tok_out),
        pl.BlockSpec((1, ROUTE_ROWS, tb), lambda b, j: (b * nj + j, 0, 0)),
        pl.BlockSpec((1, POOL_ROWS, D_POOL), lambda b, j: (b, 0, 0)),
        pl.BlockSpec((1, D_V, D_QK), lambda b, j: (b, 0, 0)),
        pl.BlockSpec((1, LANES), const2),
    ]
    scratch = [
        pltpu.VMEM((POOL_ROWS + tb, D_POOL), F32),
        pltpu.VMEM((D_V, D_QK), F32),
        pltpu.VMEM((tb, D_V), F32),
        pltpu.VMEM((1, LANES), F32),
    ]
    return pl.pallas_call(
        functools.partial(_mixer_kernel, tb=tb, lead_pad=lead_pad),
        grid=(batch, nj),
        in_specs=in_specs,
        out_specs=out_specs,
        out_shape=out_shape,
        scratch_shapes=scratch,
        compiler_params=pltpu.CompilerParams(
            dimension_semantics=("arbitrary", "arbitrary"), vmem_limit_bytes=VMEM_LIMIT),
        name=f"mixer_tb{tb}_pad{lead_pad}",
    )(*args)


def _sc_mesh():
    return plsc.VectorSubcoreMesh(core_axis_name="c", subcore_axis_name="s",
                                  num_cores=SC_CORES, num_subcores=SC_SUBCORES)


def _sc_worker():
    return lax.axis_index("s") * SC_CORES + lax.axis_index("c")


def _plane_rows(dest, planes, rows_per_plane):
    offs = (jnp.arange(planes, dtype=I32) * rows_per_plane)[None, :, None]
    return dest[:, None, :] + offs


def _dispatch(hn_p, hn_s, dest0, dest1, n_rows):
    planes, t_p, _ = hn_p.shape
    t_s = hn_s.shape[1]
    n_cp = t_p // SC_ROWS
    cp = n_cp // SC_WORKERS
    n_cs = t_s // SC_ROWS
    assert t_p == cp * SC_ROWS * SC_WORKERS and t_s == n_cs * SC_ROWS and n_cs <= SC_WORKERS
    idx0 = _plane_rows(dest0, planes, n_rows)
    idx1 = _plane_rows(dest1, planes, n_rows)

    def body(hnp_hbm, hns_hbm, d0_hbm, d1_hbm, xs_hbm, rows_v, i0_v, i1_v, is0_v, is1_v, sem_in, sem_out):
        wid = _sc_worker()
        pltpu.sync_copy(d0_hbm.at[pl.ds(wid * cp, cp)], i0_v)
        pltpu.sync_copy(d1_hbm.at[pl.ds(wid * cp, cp)], i1_v)

        def move(src_hbm, src_rows, row0, i0, i1, c):
            loads = [pltpu.async_copy(src_hbm.at[pl.ds(p * src_rows + row0, SC_ROWS)], rows_v.at[p], sem_in)
                     for p in range(planes)]
            for cpy in loads:
                cpy.wait()
            stores = []
            for p in range(planes):
                stores.append(pltpu.async_copy(rows_v.at[p], xs_hbm.at[i0.at[c, p]], sem_out))
                stores.append(pltpu.async_copy(rows_v.at[p], xs_hbm.at[i1.at[c, p]], sem_out))
            for cpy in stores:
                cpy.wait()

        @pl.loop(0, cp)
        def _(c):
            move(hnp_hbm, t_p, (wid * cp + c) * SC_ROWS, i0_v, i1_v, c)

        @pl.when(wid < n_cs)
        def _():
            pltpu.sync_copy(d0_hbm.at[pl.ds(n_cp + wid, 1)], is0_v)
            pltpu.sync_copy(d1_hbm.at[pl.ds(n_cp + wid, 1)], is1_v)
            move(hns_hbm, t_s, wid * SC_ROWS, is0_v, is1_v, 0)

    xs = pl.kernel(
        body,
        out_type=jax.ShapeDtypeStruct((planes * n_rows, LANES), I32),
        mesh=_sc_mesh(),
        scratch_types=[
            pltpu.VMEM((planes, SC_ROWS, LANES), I32),
            pltpu.VMEM((cp, planes, SC_ROWS), I32),
            pltpu.VMEM((cp, planes, SC_ROWS), I32),
            pltpu.VMEM((1, planes, SC_ROWS), I32),
            pltpu.VMEM((1, planes, SC_ROWS), I32),
            pltpu.SemaphoreType.DMA,
            pltpu.SemaphoreType.DMA,
        ],
        name="moe_dispatch_sc",
    )(hn_p.reshape(planes * t_p, LANES), hn_s.reshape(planes * t_s, LANES), idx0, idx1)
    return xs.reshape(planes, n_rows, LANES)


def _gather(ys, dest0, dest1, t_p, t_s):
    planes, n_rows, _ = ys.shape
    n_cp = t_p // SC_ROWS
    cp = n_cp // SC_WORKERS
    n_cs = t_s // SC_ROWS
    stage = planes // 2
    assert t_p == cp * SC_ROWS * SC_WORKERS and t_s == n_cs * SC_ROWS and n_cs <= SC_WORKERS
    idx = (_plane_rows(dest0, planes, n_rows), _plane_rows(dest1, planes, n_rows))

    def body(ys_hbm, d0_hbm, d1_hbm, op_hbm, os_hbm, rows_v, i0_v, i1_v, is0_v, is1_v, sem_in, sem_out):
        wid = _sc_worker()
        pltpu.sync_copy(d0_hbm.at[pl.ds(wid * cp, cp)], i0_v)
        pltpu.sync_copy(d1_hbm.at[pl.ds(wid * cp, cp)], i1_v)

        def move(out_hbm, out_rows, row0, idx_refs, c):
            for j, i_v in enumerate(idx_refs):
                for p0 in range(0, planes, stage):
                    loads = [pltpu.async_copy(ys_hbm.at[i_v.at[c, p0 + p]], rows_v.at[p], sem_in)
                             for p in range(stage)]
                    for cpy in loads:
                        cpy.wait()
                    stores = [
                        pltpu.async_copy(
                            rows_v.at[p],
                            out_hbm.at[pl.ds((j * planes + p0 + p) * out_rows + row0, SC_ROWS)], sem_out)
                        for p in range(stage)]
                    for cpy in stores:
                        cpy.wait()

        @pl.loop(0, cp)
        def _(c):
            move(op_hbm, t_p, (wid * cp + c) * SC_ROWS, (i0_v, i1_v), c)

        @pl.when(wid < n_cs)
        def _():
            pltpu.sync_copy(d0_hbm.at[pl.ds(n_cp + wid, 1)], is0_v)
            pltpu.sync_copy(d1_hbm.at[pl.ds(n_cp + wid, 1)], is1_v)
            move(os_hbm, t_s, wid * SC_ROWS, (is0_v, is1_v), 0)

    out_p, out_s = pl.kernel(
        body,
        out_type=(jax.ShapeDtypeStruct((2 * planes * t_p, LANES), ys.dtype),
                  jax.ShapeDtypeStruct((2 * planes * t_s, LANES), ys.dtype)),
        mesh=_sc_mesh(),
        scratch_types=[
            pltpu.VMEM((stage, SC_ROWS, LANES), ys.dtype),
            pltpu.VMEM((cp, planes, SC_ROWS), I32),
            pltpu.VMEM((cp, planes, SC_ROWS), I32),
            pltpu.VMEM((1, planes, SC_ROWS), I32),
            pltpu.VMEM((1, planes, SC_ROWS), I32),
            pltpu.SemaphoreType.DMA,
            pltpu.SemaphoreType.DMA,
        ],
        name="moe_gather_sc",
    )(ys.reshape(planes * n_rows, LANES), *idx)
    return out_p.reshape(2, planes, t_p, LANES), out_s.reshape(2, planes, t_s, LANES)


def _expert_kernel(be_ref, bv_ref, nb_ref, xs_ref, wg_ref, wu_ref, wd_ref, ys_ref, wgu_s, wd_s):
    i = pl.program_id(0)
    live = i < nb_ref[0]
    prev = be_ref[jnp.maximum(i - 1, 0)]
    fresh = (i == 0) | (be_ref[i] != prev)

    @pl.when(live & fresh)
    def _():
        wgu_s[:, 0:D_EXPERT] = wg_ref[0].astype(BF16)
        wgu_s[:, D_EXPERT:2 * D_EXPERT] = wu_ref[0].astype(BF16)
        wd_s[...] = wd_ref[0].astype(BF16)

    @pl.when(live)
    def _():
        valid = lax.broadcasted_iota(I32, (MOE_BLK, LANES), 0) < bv_ref[i]
        los, his = [], []
        for p in range(HN_PLANES):
            x32 = lax.bitcast_convert_type(jnp.where(valid, xs_ref[p], 0), U32)
            los.append(lax.bitcast_convert_type(x32 << 16, F32).astype(BF16))
            his.append(lax.bitcast_convert_type(x32 & jnp.uint32(0xFFFF0000), F32).astype(BF16))
        xb = jnp.concatenate(los + his, axis=-1)
        gu = _dot(xb, wgu_s[...])
        gate = gu[:, 0:D_EXPERT]
        hmid = gate * jax.nn.sigmoid(gate) * gu[:, D_EXPERT:]
        ys = _dot(hmid.astype(BF16), wd_s[...])
        for p in range(Y_PLANES):
            ys_ref[p] = ys[:, p * LANES:(p + 1) * LANES]

    @pl.when(jnp.logical_not(live))
    def _():
        ys_ref[...] = jnp.zeros_like(ys_ref)


def _experts(block_e, block_valid, nb, xs, w_eg, w_eu, w_ed):
    n_rows = xs.shape[1]
    n_blocks = n_rows // MOE_BLK
    row_idx = lambda i, be, bv, nb: (0, jnp.minimum(i, nb[0] - 1), 0)
    w_idx = lambda i, be, bv, nb: (be[i], 0, 0)
    grid_spec = pltpu.PrefetchScalarGridSpec(
        num_scalar_prefetch=3,
        grid=(n_blocks,),
        in_specs=[
            pl.BlockSpec((HN_PLANES, MOE_BLK, LANES), row_idx),
            pl.BlockSpec((1, D_MODEL, D_EXPERT), w_idx),
            pl.BlockSpec((1, D_MODEL, D_EXPERT), w_idx),
            pl.BlockSpec((1, D_EXPERT, D_MODEL), w_idx),
        ],
        out_specs=pl.BlockSpec((Y_PLANES, MOE_BLK, LANES), lambda i, be, bv, nb: (0, i, 0)),
        scratch_shapes=[
            pltpu.VMEM((D_MODEL, 2 * D_EXPERT), BF16),
            pltpu.VMEM((D_EXPERT, D_MODEL), BF16),
        ],
    )
    return pl.pallas_call(
        _expert_kernel,
        grid_spec=grid_spec,
        out_shape=jax.ShapeDtypeStruct((Y_PLANES, n_rows, LANES), F32),
        compiler_params=pltpu.CompilerParams(
            dimension_semantics=("arbitrary",), vmem_limit_bytes=VMEM_LIMIT),
        name="moe_experts",
    )(block_e, block_valid, nb, xs, w_eg, w_eu, w_ed)


def _combine_kernel(h_ref, route_ref, nfin_ref, rows_ref, y_ref):
    route = route_ref[...]
    ys1 = jnp.concatenate([rows_ref[0, p] for p in range(Y_PLANES)], axis=-1)
    ys2 = jnp.concatenate([rows_ref[1, p] for p in range(Y_PLANES)], axis=-1)
    out = h_ref[...] + (ys1 * route[:, 2:3] + ys2 * route[:, 3:4])
    y_ref[...] = _rms(out, nfin_ref[...])


def _combine(h, route, norm_final, rows):
    n_rows = h.shape[0]
    tb = COMBINE_TB
    return pl.pallas_call(
        _combine_kernel,
        grid=(n_rows // tb,),
        in_specs=[
            pl.BlockSpec((tb, D_MODEL), lambda i: (i, 0)),
            pl.BlockSpec((tb, LANES), lambda i: (i, 0)),
            pl.BlockSpec((1, D_MODEL), lambda i: (0, 0)),
            pl.BlockSpec((2, Y_PLANES, tb, LANES), lambda i: (0, 0, i, 0)),
        ],
        out_specs=pl.BlockSpec((tb, D_MODEL), lambda i: (i, 0)),
        out_shape=jax.ShapeDtypeStruct((n_rows, D_MODEL), F32),
        compiler_params=pltpu.CompilerParams(
            dimension_semantics=("arbitrary",), vmem_limit_bytes=VMEM_LIMIT),
        name="moe_combine",
    )(h, route, norm_final, rows)


def _state_to_blockdiag(s):
    eye = jnp.eye(GLA_HEADS, dtype=s.dtype)
    bd = jnp.einsum('bhkv,hg->bhvgk', s, eye)
    return bd.reshape(s.shape[0], D_V, D_QK)


def _state_from_blockdiag(st):
    st5 = st.reshape(st.shape[0], GLA_HEADS, GLA_DV, GLA_HEADS, GLA_DK)
    blocks = [st5[:, hh, :, hh, :] for hh in range(GLA_HEADS)]
    return jnp.stack(blocks, axis=1).transpose(0, 1, 3, 2)


def kernel(x_prompt, x_sample, state_pool, state_gla, meta_tokens, norm_mix, w_in, w_gate_up, b_gate, w_pool, pool_scale, gla_norm, w_out, norm_ffn, w_router_group, b_router_group, w_router_expert, b_router_expert, w_expert_gate, w_expert_up, w_expert_down, norm_final):
    assert w_in.shape[0] == 1, "one encoder layer"
    batch, seq, _ = x_prompt.shape
    dec_batch, dec_seq, _ = x_sample.shape
    assert seq % MIX_TB == 0 and dec_seq == CHUNK and N_META <= CHUNK
    t_prompt = batch * seq
    t_sample = dec_batch * dec_seq
    t_all = t_prompt + t_sample
    assert t_prompt % COMBINE_TB == 0 and t_sample % COMBINE_TB == 0

    w_in0 = w_in[0]
    zpad = LANES - GATE_RANK
    nr = N_GROUPS + N_EXPERTS
    weights = (
        norm_mix[0][None, :],
        w_in0[:, :D_MAIN].astype(BF16),
        jnp.pad(w_in0[:, D_MAIN:], ((0, 0), (0, zpad))).astype(BF16),
        jnp.pad(w_gate_up[0], ((0, zpad), (0, 0))).astype(BF16),
        b_gate[0][None, :],
        w_pool[0].astype(BF16),
        pool_scale[0][None, :],
        gla_norm[0][None, :],
        w_out[0].astype(BF16),
        norm_ffn[0][None, :],
        jnp.pad(jnp.concatenate([w_router_group[0], w_router_expert[0]], axis=1),
                ((0, 0), (0, LANES - nr))).astype(BF16),
        jnp.pad(jnp.concatenate([b_router_group[0], b_router_expert[0]]), (0, LANES - nr))[None, :],
    )

    zero_cnt = jnp.zeros((1, LANES), F32)
    x_meta = jnp.pad(meta_tokens.astype(F32), ((CHUNK - N_META, 0), (0, 0)))
    meta = _mixer(x_meta, jnp.zeros((1, POOL_ROWS, D_POOL), F32), jnp.zeros((1, D_V, D_QK), F32), zero_cnt,
                  weights, batch=1, seq=CHUNK, tb=CHUNK, lead_pad=CHUNK - N_META)
    h_m, hn_m, route_m, rt_m, pool_m, st_m, cnt_m = meta
    del h_m, hn_m, route_m, rt_m, cnt_m
    h_p, hn_p, route_p, rt_p, pool_p, st_p, cnt_p = _mixer(
        x_prompt.reshape(t_prompt, D_MODEL), pool_m, st_m, zero_cnt, weights,
        batch=batch, seq=seq, tb=MIX_TB, lead_pad=0)
    pool_s0 = jnp.pad(state_pool[0], ((0, 0), (POOL_ROWS - POOL_PAD, 0), (0, 0)))
    h_s, hn_s, route_s, rt_s, pool_s, st_s, cnt_s = _mixer(
        x_sample.reshape(t_sample, D_MODEL), pool_s0, _state_to_blockdiag(state_gla[0]),
        cnt_p, weights, batch=dec_batch, seq=dec_seq, tb=CHUNK, lead_pad=0)

    counts = cnt_s[0, :N_EXPERTS].astype(I32)
    padded = (counts + MOE_BLK - 1) // MOE_BLK * MOE_BLK
    ends = jnp.cumsum(padded)
    pstart = ends - padded
    n_blocks = (2 * t_all + N_EXPERTS * (MOE_BLK - 1) + MOE_BLK - 1) // MOE_BLK
    nb = (ends[-1] // MOE_BLK).astype(I32)
    blk_ids = jnp.minimum(jnp.arange(n_blocks, dtype=I32), nb - 1)
    block_e = jnp.sum((ends[None, :] <= (blk_ids * MOE_BLK)[:, None]).astype(I32), axis=1)
    block_e = jnp.minimum(block_e, N_EXPERTS - 1)
    block_valid = jnp.clip(pstart[block_e] + counts[block_e] - blk_ids * MOE_BLK, 0, MOE_BLK)

    def dest_rows(rt):
        rt = rt.transpose(1, 0, 2).reshape(ROUTE_ROWS, -1)
        onehot = rt[0:2].astype(I32)[..., None] == jnp.arange(N_EXPERTS, dtype=I32)
        return jnp.sum(jnp.where(onehot, pstart, 0), axis=-1) + rt[4:6].astype(I32)

    dest = jnp.concatenate([dest_rows(rt_p), dest_rows(rt_s)], axis=1)
    dest0 = dest[0].reshape(t_all // SC_ROWS, SC_ROWS)
    dest1 = dest[1].reshape(t_all // SC_ROWS, SC_ROWS)
    xs = _dispatch(hn_p, hn_s, dest0, dest1, n_blocks * MOE_BLK)
    ys = _experts(block_e, block_valid.astype(I32), nb[None], xs,
                  w_expert_gate[0], w_expert_up[0], w_expert_down[0])
    rows_p, rows_s = _gather(ys, dest0, dest1, t_prompt, t_sample)
    nfin = norm_final[None, :]
    y_prompt = _combine(h_p, route_p, nfin, rows_p).reshape(batch, seq, D_MODEL)
    y_sample = _combine(h_s, route_s, nfin, rows_s).reshape(dec_batch, dec_seq, D_MODEL)
    new_pool_prompt = pool_p[:, POOL_ROWS - POOL_PAD:][None]
    new_gla_prompt = _state_from_blockdiag(st_p)[None]
    new_pool_sample = pool_s[:, POOL_ROWS - POOL_PAD:][None]
    new_gla_sample = _state_from_blockdiag(st_s)[None]
    return (y_prompt, y_sample, new_pool_prompt, new_gla_prompt, new_pool_sample, new_gla_sample)
```

```python
import functools

import jax
import jax.numpy as jnp
from jax import lax
from jax.experimental import pallas as pl
from jax.experimental.pallas import tpu as pltpu
from jax.experimental.pallas import tpu_sc as plsc

F32 = jnp.float32
BF16 = jnp.bfloat16
U32 = jnp.uint32
I32 = jnp.int32

D_MODEL = 1024
N_META = 16
CHUNK = 64
EPS = 1e-6
D_POOL = 512
POOL_WINDOWS = (2, 4, 8, 16)
POOL_GROUP_DIM = 128
POOL_PAD = 15
POOL_ROWS = 16
GLA_HEADS = 4
GLA_DK = 64
GLA_DV = 128
D_QK = 256
D_V = 512
GATE_RANK = 16
GATE_TAU = 16.0
D_MAIN = D_POOL + 2 * D_QK + 2 * D_V
N_GROUPS = 4
EXPERTS_PER_GROUP = 8
N_EXPERTS = 32
D_EXPERT = 512

LANES = 128
HALF = D_MODEL // 2
HN_PLANES = HALF // LANES
ROUTE_ROWS = 8
MIX_TB = 256
MOE_BLK = 512
COMBINE_TB = 256
VMEM_LIMIT = 48 * 1024 * 1024
SC_CORES = 2
SC_SUBCORES = 16
SC_WORKERS = SC_CORES * SC_SUBCORES
SC_ROWS = 128


def _rms(x, g):
    return x * lax.rsqrt(jnp.mean(x * x, axis=-1, keepdims=True) + EPS) * g


def _dot(a, b):
    return jnp.dot(a, b, preferred_element_type=F32)


def _dot_nt(a, b):
    return lax.dot_general(a, b, (((1,), (1,)), ((), ())), preferred_element_type=F32)


def _pack_planes(x, ref):
    xb = x.astype(BF16)
    lo = lax.bitcast_convert_type(xb[:, :HALF].astype(F32), U32) >> 16
    hi = lax.bitcast_convert_type(xb[:, HALF:].astype(F32), U32) & jnp.uint32(0xFFFF0000)
    packed = lax.bitcast_convert_type(lo | hi, I32)
    for p in range(HN_PLANES):
        ref[p] = packed[:, p * LANES:(p + 1) * LANES]


def _unpack_planes(planes):
    words = [lax.bitcast_convert_type(p, U32) for p in planes]
    los = [lax.bitcast_convert_type(w << 16, F32) for w in words]
    his = [lax.bitcast_convert_type(w & jnp.uint32(0xFFFF0000), F32) for w in words]
    return jnp.concatenate(los + his, axis=-1)


def _dot_tn(a, b):
    return lax.dot_general(a, b, (((0,), (0,)), ((), ())), preferred_element_type=F32)


def _mixer_kernel(x_ref, pool0_ref, st0_ref, cnt0_ref, tril_ref, slow_ref,
                  nmix_ref, wmain_ref, wz_ref, wgu_ref, bgate_ref, wpool_ref, pscale_ref,
                  gnorm_ref, wout_ref, nffn_ref, wr_ref, br_ref,
                  h_ref, hn_ref, route_ref, route_t_ref, pool_out_ref, st_out_ref, cnt_out_ref,
                  ext_ref, st_ref, kbd_ref, vbd_ref, sbd_ref, o_ref, cnt_ref, *, tb, lead_pad):
    b = pl.program_id(0)
    j = pl.program_id(1)
    nj = pl.num_programs(1)
    n_chunks = tb // CHUNK

    @pl.when((b == 0) & (j == 0))
    def _():
        cnt_ref[...] = cnt0_ref[...]
        kbd_ref[...] = jnp.zeros_like(kbd_ref)
        vbd_ref[...] = jnp.zeros_like(vbd_ref)
        sbd_ref[...] = jnp.zeros_like(sbd_ref)

    @pl.when(j == 0)
    def _():
        ext_ref[0:POOL_ROWS, :] = pool0_ref[0]
        st_ref[...] = st0_ref[0]

    def put_state(c, hh, s):
        sbd_ref[c, hh * GLA_DK:(hh + 1) * GLA_DK, hh * GLA_DV:(hh + 1) * GLA_DV] = s.astype(BF16)

    for hh in range(GLA_HEADS):
        put_state(0, hh, st_ref[hh])

    x = x_ref[...]
    xn = _rms(x, nmix_ref[...]).astype(BF16)
    proj = _dot(xn, wmain_ref[...])
    z = _dot(xn, wz_ref[...])
    u = proj[:, 0:D_POOL]
    q = proj[:, D_POOL:D_POOL + D_QK]
    k = proj[:, D_POOL + D_QK:D_POOL + 2 * D_QK]
    v = proj[:, D_POOL + 2 * D_QK:D_POOL + 2 * D_QK + D_V]
    r = proj[:, D_POOL + 2 * D_QK + D_V:D_MAIN]

    row = lax.broadcasted_iota(I32, (tb, 1), 0)

    ext_ref[POOL_ROWS:POOL_ROWS + tb, :] = u
    ext = ext_ref[...]
    pys = []
    for g, w in enumerate(POOL_WINDOWS):
        sl = slice(g * POOL_GROUP_DIM, (g + 1) * POOL_GROUP_DIM)
        acc = ext[:, sl]
        for s in range(g + 1):
            acc = acc + pltpu.roll(acc, 1 << s, axis=0)
        win = acc[POOL_ROWS:, :]
        if lead_pad:
            cnt = jnp.clip(row - lead_pad + 1, 1, w).astype(F32)
            pooled = win / cnt - u[:, sl]
        else:
            pooled = win * (1.0 / w) - u[:, sl]
        pys.append(_dot(pooled.astype(BF16), wpool_ref[g]))
    pool_y = jnp.concatenate(pys, axis=-1) * pscale_ref[...]
    ext_ref[0:POOL_ROWS, :] = ext_ref[tb:tb + POOL_ROWS, :]

    gpre = _dot(z.astype(BF16), wgu_ref[...]) + bgate_ref[...]
    log_a = jax.nn.log_sigmoid(gpre) * (1.0 / GATE_TAU)
    if lead_pad:
        log_a = jnp.where(row >= lead_pad, log_a, 0.0)
    a_hi = log_a.astype(BF16)
    a_lo = (log_a - a_hi.astype(F32)).astype(BF16)
    tril = tril_ref[...]
    bcum = _dot(tril, a_hi) + _dot(tril, a_lo)
    eb = jnp.exp(bcum)
    qi = q * (GLA_DK ** -0.5) * eb
    ki = k * jnp.exp(-bcum)

    rr = lax.broadcasted_iota(I32, (CHUNK, GLA_HEADS * CHUNK), 0)
    cc = lax.broadcasted_iota(I32, (CHUNK, GLA_HEADS * CHUNK), 1)
    causal = (cc % CHUNK) <= rr

    lasts = [eb[(c + 1) * CHUNK - 1:(c + 1) * CHUNK, :] for c in range(n_chunks)]
    dcol = jnp.concatenate(lasts + [jnp.zeros((LANES - n_chunks, D_QK), F32)], axis=0).T

    for c in range(n_chunks):
        rows = slice(c * CHUNK, (c + 1) * CHUNK)
        qi_c = qi[rows].astype(BF16)
        ki_c = ki[rows]
        kd_c = (ki_c * lasts[c]).astype(BF16)
        ki_cb = ki_c.astype(BF16)
        v_cb = v[rows].astype(BF16)
        for hh in range(GLA_HEADS):
            ks = slice(hh * GLA_DK, (hh + 1) * GLA_DK)
            vs = slice(hh * GLA_DV, (hh + 1) * GLA_DV)
            kbd_ref[c, hh * CHUNK:(hh + 1) * CHUNK, ks] = ki_cb[:, ks]
            vbd_ref[c, hh * CHUNK:(hh + 1) * CHUNK, vs] = v_cb[:, vs]
        scores = _dot_nt(qi_c, kbd_ref[c])
        p = jnp.where(causal, scores, 0.0).astype(BF16)
        o_ref[rows, :] = _dot(p, vbd_ref[c]) + _dot(qi_c, sbd_ref[c])
        for hh in range(GLA_HEADS):
            ks = slice(hh * GLA_DK, (hh + 1) * GLA_DK)
            vs = slice(hh * GLA_DV, (hh + 1) * GLA_DV)
            kv = _dot_tn(kd_c[:, ks], v_cb[:, vs])
            s_new = st_ref[hh] * dcol[ks, c:c + 1] + kv
            st_ref[hh] = s_new
            if c + 1 < n_chunks:
                put_state(c + 1, hh, s_new)

    o = o_ref[...]
    ons = []
    for hh in range(GLA_HEADS):
        oh = o[:, hh * GLA_DV:(hh + 1) * GLA_DV]
        ons.append(oh * lax.rsqrt(jnp.mean(oh * oh, axis=-1, keepdims=True) + EPS))
    og = jnp.concatenate(ons, axis=-1) * gnorm_ref[...] * (r * jax.nn.sigmoid(r))
    mix = _dot(jnp.concatenate([pool_y, og], axis=-1).astype(BF16), wout_ref[...])

    h = x + mix
    h_ref[...] = h
    hn = _rms(h, nffn_ref[...]).astype(BF16)
    _pack_planes(hn, hn_ref)

    logits = _dot(hn, wr_ref[...]) + br_ref[...]
    lane = lax.broadcasted_iota(I32, (tb, LANES), 1)
    lanef = lane.astype(F32)
    neg = jnp.float32(-jnp.inf)
    big = jnp.float32(LANES)
    is_g = lane < N_GROUPS
    gl = jnp.where(is_g, logits, neg)
    gmax = jnp.max(gl, axis=-1, keepdims=True)
    gsum = jnp.sum(jnp.where(is_g, jnp.exp(logits - gmax), 0.0), axis=-1, keepdims=True)
    p_g = 1.0 / gsum
    gidx = jnp.min(jnp.where(is_g & (logits == gmax), lanef, big), axis=-1, keepdims=True)
    e_lo = N_GROUPS + EXPERTS_PER_GROUP * gidx
    is_e = (lanef >= e_lo) & (lanef < e_lo + EXPERTS_PER_GROUP)
    el = jnp.where(is_e, logits, neg)
    m1 = jnp.max(el, axis=-1, keepdims=True)
    i1 = jnp.min(jnp.where(is_e & (logits == m1), lanef, big), axis=-1, keepdims=True)
    is_e2 = is_e & (lanef != i1)
    el2 = jnp.where(is_e2, logits, neg)
    m2 = jnp.max(el2, axis=-1, keepdims=True)
    i2 = jnp.min(jnp.where(is_e2 & (logits == m2), lanef, big), axis=-1, keepdims=True)
    t2 = jnp.exp(m2 - m1)
    den = 1.0 + t2
    g1 = p_g / den
    g2 = p_g * t2 / den
    e1 = i1 - N_GROUPS
    e2 = i2 - N_GROUPS

    oh1 = lanef == e1
    oh2 = lanef == e2
    both = jnp.where(oh1 | oh2, 1.0, 0.0)
    before = _dot(slow_ref[...], both.astype(BF16)) + cnt_ref[...]
    pos1 = jnp.sum(jnp.where(oh1, before, 0.0), axis=-1, keepdims=True)
    pos2 = jnp.sum(jnp.where(oh2, before, 0.0), axis=-1, keepdims=True)
    cnt_new = cnt_ref[...] + jnp.sum(both, axis=0, keepdims=True)
    cnt_ref[...] = cnt_new
    cnt_out_ref[...] = cnt_new

    route = jnp.where(lane == 0, e1, 0.0)
    for idx, val in ((1, e2), (2, g1), (3, g2), (4, pos1), (5, pos2)):
        route = jnp.where(lane == idx, val, route)
    route_ref[...] = route
    route_t_ref[0] = route.T[0:ROUTE_ROWS, :]

    @pl.when(j == nj - 1)
    def _():
        pool_out_ref[0] = ext_ref[0:POOL_ROWS, :]
        st_out_ref[0] = st_ref[...]


def _mixer(x2d, pool0, st0, cnt0, weights, *, batch, seq, tb, lead_pad):
    nj = seq // tb
    total_rows = batch * seq
    shared = pool0.shape[0] == 1
    st_idx = (lambda b, j: (0, 0, 0)) if shared else (lambda b, j: (b, 0, 0))
    gla_idx = (lambda b, j: (0, 0, 0, 0)) if shared else (lambda b, j: (b, 0, 0, 0))
    const2 = lambda b, j: (0, 0)
    tok = lambda b, j: (b * nj + j, 0)
    tok_out = tok

    ii = jnp.arange(tb)
    tril = ((ii[:, None] >= ii[None, :]) & (ii[:, None] // CHUNK == ii[None, :] // CHUNK)).astype(BF16)
    slow = (ii[:, None] > ii[None, :]).astype(BF16)

    in_specs = [
        pl.BlockSpec((tb, D_MODEL), tok),
        pl.BlockSpec((1, POOL_ROWS, D_POOL), st_idx),
        pl.BlockSpec((1, GLA_HEADS, GLA_DK, GLA_DV), gla_idx),
        pl.BlockSpec((1, LANES), const2),
        pl.BlockSpec((tb, tb), const2),
        pl.BlockSpec((tb, tb), const2),
    ]
    for wgt in weights:
        in_specs.append(pl.BlockSpec(wgt.shape, (lambda b, j, n=wgt.ndim: (0,) * n)))
    args = [x2d, pool0, st0, cnt0, tril, slow, *weights]

    out_shape = [
        jax.ShapeDtypeStruct((total_rows, D_MODEL), F32),
        jax.ShapeDtypeStruct((HN_PLANES, total_rows, LANES), I32),
        jax.ShapeDtypeStruct((total_rows, LANES), F32),
        jax.ShapeDtypeStruct((batch * nj, ROUTE_ROWS, tb), F32),
        jax.ShapeDtypeStruct((batch, POOL_ROWS, D_POOL), F32),
        jax.ShapeDtypeStruct((batch, GLA_HEADS, GLA_DK, GLA_DV), F32),
        jax.ShapeDtypeStruct((1, LANES), F32),
    ]
    out_specs = [
        pl.BlockSpec((tb, D_MODEL), tok_out),
        pl.BlockSpec((HN_PLANES, tb, LANES), lambda b, j: (0, b * nj + j, 0)),
        pl.BlockSpec((tb, LANES), tok_out),
        pl.BlockSpec((1, ROUTE_ROWS, tb), lambda b, j: (b * nj + j, 0, 0)),
        pl.BlockSpec((1, POOL_ROWS, D_POOL), lambda b, j: (b, 0, 0)),
        pl.BlockSpec((1, GLA_HEADS, GLA_DK, GLA_DV), lambda b, j: (b, 0, 0, 0)),
        pl.BlockSpec((1, LANES), const2),
    ]
    scratch = [
        pltpu.VMEM((POOL_ROWS + tb, D_POOL), F32),
        pltpu.VMEM((GLA_HEADS, GLA_DK, GLA_DV), F32),
        pltpu.VMEM((tb // CHUNK, GLA_HEADS * CHUNK, D_QK), BF16),
        pltpu.VMEM((tb // CHUNK, GLA_HEADS * CHUNK, D_V), BF16),
        pltpu.VMEM((tb // CHUNK, D_QK, D_V), BF16),
        pltpu.VMEM((tb, D_V), F32),
        pltpu.VMEM((1, LANES), F32),
    ]
    return pl.pallas_call(
        functools.partial(_mixer_kernel, tb=tb, lead_pad=lead_pad),
        grid=(batch, nj),
        in_specs=in_specs,
        out_specs=out_specs,
        out_shape=out_shape,
        scratch_shapes=scratch,
        compiler_params=pltpu.CompilerParams(
            dimension_semantics=("arbitrary", "arbitrary"), vmem_limit_bytes=VMEM_LIMIT),
        name=f"mixer_tb{tb}_pad{lead_pad}",
    )(*args)


def _sc_mesh():
    return plsc.VectorSubcoreMesh(core_axis_name="c", subcore_axis_name="s",
                                  num_cores=SC_CORES, num_subcores=SC_SUBCORES)


def _sc_worker():
    return lax.axis_index("s") * SC_CORES + lax.axis_index("c")


def _plane_rows(dest, planes, rows_per_plane):
    offs = (jnp.arange(planes, dtype=I32) * rows_per_plane)[None, :, None]
    return dest[:, None, :] + offs


def _dispatch(hn_p, hn_s, dest0, dest1, n_rows):
    planes, t_p, _ = hn_p.shape
    t_s = hn_s.shape[1]
    n_cp = t_p // SC_ROWS
    cp = n_cp // SC_WORKERS
    n_cs = t_s // SC_ROWS
    assert t_p == cp * SC_ROWS * SC_WORKERS and t_s == n_cs * SC_ROWS and n_cs <= SC_WORKERS
    idx0 = _plane_rows(dest0, planes, n_rows)
    idx1 = _plane_rows(dest1, planes, n_rows)

    def body(hnp_hbm, hns_hbm, d0_hbm, d1_hbm, xs_hbm, rows_v, i0_v, i1_v, is0_v, is1_v, sem_in, sem_out):
        wid = _sc_worker()
        pltpu.sync_copy(d0_hbm.at[pl.ds(wid * cp, cp)], i0_v)
        pltpu.sync_copy(d1_hbm.at[pl.ds(wid * cp, cp)], i1_v)

        def move(src_hbm, src_rows, row0, i0, i1, c):
            loads = [pltpu.async_copy(src_hbm.at[pl.ds(p * src_rows + row0, SC_ROWS)], rows_v.at[p], sem_in)
                     for p in range(planes)]
            for cpy in loads:
                cpy.wait()
            stores = []
            for p in range(planes):
                stores.append(pltpu.async_copy(rows_v.at[p], xs_hbm.at[i0.at[c, p]], sem_out))
                stores.append(pltpu.async_copy(rows_v.at[p], xs_hbm.at[i1.at[c, p]], sem_out))
            for cpy in stores:
                cpy.wait()

        @pl.loop(0, cp)
        def _(c):
            move(hnp_hbm, t_p, (wid * cp + c) * SC_ROWS, i0_v, i1_v, c)

        @pl.when(wid < n_cs)
        def _():
            pltpu.sync_copy(d0_hbm.at[pl.ds(n_cp + wid, 1)], is0_v)
            pltpu.sync_copy(d1_hbm.at[pl.ds(n_cp + wid, 1)], is1_v)
            move(hns_hbm, t_s, wid * SC_ROWS, is0_v, is1_v, 0)

    xs = pl.kernel(
        body,
        out_type=jax.ShapeDtypeStruct((planes * n_rows, LANES), I32),
        mesh=_sc_mesh(),
        scratch_types=[
            pltpu.VMEM((planes, SC_ROWS, LANES), I32),
            pltpu.VMEM((cp, planes, SC_ROWS), I32),
            pltpu.VMEM((cp, planes, SC_ROWS), I32),
            pltpu.VMEM((1, planes, SC_ROWS), I32),
            pltpu.VMEM((1, planes, SC_ROWS), I32),
            pltpu.SemaphoreType.DMA,
            pltpu.SemaphoreType.DMA,
        ],
        name="moe_dispatch_sc",
    )(hn_p.reshape(planes * t_p, LANES), hn_s.reshape(planes * t_s, LANES), idx0, idx1)
    return xs.reshape(planes, n_rows, LANES)


def _gather(ys, dest0, dest1, t_p, t_s):
    planes, n_rows, _ = ys.shape
    n_cp = t_p // SC_ROWS
    cp = n_cp // SC_WORKERS
    n_cs = t_s // SC_ROWS
    stage = planes
    assert t_p == cp * SC_ROWS * SC_WORKERS and t_s == n_cs * SC_ROWS and n_cs <= SC_WORKERS
    idx = (_plane_rows(dest0, planes, n_rows), _plane_rows(dest1, planes, n_rows))

    def body(ys_hbm, d0_hbm, d1_hbm, op_hbm, os_hbm, rows_v, i0_v, i1_v, is0_v, is1_v, sem_in, sem_out):
        wid = _sc_worker()
        pltpu.sync_copy(d0_hbm.at[pl.ds(wid * cp, cp)], i0_v)
        pltpu.sync_copy(d1_hbm.at[pl.ds(wid * cp, cp)], i1_v)

        def move(out_hbm, out_rows, row0, idx_refs, c):
            for j, i_v in enumerate(idx_refs):
                for p0 in range(0, planes, stage):
                    loads = [pltpu.async_copy(ys_hbm.at[i_v.at[c, p0 + p]], rows_v.at[p], sem_in)
                             for p in range(stage)]
                    for cpy in loads:
                        cpy.wait()
                    stores = [
                        pltpu.async_copy(
                            rows_v.at[p],
                            out_hbm.at[pl.ds((j * planes + p0 + p) * out_rows + row0, SC_ROWS)], sem_out)
                        for p in range(stage)]
                    for cpy in stores:
                        cpy.wait()

        @pl.loop(0, cp)
        def _(c):
            move(op_hbm, t_p, (wid * cp + c) * SC_ROWS, (i0_v, i1_v), c)

        @pl.when(wid < n_cs)
        def _():
            pltpu.sync_copy(d0_hbm.at[pl.ds(n_cp + wid, 1)], is0_v)
            pltpu.sync_copy(d1_hbm.at[pl.ds(n_cp + wid, 1)], is1_v)
            move(os_hbm, t_s, wid * SC_ROWS, (is0_v, is1_v), 0)

    out_p, out_s = pl.kernel(
        body,
        out_type=(jax.ShapeDtypeStruct((2 * planes * t_p, LANES), ys.dtype),
                  jax.ShapeDtypeStruct((2 * planes * t_s, LANES), ys.dtype)),
        mesh=_sc_mesh(),
        scratch_types=[
            pltpu.VMEM((stage, SC_ROWS, LANES), ys.dtype),
            pltpu.VMEM((cp, planes, SC_ROWS), I32),
            pltpu.VMEM((cp, planes, SC_ROWS), I32),
            pltpu.VMEM((1, planes, SC_ROWS), I32),
            pltpu.VMEM((1, planes, SC_ROWS), I32),
            pltpu.SemaphoreType.DMA,
            pltpu.SemaphoreType.DMA,
        ],
        name="moe_gather_sc",
    )(ys.reshape(planes * n_rows, LANES), *idx)
    return out_p.reshape(2, planes, t_p, LANES), out_s.reshape(2, planes, t_s, LANES)


def _expert_kernel(be_ref, bv_ref, nb_ref, xs_ref, wg_ref, wu_ref, wd_ref, ys_ref, wgu_s, wd_s):
    i = pl.program_id(0)
    live = i < nb_ref[0]
    prev = be_ref[jnp.maximum(i - 1, 0)]
    fresh = (i == 0) | (be_ref[i] != prev)

    @pl.when(live & fresh)
    def _():
        wgu_s[:, 0:D_EXPERT] = wg_ref[0].astype(BF16)
        wgu_s[:, D_EXPERT:2 * D_EXPERT] = wu_ref[0].astype(BF16)
        wd_s[...] = wd_ref[0].astype(BF16)

    @pl.when(live)
    def _():
        valid = lax.broadcasted_iota(I32, (MOE_BLK, LANES), 0) < bv_ref[i]
        xb = _unpack_planes([jnp.where(valid, xs_ref[p], 0) for p in range(HN_PLANES)]).astype(BF16)
        gu = _dot(xb, wgu_s[...])
        gate = gu[:, 0:D_EXPERT]
        hmid = gate * jax.nn.sigmoid(gate) * gu[:, D_EXPERT:]
        _pack_planes(_dot(hmid.astype(BF16), wd_s[...]), ys_ref)

    @pl.when(jnp.logical_not(live))
    def _():
        ys_ref[...] = jnp.zeros_like(ys_ref)


def _experts(block_e, block_valid, nb, xs, w_eg, w_eu, w_ed):
    n_rows = xs.shape[1]
    n_blocks = n_rows // MOE_BLK
    row_idx = lambda i, be, bv, nb: (0, jnp.minimum(i, nb[0] - 1), 0)
    w_idx = lambda i, be, bv, nb: (be[i], 0, 0)
    grid_spec = pltpu.PrefetchScalarGridSpec(
        num_scalar_prefetch=3,
        grid=(n_blocks,),
        in_specs=[
            pl.BlockSpec((HN_PLANES, MOE_BLK, LANES), row_idx),
            pl.BlockSpec((1, D_MODEL, D_EXPERT), w_idx),
            pl.BlockSpec((1, D_MODEL, D_EXPERT), w_idx),
            pl.BlockSpec((1, D_EXPERT, D_MODEL), w_idx),
        ],
        out_specs=pl.BlockSpec((HN_PLANES, MOE_BLK, LANES), lambda i, be, bv, nb: (0, i, 0)),
        scratch_shapes=[
            pltpu.VMEM((D_MODEL, 2 * D_EXPERT), BF16),
            pltpu.VMEM((D_EXPERT, D_MODEL), BF16),
        ],
    )
    return pl.pallas_call(
        _expert_kernel,
        grid_spec=grid_spec,
        out_shape=jax.ShapeDtypeStruct((HN_PLANES, n_rows, LANES), I32),
        compiler_params=pltpu.CompilerParams(
            dimension_semantics=("arbitrary",), vmem_limit_bytes=VMEM_LIMIT),
        name="moe_experts",
    )(block_e, block_valid, nb, xs, w_eg, w_eu, w_ed)


def _combine_kernel(h_ref, route_ref, nfin_ref, rows_ref, y_ref):
    route = route_ref[...]
    ys1 = _unpack_planes([rows_ref[0, p] for p in range(HN_PLANES)])
    ys2 = _unpack_planes([rows_ref[1, p] for p in range(HN_PLANES)])
    out = h_ref[...] + (ys1 * route[:, 2:3] + ys2 * route[:, 3:4])
    y_ref[...] = _rms(out, nfin_ref[...])


def _combine(h, route, norm_final, rows):
    n_rows = h.shape[0]
    tb = COMBINE_TB
    return pl.pallas_call(
        _combine_kernel,
        grid=(n_rows // tb,),
        in_specs=[
            pl.BlockSpec((tb, D_MODEL), lambda i: (i, 0)),
            pl.BlockSpec((tb, LANES), lambda i: (i, 0)),
            pl.BlockSpec((1, D_MODEL), lambda i: (0, 0)),
            pl.BlockSpec((2, HN_PLANES, tb, LANES), lambda i: (0, 0, i, 0)),
        ],
        out_specs=pl.BlockSpec((tb, D_MODEL), lambda i: (i, 0)),
        out_shape=jax.ShapeDtypeStruct((n_rows, D_MODEL), F32),
        compiler_params=pltpu.CompilerParams(
            dimension_semantics=("arbitrary",), vmem_limit_bytes=VMEM_LIMIT),
        name="moe_combine",
    )(h, route, norm_final, rows)


def kernel(x_prompt, x_sample, state_pool, state_gla, meta_tokens, norm_mix, w_in, w_gate_up, b_gate, w_pool, pool_scale, gla_norm, w_out, norm_ffn, w_router_group, b_router_group, w_router_expert, b_router_expert, w_expert_gate, w_expert_up, w_expert_down, norm_final):
    assert w_in.shape[0] == 1, "one encoder layer"
    batch, seq, _ = x_prompt.shape
    dec_batch, dec_seq, _ = x_sample.shape
    assert seq % MIX_TB == 0 and dec_seq == CHUNK and N_META <= CHUNK
    t_prompt = batch * seq
    t_sample = dec_batch * dec_seq
    t_all = t_prompt + t_sample
    assert t_prompt % COMBINE_TB == 0 and t_sample % COMBINE_TB == 0

    w_in0 = w_in[0]
    zpad = LANES - GATE_RANK
    nr = N_GROUPS + N_EXPERTS
    weights = (
        norm_mix[0][None, :],
        w_in0[:, :D_MAIN].astype(BF16),
        jnp.pad(w_in0[:, D_MAIN:], ((0, 0), (0, zpad))).astype(BF16),
        jnp.pad(w_gate_up[0], ((0, zpad), (0, 0))).astype(BF16),
        b_gate[0][None, :],
        w_pool[0].astype(BF16),
        pool_scale[0][None, :],
        gla_norm[0][None, :],
        w_out[0].astype(BF16),
        norm_ffn[0][None, :],
        jnp.pad(jnp.concatenate([w_router_group[0], w_router_expert[0]], axis=1),
                ((0, 0), (0, LANES - nr))).astype(BF16),
        jnp.pad(jnp.concatenate([b_router_group[0], b_router_expert[0]]), (0, LANES - nr))[None, :],
    )

    zero_cnt = jnp.zeros((1, LANES), F32)
    x_meta = jnp.pad(meta_tokens.astype(F32), ((CHUNK - N_META, 0), (0, 0)))
    meta = _mixer(x_meta, jnp.zeros((1, POOL_ROWS, D_POOL), F32),
                  jnp.zeros((1, GLA_HEADS, GLA_DK, GLA_DV), F32), zero_cnt,
                  weights, batch=1, seq=CHUNK, tb=CHUNK, lead_pad=CHUNK - N_META)
    h_m, hn_m, route_m, rt_m, pool_m, st_m, cnt_m = meta
    del h_m, hn_m, route_m, rt_m, cnt_m
    h_p, hn_p, route_p, rt_p, pool_p, st_p, cnt_p = _mixer(
        x_prompt.reshape(t_prompt, D_MODEL), pool_m, st_m, zero_cnt, weights,
        batch=batch, seq=seq, tb=MIX_TB, lead_pad=0)
    pool_s0 = jnp.pad(state_pool[0], ((0, 0), (POOL_ROWS - POOL_PAD, 0), (0, 0)))
    h_s, hn_s, route_s, rt_s, pool_s, st_s, cnt_s = _mixer(
        x_sample.reshape(t_sample, D_MODEL), pool_s0, state_gla[0].astype(F32),
        cnt_p, weights, batch=dec_batch, seq=dec_seq, tb=CHUNK, lead_pad=0)

    counts = cnt_s[0, :N_EXPERTS].astype(I32)
    padded = (counts + MOE_BLK - 1) // MOE_BLK * MOE_BLK
    ends = jnp.cumsum(padded)
    pstart = ends - padded
    n_blocks = (2 * t_all + N_EXPERTS * (MOE_BLK - 1) + MOE_BLK - 1) // MOE_BLK
    nb = (ends[-1] // MOE_BLK).astype(I32)
    blk_ids = jnp.minimum(jnp.arange(n_blocks, dtype=I32), nb - 1)
    block_e = jnp.sum((ends[None, :] <= (blk_ids * MOE_BLK)[:, None]).astype(I32), axis=1)
    block_e = jnp.minimum(block_e, N_EXPERTS - 1)
    owner = block_e[:, None] == jnp.arange(N_EXPERTS, dtype=I32)
    row_end = jnp.sum(jnp.where(owner, pstart + counts, 0), axis=1)
    block_valid = jnp.clip(row_end - blk_ids * MOE_BLK, 0, MOE_BLK)

    def dest_rows(rt):
        rt = rt.transpose(1, 0, 2).reshape(ROUTE_ROWS, -1)
        onehot = rt[0:2].astype(I32)[..., None] == jnp.arange(N_EXPERTS, dtype=I32)
        return jnp.sum(jnp.where(onehot, pstart, 0), axis=-1) + rt[4:6].astype(I32)

    dest = jnp.concatenate([dest_rows(rt_p), dest_rows(rt_s)], axis=1)
    dest0 = dest[0].reshape(t_all // SC_ROWS, SC_ROWS)
    dest1 = dest[1].reshape(t_all // SC_ROWS, SC_ROWS)
    xs = _dispatch(hn_p, hn_s, dest0, dest1, n_blocks * MOE_BLK)
    ys = _experts(block_e, block_valid.astype(I32), nb[None], xs,
                  w_expert_gate[0], w_expert_up[0], w_expert_down[0])
    rows_p, rows_s = _gather(ys, dest0, dest1, t_prompt, t_sample)
    nfin = norm_final[None, :]
    y_prompt = _combine(h_p, route_p, nfin, rows_p).reshape(batch, seq, D_MODEL)
    y_sample = _combine(h_s, route_s, nfin, rows_s).reshape(dec_batch, dec_seq, D_MODEL)
    new_pool_prompt = pool_p[:, POOL_ROWS - POOL_PAD:][None]
    new_gla_prompt = st_p[None]
    new_pool_sample = pool_s[:, POOL_ROWS - POOL_PAD:][None]
    new_gla_sample = st_s[None]
    return (y_prompt, y_sample, new_pool_prompt, new_gla_prompt, new_pool_sample, new_gla_sample)
```

```python
import functools

import jax
import jax.numpy as jnp
from jax import lax
from jax.experimental import pallas as pl
from jax.experimental.pallas import tpu as pltpu
from jax.experimental.pallas import tpu_sc as plsc

F32 = jnp.float32
BF16 = jnp.bfloat16
U32 = jnp.uint32
I32 = jnp.int32

D_MODEL = 1024
N_META = 16
CHUNK = 64
EPS = 1e-6
D_POOL = 512
POOL_WINDOWS = (2, 4, 8, 16)
POOL_GROUP_DIM = 128
POOL_PAD = 15
POOL_ROWS = 16
GLA_HEADS = 4
GLA_DK = 64
GLA_DV = 128
D_QK = 256
D_V = 512
GATE_RANK = 16
GATE_TAU = 16.0
D_MAIN = D_POOL + 2 * D_QK + 2 * D_V
N_GROUPS = 4
EXPERTS_PER_GROUP = 8
N_EXPERTS = 32
D_EXPERT = 512

LANES = 128
HALF = D_MODEL // 2
HN_PLANES = HALF // LANES
ROUTE_ROWS = 8
ROUTER_ROWS = 64
MIX_TB = 256
FRONT_TILE = 256
FRONT_PLAN = (3, 1, 1, 1, 1, 1)
MOE_BLK = 512
COMBINE_TB = 512
VMEM_LIMIT = 48 * 1024 * 1024
SC_CORES = 2
SC_SUBCORES = 16
SC_WORKERS = SC_CORES * SC_SUBCORES
SC_ROWS = 128


def _rms(x, g):
    return x * lax.rsqrt(jnp.mean(x * x, axis=-1, keepdims=True) + EPS) * g


def _dot(a, b):
    return jnp.dot(a, b, preferred_element_type=F32)


def _dot_nt(a, b):
    return lax.dot_general(a, b, (((1,), (1,)), ((), ())), preferred_element_type=F32)


def _pack_planes(x, ref):
    xb = x.astype(BF16)
    lo = lax.bitcast_convert_type(xb[:, :HALF].astype(F32), U32) >> 16
    hi = lax.bitcast_convert_type(xb[:, HALF:].astype(F32), U32) & jnp.uint32(0xFFFF0000)
    packed = lax.bitcast_convert_type(lo | hi, I32)
    for p in range(HN_PLANES):
        ref[p] = packed[:, p * LANES:(p + 1) * LANES]


def _unpack_planes(planes):
    words = [lax.bitcast_convert_type(p, U32) for p in planes]
    los = [lax.bitcast_convert_type(w << 16, F32) for w in words]
    his = [lax.bitcast_convert_type(w & jnp.uint32(0xFFFF0000), F32) for w in words]
    return jnp.concatenate(los + his, axis=-1)


def _dot_tn(a, b):
    return lax.dot_general(a, b, (((0,), (0,)), ((), ())), preferred_element_type=F32)


def _mixer_kernel(x_ref, pool0_ref, st0_ref, cnt0_ref, tril_ref, sup_ref,
                  nmix_ref, wmain_ref, wz_ref, wgu_ref, bgate_ref, wpool_ref, pscale_ref,
                  gnorm_ref, wout_ref, nffn_ref, wr_ref, br_ref,
                  h_ref, hn_ref, route_ref, route_t_ref, pool_out_ref, st_out_ref, cnt_out_ref,
                  ext_ref, st_ref, kbd_ref, vbd_ref, sbd_ref, o_ref, cnt_ref,
                  xs_ref, proj_ref, z_ref, *, tb, nj, lead_pad):
    s = pl.program_id(0)
    back = jnp.maximum(s - 1, 0)
    j = lax.rem(back, nj)
    n_chunks = tb // CHUNK
    wr_slot = lax.rem(s, 2)
    rd_slot = 1 - wr_slot

    @pl.when(s == 0)
    def _():
        kbd_ref[...] = jnp.zeros_like(kbd_ref)
        vbd_ref[...] = jnp.zeros_like(vbd_ref)
        sbd_ref[...] = jnp.zeros_like(sbd_ref)
        xs_ref[1] = jnp.zeros((tb, D_MODEL), F32)
        proj_ref[1] = jnp.zeros((tb, D_MAIN), F32)
        z_ref[1] = jnp.zeros((tb, LANES), F32)

    @pl.when(s <= 1)
    def _():
        cnt_ref[...] = cnt0_ref[...]

    @pl.when(j == 0)
    def _():
        ext_ref[0:POOL_ROWS, :] = pool0_ref[0]
        st_ref[...] = st0_ref[0]

    def put_state(c, hh, st):
        sbd_ref[c, hh * GLA_DK:(hh + 1) * GLA_DK, hh * GLA_DV:(hh + 1) * GLA_DV] = st.astype(BF16)

    for hh in range(GLA_HEADS):
        put_state(0, hh, st_ref[hh])

    x_new = x_ref[...]
    xn = _rms(x_new, nmix_ref[...]).astype(BF16)
    xs_ref[wr_slot] = x_new

    tiles_done = [0]

    def front_tiles(stage):
        for t in range(tiles_done[0], tiles_done[0] + FRONT_PLAN[stage]):
            cols = slice(t * FRONT_TILE, (t + 1) * FRONT_TILE)
            proj_ref[wr_slot, :, cols] = _dot(xn, wmain_ref[:, cols])
        tiles_done[0] += FRONT_PLAN[stage]

    front_tiles(0)
    z_ref[wr_slot] = _dot(xn, wz_ref[...])

    x = xs_ref[rd_slot]
    z = z_ref[rd_slot]
    u = proj_ref[rd_slot, :, 0:D_POOL]
    q = proj_ref[rd_slot, :, D_POOL:D_POOL + D_QK]
    k = proj_ref[rd_slot, :, D_POOL + D_QK:D_POOL + 2 * D_QK]
    v = proj_ref[rd_slot, :, D_POOL + 2 * D_QK:D_POOL + 2 * D_QK + D_V]
    r = proj_ref[rd_slot, :, D_POOL + 2 * D_QK + D_V:D_MAIN]

    row = lax.broadcasted_iota(I32, (tb, 1), 0)

    ext_ref[POOL_ROWS:POOL_ROWS + tb, :] = u
    ext = ext_ref[...]
    pys = []
    for g, w in enumerate(POOL_WINDOWS):
        sl = slice(g * POOL_GROUP_DIM, (g + 1) * POOL_GROUP_DIM)
        acc = ext[:, sl]
        for d in range(g + 1):
            acc = acc + pltpu.roll(acc, 1 << d, axis=0)
        win = acc[POOL_ROWS:, :]
        if lead_pad:
            cnt = jnp.clip(row - lead_pad + 1, 1, w).astype(F32)
            pooled = win / cnt - u[:, sl]
        else:
            pooled = win * (1.0 / w) - u[:, sl]
        pys.append(_dot(pooled.astype(BF16), wpool_ref[g]))
    pool_y = jnp.concatenate(pys, axis=-1) * pscale_ref[...]
    ext_ref[0:POOL_ROWS, :] = ext_ref[tb:tb + POOL_ROWS, :]

    gpre = _dot(z.astype(BF16), wgu_ref[...]) + bgate_ref[...]
    log_a = jax.nn.log_sigmoid(gpre) * (1.0 / GATE_TAU)
    if lead_pad:
        log_a = jnp.where(row >= lead_pad, log_a, 0.0)
    a_hi = log_a.astype(BF16)
    a_lo = (log_a - a_hi.astype(F32)).astype(BF16)
    tril = tril_ref[...]
    bcum = _dot(tril, a_hi) + _dot(tril, a_lo)
    front_tiles(1)
    eb = jnp.exp(bcum)
    qi = q * (GLA_DK ** -0.5) * eb
    ki = k * jnp.exp(-bcum)

    rr = lax.broadcasted_iota(I32, (CHUNK, GLA_HEADS * CHUNK), 0)
    cc = lax.broadcasted_iota(I32, (CHUNK, GLA_HEADS * CHUNK), 1)
    causal = (cc % CHUNK) <= rr

    lasts = [eb[(c + 1) * CHUNK - 1:(c + 1) * CHUNK, :] for c in range(n_chunks)]
    dcol = jnp.concatenate(lasts + [jnp.zeros((LANES - n_chunks, D_QK), F32)], axis=0).T

    for c in range(n_chunks):
        rows = slice(c * CHUNK, (c + 1) * CHUNK)
        qi_c = qi[rows].astype(BF16)
        ki_c = ki[rows]
        kd_c = (ki_c * lasts[c]).astype(BF16)
        ki_cb = ki_c.astype(BF16)
        v_cb = v[rows].astype(BF16)
        for hh in range(GLA_HEADS):
            ks = slice(hh * GLA_DK, (hh + 1) * GLA_DK)
            vs = slice(hh * GLA_DV, (hh + 1) * GLA_DV)
            kbd_ref[c, hh * CHUNK:(hh + 1) * CHUNK, ks] = ki_cb[:, ks]
            vbd_ref[c, hh * CHUNK:(hh + 1) * CHUNK, vs] = v_cb[:, vs]
        scores = _dot_nt(qi_c, kbd_ref[c])
        p = jnp.where(causal, scores, 0.0).astype(BF16)
        o_ref[rows, :] = _dot(p, vbd_ref[c]) + _dot(qi_c, sbd_ref[c])
        for hh in range(GLA_HEADS):
            ks = slice(hh * GLA_DK, (hh + 1) * GLA_DK)
            vs = slice(hh * GLA_DV, (hh + 1) * GLA_DV)
            kv = _dot_tn(kd_c[:, ks], v_cb[:, vs])
            s_new = st_ref[hh] * dcol[ks, c:c + 1] + kv
            st_ref[hh] = s_new
            if c + 1 < n_chunks:
                put_state(c + 1, hh, s_new)
        if c == 0:
            front_tiles(2)

    o = o_ref[...]
    ons = []
    for hh in range(GLA_HEADS):
        oh = o[:, hh * GLA_DV:(hh + 1) * GLA_DV]
        ons.append(oh * lax.rsqrt(jnp.mean(oh * oh, axis=-1, keepdims=True) + EPS))
    og = jnp.concatenate(ons, axis=-1) * gnorm_ref[...] * (r * jax.nn.sigmoid(r))
    mix = _dot(jnp.concatenate([pool_y, og], axis=-1).astype(BF16), wout_ref[...])
    front_tiles(3)

    h = x + mix
    h_ref[...] = h
    hn = _rms(h, nffn_ref[...]).astype(BF16)
    _pack_planes(hn, hn_ref)

    logits = _dot_nt(wr_ref[...], hn) + br_ref[...]
    front_tiles(4)
    sub = lax.broadcasted_iota(I32, (EXPERTS_PER_GROUP, tb), 0).astype(F32)
    neg = jnp.float32(-jnp.inf)
    big = jnp.float32(EXPERTS_PER_GROUP)
    tile0 = logits[0:EXPERTS_PER_GROUP]
    is_g = sub < N_GROUPS
    gmax = jnp.max(jnp.where(is_g, tile0, neg), axis=0, keepdims=True)
    gsum = jnp.sum(jnp.where(is_g, jnp.exp(tile0 - gmax), 0.0), axis=0, keepdims=True)
    p_g = 1.0 / gsum
    gidx = jnp.min(jnp.where(is_g & (tile0 == gmax), sub, big), axis=0, keepdims=True)
    el = logits[N_GROUPS * EXPERTS_PER_GROUP:(N_GROUPS + 1) * EXPERTS_PER_GROUP]
    for g in range(N_GROUPS - 2, -1, -1):
        el = jnp.where(gidx == g, logits[(g + 1) * EXPERTS_PER_GROUP:(g + 2) * EXPERTS_PER_GROUP], el)
    m1 = jnp.max(el, axis=0, keepdims=True)
    i1 = jnp.min(jnp.where(el == m1, sub, big), axis=0, keepdims=True)
    rest = sub != i1
    m2 = jnp.max(jnp.where(rest, el, neg), axis=0, keepdims=True)
    i2 = jnp.min(jnp.where(rest & (el == m2), sub, big), axis=0, keepdims=True)
    t2 = jnp.exp(m2 - m1)
    den = 1.0 + t2
    g1 = p_g / den
    g2 = p_g * t2 / den
    e1 = gidx * EXPERTS_PER_GROUP + i1
    e2 = gidx * EXPERTS_PER_GROUP + i2

    eid = lax.broadcasted_iota(I32, (N_EXPERTS, tb), 0).astype(F32)
    oh1 = eid == e1
    oh2 = eid == e2
    both = jnp.where(oh1 | oh2, 1.0, 0.0)
    cnt = cnt_ref[...]
    before = _dot(both.astype(BF16), sup_ref[...]) + cnt
    front_tiles(5)
    assert tiles_done[0] * FRONT_TILE == D_MAIN
    pos1 =jnp.sum(jnp.where(oh1, before, 0.0), axis=0, keepdims=True)
    pos2 = jnp.sum(jnp.where(oh2, before, 0.0), axis=0, keepdims=True)
    cnt_new = cnt + jnp.sum(both, axis=1, keepdims=True)
    cnt_ref[...] = cnt_new
    cnt_out_ref[...] = cnt_new

    zero = jnp.zeros_like(e1)
    route_t = jnp.concatenate([e1, e2, g1, g2, pos1, pos2, zero, zero], axis=0)
    route_t_ref[0] = route_t
    route_ref[...] = jnp.concatenate([route_t, jnp.zeros((LANES - ROUTE_ROWS, tb), F32)], axis=0).T

    @pl.when(j == nj - 1)
    def _():
        pool_out_ref[0] = ext_ref[0:POOL_ROWS, :]
        st_out_ref[0] = st_ref[...]


def _mixer(x2d, pool0, st0, cnt0, weights, *, batch, seq, tb, lead_pad):
    nj = seq // tb
    n_blk = batch * nj
    total_rows = batch * seq
    shared = pool0.shape[0] == 1
    front = lambda s: jnp.minimum(s, n_blk - 1)
    back = lambda s: jnp.maximum(s - 1, 0)
    stream = lambda s: back(s) // nj
    st_idx = (lambda s: (0, 0, 0)) if shared else (lambda s: (stream(s), 0, 0))
    gla_idx = (lambda s: (0, 0, 0, 0)) if shared else (lambda s: (stream(s), 0, 0, 0))
    const2 = lambda s: (0, 0)
    tok_out = lambda s: (back(s), 0)

    ii = jnp.arange(tb)
    tril = ((ii[:, None] >= ii[None, :]) & (ii[:, None] // CHUNK == ii[None, :] // CHUNK)).astype(BF16)
    sup = (ii[:, None] < ii[None, :]).astype(BF16)

    in_specs = [
        pl.BlockSpec((tb, D_MODEL), lambda s: (front(s), 0)),
        pl.BlockSpec((1, POOL_ROWS, D_POOL), st_idx),
        pl.BlockSpec((1, GLA_HEADS, GLA_DK, GLA_DV), gla_idx),
        pl.BlockSpec((N_EXPERTS, 1), const2),
        pl.BlockSpec((tb, tb), const2),
        pl.BlockSpec((tb, tb), const2),
    ]
    for wgt in weights:
        in_specs.append(pl.BlockSpec(wgt.shape, (lambda s, n=wgt.ndim: (0,) * n)))
    args = [x2d, pool0, st0, cnt0, tril, sup, *weights]

    out_shape = [
        jax.ShapeDtypeStruct((total_rows, D_MODEL), F32),
        jax.ShapeDtypeStruct((HN_PLANES, total_rows, LANES), I32),
        jax.ShapeDtypeStruct((total_rows, LANES), F32),
        jax.ShapeDtypeStruct((batch * nj, ROUTE_ROWS, tb), F32),
        jax.ShapeDtypeStruct((batch, POOL_ROWS, D_POOL), F32),
        jax.ShapeDtypeStruct((batch, GLA_HEADS, GLA_DK, GLA_DV), F32),
        jax.ShapeDtypeStruct((N_EXPERTS, 1), F32),
    ]
    out_specs = [
        pl.BlockSpec((tb, D_MODEL), tok_out),
        pl.BlockSpec((HN_PLANES, tb, LANES), lambda s: (0, back(s), 0)),
        pl.BlockSpec((tb, LANES), tok_out),
        pl.BlockSpec((1, ROUTE_ROWS, tb), lambda s: (back(s), 0, 0)),
        pl.BlockSpec((1, POOL_ROWS, D_POOL), lambda s: (stream(s), 0, 0)),
        pl.BlockSpec((1, GLA_HEADS, GLA_DK, GLA_DV), lambda s: (stream(s), 0, 0, 0)),
        pl.BlockSpec((N_EXPERTS, 1), const2),
    ]
    scratch = [
        pltpu.VMEM((POOL_ROWS + tb, D_POOL), F32),
        pltpu.VMEM((GLA_HEADS, GLA_DK, GLA_DV), F32),
        pltpu.VMEM((tb // CHUNK, GLA_HEADS * CHUNK, D_QK), BF16),
        pltpu.VMEM((tb // CHUNK, GLA_HEADS * CHUNK, D_V), BF16),
        pltpu.VMEM((tb // CHUNK, D_QK, D_V), BF16),
        pltpu.VMEM((tb, D_V), F32),
        pltpu.VMEM((N_EXPERTS, 1), F32),
        pltpu.VMEM((2, tb, D_MODEL), F32),
        pltpu.VMEM((2, tb, D_MAIN), F32),
        pltpu.VMEM((2, tb, LANES), F32),
    ]
    return pl.pallas_call(
        functools.partial(_mixer_kernel, tb=tb, nj=nj, lead_pad=lead_pad),
        grid=(n_blk + 1,),
        in_specs=in_specs,
        out_specs=out_specs,
        out_shape=out_shape,
        scratch_shapes=scratch,
        compiler_params=pltpu.CompilerParams(
            dimension_semantics=("arbitrary",), vmem_limit_bytes=VMEM_LIMIT),
        name=f"mixer_tb{tb}_pad{lead_pad}",
    )(*args)


def _sc_mesh():
    return plsc.VectorSubcoreMesh(core_axis_name="c", subcore_axis_name="s",
                                  num_cores=SC_CORES, num_subcores=SC_SUBCORES)


def _sc_worker():
    return lax.axis_index("s") * SC_CORES + lax.axis_index("c")


def _plane_rows(dest, planes, rows_per_plane):
    offs = (jnp.arange(planes, dtype=I32) * rows_per_plane)[None, :, None]
    return dest[:, None, :] + offs


def _dispatch(hn_p, hn_s, dest0, dest1, n_rows):
    planes, t_p, _ = hn_p.shape
    t_s = hn_s.shape[1]
    n_cp = t_p // SC_ROWS
    cp = n_cp // SC_WORKERS
    n_cs = t_s // SC_ROWS
    assert t_p == cp * SC_ROWS * SC_WORKERS and t_s == n_cs * SC_ROWS and n_cs <= SC_WORKERS
    idx0 = _plane_rows(dest0, planes, n_rows)
    idx1 = _plane_rows(dest1, planes, n_rows)

    def body(hnp_hbm, hns_hbm, d0_hbm, d1_hbm, xs_hbm, rows_v, i0_v, i1_v, is0_v, is1_v, sem_in, sem_out):
        wid = _sc_worker()
        pltpu.sync_copy(d0_hbm.at[pl.ds(wid * cp, cp)], i0_v)
        pltpu.sync_copy(d1_hbm.at[pl.ds(wid * cp, cp)], i1_v)

        def move(src_hbm, src_rows, row0, i0, i1, c):
            loads = [pltpu.async_copy(src_hbm.at[pl.ds(p * src_rows + row0, SC_ROWS)], rows_v.at[p], sem_in)
                     for p in range(planes)]
            for cpy in loads:
                cpy.wait()
            stores = []
            for p in range(planes):
                stores.append(pltpu.async_copy(rows_v.at[p], xs_hbm.at[i0.at[c, p]], sem_out))
                stores.append(pltpu.async_copy(rows_v.at[p], xs_hbm.at[i1.at[c, p]], sem_out))
            for cpy in stores:
                cpy.wait()

        @pl.loop(0, cp)
        def _(c):
            move(hnp_hbm, t_p, (wid * cp + c) * SC_ROWS, i0_v, i1_v, c)

        @pl.when(wid < n_cs)
        def _():
            pltpu.sync_copy(d0_hbm.at[pl.ds(n_cp + wid, 1)], is0_v)
            pltpu.sync_copy(d1_hbm.at[pl.ds(n_cp + wid, 1)], is1_v)
            move(hns_hbm, t_s, wid * SC_ROWS, is0_v, is1_v, 0)

    xs = pl.kernel(
        body,
        out_type=jax.ShapeDtypeStruct((planes * n_rows, LANES), I32),
        mesh=_sc_mesh(),
        scratch_types=[
            pltpu.VMEM((planes, SC_ROWS, LANES), I32),
            pltpu.VMEM((cp, planes, SC_ROWS), I32),
            pltpu.VMEM((cp, planes, SC_ROWS), I32),
            pltpu.VMEM((1, planes, SC_ROWS), I32),
            pltpu.VMEM((1, planes, SC_ROWS), I32),
            pltpu.SemaphoreType.DMA,
            pltpu.SemaphoreType.DMA,
        ],
        name="moe_dispatch_sc",
    )(hn_p.reshape(planes * t_p, LANES), hn_s.reshape(planes * t_s, LANES), idx0, idx1)
    return xs.reshape(planes, n_rows, LANES)


def _gather(ys, dest0, dest1, t_p, t_s):
    planes, n_rows, _ = ys.shape
    n_cp = t_p // SC_ROWS
    cp = n_cp // SC_WORKERS
    n_cs = t_s // SC_ROWS
    stage = planes
    assert t_p == cp * SC_ROWS * SC_WORKERS and t_s == n_cs * SC_ROWS and n_cs <= SC_WORKERS
    idx = (_plane_rows(dest0, planes, n_rows), _plane_rows(dest1, planes, n_rows))

    def body(ys_hbm, d0_hbm, d1_hbm, op_hbm, os_hbm, rows_v, i0_v, i1_v, is0_v, is1_v, sem_in, sem_out):
        wid = _sc_worker()
        pltpu.sync_copy(d0_hbm.at[pl.ds(wid * cp, cp)], i0_v)
        pltpu.sync_copy(d1_hbm.at[pl.ds(wid * cp, cp)], i1_v)

        def move(out_hbm, out_rows, row0, idx_refs, c):
            for j, i_v in enumerate(idx_refs):
                for p0 in range(0, planes, stage):
                    loads = [pltpu.async_copy(ys_hbm.at[i_v.at[c, p0 + p]], rows_v.at[p], sem_in)
                             for p in range(stage)]
                    for cpy in loads:
                        cpy.wait()
                    stores = [
                        pltpu.async_copy(
                            rows_v.at[p],
                            out_hbm.at[pl.ds((j * planes + p0 + p) * out_rows + row0, SC_ROWS)], sem_out)
                        for p in range(stage)]
                    for cpy in stores:
                        cpy.wait()

        @pl.loop(0, cp)
        def _(c):
            move(op_hbm, t_p, (wid * cp + c) * SC_ROWS, (i0_v, i1_v), c)

        @pl.when(wid < n_cs)
        def _():
            pltpu.sync_copy(d0_hbm.at[pl.ds(n_cp + wid, 1)], is0_v)
            pltpu.sync_copy(d1_hbm.at[pl.ds(n_cp + wid, 1)], is1_v)
            move(os_hbm, t_s, wid * SC_ROWS, (is0_v, is1_v), 0)

    out_p, out_s = pl.kernel(
        body,
        out_type=(jax.ShapeDtypeStruct((2 * planes * t_p, LANES), ys.dtype),
                  jax.ShapeDtypeStruct((2 * planes * t_s, LANES), ys.dtype)),
        mesh=_sc_mesh(),
        scratch_types=[
            pltpu.VMEM((stage, SC_ROWS, LANES), ys.dtype),
            pltpu.VMEM((cp, planes, SC_ROWS), I32),
            pltpu.VMEM((cp, planes, SC_ROWS), I32),
            pltpu.VMEM((1, planes, SC_ROWS), I32),
            pltpu.VMEM((1, planes, SC_ROWS), I32),
            pltpu.SemaphoreType.DMA,
            pltpu.SemaphoreType.DMA,
        ],
        name="moe_gather_sc",
    )(ys.reshape(planes * n_rows, LANES), *idx)
    return out_p.reshape(2, planes, t_p, LANES), out_s.reshape(2, planes, t_s, LANES)


def _expert_kernel(be_ref, bv_ref, nb_ref, xs_ref, wg_ref, wu_ref, wd_ref, ys_ref, wgu_s, wd_s):
    i = pl.program_id(0)
    live = i < nb_ref[0]
    prev = be_ref[jnp.maximum(i - 1, 0)]
    fresh = (i == 0) | (be_ref[i] != prev)

    @pl.when(live & fresh)
    def _():
        wgu_s[:, 0:D_EXPERT] = wg_ref[0].astype(BF16)
        wgu_s[:, D_EXPERT:2 * D_EXPERT] = wu_ref[0].astype(BF16)
        wd_s[...] = wd_ref[0].astype(BF16)

    @pl.when(live)
    def _():
        valid = lax.broadcasted_iota(I32, (MOE_BLK, LANES), 0) < bv_ref[i]
        xb = _unpack_planes([jnp.where(valid, xs_ref[p], 0) for p in range(HN_PLANES)]).astype(BF16)
        gu = _dot(xb, wgu_s[...])
        gate = gu[:, 0:D_EXPERT]
        hmid = gate * jax.nn.sigmoid(gate) * gu[:, D_EXPERT:]
        _pack_planes(_dot(hmid.astype(BF16), wd_s[...]), ys_ref)

    @pl.when(jnp.logical_not(live))
    def _():
        ys_ref[...] = jnp.zeros_like(ys_ref)


def _experts(block_e, block_valid, nb, xs, w_eg, w_eu, w_ed):
    n_rows = xs.shape[1]
    n_blocks = n_rows // MOE_BLK
    row_idx = lambda i, be, bv, nb: (0, jnp.minimum(i, nb[0] - 1), 0)
    w_idx = lambda i, be, bv, nb: (be[i], 0, 0)
    grid_spec = pltpu.PrefetchScalarGridSpec(
        num_scalar_prefetch=3,
        grid=(n_blocks,),
        in_specs=[
            pl.BlockSpec((HN_PLANES, MOE_BLK, LANES), row_idx),
            pl.BlockSpec((1, D_MODEL, D_EXPERT), w_idx),
            pl.BlockSpec((1, D_MODEL, D_EXPERT), w_idx),
            pl.BlockSpec((1, D_EXPERT, D_MODEL), w_idx),
        ],
        out_specs=pl.BlockSpec((HN_PLANES, MOE_BLK, LANES), lambda i, be, bv, nb: (0, i, 0)),
        scratch_shapes=[
            pltpu.VMEM((D_MODEL, 2 * D_EXPERT), BF16),
            pltpu.VMEM((D_EXPERT, D_MODEL), BF16),
        ],
    )
    return pl.pallas_call(
        _expert_kernel,
        grid_spec=grid_spec,
        out_shape=jax.ShapeDtypeStruct((HN_PLANES, n_rows, LANES), I32),
        compiler_params=pltpu.CompilerParams(
            dimension_semantics=("arbitrary",), vmem_limit_bytes=VMEM_LIMIT),
        name="moe_experts",
    )(block_e, block_valid, nb, xs, w_eg, w_eu, w_ed)


def _combine_kernel(h_ref, route_ref, nfin_ref, rows_ref, y_ref):
    route = route_ref[...]
    ys1 = _unpack_planes([rows_ref[0, p] for p in range(HN_PLANES)])
    ys2 = _unpack_planes([rows_ref[1, p] for p in range(HN_PLANES)])
    out = h_ref[...] + (ys1 * route[:, 2:3] + ys2 * route[:, 3:4])
    y_ref[...] = _rms(out, nfin_ref[...])


def _combine(h, route, norm_final, rows):
    n_rows = h.shape[0]
    tb = COMBINE_TB
    return pl.pallas_call(
        _combine_kernel,
        grid=(n_rows // tb,),
        in_specs=[
            pl.BlockSpec((tb, D_MODEL), lambda i: (i, 0)),
            pl.BlockSpec((tb, LANES), lambda i: (i, 0)),
            pl.BlockSpec((1, D_MODEL), lambda i: (0, 0)),
            pl.BlockSpec((2, HN_PLANES, tb, LANES), lambda i: (0, 0, i, 0)),
        ],
        out_specs=pl.BlockSpec((tb, D_MODEL), lambda i: (i, 0)),
        out_shape=jax.ShapeDtypeStruct((n_rows, D_MODEL), F32),
        compiler_params=pltpu.CompilerParams(
            dimension_semantics=("arbitrary",), vmem_limit_bytes=VMEM_LIMIT),
        name="moe_combine",
    )(h, route, norm_final, rows)


def kernel(x_prompt, x_sample, state_pool, state_gla, meta_tokens, norm_mix, w_in, w_gate_up, b_gate, w_pool, pool_scale, gla_norm, w_out, norm_ffn, w_router_group, b_router_group, w_router_expert, b_router_expert, w_expert_gate, w_expert_up, w_expert_down, norm_final):
    assert w_in.shape[0] == 1, "one encoder layer"
    batch, seq, _ = x_prompt.shape
    dec_batch, dec_seq, _ = x_sample.shape
    assert seq % MIX_TB == 0 and dec_seq == CHUNK and N_META <= CHUNK
    t_prompt = batch * seq
    t_sample = dec_batch * dec_seq
    t_all = t_prompt + t_sample
    assert t_prompt % COMBINE_TB == 0 and t_sample % COMBINE_TB == 0

    w_in0 = w_in[0]
    zpad = LANES - GATE_RANK
    gpad = EXPERTS_PER_GROUP - N_GROUPS
    rpad = ROUTER_ROWS - EXPERTS_PER_GROUP - N_EXPERTS
    w_router = jnp.concatenate([
        w_router_group[0].T, jnp.zeros((gpad, D_MODEL), F32),
        w_router_expert[0].T, jnp.zeros((rpad, D_MODEL), F32)], axis=0)
    b_router = jnp.concatenate([
        b_router_group[0], jnp.zeros((gpad,), F32), b_router_expert[0], jnp.zeros((rpad,), F32)])
    weights = (
        norm_mix[0][None, :],
        w_in0[:, :D_MAIN].astype(BF16),
        jnp.pad(w_in0[:, D_MAIN:], ((0, 0), (0, zpad))).astype(BF16),
        jnp.pad(w_gate_up[0], ((0, zpad), (0, 0))).astype(BF16),
        b_gate[0][None, :],
        w_pool[0].astype(BF16),
        pool_scale[0][None, :],
        gla_norm[0][None, :],
        w_out[0].astype(BF16),
        norm_ffn[0][None, :],
        w_router.astype(BF16),
        b_router[:, None],
    )

    zero_cnt = jnp.zeros((N_EXPERTS, 1), F32)
    x_meta = jnp.pad(meta_tokens.astype(F32), ((CHUNK - N_META, 0), (0, 0)))
    meta = _mixer(x_meta, jnp.zeros((1, POOL_ROWS, D_POOL), F32),
                  jnp.zeros((1, GLA_HEADS, GLA_DK, GLA_DV), F32), zero_cnt,
                  weights, batch=1, seq=CHUNK, tb=CHUNK, lead_pad=CHUNK - N_META)
    h_m, hn_m, route_m, rt_m, pool_m, st_m, cnt_m = meta
    del h_m, hn_m, route_m, rt_m, cnt_m
    h_p, hn_p, route_p, rt_p, pool_p, st_p, cnt_p = _mixer(
        x_prompt.reshape(t_prompt, D_MODEL), pool_m, st_m, zero_cnt, weights,
        batch=batch, seq=seq, tb=MIX_TB, lead_pad=0)
    pool_s0 = jnp.pad(state_pool[0], ((0, 0), (POOL_ROWS - POOL_PAD, 0), (0, 0)))
    h_s, hn_s, route_s, rt_s, pool_s, st_s, cnt_s = _mixer(
        x_sample.reshape(t_sample, D_MODEL), pool_s0, state_gla[0].astype(F32),
        cnt_p, weights, batch=dec_batch, seq=dec_seq, tb=CHUNK, lead_pad=0)

    counts = cnt_s[:, 0].astype(I32)
    padded = (counts + MOE_BLK - 1) // MOE_BLK * MOE_BLK
    ends = jnp.cumsum(padded)
    pstart = ends - padded
    n_blocks = (2 * t_all + N_EXPERTS * (MOE_BLK - 1) + MOE_BLK - 1) // MOE_BLK
    nb = (ends[-1] // MOE_BLK).astype(I32)
    blk_ids = jnp.minimum(jnp.arange(n_blocks, dtype=I32), nb - 1)
    block_e = jnp.sum((ends[None, :] <= (blk_ids * MOE_BLK)[:, None]).astype(I32), axis=1)
    block_e = jnp.minimum(block_e, N_EXPERTS - 1)
    owner = block_e[:, None] == jnp.arange(N_EXPERTS, dtype=I32)
    row_end = jnp.sum(jnp.where(owner, pstart + counts, 0), axis=1)
    block_valid = jnp.clip(row_end - blk_ids * MOE_BLK, 0, MOE_BLK)

    def dest_rows(rt):
        rt = rt.transpose(1, 0, 2).reshape(ROUTE_ROWS, -1)
        onehot = rt[0:2].astype(I32)[..., None] == jnp.arange(N_EXPERTS, dtype=I32)
        return jnp.sum(jnp.where(onehot, pstart, 0), axis=-1) + rt[4:6].astype(I32)

    dest = jnp.concatenate([dest_rows(rt_p), dest_rows(rt_s)], axis=1)
    dest0 = dest[0].reshape(t_all // SC_ROWS, SC_ROWS)
    dest1 = dest[1].reshape(t_all // SC_ROWS, SC_ROWS)
    xs = _dispatch(hn_p, hn_s, dest0, dest1, n_blocks * MOE_BLK)
    ys = _experts(block_e, block_valid.astype(I32), nb[None], xs,
                  w_expert_gate[0], w_expert_up[0], w_expert_down[0])
    rows_p, rows_s = _gather(ys, dest0, dest1, t_prompt, t_sample)
    nfin = norm_final[None, :]
    y_prompt = _combine(h_p, route_p, nfin, rows_p).reshape(batch, seq, D_MODEL)
    y_sample = _combine(h_s, route_s, nfin, rows_s).reshape(dec_batch, dec_seq, D_MODEL)
    new_pool_prompt = pool_p[:, POOL_ROWS - POOL_PAD:][None]
    new_gla_prompt = st_p[None]
    new_pool_sample = pool_s[:, POOL_ROWS - POOL_PAD:][None]
    new_gla_sample = st_s[None]
    return (y_prompt, y_sample, new_pool_prompt, new_gla_prompt, new_pool_sample, new_gla_sample)
```

```python
import functools

import jax
import jax.numpy as jnp
from jax import lax
from jax.experimental import pallas as pl
from jax.experimental.pallas import tpu as pltpu
from jax.experimental.pallas import tpu_sc as plsc

F32 = jnp.float32
BF16 = jnp.bfloat16
U32 = jnp.uint32
I32 = jnp.int32

D_MODEL = 1024
N_META = 16
CHUNK = 64
EPS = 1e-6
D_POOL = 512
POOL_WINDOWS = (2, 4, 8, 16)
POOL_GROUP_DIM = 128
POOL_PAD = 15
POOL_ROWS = 16
GLA_HEADS = 4
GLA_DK = 64
GLA_DV = 128
D_QK = 256
D_V = 512
GATE_RANK = 16
GATE_TAU = 16.0
D_MAIN = D_POOL + 2 * D_QK + 2 * D_V
N_GROUPS = 4
EXPERTS_PER_GROUP = 8
N_EXPERTS = 32
D_EXPERT = 512

LANES = 128
HALF = D_MODEL // 2
HN_PLANES = HALF // LANES
ROUTE_ROWS = 8
ROUTER_ROWS = 64
MIX_TB = 256
FRONT_TILE = 256
FRONT_PLAN = (3, 2, 0, 1, 1, 1)
MOE_BLK = 512
COMBINE_TB = 512
COMBINE_PARTS = 4
VMEM_LIMIT = 48 * 1024 * 1024
SC_CORES = 2
SC_SUBCORES = 16
SC_WORKERS = SC_CORES * SC_SUBCORES
SC_ROWS = 128


def _rms(x, g):
    return x * lax.rsqrt(jnp.mean(x * x, axis=-1, keepdims=True) + EPS) * g


def _dot(a, b):
    return jnp.dot(a, b, preferred_element_type=F32)


def _dot_nt(a, b):
    return lax.dot_general(a, b, (((1,), (1,)), ((), ())), preferred_element_type=F32)


def _pack_planes(x, ref):
    xb = x.astype(BF16)
    lo = lax.bitcast_convert_type(xb[:, :HALF].astype(F32), U32) >> 16
    hi = lax.bitcast_convert_type(xb[:, HALF:].astype(F32), U32) & jnp.uint32(0xFFFF0000)
    packed = lax.bitcast_convert_type(lo | hi, I32)
    for p in range(HN_PLANES):
        ref[p] = packed[:, p * LANES:(p + 1) * LANES]


def _unpack_planes(planes):
    words = [lax.bitcast_convert_type(p, U32) for p in planes]
    los = [lax.bitcast_convert_type(w << 16, F32) for w in words]
    his = [lax.bitcast_convert_type(w & jnp.uint32(0xFFFF0000), F32) for w in words]
    return jnp.concatenate(los + his, axis=-1)


def _dot_tn(a, b):
    return lax.dot_general(a, b, (((0,), (0,)), ((), ())), preferred_element_type=F32)


def _mixer_kernel(x_ref, pool0_ref, st0_ref, cnt0_ref, tril_ref, sup_ref,
                  nmix_ref, wmain_ref, wz_ref, wgu_ref, bgate_ref, wpool_ref, pscale_ref,
                  gnorm_ref, wout_ref, nffn_ref, wr_ref, br_ref,
                  h_ref, hn_ref, route_ref, route_t_ref, pool_out_ref, st_out_ref, cnt_out_ref,
                  ext_ref, st_ref, kbd_ref, vbd_ref, sbd_ref, o_ref, cnt_ref,
                  xs_ref, proj_ref, z_ref, *, tb, nj, lead_pad):
    s = pl.program_id(0)
    back = jnp.maximum(s - 1, 0)
    j = lax.rem(back, nj)
    n_chunks = tb // CHUNK
    wr_slot = lax.rem(s, 2)
    rd_slot = 1 - wr_slot

    @pl.when(s == 0)
    def _():
        kbd_ref[...] = jnp.zeros_like(kbd_ref)
        vbd_ref[...] = jnp.zeros_like(vbd_ref)
        sbd_ref[...] = jnp.zeros_like(sbd_ref)
        xs_ref[1] = jnp.zeros((tb, D_MODEL), F32)
        proj_ref[1] = jnp.zeros((tb, D_MAIN), F32)
        z_ref[1] = jnp.zeros((tb, LANES), F32)

    @pl.when(s <= 1)
    def _():
        cnt_ref[...] = cnt0_ref[...]

    @pl.when(j == 0)
    def _():
        ext_ref[0:POOL_ROWS, :] = pool0_ref[0]
        st_ref[...] = st0_ref[0]

    def put_state(c, hh, st):
        sbd_ref[c, hh * GLA_DK:(hh + 1) * GLA_DK, hh * GLA_DV:(hh + 1) * GLA_DV] = st.astype(BF16)

    for hh in range(GLA_HEADS):
        put_state(0, hh, st_ref[hh])

    x_new = x_ref[...]
    xn = _rms(x_new, nmix_ref[...]).astype(BF16)
    xs_ref[wr_slot] = x_new

    tiles_done = [0]

    def front_tiles(stage):
        for t in range(tiles_done[0], tiles_done[0] + FRONT_PLAN[stage]):
            cols = slice(t * FRONT_TILE, (t + 1) * FRONT_TILE)
            proj_ref[wr_slot, :, cols] = _dot(xn, wmain_ref[:, cols])
        tiles_done[0] += FRONT_PLAN[stage]

    front_tiles(0)
    z_ref[wr_slot] = _dot(xn, wz_ref[...])

    x = xs_ref[rd_slot]
    z = z_ref[rd_slot]
    u = proj_ref[rd_slot, :, 0:D_POOL]
    q = proj_ref[rd_slot, :, D_POOL:D_POOL + D_QK]
    k = proj_ref[rd_slot, :, D_POOL + D_QK:D_POOL + 2 * D_QK]
    v = proj_ref[rd_slot, :, D_POOL + 2 * D_QK:D_POOL + 2 * D_QK + D_V]
    r = proj_ref[rd_slot, :, D_POOL + 2 * D_QK + D_V:D_MAIN]

    row = lax.broadcasted_iota(I32, (tb, 1), 0)

    ext_ref[POOL_ROWS:POOL_ROWS + tb, :] = u
    ext = ext_ref[...]
    pys = []
    for g, w in enumerate(POOL_WINDOWS):
        sl = slice(g * POOL_GROUP_DIM, (g + 1) * POOL_GROUP_DIM)
        acc = ext[:, sl]
        for d in range(g + 1):
            acc = acc + pltpu.roll(acc, 1 << d, axis=0)
        win = acc[POOL_ROWS:, :]
        if lead_pad:
            cnt = jnp.clip(row - lead_pad + 1, 1, w).astype(F32)
            pooled = win / cnt - u[:, sl]
        else:
            pooled = win * (1.0 / w) - u[:, sl]
        pys.append(_dot(pooled.astype(BF16), wpool_ref[g]))
    pool_y = jnp.concatenate(pys, axis=-1) * pscale_ref[...]
    ext_ref[0:POOL_ROWS, :] = ext_ref[tb:tb + POOL_ROWS, :]

    gpre = _dot(z.astype(BF16), wgu_ref[...]) + bgate_ref[...]
    log_a = jax.nn.log_sigmoid(gpre) * (1.0 / GATE_TAU)
    if lead_pad:
        log_a = jnp.where(row >= lead_pad, log_a, 0.0)
    a_hi = log_a.astype(BF16)
    a_lo = (log_a - a_hi.astype(F32)).astype(BF16)
    tril = tril_ref[...]
    bcum = _dot(tril, a_hi) + _dot(tril, a_lo)
    front_tiles(1)
    eb = jnp.exp(bcum)
    qi = q * (GLA_DK ** -0.5) * eb
    ki = k * jnp.exp(-bcum)

    rr = lax.broadcasted_iota(I32, (CHUNK, GLA_HEADS * CHUNK), 0)
    cc = lax.broadcasted_iota(I32, (CHUNK, GLA_HEADS * CHUNK), 1)
    causal = (cc % CHUNK) <= rr

    lasts = [eb[(c + 1) * CHUNK - 1:(c + 1) * CHUNK, :] for c in range(n_chunks)]
    dcol = jnp.concatenate(lasts + [jnp.zeros((LANES - n_chunks, D_QK), F32)], axis=0).T

    for c in range(n_chunks):
        rows = slice(c * CHUNK, (c + 1) * CHUNK)
        qi_c = qi[rows].astype(BF16)
        ki_c = ki[rows]
        kd_c = (ki_c * lasts[c]).astype(BF16)
        ki_cb = ki_c.astype(BF16)
        v_cb = v[rows].astype(BF16)
        for hh in range(GLA_HEADS):
            ks = slice(hh * GLA_DK, (hh + 1) * GLA_DK)
            vs = slice(hh * GLA_DV, (hh + 1) * GLA_DV)
            kbd_ref[c, hh * CHUNK:(hh + 1) * CHUNK, ks] = ki_cb[:, ks]
            vbd_ref[c, hh * CHUNK:(hh + 1) * CHUNK, vs] = v_cb[:, vs]
        scores = _dot_nt(qi_c, kbd_ref[c])
        p = jnp.where(causal, scores, 0.0).astype(BF16)
        o_ref[rows, :] = _dot(p, vbd_ref[c]) + _dot(qi_c, sbd_ref[c])
        for hh in range(GLA_HEADS):
            ks = slice(hh * GLA_DK, (hh + 1) * GLA_DK)
            vs = slice(hh * GLA_DV, (hh + 1) * GLA_DV)
            kv = _dot_tn(kd_c[:, ks], v_cb[:, vs])
            s_new = st_ref[hh] * dcol[ks, c:c + 1] + kv
            st_ref[hh] = s_new
            if c + 1 < n_chunks:
                put_state(c + 1, hh, s_new)
        if c == 0:
            front_tiles(2)

    o = o_ref[...]
    ons = []
    for hh in range(GLA_HEADS):
        oh = o[:, hh * GLA_DV:(hh + 1) * GLA_DV]
        ons.append(oh * lax.rsqrt(jnp.mean(oh * oh, axis=-1, keepdims=True) + EPS))
    og = jnp.concatenate(ons, axis=-1) * gnorm_ref[...] * (r * jax.nn.sigmoid(r))
    mix = _dot(jnp.concatenate([pool_y, og], axis=-1).astype(BF16), wout_ref[...])
    front_tiles(3)

    h = x + mix
    h_ref[...] = h
    hn = _rms(h, nffn_ref[...]).astype(BF16)
    _pack_planes(hn, hn_ref)

    logits = _dot_nt(wr_ref[...], hn) + br_ref[...]
    front_tiles(4)
    sub = lax.broadcasted_iota(I32, (EXPERTS_PER_GROUP, tb), 0).astype(F32)
    neg = jnp.float32(-jnp.inf)
    big = jnp.float32(EXPERTS_PER_GROUP)
    tile0 = logits[0:EXPERTS_PER_GROUP]
    is_g = sub < N_GROUPS
    gmax = jnp.max(jnp.where(is_g, tile0, neg), axis=0, keepdims=True)
    gsum = jnp.sum(jnp.where(is_g, jnp.exp(tile0 - gmax), 0.0), axis=0, keepdims=True)
    p_g = 1.0 / gsum
    gidx = jnp.min(jnp.where(is_g & (tile0 == gmax), sub, big), axis=0, keepdims=True)
    el = logits[N_GROUPS * EXPERTS_PER_GROUP:(N_GROUPS + 1) * EXPERTS_PER_GROUP]
    for g in range(N_GROUPS - 2, -1, -1):
        el = jnp.where(gidx == g, logits[(g + 1) * EXPERTS_PER_GROUP:(g + 2) * EXPERTS_PER_GROUP], el)
    m1 = jnp.max(el, axis=0, keepdims=True)
    i1 = jnp.min(jnp.where(el == m1, sub, big), axis=0, keepdims=True)
    rest = sub != i1
    m2 = jnp.max(jnp.where(rest, el, neg), axis=0, keepdims=True)
    i2 = jnp.min(jnp.where(rest & (el == m2), sub, big), axis=0, keepdims=True)
    t2 = jnp.exp(m2 - m1)
    den = 1.0 + t2
    g1 = p_g / den
    g2 = p_g * t2 / den
    e1 = gidx * EXPERTS_PER_GROUP + i1
    e2 = gidx * EXPERTS_PER_GROUP + i2

    eid = lax.broadcasted_iota(I32, (N_EXPERTS, tb), 0).astype(F32)
    oh1 = eid == e1
    oh2 = eid == e2
    both = jnp.where(oh1 | oh2, 1.0, 0.0)
    cnt = cnt_ref[...]
    before = _dot(both.astype(BF16), sup_ref[...]) + cnt
    front_tiles(5)
    assert tiles_done[0] * FRONT_TILE == D_MAIN
    pos1 =jnp.sum(jnp.where(oh1, before, 0.0), axis=0, keepdims=True)
    pos2 = jnp.sum(jnp.where(oh2, before, 0.0), axis=0, keepdims=True)
    cnt_new = cnt + jnp.sum(both, axis=1, keepdims=True)
    cnt_ref[...] = cnt_new
    cnt_out_ref[...] = cnt_new

    zero = jnp.zeros_like(e1)
    route_t = jnp.concatenate([e1, e2, g1, g2, pos1, pos2, zero, zero], axis=0)
    route_t_ref[0] = route_t
    route_ref[...] = jnp.concatenate([route_t, jnp.zeros((LANES - ROUTE_ROWS, tb), F32)], axis=0).T

    @pl.when(j == nj - 1)
    def _():
        pool_out_ref[0] = ext_ref[0:POOL_ROWS, :]
        st_out_ref[0] = st_ref[...]


def _mixer(x2d, pool0, st0, cnt0, weights, *, batch, seq, tb, lead_pad):
    nj = seq // tb
    n_blk = batch * nj
    total_rows = batch * seq
    shared = pool0.shape[0] == 1
    front = lambda s: jnp.minimum(s, n_blk - 1)
    back = lambda s: jnp.maximum(s - 1, 0)
    stream = lambda s: back(s) // nj
    st_idx = (lambda s: (0, 0, 0)) if shared else (lambda s: (stream(s), 0, 0))
    gla_idx = (lambda s: (0, 0, 0, 0)) if shared else (lambda s: (stream(s), 0, 0, 0))
    const2 = lambda s: (0, 0)
    tok_out = lambda s: (back(s), 0)

    ii = jnp.arange(tb)
    tril = ((ii[:, None] >= ii[None, :]) & (ii[:, None] // CHUNK == ii[None, :] // CHUNK)).astype(BF16)
    sup = (ii[:, None] < ii[None, :]).astype(BF16)

    in_specs = [
        pl.BlockSpec((tb, D_MODEL), lambda s: (front(s), 0)),
        pl.BlockSpec((1, POOL_ROWS, D_POOL), st_idx),
        pl.BlockSpec((1, GLA_HEADS, GLA_DK, GLA_DV), gla_idx),
        pl.BlockSpec((N_EXPERTS, 1), const2),
        pl.BlockSpec((tb, tb), const2),
        pl.BlockSpec((tb, tb), const2),
    ]
    for wgt in weights:
        in_specs.append(pl.BlockSpec(wgt.shape, (lambda s, n=wgt.ndim: (0,) * n)))
    args = [x2d, pool0, st0, cnt0, tril, sup, *weights]

    out_shape = [
        jax.ShapeDtypeStruct((total_rows, D_MODEL), F32),
        jax.ShapeDtypeStruct((HN_PLANES, total_rows, LANES), I32),
        jax.ShapeDtypeStruct((total_rows, LANES), F32),
        jax.ShapeDtypeStruct((batch * nj, ROUTE_ROWS, tb), F32),
        jax.ShapeDtypeStruct((batch, POOL_ROWS, D_POOL), F32),
        jax.ShapeDtypeStruct((batch, GLA_HEADS, GLA_DK, GLA_DV), F32),
        jax.ShapeDtypeStruct((N_EXPERTS, 1), F32),
    ]
    out_specs = [
        pl.BlockSpec((tb, D_MODEL), tok_out),
        pl.BlockSpec((HN_PLANES, tb, LANES), lambda s: (0, back(s), 0)),
        pl.BlockSpec((tb, LANES), tok_out),
        pl.BlockSpec((1, ROUTE_ROWS, tb), lambda s: (back(s), 0, 0)),
        pl.BlockSpec((1, POOL_ROWS, D_POOL), lambda s: (stream(s), 0, 0)),
        pl.BlockSpec((1, GLA_HEADS, GLA_DK, GLA_DV), lambda s: (stream(s), 0, 0, 0)),
        pl.BlockSpec((N_EXPERTS, 1), const2),
    ]
    scratch = [
        pltpu.VMEM((POOL_ROWS + tb, D_POOL), F32),
        pltpu.VMEM((GLA_HEADS, GLA_DK, GLA_DV), F32),
        pltpu.VMEM((tb // CHUNK, GLA_HEADS * CHUNK, D_QK), BF16),
        pltpu.VMEM((tb // CHUNK, GLA_HEADS * CHUNK, D_V), BF16),
        pltpu.VMEM((tb // CHUNK, D_QK, D_V), BF16),
        pltpu.VMEM((tb, D_V), F32),
        pltpu.VMEM((N_EXPERTS, 1), F32),
        pltpu.VMEM((2, tb, D_MODEL), F32),
        pltpu.VMEM((2, tb, D_MAIN), F32),
        pltpu.VMEM((2, tb, LANES), F32),
    ]
    return pl.pallas_call(
        functools.partial(_mixer_kernel, tb=tb, nj=nj, lead_pad=lead_pad),
        grid=(n_blk + 1,),
        in_specs=in_specs,
        out_specs=out_specs,
        out_shape=out_shape,
        scratch_shapes=scratch,
        compiler_params=pltpu.CompilerParams(
            dimension_semantics=("arbitrary",), vmem_limit_bytes=VMEM_LIMIT),
        name=f"mixer_tb{tb}_pad{lead_pad}",
    )(*args)


def _sc_mesh():
    return plsc.VectorSubcoreMesh(core_axis_name="c", subcore_axis_name="s",
                                  num_cores=SC_CORES, num_subcores=SC_SUBCORES)


def _sc_worker():
    return lax.axis_index("s") * SC_CORES + lax.axis_index("c")


def _plane_rows(dest, planes, rows_per_plane):
    offs = (jnp.arange(planes, dtype=I32) * rows_per_plane)[None, :, None]
    return dest[:, None, :] + offs


def _dispatch(hn_p, hn_s, dest0, dest1, n_rows):
    planes, t_p, _ = hn_p.shape
    t_s = hn_s.shape[1]
    n_cp = t_p // SC_ROWS
    cp = n_cp // SC_WORKERS
    n_cs = t_s // SC_ROWS
    assert t_p == cp * SC_ROWS * SC_WORKERS and t_s == n_cs * SC_ROWS and n_cs <= SC_WORKERS
    idx0 = _plane_rows(dest0, planes, n_rows)
    idx1 = _plane_rows(dest1, planes, n_rows)

    def body(hnp_hbm, hns_hbm, d0_hbm, d1_hbm, xs_hbm, rows_v, i0_v, i1_v, is0_v, is1_v, sem_in, sem_out):
        wid = _sc_worker()
        pltpu.sync_copy(d0_hbm.at[pl.ds(wid * cp, cp)], i0_v)
        pltpu.sync_copy(d1_hbm.at[pl.ds(wid * cp, cp)], i1_v)

        def move(src_hbm, src_rows, row0, i0, i1, c):
            loads = [pltpu.async_copy(src_hbm.at[pl.ds(p * src_rows + row0, SC_ROWS)], rows_v.at[p], sem_in)
                     for p in range(planes)]
            for cpy in loads:
                cpy.wait()
            stores = []
            for p in range(planes):
                stores.append(pltpu.async_copy(rows_v.at[p], xs_hbm.at[i0.at[c, p]], sem_out))
                stores.append(pltpu.async_copy(rows_v.at[p], xs_hbm.at[i1.at[c, p]], sem_out))
            for cpy in stores:
                cpy.wait()

        @pl.loop(0, cp)
        def _(c):
            move(hnp_hbm, t_p, (wid * cp + c) * SC_ROWS, i0_v, i1_v, c)

        @pl.when(wid < n_cs)
        def _():
            pltpu.sync_copy(d0_hbm.at[pl.ds(n_cp + wid, 1)], is0_v)
            pltpu.sync_copy(d1_hbm.at[pl.ds(n_cp + wid, 1)], is1_v)
            move(hns_hbm, t_s, wid * SC_ROWS, is0_v, is1_v, 0)

    xs = pl.kernel(
        body,
        out_type=jax.ShapeDtypeStruct((planes * n_rows, LANES), I32),
        mesh=_sc_mesh(),
        scratch_types=[
            pltpu.VMEM((planes, SC_ROWS, LANES), I32),
            pltpu.VMEM((cp, planes, SC_ROWS), I32),
            pltpu.VMEM((cp, planes, SC_ROWS), I32),
            pltpu.VMEM((1, planes, SC_ROWS), I32),
            pltpu.VMEM((1, planes, SC_ROWS), I32),
            pltpu.SemaphoreType.DMA,
            pltpu.SemaphoreType.DMA,
        ],
        name="moe_dispatch_sc",
    )(hn_p.reshape(planes * t_p, LANES), hn_s.reshape(planes * t_s, LANES), idx0, idx1)
    return xs.reshape(planes, n_rows, LANES)


def _gather(ys, dest0, dest1):
    planes, n_rows, _ = ys.shape
    n_chunks = dest0.shape[0]
    n_tok = n_chunks * SC_ROWS
    cpw = max(n_chunks // SC_WORKERS, 1)
    assert n_chunks <= SC_WORKERS or n_chunks == cpw * SC_WORKERS
    idx = (_plane_rows(dest0, planes, n_rows), _plane_rows(dest1, planes, n_rows))

    def body(ys_hbm, d0_hbm, d1_hbm, out_hbm, rows_v, i0_v, i1_v, sem_in, sem_out):
        wid = _sc_worker()

        def work():
            pltpu.sync_copy(d0_hbm.at[pl.ds(wid * cpw, cpw)], i0_v)
            pltpu.sync_copy(d1_hbm.at[pl.ds(wid * cpw, cpw)], i1_v)

            @pl.loop(0, cpw)
            def _(c):
                row0 = (wid * cpw + c) * SC_ROWS
                for j, i_v in enumerate((i0_v, i1_v)):
                    loads = [pltpu.async_copy(ys_hbm.at[i_v.at[c, p]], rows_v.at[p], sem_in)
                             for p in range(planes)]
                    for cpy in loads:
                        cpy.wait()
                    stores = [
                        pltpu.async_copy(
                            rows_v.at[p], out_hbm.at[pl.ds((j * planes + p) * n_tok + row0, SC_ROWS)], sem_out)
                        for p in range(planes)]
                    for cpy in stores:
                        cpy.wait()

        if n_chunks < SC_WORKERS:
            pl.when(wid < n_chunks)(work)
        else:
            work()

    out = pl.kernel(
        body,
        out_type=jax.ShapeDtypeStruct((2 * planes * n_tok, LANES), ys.dtype),
        mesh=_sc_mesh(),
        scratch_types=[
            pltpu.VMEM((planes, SC_ROWS, LANES), ys.dtype),
            pltpu.VMEM((cpw, planes, SC_ROWS), I32),
            pltpu.VMEM((cpw, planes, SC_ROWS), I32),
            pltpu.SemaphoreType.DMA,
            pltpu.SemaphoreType.DMA,
        ],
        name="moe_gather_sc",
    )(ys.reshape(planes * n_rows, LANES), *idx)
    return out.reshape(2, planes, n_tok, LANES)


def _expert_kernel(be_ref, bv_ref, nb_ref, xs_ref, wg_ref, wu_ref, wd_ref, ys_ref, wgu_s, wd_s):
    i = pl.program_id(0)
    live = i < nb_ref[0]
    prev = be_ref[jnp.maximum(i - 1, 0)]
    fresh = (i == 0) | (be_ref[i] != prev)

    @pl.when(live & fresh)
    def _():
        wgu_s[:, 0:D_EXPERT] = wg_ref[0].astype(BF16)
        wgu_s[:, D_EXPERT:2 * D_EXPERT] = wu_ref[0].astype(BF16)
        wd_s[...] = wd_ref[0].astype(BF16)

    @pl.when(live)
    def _():
        valid = lax.broadcasted_iota(I32, (MOE_BLK, LANES), 0) < bv_ref[i]
        xb = _unpack_planes([jnp.where(valid, xs_ref[p], 0) for p in range(HN_PLANES)]).astype(BF16)
        gu = _dot(xb, wgu_s[...])
        gate = gu[:, 0:D_EXPERT]
        hmid = gate * jax.nn.sigmoid(gate) * gu[:, D_EXPERT:]
        _pack_planes(_dot(hmid.astype(BF16), wd_s[...]), ys_ref)

    @pl.when(jnp.logical_not(live))
    def _():
        ys_ref[...] = jnp.zeros_like(ys_ref)


def _experts(block_e, block_valid, nb, xs, w_eg, w_eu, w_ed):
    n_rows = xs.shape[1]
    n_blocks = n_rows // MOE_BLK
    row_idx = lambda i, be, bv, nb: (0, jnp.minimum(i, nb[0] - 1), 0)
    w_idx = lambda i, be, bv, nb: (be[i], 0, 0)
    grid_spec = pltpu.PrefetchScalarGridSpec(
        num_scalar_prefetch=3,
        grid=(n_blocks,),
        in_specs=[
            pl.BlockSpec((HN_PLANES, MOE_BLK, LANES), row_idx),
            pl.BlockSpec((1, D_MODEL, D_EXPERT), w_idx),
            pl.BlockSpec((1, D_MODEL, D_EXPERT), w_idx),
            pl.BlockSpec((1, D_EXPERT, D_MODEL), w_idx),
        ],
        out_specs=pl.BlockSpec((HN_PLANES, MOE_BLK, LANES), lambda i, be, bv, nb: (0, i, 0)),
        scratch_shapes=[
            pltpu.VMEM((D_MODEL, 2 * D_EXPERT), BF16),
            pltpu.VMEM((D_EXPERT, D_MODEL), BF16),
        ],
    )
    return pl.pallas_call(
        _expert_kernel,
        grid_spec=grid_spec,
        out_shape=jax.ShapeDtypeStruct((HN_PLANES, n_rows, LANES), I32),
        compiler_params=pltpu.CompilerParams(
            dimension_semantics=("arbitrary",), vmem_limit_bytes=VMEM_LIMIT),
        name="moe_experts",
    )(block_e, block_valid, nb, xs, w_eg, w_eu, w_ed)


def _combine_kernel(h_ref, route_ref, nfin_ref, rows_ref, *rest):
    y_ref = rest[-1]
    route = route_ref[...]
    ys1 = _unpack_planes([rows_ref[0, p] for p in range(HN_PLANES)])
    ys2 = _unpack_planes([rows_ref[1, p] for p in range(HN_PLANES)])
    out = h_ref[...] + (ys1 * route[:, 2:3] + ys2 * route[:, 3:4])
    y_ref[...] = _rms(out, nfin_ref[...])


def _combine(h, route, norm_final, rows, *, row0=0, y_prev=None):
    total = h.shape[0]
    tb = COMBINE_TB
    blk0 = row0 // tb
    args = [h, route, norm_final, rows]
    in_specs = [
        pl.BlockSpec((tb, D_MODEL), lambda i: (blk0 + i, 0)),
        pl.BlockSpec((tb, LANES), lambda i: (blk0 + i, 0)),
        pl.BlockSpec((1, D_MODEL), lambda i: (0, 0)),
        pl.BlockSpec((2, HN_PLANES, tb, LANES), lambda i: (0, 0, i, 0)),
    ]
    aliases = {}
    if y_prev is not None:
        aliases[len(args)] = 0
        args.append(y_prev)
        in_specs.append(pl.BlockSpec(memory_space=pl.ANY))
    return pl.pallas_call(
        _combine_kernel,
        grid=(rows.shape[2] // tb,),
        in_specs=in_specs,
        out_specs=pl.BlockSpec((tb, D_MODEL), lambda i: (blk0 + i, 0)),
        out_shape=jax.ShapeDtypeStruct((total, D_MODEL), F32),
        input_output_aliases=aliases,
        compiler_params=pltpu.CompilerParams(
            dimension_semantics=("arbitrary",), vmem_limit_bytes=VMEM_LIMIT),
        name="moe_combine",
    )(*args)


def kernel(x_prompt, x_sample, state_pool, state_gla, meta_tokens, norm_mix, w_in, w_gate_up, b_gate, w_pool, pool_scale, gla_norm, w_out, norm_ffn, w_router_group, b_router_group, w_router_expert, b_router_expert, w_expert_gate, w_expert_up, w_expert_down, norm_final):
    assert w_in.shape[0] == 1, "one encoder layer"
    batch, seq, _ = x_prompt.shape
    dec_batch, dec_seq, _ = x_sample.shape
    assert seq % MIX_TB == 0 and dec_seq == CHUNK and N_META <= CHUNK
    t_prompt = batch * seq
    t_sample = dec_batch * dec_seq
    t_all = t_prompt + t_sample
    assert t_prompt % COMBINE_TB == 0 and t_sample % COMBINE_TB == 0

    w_in0 = w_in[0]
    zpad = LANES - GATE_RANK
    gpad = EXPERTS_PER_GROUP - N_GROUPS
    rpad = ROUTER_ROWS - EXPERTS_PER_GROUP - N_EXPERTS
    w_router = jnp.concatenate([
        w_router_group[0].T, jnp.zeros((gpad, D_MODEL), F32),
        w_router_expert[0].T, jnp.zeros((rpad, D_MODEL), F32)], axis=0)
    b_router = jnp.concatenate([
        b_router_group[0], jnp.zeros((gpad,), F32), b_router_expert[0], jnp.zeros((rpad,), F32)])
    weights = (
        norm_mix[0][None, :],
        w_in0[:, :D_MAIN].astype(BF16),
        jnp.pad(w_in0[:, D_MAIN:], ((0, 0), (0, zpad))).astype(BF16),
        jnp.pad(w_gate_up[0], ((0, zpad), (0, 0))).astype(BF16),
        b_gate[0][None, :],
        w_pool[0].astype(BF16),
        pool_scale[0][None, :],
        gla_norm[0][None, :],
        w_out[0].astype(BF16),
        norm_ffn[0][None, :],
        w_router.astype(BF16),
        b_router[:, None],
    )

    zero_cnt = jnp.zeros((N_EXPERTS, 1), F32)
    x_meta = jnp.pad(meta_tokens.astype(F32), ((CHUNK - N_META, 0), (0, 0)))
    meta = _mixer(x_meta, jnp.zeros((1, POOL_ROWS, D_POOL), F32),
                  jnp.zeros((1, GLA_HEADS, GLA_DK, GLA_DV), F32), zero_cnt,
                  weights, batch=1, seq=CHUNK, tb=CHUNK, lead_pad=CHUNK - N_META)
    h_m, hn_m, route_m, rt_m, pool_m, st_m, cnt_m = meta
    del h_m, hn_m, route_m, rt_m, cnt_m
    h_p, hn_p, route_p, rt_p, pool_p, st_p, cnt_p = _mixer(
        x_prompt.reshape(t_prompt, D_MODEL), pool_m, st_m, zero_cnt, weights,
        batch=batch, seq=seq, tb=MIX_TB, lead_pad=0)
    pool_s0 = jnp.pad(state_pool[0], ((0, 0), (POOL_ROWS - POOL_PAD, 0), (0, 0)))
    h_s, hn_s, route_s, rt_s, pool_s, st_s, cnt_s = _mixer(
        x_sample.reshape(t_sample, D_MODEL), pool_s0, state_gla[0].astype(F32),
        cnt_p, weights, batch=dec_batch, seq=dec_seq, tb=CHUNK, lead_pad=0)

    counts = cnt_s[:, 0].astype(I32)
    padded = (counts + MOE_BLK - 1) // MOE_BLK * MOE_BLK
    ends = jnp.cumsum(padded)
    pstart = ends - padded
    n_blocks = (2 * t_all + N_EXPERTS * (MOE_BLK - 1) + MOE_BLK - 1) // MOE_BLK
    nb = (ends[-1] // MOE_BLK).astype(I32)
    blk_ids = jnp.minimum(jnp.arange(n_blocks, dtype=I32), nb - 1)
    block_e = jnp.sum((ends[None, :] <= (blk_ids * MOE_BLK)[:, None]).astype(I32), axis=1)
    block_e = jnp.minimum(block_e, N_EXPERTS - 1)
    owner = block_e[:, None] == jnp.arange(N_EXPERTS, dtype=I32)
    row_end = jnp.sum(jnp.where(owner, pstart + counts, 0), axis=1)
    block_valid = jnp.clip(row_end - blk_ids * MOE_BLK, 0, MOE_BLK)

    def dest_rows(rt):
        rt = rt.transpose(1, 0, 2).reshape(ROUTE_ROWS, -1)
        onehot = rt[0:2].astype(I32)[..., None] == jnp.arange(N_EXPERTS, dtype=I32)
        return jnp.sum(jnp.where(onehot, pstart, 0), axis=-1) + rt[4:6].astype(I32)

    dest = jnp.concatenate([dest_rows(rt_p), dest_rows(rt_s)], axis=1)
    dest0 = dest[0].reshape(t_all // SC_ROWS, SC_ROWS)
    dest1 = dest[1].reshape(t_all // SC_ROWS, SC_ROWS)
    xs = _dispatch(hn_p, hn_s, dest0, dest1, n_blocks * MOE_BLK)
    ys = _experts(block_e, block_valid.astype(I32), nb[None], xs,
                  w_expert_gate[0], w_expert_up[0], w_expert_down[0])
    nfin = norm_final[None, :]
    cp_chunks = t_prompt // SC_ROWS
    part = cp_chunks // COMBINE_PARTS
    y_prompt = None
    for i in range(COMBINE_PARTS):
        ch = slice(i * part, (i + 1) * part)
        rows_i = _gather(ys, dest0[ch], dest1[ch])
        y_prompt = _combine(h_p, route_p, nfin, rows_i, row0=i * part * SC_ROWS, y_prev=y_prompt)
    y_prompt = y_prompt.reshape(batch, seq, D_MODEL)
    rows_s = _gather(ys, dest0[cp_chunks:], dest1[cp_chunks:])
    y_sample = _combine(h_s, route_s, nfin, rows_s).reshape(dec_batch, dec_seq, D_MODEL)
    new_pool_prompt = pool_p[:, POOL_ROWS - POOL_PAD:][None]
    new_gla_prompt = st_p[None]
    new_pool_sample = pool_s[:, POOL_ROWS - POOL_PAD:][None]
    new_gla_sample = st_s[None]
    return (y_prompt, y_sample, new_pool_prompt, new_gla_prompt, new_pool_sample, new_gla_sample)
```

```python
import functools

import jax
import jax.numpy as jnp
from jax import lax
from jax.experimental import pallas as pl
from jax.experimental.pallas import tpu as pltpu
from jax.experimental.pallas import tpu_sc as plsc

F32 = jnp.float32
BF16 = jnp.bfloat16
U32 = jnp.uint32
I32 = jnp.int32

D_MODEL = 1024
N_META = 16
CHUNK = 64
EPS = 1e-6
D_POOL = 512
POOL_WINDOWS = (2, 4, 8, 16)
POOL_GROUP_DIM = 128
POOL_PAD = 15
POOL_ROWS = 16
GLA_HEADS = 4
GLA_DK = 64
GLA_DV = 128
D_QK = 256
D_V = 512
GATE_RANK = 16
GATE_TAU = 16.0
D_MAIN = D_POOL + 2 * D_QK + 2 * D_V
N_GROUPS = 4
EXPERTS_PER_GROUP = 8
N_EXPERTS = 32
D_EXPERT = 512

LANES = 128
HALF = D_MODEL // 2
HN_PLANES = HALF // LANES
ROUTE_ROWS = 8
ROUTER_ROWS = 64
MIX_TB = 512
FRONT_TILE = 256
FRONT_PLAN = (3, 2, 0, 1, 1, 1)
MOE_BLK = 512
COMBINE_TB = 512
COMBINE_PARTS = 4
VMEM_LIMIT = 48 * 1024 * 1024
SC_CORES = 2
SC_SUBCORES = 16
SC_WORKERS = SC_CORES * SC_SUBCORES
SC_ROWS = 128


def _rms(x, g):
    return x * lax.rsqrt(jnp.mean(x * x, axis=-1, keepdims=True) + EPS) * g


def _dot(a, b):
    return jnp.dot(a, b, preferred_element_type=F32)


def _dot_nt(a, b):
    return lax.dot_general(a, b, (((1,), (1,)), ((), ())), preferred_element_type=F32)


def _pack_planes(x, ref):
    xb = x.astype(BF16)
    lo = lax.bitcast_convert_type(xb[:, :HALF].astype(F32), U32) >> 16
    hi = lax.bitcast_convert_type(xb[:, HALF:].astype(F32), U32) & jnp.uint32(0xFFFF0000)
    packed = lax.bitcast_convert_type(lo | hi, I32)
    for p in range(HN_PLANES):
        ref[p] = packed[:, p * LANES:(p + 1) * LANES]


def _unpack_planes(planes):
    words = [lax.bitcast_convert_type(p, U32) for p in planes]
    los = [lax.bitcast_convert_type(w << 16, F32) for w in words]
    his = [lax.bitcast_convert_type(w & jnp.uint32(0xFFFF0000), F32) for w in words]
    return jnp.concatenate(los + his, axis=-1)


def _dot_tn(a, b):
    return lax.dot_general(a, b, (((0,), (0,)), ((), ())), preferred_element_type=F32)


def _mixer_kernel(x_ref, pool0_ref, st0_ref, cnt0_ref, tril_ref, sup_ref,
                  nmix_ref, wmain_ref, wz_ref, wgu_ref, bgate_ref, wpool_ref, pscale_ref,
                  gnorm_ref, wout_ref, nffn_ref, wr_ref, br_ref,
                  h_ref, hn_ref, route_ref, route_t_ref, pool_out_ref, st_out_ref, cnt_out_ref,
                  ext_ref, st_ref, kbd_ref, vbd_ref, sbd_ref, o_ref, cnt_ref,
                  xs_ref, proj_ref, z_ref, *, tb, nj, lead_pad):
    s = pl.program_id(0)
    back = jnp.maximum(s - 1, 0)
    j = lax.rem(back, nj)
    n_chunks = tb // CHUNK
    wr_slot = lax.rem(s, 2)
    rd_slot = 1 - wr_slot

    @pl.when(s == 0)
    def _():
        kbd_ref[...] = jnp.zeros_like(kbd_ref)
        vbd_ref[...] = jnp.zeros_like(vbd_ref)
        sbd_ref[...] = jnp.zeros_like(sbd_ref)
        xs_ref[1] = jnp.zeros((tb, D_MODEL), F32)
        proj_ref[1] = jnp.zeros((tb, D_MAIN), F32)
        z_ref[1] = jnp.zeros((tb, LANES), F32)

    @pl.when(s <= 1)
    def _():
        cnt_ref[...] = cnt0_ref[...]

    @pl.when(j == 0)
    def _():
        ext_ref[0:POOL_ROWS, :] = pool0_ref[0]
        st_ref[...] = st0_ref[0]

    def put_state(c, hh, st):
        sbd_ref[c, hh * GLA_DK:(hh + 1) * GLA_DK, hh * GLA_DV:(hh + 1) * GLA_DV] = st.astype(BF16)

    for hh in range(GLA_HEADS):
        put_state(0, hh, st_ref[hh])

    x_new = x_ref[...]
    xn = _rms(x_new, nmix_ref[...]).astype(BF16)
    xs_ref[wr_slot] = x_new

    tiles_done = [0]

    def front_tiles(stage):
        for t in range(tiles_done[0], tiles_done[0] + FRONT_PLAN[stage]):
            cols = slice(t * FRONT_TILE, (t + 1) * FRONT_TILE)
            proj_ref[wr_slot, :, cols] = _dot(xn, wmain_ref[:, cols])
        tiles_done[0] += FRONT_PLAN[stage]

    front_tiles(0)
    z_ref[wr_slot] = _dot(xn, wz_ref[...])

    x = xs_ref[rd_slot]
    z = z_ref[rd_slot]
    u = proj_ref[rd_slot, :, 0:D_POOL]
    q = proj_ref[rd_slot, :, D_POOL:D_POOL + D_QK]
    k = proj_ref[rd_slot, :, D_POOL + D_QK:D_POOL + 2 * D_QK]
    v = proj_ref[rd_slot, :, D_POOL + 2 * D_QK:D_POOL + 2 * D_QK + D_V]
    r = proj_ref[rd_slot, :, D_POOL + 2 * D_QK + D_V:D_MAIN]

    row = lax.broadcasted_iota(I32, (tb, 1), 0)

    ext_ref[POOL_ROWS:POOL_ROWS + tb, :] = u
    ext = ext_ref[...]
    pys = []
    for g, w in enumerate(POOL_WINDOWS):
        sl = slice(g * POOL_GROUP_DIM, (g + 1) * POOL_GROUP_DIM)
        acc = ext[:, sl]
        for d in range(g + 1):
            acc = acc + pltpu.roll(acc, 1 << d, axis=0)
        win = acc[POOL_ROWS:, :]
        if lead_pad:
            cnt = jnp.clip(row - lead_pad + 1, 1, w).astype(F32)
            pooled = win / cnt - u[:, sl]
        else:
            pooled = win * (1.0 / w) - u[:, sl]
        pys.append(_dot(pooled.astype(BF16), wpool_ref[g]))
    pool_y = jnp.concatenate(pys, axis=-1) * pscale_ref[...]
    ext_ref[0:POOL_ROWS, :] = ext_ref[tb:tb + POOL_ROWS, :]

    gpre = _dot(z.astype(BF16), wgu_ref[...]) + bgate_ref[...]
    log_a = jax.nn.log_sigmoid(gpre) * (1.0 / GATE_TAU)
    if lead_pad:
        log_a = jnp.where(row >= lead_pad, log_a, 0.0)
    a_hi = log_a.astype(BF16)
    a_lo = (log_a - a_hi.astype(F32)).astype(BF16)
    tril = tril_ref[...]
    bcum = _dot(tril, a_hi) + _dot(tril, a_lo)
    front_tiles(1)
    eb = jnp.exp(bcum)
    qi = q * (GLA_DK ** -0.5) * eb
    ki = k * jnp.exp(-bcum)

    rr = lax.broadcasted_iota(I32, (CHUNK, GLA_HEADS * CHUNK), 0)
    cc = lax.broadcasted_iota(I32, (CHUNK, GLA_HEADS * CHUNK), 1)
    causal = (cc % CHUNK) <= rr

    lasts = [eb[(c + 1) * CHUNK - 1:(c + 1) * CHUNK, :] for c in range(n_chunks)]
    dcol = jnp.concatenate(lasts + [jnp.zeros((LANES - n_chunks, D_QK), F32)], axis=0).T

    for c in range(n_chunks):
        rows = slice(c * CHUNK, (c + 1) * CHUNK)
        qi_c = qi[rows].astype(BF16)
        ki_c = ki[rows]
        kd_c = (ki_c * lasts[c]).astype(BF16)
        ki_cb = ki_c.astype(BF16)
        v_cb = v[rows].astype(BF16)
        for hh in range(GLA_HEADS):
            ks = slice(hh * GLA_DK, (hh + 1) * GLA_DK)
            vs = slice(hh * GLA_DV, (hh + 1) * GLA_DV)
            kbd_ref[c, hh * CHUNK:(hh + 1) * CHUNK, ks] = ki_cb[:, ks]
            vbd_ref[c, hh * CHUNK:(hh + 1) * CHUNK, vs] = v_cb[:, vs]
        scores = _dot_nt(qi_c, kbd_ref[c])
        p = jnp.where(causal, scores, 0.0).astype(BF16)
        o_ref[rows, :] = _dot(p, vbd_ref[c]) + _dot(qi_c, sbd_ref[c])
        for hh in range(GLA_HEADS):
            ks = slice(hh * GLA_DK, (hh + 1) * GLA_DK)
            vs = slice(hh * GLA_DV, (hh + 1) * GLA_DV)
            kv = _dot_tn(kd_c[:, ks], v_cb[:, vs])
            s_new = st_ref[hh] * dcol[ks, c:c + 1] + kv
            st_ref[hh] = s_new
            if c + 1 < n_chunks:
                put_state(c + 1, hh, s_new)
        if c == 0:
            front_tiles(2)

    o = o_ref[...]
    ons = []
    for hh in range(GLA_HEADS):
        oh = o[:, hh * GLA_DV:(hh + 1) * GLA_DV]
        ons.append(oh * lax.rsqrt(jnp.mean(oh * oh, axis=-1, keepdims=True) + EPS))
    og = jnp.concatenate(ons, axis=-1) * gnorm_ref[...] * (r * jax.nn.sigmoid(r))
    mix = _dot(jnp.concatenate([pool_y, og], axis=-1).astype(BF16), wout_ref[...])
    front_tiles(3)

    h = x + mix
    h_ref[...] = h
    hn = _rms(h, nffn_ref[...]).astype(BF16)
    _pack_planes(hn, hn_ref)

    logits = _dot_nt(wr_ref[...], hn) + br_ref[...]
    front_tiles(4)
    sub = lax.broadcasted_iota(I32, (EXPERTS_PER_GROUP, tb), 0).astype(F32)
    neg = jnp.float32(-jnp.inf)
    big = jnp.float32(EXPERTS_PER_GROUP)
    tile0 = logits[0:EXPERTS_PER_GROUP]
    is_g = sub < N_GROUPS
    gmax = jnp.max(jnp.where(is_g, tile0, neg), axis=0, keepdims=True)
    gsum = jnp.sum(jnp.where(is_g, jnp.exp(tile0 - gmax), 0.0), axis=0, keepdims=True)
    p_g = 1.0 / gsum
    gidx = jnp.min(jnp.where(is_g & (tile0 == gmax), sub, big), axis=0, keepdims=True)
    el = logits[N_GROUPS * EXPERTS_PER_GROUP:(N_GROUPS + 1) * EXPERTS_PER_GROUP]
    for g in range(N_GROUPS - 2, -1, -1):
        el = jnp.where(gidx == g, logits[(g + 1) * EXPERTS_PER_GROUP:(g + 2) * EXPERTS_PER_GROUP], el)
    m1 = jnp.max(el, axis=0, keepdims=True)
    i1 = jnp.min(jnp.where(el == m1, sub, big), axis=0, keepdims=True)
    rest = sub != i1
    m2 = jnp.max(jnp.where(rest, el, neg), axis=0, keepdims=True)
    i2 = jnp.min(jnp.where(rest & (el == m2), sub, big), axis=0, keepdims=True)
    t2 = jnp.exp(m2 - m1)
    den = 1.0 + t2
    g1 = p_g / den
    g2 = p_g * t2 / den
    e1 = gidx * EXPERTS_PER_GROUP + i1
    e2 = gidx * EXPERTS_PER_GROUP + i2

    eid = lax.broadcasted_iota(I32, (N_EXPERTS, tb), 0).astype(F32)
    oh1 = eid == e1
    oh2 = eid == e2
    both = jnp.where(oh1 | oh2, 1.0, 0.0)
    cnt = cnt_ref[...]
    before = _dot(both.astype(BF16), sup_ref[...]) + cnt
    front_tiles(5)
    assert tiles_done[0] * FRONT_TILE == D_MAIN
    pos1 =jnp.sum(jnp.where(oh1, before, 0.0), axis=0, keepdims=True)
    pos2 = jnp.sum(jnp.where(oh2, before, 0.0), axis=0, keepdims=True)
    cnt_new = cnt + jnp.sum(both, axis=1, keepdims=True)
    cnt_ref[...] = cnt_new
    cnt_out_ref[...] = cnt_new

    zero = jnp.zeros_like(e1)
    route_t = jnp.concatenate([e1, e2, g1, g2, pos1, pos2, zero, zero], axis=0)
    route_t_ref[0] = route_t
    route_ref[...] = jnp.concatenate([route_t, jnp.zeros((LANES - ROUTE_ROWS, tb), F32)], axis=0).T

    @pl.when(j == nj - 1)
    def _():
        pool_out_ref[0] = ext_ref[0:POOL_ROWS, :]
        st_out_ref[0] = st_ref[...]


def _mixer(x2d, pool0, st0, cnt0, weights, *, batch, seq, tb, lead_pad):
    nj = seq // tb
    n_blk = batch * nj
    total_rows = batch * seq
    shared = pool0.shape[0] == 1
    front = lambda s: jnp.minimum(s, n_blk - 1)
    back = lambda s: jnp.maximum(s - 1, 0)
    stream = lambda s: back(s) // nj
    st_idx = (lambda s: (0, 0, 0)) if shared else (lambda s: (stream(s), 0, 0))
    gla_idx = (lambda s: (0, 0, 0, 0)) if shared else (lambda s: (stream(s), 0, 0, 0))
    const2 = lambda s: (0, 0)
    tok_out = lambda s: (back(s), 0)

    ii = jnp.arange(tb)
    tril = ((ii[:, None] >= ii[None, :]) & (ii[:, None] // CHUNK == ii[None, :] // CHUNK)).astype(BF16)
    sup = (ii[:, None] < ii[None, :]).astype(BF16)

    in_specs = [
        pl.BlockSpec((tb, D_MODEL), lambda s: (front(s), 0)),
        pl.BlockSpec((1, POOL_ROWS, D_POOL), st_idx),
        pl.BlockSpec((1, GLA_HEADS, GLA_DK, GLA_DV), gla_idx),
        pl.BlockSpec((N_EXPERTS, 1), const2),
        pl.BlockSpec((tb, tb), const2),
        pl.BlockSpec((tb, tb), const2),
    ]
    for wgt in weights:
        in_specs.append(pl.BlockSpec(wgt.shape, (lambda s, n=wgt.ndim: (0,) * n)))
    args = [x2d, pool0, st0, cnt0, tril, sup, *weights]

    out_shape = [
        jax.ShapeDtypeStruct((total_rows, D_MODEL), F32),
        jax.ShapeDtypeStruct((HN_PLANES, total_rows, LANES), I32),
        jax.ShapeDtypeStruct((total_rows, LANES), F32),
        jax.ShapeDtypeStruct((batch * nj, ROUTE_ROWS, tb), F32),
        jax.ShapeDtypeStruct((batch, POOL_ROWS, D_POOL), F32),
        jax.ShapeDtypeStruct((batch, GLA_HEADS, GLA_DK, GLA_DV), F32),
        jax.ShapeDtypeStruct((N_EXPERTS, 1), F32),
    ]
    out_specs = [
        pl.BlockSpec((tb, D_MODEL), tok_out),
        pl.BlockSpec((HN_PLANES, tb, LANES), lambda s: (0, back(s), 0)),
        pl.BlockSpec((tb, LANES), tok_out),
        pl.BlockSpec((1, ROUTE_ROWS, tb), lambda s: (back(s), 0, 0)),
        pl.BlockSpec((1, POOL_ROWS, D_POOL), lambda s: (stream(s), 0, 0)),
        pl.BlockSpec((1, GLA_HEADS, GLA_DK, GLA_DV), lambda s: (stream(s), 0, 0, 0)),
        pl.BlockSpec((N_EXPERTS, 1), const2),
    ]
    scratch = [
        pltpu.VMEM((POOL_ROWS + tb, D_POOL), F32),
        pltpu.VMEM((GLA_HEADS, GLA_DK, GLA_DV), F32),
        pltpu.VMEM((tb // CHUNK, GLA_HEADS * CHUNK, D_QK), BF16),
        pltpu.VMEM((tb // CHUNK, GLA_HEADS * CHUNK, D_V), BF16),
        pltpu.VMEM((tb // CHUNK, D_QK, D_V), BF16),
        pltpu.VMEM((tb, D_V), F32),
        pltpu.VMEM((N_EXPERTS, 1), F32),
        pltpu.VMEM((2, tb, D_MODEL), F32),
        pltpu.VMEM((2, tb, D_MAIN), F32),
        pltpu.VMEM((2, tb, LANES), F32),
    ]
    return pl.pallas_call(
        functools.partial(_mixer_kernel, tb=tb, nj=nj, lead_pad=lead_pad),
        grid=(n_blk + 1,),
        in_specs=in_specs,
        out_specs=out_specs,
        out_shape=out_shape,
        scratch_shapes=scratch,
        compiler_params=pltpu.CompilerParams(
            dimension_semantics=("arbitrary",), vmem_limit_bytes=VMEM_LIMIT),
        name=f"mixer_tb{tb}_pad{lead_pad}",
    )(*args)


def _sc_mesh():
    return plsc.VectorSubcoreMesh(core_axis_name="c", subcore_axis_name="s",
                                  num_cores=SC_CORES, num_subcores=SC_SUBCORES)


def _sc_worker():
    return lax.axis_index("s") * SC_CORES + lax.axis_index("c")


def _plane_rows(dest, planes, rows_per_plane):
    offs = (jnp.arange(planes, dtype=I32) * rows_per_plane)[None, :, None]
    return dest[:, None, :] + offs


def _dispatch(hn_p, hn_s, dest0, dest1, n_rows):
    planes, t_p, _ = hn_p.shape
    t_s = hn_s.shape[1]
    n_cp = t_p // SC_ROWS
    cp = n_cp // SC_WORKERS
    n_cs = t_s // SC_ROWS
    assert t_p == cp * SC_ROWS * SC_WORKERS and t_s == n_cs * SC_ROWS and n_cs <= SC_WORKERS
    idx0 = _plane_rows(dest0, planes, n_rows)
    idx1 = _plane_rows(dest1, planes, n_rows)

    def body(hnp_hbm, hns_hbm, d0_hbm, d1_hbm, xs_hbm, rows_v, i0_v, i1_v, is0_v, is1_v, sem_in, sem_out):
        wid = _sc_worker()
        pltpu.sync_copy(d0_hbm.at[pl.ds(wid * cp, cp)], i0_v)
        pltpu.sync_copy(d1_hbm.at[pl.ds(wid * cp, cp)], i1_v)

        def move(src_hbm, src_rows, row0, i0, i1, c):
            loads = [pltpu.async_copy(src_hbm.at[pl.ds(p * src_rows + row0, SC_ROWS)], rows_v.at[p], sem_in)
                     for p in range(planes)]
            for cpy in loads:
                cpy.wait()
            stores = []
            for p in range(planes):
                stores.append(pltpu.async_copy(rows_v.at[p], xs_hbm.at[i0.at[c, p]], sem_out))
                stores.append(pltpu.async_copy(rows_v.at[p], xs_hbm.at[i1.at[c, p]], sem_out))
            for cpy in stores:
                cpy.wait()

        @pl.loop(0, cp)
        def _(c):
            move(hnp_hbm, t_p, (wid * cp + c) * SC_ROWS, i0_v, i1_v, c)

        @pl.when(wid < n_cs)
        def _():
            pltpu.sync_copy(d0_hbm.at[pl.ds(n_cp + wid, 1)], is0_v)
            pltpu.sync_copy(d1_hbm.at[pl.ds(n_cp + wid, 1)], is1_v)
            move(hns_hbm, t_s, wid * SC_ROWS, is0_v, is1_v, 0)

    xs = pl.kernel(
        body,
        out_type=jax.ShapeDtypeStruct((planes * n_rows, LANES), I32),
        mesh=_sc_mesh(),
        scratch_types=[
            pltpu.VMEM((planes, SC_ROWS, LANES), I32),
            pltpu.VMEM((cp, planes, SC_ROWS), I32),
            pltpu.VMEM((cp, planes, SC_ROWS), I32),
            pltpu.VMEM((1, planes, SC_ROWS), I32),
            pltpu.VMEM((1, planes, SC_ROWS), I32),
            pltpu.SemaphoreType.DMA,
            pltpu.SemaphoreType.DMA,
        ],
        name="moe_dispatch_sc",
    )(hn_p.reshape(planes * t_p, LANES), hn_s.reshape(planes * t_s, LANES), idx0, idx1)
    return xs.reshape(planes, n_rows, LANES)


def _gather(ys, dest0, dest1):
    planes, n_rows, _ = ys.shape
    n_chunks = dest0.shape[0]
    n_tok = n_chunks * SC_ROWS
    cpw = max(n_chunks // SC_WORKERS, 1)
    assert n_chunks <= SC_WORKERS or n_chunks == cpw * SC_WORKERS
    idx = (_plane_rows(dest0, planes, n_rows), _plane_rows(dest1, planes, n_rows))

    def body(ys_hbm, d0_hbm, d1_hbm, out_hbm, rows_v, i0_v, i1_v, sem_in, sem_out):
        wid = _sc_worker()

        def work():
            pltpu.sync_copy(d0_hbm.at[pl.ds(wid * cpw, cpw)], i0_v)
            pltpu.sync_copy(d1_hbm.at[pl.ds(wid * cpw, cpw)], i1_v)

            @pl.loop(0, cpw)
            def _(c):
                row0 = (wid * cpw + c) * SC_ROWS
                for j, i_v in enumerate((i0_v, i1_v)):
                    loads = [pltpu.async_copy(ys_hbm.at[i_v.at[c, p]], rows_v.at[p], sem_in)
                             for p in range(planes)]
                    for cpy in loads:
                        cpy.wait()
                    stores = [
                        pltpu.async_copy(
                            rows_v.at[p], out_hbm.at[pl.ds((j * planes + p) * n_tok + row0, SC_ROWS)], sem_out)
                        for p in range(planes)]
                    for cpy in stores:
                        cpy.wait()

        if n_chunks < SC_WORKERS:
            pl.when(wid < n_chunks)(work)
        else:
            work()

    out = pl.kernel(
        body,
        out_type=jax.ShapeDtypeStruct((2 * planes * n_tok, LANES), ys.dtype),
        mesh=_sc_mesh(),
        scratch_types=[
            pltpu.VMEM((planes, SC_ROWS, LANES), ys.dtype),
            pltpu.VMEM((cpw, planes, SC_ROWS), I32),
            pltpu.VMEM((cpw, planes, SC_ROWS), I32),
            pltpu.SemaphoreType.DMA,
            pltpu.SemaphoreType.DMA,
        ],
        name="moe_gather_sc",
    )(ys.reshape(planes * n_rows, LANES), *idx)
    return out.reshape(2, planes, n_tok, LANES)


def _expert_kernel(be_ref, bv_ref, nb_ref, xs_ref, wg_ref, wu_ref, wd_ref, ys_ref, wgu_s, wd_s):
    i = pl.program_id(0)
    live = i < nb_ref[0]
    prev = be_ref[jnp.maximum(i - 1, 0)]
    fresh = (i == 0) | (be_ref[i] != prev)

    @pl.when(live & fresh)
    def _():
        wgu_s[:, 0:D_EXPERT] = wg_ref[0].astype(BF16)
        wgu_s[:, D_EXPERT:2 * D_EXPERT] = wu_ref[0].astype(BF16)
        wd_s[...] = wd_ref[0].astype(BF16)

    @pl.when(live)
    def _():
        valid = lax.broadcasted_iota(I32, (MOE_BLK, LANES), 0) < bv_ref[i]
        xb = _unpack_planes([jnp.where(valid, xs_ref[p], 0) for p in range(HN_PLANES)]).astype(BF16)
        gu = _dot(xb, wgu_s[...])
        gate = gu[:, 0:D_EXPERT]
        hmid = gate * jax.nn.sigmoid(gate) * gu[:, D_EXPERT:]
        _pack_planes(_dot(hmid.astype(BF16), wd_s[...]), ys_ref)

    @pl.when(jnp.logical_not(live))
    def _():
        ys_ref[...] = jnp.zeros_like(ys_ref)


def _experts(block_e, block_valid, nb, xs, w_eg, w_eu, w_ed):
    n_rows = xs.shape[1]
    n_blocks = n_rows // MOE_BLK
    row_idx = lambda i, be, bv, nb: (0, jnp.minimum(i, nb[0] - 1), 0)
    w_idx = lambda i, be, bv, nb: (be[i], 0, 0)
    grid_spec = pltpu.PrefetchScalarGridSpec(
        num_scalar_prefetch=3,
        grid=(n_blocks,),
        in_specs=[
            pl.BlockSpec((HN_PLANES, MOE_BLK, LANES), row_idx),
            pl.BlockSpec((1, D_MODEL, D_EXPERT), w_idx),
            pl.BlockSpec((1, D_MODEL, D_EXPERT), w_idx),
            pl.BlockSpec((1, D_EXPERT, D_MODEL), w_idx),
        ],
        out_specs=pl.BlockSpec((HN_PLANES, MOE_BLK, LANES), lambda i, be, bv, nb: (0, i, 0)),
        scratch_shapes=[
            pltpu.VMEM((D_MODEL, 2 * D_EXPERT), BF16),
            pltpu.VMEM((D_EXPERT, D_MODEL), BF16),
        ],
    )
    return pl.pallas_call(
        _expert_kernel,
        grid_spec=grid_spec,
        out_shape=jax.ShapeDtypeStruct((HN_PLANES, n_rows, LANES), I32),
        compiler_params=pltpu.CompilerParams(
            dimension_semantics=("arbitrary",), vmem_limit_bytes=VMEM_LIMIT),
        name="moe_experts",
    )(block_e, block_valid, nb, xs, w_eg, w_eu, w_ed)


def _combine_kernel(h_ref, route_ref, nfin_ref, rows_ref, *rest):
    y_ref = rest[-1]
    route = route_ref[...]
    ys1 = _unpack_planes([rows_ref[0, p] for p in range(HN_PLANES)])
    ys2 = _unpack_planes([rows_ref[1, p] for p in range(HN_PLANES)])
    out = h_ref[...] + (ys1 * route[:, 2:3] + ys2 * route[:, 3:4])
    y_ref[...] = _rms(out, nfin_ref[...])


def _combine(h, route, norm_final, rows, *, row0=0, y_prev=None):
    total = h.shape[0]
    tb = COMBINE_TB
    blk0 = row0 // tb
    args = [h, route, norm_final, rows]
    in_specs = [
        pl.BlockSpec((tb, D_MODEL), lambda i: (blk0 + i, 0)),
        pl.BlockSpec((tb, LANES), lambda i: (blk0 + i, 0)),
        pl.BlockSpec((1, D_MODEL), lambda i: (0, 0)),
        pl.BlockSpec((2, HN_PLANES, tb, LANES), lambda i: (0, 0, i, 0)),
    ]
    aliases = {}
    if y_prev is not None:
        aliases[len(args)] = 0
        args.append(y_prev)
        in_specs.append(pl.BlockSpec(memory_space=pl.ANY))
    return pl.pallas_call(
        _combine_kernel,
        grid=(rows.shape[2] // tb,),
        in_specs=in_specs,
        out_specs=pl.BlockSpec((tb, D_MODEL), lambda i: (blk0 + i, 0)),
        out_shape=jax.ShapeDtypeStruct((total, D_MODEL), F32),
        input_output_aliases=aliases,
        compiler_params=pltpu.CompilerParams(
            dimension_semantics=("arbitrary",), vmem_limit_bytes=VMEM_LIMIT),
        name="moe_combine",
    )(*args)


def kernel(x_prompt, x_sample, state_pool, state_gla, meta_tokens, norm_mix, w_in, w_gate_up, b_gate, w_pool, pool_scale, gla_norm, w_out, norm_ffn, w_router_group, b_router_group, w_router_expert, b_router_expert, w_expert_gate, w_expert_up, w_expert_down, norm_final):
    assert w_in.shape[0] == 1, "one encoder layer"
    batch, seq, _ = x_prompt.shape
    dec_batch, dec_seq, _ = x_sample.shape
    assert seq % MIX_TB == 0 and dec_seq == CHUNK and N_META <= CHUNK
    t_prompt = batch * seq
    t_sample = dec_batch * dec_seq
    t_all = t_prompt + t_sample
    assert t_prompt % COMBINE_TB == 0 and t_sample % COMBINE_TB == 0

    w_in0 = w_in[0]
    zpad = LANES - GATE_RANK
    gpad = EXPERTS_PER_GROUP - N_GROUPS
    rpad = ROUTER_ROWS - EXPERTS_PER_GROUP - N_EXPERTS
    w_router = jnp.concatenate([
        w_router_group[0].T, jnp.zeros((gpad, D_MODEL), F32),
        w_router_expert[0].T, jnp.zeros((rpad, D_MODEL), F32)], axis=0)
    b_router = jnp.concatenate([
        b_router_group[0], jnp.zeros((gpad,), F32), b_router_expert[0], jnp.zeros((rpad,), F32)])
    weights = (
        norm_mix[0][None, :],
        w_in0[:, :D_MAIN].astype(BF16),
        jnp.pad(w_in0[:, D_MAIN:], ((0, 0), (0, zpad))).astype(BF16),
        jnp.pad(w_gate_up[0], ((0, zpad), (0, 0))).astype(BF16),
        b_gate[0][None, :],
        w_pool[0].astype(BF16),
        pool_scale[0][None, :],
        gla_norm[0][None, :],
        w_out[0].astype(BF16),
        norm_ffn[0][None, :],
        w_router.astype(BF16),
        b_router[:, None],
    )

    zero_cnt = jnp.zeros((N_EXPERTS, 1), F32)
    x_meta = jnp.pad(meta_tokens.astype(F32), ((CHUNK - N_META, 0), (0, 0)))
    meta = _mixer(x_meta, jnp.zeros((1, POOL_ROWS, D_POOL), F32),
                  jnp.zeros((1, GLA_HEADS, GLA_DK, GLA_DV), F32), zero_cnt,
                  weights, batch=1, seq=CHUNK, tb=CHUNK, lead_pad=CHUNK - N_META)
    h_m, hn_m, route_m, rt_m, pool_m, st_m, cnt_m = meta
    del h_m, hn_m, route_m, rt_m, cnt_m
    h_p, hn_p, route_p, rt_p, pool_p, st_p, cnt_p = _mixer(
        x_prompt.reshape(t_prompt, D_MODEL), pool_m, st_m, zero_cnt, weights,
        batch=batch, seq=seq, tb=MIX_TB, lead_pad=0)
    pool_s0 = jnp.pad(state_pool[0], ((0, 0), (POOL_ROWS - POOL_PAD, 0), (0, 0)))
    h_s, hn_s, route_s, rt_s, pool_s, st_s, cnt_s = _mixer(
        x_sample.reshape(t_sample, D_MODEL), pool_s0, state_gla[0].astype(F32),
        cnt_p, weights, batch=dec_batch, seq=dec_seq, tb=CHUNK, lead_pad=0)

    counts = cnt_s[:, 0].astype(I32)
    padded = (counts + MOE_BLK - 1) // MOE_BLK * MOE_BLK
    ends = jnp.cumsum(padded)
    pstart = ends - padded
    n_blocks = (2 * t_all + N_EXPERTS * (MOE_BLK - 1) + MOE_BLK - 1) // MOE_BLK
    nb = (ends[-1] // MOE_BLK).astype(I32)
    blk_ids = jnp.minimum(jnp.arange(n_blocks, dtype=I32), nb - 1)
    block_e = jnp.sum((ends[None, :] <= (blk_ids * MOE_BLK)[:, None]).astype(I32), axis=1)
    block_e = jnp.minimum(block_e, N_EXPERTS - 1)
    owner = block_e[:, None] == jnp.arange(N_EXPERTS, dtype=I32)
    row_end = jnp.sum(jnp.where(owner, pstart + counts, 0), axis=1)
    block_valid = jnp.clip(row_end - blk_ids * MOE_BLK, 0, MOE_BLK)

    def dest_rows(rt):
        rt = rt.transpose(1, 0, 2).reshape(ROUTE_ROWS, -1)
        onehot = rt[0:2].astype(I32)[..., None] == jnp.arange(N_EXPERTS, dtype=I32)
        return jnp.sum(jnp.where(onehot, pstart, 0), axis=-1) + rt[4:6].astype(I32)

    dest = jnp.concatenate([dest_rows(rt_p), dest_rows(rt_s)], axis=1)
    dest0 = dest[0].reshape(t_all // SC_ROWS, SC_ROWS)
    dest1 = dest[1].reshape(t_all // SC_ROWS, SC_ROWS)
    xs = _dispatch(hn_p, hn_s, dest0, dest1, n_blocks * MOE_BLK)
    ys = _experts(block_e, block_valid.astype(I32), nb[None], xs,
                  w_expert_gate[0], w_expert_up[0], w_expert_down[0])
    nfin = norm_final[None, :]
    cp_chunks = t_prompt // SC_ROWS
    part = cp_chunks // COMBINE_PARTS
    y_prompt = None
    for i in range(COMBINE_PARTS):
        ch = slice(i * part, (i + 1) * part)
        rows_i = _gather(ys, dest0[ch], dest1[ch])
        y_prompt = _combine(h_p, route_p, nfin, rows_i, row0=i * part * SC_ROWS, y_prev=y_prompt)
    y_prompt = y_prompt.reshape(batch, seq, D_MODEL)
    rows_s = _gather(ys, dest0[cp_chunks:], dest1[cp_chunks:])
    y_sample = _combine(h_s, route_s, nfin, rows_s).reshape(dec_batch, dec_seq, D_MODEL)
    new_pool_prompt = pool_p[:, POOL_ROWS - POOL_PAD:][None]
    new_gla_prompt = st_p[None]
    new_pool_sample = pool_s[:, POOL_ROWS - POOL_PAD:][None]
    new_gla_sample = st_s[None]
    return (y_prompt, y_sample, new_pool_prompt, new_gla_prompt, new_pool_sample, new_gla_sample)
```

```python
import functools

import jax
import jax.numpy as jnp
from jax import lax
from jax.experimental import pallas as pl
from jax.experimental.pallas import tpu as pltpu
from jax.experimental.pallas import tpu_sc as plsc

F32 = jnp.float32
BF16 = jnp.bfloat16
U32 = jnp.uint32
I32 = jnp.int32

D_MODEL = 1024
N_META = 16
CHUNK = 64
EPS = 1e-6
D_POOL = 512
POOL_WINDOWS = (2, 4, 8, 16)
POOL_GROUP_DIM = 128
POOL_PAD = 15
POOL_ROWS = 16
GLA_HEADS = 4
GLA_DK = 64
GLA_DV = 128
D_QK = 256
D_V = 512
GATE_RANK = 16
GATE_TAU = 16.0
D_MAIN = D_POOL + 2 * D_QK + 2 * D_V
N_GROUPS = 4
EXPERTS_PER_GROUP = 8
N_EXPERTS = 32
D_EXPERT = 512

LANES = 128
MXU_DIM = 256
HALF = D_MODEL // 2
HN_PLANES = HALF // LANES
ROUTE_ROWS = 8
ROUTER_ROWS = 64
MIX_TB = 512
SAMPLE_TB = 256
FRONT_TILE = 256
FRONT_PLAN = (3, 2, 0, 1, 2, 0)
MOE_BLK = 512
COMBINE_TB = 512
COMBINE_PARTS = 4
VMEM_LIMIT = 48 * 1024 * 1024
SC_CORES = 2
SC_SUBCORES = 16
SC_WORKERS = SC_CORES * SC_SUBCORES
SC_ROWS = 128


def _rms(x, g):
    return x * lax.rsqrt(jnp.mean(x * x, axis=-1, keepdims=True) + EPS) * g


def _dot(a, b):
    return jnp.dot(a, b, preferred_element_type=F32)


def _dot_nt(a, b):
    return lax.dot_general(a, b, (((1,), (1,)), ((), ())), preferred_element_type=F32)


def _pack_planes(x, ref):
    xb = x.astype(BF16)
    lo = lax.bitcast_convert_type(xb[:, :HALF].astype(F32), U32) >> 16
    hi = lax.bitcast_convert_type(xb[:, HALF:].astype(F32), U32) & jnp.uint32(0xFFFF0000)
    packed = lax.bitcast_convert_type(lo | hi, I32)
    for p in range(HN_PLANES):
        ref[p] = packed[:, p * LANES:(p + 1) * LANES]


def _unpack_planes(planes):
    words = [lax.bitcast_convert_type(p, U32) for p in planes]
    los = [lax.bitcast_convert_type(w << 16, F32) for w in words]
    his = [lax.bitcast_convert_type(w & jnp.uint32(0xFFFF0000), F32) for w in words]
    return jnp.concatenate(los + his, axis=-1)


def _dot_tn(a, b):
    return lax.dot_general(a, b, (((0,), (0,)), ((), ())), preferred_element_type=F32)


def _mixer_kernel(x_ref, pool0_ref, st0_ref, cnt0_ref, tril_ref, sup_ref,
                  nmix_ref, wmain_ref, wz_ref, wgu_ref, bgate_ref, wpool_ref, pscale_ref,
                  gnorm_ref, wout_ref, nffn_ref, wr_ref, br_ref,
                  h_ref, hn_ref, route_ref, route_t_ref, pool_out_ref, st_out_ref, cnt_out_ref,
                  ext_ref, st_ref, kbd_ref, vbd_ref, sbd_ref, o_ref, cnt_ref,
                  xs_ref, proj_ref, z_ref, *, tb, nj, lead_pad, chained):
    s = pl.program_id(0)
    back = jnp.maximum(s - 1, 0)
    j = lax.rem(back, nj)
    n_chunks = tb // CHUNK
    wr_slot = lax.rem(s, 2)
    rd_slot = 1 - wr_slot

    @pl.when(s == 0)
    def _():
        kbd_ref[...] = jnp.zeros_like(kbd_ref)
        vbd_ref[...] = jnp.zeros_like(vbd_ref)
        sbd_ref[...] = jnp.zeros_like(sbd_ref)
        xs_ref[1] = jnp.zeros((tb, D_MODEL), F32)
        proj_ref[1] = jnp.zeros((tb, D_MAIN), F32)
        z_ref[1] = jnp.zeros((GATE_RANK, tb), F32)

    @pl.when(s <= 1)
    def _():
        cnt_ref[...] = cnt0_ref[...]

    def put_state(c, hh, st):
        sbd_ref[c, hh * GLA_DK:(hh + 1) * GLA_DK, hh * GLA_DV:(hh + 1) * GLA_DV] = st.astype(BF16)

    if chained:
        @pl.when(j == 0)
        def _():
            ext_ref[0:POOL_ROWS, :] = pool0_ref[0]
            st_ref[...] = st0_ref[0]

        for hh in range(GLA_HEADS):
            put_state(0, hh, st_ref[hh])
    else:
        for c in range(n_chunks):
            for hh in range(GLA_HEADS):
                put_state(c, hh, st0_ref[c, hh])

    x_new = x_ref[...]
    xn = _rms(x_new, nmix_ref[...]).astype(BF16)
    xs_ref[wr_slot] = x_new

    tiles_done = [0]

    def front_tiles(stage):
        for t in range(tiles_done[0], tiles_done[0] + FRONT_PLAN[stage]):
            cols = slice(t * FRONT_TILE, (t + 1) * FRONT_TILE)
            proj_ref[wr_slot, :, cols] = _dot(xn, wmain_ref[:, cols])
        tiles_done[0] += FRONT_PLAN[stage]

    front_tiles(0)
    z_ref[wr_slot] = _dot_nt(wz_ref[...], xn)

    x = xs_ref[rd_slot]
    z = z_ref[rd_slot]
    u = proj_ref[rd_slot, :, 0:D_POOL]
    q = proj_ref[rd_slot, :, D_POOL:D_POOL + D_QK]
    k = proj_ref[rd_slot, :, D_POOL + D_QK:D_POOL + 2 * D_QK]
    v = proj_ref[rd_slot, :, D_POOL + 2 * D_QK:D_POOL + 2 * D_QK + D_V]
    r = proj_ref[rd_slot, :, D_POOL + 2 * D_QK + D_V:D_MAIN]

    row = lax.broadcasted_iota(I32, (tb, 1), 0)

    pseg = POOL_ROWS + CHUNK
    if chained:
        ext_ref[POOL_ROWS:POOL_ROWS + tb, :] = u
        ext = ext_ref[...]
    else:
        ext = jnp.concatenate(
            [blk for c in range(n_chunks) for blk in (pool0_ref[c], u[c * CHUNK:(c + 1) * CHUNK])], axis=0)
    pooled = []
    for g, w in enumerate(POOL_WINDOWS):
        sl = slice(g * POOL_GROUP_DIM, (g + 1) * POOL_GROUP_DIM)
        acc = ext[:, sl]
        for d in range(g + 1):
            acc = acc + pltpu.roll(acc, 1 << d, axis=0)
        if chained:
            win = acc[POOL_ROWS:, :]
        else:
            win = jnp.concatenate([acc[c * pseg + POOL_ROWS:(c + 1) * pseg] for c in range(n_chunks)], axis=0)
        if lead_pad:
            cnt = jnp.clip(row - lead_pad + 1, 1, w).astype(F32)
            pooled.append(win / cnt - u[:, sl])
        else:
            pooled.append(win * (1.0 / w) - u[:, sl])
    pys = [_dot(jnp.concatenate(pooled[2 * i:2 * i + 2], axis=-1).astype(BF16), wpool_ref[i])
           for i in range(len(POOL_WINDOWS) // 2)]
    pool_y = jnp.concatenate(pys, axis=-1) * pscale_ref[...]
    if chained:
        ext_ref[0:POOL_ROWS, :] = ext_ref[tb:tb + POOL_ROWS, :]
    else:
        for c in range(n_chunks):
            pool_out_ref[c] = u[(c + 1) * CHUNK - POOL_ROWS:(c + 1) * CHUNK]

    gpre = _dot_tn(z.astype(BF16), wgu_ref[...]) + bgate_ref[...]
    log_a = jax.nn.log_sigmoid(gpre) * (1.0 / GATE_TAU)
    if lead_pad:
        log_a = jnp.where(row >= lead_pad, log_a, 0.0)
    a_hi = log_a.astype(BF16)
    a_lo = (log_a - a_hi.astype(F32)).astype(BF16)
    tril = tril_ref[...]
    seg = tril.shape[0]
    bcum = jnp.concatenate(
        [_dot(tril, a_hi[r0:r0 + seg]) + _dot(tril, a_lo[r0:r0 + seg]) for r0 in range(0, tb, seg)], axis=0)
    front_tiles(1)
    eb = jnp.exp(bcum)
    qi = q * (GLA_DK ** -0.5) * eb
    ki = k * jnp.exp(-bcum)

    rr = lax.broadcasted_iota(I32, (CHUNK, GLA_HEADS * CHUNK), 0)
    cc = lax.broadcasted_iota(I32, (CHUNK, GLA_HEADS * CHUNK), 1)
    causal = (cc % CHUNK) <= rr

    lasts = [eb[(c + 1) * CHUNK - 1:(c + 1) * CHUNK, :] for c in range(n_chunks)]
    dcol = jnp.concatenate(lasts + [jnp.zeros((LANES - n_chunks, D_QK), F32)], axis=0).T

    for c in range(n_chunks):
        rows = slice(c * CHUNK, (c + 1) * CHUNK)
        qi_c = qi[rows].astype(BF16)
        ki_c = ki[rows]
        kd_c = (ki_c * lasts[c]).astype(BF16)
        ki_cb = ki_c.astype(BF16)
        v_cb = v[rows].astype(BF16)
        for hh in range(GLA_HEADS):
            ks = slice(hh * GLA_DK, (hh + 1) * GLA_DK)
            vs = slice(hh * GLA_DV, (hh + 1) * GLA_DV)
            kbd_ref[c, hh * CHUNK:(hh + 1) * CHUNK, ks] = ki_cb[:, ks]
            vbd_ref[c, hh * CHUNK:(hh + 1) * CHUNK, vs] = v_cb[:, vs]
        scores = _dot_nt(qi_c, kbd_ref[c])
        p = jnp.where(causal, scores, 0.0).astype(BF16)
        o_ref[rows, :] = _dot(p, vbd_ref[c]) + _dot(qi_c, sbd_ref[c])
        for hh in range(GLA_HEADS):
            ks = slice(hh * GLA_DK, (hh + 1) * GLA_DK)
            vs = slice(hh * GLA_DV, (hh + 1) * GLA_DV)
            kv = _dot_tn(kd_c[:, ks], v_cb[:, vs])
            s_old = st_ref[hh] if chained else st0_ref[c, hh]
            s_new = s_old * dcol[ks, c:c + 1] + kv
            if not chained:
                st_out_ref[c, hh] = s_new
            else:
                st_ref[hh] = s_new
                if c + 1 < n_chunks:
                    put_state(c + 1, hh, s_new)
        if c == 0:
            front_tiles(2)

    o = o_ref[...]
    ons = []
    for hh in range(GLA_HEADS):
        oh = o[:, hh * GLA_DV:(hh + 1) * GLA_DV]
        ons.append(oh * lax.rsqrt(jnp.mean(oh * oh, axis=-1, keepdims=True) + EPS))
    og = jnp.concatenate(ons, axis=-1) * gnorm_ref[...] * (r * jax.nn.sigmoid(r))
    mix = _dot(jnp.concatenate([pool_y, og], axis=-1).astype(BF16), wout_ref[...])
    front_tiles(3)

    h = x + mix
    h_ref[...] = h
    hn = _rms(h, nffn_ref[...]).astype(BF16)
    _pack_planes(hn, hn_ref)

    logits = _dot_nt(wr_ref[...], hn) + br_ref[...]
    front_tiles(4)
    sub = lax.broadcasted_iota(I32, (EXPERTS_PER_GROUP, tb), 0).astype(F32)
    neg = jnp.float32(-jnp.inf)
    big = jnp.float32(EXPERTS_PER_GROUP)
    tile0 = logits[0:EXPERTS_PER_GROUP]
    is_g = sub < N_GROUPS
    gmax = jnp.max(jnp.where(is_g, tile0, neg), axis=0, keepdims=True)
    gsum = jnp.sum(jnp.where(is_g, jnp.exp(tile0 - gmax), 0.0), axis=0, keepdims=True)
    p_g = 1.0 / gsum
    gidx = jnp.min(jnp.where(is_g & (tile0 == gmax), sub, big), axis=0, keepdims=True)
    el = logits[N_GROUPS * EXPERTS_PER_GROUP:(N_GROUPS + 1) * EXPERTS_PER_GROUP]
    for g in range(N_GROUPS - 2, -1, -1):
        el = jnp.where(gidx == g, logits[(g + 1) * EXPERTS_PER_GROUP:(g + 2) * EXPERTS_PER_GROUP], el)
    m1 = jnp.max(el, axis=0, keepdims=True)
    i1 = jnp.min(jnp.where(el == m1, sub, big), axis=0, keepdims=True)
    rest = sub != i1
    m2 = jnp.max(jnp.where(rest, el, neg), axis=0, keepdims=True)
    i2 = jnp.min(jnp.where(rest & (el == m2), sub, big), axis=0, keepdims=True)
    t2 = jnp.exp(m2 - m1)
    den = 1.0 + t2
    g1 = p_g / den
    g2 = p_g * t2 / den
    e1 = gidx * EXPERTS_PER_GROUP + i1
    e2 = gidx * EXPERTS_PER_GROUP + i2

    eid = lax.broadcasted_iota(I32, (N_EXPERTS, tb), 0).astype(F32)
    oh1 = eid == e1
    oh2 = eid == e2
    both = jnp.where(oh1 | oh2, 1.0, 0.0)
    cnt = cnt_ref[...]
    before = _dot(both.astype(BF16), sup_ref[...]) + cnt
    front_tiles(5)
    assert tiles_done[0] * FRONT_TILE == D_MAIN
    pos1 =jnp.sum(jnp.where(oh1, before, 0.0), axis=0, keepdims=True)
    pos2 = jnp.sum(jnp.where(oh2, before, 0.0), axis=0, keepdims=True)
    cnt_new = cnt + jnp.sum(both, axis=1, keepdims=True)
    cnt_ref[...] = cnt_new
    cnt_out_ref[...] = cnt_new

    zero = jnp.zeros_like(e1)
    route_t = jnp.concatenate([e1, e2, g1, g2, pos1, pos2, zero, zero], axis=0)
    route_t_ref[0] = route_t
    route_ref[...] = jnp.concatenate([route_t, jnp.zeros((LANES - ROUTE_ROWS, tb), F32)], axis=0).T

    if chained:
        @pl.when(j == nj - 1)
        def _():
            pool_out_ref[0] = ext_ref[0:POOL_ROWS, :]
            st_out_ref[0] = st_ref[...]


def _mixer(x2d, pool0, st0, cnt0, weights, *, batch, seq, tb, lead_pad):
    chained = tb <= seq
    total_rows = batch * seq
    n_blk = total_rows // tb
    nj = seq // tb if chained else 1
    per_blk = 1 if chained else tb // seq
    assert chained or (seq == CHUNK and batch % per_blk == 0)
    shared = pool0.shape[0] == 1
    front = lambda s: jnp.minimum(s, n_blk - 1)
    back = lambda s: jnp.maximum(s - 1, 0)
    stream = lambda s: back(s) // nj
    st_idx = (lambda s: (0, 0, 0)) if shared else (lambda s: (stream(s), 0, 0))
    gla_idx = (lambda s: (0, 0, 0, 0)) if shared else (lambda s: (stream(s), 0, 0, 0))
    const2 = lambda s: (0, 0)
    tok_out = lambda s: (back(s), 0)

    seg = min(tb, MXU_DIM)
    ii = jnp.arange(seg)
    tril = ((ii[:, None] >= ii[None, :]) & (ii[:, None] // CHUNK == ii[None, :] // CHUNK)).astype(BF16)
    ii = jnp.arange(tb)
    sup = (ii[:, None] < ii[None, :]).astype(BF16)

    in_specs = [
        pl.BlockSpec((tb, D_MODEL), lambda s: (front(s), 0)),
        pl.BlockSpec((per_blk, POOL_ROWS, D_POOL), st_idx),
        pl.BlockSpec((per_blk, GLA_HEADS, GLA_DK, GLA_DV), gla_idx),
        pl.BlockSpec((N_EXPERTS, 1), const2),
        pl.BlockSpec((seg, seg), const2),
        pl.BlockSpec((tb, tb), const2),
    ]
    for wgt in weights:
        in_specs.append(pl.BlockSpec(wgt.shape, (lambda s, n=wgt.ndim: (0,) * n)))
    args = [x2d, pool0, st0, cnt0, tril, sup, *weights]

    out_shape = [
        jax.ShapeDtypeStruct((total_rows, D_MODEL), F32),
        jax.ShapeDtypeStruct((HN_PLANES, total_rows, LANES), I32),
        jax.ShapeDtypeStruct((total_rows, LANES), F32),
        jax.ShapeDtypeStruct((n_blk, ROUTE_ROWS, tb), F32),
        jax.ShapeDtypeStruct((batch, POOL_ROWS, D_POOL), F32),
        jax.ShapeDtypeStruct((batch, GLA_HEADS, GLA_DK, GLA_DV), F32),
        jax.ShapeDtypeStruct((N_EXPERTS, 1), F32),
    ]
    out_specs = [
        pl.BlockSpec((tb, D_MODEL), tok_out),
        pl.BlockSpec((HN_PLANES, tb, LANES), lambda s: (0, back(s), 0)),
        pl.BlockSpec((tb, LANES), tok_out),
        pl.BlockSpec((1, ROUTE_ROWS, tb), lambda s: (back(s), 0, 0)),
        pl.BlockSpec((per_blk, POOL_ROWS, D_POOL), lambda s: (stream(s), 0, 0)),
        pl.BlockSpec((per_blk, GLA_HEADS, GLA_DK, GLA_DV), lambda s: (stream(s), 0, 0, 0)),
        pl.BlockSpec((N_EXPERTS, 1), const2),
    ]
    scratch = [
        pltpu.VMEM((POOL_ROWS + tb, D_POOL), F32),
        pltpu.VMEM((GLA_HEADS, GLA_DK, GLA_DV), F32),
        pltpu.VMEM((tb // CHUNK, GLA_HEADS * CHUNK, D_QK), BF16),
        pltpu.VMEM((tb // CHUNK, GLA_HEADS * CHUNK, D_V), BF16),
        pltpu.VMEM((tb // CHUNK, D_QK, D_V), BF16),
        pltpu.VMEM((tb, D_V), F32),
        pltpu.VMEM((N_EXPERTS, 1), F32),
        pltpu.VMEM((2, tb, D_MODEL), F32),
        pltpu.VMEM((2, tb, D_MAIN), F32),
        pltpu.VMEM((2, GATE_RANK, tb), F32),
    ]
    return pl.pallas_call(
        functools.partial(_mixer_kernel, tb=tb, nj=nj, lead_pad=lead_pad, chained=chained),
        grid=(n_blk + 1,),
        in_specs=in_specs,
        out_specs=out_specs,
        out_shape=out_shape,
        scratch_shapes=scratch,
        compiler_params=pltpu.CompilerParams(
            dimension_semantics=("arbitrary",), vmem_limit_bytes=VMEM_LIMIT),
        name=f"mixer_tb{tb}_pad{lead_pad}",
    )(*args)


def _sc_mesh():
    return plsc.VectorSubcoreMesh(core_axis_name="c", subcore_axis_name="s",
                                  num_cores=SC_CORES, num_subcores=SC_SUBCORES)


def _sc_worker():
    return lax.axis_index("s") * SC_CORES + lax.axis_index("c")


def _plane_rows(dest, planes, rows_per_plane):
    offs = (jnp.arange(planes, dtype=I32) * rows_per_plane)[None, :, None]
    return dest[:, None, :] + offs


def _dispatch(hn_p, hn_s, dest0, dest1, n_rows):
    planes, t_p, _ = hn_p.shape
    t_s = hn_s.shape[1]
    n_cp = t_p // SC_ROWS
    cp = n_cp // SC_WORKERS
    n_cs = t_s // SC_ROWS
    assert t_p == cp * SC_ROWS * SC_WORKERS and t_s == n_cs * SC_ROWS and n_cs <= SC_WORKERS
    idx0 = _plane_rows(dest0, planes, n_rows)
    idx1 = _plane_rows(dest1, planes, n_rows)

    def body(hnp_hbm, hns_hbm, d0_hbm, d1_hbm, xs_hbm, rows_v, i0_v, i1_v, is0_v, is1_v, sem_in, sem_out):
        wid = _sc_worker()
        pltpu.sync_copy(d0_hbm.at[pl.ds(wid * cp, cp)], i0_v)
        pltpu.sync_copy(d1_hbm.at[pl.ds(wid * cp, cp)], i1_v)

        def move(src_hbm, src_rows, row0, i0, i1, c):
            loads = [pltpu.async_copy(src_hbm.at[pl.ds(p * src_rows + row0, SC_ROWS)], rows_v.at[p], sem_in)
                     for p in range(planes)]
            for cpy in loads:
                cpy.wait()
            stores = []
            for p in range(planes):
                stores.append(pltpu.async_copy(rows_v.at[p], xs_hbm.at[i0.at[c, p]], sem_out))
                stores.append(pltpu.async_copy(rows_v.at[p], xs_hbm.at[i1.at[c, p]], sem_out))
            for cpy in stores:
                cpy.wait()

        @pl.loop(0, cp)
        def _(c):
            move(hnp_hbm, t_p, (wid * cp + c) * SC_ROWS, i0_v, i1_v, c)

        @pl.when(wid < n_cs)
        def _():
            pltpu.sync_copy(d0_hbm.at[pl.ds(n_cp + wid, 1)], is0_v)
            pltpu.sync_copy(d1_hbm.at[pl.ds(n_cp + wid, 1)], is1_v)
            move(hns_hbm, t_s, wid * SC_ROWS, is0_v, is1_v, 0)

    xs = pl.kernel(
        body,
        out_type=jax.ShapeDtypeStruct((planes * n_rows, LANES), I32),
        mesh=_sc_mesh(),
        scratch_types=[
            pltpu.VMEM((planes, SC_ROWS, LANES), I32),
            pltpu.VMEM((cp, planes, SC_ROWS), I32),
            pltpu.VMEM((cp, planes, SC_ROWS), I32),
            pltpu.VMEM((1, planes, SC_ROWS), I32),
            pltpu.VMEM((1, planes, SC_ROWS), I32),
            pltpu.SemaphoreType.DMA,
            pltpu.SemaphoreType.DMA,
        ],
        name="moe_dispatch_sc",
    )(hn_p.reshape(planes * t_p, LANES), hn_s.reshape(planes * t_s, LANES), idx0, idx1)
    return xs.reshape(planes, n_rows, LANES)


def _gather(ys, dest0, dest1):
    planes, n_rows, _ = ys.shape
    n_chunks = dest0.shape[0]
    n_tok = n_chunks * SC_ROWS
    cpw = max(n_chunks // SC_WORKERS, 1)
    assert n_chunks <= SC_WORKERS or n_chunks == cpw * SC_WORKERS
    idx = (_plane_rows(dest0, planes, n_rows), _plane_rows(dest1, planes, n_rows))

    def body(ys_hbm, d0_hbm, d1_hbm, out_hbm, rows_v, i0_v, i1_v, sem_in, sem_out):
        wid = _sc_worker()

        def work():
            pltpu.sync_copy(d0_hbm.at[pl.ds(wid * cpw, cpw)], i0_v)
            pltpu.sync_copy(d1_hbm.at[pl.ds(wid * cpw, cpw)], i1_v)

            @pl.loop(0, cpw)
            def _(c):
                row0 = (wid * cpw + c) * SC_ROWS
                for j, i_v in enumerate((i0_v, i1_v)):
                    loads = [pltpu.async_copy(ys_hbm.at[i_v.at[c, p]], rows_v.at[p], sem_in)
                             for p in range(planes)]
                    for cpy in loads:
                        cpy.wait()
                    stores = [
                        pltpu.async_copy(
                            rows_v.at[p], out_hbm.at[pl.ds((j * planes + p) * n_tok + row0, SC_ROWS)], sem_out)
                        for p in range(planes)]
                    for cpy in stores:
                        cpy.wait()

        if n_chunks < SC_WORKERS:
            pl.when(wid < n_chunks)(work)
        else:
            work()

    out = pl.kernel(
        body,
        out_type=jax.ShapeDtypeStruct((2 * planes * n_tok, LANES), ys.dtype),
        mesh=_sc_mesh(),
        scratch_types=[
            pltpu.VMEM((planes, SC_ROWS, LANES), ys.dtype),
            pltpu.VMEM((cpw, planes, SC_ROWS), I32),
            pltpu.VMEM((cpw, planes, SC_ROWS), I32),
            pltpu.SemaphoreType.DMA,
            pltpu.SemaphoreType.DMA,
        ],
        name="moe_gather_sc",
    )(ys.reshape(planes * n_rows, LANES), *idx)
    return out.reshape(2, planes, n_tok, LANES)


def _expert_kernel(be_ref, bv_ref, nb_ref, xs_ref, wg_ref, wu_ref, wd_ref, ys_ref, wgu_s, wd_s):
    i = pl.program_id(0)
    live = i < nb_ref[0]
    prev = be_ref[jnp.maximum(i - 1, 0)]
    fresh = (i == 0) | (be_ref[i] != prev)

    @pl.when(live & fresh)
    def _():
        wgu_s[:, 0:D_EXPERT] = wg_ref[0].astype(BF16)
        wgu_s[:, D_EXPERT:2 * D_EXPERT] = wu_ref[0].astype(BF16)
        wd_s[...] = wd_ref[0].astype(BF16)

    @pl.when(live)
    def _():
        valid = lax.broadcasted_iota(I32, (MOE_BLK, LANES), 0) < bv_ref[i]
        xb = _unpack_planes([jnp.where(valid, xs_ref[p], 0) for p in range(HN_PLANES)]).astype(BF16)
        gu = _dot(xb, wgu_s[...])
        gate = gu[:, 0:D_EXPERT]
        hmid = gate * jax.nn.sigmoid(gate) * gu[:, D_EXPERT:]
        _pack_planes(_dot(hmid.astype(BF16), wd_s[...]), ys_ref)

    @pl.when(jnp.logical_not(live))
    def _():
        ys_ref[...] = jnp.zeros_like(ys_ref)


def _experts(block_e, block_valid, nb, xs, w_eg, w_eu, w_ed):
    n_rows = xs.shape[1]
    n_blocks = n_rows // MOE_BLK
    row_idx = lambda i, be, bv, nb: (0, jnp.minimum(i, nb[0] - 1), 0)
    w_idx = lambda i, be, bv, nb: (be[i], 0, 0)
    grid_spec = pltpu.PrefetchScalarGridSpec(
        num_scalar_prefetch=3,
        grid=(n_blocks,),
        in_specs=[
            pl.BlockSpec((HN_PLANES, MOE_BLK, LANES), row_idx),
            pl.BlockSpec((1, D_MODEL, D_EXPERT), w_idx),
            pl.BlockSpec((1, D_MODEL, D_EXPERT), w_idx),
            pl.BlockSpec((1, D_EXPERT, D_MODEL), w_idx),
        ],
        out_specs=pl.BlockSpec((HN_PLANES, MOE_BLK, LANES), lambda i, be, bv, nb: (0, i, 0)),
        scratch_shapes=[
            pltpu.VMEM((D_MODEL, 2 * D_EXPERT), BF16),
            pltpu.VMEM((D_EXPERT, D_MODEL), BF16),
        ],
    )
    return pl.pallas_call(
        _expert_kernel,
        grid_spec=grid_spec,
        out_shape=jax.ShapeDtypeStruct((HN_PLANES, n_rows, LANES), I32),
        compiler_params=pltpu.CompilerParams(
            dimension_semantics=("arbitrary",), vmem_limit_bytes=VMEM_LIMIT),
        name="moe_experts",
    )(block_e, block_valid, nb, xs, w_eg, w_eu, w_ed)


def _combine_kernel(h_ref, route_ref, nfin_ref, rows_ref, *rest):
    y_ref = rest[-1]
    route = route_ref[...]
    ys1 = _unpack_planes([rows_ref[0, p] for p in range(HN_PLANES)])
    ys2 = _unpack_planes([rows_ref[1, p] for p in range(HN_PLANES)])
    out = h_ref[...] + (ys1 * route[:, 2:3] + ys2 * route[:, 3:4])
    y_ref[...] = _rms(out, nfin_ref[...])


def _combine(h, route, norm_final, rows, *, row0=0, y_prev=None):
    total = h.shape[0]
    tb = COMBINE_TB
    blk0 = row0 // tb
    args = [h, route, norm_final, rows]
    in_specs = [
        pl.BlockSpec((tb, D_MODEL), lambda i: (blk0 + i, 0)),
        pl.BlockSpec((tb, LANES), lambda i: (blk0 + i, 0)),
        pl.BlockSpec((1, D_MODEL), lambda i: (0, 0)),
        pl.BlockSpec((2, HN_PLANES, tb, LANES), lambda i: (0, 0, i, 0)),
    ]
    aliases = {}
    if y_prev is not None:
        aliases[len(args)] = 0
        args.append(y_prev)
        in_specs.append(pl.BlockSpec(memory_space=pl.ANY))
    return pl.pallas_call(
        _combine_kernel,
        grid=(rows.shape[2] // tb,),
        in_specs=in_specs,
        out_specs=pl.BlockSpec((tb, D_MODEL), lambda i: (blk0 + i, 0)),
        out_shape=jax.ShapeDtypeStruct((total, D_MODEL), F32),
        input_output_aliases=aliases,
        compiler_params=pltpu.CompilerParams(
            dimension_semantics=("arbitrary",), vmem_limit_bytes=VMEM_LIMIT),
        name="moe_combine",
    )(*args)


def kernel(x_prompt, x_sample, state_pool, state_gla, meta_tokens, norm_mix, w_in, w_gate_up, b_gate, w_pool, pool_scale, gla_norm, w_out, norm_ffn, w_router_group, b_router_group, w_router_expert, b_router_expert, w_expert_gate, w_expert_up, w_expert_down, norm_final):
    assert w_in.shape[0] == 1, "one encoder layer"
    batch, seq, _ = x_prompt.shape
    dec_batch, dec_seq, _ = x_sample.shape
    assert seq % MIX_TB == 0 and dec_seq == CHUNK and N_META <= CHUNK
    t_prompt = batch * seq
    t_sample = dec_batch * dec_seq
    t_all = t_prompt + t_sample
    assert t_prompt % COMBINE_TB == 0 and t_sample % COMBINE_TB == 0

    w_in0 = w_in[0]
    gpad = EXPERTS_PER_GROUP - N_GROUPS
    rpad = ROUTER_ROWS - EXPERTS_PER_GROUP - N_EXPERTS
    w_router = jnp.concatenate([
        w_router_group[0].T, jnp.zeros((gpad, D_MODEL), F32),
        w_router_expert[0].T, jnp.zeros((rpad, D_MODEL), F32)], axis=0)
    b_router = jnp.concatenate([
        b_router_group[0], jnp.zeros((gpad,), F32), b_router_expert[0], jnp.zeros((rpad,), F32)])
    zg = jnp.zeros((POOL_GROUP_DIM, POOL_GROUP_DIM), F32)
    w_pool_pairs = jnp.stack([
        jnp.block([[w_pool[0, 2 * i], zg], [zg, w_pool[0, 2 * i + 1]]]) for i in range(len(POOL_WINDOWS) // 2)])
    weights = (
        norm_mix[0][None, :],
        w_in0[:, :D_MAIN].astype(BF16),
        w_in0[:, D_MAIN:].T.astype(BF16),
        w_gate_up[0].astype(BF16),
        b_gate[0][None, :],
        w_pool_pairs.astype(BF16),
        pool_scale[0][None, :],
        gla_norm[0][None, :],
        w_out[0].astype(BF16),
        norm_ffn[0][None, :],
        w_router.astype(BF16),
        b_router[:, None],
    )

    zero_cnt = jnp.zeros((N_EXPERTS, 1), F32)
    x_meta = jnp.pad(meta_tokens.astype(F32), ((CHUNK - N_META, 0), (0, 0)))
    meta = _mixer(x_meta, jnp.zeros((1, POOL_ROWS, D_POOL), F32),
                  jnp.zeros((1, GLA_HEADS, GLA_DK, GLA_DV), F32), zero_cnt,
                  weights, batch=1, seq=CHUNK, tb=CHUNK, lead_pad=CHUNK - N_META)
    h_m, hn_m, route_m, rt_m, pool_m, st_m, cnt_m = meta
    del h_m, hn_m, route_m, rt_m, cnt_m
    h_p, hn_p, route_p, rt_p, pool_p, st_p, cnt_p = _mixer(
        x_prompt.reshape(t_prompt, D_MODEL), pool_m, st_m, zero_cnt, weights,
        batch=batch, seq=seq, tb=MIX_TB, lead_pad=0)
    pool_s0 = jnp.pad(state_pool[0], ((0, 0), (POOL_ROWS - POOL_PAD, 0), (0, 0)))
    h_s, hn_s, route_s, rt_s, pool_s, st_s, cnt_s = _mixer(
        x_sample.reshape(t_sample, D_MODEL), pool_s0, state_gla[0].astype(F32),
        cnt_p, weights, batch=dec_batch, seq=dec_seq, tb=SAMPLE_TB, lead_pad=0)

    counts = cnt_s[:, 0].astype(I32)
    padded = (counts + MOE_BLK - 1) // MOE_BLK * MOE_BLK
    ends = jnp.cumsum(padded)
    pstart = ends - padded
    n_blocks = (2 * t_all + N_EXPERTS * (MOE_BLK - 1) + MOE_BLK - 1) // MOE_BLK
    nb = (ends[-1] // MOE_BLK).astype(I32)
    blk_ids = jnp.minimum(jnp.arange(n_blocks, dtype=I32), nb - 1)
    block_e = jnp.sum((ends[None, :] <= (blk_ids * MOE_BLK)[:, None]).astype(I32), axis=1)
    block_e = jnp.minimum(block_e, N_EXPERTS - 1)
    owner = block_e[:, None] == jnp.arange(N_EXPERTS, dtype=I32)
    row_end = jnp.sum(jnp.where(owner, pstart + counts, 0), axis=1)
    block_valid = jnp.clip(row_end - blk_ids * MOE_BLK, 0, MOE_BLK)

    def dest_rows(rt):
        rt = rt.transpose(1, 0, 2).reshape(ROUTE_ROWS, -1)
        onehot = rt[0:2].astype(I32)[..., None] == jnp.arange(N_EXPERTS, dtype=I32)
        return jnp.sum(jnp.where(onehot, pstart, 0), axis=-1) + rt[4:6].astype(I32)

    dest = jnp.concatenate([dest_rows(rt_p), dest_rows(rt_s)], axis=1)
    dest0 = dest[0].reshape(t_all // SC_ROWS, SC_ROWS)
    dest1 = dest[1].reshape(t_all // SC_ROWS, SC_ROWS)
    xs = _dispatch(hn_p, hn_s, dest0, dest1, n_blocks * MOE_BLK)
    ys = _experts(block_e, block_valid.astype(I32), nb[None], xs,
                  w_expert_gate[0], w_expert_up[0], w_expert_down[0])
    nfin = norm_final[None, :]
    cp_chunks = t_prompt // SC_ROWS
    part = cp_chunks // COMBINE_PARTS
    y_prompt = None
    for i in range(COMBINE_PARTS):
        ch = slice(i * part, (i + 1) * part)
        rows_i = _gather(ys, dest0[ch], dest1[ch])
        y_prompt = _combine(h_p, route_p, nfin, rows_i, row0=i * part * SC_ROWS, y_prev=y_prompt)
    y_prompt = y_prompt.reshape(batch, seq, D_MODEL)
    rows_s = _gather(ys, dest0[cp_chunks:], dest1[cp_chunks:])
    y_sample = _combine(h_s, route_s, nfin, rows_s).reshape(dec_batch, dec_seq, D_MODEL)
    new_pool_prompt = pool_p[:, POOL_ROWS - POOL_PAD:][None]
    new_gla_prompt = st_p[None]
    new_pool_sample = pool_s[:, POOL_ROWS - POOL_PAD:][None]
    new_gla_sample = st_s[None]
    return (y_prompt, y_sample, new_pool_prompt, new_gla_prompt, new_pool_sample, new_gla_sample)
```

```python
import functools

import jax
import jax.numpy as jnp
from jax import lax
from jax.experimental import pallas as pl
from jax.experimental.pallas import tpu as pltpu
from jax.experimental.pallas import tpu_sc as plsc

F32 = jnp.float32
BF16 = jnp.bfloat16
U32 = jnp.uint32
I32 = jnp.int32

D_MODEL = 1024
N_META = 16
CHUNK = 64
EPS = 1e-6
D_POOL = 512
POOL_WINDOWS = (2, 4, 8, 16)
POOL_GROUP_DIM = 128
POOL_PAD = 15
POOL_ROWS = 16
GLA_HEADS = 4
GLA_DK = 64
GLA_DV = 128
D_QK = 256
D_V = 512
HEAD_GROUP = 2
GATE_RANK = 16
GATE_TAU = 16.0
D_MAIN = D_POOL + 2 * D_QK + 2 * D_V
N_GROUPS = 4
EXPERTS_PER_GROUP = 8
N_EXPERTS = 32
D_EXPERT = 512

LANES = 128
MXU_DIM = 256
HALF = D_MODEL // 2
HN_PLANES = HALF // LANES
ROUTE_ROWS = 8
ROUTER_ROWS = 64
MIX_TB = 512
SAMPLE_TB = 256
FRONT_TILE = 256
FRONT_PLAN = (2, 0, 2, 2, 2, 0)
MOE_BLK = 512
COMBINE_TB = 512
COMBINE_PARTS = 4
VMEM_LIMIT = 48 * 1024 * 1024
SC_CORES = 2
SC_SUBCORES = 16
SC_WORKERS = SC_CORES * SC_SUBCORES
SC_ROWS = 128


def _rms(x, g):
    return x * lax.rsqrt(jnp.mean(x * x, axis=-1, keepdims=True) + EPS) * g


def _dot(a, b):
    return jnp.dot(a, b, preferred_element_type=F32)


def _dot_nt(a, b):
    return lax.dot_general(a, b, (((1,), (1,)), ((), ())), preferred_element_type=F32)


def _pack_planes(x, ref):
    xb = x.astype(BF16)
    lo = lax.bitcast_convert_type(xb[:, :HALF].astype(F32), U32) >> 16
    hi = lax.bitcast_convert_type(xb[:, HALF:].astype(F32), U32) & jnp.uint32(0xFFFF0000)
    packed = lax.bitcast_convert_type(lo | hi, I32)
    for p in range(HN_PLANES):
        ref[p] = packed[:, p * LANES:(p + 1) * LANES]


def _unpack_planes(planes):
    words = [lax.bitcast_convert_type(p, U32) for p in planes]
    los = [lax.bitcast_convert_type(w << 16, F32) for w in words]
    his = [lax.bitcast_convert_type(w & jnp.uint32(0xFFFF0000), F32) for w in words]
    return jnp.concatenate(los + his, axis=-1)


def _dot_tn(a, b):
    return lax.dot_general(a, b, (((0,), (0,)), ((), ())), preferred_element_type=F32)


def _mixer_kernel(x_ref, pool0_ref, st0_ref, cnt0_ref, tril_ref, sup_ref,
                  nmix_ref, wmain_ref, wz_ref, wgu_ref, bgate_ref, wpool_ref, pscale_ref,
                  gnorm_ref, wout_ref, nffn_ref, wr_ref, br_ref,
                  h_ref, hn_ref, route_ref, route_t_ref, pool_out_ref, st_out_ref, cnt_out_ref,
                  ext_ref, st_ref, kbd_ref, vbd_ref, sbd_ref, o_ref, cnt_ref,
                  xs_ref, proj_ref, z_ref, *, tb, nj, lead_pad, chained):
    s = pl.program_id(0)
    back = jnp.maximum(s - 1, 0)
    j = lax.rem(back, nj)
    n_chunks = tb // CHUNK
    wr_slot = lax.rem(s, 2)
    rd_slot = 1 - wr_slot

    @pl.when(s == 0)
    def _():
        kbd_ref[...] = jnp.zeros_like(kbd_ref)
        vbd_ref[...] = jnp.zeros_like(vbd_ref)
        sbd_ref[...] = jnp.zeros_like(sbd_ref)
        xs_ref[1] = jnp.zeros((tb, D_MODEL), F32)
        proj_ref[1] = jnp.zeros((tb, D_MAIN), F32)
        z_ref[1] = jnp.zeros((GATE_RANK, tb), F32)

    @pl.when(s <= 1)
    def _():
        cnt_ref[...] = cnt0_ref[...]

    def put_state(c, hh, st):
        gg, hp = divmod(hh, HEAD_GROUP)
        sbd_ref[c, gg, hp * GLA_DK:(hp + 1) * GLA_DK, hp * GLA_DV:(hp + 1) * GLA_DV] = st.astype(BF16)

    if chained:
        @pl.when(j == 0)
        def _():
            ext_ref[0:POOL_ROWS, :] = pool0_ref[0]
            st_ref[...] = st0_ref[0]

        for hh in range(GLA_HEADS):
            put_state(0, hh, st_ref[hh])
    else:
        for c in range(n_chunks):
            for hh in range(GLA_HEADS):
                put_state(c, hh, st0_ref[c, hh])

    x_new = x_ref[...]
    xn = _rms(x_new, nmix_ref[...]).astype(BF16)
    xs_ref[wr_slot] = x_new

    tiles_done = [0]

    def front_tiles(stage):
        for t in range(tiles_done[0], tiles_done[0] + FRONT_PLAN[stage]):
            cols = slice(t * FRONT_TILE, (t + 1) * FRONT_TILE)
            proj_ref[wr_slot, :, cols] = _dot(xn, wmain_ref[:, cols])
        tiles_done[0] += FRONT_PLAN[stage]

    front_tiles(0)
    z_ref[wr_slot] = _dot_nt(wz_ref[...], xn)

    x = xs_ref[rd_slot]
    z = z_ref[rd_slot]
    u = proj_ref[rd_slot, :, 0:D_POOL]
    q = proj_ref[rd_slot, :, D_POOL:D_POOL + D_QK]
    k = proj_ref[rd_slot, :, D_POOL + D_QK:D_POOL + 2 * D_QK]
    v = proj_ref[rd_slot, :, D_POOL + 2 * D_QK:D_POOL + 2 * D_QK + D_V]
    r = proj_ref[rd_slot, :, D_POOL + 2 * D_QK + D_V:D_MAIN]

    row = lax.broadcasted_iota(I32, (tb, 1), 0)

    pseg = POOL_ROWS + CHUNK
    if chained:
        ext_ref[POOL_ROWS:POOL_ROWS + tb, :] = u
        ext = ext_ref[...]
    else:
        ext = jnp.concatenate(
            [blk for c in range(n_chunks) for blk in (pool0_ref[c], u[c * CHUNK:(c + 1) * CHUNK])], axis=0)
    pooled = []
    for g, w in enumerate(POOL_WINDOWS):
        sl = slice(g * POOL_GROUP_DIM, (g + 1) * POOL_GROUP_DIM)
        acc = ext[:, sl]
        for d in range(g + 1):
            acc = acc + pltpu.roll(acc, 1 << d, axis=0)
        if chained:
            win = acc[POOL_ROWS:, :]
        else:
            win = jnp.concatenate([acc[c * pseg + POOL_ROWS:(c + 1) * pseg] for c in range(n_chunks)], axis=0)
        if lead_pad:
            cnt = jnp.clip(row - lead_pad + 1, 1, w).astype(F32)
            pooled.append(win / cnt - u[:, sl])
        else:
            pooled.append(win * (1.0 / w) - u[:, sl])
    pys = [_dot(jnp.concatenate(pooled[2 * i:2 * i + 2], axis=-1).astype(BF16), wpool_ref[i])
           for i in range(len(POOL_WINDOWS) // 2)]
    pool_y = jnp.concatenate(pys, axis=-1) * pscale_ref[...]
    if chained:
        ext_ref[0:POOL_ROWS, :] = ext_ref[tb:tb + POOL_ROWS, :]
    else:
        for c in range(n_chunks):
            pool_out_ref[c] = u[(c + 1) * CHUNK - POOL_ROWS:(c + 1) * CHUNK]

    gpre = _dot_tn(z.astype(BF16), wgu_ref[...]) + bgate_ref[...]
    log_a = jax.nn.log_sigmoid(gpre) * (1.0 / GATE_TAU)
    if lead_pad:
        log_a = jnp.where(row >= lead_pad, log_a, 0.0)
    a_hi = log_a.astype(BF16)
    a_lo = (log_a - a_hi.astype(F32)).astype(BF16)
    tril = tril_ref[...]
    seg = tril.shape[0]
    bcum = jnp.concatenate(
        [_dot(tril, a_hi[r0:r0 + seg]) + _dot(tril, a_lo[r0:r0 + seg]) for r0 in range(0, tb, seg)], axis=0)
    front_tiles(1)
    eb = jnp.exp(bcum)
    qi = q * (GLA_DK ** -0.5) * eb
    ki = k * jnp.exp(-bcum)

    rr = lax.broadcasted_iota(I32, (CHUNK, HEAD_GROUP * CHUNK), 0)
    cc = lax.broadcasted_iota(I32, (CHUNK, HEAD_GROUP * CHUNK), 1)
    causal = (cc % CHUNK) <= rr

    lasts = [eb[(c + 1) * CHUNK - 1:(c + 1) * CHUNK, :] for c in range(n_chunks)]
    dcol = jnp.concatenate(lasts + [jnp.zeros((LANES - n_chunks, D_QK), F32)], axis=0).T

    n_grp = GLA_HEADS // HEAD_GROUP
    chunk_rows = [slice(c * CHUNK, (c + 1) * CHUNK) for c in range(n_chunks)]
    grp_k = [slice(g * HEAD_GROUP * GLA_DK, (g + 1) * HEAD_GROUP * GLA_DK) for g in range(n_grp)]
    grp_v = [slice(g * HEAD_GROUP * GLA_DV, (g + 1) * HEAD_GROUP * GLA_DV) for g in range(n_grp)]
    qi_b = qi.astype(BF16)
    ki_b = ki.astype(BF16)
    v_b = v.astype(BF16)

    scores = {}
    for c in range(n_chunks):
        for hh in range(GLA_HEADS):
            gg, hp = divmod(hh, HEAD_GROUP)
            kbd_ref[c, gg, hp * CHUNK:(hp + 1) * CHUNK, hp * GLA_DK:(hp + 1) * GLA_DK] = (
                ki_b[chunk_rows[c], hh * GLA_DK:(hh + 1) * GLA_DK])
            vbd_ref[c, gg, hp * CHUNK:(hp + 1) * CHUNK, hp * GLA_DV:(hp + 1) * GLA_DV] = (
                v_b[chunk_rows[c], hh * GLA_DV:(hh + 1) * GLA_DV])
        for gg in range(n_grp):
            scores[c, gg] = _dot_nt(qi_b[chunk_rows[c], grp_k[gg]], kbd_ref[c, gg])

    kvs = {}
    for c in range(n_chunks):
        kd_c = (ki[chunk_rows[c]] * lasts[c]).astype(BF16)
        for hh in range(GLA_HEADS):
            kvs[c, hh] = _dot_tn(kd_c[:, hh * GLA_DK:(hh + 1) * GLA_DK],
                                 v_b[chunk_rows[c], hh * GLA_DV:(hh + 1) * GLA_DV])
    front_tiles(2)

    for hh in range(GLA_HEADS):
        st = st_ref[hh] if chained else None
        for c in range(n_chunks):
            s_old = st if chained else st0_ref[c, hh]
            s_new = s_old * dcol[hh * GLA_DK:(hh + 1) * GLA_DK, c:c + 1] + kvs[c, hh]
            if not chained:
                st_out_ref[c, hh] = s_new
            else:
                st = s_new
                if c + 1 < n_chunks:
                    put_state(c + 1, hh, s_new)
        if chained:
            st_ref[hh] = st

    for c in range(n_chunks):
        for gg in range(n_grp):
            p = jnp.where(causal, scores[c, gg], 0.0).astype(BF16)
            o_ref[chunk_rows[c], grp_v[gg]] = (
                _dot(p, vbd_ref[c, gg]) + _dot(qi_b[chunk_rows[c], grp_k[gg]], sbd_ref[c, gg]))

    o = o_ref[...]
    ons = []
    for hh in range(GLA_HEADS):
        oh = o[:, hh * GLA_DV:(hh + 1) * GLA_DV]
        ons.append(oh * lax.rsqrt(jnp.mean(oh * oh, axis=-1, keepdims=True) + EPS))
    og = jnp.concatenate(ons, axis=-1) * gnorm_ref[...] * (r * jax.nn.sigmoid(r))
    mix = _dot(jnp.concatenate([pool_y, og], axis=-1).astype(BF16), wout_ref[...])
    front_tiles(3)

    h = x + mix
    h_ref[...] = h
    hn = _rms(h, nffn_ref[...]).astype(BF16)
    _pack_planes(hn, hn_ref)

    logits = _dot_nt(wr_ref[...], hn) + br_ref[...]
    front_tiles(4)
    sub = lax.broadcasted_iota(I32, (EXPERTS_PER_GROUP, tb), 0).astype(F32)
    neg = jnp.float32(-jnp.inf)
    big = jnp.float32(EXPERTS_PER_GROUP)
    tile0 = logits[0:EXPERTS_PER_GROUP]
    is_g = sub < N_GROUPS
    gmax = jnp.max(jnp.where(is_g, tile0, neg), axis=0, keepdims=True)
    gsum = jnp.sum(jnp.where(is_g, jnp.exp(tile0 - gmax), 0.0), axis=0, keepdims=True)
    p_g = 1.0 / gsum
    gidx = jnp.min(jnp.where(is_g & (tile0 == gmax), sub, big), axis=0, keepdims=True)
    el = logits[N_GROUPS * EXPERTS_PER_GROUP:(N_GROUPS + 1) * EXPERTS_PER_GROUP]
    for g in range(N_GROUPS - 2, -1, -1):
        el = jnp.where(gidx == g, logits[(g + 1) * EXPERTS_PER_GROUP:(g + 2) * EXPERTS_PER_GROUP], el)
    m1 = jnp.max(el, axis=0, keepdims=True)
    i1 = jnp.min(jnp.where(el == m1, sub, big), axis=0, keepdims=True)
    rest = sub != i1
    m2 = jnp.max(jnp.where(rest, el, neg), axis=0, keepdims=True)
    i2 = jnp.min(jnp.where(rest & (el == m2), sub, big), axis=0, keepdims=True)
    t2 = jnp.exp(m2 - m1)
    den = 1.0 + t2
    g1 = p_g / den
    g2 = p_g * t2 / den
    e1 = gidx * EXPERTS_PER_GROUP + i1
    e2 = gidx * EXPERTS_PER_GROUP + i2

    eid = lax.broadcasted_iota(I32, (N_EXPERTS, tb), 0).astype(F32)
    oh1 = eid == e1
    oh2 = eid == e2
    both = jnp.where(oh1 | oh2, 1.0, 0.0)
    cnt = cnt_ref[...]
    before = _dot(both.astype(BF16), sup_ref[...]) + cnt
    front_tiles(5)
    assert tiles_done[0] * FRONT_TILE == D_MAIN
    pos1 =jnp.sum(jnp.where(oh1, before, 0.0), axis=0, keepdims=True)
    pos2 = jnp.sum(jnp.where(oh2, before, 0.0), axis=0, keepdims=True)
    cnt_new = cnt + jnp.sum(both, axis=1, keepdims=True)
    cnt_ref[...] = cnt_new
    cnt_out_ref[...] = cnt_new

    zero = jnp.zeros_like(e1)
    route_t = jnp.concatenate([e1, e2, g1, g2, pos1, pos2, zero, zero], axis=0)
    route_t_ref[0] = route_t
    route_ref[...] = jnp.concatenate([route_t, jnp.zeros((LANES - ROUTE_ROWS, tb), F32)], axis=0).T

    if chained:
        @pl.when(j == nj - 1)
        def _():
            pool_out_ref[0] = ext_ref[0:POOL_ROWS, :]
            st_out_ref[0] = st_ref[...]


def _mixer(x2d, pool0, st0, cnt0, weights, *, batch, seq, tb, lead_pad):
    chained = tb <= seq
    total_rows = batch * seq
    n_blk = total_rows // tb
    nj = seq // tb if chained else 1
    per_blk = 1 if chained else tb // seq
    assert chained or (seq == CHUNK and batch % per_blk == 0)
    shared = pool0.shape[0] == 1
    front = lambda s: jnp.minimum(s, n_blk - 1)
    back = lambda s: jnp.maximum(s - 1, 0)
    stream = lambda s: back(s) // nj
    st_idx = (lambda s: (0, 0, 0)) if shared else (lambda s: (stream(s), 0, 0))
    gla_idx = (lambda s: (0, 0, 0, 0)) if shared else (lambda s: (stream(s), 0, 0, 0))
    const2 = lambda s: (0, 0)
    tok_out = lambda s: (back(s), 0)

    seg = min(tb, MXU_DIM)
    ii = jnp.arange(seg)
    tril = ((ii[:, None] >= ii[None, :]) & (ii[:, None] // CHUNK == ii[None, :] // CHUNK)).astype(BF16)
    ii = jnp.arange(tb)
    sup = (ii[:, None] < ii[None, :]).astype(BF16)

    in_specs = [
        pl.BlockSpec((tb, D_MODEL), lambda s: (front(s), 0)),
        pl.BlockSpec((per_blk, POOL_ROWS, D_POOL), st_idx),
        pl.BlockSpec((per_blk, GLA_HEADS, GLA_DK, GLA_DV), gla_idx),
        pl.BlockSpec((N_EXPERTS, 1), const2),
        pl.BlockSpec((seg, seg), const2),
        pl.BlockSpec((tb, tb), const2),
    ]
    for wgt in weights:
        in_specs.append(pl.BlockSpec(wgt.shape, (lambda s, n=wgt.ndim: (0,) * n)))
    args = [x2d, pool0, st0, cnt0, tril, sup, *weights]

    out_shape = [
        jax.ShapeDtypeStruct((total_rows, D_MODEL), F32),
        jax.ShapeDtypeStruct((HN_PLANES, total_rows, LANES), I32),
        jax.ShapeDtypeStruct((total_rows, LANES), F32),
        jax.ShapeDtypeStruct((n_blk, ROUTE_ROWS, tb), F32),
        jax.ShapeDtypeStruct((batch, POOL_ROWS, D_POOL), F32),
        jax.ShapeDtypeStruct((batch, GLA_HEADS, GLA_DK, GLA_DV), F32),
        jax.ShapeDtypeStruct((N_EXPERTS, 1), F32),
    ]
    out_specs = [
        pl.BlockSpec((tb, D_MODEL), tok_out),
        pl.BlockSpec((HN_PLANES, tb, LANES), lambda s: (0, back(s), 0)),
        pl.BlockSpec((tb, LANES), tok_out),
        pl.BlockSpec((1, ROUTE_ROWS, tb), lambda s: (back(s), 0, 0)),
        pl.BlockSpec((per_blk, POOL_ROWS, D_POOL), lambda s: (stream(s), 0, 0)),
        pl.BlockSpec((per_blk, GLA_HEADS, GLA_DK, GLA_DV), lambda s: (stream(s), 0, 0, 0)),
        pl.BlockSpec((N_EXPERTS, 1), const2),
    ]
    n_grp = GLA_HEADS // HEAD_GROUP
    scratch = [
        pltpu.VMEM((POOL_ROWS + tb, D_POOL), F32),
        pltpu.VMEM((GLA_HEADS, GLA_DK, GLA_DV), F32),
        pltpu.VMEM((tb // CHUNK, n_grp, HEAD_GROUP * CHUNK, HEAD_GROUP * GLA_DK), BF16),
        pltpu.VMEM((tb // CHUNK, n_grp, HEAD_GROUP * CHUNK, HEAD_GROUP * GLA_DV), BF16),
        pltpu.VMEM((tb // CHUNK, n_grp, HEAD_GROUP * GLA_DK, HEAD_GROUP * GLA_DV), BF16),
        pltpu.VMEM((tb, D_V), F32),
        pltpu.VMEM((N_EXPERTS, 1), F32),
        pltpu.VMEM((2, tb, D_MODEL), F32),
        pltpu.VMEM((2, tb, D_MAIN), F32),
        pltpu.VMEM((2, GATE_RANK, tb), F32),
    ]
    return pl.pallas_call(
        functools.partial(_mixer_kernel, tb=tb, nj=nj, lead_pad=lead_pad, chained=chained),
        grid=(n_blk + 1,),
        in_specs=in_specs,
        out_specs=out_specs,
        out_shape=out_shape,
        scratch_shapes=scratch,
        compiler_params=pltpu.CompilerParams(
            dimension_semantics=("arbitrary",), vmem_limit_bytes=VMEM_LIMIT),
        name=f"mixer_tb{tb}_pad{lead_pad}",
    )(*args)


def _sc_mesh():
    return plsc.VectorSubcoreMesh(core_axis_name="c", subcore_axis_name="s",
                                  num_cores=SC_CORES, num_subcores=SC_SUBCORES)


def _sc_worker():
    return lax.axis_index("s") * SC_CORES + lax.axis_index("c")


def _plane_rows(dest, planes, rows_per_plane):
    offs = (jnp.arange(planes, dtype=I32) * rows_per_plane)[None, :, None]
    return dest[:, None, :] + offs


def _dispatch(hn_p, hn_s, dest0, dest1, n_rows):
    planes, t_p, _ = hn_p.shape
    t_s = hn_s.shape[1]
    n_cp = t_p // SC_ROWS
    cp = n_cp // SC_WORKERS
    n_cs = t_s // SC_ROWS
    assert t_p == cp * SC_ROWS * SC_WORKERS and t_s == n_cs * SC_ROWS and n_cs <= SC_WORKERS
    idx0 = _plane_rows(dest0, planes, n_rows)
    idx1 = _plane_rows(dest1, planes, n_rows)

    def body(hnp_hbm, hns_hbm, d0_hbm, d1_hbm, xs_hbm, rows_v, i0_v, i1_v, is0_v, is1_v, sem_in, sem_out):
        wid = _sc_worker()
        pltpu.sync_copy(d0_hbm.at[pl.ds(wid * cp, cp)], i0_v)
        pltpu.sync_copy(d1_hbm.at[pl.ds(wid * cp, cp)], i1_v)

        def move(src_hbm, src_rows, row0, i0, i1, c):
            loads = [pltpu.async_copy(src_hbm.at[pl.ds(p * src_rows + row0, SC_ROWS)], rows_v.at[p], sem_in)
                     for p in range(planes)]
            for cpy in loads:
                cpy.wait()
            stores = []
            for p in range(planes):
                stores.append(pltpu.async_copy(rows_v.at[p], xs_hbm.at[i0.at[c, p]], sem_out))
                stores.append(pltpu.async_copy(rows_v.at[p], xs_hbm.at[i1.at[c, p]], sem_out))
            for cpy in stores:
                cpy.wait()

        @pl.loop(0, cp)
        def _(c):
            move(hnp_hbm, t_p, (wid * cp + c) * SC_ROWS, i0_v, i1_v, c)

        @pl.when(wid < n_cs)
        def _():
            pltpu.sync_copy(d0_hbm.at[pl.ds(n_cp + wid, 1)], is0_v)
            pltpu.sync_copy(d1_hbm.at[pl.ds(n_cp + wid, 1)], is1_v)
            move(hns_hbm, t_s, wid * SC_ROWS, is0_v, is1_v, 0)

    xs = pl.kernel(
        body,
        out_type=jax.ShapeDtypeStruct((planes * n_rows, LANES), I32),
        mesh=_sc_mesh(),
        scratch_types=[
            pltpu.VMEM((planes, SC_ROWS, LANES), I32),
            pltpu.VMEM((cp, planes, SC_ROWS), I32),
            pltpu.VMEM((cp, planes, SC_ROWS), I32),
            pltpu.VMEM((1, planes, SC_ROWS), I32),
            pltpu.VMEM((1, planes, SC_ROWS), I32),
            pltpu.SemaphoreType.DMA,
            pltpu.SemaphoreType.DMA,
        ],
        name="moe_dispatch_sc",
    )(hn_p.reshape(planes * t_p, LANES), hn_s.reshape(planes * t_s, LANES), idx0, idx1)
    return xs.reshape(planes, n_rows, LANES)


def _gather(ys, dest0, dest1):
    planes, n_rows, _ = ys.shape
    n_chunks = dest0.shape[0]
    n_tok = n_chunks * SC_ROWS
    cpw = max(n_chunks // SC_WORKERS, 1)
    assert n_chunks <= SC_WORKERS or n_chunks == cpw * SC_WORKERS
    idx = (_plane_rows(dest0, planes, n_rows), _plane_rows(dest1, planes, n_rows))

    def body(ys_hbm, d0_hbm, d1_hbm, out_hbm, rows_v, i0_v, i1_v, sem_in, sem_out):
        wid = _sc_worker()

        def work():
            pltpu.sync_copy(d0_hbm.at[pl.ds(wid * cpw, cpw)], i0_v)
            pltpu.sync_copy(d1_hbm.at[pl.ds(wid * cpw, cpw)], i1_v)

            @pl.loop(0, cpw)
            def _(c):
                row0 = (wid * cpw + c) * SC_ROWS
                for j, i_v in enumerate((i0_v, i1_v)):
                    loads = [pltpu.async_copy(ys_hbm.at[i_v.at[c, p]], rows_v.at[p], sem_in)
                             for p in range(planes)]
                    for cpy in loads:
                        cpy.wait()
                    stores = [
                        pltpu.async_copy(
                            rows_v.at[p], out_hbm.at[pl.ds((j * planes + p) * n_tok + row0, SC_ROWS)], sem_out)
                        for p in range(planes)]
                    for cpy in stores:
                        cpy.wait()

        if n_chunks < SC_WORKERS:
            pl.when(wid < n_chunks)(work)
        else:
            work()

    out = pl.kernel(
        body,
        out_type=jax.ShapeDtypeStruct((2 * planes * n_tok, LANES), ys.dtype),
        mesh=_sc_mesh(),
        scratch_types=[
            pltpu.VMEM((planes, SC_ROWS, LANES), ys.dtype),
            pltpu.VMEM((cpw, planes, SC_ROWS), I32),
            pltpu.VMEM((cpw, planes, SC_ROWS), I32),
            pltpu.SemaphoreType.DMA,
            pltpu.SemaphoreType.DMA,
        ],
        name="moe_gather_sc",
    )(ys.reshape(planes * n_rows, LANES), *idx)
    return out.reshape(2, planes, n_tok, LANES)


def _expert_kernel(be_ref, bv_ref, nb_ref, xs_ref, wg_ref, wu_ref, wd_ref, ys_ref, wgu_s, wd_s):
    i = pl.program_id(0)
    live = i < nb_ref[0]
    prev = be_ref[jnp.maximum(i - 1, 0)]
    fresh = (i == 0) | (be_ref[i] != prev)

    @pl.when(live & fresh)
    def _():
        wgu_s[:, 0:D_EXPERT] = wg_ref[0].astype(BF16)
        wgu_s[:, D_EXPERT:2 * D_EXPERT] = wu_ref[0].astype(BF16)
        wd_s[...] = wd_ref[0].astype(BF16)

    @pl.when(live)
    def _():
        valid = lax.broadcasted_iota(I32, (MOE_BLK, LANES), 0) < bv_ref[i]
        xb = _unpack_planes([jnp.where(valid, xs_ref[p], 0) for p in range(HN_PLANES)]).astype(BF16)
        gu = _dot(xb, wgu_s[...])
        gate = gu[:, 0:D_EXPERT]
        hmid = gate * jax.nn.sigmoid(gate) * gu[:, D_EXPERT:]
        _pack_planes(_dot(hmid.astype(BF16), wd_s[...]), ys_ref)

    @pl.when(jnp.logical_not(live))
    def _():
        ys_ref[...] = jnp.zeros_like(ys_ref)


def _experts(block_e, block_valid, nb, xs, w_eg, w_eu, w_ed):
    n_rows = xs.shape[1]
    n_blocks = n_rows // MOE_BLK
    row_idx = lambda i, be, bv, nb: (0, jnp.minimum(i, nb[0] - 1), 0)
    w_idx = lambda i, be, bv, nb: (be[i], 0, 0)
    grid_spec = pltpu.PrefetchScalarGridSpec(
        num_scalar_prefetch=3,
        grid=(n_blocks,),
        in_specs=[
            pl.BlockSpec((HN_PLANES, MOE_BLK, LANES), row_idx),
            pl.BlockSpec((1, D_MODEL, D_EXPERT), w_idx),
            pl.BlockSpec((1, D_MODEL, D_EXPERT), w_idx),
            pl.BlockSpec((1, D_EXPERT, D_MODEL), w_idx),
        ],
        out_specs=pl.BlockSpec((HN_PLANES, MOE_BLK, LANES), lambda i, be, bv, nb: (0, i, 0)),
        scratch_shapes=[
            pltpu.VMEM((D_MODEL, 2 * D_EXPERT), BF16),
            pltpu.VMEM((D_EXPERT, D_MODEL), BF16),
        ],
    )
    return pl.pallas_call(
        _expert_kernel,
        grid_spec=grid_spec,
        out_shape=jax.ShapeDtypeStruct((HN_PLANES, n_rows, LANES), I32),
        compiler_params=pltpu.CompilerParams(
            dimension_semantics=("arbitrary",), vmem_limit_bytes=VMEM_LIMIT),
        name="moe_experts",
    )(block_e, block_valid, nb, xs, w_eg, w_eu, w_ed)


def _combine_kernel(h_ref, route_ref, nfin_ref, rows_ref, *rest):
    y_ref = rest[-1]
    route = route_ref[...]
    ys1 = _unpack_planes([rows_ref[0, p] for p in range(HN_PLANES)])
    ys2 = _unpack_planes([rows_ref[1, p] for p in range(HN_PLANES)])
    out = h_ref[...] + (ys1 * route[:, 2:3] + ys2 * route[:, 3:4])
    y_ref[...] = _rms(out, nfin_ref[...])


def _combine(h, route, norm_final, rows, *, row0=0, y_prev=None):
    total = h.shape[0]
    tb = COMBINE_TB
    blk0 = row0 // tb
    args = [h, route, norm_final, rows]
    in_specs = [
        pl.BlockSpec((tb, D_MODEL), lambda i: (blk0 + i, 0)),
        pl.BlockSpec((tb, LANES), lambda i: (blk0 + i, 0)),
        pl.BlockSpec((1, D_MODEL), lambda i: (0, 0)),
        pl.BlockSpec((2, HN_PLANES, tb, LANES), lambda i: (0, 0, i, 0)),
    ]
    aliases = {}
    if y_prev is not None:
        aliases[len(args)] = 0
        args.append(y_prev)
        in_specs.append(pl.BlockSpec(memory_space=pl.ANY))
    return pl.pallas_call(
        _combine_kernel,
        grid=(rows.shape[2] // tb,),
        in_specs=in_specs,
        out_specs=pl.BlockSpec((tb, D_MODEL), lambda i: (blk0 + i, 0)),
        out_shape=jax.ShapeDtypeStruct((total, D_MODEL), F32),
        input_output_aliases=aliases,
        compiler_params=pltpu.CompilerParams(
            dimension_semantics=("arbitrary",), vmem_limit_bytes=VMEM_LIMIT),
        name="moe_combine",
    )(*args)


def kernel(x_prompt, x_sample, state_pool, state_gla, meta_tokens, norm_mix, w_in, w_gate_up, b_gate, w_pool, pool_scale, gla_norm, w_out, norm_ffn, w_router_group, b_router_group, w_router_expert, b_router_expert, w_expert_gate, w_expert_up, w_expert_down, norm_final):
    assert w_in.shape[0] == 1, "one encoder layer"
    batch, seq, _ = x_prompt.shape
    dec_batch, dec_seq, _ = x_sample.shape
    assert seq % MIX_TB == 0 and dec_seq == CHUNK and N_META <= CHUNK
    t_prompt = batch * seq
    t_sample = dec_batch * dec_seq
    t_all = t_prompt + t_sample
    assert t_prompt % COMBINE_TB == 0 and t_sample % COMBINE_TB == 0

    w_in0 = w_in[0]
    gpad = EXPERTS_PER_GROUP - N_GROUPS
    rpad = ROUTER_ROWS - EXPERTS_PER_GROUP - N_EXPERTS
    w_router = jnp.concatenate([
        w_router_group[0].T, jnp.zeros((gpad, D_MODEL), F32),
        w_router_expert[0].T, jnp.zeros((rpad, D_MODEL), F32)], axis=0)
    b_router = jnp.concatenate([
        b_router_group[0], jnp.zeros((gpad,), F32), b_router_expert[0], jnp.zeros((rpad,), F32)])
    zg = jnp.zeros((POOL_GROUP_DIM, POOL_GROUP_DIM), F32)
    w_pool_pairs = jnp.stack([
        jnp.block([[w_pool[0, 2 * i], zg], [zg, w_pool[0, 2 * i + 1]]]) for i in range(len(POOL_WINDOWS) // 2)])
    weights = (
        norm_mix[0][None, :],
        w_in0[:, :D_MAIN].astype(BF16),
        w_in0[:, D_MAIN:].T.astype(BF16),
        w_gate_up[0].astype(BF16),
        b_gate[0][None, :],
        w_pool_pairs.astype(BF16),
        pool_scale[0][None, :],
        gla_norm[0][None, :],
        w_out[0].astype(BF16),
        norm_ffn[0][None, :],
        w_router.astype(BF16),
        b_router[:, None],
    )

    zero_cnt = jnp.zeros((N_EXPERTS, 1), F32)
    x_meta = jnp.pad(meta_tokens.astype(F32), ((CHUNK - N_META, 0), (0, 0)))
    meta = _mixer(x_meta, jnp.zeros((1, POOL_ROWS, D_POOL), F32),
                  jnp.zeros((1, GLA_HEADS, GLA_DK, GLA_DV), F32), zero_cnt,
                  weights, batch=1, seq=CHUNK, tb=CHUNK, lead_pad=CHUNK - N_META)
    h_m, hn_m, route_m, rt_m, pool_m, st_m, cnt_m = meta
    del h_m, hn_m, route_m, rt_m, cnt_m
    h_p, hn_p, route_p, rt_p, pool_p, st_p, cnt_p = _mixer(
        x_prompt.reshape(t_prompt, D_MODEL), pool_m, st_m, zero_cnt, weights,
        batch=batch, seq=seq, tb=MIX_TB, lead_pad=0)
    pool_s0 = jnp.pad(state_pool[0], ((0, 0), (POOL_ROWS - POOL_PAD, 0), (0, 0)))
    h_s, hn_s, route_s, rt_s, pool_s, st_s, cnt_s = _mixer(
        x_sample.reshape(t_sample, D_MODEL), pool_s0, state_gla[0].astype(F32),
        cnt_p, weights, batch=dec_batch, seq=dec_seq, tb=SAMPLE_TB, lead_pad=0)

    counts = cnt_s[:, 0].astype(I32)
    padded = (counts + MOE_BLK - 1) // MOE_BLK * MOE_BLK
    ends = jnp.cumsum(padded)
    pstart = ends - padded
    n_blocks = (2 * t_all + N_EXPERTS * (MOE_BLK - 1) + MOE_BLK - 1) // MOE_BLK
    nb = (ends[-1] // MOE_BLK).astype(I32)
    blk_ids = jnp.minimum(jnp.arange(n_blocks, dtype=I32), nb - 1)
    block_e = jnp.sum((ends[None, :] <= (blk_ids * MOE_BLK)[:, None]).astype(I32), axis=1)
    block_e = jnp.minimum(block_e, N_EXPERTS - 1)
    owner = block_e[:, None] == jnp.arange(N_EXPERTS, dtype=I32)
    row_end = jnp.sum(jnp.where(owner, pstart + counts, 0), axis=1)
    block_valid = jnp.clip(row_end - blk_ids * MOE_BLK, 0, MOE_BLK)

    def dest_rows(rt):
        rt = rt.transpose(1, 0, 2).reshape(ROUTE_ROWS, -1)
        onehot = rt[0:2].astype(I32)[..., None] == jnp.arange(N_EXPERTS, dtype=I32)
        return jnp.sum(jnp.where(onehot, pstart, 0), axis=-1) + rt[4:6].astype(I32)

    dest = jnp.concatenate([dest_rows(rt_p), dest_rows(rt_s)], axis=1)
    dest0 = dest[0].reshape(t_all // SC_ROWS, SC_ROWS)
    dest1 = dest[1].reshape(t_all // SC_ROWS, SC_ROWS)
    xs = _dispatch(hn_p, hn_s, dest0, dest1, n_blocks * MOE_BLK)
    ys = _experts(block_e, block_valid.astype(I32), nb[None], xs,
                  w_expert_gate[0], w_expert_up[0], w_expert_down[0])
    nfin = norm_final[None, :]
    cp_chunks = t_prompt // SC_ROWS
    part = cp_chunks // COMBINE_PARTS
    y_prompt = None
    for i in range(COMBINE_PARTS):
        ch = slice(i * part, (i + 1) * part)
        rows_i = _gather(ys, dest0[ch], dest1[ch])
        y_prompt = _combine(h_p, route_p, nfin, rows_i, row0=i * part * SC_ROWS, y_prev=y_prompt)
    y_prompt = y_prompt.reshape(batch, seq, D_MODEL)
    rows_s = _gather(ys, dest0[cp_chunks:], dest1[cp_chunks:])
    y_sample = _combine(h_s, route_s, nfin, rows_s).reshape(dec_batch, dec_seq, D_MODEL)
    new_pool_prompt = pool_p[:, POOL_ROWS - POOL_PAD:][None]
    new_gla_prompt = st_p[None]
    new_pool_sample = pool_s[:, POOL_ROWS - POOL_PAD:][None]
    new_gla_sample = st_s[None]
    return (y_prompt, y_sample, new_pool_prompt, new_gla_prompt, new_pool_sample, new_gla_sample)
```

```python
import functools

import jax
import jax.numpy as jnp
from jax import lax
from jax.experimental import pallas as pl
from jax.experimental.pallas import tpu as pltpu
from jax.experimental.pallas import tpu_sc as plsc

F32 = jnp.float32
BF16 = jnp.bfloat16
U32 = jnp.uint32
I32 = jnp.int32

D_MODEL = 1024
N_META = 16
CHUNK = 64
EPS = 1e-6
D_POOL = 512
POOL_WINDOWS = (2, 4, 8, 16)
POOL_GROUP_DIM = 128
POOL_PAD = 15
POOL_ROWS = 16
GLA_HEADS = 4
GLA_DK = 64
GLA_DV = 128
D_QK = 256
D_V = 512
HEAD_GROUP = 2
GATE_RANK = 16
GATE_TAU = 16.0
D_MAIN = D_POOL + 2 * D_QK + 2 * D_V
N_GROUPS = 4
EXPERTS_PER_GROUP = 8
N_EXPERTS = 32
D_EXPERT = 512

LANES = 128
MXU_DIM = 256
HALF = D_MODEL // 2
HN_PLANES = HALF // LANES
ROUTE_ROWS = 8
ROUTER_ROWS = 64
MIX_TB = 512
SAMPLE_TB = 256
FRONT_TILE = 256
FRONT_PLAN = (2, 0, 2, 2, 2, 0)
MOE_BLK = 512
COMBINE_TB = 512
COMBINE_PARTS = 4
VMEM_LIMIT = 48 * 1024 * 1024
SC_CORES = 2
SC_SUBCORES = 16
SC_WORKERS = SC_CORES * SC_SUBCORES
SC_ROWS = 128


def _rms(x, g):
    return x * lax.rsqrt(jnp.mean(x * x, axis=-1, keepdims=True) + EPS) * g


def _dot(a, b):
    return jnp.dot(a, b, preferred_element_type=F32)


def _dot_nt(a, b):
    return lax.dot_general(a, b, (((1,), (1,)), ((), ())), preferred_element_type=F32)


def _pack_planes(x, ref):
    xb = x.astype(BF16)
    lo = lax.bitcast_convert_type(xb[:, :HALF].astype(F32), U32) >> 16
    hi = lax.bitcast_convert_type(xb[:, HALF:].astype(F32), U32) & jnp.uint32(0xFFFF0000)
    packed = lax.bitcast_convert_type(lo | hi, I32)
    for p in range(HN_PLANES):
        ref[p] = packed[:, p * LANES:(p + 1) * LANES]


def _unpack_planes(planes):
    words = [lax.bitcast_convert_type(p, U32) for p in planes]
    los = [lax.bitcast_convert_type(w << 16, F32) for w in words]
    his = [lax.bitcast_convert_type(w & jnp.uint32(0xFFFF0000), F32) for w in words]
    return jnp.concatenate(los + his, axis=-1)


def _dot_tn(a, b):
    return lax.dot_general(a, b, (((0,), (0,)), ((), ())), preferred_element_type=F32)


def _mixer_kernel(x_ref, pool0_ref, st0_ref, cnt0_ref, tril_ref, sup_ref,
                  nmix_ref, wmain_ref, wz_ref, wgu_ref, bgate_ref, wpool_ref, pscale_ref,
                  gnorm_ref, wout_ref, nffn_ref, wr_ref, br_ref,
                  h_ref, hn_ref, route_ref, route_t_ref, pool_out_ref, st_out_ref, cnt_out_ref,
                  ext_ref, st_ref, kbd_ref, vbd_ref, sbd_ref, o_ref, cnt_ref,
                  xs_ref, proj_ref, z_ref, *, tb, nj, lead_pad, chained):
    s = pl.program_id(0)
    back = jnp.maximum(s - 1, 0)
    j = lax.rem(back, nj)
    n_chunks = tb // CHUNK
    wr_slot = lax.rem(s, 2)
    rd_slot = 1 - wr_slot

    @pl.when(s == 0)
    def _():
        kbd_ref[...] = jnp.zeros_like(kbd_ref)
        vbd_ref[...] = jnp.zeros_like(vbd_ref)
        sbd_ref[...] = jnp.zeros_like(sbd_ref)
        xs_ref[1] = jnp.zeros((tb, D_MODEL), F32)
        proj_ref[1] = jnp.zeros((tb, D_MAIN), F32)
        z_ref[1] = jnp.zeros((GATE_RANK, tb), F32)

    @pl.when(s <= 1)
    def _():
        cnt_ref[...] = cnt0_ref[...]

    def put_state(c, hh, st):
        gg, hp = divmod(hh, HEAD_GROUP)
        sbd_ref[c, gg, hp * GLA_DK:(hp + 1) * GLA_DK, hp * GLA_DV:(hp + 1) * GLA_DV] = st.astype(BF16)

    if chained:
        @pl.when(j == 0)
        def _():
            ext_ref[0:POOL_ROWS, :] = pool0_ref[0]
            st_ref[...] = st0_ref[0]

        for hh in range(GLA_HEADS):
            put_state(0, hh, st_ref[hh])
    else:
        for c in range(n_chunks):
            for hh in range(GLA_HEADS):
                put_state(c, hh, st0_ref[c, hh])

    x_new = x_ref[...]
    xn = _rms(x_new, nmix_ref[...]).astype(BF16)
    xs_ref[wr_slot] = x_new

    tiles_done = [0]

    def front_tiles(stage):
        for t in range(tiles_done[0], tiles_done[0] + FRONT_PLAN[stage]):
            cols = slice(t * FRONT_TILE, (t + 1) * FRONT_TILE)
            proj_ref[wr_slot, :, cols] = _dot(xn, wmain_ref[:, cols])
        tiles_done[0] += FRONT_PLAN[stage]

    front_tiles(0)
    z_ref[wr_slot] = _dot_nt(wz_ref[...], xn)

    x = xs_ref[rd_slot]
    z = z_ref[rd_slot]
    u = proj_ref[rd_slot, :, 0:D_POOL]
    q = proj_ref[rd_slot, :, D_POOL:D_POOL + D_QK]
    k = proj_ref[rd_slot, :, D_POOL + D_QK:D_POOL + 2 * D_QK]
    v = proj_ref[rd_slot, :, D_POOL + 2 * D_QK:D_POOL + 2 * D_QK + D_V]
    r = proj_ref[rd_slot, :, D_POOL + 2 * D_QK + D_V:D_MAIN]

    row = lax.broadcasted_iota(I32, (tb, 1), 0)

    pseg = POOL_ROWS + CHUNK
    if chained:
        ext_ref[POOL_ROWS:POOL_ROWS + tb, :] = u
        ext = ext_ref[...]
    else:
        ext = jnp.concatenate(
            [blk for c in range(n_chunks) for blk in (pool0_ref[c], u[c * CHUNK:(c + 1) * CHUNK])], axis=0)
    pooled = []
    for g, w in enumerate(POOL_WINDOWS):
        sl = slice(g * POOL_GROUP_DIM, (g + 1) * POOL_GROUP_DIM)
        acc = ext[:, sl]
        for d in range(g + 1):
            acc = acc + pltpu.roll(acc, 1 << d, axis=0)
        if chained:
            win = acc[POOL_ROWS:, :]
        else:
            win = jnp.concatenate([acc[c * pseg + POOL_ROWS:(c + 1) * pseg] for c in range(n_chunks)], axis=0)
        if lead_pad:
            cnt = jnp.clip(row - lead_pad + 1, 1, w).astype(F32)
            pooled.append(win / cnt - u[:, sl])
        else:
            pooled.append(win * (1.0 / w) - u[:, sl])
    pys = [_dot(jnp.concatenate(pooled[2 * i:2 * i + 2], axis=-1).astype(BF16), wpool_ref[i])
           for i in range(len(POOL_WINDOWS) // 2)]
    pool_y = jnp.concatenate(pys, axis=-1) * pscale_ref[...]
    if chained:
        ext_ref[0:POOL_ROWS, :] = ext_ref[tb:tb + POOL_ROWS, :]
    else:
        for c in range(n_chunks):
            pool_out_ref[c] = u[(c + 1) * CHUNK - POOL_ROWS:(c + 1) * CHUNK]

    gpre = _dot_tn(z.astype(BF16), wgu_ref[...]) + bgate_ref[...]
    log_a = jax.nn.log_sigmoid(gpre) * (1.0 / GATE_TAU)
    if lead_pad:
        log_a = jnp.where(row >= lead_pad, log_a, 0.0)
    a_hi = log_a.astype(BF16)
    a_lo = (log_a - a_hi.astype(F32)).astype(BF16)
    tril = tril_ref[...]
    seg = tril.shape[0]
    bcum = jnp.concatenate(
        [_dot(tril, a_hi[r0:r0 + seg]) + _dot(tril, a_lo[r0:r0 + seg]) for r0 in range(0, tb, seg)], axis=0)
    front_tiles(1)
    eb = jnp.exp(bcum)
    qi = q * (GLA_DK ** -0.5) * eb
    ki = k * jnp.exp(-bcum)

    rr = lax.broadcasted_iota(I32, (CHUNK, HEAD_GROUP * CHUNK), 0)
    cc = lax.broadcasted_iota(I32, (CHUNK, HEAD_GROUP * CHUNK), 1)
    causal = (cc % CHUNK) <= rr

    lasts = [eb[(c + 1) * CHUNK - 1:(c + 1) * CHUNK, :] for c in range(n_chunks)]
    dcol = jnp.concatenate(lasts + [jnp.zeros((LANES - n_chunks, D_QK), F32)], axis=0).T

    n_grp = GLA_HEADS // HEAD_GROUP
    chunk_rows = [slice(c * CHUNK, (c + 1) * CHUNK) for c in range(n_chunks)]
    grp_k = [slice(g * HEAD_GROUP * GLA_DK, (g + 1) * HEAD_GROUP * GLA_DK) for g in range(n_grp)]
    grp_v = [slice(g * HEAD_GROUP * GLA_DV, (g + 1) * HEAD_GROUP * GLA_DV) for g in range(n_grp)]
    qi_b = qi.astype(BF16)
    ki_b = ki.astype(BF16)
    v_b = v.astype(BF16)

    scores = {}
    for c in range(n_chunks):
        for hh in range(GLA_HEADS):
            gg, hp = divmod(hh, HEAD_GROUP)
            kbd_ref[c, gg, hp * CHUNK:(hp + 1) * CHUNK, hp * GLA_DK:(hp + 1) * GLA_DK] = (
                ki_b[chunk_rows[c], hh * GLA_DK:(hh + 1) * GLA_DK])
            vbd_ref[c, gg, hp * CHUNK:(hp + 1) * CHUNK, hp * GLA_DV:(hp + 1) * GLA_DV] = (
                v_b[chunk_rows[c], hh * GLA_DV:(hh + 1) * GLA_DV])
        for gg in range(n_grp):
            scores[c, gg] = _dot_nt(qi_b[chunk_rows[c], grp_k[gg]], kbd_ref[c, gg])

    kvs = {}
    for c in range(n_chunks):
        kd_c = (ki[chunk_rows[c]] * lasts[c]).astype(BF16)
        for hh in range(GLA_HEADS):
            kvs[c, hh] = _dot_tn(kd_c[:, hh * GLA_DK:(hh + 1) * GLA_DK],
                                 v_b[chunk_rows[c], hh * GLA_DV:(hh + 1) * GLA_DV])
    front_tiles(2)

    for hh in range(GLA_HEADS):
        st = st_ref[hh] if chained else None
        for c in range(n_chunks):
            s_old = st if chained else st0_ref[c, hh]
            s_new = s_old * dcol[hh * GLA_DK:(hh + 1) * GLA_DK, c:c + 1] + kvs[c, hh]
            if not chained:
                st_out_ref[c, hh] = s_new
            else:
                st = s_new
                if c + 1 < n_chunks:
                    put_state(c + 1, hh, s_new)
        if chained:
            st_ref[hh] = st

    for c in range(n_chunks):
        for gg in range(n_grp):
            p = jnp.where(causal, scores[c, gg], 0.0).astype(BF16)
            o_ref[chunk_rows[c], grp_v[gg]] = (
                _dot(p, vbd_ref[c, gg]) + _dot(qi_b[chunk_rows[c], grp_k[gg]], sbd_ref[c, gg]))

    o = o_ref[...]
    ons = []
    for hh in range(GLA_HEADS):
        oh = o[:, hh * GLA_DV:(hh + 1) * GLA_DV]
        ons.append(oh * lax.rsqrt(jnp.mean(oh * oh, axis=-1, keepdims=True) + EPS))
    og = jnp.concatenate(ons, axis=-1) * gnorm_ref[...] * (r * jax.nn.sigmoid(r))
    mix = _dot(jnp.concatenate([pool_y, og], axis=-1).astype(BF16), wout_ref[...])
    front_tiles(3)

    h = x + mix
    h_ref[...] = h
    hn = _rms(h, nffn_ref[...]).astype(BF16)
    _pack_planes(hn, hn_ref)

    logits = _dot_nt(wr_ref[...], hn) + br_ref[...]
    front_tiles(4)
    sub = lax.broadcasted_iota(I32, (EXPERTS_PER_GROUP, tb), 0).astype(F32)
    neg = jnp.float32(-jnp.inf)
    big = jnp.float32(EXPERTS_PER_GROUP)
    tile0 = logits[0:EXPERTS_PER_GROUP]
    is_g = sub < N_GROUPS
    gmax = jnp.max(jnp.where(is_g, tile0, neg), axis=0, keepdims=True)
    gsum = jnp.sum(jnp.where(is_g, jnp.exp(tile0 - gmax), 0.0), axis=0, keepdims=True)
    p_g = 1.0 / gsum
    gidx = jnp.min(jnp.where(is_g & (tile0 == gmax), sub, big), axis=0, keepdims=True)
    el = logits[N_GROUPS * EXPERTS_PER_GROUP:(N_GROUPS + 1) * EXPERTS_PER_GROUP]
    for g in range(N_GROUPS - 2, -1, -1):
        el = jnp.where(gidx == g, logits[(g + 1) * EXPERTS_PER_GROUP:(g + 2) * EXPERTS_PER_GROUP], el)
    m1 = jnp.max(el, axis=0, keepdims=True)
    i1 = jnp.min(jnp.where(el == m1, sub, big), axis=0, keepdims=True)
    rest = sub != i1
    m2 = jnp.max(jnp.where(rest, el, neg), axis=0, keepdims=True)
    i2 = jnp.min(jnp.where(rest & (el == m2), sub, big), axis=0, keepdims=True)
    t2 = jnp.exp(m2 - m1)
    den = 1.0 + t2
    g1 = p_g / den
    g2 = p_g * t2 / den
    e1 = gidx * EXPERTS_PER_GROUP + i1
    e2 = gidx * EXPERTS_PER_GROUP + i2

    eid = lax.broadcasted_iota(I32, (N_EXPERTS, tb), 0).astype(F32)
    oh1 = eid == e1
    oh2 = eid == e2
    both = jnp.where(oh1 | oh2, 1.0, 0.0)
    cnt = cnt_ref[...]
    before = _dot(both.astype(BF16), sup_ref[...]) + cnt
    front_tiles(5)
    assert tiles_done[0] * FRONT_TILE == D_MAIN
    pos1 =jnp.sum(jnp.where(oh1, before, 0.0), axis=0, keepdims=True)
    pos2 = jnp.sum(jnp.where(oh2, before, 0.0), axis=0, keepdims=True)
    cnt_new = cnt + jnp.sum(both, axis=1, keepdims=True)
    cnt_ref[...] = cnt_new
    cnt_out_ref[...] = cnt_new

    zero = jnp.zeros_like(e1)
    route_t = jnp.concatenate([e1, e2, g1, g2, pos1, pos2, zero, zero], axis=0)
    route_t_ref[0] = route_t
    route_ref[...] = jnp.concatenate([route_t, jnp.zeros((LANES - ROUTE_ROWS, tb), F32)], axis=0).T

    if chained:
        @pl.when(j == nj - 1)
        def _():
            pool_out_ref[0] = ext_ref[0:POOL_ROWS, :]
            st_out_ref[0] = st_ref[...]


def _mixer(x2d, pool0, st0, cnt0, weights, *, batch, seq, tb, lead_pad):
    chained = tb <= seq
    total_rows = batch * seq
    n_blk = total_rows // tb
    nj = seq // tb if chained else 1
    per_blk = 1 if chained else tb // seq
    assert chained or (seq == CHUNK and batch % per_blk == 0)
    shared = pool0.shape[0] == 1
    front = lambda s: jnp.minimum(s, n_blk - 1)
    back = lambda s: jnp.maximum(s - 1, 0)
    stream = lambda s: back(s) // nj
    st_idx = (lambda s: (0, 0, 0)) if shared else (lambda s: (stream(s), 0, 0))
    gla_idx = (lambda s: (0, 0, 0, 0)) if shared else (lambda s: (stream(s), 0, 0, 0))
    const2 = lambda s: (0, 0)
    tok_out = lambda s: (back(s), 0)

    seg = min(tb, MXU_DIM)
    ii = jnp.arange(seg)
    tril = ((ii[:, None] >= ii[None, :]) & (ii[:, None] // CHUNK == ii[None, :] // CHUNK)).astype(BF16)
    ii = jnp.arange(tb)
    sup = (ii[:, None] < ii[None, :]).astype(BF16)

    in_specs = [
        pl.BlockSpec((tb, D_MODEL), lambda s: (front(s), 0)),
        pl.BlockSpec((per_blk, POOL_ROWS, D_POOL), st_idx),
        pl.BlockSpec((per_blk, GLA_HEADS, GLA_DK, GLA_DV), gla_idx),
        pl.BlockSpec((N_EXPERTS, 1), const2),
        pl.BlockSpec((seg, seg), const2),
        pl.BlockSpec((tb, tb), const2),
    ]
    for wgt in weights:
        in_specs.append(pl.BlockSpec(wgt.shape, (lambda s, n=wgt.ndim: (0,) * n)))
    args = [x2d, pool0, st0, cnt0, tril, sup, *weights]

    out_shape = [
        jax.ShapeDtypeStruct((total_rows, D_MODEL), F32),
        jax.ShapeDtypeStruct((HN_PLANES, total_rows, LANES), I32),
        jax.ShapeDtypeStruct((total_rows, LANES), F32),
        jax.ShapeDtypeStruct((n_blk, ROUTE_ROWS, tb), F32),
        jax.ShapeDtypeStruct((batch, POOL_ROWS, D_POOL), F32),
        jax.ShapeDtypeStruct((batch, GLA_HEADS, GLA_DK, GLA_DV), F32),
        jax.ShapeDtypeStruct((N_EXPERTS, 1), F32),
    ]
    out_specs = [
        pl.BlockSpec((tb, D_MODEL), tok_out),
        pl.BlockSpec((HN_PLANES, tb, LANES), lambda s: (0, back(s), 0)),
        pl.BlockSpec((tb, LANES), tok_out),
        pl.BlockSpec((1, ROUTE_ROWS, tb), lambda s: (back(s), 0, 0)),
        pl.BlockSpec((per_blk, POOL_ROWS, D_POOL), lambda s: (stream(s), 0, 0)),
        pl.BlockSpec((per_blk, GLA_HEADS, GLA_DK, GLA_DV), lambda s: (stream(s), 0, 0, 0)),
        pl.BlockSpec((N_EXPERTS, 1), const2),
    ]
    n_grp = GLA_HEADS // HEAD_GROUP
    scratch = [
        pltpu.VMEM((POOL_ROWS + tb, D_POOL), F32),
        pltpu.VMEM((GLA_HEADS, GLA_DK, GLA_DV), F32),
        pltpu.VMEM((tb // CHUNK, n_grp, HEAD_GROUP * CHUNK, HEAD_GROUP * GLA_DK), BF16),
        pltpu.VMEM((tb // CHUNK, n_grp, HEAD_GROUP * CHUNK, HEAD_GROUP * GLA_DV), BF16),
        pltpu.VMEM((tb // CHUNK, n_grp, HEAD_GROUP * GLA_DK, HEAD_GROUP * GLA_DV), BF16),
        pltpu.VMEM((tb, D_V), F32),
        pltpu.VMEM((N_EXPERTS, 1), F32),
        pltpu.VMEM((2, tb, D_MODEL), F32),
        pltpu.VMEM((2, tb, D_MAIN), F32),
        pltpu.VMEM((2, GATE_RANK, tb), F32),
    ]
    return pl.pallas_call(
        functools.partial(_mixer_kernel, tb=tb, nj=nj, lead_pad=lead_pad, chained=chained),
        grid=(n_blk + 1,),
        in_specs=in_specs,
        out_specs=out_specs,
        out_shape=out_shape,
        scratch_shapes=scratch,
        compiler_params=pltpu.CompilerParams(
            dimension_semantics=("arbitrary",), vmem_limit_bytes=VMEM_LIMIT),
        name=f"mixer_tb{tb}_pad{lead_pad}",
    )(*args)


def _sc_mesh():
    return plsc.VectorSubcoreMesh(core_axis_name="c", subcore_axis_name="s",
                                  num_cores=SC_CORES, num_subcores=SC_SUBCORES)


def _sc_worker():
    return lax.axis_index("s") * SC_CORES + lax.axis_index("c")


def _plane_rows(dest, planes, rows_per_plane):
    offs = (jnp.arange(planes, dtype=I32) * rows_per_plane)[None, :, None]
    return dest[:, None, :] + offs


def _dispatch(hn_p, hn_s, dest0, dest1, n_rows):
    planes, t_p, _ = hn_p.shape
    t_s = hn_s.shape[1]
    n_cp = t_p // SC_ROWS
    cp = n_cp // SC_WORKERS
    n_cs = t_s // SC_ROWS
    assert t_p == cp * SC_ROWS * SC_WORKERS and t_s == n_cs * SC_ROWS and n_cs <= SC_WORKERS
    idx0 = _plane_rows(dest0, planes, n_rows)
    idx1 = _plane_rows(dest1, planes, n_rows)

    def body(hnp_hbm, hns_hbm, d0_hbm, d1_hbm, xs_hbm, rows_v, i0_v, i1_v, is0_v, is1_v, sem_in, sem_out):
        wid = _sc_worker()
        pltpu.sync_copy(d0_hbm.at[pl.ds(wid * cp, cp)], i0_v)
        pltpu.sync_copy(d1_hbm.at[pl.ds(wid * cp, cp)], i1_v)

        def move(src_hbm, src_rows, row0, i0, i1, c):
            loads = [pltpu.async_copy(src_hbm.at[pl.ds(p * src_rows + row0, SC_ROWS)], rows_v.at[p], sem_in)
                     for p in range(planes)]
            for cpy in loads:
                cpy.wait()
            stores = []
            for p in range(planes):
                stores.append(pltpu.async_copy(rows_v.at[p], xs_hbm.at[i0.at[c, p]], sem_out))
                stores.append(pltpu.async_copy(rows_v.at[p], xs_hbm.at[i1.at[c, p]], sem_out))
            for cpy in stores:
                cpy.wait()

        @pl.loop(0, cp)
        def _(c):
            move(hnp_hbm, t_p, (wid * cp + c) * SC_ROWS, i0_v, i1_v, c)

        @pl.when(wid < n_cs)
        def _():
            pltpu.sync_copy(d0_hbm.at[pl.ds(n_cp + wid, 1)], is0_v)
            pltpu.sync_copy(d1_hbm.at[pl.ds(n_cp + wid, 1)], is1_v)
            move(hns_hbm, t_s, wid * SC_ROWS, is0_v, is1_v, 0)

    xs = pl.kernel(
        body,
        out_type=jax.ShapeDtypeStruct((planes * n_rows, LANES), I32),
        mesh=_sc_mesh(),
        scratch_types=[
            pltpu.VMEM((planes, SC_ROWS, LANES), I32),
            pltpu.VMEM((cp, planes, SC_ROWS), I32),
            pltpu.VMEM((cp, planes, SC_ROWS), I32),
            pltpu.VMEM((1, planes, SC_ROWS), I32),
            pltpu.VMEM((1, planes, SC_ROWS), I32),
            pltpu.SemaphoreType.DMA,
            pltpu.SemaphoreType.DMA,
        ],
        name="moe_dispatch_sc",
    )(hn_p.reshape(planes * t_p, LANES), hn_s.reshape(planes * t_s, LANES), idx0, idx1)
    return xs.reshape(planes, n_rows, LANES)


def _gather(ys, dest0, dest1):
    planes, n_rows, _ = ys.shape
    n_chunks = dest0.shape[0]
    n_tok = n_chunks * SC_ROWS
    cpw = max(n_chunks // SC_WORKERS, 1)
    assert n_chunks <= SC_WORKERS or n_chunks == cpw * SC_WORKERS
    idx = (_plane_rows(dest0, planes, n_rows), _plane_rows(dest1, planes, n_rows))

    def body(ys_hbm, d0_hbm, d1_hbm, out_hbm, rows_v, i0_v, i1_v, sem_in, sem_out):
        wid = _sc_worker()

        def work():
            pltpu.sync_copy(d0_hbm.at[pl.ds(wid * cpw, cpw)], i0_v)
            pltpu.sync_copy(d1_hbm.at[pl.ds(wid * cpw, cpw)], i1_v)

            @pl.loop(0, cpw)
            def _(c):
                row0 = (wid * cpw + c) * SC_ROWS
                for j, i_v in enumerate((i0_v, i1_v)):
                    loads = [pltpu.async_copy(ys_hbm.at[i_v.at[c, p]], rows_v.at[p], sem_in)
                             for p in range(planes)]
                    for cpy in loads:
                        cpy.wait()
                    stores = [
                        pltpu.async_copy(
                            rows_v.at[p], out_hbm.at[pl.ds((j * planes + p) * n_tok + row0, SC_ROWS)], sem_out)
                        for p in range(planes)]
                    for cpy in stores:
                        cpy.wait()

        if n_chunks < SC_WORKERS:
            pl.when(wid < n_chunks)(work)
        else:
            work()

    out = pl.kernel(
        body,
        out_type=jax.ShapeDtypeStruct((2 * planes * n_tok, LANES), ys.dtype),
        mesh=_sc_mesh(),
        scratch_types=[
            pltpu.VMEM((planes, SC_ROWS, LANES), ys.dtype),
            pltpu.VMEM((cpw, planes, SC_ROWS), I32),
            pltpu.VMEM((cpw, planes, SC_ROWS), I32),
            pltpu.SemaphoreType.DMA,
            pltpu.SemaphoreType.DMA,
        ],
        name="moe_gather_sc",
    )(ys.reshape(planes * n_rows, LANES), *idx)
    return out.reshape(2, planes, n_tok, LANES)


def _expert_kernel(first_ref, nblk_ref, bv_ref, nb_ref, xs_hbm, wg_ref, wu_ref, wd_ref, ys_hbm,
                   wgu_s, wd_s, xbuf, ybuf, sem_in, sem_out, *, n_blocks):
    e = pl.program_id(0)
    n = nblk_ref[e]
    base = first_ref[e]

    def rows(blk):
        return pl.ds(blk * MOE_BLK, MOE_BLK)

    def in_copy(blk, slot):
        return pltpu.make_async_copy(xs_hbm.at[:, rows(blk), :], xbuf.at[slot], sem_in.at[slot])

    def out_copy(blk, slot):
        return pltpu.make_async_copy(ybuf.at[slot], ys_hbm.at[:, rows(blk), :], sem_out.at[slot])

    @pl.when(n > 0)
    def _():
        in_copy(base, 0).start()
        wgu_s[:, 0:D_EXPERT] = wg_ref[0].astype(BF16)
        wgu_s[:, D_EXPERT:2 * D_EXPERT] = wu_ref[0].astype(BF16)
        wd_s[...] = wd_ref[0].astype(BF16)

    def block(b, carry):
        slot = lax.rem(b, 2)
        in_copy(base + b, slot).wait()

        @pl.when(b + 1 < n)
        def _():
            in_copy(base + b + 1, 1 - slot).start()

        @pl.when(b >= 2)
        def _():
            out_copy(base + b - 2, slot).wait()

        valid = lax.broadcasted_iota(I32, (MOE_BLK, LANES), 0) < bv_ref[base + b]
        xb = _unpack_planes([jnp.where(valid, xbuf[slot, p], 0) for p in range(HN_PLANES)]).astype(BF16)
        gu = _dot(xb, wgu_s[...])
        gate = gu[:, 0:D_EXPERT]
        hmid = gate * jax.nn.sigmoid(gate) * gu[:, D_EXPERT:]
        _pack_planes(_dot(hmid.astype(BF16), wd_s[...]), ybuf.at[slot])
        out_copy(base + b, slot).start()
        return carry

    lax.fori_loop(0, n, block, 0)

    @pl.when(n >= 2)
    def _():
        out_copy(base + n - 2, lax.rem(n, 2)).wait()

    @pl.when(n >= 1)
    def _():
        out_copy(base + n - 1, lax.rem(n - 1, 2)).wait()

    @pl.when(e == pl.num_programs(0) - 1)
    def _():
        ybuf[0] = jnp.zeros((HN_PLANES, MOE_BLK, LANES), I32)

        def fill(blk, carry):
            cp = out_copy(blk, 0)
            cp.start()
            cp.wait()
            return carry

        lax.fori_loop(nb_ref[0], n_blocks, fill, 0)


def _experts(first_blk, n_blk, block_valid, nb, xs, w_eg, w_eu, w_ed):
    n_rows = xs.shape[1]
    n_blocks = n_rows // MOE_BLK
    w_idx = lambda e, *_: (e, 0, 0)
    grid_spec = pltpu.PrefetchScalarGridSpec(
        num_scalar_prefetch=4,
        grid=(N_EXPERTS,),
        in_specs=[
            pl.BlockSpec(memory_space=pl.ANY),
            pl.BlockSpec((1, D_MODEL, D_EXPERT), w_idx),
            pl.BlockSpec((1, D_MODEL, D_EXPERT), w_idx),
            pl.BlockSpec((1, D_EXPERT, D_MODEL), w_idx),
        ],
        out_specs=pl.BlockSpec(memory_space=pl.ANY),
        scratch_shapes=[
            pltpu.VMEM((D_MODEL, 2 * D_EXPERT), BF16),
            pltpu.VMEM((D_EXPERT, D_MODEL), BF16),
            pltpu.VMEM((2, HN_PLANES, MOE_BLK, LANES), I32),
            pltpu.VMEM((2, HN_PLANES, MOE_BLK, LANES), I32),
            pltpu.SemaphoreType.DMA((2,)),
            pltpu.SemaphoreType.DMA((2,)),
        ],
    )
    return pl.pallas_call(
        functools.partial(_expert_kernel, n_blocks=n_blocks),
        grid_spec=grid_spec,
        out_shape=jax.ShapeDtypeStruct((HN_PLANES, n_rows, LANES), I32),
        compiler_params=pltpu.CompilerParams(
            dimension_semantics=("arbitrary",), vmem_limit_bytes=VMEM_LIMIT),
        name="moe_experts",
    )(first_blk, n_blk, block_valid, nb, xs, w_eg, w_eu, w_ed)


def _combine_kernel(h_ref, route_ref, nfin_ref, rows_ref, *rest):
    y_ref = rest[-1]
    route = route_ref[...]
    ys1 = _unpack_planes([rows_ref[0, p] for p in range(HN_PLANES)])
    ys2 = _unpack_planes([rows_ref[1, p] for p in range(HN_PLANES)])
    out = h_ref[...] + (ys1 * route[:, 2:3] + ys2 * route[:, 3:4])
    y_ref[...] = _rms(out, nfin_ref[...])


def _combine(h, route, norm_final, rows, *, row0=0, y_prev=None):
    total = h.shape[0]
    tb = COMBINE_TB
    blk0 = row0 // tb
    args = [h, route, norm_final, rows]
    in_specs = [
        pl.BlockSpec((tb, D_MODEL), lambda i: (blk0 + i, 0)),
        pl.BlockSpec((tb, LANES), lambda i: (blk0 + i, 0)),
        pl.BlockSpec((1, D_MODEL), lambda i: (0, 0)),
        pl.BlockSpec((2, HN_PLANES, tb, LANES), lambda i: (0, 0, i, 0)),
    ]
    aliases = {}
    if y_prev is not None:
        aliases[len(args)] = 0
        args.append(y_prev)
        in_specs.append(pl.BlockSpec(memory_space=pl.ANY))
    return pl.pallas_call(
        _combine_kernel,
        grid=(rows.shape[2] // tb,),
        in_specs=in_specs,
        out_specs=pl.BlockSpec((tb, D_MODEL), lambda i: (blk0 + i, 0)),
        out_shape=jax.ShapeDtypeStruct((total, D_MODEL), F32),
        input_output_aliases=aliases,
        compiler_params=pltpu.CompilerParams(
            dimension_semantics=("arbitrary",), vmem_limit_bytes=VMEM_LIMIT),
        name="moe_combine",
    )(*args)


def kernel(x_prompt, x_sample, state_pool, state_gla, meta_tokens, norm_mix, w_in, w_gate_up, b_gate, w_pool, pool_scale, gla_norm, w_out, norm_ffn, w_router_group, b_router_group, w_router_expert, b_router_expert, w_expert_gate, w_expert_up, w_expert_down, norm_final):
    assert w_in.shape[0] == 1, "one encoder layer"
    batch, seq, _ = x_prompt.shape
    dec_batch, dec_seq, _ = x_sample.shape
    assert seq % MIX_TB == 0 and dec_seq == CHUNK and N_META <= CHUNK
    t_prompt = batch * seq
    t_sample = dec_batch * dec_seq
    t_all = t_prompt + t_sample
    assert t_prompt % COMBINE_TB == 0 and t_sample % COMBINE_TB == 0

    w_in0 = w_in[0]
    gpad = EXPERTS_PER_GROUP - N_GROUPS
    rpad = ROUTER_ROWS - EXPERTS_PER_GROUP - N_EXPERTS
    w_router = jnp.concatenate([
        w_router_group[0].T, jnp.zeros((gpad, D_MODEL), F32),
        w_router_expert[0].T, jnp.zeros((rpad, D_MODEL), F32)], axis=0)
    b_router = jnp.concatenate([
        b_router_group[0], jnp.zeros((gpad,), F32), b_router_expert[0], jnp.zeros((rpad,), F32)])
    zg = jnp.zeros((POOL_GROUP_DIM, POOL_GROUP_DIM), F32)
    w_pool_pairs = jnp.stack([
        jnp.block([[w_pool[0, 2 * i], zg], [zg, w_pool[0, 2 * i + 1]]]) for i in range(len(POOL_WINDOWS) // 2)])
    weights = (
        norm_mix[0][None, :],
        w_in0[:, :D_MAIN].astype(BF16),
        w_in0[:, D_MAIN:].T.astype(BF16),
        w_gate_up[0].astype(BF16),
        b_gate[0][None, :],
        w_pool_pairs.astype(BF16),
        pool_scale[0][None, :],
        gla_norm[0][None, :],
        w_out[0].astype(BF16),
        norm_ffn[0][None, :],
        w_router.astype(BF16),
        b_router[:, None],
    )

    zero_cnt = jnp.zeros((N_EXPERTS, 1), F32)
    x_meta = jnp.pad(meta_tokens.astype(F32), ((CHUNK - N_META, 0), (0, 0)))
    meta = _mixer(x_meta, jnp.zeros((1, POOL_ROWS, D_POOL), F32),
                  jnp.zeros((1, GLA_HEADS, GLA_DK, GLA_DV), F32), zero_cnt,
                  weights, batch=1, seq=CHUNK, tb=CHUNK, lead_pad=CHUNK - N_META)
    h_m, hn_m, route_m, rt_m, pool_m, st_m, cnt_m = meta
    del h_m, hn_m, route_m, rt_m, cnt_m
    h_p, hn_p, route_p, rt_p, pool_p, st_p, cnt_p = _mixer(
        x_prompt.reshape(t_prompt, D_MODEL), pool_m, st_m, zero_cnt, weights,
        batch=batch, seq=seq, tb=MIX_TB, lead_pad=0)
    pool_s0 = jnp.pad(state_pool[0], ((0, 0), (POOL_ROWS - POOL_PAD, 0), (0, 0)))
    h_s, hn_s, route_s, rt_s, pool_s, st_s, cnt_s = _mixer(
        x_sample.reshape(t_sample, D_MODEL), pool_s0, state_gla[0].astype(F32),
        cnt_p, weights, batch=dec_batch, seq=dec_seq, tb=SAMPLE_TB, lead_pad=0)

    counts = cnt_s[:, 0].astype(I32)
    padded = (counts + MOE_BLK - 1) // MOE_BLK * MOE_BLK
    ends = jnp.cumsum(padded)
    pstart = ends - padded
    n_blocks = (2 * t_all + N_EXPERTS * (MOE_BLK - 1) + MOE_BLK - 1) // MOE_BLK
    nb = (ends[-1] // MOE_BLK).astype(I32)
    blk_ids = jnp.minimum(jnp.arange(n_blocks, dtype=I32), nb - 1)
    block_e = jnp.sum((ends[None, :] <= (blk_ids * MOE_BLK)[:, None]).astype(I32), axis=1)
    block_e = jnp.minimum(block_e, N_EXPERTS - 1)
    owner = block_e[:, None] == jnp.arange(N_EXPERTS, dtype=I32)
    row_end = jnp.sum(jnp.where(owner, pstart + counts, 0), axis=1)
    block_valid = jnp.clip(row_end - blk_ids * MOE_BLK, 0, MOE_BLK)

    def dest_rows(rt):
        rt = rt.transpose(1, 0, 2).reshape(ROUTE_ROWS, -1)
        onehot = rt[0:2].astype(I32)[..., None] == jnp.arange(N_EXPERTS, dtype=I32)
        return jnp.sum(jnp.where(onehot, pstart, 0), axis=-1) + rt[4:6].astype(I32)

    dest = jnp.concatenate([dest_rows(rt_p), dest_rows(rt_s)], axis=1)
    dest0 = dest[0].reshape(t_all // SC_ROWS, SC_ROWS)
    dest1 = dest[1].reshape(t_all // SC_ROWS, SC_ROWS)
    xs = _dispatch(hn_p, hn_s, dest0, dest1, n_blocks * MOE_BLK)
    ys = _experts((pstart // MOE_BLK).astype(I32), (padded // MOE_BLK).astype(I32),
                  block_valid.astype(I32), nb[None], xs,
                  w_expert_gate[0], w_expert_up[0], w_expert_down[0])
    nfin = norm_final[None, :]
    cp_chunks = t_prompt // SC_ROWS
    part = cp_chunks // COMBINE_PARTS
    y_prompt = None
    for i in range(COMBINE_PARTS):
        ch = slice(i * part, (i + 1) * part)
        rows_i = _gather(ys, dest0[ch], dest1[ch])
        y_prompt = _combine(h_p, route_p, nfin, rows_i, row0=i * part * SC_ROWS, y_prev=y_prompt)
    y_prompt = y_prompt.reshape(batch, seq, D_MODEL)
    rows_s = _gather(ys, dest0[cp_chunks:], dest1[cp_chunks:])
    y_sample = _combine(h_s, route_s, nfin, rows_s).reshape(dec_batch, dec_seq, D_MODEL)
    new_pool_prompt = pool_p[:, POOL_ROWS - POOL_PAD:][None]
    new_gla_prompt = st_p[None]
    new_pool_sample = pool_s[:, POOL_ROWS - POOL_PAD:][None]
    new_gla_sample = st_s[None]
    return (y_prompt, y_sample, new_pool_prompt, new_gla_prompt, new_pool_sample, new_gla_sample)
```

```python
import functools

import jax
import jax.numpy as jnp
from jax import lax
from jax.experimental import pallas as pl
from jax.experimental.pallas import tpu as pltpu
from jax.experimental.pallas import tpu_sc as plsc

F32 = jnp.float32
BF16 = jnp.bfloat16
U32 = jnp.uint32
I32 = jnp.int32

D_MODEL = 1024
N_META = 16
CHUNK = 64
EPS = 1e-6
D_POOL = 512
POOL_WINDOWS = (2, 4, 8, 16)
POOL_GROUP_DIM = 128
POOL_PAD = 15
POOL_ROWS = 16
GLA_HEADS = 4
GLA_DK = 64
GLA_DV = 128
D_QK = 256
D_V = 512
HEAD_GROUP = 2
GATE_RANK = 16
GATE_TAU = 16.0
D_MAIN = D_POOL + 2 * D_QK + 2 * D_V
N_GROUPS = 4
EXPERTS_PER_GROUP = 8
N_EXPERTS = 32
D_EXPERT = 512

LANES = 128
MXU_DIM = 256
HALF = D_MODEL // 2
HN_PLANES = HALF // LANES
ROUTE_ROWS = 8
ROUTER_ROWS = 64
MIX_TB = 512
SAMPLE_TB = 256
FRONT_TILE = 256
FRONT_PLAN = (2, 0, 2, 2, 2, 0)
MOE_BLK = 512
ROW_DMA_PRIORITY = 1
COMBINE_TB = 512
COMBINE_PARTS = 4
VMEM_LIMIT = 48 * 1024 * 1024
SC_CORES = 2
SC_SUBCORES = 16
SC_WORKERS = SC_CORES * SC_SUBCORES
SC_ROWS = 128


def _rms(x, g):
    return x * lax.rsqrt(jnp.mean(x * x, axis=-1, keepdims=True) + EPS) * g


def _dot(a, b):
    return jnp.dot(a, b, preferred_element_type=F32)


def _dot_nt(a, b):
    return lax.dot_general(a, b, (((1,), (1,)), ((), ())), preferred_element_type=F32)


def _pack_planes(x, ref):
    xb = x.astype(BF16)
    lo = lax.bitcast_convert_type(xb[:, :HALF].astype(F32), U32) >> 16
    hi = lax.bitcast_convert_type(xb[:, HALF:].astype(F32), U32) & jnp.uint32(0xFFFF0000)
    packed = lax.bitcast_convert_type(lo | hi, I32)
    for p in range(HN_PLANES):
        ref[p] = packed[:, p * LANES:(p + 1) * LANES]


def _unpack_planes(planes):
    words = [lax.bitcast_convert_type(p, U32) for p in planes]
    los = [lax.bitcast_convert_type(w << 16, F32) for w in words]
    his = [lax.bitcast_convert_type(w & jnp.uint32(0xFFFF0000), F32) for w in words]
    return jnp.concatenate(los + his, axis=-1)


def _dot_tn(a, b):
    return lax.dot_general(a, b, (((0,), (0,)), ((), ())), preferred_element_type=F32)


def _mixer_kernel(x_ref, pool0_ref, st0_ref, cnt0_ref, tril_ref, sup_ref,
                  nmix_ref, wmain_ref, wz_ref, wgu_ref, bgate_ref, wpool_ref, pscale_ref,
                  gnorm_ref, wout_ref, nffn_ref, wr_ref, br_ref,
                  h_ref, hn_ref, route_ref, route_t_ref, pool_out_ref, st_out_ref, cnt_out_ref,
                  ext_ref, st_ref, kbd_ref, vbd_ref, sbd_ref, o_ref, cnt_ref,
                  xs_ref, proj_ref, z_ref, *, tb, nj, lead_pad, chained):
    s = pl.program_id(0)
    back = jnp.maximum(s - 1, 0)
    j = lax.rem(back, nj)
    n_chunks = tb // CHUNK
    wr_slot = lax.rem(s, 2)
    rd_slot = 1 - wr_slot

    @pl.when(s == 0)
    def _():
        kbd_ref[...] = jnp.zeros_like(kbd_ref)
        vbd_ref[...] = jnp.zeros_like(vbd_ref)
        sbd_ref[...] = jnp.zeros_like(sbd_ref)
        xs_ref[1] = jnp.zeros((tb, D_MODEL), F32)
        proj_ref[1] = jnp.zeros((tb, D_MAIN), F32)
        z_ref[1] = jnp.zeros((GATE_RANK, tb), F32)

    @pl.when(s <= 1)
    def _():
        cnt_ref[...] = cnt0_ref[...]

    def put_state(c, hh, st):
        gg, hp = divmod(hh, HEAD_GROUP)
        sbd_ref[c, gg, hp * GLA_DK:(hp + 1) * GLA_DK, hp * GLA_DV:(hp + 1) * GLA_DV] = st.astype(BF16)

    if chained:
        @pl.when(j == 0)
        def _():
            ext_ref[0:POOL_ROWS, :] = pool0_ref[0]
            st_ref[...] = st0_ref[0]

        for hh in range(GLA_HEADS):
            put_state(0, hh, st_ref[hh])
    else:
        for c in range(n_chunks):
            for hh in range(GLA_HEADS):
                put_state(c, hh, st0_ref[c, hh])

    x_new = x_ref[...]
    xn = _rms(x_new, nmix_ref[...]).astype(BF16)
    xs_ref[wr_slot] = x_new

    tiles_done = [0]

    def front_tiles(stage):
        for t in range(tiles_done[0], tiles_done[0] + FRONT_PLAN[stage]):
            cols = slice(t * FRONT_TILE, (t + 1) * FRONT_TILE)
            proj_ref[wr_slot, :, cols] = _dot(xn, wmain_ref[:, cols])
        tiles_done[0] += FRONT_PLAN[stage]

    front_tiles(0)
    z_ref[wr_slot] = _dot_nt(wz_ref[...], xn)

    x = xs_ref[rd_slot]
    z = z_ref[rd_slot]
    u = proj_ref[rd_slot, :, 0:D_POOL]
    q = proj_ref[rd_slot, :, D_POOL:D_POOL + D_QK]
    k = proj_ref[rd_slot, :, D_POOL + D_QK:D_POOL + 2 * D_QK]
    v = proj_ref[rd_slot, :, D_POOL + 2 * D_QK:D_POOL + 2 * D_QK + D_V]
    r = proj_ref[rd_slot, :, D_POOL + 2 * D_QK + D_V:D_MAIN]

    row = lax.broadcasted_iota(I32, (tb, 1), 0)

    pseg = POOL_ROWS + CHUNK
    if chained:
        ext_ref[POOL_ROWS:POOL_ROWS + tb, :] = u
        ext = ext_ref[...]
    else:
        ext = jnp.concatenate(
            [blk for c in range(n_chunks) for blk in (pool0_ref[c], u[c * CHUNK:(c + 1) * CHUNK])], axis=0)
    pooled = []
    for g, w in enumerate(POOL_WINDOWS):
        sl = slice(g * POOL_GROUP_DIM, (g + 1) * POOL_GROUP_DIM)
        acc = ext[:, sl]
        for d in range(g + 1):
            acc = acc + pltpu.roll(acc, 1 << d, axis=0)
        if chained:
            win = acc[POOL_ROWS:, :]
        else:
            win = jnp.concatenate([acc[c * pseg + POOL_ROWS:(c + 1) * pseg] for c in range(n_chunks)], axis=0)
        if lead_pad:
            cnt = jnp.clip(row - lead_pad + 1, 1, w).astype(F32)
            pooled.append(win / cnt - u[:, sl])
        else:
            pooled.append(win * (1.0 / w) - u[:, sl])
    pys = [_dot(jnp.concatenate(pooled[2 * i:2 * i + 2], axis=-1).astype(BF16), wpool_ref[i])
           for i in range(len(POOL_WINDOWS) // 2)]
    pool_y = jnp.concatenate(pys, axis=-1) * pscale_ref[...]
    if chained:
        ext_ref[0:POOL_ROWS, :] = ext_ref[tb:tb + POOL_ROWS, :]
    else:
        for c in range(n_chunks):
            pool_out_ref[c] = u[(c + 1) * CHUNK - POOL_ROWS:(c + 1) * CHUNK]

    gpre = _dot_tn(z.astype(BF16), wgu_ref[...]) + bgate_ref[...]
    log_a = jax.nn.log_sigmoid(gpre) * (1.0 / GATE_TAU)
    if lead_pad:
        log_a = jnp.where(row >= lead_pad, log_a, 0.0)
    a_hi = log_a.astype(BF16)
    a_lo = (log_a - a_hi.astype(F32)).astype(BF16)
    tril = tril_ref[...]
    seg = tril.shape[0]
    bcum = jnp.concatenate(
        [_dot(tril, a_hi[r0:r0 + seg]) + _dot(tril, a_lo[r0:r0 + seg]) for r0 in range(0, tb, seg)], axis=0)
    front_tiles(1)
    eb = jnp.exp(bcum)
    qi = q * (GLA_DK ** -0.5) * eb
    ki = k * jnp.exp(-bcum)

    rr = lax.broadcasted_iota(I32, (CHUNK, HEAD_GROUP * CHUNK), 0)
    cc = lax.broadcasted_iota(I32, (CHUNK, HEAD_GROUP * CHUNK), 1)
    causal = (cc % CHUNK) <= rr

    lasts = [eb[(c + 1) * CHUNK - 1:(c + 1) * CHUNK, :] for c in range(n_chunks)]
    dcol = jnp.concatenate(lasts + [jnp.zeros((LANES - n_chunks, D_QK), F32)], axis=0).T

    n_grp = GLA_HEADS // HEAD_GROUP
    chunk_rows = [slice(c * CHUNK, (c + 1) * CHUNK) for c in range(n_chunks)]
    grp_k = [slice(g * HEAD_GROUP * GLA_DK, (g + 1) * HEAD_GROUP * GLA_DK) for g in range(n_grp)]
    grp_v = [slice(g * HEAD_GROUP * GLA_DV, (g + 1) * HEAD_GROUP * GLA_DV) for g in range(n_grp)]
    qi_b = qi.astype(BF16)
    ki_b = ki.astype(BF16)
    v_b = v.astype(BF16)

    scores = {}
    for c in range(n_chunks):
        for hh in range(GLA_HEADS):
            gg, hp = divmod(hh, HEAD_GROUP)
            kbd_ref[c, gg, hp * CHUNK:(hp + 1) * CHUNK, hp * GLA_DK:(hp + 1) * GLA_DK] = (
                ki_b[chunk_rows[c], hh * GLA_DK:(hh + 1) * GLA_DK])
            vbd_ref[c, gg, hp * CHUNK:(hp + 1) * CHUNK, hp * GLA_DV:(hp + 1) * GLA_DV] = (
                v_b[chunk_rows[c], hh * GLA_DV:(hh + 1) * GLA_DV])
        for gg in range(n_grp):
            scores[c, gg] = _dot_nt(qi_b[chunk_rows[c], grp_k[gg]], kbd_ref[c, gg])

    kvs = {}
    for c in range(n_chunks):
        kd_c = (ki[chunk_rows[c]] * lasts[c]).astype(BF16)
        for hh in range(GLA_HEADS):
            kvs[c, hh] = _dot_tn(kd_c[:, hh * GLA_DK:(hh + 1) * GLA_DK],
                                 v_b[chunk_rows[c], hh * GLA_DV:(hh + 1) * GLA_DV])
    front_tiles(2)

    for hh in range(GLA_HEADS):
        st = st_ref[hh] if chained else None
        for c in range(n_chunks):
            s_old = st if chained else st0_ref[c, hh]
            s_new = s_old * dcol[hh * GLA_DK:(hh + 1) * GLA_DK, c:c + 1] + kvs[c, hh]
            if not chained:
                st_out_ref[c, hh] = s_new
            else:
                st = s_new
                if c + 1 < n_chunks:
                    put_state(c + 1, hh, s_new)
        if chained:
            st_ref[hh] = st

    for c in range(n_chunks):
        for gg in range(n_grp):
            p = jnp.where(causal, scores[c, gg], 0.0).astype(BF16)
            o_ref[chunk_rows[c], grp_v[gg]] = (
                _dot(p, vbd_ref[c, gg]) + _dot(qi_b[chunk_rows[c], grp_k[gg]], sbd_ref[c, gg]))

    o = o_ref[...]
    ons = []
    for hh in range(GLA_HEADS):
        oh = o[:, hh * GLA_DV:(hh + 1) * GLA_DV]
        ons.append(oh * lax.rsqrt(jnp.mean(oh * oh, axis=-1, keepdims=True) + EPS))
    og = jnp.concatenate(ons, axis=-1) * gnorm_ref[...] * (r * jax.nn.sigmoid(r))
    mix = _dot(jnp.concatenate([pool_y, og], axis=-1).astype(BF16), wout_ref[...])
    front_tiles(3)

    h = x + mix
    h_ref[...] = h
    hn = _rms(h, nffn_ref[...]).astype(BF16)
    _pack_planes(hn, hn_ref)

    logits = _dot_nt(wr_ref[...], hn) + br_ref[...]
    front_tiles(4)
    sub = lax.broadcasted_iota(I32, (EXPERTS_PER_GROUP, tb), 0).astype(F32)
    neg = jnp.float32(-jnp.inf)
    big = jnp.float32(EXPERTS_PER_GROUP)
    tile0 = logits[0:EXPERTS_PER_GROUP]
    is_g = sub < N_GROUPS
    gmax = jnp.max(jnp.where(is_g, tile0, neg), axis=0, keepdims=True)
    gsum = jnp.sum(jnp.where(is_g, jnp.exp(tile0 - gmax), 0.0), axis=0, keepdims=True)
    p_g = 1.0 / gsum
    gidx = jnp.min(jnp.where(is_g & (tile0 == gmax), sub, big), axis=0, keepdims=True)
    el = logits[N_GROUPS * EXPERTS_PER_GROUP:(N_GROUPS + 1) * EXPERTS_PER_GROUP]
    for g in range(N_GROUPS - 2, -1, -1):
        el = jnp.where(gidx == g, logits[(g + 1) * EXPERTS_PER_GROUP:(g + 2) * EXPERTS_PER_GROUP], el)
    m1 = jnp.max(el, axis=0, keepdims=True)
    i1 = jnp.min(jnp.where(el == m1, sub, big), axis=0, keepdims=True)
    rest = sub != i1
    m2 = jnp.max(jnp.where(rest, el, neg), axis=0, keepdims=True)
    i2 = jnp.min(jnp.where(rest & (el == m2), sub, big), axis=0, keepdims=True)
    t2 = jnp.exp(m2 - m1)
    den = 1.0 + t2
    g1 = p_g / den
    g2 = p_g * t2 / den
    e1 = gidx * EXPERTS_PER_GROUP + i1
    e2 = gidx * EXPERTS_PER_GROUP + i2

    eid = lax.broadcasted_iota(I32, (N_EXPERTS, tb), 0).astype(F32)
    oh1 = eid == e1
    oh2 = eid == e2
    both = jnp.where(oh1 | oh2, 1.0, 0.0)
    cnt = cnt_ref[...]
    before = _dot(both.astype(BF16), sup_ref[...]) + cnt
    front_tiles(5)
    assert tiles_done[0] * FRONT_TILE == D_MAIN
    pos1 =jnp.sum(jnp.where(oh1, before, 0.0), axis=0, keepdims=True)
    pos2 = jnp.sum(jnp.where(oh2, before, 0.0), axis=0, keepdims=True)
    cnt_new = cnt + jnp.sum(both, axis=1, keepdims=True)
    cnt_ref[...] = cnt_new
    cnt_out_ref[...] = cnt_new

    zero = jnp.zeros_like(e1)
    route_t = jnp.concatenate([e1, e2, g1, g2, pos1, pos2, zero, zero], axis=0)
    route_t_ref[0] = route_t
    route_ref[...] = jnp.concatenate([route_t, jnp.zeros((LANES - ROUTE_ROWS, tb), F32)], axis=0).T

    if chained:
        @pl.when(j == nj - 1)
        def _():
            pool_out_ref[0] = ext_ref[0:POOL_ROWS, :]
            st_out_ref[0] = st_ref[...]


def _mixer(x2d, pool0, st0, cnt0, weights, *, batch, seq, tb, lead_pad):
    chained = tb <= seq
    total_rows = batch * seq
    n_blk = total_rows // tb
    nj = seq // tb if chained else 1
    per_blk = 1 if chained else tb // seq
    assert chained or (seq == CHUNK and batch % per_blk == 0)
    shared = pool0.shape[0] == 1
    front = lambda s: jnp.minimum(s, n_blk - 1)
    back = lambda s: jnp.maximum(s - 1, 0)
    stream = lambda s: back(s) // nj
    st_idx = (lambda s: (0, 0, 0)) if shared else (lambda s: (stream(s), 0, 0))
    gla_idx = (lambda s: (0, 0, 0, 0)) if shared else (lambda s: (stream(s), 0, 0, 0))
    const2 = lambda s: (0, 0)
    tok_out = lambda s: (back(s), 0)

    seg = min(tb, MXU_DIM)
    ii = jnp.arange(seg)
    tril = ((ii[:, None] >= ii[None, :]) & (ii[:, None] // CHUNK == ii[None, :] // CHUNK)).astype(BF16)
    ii = jnp.arange(tb)
    sup = (ii[:, None] < ii[None, :]).astype(BF16)

    in_specs = [
        pl.BlockSpec((tb, D_MODEL), lambda s: (front(s), 0)),
        pl.BlockSpec((per_blk, POOL_ROWS, D_POOL), st_idx),
        pl.BlockSpec((per_blk, GLA_HEADS, GLA_DK, GLA_DV), gla_idx),
        pl.BlockSpec((N_EXPERTS, 1), const2),
        pl.BlockSpec((seg, seg), const2),
        pl.BlockSpec((tb, tb), const2),
    ]
    for wgt in weights:
        in_specs.append(pl.BlockSpec(wgt.shape, (lambda s, n=wgt.ndim: (0,) * n)))
    args = [x2d, pool0, st0, cnt0, tril, sup, *weights]

    out_shape = [
        jax.ShapeDtypeStruct((total_rows, D_MODEL), F32),
        jax.ShapeDtypeStruct((HN_PLANES, total_rows, LANES), I32),
        jax.ShapeDtypeStruct((total_rows, LANES), F32),
        jax.ShapeDtypeStruct((n_blk, ROUTE_ROWS, tb), F32),
        jax.ShapeDtypeStruct((batch, POOL_ROWS, D_POOL), F32),
        jax.ShapeDtypeStruct((batch, GLA_HEADS, GLA_DK, GLA_DV), F32),
        jax.ShapeDtypeStruct((N_EXPERTS, 1), F32),
    ]
    out_specs = [
        pl.BlockSpec((tb, D_MODEL), tok_out),
        pl.BlockSpec((HN_PLANES, tb, LANES), lambda s: (0, back(s), 0)),
        pl.BlockSpec((tb, LANES), tok_out),
        pl.BlockSpec((1, ROUTE_ROWS, tb), lambda s: (back(s), 0, 0)),
        pl.BlockSpec((per_blk, POOL_ROWS, D_POOL), lambda s: (stream(s), 0, 0)),
        pl.BlockSpec((per_blk, GLA_HEADS, GLA_DK, GLA_DV), lambda s: (stream(s), 0, 0, 0)),
        pl.BlockSpec((N_EXPERTS, 1), const2),
    ]
    n_grp = GLA_HEADS // HEAD_GROUP
    scratch = [
        pltpu.VMEM((POOL_ROWS + tb, D_POOL), F32),
        pltpu.VMEM((GLA_HEADS, GLA_DK, GLA_DV), F32),
        pltpu.VMEM((tb // CHUNK, n_grp, HEAD_GROUP * CHUNK, HEAD_GROUP * GLA_DK), BF16),
        pltpu.VMEM((tb // CHUNK, n_grp, HEAD_GROUP * CHUNK, HEAD_GROUP * GLA_DV), BF16),
        pltpu.VMEM((tb // CHUNK, n_grp, HEAD_GROUP * GLA_DK, HEAD_GROUP * GLA_DV), BF16),
        pltpu.VMEM((tb, D_V), F32),
        pltpu.VMEM((N_EXPERTS, 1), F32),
        pltpu.VMEM((2, tb, D_MODEL), F32),
        pltpu.VMEM((2, tb, D_MAIN), F32),
        pltpu.VMEM((2, GATE_RANK, tb), F32),
    ]
    return pl.pallas_call(
        functools.partial(_mixer_kernel, tb=tb, nj=nj, lead_pad=lead_pad, chained=chained),
        grid=(n_blk + 1,),
        in_specs=in_specs,
        out_specs=out_specs,
        out_shape=out_shape,
        scratch_shapes=scratch,
        compiler_params=pltpu.CompilerParams(
            dimension_semantics=("arbitrary",), vmem_limit_bytes=VMEM_LIMIT),
        name=f"mixer_tb{tb}_pad{lead_pad}",
    )(*args)


def _sc_mesh():
    return plsc.VectorSubcoreMesh(core_axis_name="c", subcore_axis_name="s",
                                  num_cores=SC_CORES, num_subcores=SC_SUBCORES)


def _sc_worker():
    return lax.axis_index("s") * SC_CORES + lax.axis_index("c")


def _plane_rows(dest, planes, rows_per_plane):
    offs = (jnp.arange(planes, dtype=I32) * rows_per_plane)[None, :, None]
    return dest[:, None, :] + offs


def _dispatch(hn_p, hn_s, dest0, dest1, n_rows):
    planes, t_p, _ = hn_p.shape
    t_s = hn_s.shape[1]
    n_cp = t_p // SC_ROWS
    cp = n_cp // SC_WORKERS
    n_cs = t_s // SC_ROWS
    assert t_p == cp * SC_ROWS * SC_WORKERS and t_s == n_cs * SC_ROWS and n_cs <= SC_WORKERS
    idx0 = _plane_rows(dest0, planes, n_rows)
    idx1 = _plane_rows(dest1, planes, n_rows)

    def body(hnp_hbm, hns_hbm, d0_hbm, d1_hbm, xs_hbm, rows_v, i0_v, i1_v, is0_v, is1_v, sem_in, sem_out):
        wid = _sc_worker()
        pltpu.sync_copy(d0_hbm.at[pl.ds(wid * cp, cp)], i0_v)
        pltpu.sync_copy(d1_hbm.at[pl.ds(wid * cp, cp)], i1_v)

        def move(src_hbm, src_rows, row0, i0, i1, c):
            loads = [pltpu.async_copy(src_hbm.at[pl.ds(p * src_rows + row0, SC_ROWS)], rows_v.at[p], sem_in)
                     for p in range(planes)]
            for cpy in loads:
                cpy.wait()
            stores = []
            for p in range(planes):
                stores.append(pltpu.async_copy(rows_v.at[p], xs_hbm.at[i0.at[c, p]], sem_out))
                stores.append(pltpu.async_copy(rows_v.at[p], xs_hbm.at[i1.at[c, p]], sem_out))
            for cpy in stores:
                cpy.wait()

        @pl.loop(0, cp)
        def _(c):
            move(hnp_hbm, t_p, (wid * cp + c) * SC_ROWS, i0_v, i1_v, c)

        @pl.when(wid < n_cs)
        def _():
            pltpu.sync_copy(d0_hbm.at[pl.ds(n_cp + wid, 1)], is0_v)
            pltpu.sync_copy(d1_hbm.at[pl.ds(n_cp + wid, 1)], is1_v)
            move(hns_hbm, t_s, wid * SC_ROWS, is0_v, is1_v, 0)

    xs = pl.kernel(
        body,
        out_type=jax.ShapeDtypeStruct((planes * n_rows, LANES), I32),
        mesh=_sc_mesh(),
        scratch_types=[
            pltpu.VMEM((planes, SC_ROWS, LANES), I32),
            pltpu.VMEM((cp, planes, SC_ROWS), I32),
            pltpu.VMEM((cp, planes, SC_ROWS), I32),
            pltpu.VMEM((1, planes, SC_ROWS), I32),
            pltpu.VMEM((1, planes, SC_ROWS), I32),
            pltpu.SemaphoreType.DMA,
            pltpu.SemaphoreType.DMA,
        ],
        name="moe_dispatch_sc",
    )(hn_p.reshape(planes * t_p, LANES), hn_s.reshape(planes * t_s, LANES), idx0, idx1)
    return xs.reshape(planes, n_rows, LANES)


def _gather(ys, dest0, dest1):
    planes, n_rows, _ = ys.shape
    n_chunks = dest0.shape[0]
    n_tok = n_chunks * SC_ROWS
    cpw = max(n_chunks // SC_WORKERS, 1)
    assert n_chunks <= SC_WORKERS or n_chunks == cpw * SC_WORKERS
    idx = (_plane_rows(dest0, planes, n_rows), _plane_rows(dest1, planes, n_rows))

    def body(ys_hbm, d0_hbm, d1_hbm, out_hbm, rows_v, i0_v, i1_v, sem_in, sem_out):
        wid = _sc_worker()

        def work():
            pltpu.sync_copy(d0_hbm.at[pl.ds(wid * cpw, cpw)], i0_v)
            pltpu.sync_copy(d1_hbm.at[pl.ds(wid * cpw, cpw)], i1_v)

            @pl.loop(0, cpw)
            def _(c):
                row0 = (wid * cpw + c) * SC_ROWS
                for j, i_v in enumerate((i0_v, i1_v)):
                    loads = [pltpu.async_copy(ys_hbm.at[i_v.at[c, p]], rows_v.at[p], sem_in)
                             for p in range(planes)]
                    for cpy in loads:
                        cpy.wait()
                    stores = [
                        pltpu.async_copy(
                            rows_v.at[p], out_hbm.at[pl.ds((j * planes + p) * n_tok + row0, SC_ROWS)], sem_out)
                        for p in range(planes)]
                    for cpy in stores:
                        cpy.wait()

        if n_chunks < SC_WORKERS:
            pl.when(wid < n_chunks)(work)
        else:
            work()

    out = pl.kernel(
        body,
        out_type=jax.ShapeDtypeStruct((2 * planes * n_tok, LANES), ys.dtype),
        mesh=_sc_mesh(),
        scratch_types=[
            pltpu.VMEM((planes, SC_ROWS, LANES), ys.dtype),
            pltpu.VMEM((cpw, planes, SC_ROWS), I32),
            pltpu.VMEM((cpw, planes, SC_ROWS), I32),
            pltpu.SemaphoreType.DMA,
            pltpu.SemaphoreType.DMA,
        ],
        name="moe_gather_sc",
    )(ys.reshape(planes * n_rows, LANES), *idx)
    return out.reshape(2, planes, n_tok, LANES)


def _expert_kernel(first_ref, nblk_ref, bv_ref, nb_ref, xs_hbm, wg_ref, wu_ref, wd_ref, ys_hbm,
                   wgu_s, wd_s, xbuf, ybuf, sem_in, sem_out, *, n_blocks):
    e = pl.program_id(0)
    n = nblk_ref[e]
    base = first_ref[e]

    def rows(blk):
        return pl.ds(blk * MOE_BLK, MOE_BLK)

    def in_copy(blk, slot):
        return pltpu.make_async_copy(xs_hbm.at[:, rows(blk), :], xbuf.at[slot], sem_in.at[slot])

    def out_copy(blk, slot):
        return pltpu.make_async_copy(ybuf.at[slot], ys_hbm.at[:, rows(blk), :], sem_out.at[slot])

    @pl.when(n > 0)
    def _():
        in_copy(base, 0).start(priority=ROW_DMA_PRIORITY)
        wgu_s[:, 0:D_EXPERT] = wg_ref[0].astype(BF16)
        wgu_s[:, D_EXPERT:2 * D_EXPERT] = wu_ref[0].astype(BF16)
        wd_s[...] = wd_ref[0].astype(BF16)

    def block(b, carry):
        slot = lax.rem(b, 2)
        in_copy(base + b, slot).wait()

        @pl.when(b + 1 < n)
        def _():
            in_copy(base + b + 1, 1 - slot).start(priority=ROW_DMA_PRIORITY)

        @pl.when(b >= 2)
        def _():
            out_copy(base + b - 2, slot).wait()

        valid = lax.broadcasted_iota(I32, (MOE_BLK, LANES), 0) < bv_ref[base + b]
        xb = _unpack_planes([jnp.where(valid, xbuf[slot, p], 0) for p in range(HN_PLANES)]).astype(BF16)
        gu = _dot(xb, wgu_s[...])
        gate = gu[:, 0:D_EXPERT]
        hmid = gate * jax.nn.sigmoid(gate) * gu[:, D_EXPERT:]
        _pack_planes(_dot(hmid.astype(BF16), wd_s[...]), ybuf.at[slot])
        out_copy(base + b, slot).start(priority=ROW_DMA_PRIORITY)
        return carry

    lax.fori_loop(0, n, block, 0)

    @pl.when(n >= 2)
    def _():
        out_copy(base + n - 2, lax.rem(n, 2)).wait()

    @pl.when(n >= 1)
    def _():
        out_copy(base + n - 1, lax.rem(n - 1, 2)).wait()

    @pl.when(e == pl.num_programs(0) - 1)
    def _():
        ybuf[0] = jnp.zeros((HN_PLANES, MOE_BLK, LANES), I32)

        def fill(blk, carry):
            cp = out_copy(blk, 0)
            cp.start()
            cp.wait()
            return carry

        lax.fori_loop(nb_ref[0], n_blocks, fill, 0)


def _experts(first_blk, n_blk, block_valid, nb, xs, w_eg, w_eu, w_ed):
    n_rows = xs.shape[1]
    n_blocks = n_rows // MOE_BLK
    w_idx = lambda e, *_: (e, 0, 0)
    grid_spec = pltpu.PrefetchScalarGridSpec(
        num_scalar_prefetch=4,
        grid=(N_EXPERTS,),
        in_specs=[
            pl.BlockSpec(memory_space=pl.ANY),
            pl.BlockSpec((1, D_MODEL, D_EXPERT), w_idx),
            pl.BlockSpec((1, D_MODEL, D_EXPERT), w_idx),
            pl.BlockSpec((1, D_EXPERT, D_MODEL), w_idx),
        ],
        out_specs=pl.BlockSpec(memory_space=pl.ANY),
        scratch_shapes=[
            pltpu.VMEM((D_MODEL, 2 * D_EXPERT), BF16),
            pltpu.VMEM((D_EXPERT, D_MODEL), BF16),
            pltpu.VMEM((2, HN_PLANES, MOE_BLK, LANES), I32),
            pltpu.VMEM((2, HN_PLANES, MOE_BLK, LANES), I32),
            pltpu.SemaphoreType.DMA((2,)),
            pltpu.SemaphoreType.DMA((2,)),
        ],
    )
    return pl.pallas_call(
        functools.partial(_expert_kernel, n_blocks=n_blocks),
        grid_spec=grid_spec,
        out_shape=jax.ShapeDtypeStruct((HN_PLANES, n_rows, LANES), I32),
        compiler_params=pltpu.CompilerParams(
            dimension_semantics=("arbitrary",), vmem_limit_bytes=VMEM_LIMIT),
        name="moe_experts",
    )(first_blk, n_blk, block_valid, nb, xs, w_eg, w_eu, w_ed)


def _combine_kernel(h_ref, route_ref, nfin_ref, rows_ref, *rest):
    y_ref = rest[-1]
    route = route_ref[...]
    ys1 = _unpack_planes([rows_ref[0, p] for p in range(HN_PLANES)])
    ys2 = _unpack_planes([rows_ref[1, p] for p in range(HN_PLANES)])
    out = h_ref[...] + (ys1 * route[:, 2:3] + ys2 * route[:, 3:4])
    y_ref[...] = _rms(out, nfin_ref[...])


def _combine(h, route, norm_final, rows, *, row0=0, y_prev=None):
    total = h.shape[0]
    tb = COMBINE_TB
    blk0 = row0 // tb
    args = [h, route, norm_final, rows]
    in_specs = [
        pl.BlockSpec((tb, D_MODEL), lambda i: (blk0 + i, 0)),
        pl.BlockSpec((tb, LANES), lambda i: (blk0 + i, 0)),
        pl.BlockSpec((1, D_MODEL), lambda i: (0, 0)),
        pl.BlockSpec((2, HN_PLANES, tb, LANES), lambda i: (0, 0, i, 0)),
    ]
    aliases = {}
    if y_prev is not None:
        aliases[len(args)] = 0
        args.append(y_prev)
        in_specs.append(pl.BlockSpec(memory_space=pl.ANY))
    return pl.pallas_call(
        _combine_kernel,
        grid=(rows.shape[2] // tb,),
        in_specs=in_specs,
        out_specs=pl.BlockSpec((tb, D_MODEL), lambda i: (blk0 + i, 0)),
        out_shape=jax.ShapeDtypeStruct((total, D_MODEL), F32),
        input_output_aliases=aliases,
        compiler_params=pltpu.CompilerParams(
            dimension_semantics=("arbitrary",), vmem_limit_bytes=VMEM_LIMIT),
        name="moe_combine",
    )(*args)


def kernel(x_prompt, x_sample, state_pool, state_gla, meta_tokens, norm_mix, w_in, w_gate_up, b_gate, w_pool, pool_scale, gla_norm, w_out, norm_ffn, w_router_group, b_router_group, w_router_expert, b_router_expert, w_expert_gate, w_expert_up, w_expert_down, norm_final):
    assert w_in.shape[0] == 1, "one encoder layer"
    batch, seq, _ = x_prompt.shape
    dec_batch, dec_seq, _ = x_sample.shape
    assert seq % MIX_TB == 0 and dec_seq == CHUNK and N_META <= CHUNK
    t_prompt = batch * seq
    t_sample = dec_batch * dec_seq
    t_all = t_prompt + t_sample
    assert t_prompt % COMBINE_TB == 0 and t_sample % COMBINE_TB == 0

    w_in0 = w_in[0]
    gpad = EXPERTS_PER_GROUP - N_GROUPS
    rpad = ROUTER_ROWS - EXPERTS_PER_GROUP - N_EXPERTS
    w_router = jnp.concatenate([
        w_router_group[0].T, jnp.zeros((gpad, D_MODEL), F32),
        w_router_expert[0].T, jnp.zeros((rpad, D_MODEL), F32)], axis=0)
    b_router = jnp.concatenate([
        b_router_group[0], jnp.zeros((gpad,), F32), b_router_expert[0], jnp.zeros((rpad,), F32)])
    zg = jnp.zeros((POOL_GROUP_DIM, POOL_GROUP_DIM), F32)
    w_pool_pairs = jnp.stack([
        jnp.block([[w_pool[0, 2 * i], zg], [zg, w_pool[0, 2 * i + 1]]]) for i in range(len(POOL_WINDOWS) // 2)])
    weights = (
        norm_mix[0][None, :],
        w_in0[:, :D_MAIN].astype(BF16),
        w_in0[:, D_MAIN:].T.astype(BF16),
        w_gate_up[0].astype(BF16),
        b_gate[0][None, :],
        w_pool_pairs.astype(BF16),
        pool_scale[0][None, :],
        gla_norm[0][None, :],
        w_out[0].astype(BF16),
        norm_ffn[0][None, :],
        w_router.astype(BF16),
        b_router[:, None],
    )

    zero_cnt = jnp.zeros((N_EXPERTS, 1), F32)
    x_meta = jnp.pad(meta_tokens.astype(F32), ((CHUNK - N_META, 0), (0, 0)))
    meta = _mixer(x_meta, jnp.zeros((1, POOL_ROWS, D_POOL), F32),
                  jnp.zeros((1, GLA_HEADS, GLA_DK, GLA_DV), F32), zero_cnt,
                  weights, batch=1, seq=CHUNK, tb=CHUNK, lead_pad=CHUNK - N_META)
    h_m, hn_m, route_m, rt_m, pool_m, st_m, cnt_m = meta
    del h_m, hn_m, route_m, rt_m, cnt_m
    h_p, hn_p, route_p, rt_p, pool_p, st_p, cnt_p = _mixer(
        x_prompt.reshape(t_prompt, D_MODEL), pool_m, st_m, zero_cnt, weights,
        batch=batch, seq=seq, tb=MIX_TB, lead_pad=0)
    pool_s0 = jnp.pad(state_pool[0], ((0, 0), (POOL_ROWS - POOL_PAD, 0), (0, 0)))
    h_s, hn_s, route_s, rt_s, pool_s, st_s, cnt_s = _mixer(
        x_sample.reshape(t_sample, D_MODEL), pool_s0, state_gla[0].astype(F32),
        cnt_p, weights, batch=dec_batch, seq=dec_seq, tb=SAMPLE_TB, lead_pad=0)

    counts = cnt_s[:, 0].astype(I32)
    padded = (counts + MOE_BLK - 1) // MOE_BLK * MOE_BLK
    ends = jnp.cumsum(padded)
    pstart = ends - padded
    n_blocks = (2 * t_all + N_EXPERTS * (MOE_BLK - 1) + MOE_BLK - 1) // MOE_BLK
    nb = (ends[-1] // MOE_BLK).astype(I32)
    blk_ids = jnp.minimum(jnp.arange(n_blocks, dtype=I32), nb - 1)
    block_e = jnp.sum((ends[None, :] <= (blk_ids * MOE_BLK)[:, None]).astype(I32), axis=1)
    block_e = jnp.minimum(block_e, N_EXPERTS - 1)
    owner = block_e[:, None] == jnp.arange(N_EXPERTS, dtype=I32)
    row_end = jnp.sum(jnp.where(owner, pstart + counts, 0), axis=1)
    block_valid = jnp.clip(row_end - blk_ids * MOE_BLK, 0, MOE_BLK)

    def dest_rows(rt):
        rt = rt.transpose(1, 0, 2).reshape(ROUTE_ROWS, -1)
        onehot = rt[0:2].astype(I32)[..., None] == jnp.arange(N_EXPERTS, dtype=I32)
        return jnp.sum(jnp.where(onehot, pstart, 0), axis=-1) + rt[4:6].astype(I32)

    dest = jnp.concatenate([dest_rows(rt_p), dest_rows(rt_s)], axis=1)
    dest0 = dest[0].reshape(t_all // SC_ROWS, SC_ROWS)
    dest1 = dest[1].reshape(t_all // SC_ROWS, SC_ROWS)
    xs = _dispatch(hn_p, hn_s, dest0, dest1, n_blocks * MOE_BLK)
    ys = _experts((pstart // MOE_BLK).astype(I32), (padded // MOE_BLK).astype(I32),
                  block_valid.astype(I32), nb[None], xs,
                  w_expert_gate[0], w_expert_up[0], w_expert_down[0])
    nfin = norm_final[None, :]
    cp_chunks = t_prompt // SC_ROWS
    part = cp_chunks // COMBINE_PARTS
    y_prompt = None
    for i in range(COMBINE_PARTS):
        ch = slice(i * part, (i + 1) * part)
        rows_i = _gather(ys, dest0[ch], dest1[ch])
        y_prompt = _combine(h_p, route_p, nfin, rows_i, row0=i * part * SC_ROWS, y_prev=y_prompt)
    y_prompt = y_prompt.reshape(batch, seq, D_MODEL)
    rows_s = _gather(ys, dest0[cp_chunks:], dest1[cp_chunks:])
    y_sample = _combine(h_s, route_s, nfin, rows_s).reshape(dec_batch, dec_seq, D_MODEL)
    new_pool_prompt = pool_p[:, POOL_ROWS - POOL_PAD:][None]
    new_gla_prompt = st_p[None]
    new_pool_sample = pool_s[:, POOL_ROWS - POOL_PAD:][None]
    new_gla_sample = st_s[None]
    return (y_prompt, y_sample, new_pool_prompt, new_gla_prompt, new_pool_sample, new_gla_sample)
```

```python
import functools

import jax
import jax.numpy as jnp
from jax import lax
from jax.experimental import pallas as pl
from jax.experimental.pallas import tpu as pltpu
from jax.experimental.pallas import tpu_sc as plsc

F32 = jnp.float32
BF16 = jnp.bfloat16
U32 = jnp.uint32
I32 = jnp.int32

D_MODEL = 1024
N_META = 16
CHUNK = 64
EPS = 1e-6
D_POOL = 512
POOL_WINDOWS = (2, 4, 8, 16)
POOL_GROUP_DIM = 128
POOL_PAD = 15
POOL_ROWS = 16
GLA_HEADS = 4
GLA_DK = 64
GLA_DV = 128
D_QK = 256
D_V = 512
HEAD_GROUP = 2
GATE_RANK = 16
GATE_TAU = 16.0
D_MAIN = D_POOL + 2 * D_QK + 2 * D_V
N_GROUPS = 4
EXPERTS_PER_GROUP = 8
N_EXPERTS = 32
D_EXPERT = 512

LANES = 128
MXU_DIM = 256
HALF = D_MODEL // 2
HN_PLANES = HALF // LANES
ROUTE_ROWS = 8
ROUTER_ROWS = 64
MIX_TB = 512
SAMPLE_TB = 256
FRONT_TILE = 256
FRONT_PLAN = (2, 0, 2, 2, 2, 0)
MOE_BLK = 512
ROW_DMA_PRIORITY = 1
COMBINE_TB = 512
COMBINE_PARTS = 4
VMEM_LIMIT = 48 * 1024 * 1024
SC_CORES = 2
SC_SUBCORES = 16
SC_WORKERS = SC_CORES * SC_SUBCORES
SC_ROWS = 128


def _rms(x, g):
    return x * lax.rsqrt(jnp.mean(x * x, axis=-1, keepdims=True) + EPS) * g


def _dot(a, b):
    return jnp.dot(a, b, preferred_element_type=F32)


def _dot_nt(a, b):
    return lax.dot_general(a, b, (((1,), (1,)), ((), ())), preferred_element_type=F32)


def _pack_planes(x, ref):
    xb = x.astype(BF16)
    lo = lax.bitcast_convert_type(xb[:, :HALF].astype(F32), U32) >> 16
    hi = lax.bitcast_convert_type(xb[:, HALF:].astype(F32), U32) & jnp.uint32(0xFFFF0000)
    packed = lax.bitcast_convert_type(lo | hi, I32)
    for p in range(HN_PLANES):
        ref[p] = packed[:, p * LANES:(p + 1) * LANES]


def _unpack_planes(planes):
    words = [lax.bitcast_convert_type(p, U32) for p in planes]
    los = [lax.bitcast_convert_type(w << 16, F32) for w in words]
    his = [lax.bitcast_convert_type(w & jnp.uint32(0xFFFF0000), F32) for w in words]
    return jnp.concatenate(los + his, axis=-1)


def _dot_tn(a, b):
    return lax.dot_general(a, b, (((0,), (0,)), ((), ())), preferred_element_type=F32)


def _mixer_kernel(x_ref, pool0_ref, st0_ref, cnt0_ref, tril_ref, sup_ref,
                  nmix_ref, wmain_ref, wz_ref, wgu_ref, bgate_ref, wpool_ref, pscale_ref,
                  gnorm_ref, wout_ref, nffn_ref, wr_ref, br_ref,
                  h_ref, hn_ref, route_ref, route_t_ref, pool_out_ref, st_out_ref, cnt_out_ref,
                  ext_ref, st_ref, kbd_ref, vbd_ref, sbd_ref, o_ref, cnt_ref,
                  xs_ref, proj_ref, z_ref, *, tb, nj, lead_pad, chained):
    s = pl.program_id(0)
    back = jnp.maximum(s - 1, 0)
    j = lax.rem(back, nj)
    n_chunks = tb // CHUNK
    wr_slot = lax.rem(s, 2)
    rd_slot = 1 - wr_slot

    @pl.when(s == 0)
    def _():
        kbd_ref[...] = jnp.zeros_like(kbd_ref)
        vbd_ref[...] = jnp.zeros_like(vbd_ref)
        sbd_ref[...] = jnp.zeros_like(sbd_ref)
        xs_ref[1] = jnp.zeros((tb, D_MODEL), F32)
        proj_ref[1] = jnp.zeros((tb, D_MAIN), F32)
        z_ref[1] = jnp.zeros((GATE_RANK, tb), F32)

    @pl.when(s <= 1)
    def _():
        cnt_ref[...] = cnt0_ref[...]

    def put_state(c, hh, st):
        gg, hp = divmod(hh, HEAD_GROUP)
        sbd_ref[c, gg, hp * GLA_DK:(hp + 1) * GLA_DK, hp * GLA_DV:(hp + 1) * GLA_DV] = st.astype(BF16)

    if chained:
        @pl.when(j == 0)
        def _():
            ext_ref[0:POOL_ROWS, :] = pool0_ref[0]
            st_ref[...] = st0_ref[0]

        for hh in range(GLA_HEADS):
            put_state(0, hh, st_ref[hh])
    else:
        for c in range(n_chunks):
            for hh in range(GLA_HEADS):
                put_state(c, hh, st0_ref[c, hh])

    x_new = x_ref[...]
    xn = _rms(x_new, nmix_ref[...]).astype(BF16)
    xs_ref[wr_slot] = x_new

    tiles_done = [0]

    def front_tiles(stage):
        for t in range(tiles_done[0], tiles_done[0] + FRONT_PLAN[stage]):
            cols = slice(t * FRONT_TILE, (t + 1) * FRONT_TILE)
            proj_ref[wr_slot, :, cols] = _dot(xn, wmain_ref[:, cols])
        tiles_done[0] += FRONT_PLAN[stage]

    front_tiles(0)
    z_ref[wr_slot] = _dot_nt(wz_ref[...], xn)

    x = xs_ref[rd_slot]
    z = z_ref[rd_slot]
    u = proj_ref[rd_slot, :, 0:D_POOL]
    q = proj_ref[rd_slot, :, D_POOL:D_POOL + D_QK]
    k = proj_ref[rd_slot, :, D_POOL + D_QK:D_POOL + 2 * D_QK]
    v = proj_ref[rd_slot, :, D_POOL + 2 * D_QK:D_POOL + 2 * D_QK + D_V]
    r = proj_ref[rd_slot, :, D_POOL + 2 * D_QK + D_V:D_MAIN]

    row = lax.broadcasted_iota(I32, (tb, 1), 0)

    pseg = POOL_ROWS + CHUNK
    if chained:
        ext_ref[POOL_ROWS:POOL_ROWS + tb, :] = u
        ext = ext_ref[...]
    else:
        ext = jnp.concatenate(
            [blk for c in range(n_chunks) for blk in (pool0_ref[c], u[c * CHUNK:(c + 1) * CHUNK])], axis=0)
    pooled = []
    for g, w in enumerate(POOL_WINDOWS):
        sl = slice(g * POOL_GROUP_DIM, (g + 1) * POOL_GROUP_DIM)
        acc = ext[:, sl]
        for d in range(g + 1):
            acc = acc + pltpu.roll(acc, 1 << d, axis=0)
        if chained:
            win = acc[POOL_ROWS:, :]
        else:
            win = jnp.concatenate([acc[c * pseg + POOL_ROWS:(c + 1) * pseg] for c in range(n_chunks)], axis=0)
        if lead_pad:
            cnt = jnp.clip(row - lead_pad + 1, 1, w).astype(F32)
            pooled.append(win / cnt - u[:, sl])
        else:
            pooled.append(win * (1.0 / w) - u[:, sl])
    pys = [_dot(jnp.concatenate(pooled[2 * i:2 * i + 2], axis=-1).astype(BF16), wpool_ref[i])
           for i in range(len(POOL_WINDOWS) // 2)]
    pool_y = jnp.concatenate(pys, axis=-1) * pscale_ref[...]
    if chained:
        ext_ref[0:POOL_ROWS, :] = ext_ref[tb:tb + POOL_ROWS, :]
    else:
        for c in range(n_chunks):
            pool_out_ref[c] = u[(c + 1) * CHUNK - POOL_ROWS:(c + 1) * CHUNK]

    gpre = _dot_tn(z.astype(BF16), wgu_ref[...]) + bgate_ref[...]
    log_a = jax.nn.log_sigmoid(gpre) * (1.0 / GATE_TAU)
    if lead_pad:
        log_a = jnp.where(row >= lead_pad, log_a, 0.0)
    a_hi = log_a.astype(BF16)
    a_lo = (log_a - a_hi.astype(F32)).astype(BF16)
    tril = tril_ref[...]
    seg = tril.shape[0]
    bcum = jnp.concatenate(
        [_dot(tril, a_hi[r0:r0 + seg]) + _dot(tril, a_lo[r0:r0 + seg]) for r0 in range(0, tb, seg)], axis=0)
    front_tiles(1)
    eb = jnp.exp(bcum)
    qi = q * (GLA_DK ** -0.5) * eb
    ki = k * jnp.exp(-bcum)

    rr = lax.broadcasted_iota(I32, (CHUNK, HEAD_GROUP * CHUNK), 0)
    cc = lax.broadcasted_iota(I32, (CHUNK, HEAD_GROUP * CHUNK), 1)
    causal = (cc % CHUNK) <= rr

    lasts = [eb[(c + 1) * CHUNK - 1:(c + 1) * CHUNK, :] for c in range(n_chunks)]
    dcol = jnp.concatenate(lasts + [jnp.zeros((LANES - n_chunks, D_QK), F32)], axis=0).T

    n_grp = GLA_HEADS // HEAD_GROUP
    chunk_rows = [slice(c * CHUNK, (c + 1) * CHUNK) for c in range(n_chunks)]
    grp_k = [slice(g * HEAD_GROUP * GLA_DK, (g + 1) * HEAD_GROUP * GLA_DK) for g in range(n_grp)]
    grp_v = [slice(g * HEAD_GROUP * GLA_DV, (g + 1) * HEAD_GROUP * GLA_DV) for g in range(n_grp)]
    qi_b = qi.astype(BF16)
    ki_b = ki.astype(BF16)
    v_b = v.astype(BF16)

    scores = {}
    for c in range(n_chunks):
        for hh in range(GLA_HEADS):
            gg, hp = divmod(hh, HEAD_GROUP)
            kbd_ref[c, gg, hp * CHUNK:(hp + 1) * CHUNK, hp * GLA_DK:(hp + 1) * GLA_DK] = (
                ki_b[chunk_rows[c], hh * GLA_DK:(hh + 1) * GLA_DK])
            vbd_ref[c, gg, hp * CHUNK:(hp + 1) * CHUNK, hp * GLA_DV:(hp + 1) * GLA_DV] = (
                v_b[chunk_rows[c], hh * GLA_DV:(hh + 1) * GLA_DV])
        for gg in range(n_grp):
            scores[c, gg] = _dot_nt(qi_b[chunk_rows[c], grp_k[gg]], kbd_ref[c, gg])

    kvs = {}
    for c in range(n_chunks):
        kd_c = (ki[chunk_rows[c]] * lasts[c]).astype(BF16)
        for hh in range(GLA_HEADS):
            kvs[c, hh] = _dot_tn(kd_c[:, hh * GLA_DK:(hh + 1) * GLA_DK],
                                 v_b[chunk_rows[c], hh * GLA_DV:(hh + 1) * GLA_DV])
    front_tiles(2)

    for hh in range(GLA_HEADS):
        st = st_ref[hh] if chained else None
        for c in range(n_chunks):
            s_old = st if chained else st0_ref[c, hh]
            s_new = s_old * dcol[hh * GLA_DK:(hh + 1) * GLA_DK, c:c + 1] + kvs[c, hh]
            if not chained:
                st_out_ref[c, hh] = s_new
            else:
                st = s_new
                if c + 1 < n_chunks:
                    put_state(c + 1, hh, s_new)
        if chained:
            st_ref[hh] = st

    for c in range(n_chunks):
        for gg in range(n_grp):
            p = jnp.where(causal, scores[c, gg], 0.0).astype(BF16)
            o_ref[chunk_rows[c], grp_v[gg]] = (
                _dot(p, vbd_ref[c, gg]) + _dot(qi_b[chunk_rows[c], grp_k[gg]], sbd_ref[c, gg]))

    o = o_ref[...]
    ons = []
    for hh in range(GLA_HEADS):
        oh = o[:, hh * GLA_DV:(hh + 1) * GLA_DV]
        ons.append(oh * lax.rsqrt(jnp.mean(oh * oh, axis=-1, keepdims=True) + EPS))
    og = jnp.concatenate(ons, axis=-1) * gnorm_ref[...] * (r * jax.nn.sigmoid(r))
    mix = _dot(jnp.concatenate([pool_y, og], axis=-1).astype(BF16), wout_ref[...])
    front_tiles(3)

    h = x + mix
    h_ref[...] = h
    hn = _rms(h, nffn_ref[...]).astype(BF16)
    _pack_planes(hn, hn_ref)

    logits = _dot_nt(wr_ref[...], hn) + br_ref[...]
    front_tiles(4)
    sub = lax.broadcasted_iota(I32, (EXPERTS_PER_GROUP, tb), 0).astype(F32)
    neg = jnp.float32(-jnp.inf)
    big = jnp.float32(EXPERTS_PER_GROUP)
    tile0 = logits[0:EXPERTS_PER_GROUP]
    is_g = sub < N_GROUPS
    gmax = jnp.max(jnp.where(is_g, tile0, neg), axis=0, keepdims=True)
    gsum = jnp.sum(jnp.where(is_g, jnp.exp(tile0 - gmax), 0.0), axis=0, keepdims=True)
    p_g = 1.0 / gsum
    gidx = jnp.min(jnp.where(is_g & (tile0 == gmax), sub, big), axis=0, keepdims=True)
    el = logits[N_GROUPS * EXPERTS_PER_GROUP:(N_GROUPS + 1) * EXPERTS_PER_GROUP]
    for g in range(N_GROUPS - 2, -1, -1):
        el = jnp.where(gidx == g, logits[(g + 1) * EXPERTS_PER_GROUP:(g + 2) * EXPERTS_PER_GROUP], el)
    m1 = jnp.max(el, axis=0, keepdims=True)
    i1 = jnp.min(jnp.where(el == m1, sub, big), axis=0, keepdims=True)
    rest = sub != i1
    m2 = jnp.max(jnp.where(rest, el, neg), axis=0, keepdims=True)
    i2 = jnp.min(jnp.where(rest & (el == m2), sub, big), axis=0, keepdims=True)
    t2 = jnp.exp(m2 - m1)
    den = 1.0 + t2
    g1 = p_g / den
    g2 = p_g * t2 / den
    e1 = gidx * EXPERTS_PER_GROUP + i1
    e2 = gidx * EXPERTS_PER_GROUP + i2

    eid = lax.broadcasted_iota(I32, (N_EXPERTS, tb), 0).astype(F32)
    oh1 = eid == e1
    oh2 = eid == e2
    both = jnp.where(oh1 | oh2, 1.0, 0.0)
    cnt = cnt_ref[...]
    before = _dot(both.astype(BF16), sup_ref[...]) + cnt
    front_tiles(5)
    assert tiles_done[0] * FRONT_TILE == D_MAIN
    pos1 =jnp.sum(jnp.where(oh1, before, 0.0), axis=0, keepdims=True)
    pos2 = jnp.sum(jnp.where(oh2, before, 0.0), axis=0, keepdims=True)
    cnt_new = cnt + jnp.sum(both, axis=1, keepdims=True)
    cnt_ref[...] = cnt_new
    cnt_out_ref[...] = cnt_new

    zero = jnp.zeros_like(e1)
    route_t = jnp.concatenate([e1, e2, g1, g2, pos1, pos2, zero, zero], axis=0)
    route_t_ref[0] = route_t
    route_ref[...] = jnp.concatenate([route_t, jnp.zeros((LANES - ROUTE_ROWS, tb), F32)], axis=0).T

    if chained:
        @pl.when(j == nj - 1)
        def _():
            pool_out_ref[0] = ext_ref[0:POOL_ROWS, :]
            st_out_ref[0] = st_ref[...]


def _mixer(x2d, pool0, st0, cnt0, weights, *, batch, seq, tb, lead_pad):
    chained = tb <= seq
    total_rows = batch * seq
    n_blk = total_rows // tb
    nj = seq // tb if chained else 1
    per_blk = 1 if chained else tb // seq
    assert chained or (seq == CHUNK and batch % per_blk == 0)
    shared = pool0.shape[0] == 1
    front = lambda s: jnp.minimum(s, n_blk - 1)
    back = lambda s: jnp.maximum(s - 1, 0)
    stream = lambda s: back(s) // nj
    st_idx = (lambda s: (0, 0, 0)) if shared else (lambda s: (stream(s), 0, 0))
    gla_idx = (lambda s: (0, 0, 0, 0)) if shared else (lambda s: (stream(s), 0, 0, 0))
    const2 = lambda s: (0, 0)
    tok_out = lambda s: (back(s), 0)

    seg = min(tb, MXU_DIM)
    ii = jnp.arange(seg)
    tril = ((ii[:, None] >= ii[None, :]) & (ii[:, None] // CHUNK == ii[None, :] // CHUNK)).astype(BF16)
    ii = jnp.arange(tb)
    sup = (ii[:, None] < ii[None, :]).astype(BF16)

    in_specs = [
        pl.BlockSpec((tb, D_MODEL), lambda s: (front(s), 0)),
        pl.BlockSpec((per_blk, POOL_ROWS, D_POOL), st_idx),
        pl.BlockSpec((per_blk, GLA_HEADS, GLA_DK, GLA_DV), gla_idx),
        pl.BlockSpec((N_EXPERTS, 1), const2),
        pl.BlockSpec((seg, seg), const2),
        pl.BlockSpec((tb, tb), const2),
    ]
    for wgt in weights:
        in_specs.append(pl.BlockSpec(wgt.shape, (lambda s, n=wgt.ndim: (0,) * n)))
    args = [x2d, pool0, st0, cnt0, tril, sup, *weights]

    out_shape = [
        jax.ShapeDtypeStruct((total_rows, D_MODEL), F32),
        jax.ShapeDtypeStruct((HN_PLANES, total_rows, LANES), I32),
        jax.ShapeDtypeStruct((total_rows, LANES), F32),
        jax.ShapeDtypeStruct((n_blk, ROUTE_ROWS, tb), F32),
        jax.ShapeDtypeStruct((batch, POOL_ROWS, D_POOL), F32),
        jax.ShapeDtypeStruct((batch, GLA_HEADS, GLA_DK, GLA_DV), F32),
        jax.ShapeDtypeStruct((N_EXPERTS, 1), F32),
    ]
    out_specs = [
        pl.BlockSpec((tb, D_MODEL), tok_out),
        pl.BlockSpec((HN_PLANES, tb, LANES), lambda s: (0, back(s), 0)),
        pl.BlockSpec((tb, LANES), tok_out),
        pl.BlockSpec((1, ROUTE_ROWS, tb), lambda s: (back(s), 0, 0)),
        pl.BlockSpec((per_blk, POOL_ROWS, D_POOL), lambda s: (stream(s), 0, 0)),
        pl.BlockSpec((per_blk, GLA_HEADS, GLA_DK, GLA_DV), lambda s: (stream(s), 0, 0, 0)),
        pl.BlockSpec((N_EXPERTS, 1), const2),
    ]
    n_grp = GLA_HEADS // HEAD_GROUP
    scratch = [
        pltpu.VMEM((POOL_ROWS + tb, D_POOL), F32),
        pltpu.VMEM((GLA_HEADS, GLA_DK, GLA_DV), F32),
        pltpu.VMEM((tb // CHUNK, n_grp, HEAD_GROUP * CHUNK, HEAD_GROUP * GLA_DK), BF16),
        pltpu.VMEM((tb // CHUNK, n_grp, HEAD_GROUP * CHUNK, HEAD_GROUP * GLA_DV), BF16),
        pltpu.VMEM((tb // CHUNK, n_grp, HEAD_GROUP * GLA_DK, HEAD_GROUP * GLA_DV), BF16),
        pltpu.VMEM((tb, D_V), F32),
        pltpu.VMEM((N_EXPERTS, 1), F32),
        pltpu.VMEM((2, tb, D_MODEL), F32),
        pltpu.VMEM((2, tb, D_MAIN), F32),
        pltpu.VMEM((2, GATE_RANK, tb), F32),
    ]
    return pl.pallas_call(
        functools.partial(_mixer_kernel, tb=tb, nj=nj, lead_pad=lead_pad, chained=chained),
        grid=(n_blk + 1,),
        in_specs=in_specs,
        out_specs=out_specs,
        out_shape=out_shape,
        scratch_shapes=scratch,
        compiler_params=pltpu.CompilerParams(
            dimension_semantics=("arbitrary",), vmem_limit_bytes=VMEM_LIMIT),
        name=f"mixer_tb{tb}_pad{lead_pad}",
    )(*args)


def _sc_mesh():
    return plsc.VectorSubcoreMesh(core_axis_name="c", subcore_axis_name="s",
                                  num_cores=SC_CORES, num_subcores=SC_SUBCORES)


def _sc_worker():
    return lax.axis_index("s") * SC_CORES + lax.axis_index("c")


def _plane_rows(dest, planes, rows_per_plane):
    offs = (jnp.arange(planes, dtype=I32) * rows_per_plane)[None, :, None]
    return dest[:, None, :] + offs


def _dispatch(hn_p, hn_s, dest0, dest1, n_rows):
    planes, t_p, _ = hn_p.shape
    t_s = hn_s.shape[1]
    n_cp = t_p // SC_ROWS
    cp = n_cp // SC_WORKERS
    n_cs = t_s // SC_ROWS
    assert t_p == cp * SC_ROWS * SC_WORKERS and t_s == n_cs * SC_ROWS and n_cs <= SC_WORKERS
    idx0 = _plane_rows(dest0, planes, n_rows)
    idx1 = _plane_rows(dest1, planes, n_rows)

    def body(hnp_hbm, hns_hbm, d0_hbm, d1_hbm, xs_hbm, rows_v, i0_v, i1_v, is0_v, is1_v, sem_in, sem_out):
        wid = _sc_worker()
        pltpu.sync_copy(d0_hbm.at[pl.ds(wid * cp, cp)], i0_v)
        pltpu.sync_copy(d1_hbm.at[pl.ds(wid * cp, cp)], i1_v)

        def move(src_hbm, src_rows, row0, i0, i1, c):
            loads = [pltpu.async_copy(src_hbm.at[pl.ds(p * src_rows + row0, SC_ROWS)], rows_v.at[p], sem_in)
                     for p in range(planes)]
            for cpy in loads:
                cpy.wait()
            stores = []
            for p in range(planes):
                stores.append(pltpu.async_copy(rows_v.at[p], xs_hbm.at[i0.at[c, p]], sem_out))
                stores.append(pltpu.async_copy(rows_v.at[p], xs_hbm.at[i1.at[c, p]], sem_out))
            for cpy in stores:
                cpy.wait()

        @pl.loop(0, cp)
        def _(c):
            move(hnp_hbm, t_p, (wid * cp + c) * SC_ROWS, i0_v, i1_v, c)

        @pl.when(wid < n_cs)
        def _():
            pltpu.sync_copy(d0_hbm.at[pl.ds(n_cp + wid, 1)], is0_v)
            pltpu.sync_copy(d1_hbm.at[pl.ds(n_cp + wid, 1)], is1_v)
            move(hns_hbm, t_s, wid * SC_ROWS, is0_v, is1_v, 0)

    xs = pl.kernel(
        body,
        out_type=jax.ShapeDtypeStruct((planes * n_rows, LANES), I32),
        mesh=_sc_mesh(),
        scratch_types=[
            pltpu.VMEM((planes, SC_ROWS, LANES), I32),
            pltpu.VMEM((cp, planes, SC_ROWS), I32),
            pltpu.VMEM((cp, planes, SC_ROWS), I32),
            pltpu.VMEM((1, planes, SC_ROWS), I32),
            pltpu.VMEM((1, planes, SC_ROWS), I32),
            pltpu.SemaphoreType.DMA,
            pltpu.SemaphoreType.DMA,
        ],
        name="moe_dispatch_sc",
    )(hn_p.reshape(planes * t_p, LANES), hn_s.reshape(planes * t_s, LANES), idx0, idx1)
    return xs.reshape(planes, n_rows, LANES)


def _gather(ys, dest0, dest1):
    planes, n_rows, _ = ys.shape
    n_chunks = dest0.shape[0]
    n_tok = n_chunks * SC_ROWS
    cpw = max(n_chunks // SC_WORKERS, 1)
    assert n_chunks <= SC_WORKERS or n_chunks == cpw * SC_WORKERS
    idx = (_plane_rows(dest0, planes, n_rows), _plane_rows(dest1, planes, n_rows))

    def body(ys_hbm, d0_hbm, d1_hbm, out_hbm, rows_v, i0_v, i1_v, sem_in, sem_out):
        wid = _sc_worker()

        def work():
            pltpu.sync_copy(d0_hbm.at[pl.ds(wid * cpw, cpw)], i0_v)
            pltpu.sync_copy(d1_hbm.at[pl.ds(wid * cpw, cpw)], i1_v)

            @pl.loop(0, cpw)
            def _(c):
                row0 = (wid * cpw + c) * SC_ROWS
                for j, i_v in enumerate((i0_v, i1_v)):
                    loads = [pltpu.async_copy(ys_hbm.at[i_v.at[c, p]], rows_v.at[p], sem_in)
                             for p in range(planes)]
                    for cpy in loads:
                        cpy.wait()
                    stores = [
                        pltpu.async_copy(
                            rows_v.at[p], out_hbm.at[pl.ds((j * planes + p) * n_tok + row0, SC_ROWS)], sem_out)
                        for p in range(planes)]
                    for cpy in stores:
                        cpy.wait()

        if n_chunks < SC_WORKERS:
            pl.when(wid < n_chunks)(work)
        else:
            work()

    out = pl.kernel(
        body,
        out_type=jax.ShapeDtypeStruct((2 * planes * n_tok, LANES), ys.dtype),
        mesh=_sc_mesh(),
        scratch_types=[
            pltpu.VMEM((planes, SC_ROWS, LANES), ys.dtype),
            pltpu.VMEM((cpw, planes, SC_ROWS), I32),
            pltpu.VMEM((cpw, planes, SC_ROWS), I32),
            pltpu.SemaphoreType.DMA,
            pltpu.SemaphoreType.DMA,
        ],
        name="moe_gather_sc",
    )(ys.reshape(planes * n_rows, LANES), *idx)
    return out.reshape(2, planes, n_tok, LANES)


def _expert_kernel(first_ref, nblk_ref, bv_ref, nb_ref, xs_hbm, wg_ref, wu_ref, wd_ref, ys_hbm,
                   wgu_s, wd_s, xbuf, ybuf, sem_in, sem_out, *, n_blocks):
    e = pl.program_id(0)
    n = nblk_ref[e]
    base = first_ref[e]

    def rows(blk):
        return pl.ds(blk * MOE_BLK, MOE_BLK)

    def in_copy(blk, slot):
        return pltpu.make_async_copy(xs_hbm.at[:, rows(blk), :], xbuf.at[slot], sem_in.at[slot])

    def out_copy(blk, slot):
        return pltpu.make_async_copy(ybuf.at[slot], ys_hbm.at[:, rows(blk), :], sem_out.at[slot])

    @pl.when(n > 0)
    def _():
        in_copy(base, 0).start(priority=ROW_DMA_PRIORITY)
        wgu_s[:, 0:D_EXPERT] = wg_ref[0].astype(BF16)
        wgu_s[:, D_EXPERT:2 * D_EXPERT] = wu_ref[0].astype(BF16)
        wd_s[...] = wd_ref[0].astype(BF16)

    def block(b, carry):
        slot = lax.rem(b, 2)
        in_copy(base + b, slot).wait()

        @pl.when(b + 1 < n)
        def _():
            in_copy(base + b + 1, 1 - slot).start(priority=ROW_DMA_PRIORITY)

        @pl.when(b >= 2)
        def _():
            out_copy(base + b - 2, slot).wait()

        valid = lax.broadcasted_iota(I32, (MOE_BLK, LANES), 0) < bv_ref[base + b]
        xb = _unpack_planes([jnp.where(valid, xbuf[slot, p], 0) for p in range(HN_PLANES)]).astype(BF16)
        gu = _dot(xb, wgu_s[...])
        gate = gu[:, 0:D_EXPERT]
        hmid = gate * jax.nn.sigmoid(gate) * gu[:, D_EXPERT:]
        _pack_planes(_dot(hmid.astype(BF16), wd_s[...]), ybuf.at[slot])
        out_copy(base + b, slot).start()
        return carry

    lax.fori_loop(0, n, block, 0)

    @pl.when(n >= 2)
    def _():
        out_copy(base + n - 2, lax.rem(n, 2)).wait()

    @pl.when(n >= 1)
    def _():
        out_copy(base + n - 1, lax.rem(n - 1, 2)).wait()

    @pl.when(e == pl.num_programs(0) - 1)
    def _():
        ybuf[0] = jnp.zeros((HN_PLANES, MOE_BLK, LANES), I32)

        def fill(blk, carry):
            cp = out_copy(blk, 0)
            cp.start()
            cp.wait()
            return carry

        lax.fori_loop(nb_ref[0], n_blocks, fill, 0)


def _experts(first_blk, n_blk, block_valid, nb, xs, w_eg, w_eu, w_ed):
    n_rows = xs.shape[1]
    n_blocks = n_rows // MOE_BLK
    w_idx = lambda e, *_: (e, 0, 0)
    grid_spec = pltpu.PrefetchScalarGridSpec(
        num_scalar_prefetch=4,
        grid=(N_EXPERTS,),
        in_specs=[
            pl.BlockSpec(memory_space=pl.ANY),
            pl.BlockSpec((1, D_MODEL, D_EXPERT), w_idx),
            pl.BlockSpec((1, D_MODEL, D_EXPERT), w_idx),
            pl.BlockSpec((1, D_EXPERT, D_MODEL), w_idx),
        ],
        out_specs=pl.BlockSpec(memory_space=pl.ANY),
        scratch_shapes=[
            pltpu.VMEM((D_MODEL, 2 * D_EXPERT), BF16),
            pltpu.VMEM((D_EXPERT, D_MODEL), BF16),
            pltpu.VMEM((2, HN_PLANES, MOE_BLK, LANES), I32),
            pltpu.VMEM((2, HN_PLANES, MOE_BLK, LANES), I32),
            pltpu.SemaphoreType.DMA((2,)),
            pltpu.SemaphoreType.DMA((2,)),
        ],
    )
    return pl.pallas_call(
        functools.partial(_expert_kernel, n_blocks=n_blocks),
        grid_spec=grid_spec,
        out_shape=jax.ShapeDtypeStruct((HN_PLANES, n_rows, LANES), I32),
        compiler_params=pltpu.CompilerParams(
            dimension_semantics=("arbitrary",), vmem_limit_bytes=VMEM_LIMIT),
        name="moe_experts",
    )(first_blk, n_blk, block_valid, nb, xs, w_eg, w_eu, w_ed)


def _combine_kernel(h_ref, route_ref, nfin_ref, rows_ref, *rest):
    y_ref = rest[-1]
    route = route_ref[...]
    ys1 = _unpack_planes([rows_ref[0, p] for p in range(HN_PLANES)])
    ys2 = _unpack_planes([rows_ref[1, p] for p in range(HN_PLANES)])
    out = h_ref[...] + (ys1 * route[:, 2:3] + ys2 * route[:, 3:4])
    y_ref[...] = _rms(out, nfin_ref[...])


def _combine(h, route, norm_final, rows, *, row0=0, y_prev=None):
    total = h.shape[0]
    tb = COMBINE_TB
    blk0 = row0 // tb
    args = [h, route, norm_final, rows]
    in_specs = [
        pl.BlockSpec((tb, D_MODEL), lambda i: (blk0 + i, 0)),
        pl.BlockSpec((tb, LANES), lambda i: (blk0 + i, 0)),
        pl.BlockSpec((1, D_MODEL), lambda i: (0, 0)),
        pl.BlockSpec((2, HN_PLANES, tb, LANES), lambda i: (0, 0, i, 0)),
    ]
    aliases = {}
    if y_prev is not None:
        aliases[len(args)] = 0
        args.append(y_prev)
        in_specs.append(pl.BlockSpec(memory_space=pl.ANY))
    return pl.pallas_call(
        _combine_kernel,
        grid=(rows.shape[2] // tb,),
        in_specs=in_specs,
        out_specs=pl.BlockSpec((tb, D_MODEL), lambda i: (blk0 + i, 0)),
        out_shape=jax.ShapeDtypeStruct((total, D_MODEL), F32),
        input_output_aliases=aliases,
        compiler_params=pltpu.CompilerParams(
            dimension_semantics=("arbitrary",), vmem_limit_bytes=VMEM_LIMIT),
        name="moe_combine",
    )(*args)


def kernel(x_prompt, x_sample, state_pool, state_gla, meta_tokens, norm_mix, w_in, w_gate_up, b_gate, w_pool, pool_scale, gla_norm, w_out, norm_ffn, w_router_group, b_router_group, w_router_expert, b_router_expert, w_expert_gate, w_expert_up, w_expert_down, norm_final):
    assert w_in.shape[0] == 1, "one encoder layer"
    batch, seq, _ = x_prompt.shape
    dec_batch, dec_seq, _ = x_sample.shape
    assert seq % MIX_TB == 0 and dec_seq == CHUNK and N_META <= CHUNK
    t_prompt = batch * seq
    t_sample = dec_batch * dec_seq
    t_all = t_prompt + t_sample
    assert t_prompt % COMBINE_TB == 0 and t_sample % COMBINE_TB == 0

    w_in0 = w_in[0]
    gpad = EXPERTS_PER_GROUP - N_GROUPS
    rpad = ROUTER_ROWS - EXPERTS_PER_GROUP - N_EXPERTS
    w_router = jnp.concatenate([
        w_router_group[0].T, jnp.zeros((gpad, D_MODEL), F32),
        w_router_expert[0].T, jnp.zeros((rpad, D_MODEL), F32)], axis=0)
    b_router = jnp.concatenate([
        b_router_group[0], jnp.zeros((gpad,), F32), b_router_expert[0], jnp.zeros((rpad,), F32)])
    zg = jnp.zeros((POOL_GROUP_DIM, POOL_GROUP_DIM), F32)
    w_pool_pairs = jnp.stack([
        jnp.block([[w_pool[0, 2 * i], zg], [zg, w_pool[0, 2 * i + 1]]]) for i in range(len(POOL_WINDOWS) // 2)])
    weights = (
        norm_mix[0][None, :],
        w_in0[:, :D_MAIN].astype(BF16),
        w_in0[:, D_MAIN:].T.astype(BF16),
        w_gate_up[0].astype(BF16),
        b_gate[0][None, :],
        w_pool_pairs.astype(BF16),
        pool_scale[0][None, :],
        gla_norm[0][None, :],
        w_out[0].astype(BF16),
        norm_ffn[0][None, :],
        w_router.astype(BF16),
        b_router[:, None],
    )

    zero_cnt = jnp.zeros((N_EXPERTS, 1), F32)
    x_meta = jnp.pad(meta_tokens.astype(F32), ((CHUNK - N_META, 0), (0, 0)))
    meta = _mixer(x_meta, jnp.zeros((1, POOL_ROWS, D_POOL), F32),
                  jnp.zeros((1, GLA_HEADS, GLA_DK, GLA_DV), F32), zero_cnt,
                  weights, batch=1, seq=CHUNK, tb=CHUNK, lead_pad=CHUNK - N_META)
    h_m, hn_m, route_m, rt_m, pool_m, st_m, cnt_m = meta
    del h_m, hn_m, route_m, rt_m, cnt_m
    h_p, hn_p, route_p, rt_p, pool_p, st_p, cnt_p = _mixer(
        x_prompt.reshape(t_prompt, D_MODEL), pool_m, st_m, zero_cnt, weights,
        batch=batch, seq=seq, tb=MIX_TB, lead_pad=0)
    pool_s0 = jnp.pad(state_pool[0], ((0, 0), (POOL_ROWS - POOL_PAD, 0), (0, 0)))
    h_s, hn_s, route_s, rt_s, pool_s, st_s, cnt_s = _mixer(
        x_sample.reshape(t_sample, D_MODEL), pool_s0, state_gla[0].astype(F32),
        cnt_p, weights, batch=dec_batch, seq=dec_seq, tb=SAMPLE_TB, lead_pad=0)

    counts = cnt_s[:, 0].astype(I32)
    padded = (counts + MOE_BLK - 1) // MOE_BLK * MOE_BLK
    ends = jnp.cumsum(padded)
    pstart = ends - padded
    n_blocks = (2 * t_all + N_EXPERTS * (MOE_BLK - 1) + MOE_BLK - 1) // MOE_BLK
    nb = (ends[-1] // MOE_BLK).astype(I32)
    blk_ids = jnp.minimum(jnp.arange(n_blocks, dtype=I32), nb - 1)
    block_e = jnp.sum((ends[None, :] <= (blk_ids * MOE_BLK)[:, None]).astype(I32), axis=1)
    block_e = jnp.minimum(block_e, N_EXPERTS - 1)
    owner = block_e[:, None] == jnp.arange(N_EXPERTS, dtype=I32)
    row_end = jnp.sum(jnp.where(owner, pstart + counts, 0), axis=1)
    block_valid = jnp.clip(row_end - blk_ids * MOE_BLK, 0, MOE_BLK)

    def dest_rows(rt):
        rt = rt.transpose(1, 0, 2).reshape(ROUTE_ROWS, -1)
        onehot = rt[0:2].astype(I32)[..., None] == jnp.arange(N_EXPERTS, dtype=I32)
        return jnp.sum(jnp.where(onehot, pstart, 0), axis=-1) + rt[4:6].astype(I32)

    dest = jnp.concatenate([dest_rows(rt_p), dest_rows(rt_s)], axis=1)
    dest0 = dest[0].reshape(t_all // SC_ROWS, SC_ROWS)
    dest1 = dest[1].reshape(t_all // SC_ROWS, SC_ROWS)
    xs = _dispatch(hn_p, hn_s, dest0, dest1, n_blocks * MOE_BLK)
    ys = _experts((pstart // MOE_BLK).astype(I32), (padded // MOE_BLK).astype(I32),
                  block_valid.astype(I32), nb[None], xs,
                  w_expert_gate[0], w_expert_up[0], w_expert_down[0])
    nfin = norm_final[None, :]
    cp_chunks = t_prompt // SC_ROWS
    part = cp_chunks // COMBINE_PARTS
    y_prompt = None
    for i in range(COMBINE_PARTS):
        ch = slice(i * part, (i + 1) * part)
        rows_i = _gather(ys, dest0[ch], dest1[ch])
        y_prompt = _combine(h_p, route_p, nfin, rows_i, row0=i * part * SC_ROWS, y_prev=y_prompt)
    y_prompt = y_prompt.reshape(batch, seq, D_MODEL)
    rows_s = _gather(ys, dest0[cp_chunks:], dest1[cp_chunks:])
    y_sample = _combine(h_s, route_s, nfin, rows_s).reshape(dec_batch, dec_seq, D_MODEL)
    new_pool_prompt = pool_p[:, POOL_ROWS - POOL_PAD:][None]
    new_gla_prompt = st_p[None]
    new_pool_sample = pool_s[:, POOL_ROWS - POOL_PAD:][None]
    new_gla_sample = st_s[None]
    return (y_prompt, y_sample, new_pool_prompt, new_gla_prompt, new_pool_sample, new_gla_sample)
```

```python
import functools

import jax
import jax.numpy as jnp
from jax import lax
from jax.experimental import pallas as pl
from jax.experimental.pallas import tpu as pltpu
from jax.experimental.pallas import tpu_sc as plsc

F32 = jnp.float32
BF16 = jnp.bfloat16
U32 = jnp.uint32
I32 = jnp.int32

D_MODEL = 1024
N_META = 16
CHUNK = 64
EPS = 1e-6
D_POOL = 512
POOL_WINDOWS = (2, 4, 8, 16)
POOL_GROUP_DIM = 128
POOL_PAD = 15
POOL_ROWS = 16
GLA_HEADS = 4
GLA_DK = 64
GLA_DV = 128
D_QK = 256
D_V = 512
HEAD_GROUP = 2
GATE_RANK = 16
GATE_TAU = 16.0
D_MAIN = D_POOL + 2 * D_QK + 2 * D_V
N_GROUPS = 4
EXPERTS_PER_GROUP = 8
N_EXPERTS = 32
D_EXPERT = 512

LANES = 128
MXU_DIM = 256
HALF = D_MODEL // 2
HN_PLANES = HALF // LANES
ROUTE_ROWS = 8
ROUTER_ROWS = 64
MIX_TB = 512
SAMPLE_TB = 256
FRONT_TILE = 256
FRONT_PLAN = (2, 0, 2, 2, 2, 0)
MOE_BLK = 512
COMBINE_TB = 512
COMBINE_PARTS = 4
VMEM_LIMIT = 48 * 1024 * 1024
SC_CORES = 2
SC_SUBCORES = 16
SC_WORKERS = SC_CORES * SC_SUBCORES
SC_ROWS = 128


def _rms(x, g):
    return x * lax.rsqrt(jnp.mean(x * x, axis=-1, keepdims=True) + EPS) * g


def _dot(a, b):
    return jnp.dot(a, b, preferred_element_type=F32)


def _dot_nt(a, b):
    return lax.dot_general(a, b, (((1,), (1,)), ((), ())), preferred_element_type=F32)


def _pack_planes(x, ref):
    xb = x.astype(BF16)
    lo = lax.bitcast_convert_type(xb[:, :HALF].astype(F32), U32) >> 16
    hi = lax.bitcast_convert_type(xb[:, HALF:].astype(F32), U32) & jnp.uint32(0xFFFF0000)
    packed = lax.bitcast_convert_type(lo | hi, I32)
    for p in range(HN_PLANES):
        ref[p] = packed[:, p * LANES:(p + 1) * LANES]


def _unpack_planes(planes):
    words = [lax.bitcast_convert_type(p, U32) for p in planes]
    los = [lax.bitcast_convert_type(w << 16, F32) for w in words]
    his = [lax.bitcast_convert_type(w & jnp.uint32(0xFFFF0000), F32) for w in words]
    return jnp.concatenate(los + his, axis=-1)


def _dot_tn(a, b):
    return lax.dot_general(a, b, (((0,), (0,)), ((), ())), preferred_element_type=F32)


def _mixer_kernel(x_ref, pool0_ref, st0_ref, cnt0_ref, tril_ref, sup_ref,
                  nmix_ref, wmain_ref, wz_ref, wgu_ref, bgate_ref, wpool_ref, pscale_ref,
                  gnorm_ref, wout_ref, nffn_ref, wr_ref, br_ref,
                  h_ref, hn_ref, route_ref, route_t_ref, pool_out_ref, st_out_ref, cnt_out_ref,
                  ext_ref, st_ref, kbd_ref, vbd_ref, sbd_ref, o_ref, cnt_ref,
                  xs_ref, proj_ref, z_ref, *, tb, nj, lead_pad, chained):
    s = pl.program_id(0)
    back = jnp.maximum(s - 1, 0)
    j = lax.rem(back, nj)
    n_chunks = tb // CHUNK
    wr_slot = lax.rem(s, 2)
    rd_slot = 1 - wr_slot

    @pl.when(s == 0)
    def _():
        kbd_ref[...] = jnp.zeros_like(kbd_ref)
        vbd_ref[...] = jnp.zeros_like(vbd_ref)
        sbd_ref[...] = jnp.zeros_like(sbd_ref)
        xs_ref[1] = jnp.zeros((tb, D_MODEL), F32)
        proj_ref[1] = jnp.zeros((tb, D_MAIN), F32)
        z_ref[1] = jnp.zeros((GATE_RANK, tb), F32)

    @pl.when(s <= 1)
    def _():
        cnt_ref[...] = cnt0_ref[...]

    def put_state(c, hh, st):
        gg, hp = divmod(hh, HEAD_GROUP)
        sbd_ref[c, gg, hp * GLA_DK:(hp + 1) * GLA_DK, hp * GLA_DV:(hp + 1) * GLA_DV] = st.astype(BF16)

    if chained:
        @pl.when(j == 0)
        def _():
            ext_ref[0:POOL_ROWS, :] = pool0_ref[0]
            st_ref[...] = st0_ref[0]

        for hh in range(GLA_HEADS):
            put_state(0, hh, st_ref[hh])
    else:
        for c in range(n_chunks):
            for hh in range(GLA_HEADS):
                put_state(c, hh, st0_ref[c, hh])

    x_new = x_ref[...]
    xn = _rms(x_new, nmix_ref[...]).astype(BF16)
    xs_ref[wr_slot] = x_new

    tiles_done = [0]

    def front_tiles(stage):
        for t in range(tiles_done[0], tiles_done[0] + FRONT_PLAN[stage]):
            cols = slice(t * FRONT_TILE, (t + 1) * FRONT_TILE)
            proj_ref[wr_slot, :, cols] = _dot(xn, wmain_ref[:, cols])
        tiles_done[0] += FRONT_PLAN[stage]

    front_tiles(0)
    z_ref[wr_slot] = _dot_nt(wz_ref[...], xn)

    x = xs_ref[rd_slot]
    z = z_ref[rd_slot]
    u = proj_ref[rd_slot, :, 0:D_POOL]
    q = proj_ref[rd_slot, :, D_POOL:D_POOL + D_QK]
    k = proj_ref[rd_slot, :, D_POOL + D_QK:D_POOL + 2 * D_QK]
    v = proj_ref[rd_slot, :, D_POOL + 2 * D_QK:D_POOL + 2 * D_QK + D_V]
    r = proj_ref[rd_slot, :, D_POOL + 2 * D_QK + D_V:D_MAIN]

    row = lax.broadcasted_iota(I32, (tb, 1), 0)

    pseg = POOL_ROWS + CHUNK
    if chained:
        ext_ref[POOL_ROWS:POOL_ROWS + tb, :] = u
        ext = ext_ref[...]
    else:
        ext = jnp.concatenate(
            [blk for c in range(n_chunks) for blk in (pool0_ref[c], u[c * CHUNK:(c + 1) * CHUNK])], axis=0)
    pooled = []
    for g, w in enumerate(POOL_WINDOWS):
        sl = slice(g * POOL_GROUP_DIM, (g + 1) * POOL_GROUP_DIM)
        acc = ext[:, sl]
        for d in range(g + 1):
            acc = acc + pltpu.roll(acc, 1 << d, axis=0)
        if chained:
            win = acc[POOL_ROWS:, :]
        else:
            win = jnp.concatenate([acc[c * pseg + POOL_ROWS:(c + 1) * pseg] for c in range(n_chunks)], axis=0)
        if lead_pad:
            cnt = jnp.clip(row - lead_pad + 1, 1, w).astype(F32)
            pooled.append(win / cnt - u[:, sl])
        else:
            pooled.append(win * (1.0 / w) - u[:, sl])
    pys = [_dot(jnp.concatenate(pooled[2 * i:2 * i + 2], axis=-1).astype(BF16), wpool_ref[i])
           for i in range(len(POOL_WINDOWS) // 2)]
    pool_y = jnp.concatenate(pys, axis=-1) * pscale_ref[...]
    if chained:
        ext_ref[0:POOL_ROWS, :] = ext_ref[tb:tb + POOL_ROWS, :]
    else:
        for c in range(n_chunks):
            pool_out_ref[c] = u[(c + 1) * CHUNK - POOL_ROWS:(c + 1) * CHUNK]

    gpre = _dot_tn(z.astype(BF16), wgu_ref[...]) + bgate_ref[...]
    log_a = (jnp.minimum(gpre, 0.0) - jnp.log(1.0 + jnp.exp(-jnp.abs(gpre)))) * (1.0 / GATE_TAU)
    if lead_pad:
        log_a = jnp.where(row >= lead_pad, log_a, 0.0)
    a_hi = log_a.astype(BF16)
    a_lo = (log_a - a_hi.astype(F32)).astype(BF16)
    tril = tril_ref[...]
    seg = tril.shape[0]
    bcum = jnp.concatenate(
        [_dot(tril, a_hi[r0:r0 + seg]) + _dot(tril, a_lo[r0:r0 + seg]) for r0 in range(0, tb, seg)], axis=0)
    front_tiles(1)
    eb = jnp.exp(bcum)
    qi = q * (GLA_DK ** -0.5) * eb
    ki = k * jnp.exp(-bcum)

    rr = lax.broadcasted_iota(I32, (CHUNK, HEAD_GROUP * CHUNK), 0)
    cc = lax.broadcasted_iota(I32, (CHUNK, HEAD_GROUP * CHUNK), 1)
    causal = (cc % CHUNK) <= rr

    lasts = [eb[(c + 1) * CHUNK - 1:(c + 1) * CHUNK, :] for c in range(n_chunks)]
    dcol = jnp.concatenate(lasts + [jnp.zeros((LANES - n_chunks, D_QK), F32)], axis=0).T

    n_grp = GLA_HEADS // HEAD_GROUP
    chunk_rows = [slice(c * CHUNK, (c + 1) * CHUNK) for c in range(n_chunks)]
    grp_k = [slice(g * HEAD_GROUP * GLA_DK, (g + 1) * HEAD_GROUP * GLA_DK) for g in range(n_grp)]
    grp_v = [slice(g * HEAD_GROUP * GLA_DV, (g + 1) * HEAD_GROUP * GLA_DV) for g in range(n_grp)]
    qi_b = qi.astype(BF16)
    ki_b = ki.astype(BF16)
    v_b = v.astype(BF16)

    scores = {}
    for c in range(n_chunks):
        for hh in range(GLA_HEADS):
            gg, hp = divmod(hh, HEAD_GROUP)
            kbd_ref[c, gg, hp * CHUNK:(hp + 1) * CHUNK, hp * GLA_DK:(hp + 1) * GLA_DK] = (
                ki_b[chunk_rows[c], hh * GLA_DK:(hh + 1) * GLA_DK])
            vbd_ref[c, gg, hp * CHUNK:(hp + 1) * CHUNK, hp * GLA_DV:(hp + 1) * GLA_DV] = (
                v_b[chunk_rows[c], hh * GLA_DV:(hh + 1) * GLA_DV])
        for gg in range(n_grp):
            scores[c, gg] = _dot_nt(qi_b[chunk_rows[c], grp_k[gg]], kbd_ref[c, gg])

    kvs = {}
    for c in range(n_chunks):
        for hh in range(GLA_HEADS):
            kvs[c, hh] = _dot_tn(ki_b[chunk_rows[c], hh * GLA_DK:(hh + 1) * GLA_DK],
                                 v_b[chunk_rows[c], hh * GLA_DV:(hh + 1) * GLA_DV])
    front_tiles(2)

    for hh in range(GLA_HEADS):
        st = st_ref[hh] if chained else None
        for c in range(n_chunks):
            s_old = st if chained else st0_ref[c, hh]
            s_new = (s_old + kvs[c, hh]) * dcol[hh * GLA_DK:(hh + 1) * GLA_DK, c:c + 1]
            if not chained:
                st_out_ref[c, hh] = s_new
            else:
                st = s_new
                if c + 1 < n_chunks:
                    put_state(c + 1, hh, s_new)
        if chained:
            st_ref[hh] = st

    for c in range(n_chunks):
        for gg in range(n_grp):
            p = jnp.where(causal, scores[c, gg], 0.0).astype(BF16)
            o_ref[chunk_rows[c], grp_v[gg]] = (
                _dot(p, vbd_ref[c, gg]) + _dot(qi_b[chunk_rows[c], grp_k[gg]], sbd_ref[c, gg]))

    o = o_ref[...]
    ons = []
    for hh in range(GLA_HEADS):
        oh = o[:, hh * GLA_DV:(hh + 1) * GLA_DV]
        ons.append(oh * lax.rsqrt(jnp.mean(oh * oh, axis=-1, keepdims=True) + EPS))
    og = jnp.concatenate(ons, axis=-1) * gnorm_ref[...] * (r * jax.nn.sigmoid(r))
    mix = _dot(jnp.concatenate([pool_y, og], axis=-1).astype(BF16), wout_ref[...])
    front_tiles(3)

    h = x + mix
    h_ref[...] = h
    hn = _rms(h, nffn_ref[...]).astype(BF16)
    _pack_planes(hn, hn_ref)

    logits = _dot_nt(wr_ref[...], hn) + br_ref[...]
    front_tiles(4)
    sub = lax.broadcasted_iota(I32, (EXPERTS_PER_GROUP, tb), 0).astype(F32)
    neg = jnp.float32(-jnp.inf)
    big = jnp.float32(EXPERTS_PER_GROUP)
    tile0 = logits[0:EXPERTS_PER_GROUP]
    is_g = sub < N_GROUPS
    gmax = jnp.max(jnp.where(is_g, tile0, neg), axis=0, keepdims=True)
    gsum = jnp.sum(jnp.where(is_g, jnp.exp(tile0 - gmax), 0.0), axis=0, keepdims=True)
    p_g = 1.0 / gsum
    gidx = jnp.min(jnp.where(is_g & (tile0 == gmax), sub, big), axis=0, keepdims=True)
    el = logits[N_GROUPS * EXPERTS_PER_GROUP:(N_GROUPS + 1) * EXPERTS_PER_GROUP]
    for g in range(N_GROUPS - 2, -1, -1):
        el = jnp.where(gidx == g, logits[(g + 1) * EXPERTS_PER_GROUP:(g + 2) * EXPERTS_PER_GROUP], el)
    m1 = jnp.max(el, axis=0, keepdims=True)
    i1 = jnp.min(jnp.where(el == m1, sub, big), axis=0, keepdims=True)
    rest = sub != i1
    m2 = jnp.max(jnp.where(rest, el, neg), axis=0, keepdims=True)
    i2 = jnp.min(jnp.where(rest & (el == m2), sub, big), axis=0, keepdims=True)
    t2 = jnp.exp(m2 - m1)
    den = 1.0 + t2
    g1 = p_g / den
    g2 = p_g * t2 / den
    e1 = gidx * EXPERTS_PER_GROUP + i1
    e2 = gidx * EXPERTS_PER_GROUP + i2

    eid = lax.broadcasted_iota(I32, (N_EXPERTS, tb), 0).astype(F32)
    oh1 = eid == e1
    oh2 = eid == e2
    both = jnp.where(oh1 | oh2, 1.0, 0.0)
    cnt = cnt_ref[...]
    before = _dot(both.astype(BF16), sup_ref[...]) + cnt
    front_tiles(5)
    assert tiles_done[0] * FRONT_TILE == D_MAIN
    pos1 = jnp.sum(jnp.where(oh1, before, 0.0), axis=0, keepdims=True)
    pos2 = jnp.sum(jnp.where(oh2, before, 0.0), axis=0, keepdims=True)
    cnt_new = cnt + jnp.sum(both, axis=1, keepdims=True)
    cnt_ref[...] = cnt_new
    cnt_out_ref[...] = cnt_new

    zero = jnp.zeros_like(e1)
    route_t = jnp.concatenate([e1, e2, g1, g2, pos1, pos2, zero, zero], axis=0)
    route_t_ref[0] = route_t
    route_ref[...] = jnp.concatenate([route_t, jnp.zeros((LANES - ROUTE_ROWS, tb), F32)], axis=0).T

    if chained:
        @pl.when(j == nj - 1)
        def _():
            pool_out_ref[0] = ext_ref[0:POOL_ROWS, :]
            st_out_ref[0] = st_ref[...]


def _mixer(x2d, pool0, st0, cnt0, weights, *, batch, seq, tb, lead_pad):
    chained = tb <= seq
    total_rows = batch * seq
    n_blk = total_rows // tb
    nj = seq // tb if chained else 1
    per_blk = 1 if chained else tb // seq
    assert chained or (seq == CHUNK and batch % per_blk == 0)
    shared = pool0.shape[0] == 1
    front = lambda s: jnp.minimum(s, n_blk - 1)
    back = lambda s: jnp.maximum(s - 1, 0)
    stream = lambda s: back(s) // nj
    st_idx = (lambda s: (0, 0, 0)) if shared else (lambda s: (stream(s), 0, 0))
    gla_idx = (lambda s: (0, 0, 0, 0)) if shared else (lambda s: (stream(s), 0, 0, 0))
    const2 = lambda s: (0, 0)
    tok_out = lambda s: (back(s), 0)

    seg = min(tb, MXU_DIM)
    ii = jnp.arange(seg)
    tril = ((ii[:, None] >= ii[None, :]) & (ii[:, None] // CHUNK == ii[None, :] // CHUNK)).astype(BF16)
    ii = jnp.arange(tb)
    sup = (ii[:, None] < ii[None, :]).astype(BF16)

    in_specs = [
        pl.BlockSpec((tb, D_MODEL), lambda s: (front(s), 0)),
        pl.BlockSpec((per_blk, POOL_ROWS, D_POOL), st_idx),
        pl.BlockSpec((per_blk, GLA_HEADS, GLA_DK, GLA_DV), gla_idx),
        pl.BlockSpec((N_EXPERTS, 1), const2),
        pl.BlockSpec((seg, seg), const2),
        pl.BlockSpec((tb, tb), const2),
    ]
    for wgt in weights:
        in_specs.append(pl.BlockSpec(wgt.shape, (lambda s, n=wgt.ndim: (0,) * n)))
    args = [x2d, pool0, st0, cnt0, tril, sup, *weights]

    out_shape = [
        jax.ShapeDtypeStruct((total_rows, D_MODEL), F32),
        jax.ShapeDtypeStruct((HN_PLANES, total_rows, LANES), I32),
        jax.ShapeDtypeStruct((total_rows, LANES), F32),
        jax.ShapeDtypeStruct((n_blk, ROUTE_ROWS, tb), F32),
        jax.ShapeDtypeStruct((batch, POOL_ROWS, D_POOL), F32),
        jax.ShapeDtypeStruct((batch, GLA_HEADS, GLA_DK, GLA_DV), F32),
        jax.ShapeDtypeStruct((N_EXPERTS, 1), F32),
    ]
    out_specs = [
        pl.BlockSpec((tb, D_MODEL), tok_out),
        pl.BlockSpec((HN_PLANES, tb, LANES), lambda s: (0, back(s), 0)),
        pl.BlockSpec((tb, LANES), tok_out),
        pl.BlockSpec((1, ROUTE_ROWS, tb), lambda s: (back(s), 0, 0)),
        pl.BlockSpec((per_blk, POOL_ROWS, D_POOL), lambda s: (stream(s), 0, 0)),
        pl.BlockSpec((per_blk, GLA_HEADS, GLA_DK, GLA_DV), lambda s: (stream(s), 0, 0, 0)),
        pl.BlockSpec((N_EXPERTS, 1), const2),
    ]
    n_grp = GLA_HEADS // HEAD_GROUP
    scratch = [
        pltpu.VMEM((POOL_ROWS + tb, D_POOL), F32),
        pltpu.VMEM((GLA_HEADS, GLA_DK, GLA_DV), F32),
        pltpu.VMEM((tb // CHUNK, n_grp, HEAD_GROUP * CHUNK, HEAD_GROUP * GLA_DK), BF16),
        pltpu.VMEM((tb // CHUNK, n_grp, HEAD_GROUP * CHUNK, HEAD_GROUP * GLA_DV), BF16),
        pltpu.VMEM((tb // CHUNK, n_grp, HEAD_GROUP * GLA_DK, HEAD_GROUP * GLA_DV), BF16),
        pltpu.VMEM((tb, D_V), F32),
        pltpu.VMEM((N_EXPERTS, 1), F32),
        pltpu.VMEM((2, tb, D_MODEL), F32),
        pltpu.VMEM((2, tb, D_MAIN), F32),
        pltpu.VMEM((2, GATE_RANK, tb), F32),
    ]
    return pl.pallas_call(
        functools.partial(_mixer_kernel, tb=tb, nj=nj, lead_pad=lead_pad, chained=chained),
        grid=(n_blk + 1,),
        in_specs=in_specs,
        out_specs=out_specs,
        out_shape=out_shape,
        scratch_shapes=scratch,
        compiler_params=pltpu.CompilerParams(
            dimension_semantics=("arbitrary",), vmem_limit_bytes=VMEM_LIMIT),
        name=f"mixer_tb{tb}_pad{lead_pad}",
    )(*args)


def _sc_mesh():
    return plsc.VectorSubcoreMesh(core_axis_name="c", subcore_axis_name="s",
                                  num_cores=SC_CORES, num_subcores=SC_SUBCORES)


def _sc_worker():
    return lax.axis_index("s") * SC_CORES + lax.axis_index("c")


def _plane_rows(dest, planes, rows_per_plane):
    offs = (jnp.arange(planes, dtype=I32) * rows_per_plane)[None, :, None]
    return dest[:, None, :] + offs


def _dispatch(hn_p, hn_s, dest0, dest1, n_rows):
    planes, t_p, _ = hn_p.shape
    t_s = hn_s.shape[1]
    n_cp = t_p // SC_ROWS
    cp = n_cp // SC_WORKERS
    n_cs = t_s // SC_ROWS
    assert t_p == cp * SC_ROWS * SC_WORKERS and t_s == n_cs * SC_ROWS and n_cs <= SC_WORKERS
    idx0 = _plane_rows(dest0, planes, n_rows)
    idx1 = _plane_rows(dest1, planes, n_rows)

    def body(hnp_hbm, hns_hbm, d0_hbm, d1_hbm, xs_hbm, rows_v, i0_v, i1_v, is0_v, is1_v, sem_in, sem_out):
        wid = _sc_worker()
        pltpu.sync_copy(d0_hbm.at[pl.ds(wid * cp, cp)], i0_v)
        pltpu.sync_copy(d1_hbm.at[pl.ds(wid * cp, cp)], i1_v)

        def move(src_hbm, src_rows, row0, i0, i1, c):
            loads = [pltpu.async_copy(src_hbm.at[pl.ds(p * src_rows + row0, SC_ROWS)], rows_v.at[p], sem_in)
                     for p in range(planes)]
            for cpy in loads:
                cpy.wait()
            stores = []
            for p in range(planes):
                stores.append(pltpu.async_copy(rows_v.at[p], xs_hbm.at[i0.at[c, p]], sem_out))
                stores.append(pltpu.async_copy(rows_v.at[p], xs_hbm.at[i1.at[c, p]], sem_out))
            for cpy in stores:
                cpy.wait()

        @pl.loop(0, cp)
        def _(c):
            move(hnp_hbm, t_p, (wid * cp + c) * SC_ROWS, i0_v, i1_v, c)

        @pl.when(wid < n_cs)
        def _():
            pltpu.sync_copy(d0_hbm.at[pl.ds(n_cp + wid, 1)], is0_v)
            pltpu.sync_copy(d1_hbm.at[pl.ds(n_cp + wid, 1)], is1_v)
            move(hns_hbm, t_s, wid * SC_ROWS, is0_v, is1_v, 0)

    xs = pl.kernel(
        body,
        out_type=jax.ShapeDtypeStruct((planes * n_rows, LANES), I32),
        mesh=_sc_mesh(),
        scratch_types=[
            pltpu.VMEM((planes, SC_ROWS, LANES), I32),
            pltpu.VMEM((cp, planes, SC_ROWS), I32),
            pltpu.VMEM((cp, planes, SC_ROWS), I32),
            pltpu.VMEM((1, planes, SC_ROWS), I32),
            pltpu.VMEM((1, planes, SC_ROWS), I32),
            pltpu.SemaphoreType.DMA,
            pltpu.SemaphoreType.DMA,
        ],
        name="moe_dispatch_sc",
    )(hn_p.reshape(planes * t_p, LANES), hn_s.reshape(planes * t_s, LANES), idx0, idx1)
    return xs.reshape(planes, n_rows, LANES)


def _gather(ys, dest0, dest1):
    planes, n_rows, _ = ys.shape
    n_chunks = dest0.shape[0]
    n_tok = n_chunks * SC_ROWS
    cpw = max(n_chunks // SC_WORKERS, 1)
    assert n_chunks <= SC_WORKERS or n_chunks == cpw * SC_WORKERS
    idx = (_plane_rows(dest0, planes, n_rows), _plane_rows(dest1, planes, n_rows))

    def body(ys_hbm, d0_hbm, d1_hbm, out_hbm, rows_v, i0_v, i1_v, sem_in, sem_out):
        wid = _sc_worker()

        def work():
            pltpu.sync_copy(d0_hbm.at[pl.ds(wid * cpw, cpw)], i0_v)
            pltpu.sync_copy(d1_hbm.at[pl.ds(wid * cpw, cpw)], i1_v)

            @pl.loop(0, cpw)
            def _(c):
                row0 = (wid * cpw + c) * SC_ROWS
                for j, i_v in enumerate((i0_v, i1_v)):
                    loads = [pltpu.async_copy(ys_hbm.at[i_v.at[c, p]], rows_v.at[p], sem_in)
                             for p in range(planes)]
                    for cpy in loads:
                        cpy.wait()
                    stores = [
                        pltpu.async_copy(
                            rows_v.at[p], out_hbm.at[pl.ds((j * planes + p) * n_tok + row0, SC_ROWS)], sem_out)
                        for p in range(planes)]
                    for cpy in stores:
                        cpy.wait()

        if n_chunks < SC_WORKERS:
            pl.when(wid < n_chunks)(work)
        else:
            work()

    out = pl.kernel(
        body,
        out_type=jax.ShapeDtypeStruct((2 * planes * n_tok, LANES), ys.dtype),
        mesh=_sc_mesh(),
        scratch_types=[
            pltpu.VMEM((planes, SC_ROWS, LANES), ys.dtype),
            pltpu.VMEM((cpw, planes, SC_ROWS), I32),
            pltpu.VMEM((cpw, planes, SC_ROWS), I32),
            pltpu.SemaphoreType.DMA,
            pltpu.SemaphoreType.DMA,
        ],
        name="moe_gather_sc",
    )(ys.reshape(planes * n_rows, LANES), *idx)
    return out.reshape(2, planes, n_tok, LANES)


def _expert_kernel(be_ref, bv_ref, nb_ref, xs_ref, wg_ref, wu_ref, wd_ref, ys_ref, wgu_s, wd_s):
    i = pl.program_id(0)
    live = i < nb_ref[0]
    prev = be_ref[jnp.maximum(i - 1, 0)]
    fresh = (i == 0) | (be_ref[i] != prev)

    @pl.when(live & fresh)
    def _():
        wgu_s[:, 0:D_EXPERT] = wg_ref[0].astype(BF16)
        wgu_s[:, D_EXPERT:2 * D_EXPERT] = wu_ref[0].astype(BF16)
        wd_s[...] = wd_ref[0].astype(BF16)

    @pl.when(live)
    def _():
        valid = lax.broadcasted_iota(I32, (MOE_BLK, LANES), 0) < bv_ref[i]
        xb = _unpack_planes([jnp.where(valid, xs_ref[p], 0) for p in range(HN_PLANES)]).astype(BF16)
        gu = _dot(xb, wgu_s[...])
        gate = gu[:, 0:D_EXPERT]
        hmid = gate * jax.nn.sigmoid(gate) * gu[:, D_EXPERT:]
        _pack_planes(_dot(hmid.astype(BF16), wd_s[...]), ys_ref)

    @pl.when(jnp.logical_not(live))
    def _():
        ys_ref[...] = jnp.zeros_like(ys_ref)


def _experts(block_e, block_valid, nb, xs, w_eg, w_eu, w_ed):
    n_rows = xs.shape[1]
    n_blocks = n_rows // MOE_BLK
    row_idx = lambda i, be, bv, nb: (0, jnp.minimum(i, nb[0] - 1), 0)
    w_idx = lambda i, be, bv, nb: (be[i], 0, 0)
    grid_spec = pltpu.PrefetchScalarGridSpec(
        num_scalar_prefetch=3,
        grid=(n_blocks,),
        in_specs=[
            pl.BlockSpec((HN_PLANES, MOE_BLK, LANES), row_idx),
            pl.BlockSpec((1, D_MODEL, D_EXPERT), w_idx),
            pl.BlockSpec((1, D_MODEL, D_EXPERT), w_idx),
            pl.BlockSpec((1, D_EXPERT, D_MODEL), w_idx),
        ],
        out_specs=pl.BlockSpec((HN_PLANES, MOE_BLK, LANES), lambda i, be, bv, nb: (0, i, 0)),
        scratch_shapes=[
            pltpu.VMEM((D_MODEL, 2 * D_EXPERT), BF16),
            pltpu.VMEM((D_EXPERT, D_MODEL), BF16),
        ],
    )
    return pl.pallas_call(
        _expert_kernel,
        grid_spec=grid_spec,
        out_shape=jax.ShapeDtypeStruct((HN_PLANES, n_rows, LANES), I32),
        compiler_params=pltpu.CompilerParams(
            dimension_semantics=("arbitrary",), vmem_limit_bytes=VMEM_LIMIT),
        name="moe_experts",
    )(block_e, block_valid, nb, xs, w_eg, w_eu, w_ed)


def _combine_kernel(h_ref, route_ref, nfin_ref, rows_ref, *rest):
    y_ref = rest[-1]
    route = route_ref[...]
    ys1 = _unpack_planes([rows_ref[0, p] for p in range(HN_PLANES)])
    ys2 = _unpack_planes([rows_ref[1, p] for p in range(HN_PLANES)])
    out = h_ref[...] + (ys1 * route[:, 2:3] + ys2 * route[:, 3:4])
    y_ref[...] = _rms(out, nfin_ref[...])


def _combine(h, route, norm_final, rows, *, row0=0, y_prev=None):
    total = h.shape[0]
    tb = COMBINE_TB
    blk0 = row0 // tb
    args = [h, route, norm_final, rows]
    in_specs = [
        pl.BlockSpec((tb, D_MODEL), lambda i: (blk0 + i, 0)),
        pl.BlockSpec((tb, LANES), lambda i: (blk0 + i, 0)),
        pl.BlockSpec((1, D_MODEL), lambda i: (0, 0)),
        pl.BlockSpec((2, HN_PLANES, tb, LANES), lambda i: (0, 0, i, 0)),
    ]
    aliases = {}
    if y_prev is not None:
        aliases[len(args)] = 0
        args.append(y_prev)
        in_specs.append(pl.BlockSpec(memory_space=pl.ANY))
    return pl.pallas_call(
        _combine_kernel,
        grid=(rows.shape[2] // tb,),
        in_specs=in_specs,
        out_specs=pl.BlockSpec((tb, D_MODEL), lambda i: (blk0 + i, 0)),
        out_shape=jax.ShapeDtypeStruct((total, D_MODEL), F32),
        input_output_aliases=aliases,
        compiler_params=pltpu.CompilerParams(
            dimension_semantics=("arbitrary",), vmem_limit_bytes=VMEM_LIMIT),
        name="moe_combine",
    )(*args)


def kernel(x_prompt, x_sample, state_pool, state_gla, meta_tokens, norm_mix, w_in, w_gate_up, b_gate, w_pool, pool_scale, gla_norm, w_out, norm_ffn, w_router_group, b_router_group, w_router_expert, b_router_expert, w_expert_gate, w_expert_up, w_expert_down, norm_final):
    assert w_in.shape[0] == 1, "one encoder layer"
    batch, seq, _ = x_prompt.shape
    dec_batch, dec_seq, _ = x_sample.shape
    assert seq % MIX_TB == 0 and dec_seq == CHUNK and N_META <= CHUNK
    t_prompt = batch * seq
    t_sample = dec_batch * dec_seq
    t_all = t_prompt + t_sample
    assert t_prompt % COMBINE_TB == 0 and t_sample % COMBINE_TB == 0

    w_in0 = w_in[0]
    gpad = EXPERTS_PER_GROUP - N_GROUPS
    rpad = ROUTER_ROWS - EXPERTS_PER_GROUP - N_EXPERTS
    w_router = jnp.concatenate([
        w_router_group[0].T, jnp.zeros((gpad, D_MODEL), F32),
        w_router_expert[0].T, jnp.zeros((rpad, D_MODEL), F32)], axis=0)
    b_router = jnp.concatenate([
        b_router_group[0], jnp.zeros((gpad,), F32), b_router_expert[0], jnp.zeros((rpad,), F32)])
    zg = jnp.zeros((POOL_GROUP_DIM, POOL_GROUP_DIM), F32)
    w_pool_pairs = jnp.stack([
        jnp.block([[w_pool[0, 2 * i], zg], [zg, w_pool[0, 2 * i + 1]]]) for i in range(len(POOL_WINDOWS) // 2)])
    weights = (
        norm_mix[0][None, :],
        w_in0[:, :D_MAIN].astype(BF16),
        w_in0[:, D_MAIN:].T.astype(BF16),
        w_gate_up[0].astype(BF16),
        b_gate[0][None, :],
        w_pool_pairs.astype(BF16),
        pool_scale[0][None, :],
        gla_norm[0][None, :],
        w_out[0].astype(BF16),
        norm_ffn[0][None, :],
        w_router.astype(BF16),
        b_router[:, None],
    )

    zero_cnt = jnp.zeros((N_EXPERTS, 1), F32)
    x_meta = jnp.pad(meta_tokens.astype(F32), ((CHUNK - N_META, 0), (0, 0)))
    meta = _mixer(x_meta, jnp.zeros((1, POOL_ROWS, D_POOL), F32),
                  jnp.zeros((1, GLA_HEADS, GLA_DK, GLA_DV), F32), zero_cnt,
                  weights, batch=1, seq=CHUNK, tb=CHUNK, lead_pad=CHUNK - N_META)
    h_m, hn_m, route_m, rt_m, pool_m, st_m, cnt_m = meta
    del h_m, hn_m, route_m, rt_m, cnt_m
    h_p, hn_p, route_p, rt_p, pool_p, st_p, cnt_p = _mixer(
        x_prompt.reshape(t_prompt, D_MODEL), pool_m, st_m, zero_cnt, weights,
        batch=batch, seq=seq, tb=MIX_TB, lead_pad=0)
    pool_s0 = jnp.pad(state_pool[0], ((0, 0), (POOL_ROWS - POOL_PAD, 0), (0, 0)))
    h_s, hn_s, route_s, rt_s, pool_s, st_s, cnt_s = _mixer(
        x_sample.reshape(t_sample, D_MODEL), pool_s0, state_gla[0].astype(F32),
        cnt_p, weights, batch=dec_batch, seq=dec_seq, tb=SAMPLE_TB, lead_pad=0)

    counts = cnt_s[:, 0].astype(I32)
    padded = (counts + MOE_BLK - 1) // MOE_BLK * MOE_BLK
    ends = jnp.cumsum(padded)
    pstart = ends - padded
    n_blocks = (2 * t_all + N_EXPERTS * (MOE_BLK - 1) + MOE_BLK - 1) // MOE_BLK
    nb = (ends[-1] // MOE_BLK).astype(I32)
    blk_ids = jnp.minimum(jnp.arange(n_blocks, dtype=I32), nb - 1)
    block_e = jnp.sum((ends[None, :] <= (blk_ids * MOE_BLK)[:, None]).astype(I32), axis=1)
    block_e = jnp.minimum(block_e, N_EXPERTS - 1)
    owner = block_e[:, None] == jnp.arange(N_EXPERTS, dtype=I32)
    row_end = jnp.sum(jnp.where(owner, pstart + counts, 0), axis=1)
    block_valid = jnp.clip(row_end - blk_ids * MOE_BLK, 0, MOE_BLK)

    def dest_rows(rt):
        rt = rt.transpose(1, 0, 2).reshape(ROUTE_ROWS, -1)
        onehot = rt[0:2].astype(I32)[..., None] == jnp.arange(N_EXPERTS, dtype=I32)
        return jnp.sum(jnp.where(onehot, pstart, 0), axis=-1) + rt[4:6].astype(I32)

    dest = jnp.concatenate([dest_rows(rt_p), dest_rows(rt_s)], axis=1)
    dest0 = dest[0].reshape(t_all // SC_ROWS, SC_ROWS)
    dest1 = dest[1].reshape(t_all // SC_ROWS, SC_ROWS)
    xs = _dispatch(hn_p, hn_s, dest0, dest1, n_blocks * MOE_BLK)
    ys = _experts(block_e, block_valid.astype(I32), nb[None], xs,
                  w_expert_gate[0], w_expert_up[0], w_expert_down[0])
    nfin = norm_final[None, :]
    cp_chunks = t_prompt // SC_ROWS
    part = cp_chunks // COMBINE_PARTS
    y_prompt = None
    for i in range(COMBINE_PARTS):
        ch = slice(i * part, (i + 1) * part)
        rows_i = _gather(ys, dest0[ch], dest1[ch])
        y_prompt = _combine(h_p, route_p, nfin, rows_i, row0=i * part * SC_ROWS, y_prev=y_prompt)
    y_prompt = y_prompt.reshape(batch, seq, D_MODEL)
    rows_s = _gather(ys, dest0[cp_chunks:], dest1[cp_chunks:])
    y_sample = _combine(h_s, route_s, nfin, rows_s).reshape(dec_batch, dec_seq, D_MODEL)
    new_pool_prompt = pool_p[:, POOL_ROWS - POOL_PAD:][None]
    new_gla_prompt = st_p[None]
    new_pool_sample = pool_s[:, POOL_ROWS - POOL_PAD:][None]
    new_gla_sample = st_s[None]
    return (y_prompt, y_sample, new_pool_prompt, new_gla_prompt, new_pool_sample, new_gla_sample)
```

```python
import functools

import jax
import jax.numpy as jnp
from jax import lax
from jax.experimental import pallas as pl
from jax.experimental.pallas import tpu as pltpu
from jax.experimental.pallas import tpu_sc as plsc

F32 = jnp.float32
BF16 = jnp.bfloat16
U32 = jnp.uint32
I32 = jnp.int32

D_MODEL = 1024
N_META = 16
CHUNK = 64
EPS = 1e-6
D_POOL = 512
POOL_WINDOWS = (2, 4, 8, 16)
POOL_GROUP_DIM = 128
POOL_PAD = 15
POOL_ROWS = 16
GLA_HEADS = 4
GLA_DK = 64
GLA_DV = 128
D_QK = 256
D_V = 512
HEAD_GROUP = 2
GATE_RANK = 16
GATE_TAU = 16.0
D_MAIN = D_POOL + 2 * D_QK + 2 * D_V
N_GROUPS = 4
EXPERTS_PER_GROUP = 8
N_EXPERTS = 32
D_EXPERT = 512

LANES = 128
MXU_DIM = 256
HALF = D_MODEL // 2
HN_PLANES = HALF // LANES
ROUTE_ROWS = 8
ROUTER_ROWS = 64
MIX_TB = 512
SAMPLE_TB = 256
FRONT_TILE = 256
FRONT_PLAN = (3, 0, 1, 2, 2, 0)
MOE_BLK = 1024
COMBINE_TB = 512
COMBINE_PARTS = 4
VMEM_LIMIT = 48 * 1024 * 1024
SC_CORES = 2
SC_SUBCORES = 16
SC_WORKERS = SC_CORES * SC_SUBCORES
SC_ROWS = 128


def _rms(x, g):
    return x * lax.rsqrt(jnp.mean(x * x, axis=-1, keepdims=True) + EPS) * g


def _dot(a, b):
    return jnp.dot(a, b, preferred_element_type=F32)


def _dot_nt(a, b):
    return lax.dot_general(a, b, (((1,), (1,)), ((), ())), preferred_element_type=F32)


def _pack_planes(x, ref):
    xb = x.astype(BF16)
    lo = lax.bitcast_convert_type(xb[:, :HALF].astype(F32), U32) >> 16
    hi = lax.bitcast_convert_type(xb[:, HALF:].astype(F32), U32) & jnp.uint32(0xFFFF0000)
    packed = lax.bitcast_convert_type(lo | hi, I32)
    for p in range(HN_PLANES):
        ref[p] = packed[:, p * LANES:(p + 1) * LANES]


def _unpack_planes(planes):
    words = [lax.bitcast_convert_type(p, U32) for p in planes]
    los = [lax.bitcast_convert_type(w << 16, F32) for w in words]
    his = [lax.bitcast_convert_type(w & jnp.uint32(0xFFFF0000), F32) for w in words]
    return jnp.concatenate(los + his, axis=-1)


def _dot_tn(a, b):
    return lax.dot_general(a, b, (((0,), (0,)), ((), ())), preferred_element_type=F32)


def _mixer_kernel(x_ref, pool0_ref, st0_ref, cnt0_ref, tril_ref, sup_ref,
                  nmix_ref, wmain_ref, wz_ref, wgu_ref, bgate_ref, wpool_ref, pscale_ref,
                  gnorm_ref, wout_ref, nffn_ref, wr_ref, br_ref,
                  h_ref, hn_ref, route_ref, route_t_ref, pool_out_ref, st_out_ref, cnt_out_ref,
                  ext_ref, st_ref, kbd_ref, vbd_ref, sbd_ref, o_ref, cnt_ref,
                  xs_ref, proj_ref, z_ref, *, tb, nj, lead_pad, chained):
    s = pl.program_id(0)
    back = jnp.maximum(s - 1, 0)
    j = lax.rem(back, nj)
    n_chunks = tb // CHUNK
    wr_slot = lax.rem(s, 2)
    rd_slot = 1 - wr_slot

    @pl.when(s == 0)
    def _():
        kbd_ref[...] = jnp.zeros_like(kbd_ref)
        vbd_ref[...] = jnp.zeros_like(vbd_ref)
        sbd_ref[...] = jnp.zeros_like(sbd_ref)
        xs_ref[1] = jnp.zeros((tb, D_MODEL), F32)
        proj_ref[1] = jnp.zeros((tb, D_MAIN), F32)
        z_ref[1] = jnp.zeros((GATE_RANK, tb), F32)

    @pl.when(s <= 1)
    def _():
        cnt_ref[...] = cnt0_ref[...]

    def put_state(c, hh, st):
        gg, hp = divmod(hh, HEAD_GROUP)
        sbd_ref[c, gg, hp * GLA_DK:(hp + 1) * GLA_DK, hp * GLA_DV:(hp + 1) * GLA_DV] = st.astype(BF16)

    if chained:
        @pl.when(j == 0)
        def _():
            ext_ref[0:POOL_ROWS, :] = pool0_ref[0]
            st_ref[...] = st0_ref[0]

        for hh in range(GLA_HEADS):
            put_state(0, hh, st_ref[hh])
    else:
        for c in range(n_chunks):
            for hh in range(GLA_HEADS):
                put_state(c, hh, st0_ref[c, hh])

    x_new = x_ref[...]
    xn = _rms(x_new, nmix_ref[...]).astype(BF16)
    xs_ref[wr_slot] = x_new

    tiles_done = [0]

    def front_tiles(stage):
        for t in range(tiles_done[0], tiles_done[0] + FRONT_PLAN[stage]):
            cols = slice(t * FRONT_TILE, (t + 1) * FRONT_TILE)
            proj_ref[wr_slot, :, cols] = _dot(xn, wmain_ref[:, cols])
        tiles_done[0] += FRONT_PLAN[stage]

    front_tiles(0)
    z_ref[wr_slot] = _dot_nt(wz_ref[...], xn)

    x = xs_ref[rd_slot]
    z = z_ref[rd_slot]
    u = proj_ref[rd_slot, :, 0:D_POOL]
    q = proj_ref[rd_slot, :, D_POOL:D_POOL + D_QK]
    k = proj_ref[rd_slot, :, D_POOL + D_QK:D_POOL + 2 * D_QK]
    v = proj_ref[rd_slot, :, D_POOL + 2 * D_QK:D_POOL + 2 * D_QK + D_V]
    r = proj_ref[rd_slot, :, D_POOL + 2 * D_QK + D_V:D_MAIN]

    row = lax.broadcasted_iota(I32, (tb, 1), 0)

    pseg = POOL_ROWS + CHUNK
    if chained:
        ext_ref[POOL_ROWS:POOL_ROWS + tb, :] = u
        ext = ext_ref[...]
    else:
        ext = jnp.concatenate(
            [blk for c in range(n_chunks) for blk in (pool0_ref[c], u[c * CHUNK:(c + 1) * CHUNK])], axis=0)
    pooled = []
    for g, w in enumerate(POOL_WINDOWS):
        sl = slice(g * POOL_GROUP_DIM, (g + 1) * POOL_GROUP_DIM)
        acc = ext[:, sl]
        for d in range(g + 1):
            acc = acc + pltpu.roll(acc, 1 << d, axis=0)
        if chained:
            win = acc[POOL_ROWS:, :]
        else:
            win = jnp.concatenate([acc[c * pseg + POOL_ROWS:(c + 1) * pseg] for c in range(n_chunks)], axis=0)
        if lead_pad:
            cnt = jnp.clip(row - lead_pad + 1, 1, w).astype(F32)
            pooled.append(win / cnt - u[:, sl])
        else:
            pooled.append(win * (1.0 / w) - u[:, sl])
    pys = [_dot(jnp.concatenate(pooled[2 * i:2 * i + 2], axis=-1).astype(BF16), wpool_ref[i])
           for i in range(len(POOL_WINDOWS) // 2)]
    pool_y = jnp.concatenate(pys, axis=-1) * pscale_ref[...]
    if chained:
        ext_ref[0:POOL_ROWS, :] = ext_ref[tb:tb + POOL_ROWS, :]
    else:
        for c in range(n_chunks):
            pool_out_ref[c] = u[(c + 1) * CHUNK - POOL_ROWS:(c + 1) * CHUNK]

    gpre = _dot_tn(z.astype(BF16), wgu_ref[...]) + bgate_ref[...]
    log_a = (jnp.minimum(gpre, 0.0) - jnp.log(1.0 + jnp.exp(-jnp.abs(gpre)))) * (1.0 / GATE_TAU)
    if lead_pad:
        log_a = jnp.where(row >= lead_pad, log_a, 0.0)
    a_hi = log_a.astype(BF16)
    a_lo = (log_a - a_hi.astype(F32)).astype(BF16)
    tril = tril_ref[...]
    seg = tril.shape[0]
    bcum = jnp.concatenate(
        [_dot(tril, a_hi[r0:r0 + seg]) + _dot(tril, a_lo[r0:r0 + seg]) for r0 in range(0, tb, seg)], axis=0)
    front_tiles(1)
    eb = jnp.exp(bcum)
    qi = q * (GLA_DK ** -0.5) * eb
    ki = k * jnp.exp(-bcum)

    rr = lax.broadcasted_iota(I32, (CHUNK, HEAD_GROUP * CHUNK), 0)
    cc = lax.broadcasted_iota(I32, (CHUNK, HEAD_GROUP * CHUNK), 1)
    causal = (cc % CHUNK) <= rr

    lasts = [eb[(c + 1) * CHUNK - 1:(c + 1) * CHUNK, :] for c in range(n_chunks)]
    dcol = jnp.concatenate(lasts + [jnp.zeros((LANES - n_chunks, D_QK), F32)], axis=0).T

    n_grp = GLA_HEADS // HEAD_GROUP
    chunk_rows = [slice(c * CHUNK, (c + 1) * CHUNK) for c in range(n_chunks)]
    grp_k = [slice(g * HEAD_GROUP * GLA_DK, (g + 1) * HEAD_GROUP * GLA_DK) for g in range(n_grp)]
    grp_v = [slice(g * HEAD_GROUP * GLA_DV, (g + 1) * HEAD_GROUP * GLA_DV) for g in range(n_grp)]
    qi_b = qi.astype(BF16)
    ki_b = ki.astype(BF16)
    v_b = v.astype(BF16)

    scores = {}
    for c in range(n_chunks):
        for hh in range(GLA_HEADS):
            gg, hp = divmod(hh, HEAD_GROUP)
            kbd_ref[c, gg, hp * CHUNK:(hp + 1) * CHUNK, hp * GLA_DK:(hp + 1) * GLA_DK] = (
                ki_b[chunk_rows[c], hh * GLA_DK:(hh + 1) * GLA_DK])
            vbd_ref[c, gg, hp * CHUNK:(hp + 1) * CHUNK, hp * GLA_DV:(hp + 1) * GLA_DV] = (
                v_b[chunk_rows[c], hh * GLA_DV:(hh + 1) * GLA_DV])
        for gg in range(n_grp):
            scores[c, gg] = _dot_nt(qi_b[chunk_rows[c], grp_k[gg]], kbd_ref[c, gg])

    kvs = {}
    for c in range(n_chunks):
        for hh in range(GLA_HEADS):
            kvs[c, hh] = _dot_tn(ki_b[chunk_rows[c], hh * GLA_DK:(hh + 1) * GLA_DK],
                                 v_b[chunk_rows[c], hh * GLA_DV:(hh + 1) * GLA_DV])
    front_tiles(2)

    for hh in range(GLA_HEADS):
        st = st_ref[hh] if chained else None
        for c in range(n_chunks):
            s_old = st if chained else st0_ref[c, hh]
            s_new = (s_old + kvs[c, hh]) * dcol[hh * GLA_DK:(hh + 1) * GLA_DK, c:c + 1]
            if not chained:
                st_out_ref[c, hh] = s_new
            else:
                st = s_new
                if c + 1 < n_chunks:
                    put_state(c + 1, hh, s_new)
        if chained:
            st_ref[hh] = st

    for c in range(n_chunks):
        for gg in range(n_grp):
            p = jnp.where(causal, scores[c, gg], 0.0).astype(BF16)
            o_ref[chunk_rows[c], grp_v[gg]] = (
                _dot(p, vbd_ref[c, gg]) + _dot(qi_b[chunk_rows[c], grp_k[gg]], sbd_ref[c, gg]))

    o = o_ref[...]
    ons = []
    for hh in range(GLA_HEADS):
        oh = o[:, hh * GLA_DV:(hh + 1) * GLA_DV]
        ons.append(oh * lax.rsqrt(jnp.mean(oh * oh, axis=-1, keepdims=True) + EPS))
    og = jnp.concatenate(ons, axis=-1) * gnorm_ref[...] * (r * jax.nn.sigmoid(r))
    mix = _dot(jnp.concatenate([pool_y, og], axis=-1).astype(BF16), wout_ref[...])
    front_tiles(3)

    h = x + mix
    h_ref[...] = h
    hn = _rms(h, nffn_ref[...]).astype(BF16)
    _pack_planes(hn, hn_ref)

    logits = _dot_nt(wr_ref[...], hn) + br_ref[...]
    front_tiles(4)
    sub = lax.broadcasted_iota(I32, (EXPERTS_PER_GROUP, tb), 0).astype(F32)
    neg = jnp.float32(-jnp.inf)
    big = jnp.float32(EXPERTS_PER_GROUP)
    tile0 = logits[0:EXPERTS_PER_GROUP]
    is_g = sub < N_GROUPS
    gmax = jnp.max(jnp.where(is_g, tile0, neg), axis=0, keepdims=True)
    gsum = jnp.sum(jnp.where(is_g, jnp.exp(tile0 - gmax), 0.0), axis=0, keepdims=True)
    p_g = 1.0 / gsum
    gidx = jnp.min(jnp.where(is_g & (tile0 == gmax), sub, big), axis=0, keepdims=True)
    el = logits[N_GROUPS * EXPERTS_PER_GROUP:(N_GROUPS + 1) * EXPERTS_PER_GROUP]
    for g in range(N_GROUPS - 2, -1, -1):
        el = jnp.where(gidx == g, logits[(g + 1) * EXPERTS_PER_GROUP:(g + 2) * EXPERTS_PER_GROUP], el)
    m1 = jnp.max(el, axis=0, keepdims=True)
    i1 = jnp.min(jnp.where(el == m1, sub, big), axis=0, keepdims=True)
    rest = sub != i1
    m2 = jnp.max(jnp.where(rest, el, neg), axis=0, keepdims=True)
    i2 = jnp.min(jnp.where(rest & (el == m2), sub, big), axis=0, keepdims=True)
    t2 = jnp.exp(m2 - m1)
    den = 1.0 + t2
    g1 = p_g / den
    g2 = p_g * t2 / den
    e1 = gidx * EXPERTS_PER_GROUP + i1
    e2 = gidx * EXPERTS_PER_GROUP + i2

    eid = lax.broadcasted_iota(I32, (N_EXPERTS, tb), 0).astype(F32)
    oh1 = eid == e1
    oh2 = eid == e2
    both = jnp.where(oh1 | oh2, 1.0, 0.0)
    cnt = cnt_ref[...]
    before = _dot(both.astype(BF16), sup_ref[...]) + cnt
    front_tiles(5)
    assert tiles_done[0] * FRONT_TILE == D_MAIN
    pos1 = jnp.sum(jnp.where(oh1, before, 0.0), axis=0, keepdims=True)
    pos2 = jnp.sum(jnp.where(oh2, before, 0.0), axis=0, keepdims=True)
    cnt_new = cnt + jnp.sum(both, axis=1, keepdims=True)
    cnt_ref[...] = cnt_new
    cnt_out_ref[...] = cnt_new

    zero = jnp.zeros_like(e1)
    route_t = jnp.concatenate([e1, e2, g1, g2, pos1, pos2, zero, zero], axis=0)
    route_t_ref[0] = route_t
    route_ref[...] = jnp.concatenate([route_t, jnp.zeros((LANES - ROUTE_ROWS, tb), F32)], axis=0).T

    if chained:
        @pl.when(j == nj - 1)
        def _():
            pool_out_ref[0] = ext_ref[0:POOL_ROWS, :]
            st_out_ref[0] = st_ref[...]


def _mixer(x2d, pool0, st0, cnt0, weights, *, batch, seq, tb, lead_pad):
    chained = tb <= seq
    total_rows = batch * seq
    n_blk = total_rows // tb
    nj = seq // tb if chained else 1
    per_blk = 1 if chained else tb // seq
    assert chained or (seq == CHUNK and batch % per_blk == 0)
    shared = pool0.shape[0] == 1
    front = lambda s: jnp.minimum(s, n_blk - 1)
    back = lambda s: jnp.maximum(s - 1, 0)
    stream = lambda s: back(s) // nj
    st_idx = (lambda s: (0, 0, 0)) if shared else (lambda s: (stream(s), 0, 0))
    gla_idx = (lambda s: (0, 0, 0, 0)) if shared else (lambda s: (stream(s), 0, 0, 0))
    const2 = lambda s: (0, 0)
    tok_out = lambda s: (back(s), 0)

    seg = min(tb, MXU_DIM)
    ii = jnp.arange(seg)
    tril = ((ii[:, None] >= ii[None, :]) & (ii[:, None] // CHUNK == ii[None, :] // CHUNK)).astype(BF16)
    ii = jnp.arange(tb)
    sup = (ii[:, None] < ii[None, :]).astype(BF16)

    in_specs = [
        pl.BlockSpec((tb, D_MODEL), lambda s: (front(s), 0)),
        pl.BlockSpec((per_blk, POOL_ROWS, D_POOL), st_idx),
        pl.BlockSpec((per_blk, GLA_HEADS, GLA_DK, GLA_DV), gla_idx),
        pl.BlockSpec((N_EXPERTS, 1), const2),
        pl.BlockSpec((seg, seg), const2),
        pl.BlockSpec((tb, tb), const2),
    ]
    for wgt in weights:
        in_specs.append(pl.BlockSpec(wgt.shape, (lambda s, n=wgt.ndim: (0,) * n)))
    args = [x2d, pool0, st0, cnt0, tril, sup, *weights]

    out_shape = [
        jax.ShapeDtypeStruct((total_rows, D_MODEL), F32),
        jax.ShapeDtypeStruct((HN_PLANES, total_rows, LANES), I32),
        jax.ShapeDtypeStruct((total_rows, LANES), F32),
        jax.ShapeDtypeStruct((n_blk, ROUTE_ROWS, tb), F32),
        jax.ShapeDtypeStruct((batch, POOL_ROWS, D_POOL), F32),
        jax.ShapeDtypeStruct((batch, GLA_HEADS, GLA_DK, GLA_DV), F32),
        jax.ShapeDtypeStruct((N_EXPERTS, 1), F32),
    ]
    out_specs = [
        pl.BlockSpec((tb, D_MODEL), tok_out),
        pl.BlockSpec((HN_PLANES, tb, LANES), lambda s: (0, back(s), 0)),
        pl.BlockSpec((tb, LANES), tok_out),
        pl.BlockSpec((1, ROUTE_ROWS, tb), lambda s: (back(s), 0, 0)),
        pl.BlockSpec((per_blk, POOL_ROWS, D_POOL), lambda s: (stream(s), 0, 0)),
        pl.BlockSpec((per_blk, GLA_HEADS, GLA_DK, GLA_DV), lambda s: (stream(s), 0, 0, 0)),
        pl.BlockSpec((N_EXPERTS, 1), const2),
    ]
    n_grp = GLA_HEADS // HEAD_GROUP
    scratch = [
        pltpu.VMEM((POOL_ROWS + tb, D_POOL), F32),
        pltpu.VMEM((GLA_HEADS, GLA_DK, GLA_DV), F32),
        pltpu.VMEM((tb // CHUNK, n_grp, HEAD_GROUP * CHUNK, HEAD_GROUP * GLA_DK), BF16),
        pltpu.VMEM((tb // CHUNK, n_grp, HEAD_GROUP * CHUNK, HEAD_GROUP * GLA_DV), BF16),
        pltpu.VMEM((tb // CHUNK, n_grp, HEAD_GROUP * GLA_DK, HEAD_GROUP * GLA_DV), BF16),
        pltpu.VMEM((tb, D_V), F32),
        pltpu.VMEM((N_EXPERTS, 1), F32),
        pltpu.VMEM((2, tb, D_MODEL), F32),
        pltpu.VMEM((2, tb, D_MAIN), F32),
        pltpu.VMEM((2, GATE_RANK, tb), F32),
    ]
    return pl.pallas_call(
        functools.partial(_mixer_kernel, tb=tb, nj=nj, lead_pad=lead_pad, chained=chained),
        grid=(n_blk + 1,),
        in_specs=in_specs,
        out_specs=out_specs,
        out_shape=out_shape,
        scratch_shapes=scratch,
        compiler_params=pltpu.CompilerParams(
            dimension_semantics=("arbitrary",), vmem_limit_bytes=VMEM_LIMIT),
        name=f"mixer_tb{tb}_pad{lead_pad}",
    )(*args)


def _sc_mesh():
    return plsc.VectorSubcoreMesh(core_axis_name="c", subcore_axis_name="s",
                                  num_cores=SC_CORES, num_subcores=SC_SUBCORES)


def _sc_worker():
    return lax.axis_index("s") * SC_CORES + lax.axis_index("c")


def _plane_rows(dest, planes, rows_per_plane):
    offs = (jnp.arange(planes, dtype=I32) * rows_per_plane)[None, :, None]
    return dest[:, None, :] + offs


def _dispatch(hn_p, hn_s, dest0, dest1, n_rows):
    planes, t_p, _ = hn_p.shape
    t_s = hn_s.shape[1]
    n_cp = t_p // SC_ROWS
    cp = n_cp // SC_WORKERS
    n_cs = t_s // SC_ROWS
    assert t_p == cp * SC_ROWS * SC_WORKERS and t_s == n_cs * SC_ROWS and n_cs <= SC_WORKERS
    idx0 = _plane_rows(dest0, planes, n_rows)
    idx1 = _plane_rows(dest1, planes, n_rows)

    def body(hnp_hbm, hns_hbm, d0_hbm, d1_hbm, xs_hbm, rows_v, i0_v, i1_v, is0_v, is1_v, sem_in, sem_out):
        wid = _sc_worker()
        pltpu.sync_copy(d0_hbm.at[pl.ds(wid * cp, cp)], i0_v)
        pltpu.sync_copy(d1_hbm.at[pl.ds(wid * cp, cp)], i1_v)

        def move(src_hbm, src_rows, row0, i0, i1, c):
            loads = [pltpu.async_copy(src_hbm.at[pl.ds(p * src_rows + row0, SC_ROWS)], rows_v.at[p], sem_in)
                     for p in range(planes)]
            for cpy in loads:
                cpy.wait()
            stores = []
            for p in range(planes):
                stores.append(pltpu.async_copy(rows_v.at[p], xs_hbm.at[i0.at[c, p]], sem_out))
                stores.append(pltpu.async_copy(rows_v.at[p], xs_hbm.at[i1.at[c, p]], sem_out))
            for cpy in stores:
                cpy.wait()

        @pl.loop(0, cp)
        def _(c):
            move(hnp_hbm, t_p, (wid * cp + c) * SC_ROWS, i0_v, i1_v, c)

        @pl.when(wid < n_cs)
        def _():
            pltpu.sync_copy(d0_hbm.at[pl.ds(n_cp + wid, 1)], is0_v)
            pltpu.sync_copy(d1_hbm.at[pl.ds(n_cp + wid, 1)], is1_v)
            move(hns_hbm, t_s, wid * SC_ROWS, is0_v, is1_v, 0)

    xs = pl.kernel(
        body,
        out_type=jax.ShapeDtypeStruct((planes * n_rows, LANES), I32),
        mesh=_sc_mesh(),
        scratch_types=[
            pltpu.VMEM((planes, SC_ROWS, LANES), I32),
            pltpu.VMEM((cp, planes, SC_ROWS), I32),
            pltpu.VMEM((cp, planes, SC_ROWS), I32),
            pltpu.VMEM((1, planes, SC_ROWS), I32),
            pltpu.VMEM((1, planes, SC_ROWS), I32),
            pltpu.SemaphoreType.DMA,
            pltpu.SemaphoreType.DMA,
        ],
        name="moe_dispatch_sc",
    )(hn_p.reshape(planes * t_p, LANES), hn_s.reshape(planes * t_s, LANES), idx0, idx1)
    return xs.reshape(planes, n_rows, LANES)


def _gather(ys, dest0, dest1):
    planes, n_rows, _ = ys.shape
    n_chunks = dest0.shape[0]
    n_tok = n_chunks * SC_ROWS
    cpw = max(n_chunks // SC_WORKERS, 1)
    assert n_chunks <= SC_WORKERS or n_chunks == cpw * SC_WORKERS
    idx = (_plane_rows(dest0, planes, n_rows), _plane_rows(dest1, planes, n_rows))

    def body(ys_hbm, d0_hbm, d1_hbm, out_hbm, rows_v, i0_v, i1_v, sem_in, sem_out):
        wid = _sc_worker()

        def work():
            pltpu.sync_copy(d0_hbm.at[pl.ds(wid * cpw, cpw)], i0_v)
            pltpu.sync_copy(d1_hbm.at[pl.ds(wid * cpw, cpw)], i1_v)

            @pl.loop(0, cpw)
            def _(c):
                row0 = (wid * cpw + c) * SC_ROWS
                for j, i_v in enumerate((i0_v, i1_v)):
                    loads = [pltpu.async_copy(ys_hbm.at[i_v.at[c, p]], rows_v.at[p], sem_in)
                             for p in range(planes)]
                    for cpy in loads:
                        cpy.wait()
                    stores = [
                        pltpu.async_copy(
                            rows_v.at[p], out_hbm.at[pl.ds((j * planes + p) * n_tok + row0, SC_ROWS)], sem_out)
                        for p in range(planes)]
                    for cpy in stores:
                        cpy.wait()

        if n_chunks < SC_WORKERS:
            pl.when(wid < n_chunks)(work)
        else:
            work()

    out = pl.kernel(
        body,
        out_type=jax.ShapeDtypeStruct((2 * planes * n_tok, LANES), ys.dtype),
        mesh=_sc_mesh(),
        scratch_types=[
            pltpu.VMEM((planes, SC_ROWS, LANES), ys.dtype),
            pltpu.VMEM((cpw, planes, SC_ROWS), I32),
            pltpu.VMEM((cpw, planes, SC_ROWS), I32),
            pltpu.SemaphoreType.DMA,
            pltpu.SemaphoreType.DMA,
        ],
        name="moe_gather_sc",
    )(ys.reshape(planes * n_rows, LANES), *idx)
    return out.reshape(2, planes, n_tok, LANES)


def _expert_kernel(be_ref, bv_ref, nb_ref, xs_ref, wg_ref, wu_ref, wd_ref, ys_ref, wgu_s, wd_s):
    i = pl.program_id(0)
    live = i < nb_ref[0]
    prev = be_ref[jnp.maximum(i - 1, 0)]
    fresh = (i == 0) | (be_ref[i] != prev)

    @pl.when(live & fresh)
    def _():
        wgu_s[:, 0:D_EXPERT] = wg_ref[0].astype(BF16)
        wgu_s[:, D_EXPERT:2 * D_EXPERT] = wu_ref[0].astype(BF16)
        wd_s[...] = wd_ref[0].astype(BF16)

    @pl.when(live)
    def _():
        valid = lax.broadcasted_iota(I32, (MOE_BLK, LANES), 0) < bv_ref[i]
        xb = _unpack_planes([jnp.where(valid, xs_ref[p], 0) for p in range(HN_PLANES)]).astype(BF16)
        gu = _dot(xb, wgu_s[...])
        gate = gu[:, 0:D_EXPERT]
        hmid = gate * jax.nn.sigmoid(gate) * gu[:, D_EXPERT:]
        _pack_planes(_dot(hmid.astype(BF16), wd_s[...]), ys_ref)

    @pl.when(jnp.logical_not(live))
    def _():
        ys_ref[...] = jnp.zeros_like(ys_ref)


def _experts(block_e, block_valid, nb, xs, w_eg, w_eu, w_ed):
    n_rows = xs.shape[1]
    n_blocks = n_rows // MOE_BLK
    row_idx = lambda i, be, bv, nb: (0, jnp.minimum(i, nb[0] - 1), 0)
    w_idx = lambda i, be, bv, nb: (be[i], 0, 0)
    grid_spec = pltpu.PrefetchScalarGridSpec(
        num_scalar_prefetch=3,
        grid=(n_blocks,),
        in_specs=[
            pl.BlockSpec((HN_PLANES, MOE_BLK, LANES), row_idx),
            pl.BlockSpec((1, D_MODEL, D_EXPERT), w_idx),
            pl.BlockSpec((1, D_MODEL, D_EXPERT), w_idx),
            pl.BlockSpec((1, D_EXPERT, D_MODEL), w_idx),
        ],
        out_specs=pl.BlockSpec((HN_PLANES, MOE_BLK, LANES), lambda i, be, bv, nb: (0, i, 0)),
        scratch_shapes=[
            pltpu.VMEM((D_MODEL, 2 * D_EXPERT), BF16),
            pltpu.VMEM((D_EXPERT, D_MODEL), BF16),
        ],
    )
    return pl.pallas_call(
        _expert_kernel,
        grid_spec=grid_spec,
        out_shape=jax.ShapeDtypeStruct((HN_PLANES, n_rows, LANES), I32),
        compiler_params=pltpu.CompilerParams(
            dimension_semantics=("arbitrary",), vmem_limit_bytes=VMEM_LIMIT),
        name="moe_experts",
    )(block_e, block_valid, nb, xs, w_eg, w_eu, w_ed)


def _combine_kernel(h_ref, route_ref, nfin_ref, rows_ref, *rest):
    y_ref = rest[-1]
    route = route_ref[...]
    ys1 = _unpack_planes([rows_ref[0, p] for p in range(HN_PLANES)])
    ys2 = _unpack_planes([rows_ref[1, p] for p in range(HN_PLANES)])
    out = h_ref[...] + (ys1 * route[:, 2:3] + ys2 * route[:, 3:4])
    y_ref[...] = _rms(out, nfin_ref[...])


def _combine(h, route, norm_final, rows, *, row0=0, y_prev=None):
    total = h.shape[0]
    tb = COMBINE_TB
    blk0 = row0 // tb
    args = [h, route, norm_final, rows]
    in_specs = [
        pl.BlockSpec((tb, D_MODEL), lambda i: (blk0 + i, 0)),
        pl.BlockSpec((tb, LANES), lambda i: (blk0 + i, 0)),
        pl.BlockSpec((1, D_MODEL), lambda i: (0, 0)),
        pl.BlockSpec((2, HN_PLANES, tb, LANES), lambda i: (0, 0, i, 0)),
    ]
    aliases = {}
    if y_prev is not None:
        aliases[len(args)] = 0
        args.append(y_prev)
        in_specs.append(pl.BlockSpec(memory_space=pl.ANY))
    return pl.pallas_call(
        _combine_kernel,
        grid=(rows.shape[2] // tb,),
        in_specs=in_specs,
        out_specs=pl.BlockSpec((tb, D_MODEL), lambda i: (blk0 + i, 0)),
        out_shape=jax.ShapeDtypeStruct((total, D_MODEL), F32),
        input_output_aliases=aliases,
        compiler_params=pltpu.CompilerParams(
            dimension_semantics=("arbitrary",), vmem_limit_bytes=VMEM_LIMIT),
        name="moe_combine",
    )(*args)


def kernel(x_prompt, x_sample, state_pool, state_gla, meta_tokens, norm_mix, w_in, w_gate_up, b_gate, w_pool, pool_scale, gla_norm, w_out, norm_ffn, w_router_group, b_router_group, w_router_expert, b_router_expert, w_expert_gate, w_expert_up, w_expert_down, norm_final):
    assert w_in.shape[0] == 1, "one encoder layer"
    batch, seq, _ = x_prompt.shape
    dec_batch, dec_seq, _ = x_sample.shape
    assert seq % MIX_TB == 0 and dec_seq == CHUNK and N_META <= CHUNK
    t_prompt = batch * seq
    t_sample = dec_batch * dec_seq
    t_all = t_prompt + t_sample
    assert t_prompt % COMBINE_TB == 0 and t_sample % COMBINE_TB == 0

    w_in0 = w_in[0]
    gpad = EXPERTS_PER_GROUP - N_GROUPS
    rpad = ROUTER_ROWS - EXPERTS_PER_GROUP - N_EXPERTS
    w_router = jnp.concatenate([
        w_router_group[0].T, jnp.zeros((gpad, D_MODEL), F32),
        w_router_expert[0].T, jnp.zeros((rpad, D_MODEL), F32)], axis=0)
    b_router = jnp.concatenate([
        b_router_group[0], jnp.zeros((gpad,), F32), b_router_expert[0], jnp.zeros((rpad,), F32)])
    zg = jnp.zeros((POOL_GROUP_DIM, POOL_GROUP_DIM), F32)
    w_pool_pairs = jnp.stack([
        jnp.block([[w_pool[0, 2 * i], zg], [zg, w_pool[0, 2 * i + 1]]]) for i in range(len(POOL_WINDOWS) // 2)])
    weights = (
        norm_mix[0][None, :],
        w_in0[:, :D_MAIN].astype(BF16),
        w_in0[:, D_MAIN:].T.astype(BF16),
        w_gate_up[0].astype(BF16),
        b_gate[0][None, :],
        w_pool_pairs.astype(BF16),
        pool_scale[0][None, :],
        gla_norm[0][None, :],
        w_out[0].astype(BF16),
        norm_ffn[0][None, :],
        w_router.astype(BF16),
        b_router[:, None],
    )

    zero_cnt = jnp.zeros((N_EXPERTS, 1), F32)
    x_meta = jnp.pad(meta_tokens.astype(F32), ((CHUNK - N_META, 0), (0, 0)))
    meta = _mixer(x_meta, jnp.zeros((1, POOL_ROWS, D_POOL), F32),
                  jnp.zeros((1, GLA_HEADS, GLA_DK, GLA_DV), F32), zero_cnt,
                  weights, batch=1, seq=CHUNK, tb=CHUNK, lead_pad=CHUNK - N_META)
    h_m, hn_m, route_m, rt_m, pool_m, st_m, cnt_m = meta
    del h_m, hn_m, route_m, rt_m, cnt_m
    h_p, hn_p, route_p, rt_p, pool_p, st_p, cnt_p = _mixer(
        x_prompt.reshape(t_prompt, D_MODEL), pool_m, st_m, zero_cnt, weights,
        batch=batch, seq=seq, tb=MIX_TB, lead_pad=0)
    pool_s0 = jnp.pad(state_pool[0], ((0, 0), (POOL_ROWS - POOL_PAD, 0), (0, 0)))
    h_s, hn_s, route_s, rt_s, pool_s, st_s, cnt_s = _mixer(
        x_sample.reshape(t_sample, D_MODEL), pool_s0, state_gla[0].astype(F32),
        cnt_p, weights, batch=dec_batch, seq=dec_seq, tb=SAMPLE_TB, lead_pad=0)

    counts = cnt_s[:, 0].astype(I32)
    padded = (counts + MOE_BLK - 1) // MOE_BLK * MOE_BLK
    ends = jnp.cumsum(padded)
    pstart = ends - padded
    n_blocks = (2 * t_all + N_EXPERTS * (MOE_BLK - 1) + MOE_BLK - 1) // MOE_BLK
    nb = (ends[-1] // MOE_BLK).astype(I32)
    blk_ids = jnp.minimum(jnp.arange(n_blocks, dtype=I32), nb - 1)
    block_e = jnp.sum((ends[None, :] <= (blk_ids * MOE_BLK)[:, None]).astype(I32), axis=1)
    block_e = jnp.minimum(block_e, N_EXPERTS - 1)
    owner = block_e[:, None] == jnp.arange(N_EXPERTS, dtype=I32)
    row_end = jnp.sum(jnp.where(owner, pstart + counts, 0), axis=1)
    block_valid = jnp.clip(row_end - blk_ids * MOE_BLK, 0, MOE_BLK)

    def dest_rows(rt):
        rt = rt.transpose(1, 0, 2).reshape(ROUTE_ROWS, -1)
        onehot = rt[0:2].astype(I32)[..., None] == jnp.arange(N_EXPERTS, dtype=I32)
        return jnp.sum(jnp.where(onehot, pstart, 0), axis=-1) + rt[4:6].astype(I32)

    dest = jnp.concatenate([dest_rows(rt_p), dest_rows(rt_s)], axis=1)
    dest0 = dest[0].reshape(t_all // SC_ROWS, SC_ROWS)
    dest1 = dest[1].reshape(t_all // SC_ROWS, SC_ROWS)
    xs = _dispatch(hn_p, hn_s, dest0, dest1, n_blocks * MOE_BLK)
    ys = _experts(block_e, block_valid.astype(I32), nb[None], xs,
                  w_expert_gate[0], w_expert_up[0], w_expert_down[0])
    nfin = norm_final[None, :]
    cp_chunks = t_prompt // SC_ROWS
    part = cp_chunks // COMBINE_PARTS
    y_prompt = None
    for i in range(COMBINE_PARTS):
        ch = slice(i * part, (i + 1) * part)
        rows_i = _gather(ys, dest0[ch], dest1[ch])
        y_prompt = _combine(h_p, route_p, nfin, rows_i, row0=i * part * SC_ROWS, y_prev=y_prompt)
    y_prompt = y_prompt.reshape(batch, seq, D_MODEL)
    rows_s = _gather(ys, dest0[cp_chunks:], dest1[cp_chunks:])
    y_sample = _combine(h_s, route_s, nfin, rows_s).reshape(dec_batch, dec_seq, D_MODEL)
    new_pool_prompt = pool_p[:, POOL_ROWS - POOL_PAD:][None]
    new_gla_prompt = st_p[None]
    new_pool_sample = pool_s[:, POOL_ROWS - POOL_PAD:][None]
    new_gla_sample = st_s[None]
    return (y_prompt, y_sample, new_pool_prompt, new_gla_prompt, new_pool_sample, new_gla_sample)
```

```python
import functools

import jax
import jax.numpy as jnp
from jax import lax
from jax.experimental import pallas as pl
from jax.experimental.pallas import tpu as pltpu
from jax.experimental.pallas import tpu_sc as plsc

F32 = jnp.float32
BF16 = jnp.bfloat16
U32 = jnp.uint32
I32 = jnp.int32

D_MODEL = 1024
N_META = 16
CHUNK = 64
EPS = 1e-6
D_POOL = 512
POOL_WINDOWS = (2, 4, 8, 16)
POOL_GROUP_DIM = 128
POOL_PAD = 15
POOL_ROWS = 16
GLA_HEADS = 4
GLA_DK = 64
GLA_DV = 128
D_QK = 256
D_V = 512
HEAD_GROUP = 2
GATE_RANK = 16
GATE_TAU = 16.0
D_MAIN = D_POOL + 2 * D_QK + 2 * D_V
N_GROUPS = 4
EXPERTS_PER_GROUP = 8
N_EXPERTS = 32
D_EXPERT = 512

LANES = 128
MXU_DIM = 256
HALF = D_MODEL // 2
HN_PLANES = HALF // LANES
ROUTE_ROWS = 8
ROUTER_ROWS = 64
MIX_TB = 512
SAMPLE_TB = 256
FRONT_TILE = 256
FRONT_PLAN = (3, 0, 1, 2, 2, 0)
MOE_BLK = 1024
COMBINE_TB = 512
COMBINE_PARTS = 4
VMEM_LIMIT = 48 * 1024 * 1024
SC_CORES = 2
SC_SUBCORES = 16
SC_WORKERS = SC_CORES * SC_SUBCORES
SC_ROWS = 128


def _rms(x, g):
    return x * lax.rsqrt(jnp.mean(x * x, axis=-1, keepdims=True) + EPS) * g


def _dot(a, b):
    return jnp.dot(a, b, preferred_element_type=F32)


def _dot_nt(a, b):
    return lax.dot_general(a, b, (((1,), (1,)), ((), ())), preferred_element_type=F32)


def _pack_planes(x, ref):
    xb = x.astype(BF16)
    lo = lax.bitcast_convert_type(xb[:, :HALF].astype(F32), U32) >> 16
    hi = lax.bitcast_convert_type(xb[:, HALF:].astype(F32), U32) & jnp.uint32(0xFFFF0000)
    packed = lax.bitcast_convert_type(lo | hi, I32)
    for p in range(HN_PLANES):
        ref[p] = packed[:, p * LANES:(p + 1) * LANES]


def _unpack_planes(planes):
    words = [lax.bitcast_convert_type(p, U32) for p in planes]
    los = [lax.bitcast_convert_type(w << 16, F32) for w in words]
    his = [lax.bitcast_convert_type(w & jnp.uint32(0xFFFF0000), F32) for w in words]
    return jnp.concatenate(los + his, axis=-1)


def _dot_tn(a, b):
    return lax.dot_general(a, b, (((0,), (0,)), ((), ())), preferred_element_type=F32)


def _mixer_kernel(x_ref, pool0_ref, st0_ref, cnt0_ref, tril_ref, sup_ref,
                  nmix_ref, wmain_ref, wz_ref, wgu_ref, bgate_ref, wpool_ref, pscale_ref,
                  gnorm_ref, wout_ref, nffn_ref, wr_ref, br_ref,
                  h_ref, hn_ref, route_ref, route_t_ref, pool_out_ref, st_out_ref, cnt_out_ref,
                  ext_ref, st_ref, kbd_ref, vbd_ref, sbd_ref, o_ref, cnt_ref,
                  xs_ref, proj_ref, z_ref, *, tb, nj, lead_pad, chained):
    s = pl.program_id(0)
    back = jnp.maximum(s - 1, 0)
    j = lax.rem(back, nj)
    n_chunks = tb // CHUNK
    wr_slot = lax.rem(s, 2)
    rd_slot = 1 - wr_slot

    @pl.when(s == 0)
    def _():
        kbd_ref[...] = jnp.zeros_like(kbd_ref)
        vbd_ref[...] = jnp.zeros_like(vbd_ref)
        sbd_ref[...] = jnp.zeros_like(sbd_ref)
        xs_ref[1] = jnp.zeros((tb, D_MODEL), F32)
        proj_ref[1] = jnp.zeros((tb, D_MAIN), F32)
        z_ref[1] = jnp.zeros((GATE_RANK, tb), F32)

    @pl.when(s <= 1)
    def _():
        cnt_ref[...] = cnt0_ref[...]

    def put_state(c, hh, st):
        gg, hp = divmod(hh, HEAD_GROUP)
        sbd_ref[c, gg, hp * GLA_DK:(hp + 1) * GLA_DK, hp * GLA_DV:(hp + 1) * GLA_DV] = st.astype(BF16)

    if chained:
        @pl.when(j == 0)
        def _():
            ext_ref[0:POOL_ROWS, :] = pool0_ref[0]
            st_ref[...] = st0_ref[0]

        for hh in range(GLA_HEADS):
            put_state(0, hh, st_ref[hh])
    else:
        for c in range(n_chunks):
            for hh in range(GLA_HEADS):
                put_state(c, hh, st0_ref[c, hh])

    x_new = x_ref[...]
    xn = _rms(x_new, nmix_ref[...]).astype(BF16)
    xs_ref[wr_slot] = x_new

    tiles_done = [0]

    def front_tiles(stage):
        for t in range(tiles_done[0], tiles_done[0] + FRONT_PLAN[stage]):
            cols = slice(t * FRONT_TILE, (t + 1) * FRONT_TILE)
            proj_ref[wr_slot, :, cols] = _dot(xn, wmain_ref[:, cols])
        tiles_done[0] += FRONT_PLAN[stage]

    front_tiles(0)
    z_ref[wr_slot] = _dot_nt(wz_ref[...], xn)

    x = xs_ref[rd_slot]
    z = z_ref[rd_slot]
    u = proj_ref[rd_slot, :, 0:D_POOL]
    q = proj_ref[rd_slot, :, D_POOL:D_POOL + D_QK]
    k = proj_ref[rd_slot, :, D_POOL + D_QK:D_POOL + 2 * D_QK]
    v = proj_ref[rd_slot, :, D_POOL + 2 * D_QK:D_POOL + 2 * D_QK + D_V]
    r = proj_ref[rd_slot, :, D_POOL + 2 * D_QK + D_V:D_MAIN]

    row = lax.broadcasted_iota(I32, (tb, 1), 0)

    pseg = POOL_ROWS + CHUNK
    if chained:
        ext_ref[POOL_ROWS:POOL_ROWS + tb, :] = u
        ext = ext_ref[...]
    else:
        ext = jnp.concatenate(
            [blk for c in range(n_chunks) for blk in (pool0_ref[c], u[c * CHUNK:(c + 1) * CHUNK])], axis=0)
    pooled = []
    for g, w in enumerate(POOL_WINDOWS):
        sl = slice(g * POOL_GROUP_DIM, (g + 1) * POOL_GROUP_DIM)
        acc = ext[:, sl]
        for d in range(g + 1):
            acc = acc + pltpu.roll(acc, 1 << d, axis=0)
        if chained:
            win = acc[POOL_ROWS:, :]
        else:
            win = jnp.concatenate([acc[c * pseg + POOL_ROWS:(c + 1) * pseg] for c in range(n_chunks)], axis=0)
        if lead_pad:
            cnt = jnp.clip(row - lead_pad + 1, 1, w).astype(F32)
            pooled.append(win / cnt - u[:, sl])
        else:
            pooled.append(win * (1.0 / w) - u[:, sl])
    pys = [_dot(jnp.concatenate(pooled[2 * i:2 * i + 2], axis=-1).astype(BF16), wpool_ref[i])
           for i in range(len(POOL_WINDOWS) // 2)]
    pool_y = jnp.concatenate(pys, axis=-1) * pscale_ref[...]
    if chained:
        ext_ref[0:POOL_ROWS, :] = ext_ref[tb:tb + POOL_ROWS, :]
    else:
        for c in range(n_chunks):
            pool_out_ref[c] = u[(c + 1) * CHUNK - POOL_ROWS:(c + 1) * CHUNK]

    gpre = _dot_tn(z.astype(BF16), wgu_ref[...]) + bgate_ref[...]
    log_a = (jnp.minimum(gpre, 0.0) - jnp.log(1.0 + jnp.exp(-jnp.abs(gpre)))) * (1.0 / GATE_TAU)
    if lead_pad:
        log_a = jnp.where(row >= lead_pad, log_a, 0.0)
    a_hi = log_a.astype(BF16)
    a_lo = (log_a - a_hi.astype(F32)).astype(BF16)
    tril = tril_ref[...]
    seg = tril.shape[0]
    bcum = jnp.concatenate(
        [_dot(tril, a_hi[r0:r0 + seg]) + _dot(tril, a_lo[r0:r0 + seg]) for r0 in range(0, tb, seg)], axis=0)
    front_tiles(1)
    eb = jnp.exp(bcum)
    qi = q * (GLA_DK ** -0.5) * eb
    ki = k * jnp.exp(-bcum)

    rr = lax.broadcasted_iota(I32, (CHUNK, HEAD_GROUP * CHUNK), 0)
    cc = lax.broadcasted_iota(I32, (CHUNK, HEAD_GROUP * CHUNK), 1)
    causal = (cc % CHUNK) <= rr

    lasts = [eb[(c + 1) * CHUNK - 1:(c + 1) * CHUNK, :] for c in range(n_chunks)]
    dcol = jnp.concatenate(lasts + [jnp.zeros((LANES - n_chunks, D_QK), F32)], axis=0).T

    n_grp = GLA_HEADS // HEAD_GROUP
    chunk_rows = [slice(c * CHUNK, (c + 1) * CHUNK) for c in range(n_chunks)]
    grp_k = [slice(g * HEAD_GROUP * GLA_DK, (g + 1) * HEAD_GROUP * GLA_DK) for g in range(n_grp)]
    grp_v = [slice(g * HEAD_GROUP * GLA_DV, (g + 1) * HEAD_GROUP * GLA_DV) for g in range(n_grp)]
    qi_b = qi.astype(BF16)
    ki_b = ki.astype(BF16)
    v_b = v.astype(BF16)

    scores = {}
    for c in range(n_chunks):
        for hh in range(GLA_HEADS):
            gg, hp = divmod(hh, HEAD_GROUP)
            kbd_ref[c, gg, hp * CHUNK:(hp + 1) * CHUNK, hp * GLA_DK:(hp + 1) * GLA_DK] = (
                ki_b[chunk_rows[c], hh * GLA_DK:(hh + 1) * GLA_DK])
            vbd_ref[c, gg, hp * CHUNK:(hp + 1) * CHUNK, hp * GLA_DV:(hp + 1) * GLA_DV] = (
                v_b[chunk_rows[c], hh * GLA_DV:(hh + 1) * GLA_DV])
        for gg in range(n_grp):
            scores[c, gg] = _dot_nt(qi_b[chunk_rows[c], grp_k[gg]], kbd_ref[c, gg])

    kvs = {}
    for c in range(n_chunks):
        for hh in range(GLA_HEADS):
            kvs[c, hh] = _dot_tn(ki_b[chunk_rows[c], hh * GLA_DK:(hh + 1) * GLA_DK],
                                 v_b[chunk_rows[c], hh * GLA_DV:(hh + 1) * GLA_DV])
    front_tiles(2)

    for hh in range(GLA_HEADS):
        st = st_ref[hh] if chained else None
        for c in range(n_chunks):
            s_old = st if chained else st0_ref[c, hh]
            s_new = (s_old + kvs[c, hh]) * dcol[hh * GLA_DK:(hh + 1) * GLA_DK, c:c + 1]
            if not chained:
                st_out_ref[c, hh] = s_new
            else:
                st = s_new
                if c + 1 < n_chunks:
                    put_state(c + 1, hh, s_new)
        if chained:
            st_ref[hh] = st

    for c in range(n_chunks):
        for gg in range(n_grp):
            p = jnp.where(causal, scores[c, gg], 0.0).astype(BF16)
            o_ref[chunk_rows[c], grp_v[gg]] = (
                _dot(p, vbd_ref[c, gg]) + _dot(qi_b[chunk_rows[c], grp_k[gg]], sbd_ref[c, gg]))

    o = o_ref[...]
    ons = []
    for hh in range(GLA_HEADS):
        oh = o[:, hh * GLA_DV:(hh + 1) * GLA_DV]
        ons.append(oh * lax.rsqrt(jnp.mean(oh * oh, axis=-1, keepdims=True) + EPS))
    og = jnp.concatenate(ons, axis=-1) * gnorm_ref[...] * (r * jax.nn.sigmoid(r))
    mix = _dot(jnp.concatenate([pool_y, og], axis=-1).astype(BF16), wout_ref[...])
    front_tiles(3)

    h = x + mix
    h_ref[...] = h
    hn = _rms(h, nffn_ref[...]).astype(BF16)
    _pack_planes(hn, hn_ref)

    logits = _dot_nt(wr_ref[...], hn) + br_ref[...]
    front_tiles(4)
    sub = lax.broadcasted_iota(I32, (EXPERTS_PER_GROUP, tb), 0).astype(F32)
    neg = jnp.float32(-jnp.inf)
    big = jnp.float32(EXPERTS_PER_GROUP)
    tile0 = logits[0:EXPERTS_PER_GROUP]
    is_g = sub < N_GROUPS
    gmax = jnp.max(jnp.where(is_g, tile0, neg), axis=0, keepdims=True)
    gsum = jnp.sum(jnp.where(is_g, jnp.exp(tile0 - gmax), 0.0), axis=0, keepdims=True)
    p_g = 1.0 / gsum
    gidx = jnp.min(jnp.where(is_g & (tile0 == gmax), sub, big), axis=0, keepdims=True)
    el = logits[N_GROUPS * EXPERTS_PER_GROUP:(N_GROUPS + 1) * EXPERTS_PER_GROUP]
    for g in range(N_GROUPS - 2, -1, -1):
        el = jnp.where(gidx == g, logits[(g + 1) * EXPERTS_PER_GROUP:(g + 2) * EXPERTS_PER_GROUP], el)
    m1 = jnp.max(el, axis=0, keepdims=True)
    i1 = jnp.min(jnp.where(el == m1, sub, big), axis=0, keepdims=True)
    rest = sub != i1
    m2 = jnp.max(jnp.where(rest, el, neg), axis=0, keepdims=True)
    i2 = jnp.min(jnp.where(rest & (el == m2), sub, big), axis=0, keepdims=True)
    t2 = jnp.exp(m2 - m1)
    den = 1.0 + t2
    g1 = p_g / den
    g2 = p_g * t2 / den
    e1 = gidx * EXPERTS_PER_GROUP + i1
    e2 = gidx * EXPERTS_PER_GROUP + i2

    eid = lax.broadcasted_iota(I32, (N_EXPERTS, tb), 0).astype(F32)
    oh1 = eid == e1
    oh2 = eid == e2
    both = jnp.where(oh1 | oh2, 1.0, 0.0)
    cnt = cnt_ref[...]
    before = _dot(both.astype(BF16), sup_ref[...]) + cnt
    front_tiles(5)
    assert tiles_done[0] * FRONT_TILE == D_MAIN
    pos1 = jnp.sum(jnp.where(oh1, before, 0.0), axis=0, keepdims=True)
    pos2 = jnp.sum(jnp.where(oh2, before, 0.0), axis=0, keepdims=True)
    cnt_new = cnt + jnp.sum(both, axis=1, keepdims=True)
    cnt_ref[...] = cnt_new
    cnt_out_ref[...] = cnt_new

    zero = jnp.zeros_like(e1)
    route_t = jnp.concatenate([e1, e2, g1, g2, pos1, pos2, zero, zero], axis=0)
    route_t_ref[0] = route_t
    route_ref[...] = jnp.concatenate([route_t, jnp.zeros((LANES - ROUTE_ROWS, tb), F32)], axis=0).T

    if chained:
        @pl.when(j == nj - 1)
        def _():
            pool_out_ref[0] = ext_ref[0:POOL_ROWS, :]
            st_out_ref[0] = st_ref[...]


def _mixer(x2d, pool0, st0, cnt0, weights, *, batch, seq, tb, lead_pad):
    chained = tb <= seq
    total_rows = batch * seq
    n_blk = total_rows // tb
    nj = seq // tb if chained else 1
    per_blk = 1 if chained else tb // seq
    assert chained or (seq == CHUNK and batch % per_blk == 0)
    shared = pool0.shape[0] == 1
    front = lambda s: jnp.minimum(s, n_blk - 1)
    back = lambda s: jnp.maximum(s - 1, 0)
    stream = lambda s: back(s) // nj
    st_idx = (lambda s: (0, 0, 0)) if shared else (lambda s: (stream(s), 0, 0))
    gla_idx = (lambda s: (0, 0, 0, 0)) if shared else (lambda s: (stream(s), 0, 0, 0))
    const2 = lambda s: (0, 0)
    tok_out = lambda s: (back(s), 0)

    seg = min(tb, MXU_DIM)
    ii = jnp.arange(seg)
    tril = ((ii[:, None] >= ii[None, :]) & (ii[:, None] // CHUNK == ii[None, :] // CHUNK)).astype(BF16)
    ii = jnp.arange(tb)
    sup = (ii[:, None] < ii[None, :]).astype(BF16)

    in_specs = [
        pl.BlockSpec((tb, D_MODEL), lambda s: (front(s), 0)),
        pl.BlockSpec((per_blk, POOL_ROWS, D_POOL), st_idx),
        pl.BlockSpec((per_blk, GLA_HEADS, GLA_DK, GLA_DV), gla_idx),
        pl.BlockSpec((N_EXPERTS, 1), const2),
        pl.BlockSpec((seg, seg), const2),
        pl.BlockSpec((tb, tb), const2),
    ]
    for wgt in weights:
        in_specs.append(pl.BlockSpec(wgt.shape, (lambda s, n=wgt.ndim: (0,) * n)))
    args = [x2d, pool0, st0, cnt0, tril, sup, *weights]

    out_shape = [
        jax.ShapeDtypeStruct((total_rows, D_MODEL), F32),
        jax.ShapeDtypeStruct((HN_PLANES, total_rows, LANES), I32),
        jax.ShapeDtypeStruct((total_rows, LANES), F32),
        jax.ShapeDtypeStruct((n_blk, ROUTE_ROWS, tb), F32),
        jax.ShapeDtypeStruct((batch, POOL_ROWS, D_POOL), F32),
        jax.ShapeDtypeStruct((batch, GLA_HEADS, GLA_DK, GLA_DV), F32),
        jax.ShapeDtypeStruct((N_EXPERTS, 1), F32),
    ]
    out_specs = [
        pl.BlockSpec((tb, D_MODEL), tok_out),
        pl.BlockSpec((HN_PLANES, tb, LANES), lambda s: (0, back(s), 0)),
        pl.BlockSpec((tb, LANES), tok_out),
        pl.BlockSpec((1, ROUTE_ROWS, tb), lambda s: (back(s), 0, 0)),
        pl.BlockSpec((per_blk, POOL_ROWS, D_POOL), lambda s: (stream(s), 0, 0)),
        pl.BlockSpec((per_blk, GLA_HEADS, GLA_DK, GLA_DV), lambda s: (stream(s), 0, 0, 0)),
        pl.BlockSpec((N_EXPERTS, 1), const2),
    ]
    n_grp = GLA_HEADS // HEAD_GROUP
    scratch = [
        pltpu.VMEM((POOL_ROWS + tb, D_POOL), F32),
        pltpu.VMEM((GLA_HEADS, GLA_DK, GLA_DV), F32),
        pltpu.VMEM((tb // CHUNK, n_grp, HEAD_GROUP * CHUNK, HEAD_GROUP * GLA_DK), BF16),
        pltpu.VMEM((tb // CHUNK, n_grp, HEAD_GROUP * CHUNK, HEAD_GROUP * GLA_DV), BF16),
        pltpu.VMEM((tb // CHUNK, n_grp, HEAD_GROUP * GLA_DK, HEAD_GROUP * GLA_DV), BF16),
        pltpu.VMEM((tb, D_V), F32),
        pltpu.VMEM((N_EXPERTS, 1), F32),
        pltpu.VMEM((2, tb, D_MODEL), F32),
        pltpu.VMEM((2, tb, D_MAIN), F32),
        pltpu.VMEM((2, GATE_RANK, tb), F32),
    ]
    return pl.pallas_call(
        functools.partial(_mixer_kernel, tb=tb, nj=nj, lead_pad=lead_pad, chained=chained),
        grid=(n_blk + 1,),
        in_specs=in_specs,
        out_specs=out_specs,
        out_shape=out_shape,
        scratch_shapes=scratch,
        compiler_params=pltpu.CompilerParams(
            dimension_semantics=("arbitrary",), vmem_limit_bytes=VMEM_LIMIT),
        name=f"mixer_tb{tb}_pad{lead_pad}",
    )(*args)


def _sc_mesh():
    return plsc.VectorSubcoreMesh(core_axis_name="c", subcore_axis_name="s",
                                  num_cores=SC_CORES, num_subcores=SC_SUBCORES)


def _sc_worker():
    return lax.axis_index("s") * SC_CORES + lax.axis_index("c")


def _plane_rows(dest, planes, rows_per_plane):
    offs = (jnp.arange(planes, dtype=I32) * rows_per_plane)[None, :, None]
    return dest[:, None, :] + offs


def _dispatch(hn_p, hn_s, dest0, dest1, n_rows):
    planes, t_p, _ = hn_p.shape
    t_s = hn_s.shape[1]
    n_cp = t_p // SC_ROWS
    cp = n_cp // SC_WORKERS
    n_cs = t_s // SC_ROWS
    assert t_p == cp * SC_ROWS * SC_WORKERS and t_s == n_cs * SC_ROWS and n_cs <= SC_WORKERS
    idx0 = _plane_rows(dest0, planes, n_rows)
    idx1 = _plane_rows(dest1, planes, n_rows)

    def body(hnp_hbm, hns_hbm, d0_hbm, d1_hbm, xs_hbm, rows_v, i0_v, i1_v, is0_v, is1_v, sem_in, sem_out):
        wid = _sc_worker()
        pltpu.sync_copy(d0_hbm.at[pl.ds(wid * cp, cp)], i0_v)
        pltpu.sync_copy(d1_hbm.at[pl.ds(wid * cp, cp)], i1_v)

        def move(src_hbm, src_rows, row0, i0, i1, c):
            loads = [pltpu.async_copy(src_hbm.at[pl.ds(p * src_rows + row0, SC_ROWS)], rows_v.at[p], sem_in)
                     for p in range(planes)]
            for cpy in loads:
                cpy.wait()
            stores = []
            for p in range(planes):
                stores.append(pltpu.async_copy(rows_v.at[p], xs_hbm.at[i0.at[c, p]], sem_out))
                stores.append(pltpu.async_copy(rows_v.at[p], xs_hbm.at[i1.at[c, p]], sem_out))
            for cpy in stores:
                cpy.wait()

        @pl.loop(0, cp)
        def _(c):
            move(hnp_hbm, t_p, (wid * cp + c) * SC_ROWS, i0_v, i1_v, c)

        @pl.when(wid < n_cs)
        def _():
            pltpu.sync_copy(d0_hbm.at[pl.ds(n_cp + wid, 1)], is0_v)
            pltpu.sync_copy(d1_hbm.at[pl.ds(n_cp + wid, 1)], is1_v)
            move(hns_hbm, t_s, wid * SC_ROWS, is0_v, is1_v, 0)

    xs = pl.kernel(
        body,
        out_type=jax.ShapeDtypeStruct((planes * n_rows, LANES), I32),
        mesh=_sc_mesh(),
        scratch_types=[
            pltpu.VMEM((planes, SC_ROWS, LANES), I32),
            pltpu.VMEM((cp, planes, SC_ROWS), I32),
            pltpu.VMEM((cp, planes, SC_ROWS), I32),
            pltpu.VMEM((1, planes, SC_ROWS), I32),
            pltpu.VMEM((1, planes, SC_ROWS), I32),
            pltpu.SemaphoreType.DMA,
            pltpu.SemaphoreType.DMA,
        ],
        name="moe_dispatch_sc",
    )(hn_p.reshape(planes * t_p, LANES), hn_s.reshape(planes * t_s, LANES), idx0, idx1)
    return xs.reshape(planes, n_rows, LANES)


def _gather(ys, dest0, dest1):
    planes, n_rows, _ = ys.shape
    n_chunks = dest0.shape[0]
    n_tok = n_chunks * SC_ROWS
    cpw = max(n_chunks // SC_WORKERS, 1)
    assert n_chunks <= SC_WORKERS or n_chunks == cpw * SC_WORKERS
    idx = (_plane_rows(dest0, planes, n_rows), _plane_rows(dest1, planes, n_rows))

    def body(ys_hbm, d0_hbm, d1_hbm, out_hbm, rows_v, i0_v, i1_v, sem_in, sem_out):
        wid = _sc_worker()

        def work():
            pltpu.sync_copy(d0_hbm.at[pl.ds(wid * cpw, cpw)], i0_v)
            pltpu.sync_copy(d1_hbm.at[pl.ds(wid * cpw, cpw)], i1_v)

            @pl.loop(0, cpw)
            def _(c):
                row0 = (wid * cpw + c) * SC_ROWS
                for j, i_v in enumerate((i0_v, i1_v)):
                    loads = [pltpu.async_copy(ys_hbm.at[i_v.at[c, p]], rows_v.at[p], sem_in)
                             for p in range(planes)]
                    for cpy in loads:
                        cpy.wait()
                    stores = [
                        pltpu.async_copy(
                            rows_v.at[p], out_hbm.at[pl.ds((j * planes + p) * n_tok + row0, SC_ROWS)], sem_out)
                        for p in range(planes)]
                    for cpy in stores:
                        cpy.wait()

        if n_chunks < SC_WORKERS:
            pl.when(wid < n_chunks)(work)
        else:
            work()

    out = pl.kernel(
        body,
        out_type=jax.ShapeDtypeStruct((2 * planes * n_tok, LANES), ys.dtype),
        mesh=_sc_mesh(),
        scratch_types=[
            pltpu.VMEM((planes, SC_ROWS, LANES), ys.dtype),
            pltpu.VMEM((cpw, planes, SC_ROWS), I32),
            pltpu.VMEM((cpw, planes, SC_ROWS), I32),
            pltpu.SemaphoreType.DMA,
            pltpu.SemaphoreType.DMA,
        ],
        name="moe_gather_sc",
    )(ys.reshape(planes * n_rows, LANES), *idx)
    return out.reshape(2, planes, n_tok, LANES)


def _expert_kernel(be_ref, bv_ref, nb_ref, xs_ref, wg_ref, wu_ref, wd_ref, ys_ref, wgu_s, wd_s):
    i = pl.program_id(0)
    live = i < nb_ref[0]
    prev = be_ref[jnp.maximum(i - 1, 0)]
    fresh = (i == 0) | (be_ref[i] != prev)

    @pl.when(live & fresh)
    def _():
        wgu_s[:, 0:D_EXPERT] = wg_ref[0].astype(BF16)
        wgu_s[:, D_EXPERT:2 * D_EXPERT] = wu_ref[0].astype(BF16)
        wd_s[...] = wd_ref[0].astype(BF16)

    def run(n_rows):
        valid = lax.broadcasted_iota(I32, (n_rows, LANES), 0) < bv_ref[i]
        xb = _unpack_planes([jnp.where(valid, xs_ref[p, 0:n_rows, :], 0) for p in range(HN_PLANES)])
        gu = _dot(xb.astype(BF16), wgu_s[...])
        gate = gu[:, 0:D_EXPERT]
        hmid = gate * jax.nn.sigmoid(gate) * gu[:, D_EXPERT:]
        _pack_planes(_dot(hmid.astype(BF16), wd_s[...]), ys_ref.at[:, 0:n_rows, :])
        if n_rows < MOE_BLK:
            ys_ref[:, n_rows:, :] = jnp.zeros((HN_PLANES, MOE_BLK - n_rows, LANES), I32)

    half_full = bv_ref[i] <= MOE_BLK // 2

    @pl.when(live & jnp.logical_not(half_full))
    def _():
        run(MOE_BLK)

    @pl.when(live & half_full)
    def _():
        run(MOE_BLK // 2)

    @pl.when(jnp.logical_not(live))
    def _():
        ys_ref[...] = jnp.zeros_like(ys_ref)


def _experts(block_e, block_valid, nb, xs, w_eg, w_eu, w_ed):
    n_rows = xs.shape[1]
    n_blocks = n_rows // MOE_BLK
    row_idx = lambda i, be, bv, nb: (0, jnp.minimum(i, nb[0] - 1), 0)
    w_idx = lambda i, be, bv, nb: (be[i], 0, 0)
    grid_spec = pltpu.PrefetchScalarGridSpec(
        num_scalar_prefetch=3,
        grid=(n_blocks,),
        in_specs=[
            pl.BlockSpec((HN_PLANES, MOE_BLK, LANES), row_idx),
            pl.BlockSpec((1, D_MODEL, D_EXPERT), w_idx),
            pl.BlockSpec((1, D_MODEL, D_EXPERT), w_idx),
            pl.BlockSpec((1, D_EXPERT, D_MODEL), w_idx),
        ],
        out_specs=pl.BlockSpec((HN_PLANES, MOE_BLK, LANES), lambda i, be, bv, nb: (0, i, 0)),
        scratch_shapes=[
            pltpu.VMEM((D_MODEL, 2 * D_EXPERT), BF16),
            pltpu.VMEM((D_EXPERT, D_MODEL), BF16),
        ],
    )
    return pl.pallas_call(
        _expert_kernel,
        grid_spec=grid_spec,
        out_shape=jax.ShapeDtypeStruct((HN_PLANES, n_rows, LANES), I32),
        compiler_params=pltpu.CompilerParams(
            dimension_semantics=("arbitrary",), vmem_limit_bytes=VMEM_LIMIT),
        name="moe_experts",
    )(block_e, block_valid, nb, xs, w_eg, w_eu, w_ed)


def _combine_kernel(h_ref, route_ref, nfin_ref, rows_ref, *rest):
    y_ref = rest[-1]
    route = route_ref[...]
    ys1 = _unpack_planes([rows_ref[0, p] for p in range(HN_PLANES)])
    ys2 = _unpack_planes([rows_ref[1, p] for p in range(HN_PLANES)])
    out = h_ref[...] + (ys1 * route[:, 2:3] + ys2 * route[:, 3:4])
    y_ref[...] = _rms(out, nfin_ref[...])


def _combine(h, route, norm_final, rows, *, row0=0, y_prev=None):
    total = h.shape[0]
    tb = COMBINE_TB
    blk0 = row0 // tb
    args = [h, route, norm_final, rows]
    in_specs = [
        pl.BlockSpec((tb, D_MODEL), lambda i: (blk0 + i, 0)),
        pl.BlockSpec((tb, LANES), lambda i: (blk0 + i, 0)),
        pl.BlockSpec((1, D_MODEL), lambda i: (0, 0)),
        pl.BlockSpec((2, HN_PLANES, tb, LANES), lambda i: (0, 0, i, 0)),
    ]
    aliases = {}
    if y_prev is not None:
        aliases[len(args)] = 0
        args.append(y_prev)
        in_specs.append(pl.BlockSpec(memory_space=pl.ANY))
    return pl.pallas_call(
        _combine_kernel,
        grid=(rows.shape[2] // tb,),
        in_specs=in_specs,
        out_specs=pl.BlockSpec((tb, D_MODEL), lambda i: (blk0 + i, 0)),
        out_shape=jax.ShapeDtypeStruct((total, D_MODEL), F32),
        input_output_aliases=aliases,
        compiler_params=pltpu.CompilerParams(
            dimension_semantics=("arbitrary",), vmem_limit_bytes=VMEM_LIMIT),
        name="moe_combine",
    )(*args)


def kernel(x_prompt, x_sample, state_pool, state_gla, meta_tokens, norm_mix, w_in, w_gate_up, b_gate, w_pool, pool_scale, gla_norm, w_out, norm_ffn, w_router_group, b_router_group, w_router_expert, b_router_expert, w_expert_gate, w_expert_up, w_expert_down, norm_final):
    assert w_in.shape[0] == 1, "one encoder layer"
    batch, seq, _ = x_prompt.shape
    dec_batch, dec_seq, _ = x_sample.shape
    assert seq % MIX_TB == 0 and dec_seq == CHUNK and N_META <= CHUNK
    t_prompt = batch * seq
    t_sample = dec_batch * dec_seq
    t_all = t_prompt + t_sample
    assert t_prompt % COMBINE_TB == 0 and t_sample % COMBINE_TB == 0

    w_in0 = w_in[0]
    gpad = EXPERTS_PER_GROUP - N_GROUPS
    rpad = ROUTER_ROWS - EXPERTS_PER_GROUP - N_EXPERTS
    w_router = jnp.concatenate([
        w_router_group[0].T, jnp.zeros((gpad, D_MODEL), F32),
        w_router_expert[0].T, jnp.zeros((rpad, D_MODEL), F32)], axis=0)
    b_router = jnp.concatenate([
        b_router_group[0], jnp.zeros((gpad,), F32), b_router_expert[0], jnp.zeros((rpad,), F32)])
    zg = jnp.zeros((POOL_GROUP_DIM, POOL_GROUP_DIM), F32)
    w_pool_pairs = jnp.stack([
        jnp.block([[w_pool[0, 2 * i], zg], [zg, w_pool[0, 2 * i + 1]]]) for i in range(len(POOL_WINDOWS) // 2)])
    weights = (
        norm_mix[0][None, :],
        w_in0[:, :D_MAIN].astype(BF16),
        w_in0[:, D_MAIN:].T.astype(BF16),
        w_gate_up[0].astype(BF16),
        b_gate[0][None, :],
        w_pool_pairs.astype(BF16),
        pool_scale[0][None, :],
        gla_norm[0][None, :],
        w_out[0].astype(BF16),
        norm_ffn[0][None, :],
        w_router.astype(BF16),
        b_router[:, None],
    )

    zero_cnt = jnp.zeros((N_EXPERTS, 1), F32)
    x_meta = jnp.pad(meta_tokens.astype(F32), ((CHUNK - N_META, 0), (0, 0)))
    meta = _mixer(x_meta, jnp.zeros((1, POOL_ROWS, D_POOL), F32),
                  jnp.zeros((1, GLA_HEADS, GLA_DK, GLA_DV), F32), zero_cnt,
                  weights, batch=1, seq=CHUNK, tb=CHUNK, lead_pad=CHUNK - N_META)
    h_m, hn_m, route_m, rt_m, pool_m, st_m, cnt_m = meta
    del h_m, hn_m, route_m, rt_m, cnt_m
    h_p, hn_p, route_p, rt_p, pool_p, st_p, cnt_p = _mixer(
        x_prompt.reshape(t_prompt, D_MODEL), pool_m, st_m, zero_cnt, weights,
        batch=batch, seq=seq, tb=MIX_TB, lead_pad=0)
    pool_s0 = jnp.pad(state_pool[0], ((0, 0), (POOL_ROWS - POOL_PAD, 0), (0, 0)))
    h_s, hn_s, route_s, rt_s, pool_s, st_s, cnt_s = _mixer(
        x_sample.reshape(t_sample, D_MODEL), pool_s0, state_gla[0].astype(F32),
        cnt_p, weights, batch=dec_batch, seq=dec_seq, tb=SAMPLE_TB, lead_pad=0)

    counts = cnt_s[:, 0].astype(I32)
    padded = (counts + MOE_BLK - 1) // MOE_BLK * MOE_BLK
    ends = jnp.cumsum(padded)
    pstart = ends - padded
    n_blocks = (2 * t_all + N_EXPERTS * (MOE_BLK - 1) + MOE_BLK - 1) // MOE_BLK
    nb = (ends[-1] // MOE_BLK).astype(I32)
    blk_ids = jnp.minimum(jnp.arange(n_blocks, dtype=I32), nb - 1)
    block_e = jnp.sum((ends[None, :] <= (blk_ids * MOE_BLK)[:, None]).astype(I32), axis=1)
    block_e = jnp.minimum(block_e, N_EXPERTS - 1)
    owner = block_e[:, None] == jnp.arange(N_EXPERTS, dtype=I32)
    row_end = jnp.sum(jnp.where(owner, pstart + counts, 0), axis=1)
    block_valid = jnp.clip(row_end - blk_ids * MOE_BLK, 0, MOE_BLK)

    def dest_rows(rt):
        rt = rt.transpose(1, 0, 2).reshape(ROUTE_ROWS, -1)
        onehot = rt[0:2].astype(I32)[..., None] == jnp.arange(N_EXPERTS, dtype=I32)
        return jnp.sum(jnp.where(onehot, pstart, 0), axis=-1) + rt[4:6].astype(I32)

    dest = jnp.concatenate([dest_rows(rt_p), dest_rows(rt_s)], axis=1)
    dest0 = dest[0].reshape(t_all // SC_ROWS, SC_ROWS)
    dest1 = dest[1].reshape(t_all // SC_ROWS, SC_ROWS)
    xs = _dispatch(hn_p, hn_s, dest0, dest1, n_blocks * MOE_BLK)
    ys = _experts(block_e, block_valid.astype(I32), nb[None], xs,
                  w_expert_gate[0], w_expert_up[0], w_expert_down[0])
    nfin = norm_final[None, :]
    cp_chunks = t_prompt // SC_ROWS
    part = cp_chunks // COMBINE_PARTS
    y_prompt = None
    for i in range(COMBINE_PARTS):
        ch = slice(i * part, (i + 1) * part)
        rows_i = _gather(ys, dest0[ch], dest1[ch])
        y_prompt = _combine(h_p, route_p, nfin, rows_i, row0=i * part * SC_ROWS, y_prev=y_prompt)
    y_prompt = y_prompt.reshape(batch, seq, D_MODEL)
    rows_s = _gather(ys, dest0[cp_chunks:], dest1[cp_chunks:])
    y_sample = _combine(h_s, route_s, nfin, rows_s).reshape(dec_batch, dec_seq, D_MODEL)
    new_pool_prompt = pool_p[:, POOL_ROWS - POOL_PAD:][None]
    new_gla_prompt = st_p[None]
    new_pool_sample = pool_s[:, POOL_ROWS - POOL_PAD:][None]
    new_gla_sample = st_s[None]
    return (y_prompt, y_sample, new_pool_prompt, new_gla_prompt, new_pool_sample, new_gla_sample)
```

```python
import functools

import jax
import jax.numpy as jnp
from jax import lax
from jax.experimental import pallas as pl
from jax.experimental.pallas import tpu as pltpu
from jax.experimental.pallas import tpu_sc as plsc

F32 = jnp.float32
BF16 = jnp.bfloat16
U32 = jnp.uint32
I32 = jnp.int32

D_MODEL = 1024
N_META = 16
CHUNK = 64
EPS = 1e-6
D_POOL = 512
POOL_WINDOWS = (2, 4, 8, 16)
POOL_GROUP_DIM = 128
POOL_PAD = 15
POOL_ROWS = 16
GLA_HEADS = 4
GLA_DK = 64
GLA_DV = 128
D_QK = 256
D_V = 512
HEAD_GROUP = 2
GATE_RANK = 16
GATE_TAU = 16.0
D_MAIN = D_POOL + 2 * D_QK + 2 * D_V
N_GROUPS = 4
EXPERTS_PER_GROUP = 8
N_EXPERTS = 32
D_EXPERT = 512

LANES = 128
MXU_DIM = 256
HALF = D_MODEL // 2
HN_PLANES = HALF // LANES
ROUTE_ROWS = 8
ROUTER_ROWS = 64
MIX_TB = 512
SAMPLE_TB = 256
FRONT_TILE = 256
FRONT_PLAN = (3, 0, 1, 2, 2, 0)
MOE_BLK = 768
COMBINE_TB = 512
COMBINE_PARTS = 4
VMEM_LIMIT = 48 * 1024 * 1024
SC_CORES = 2
SC_SUBCORES = 16
SC_WORKERS = SC_CORES * SC_SUBCORES
SC_ROWS = 128


def _rms(x, g):
    return x * lax.rsqrt(jnp.mean(x * x, axis=-1, keepdims=True) + EPS) * g


def _dot(a, b):
    return jnp.dot(a, b, preferred_element_type=F32)


def _dot_nt(a, b):
    return lax.dot_general(a, b, (((1,), (1,)), ((), ())), preferred_element_type=F32)


def _pack_planes(x, ref):
    xb = x.astype(BF16)
    lo = lax.bitcast_convert_type(xb[:, :HALF].astype(F32), U32) >> 16
    hi = lax.bitcast_convert_type(xb[:, HALF:].astype(F32), U32) & jnp.uint32(0xFFFF0000)
    packed = lax.bitcast_convert_type(lo | hi, I32)
    for p in range(HN_PLANES):
        ref[p] = packed[:, p * LANES:(p + 1) * LANES]


def _unpack_planes(planes):
    words = [lax.bitcast_convert_type(p, U32) for p in planes]
    los = [lax.bitcast_convert_type(w << 16, F32) for w in words]
    his = [lax.bitcast_convert_type(w & jnp.uint32(0xFFFF0000), F32) for w in words]
    return jnp.concatenate(los + his, axis=-1)


def _dot_tn(a, b):
    return lax.dot_general(a, b, (((0,), (0,)), ((), ())), preferred_element_type=F32)


def _mixer_kernel(x_ref, pool0_ref, st0_ref, cnt0_ref, tril_ref, sup_ref,
                  nmix_ref, wmain_ref, wz_ref, wgu_ref, bgate_ref, wpool_ref, pscale_ref,
                  gnorm_ref, wout_ref, nffn_ref, wr_ref, br_ref,
                  h_ref, hn_ref, route_ref, route_t_ref, pool_out_ref, st_out_ref, cnt_out_ref,
                  ext_ref, st_ref, kbd_ref, vbd_ref, sbd_ref, o_ref, cnt_ref,
                  xs_ref, proj_ref, z_ref, *, tb, nj, lead_pad, chained):
    s = pl.program_id(0)
    back = jnp.maximum(s - 1, 0)
    j = lax.rem(back, nj)
    n_chunks = tb // CHUNK
    wr_slot = lax.rem(s, 2)
    rd_slot = 1 - wr_slot

    @pl.when(s == 0)
    def _():
        kbd_ref[...] = jnp.zeros_like(kbd_ref)
        vbd_ref[...] = jnp.zeros_like(vbd_ref)
        sbd_ref[...] = jnp.zeros_like(sbd_ref)
        xs_ref[1] = jnp.zeros((tb, D_MODEL), F32)
        proj_ref[1] = jnp.zeros((tb, D_MAIN), F32)
        z_ref[1] = jnp.zeros((GATE_RANK, tb), F32)

    @pl.when(s <= 1)
    def _():
        cnt_ref[...] = cnt0_ref[...]

    def put_state(c, hh, st):
        gg, hp = divmod(hh, HEAD_GROUP)
        sbd_ref[c, gg, hp * GLA_DK:(hp + 1) * GLA_DK, hp * GLA_DV:(hp + 1) * GLA_DV] = st.astype(BF16)

    if chained:
        @pl.when(j == 0)
        def _():
            ext_ref[0:POOL_ROWS, :] = pool0_ref[0]
            st_ref[...] = st0_ref[0]

        for hh in range(GLA_HEADS):
            put_state(0, hh, st_ref[hh])
    else:
        for c in range(n_chunks):
            for hh in range(GLA_HEADS):
                put_state(c, hh, st0_ref[c, hh])

    x_new = x_ref[...]
    xn = _rms(x_new, nmix_ref[...]).astype(BF16)
    xs_ref[wr_slot] = x_new

    tiles_done = [0]

    def front_tiles(stage):
        for t in range(tiles_done[0], tiles_done[0] + FRONT_PLAN[stage]):
            cols = slice(t * FRONT_TILE, (t + 1) * FRONT_TILE)
            proj_ref[wr_slot, :, cols] = _dot(xn, wmain_ref[:, cols])
        tiles_done[0] += FRONT_PLAN[stage]

    front_tiles(0)
    z_ref[wr_slot] = _dot_nt(wz_ref[...], xn)

    x = xs_ref[rd_slot]
    z = z_ref[rd_slot]
    u = proj_ref[rd_slot, :, 0:D_POOL]
    q = proj_ref[rd_slot, :, D_POOL:D_POOL + D_QK]
    k = proj_ref[rd_slot, :, D_POOL + D_QK:D_POOL + 2 * D_QK]
    v = proj_ref[rd_slot, :, D_POOL + 2 * D_QK:D_POOL + 2 * D_QK + D_V]
    r = proj_ref[rd_slot, :, D_POOL + 2 * D_QK + D_V:D_MAIN]

    row = lax.broadcasted_iota(I32, (tb, 1), 0)

    pseg = POOL_ROWS + CHUNK
    if chained:
        ext_ref[POOL_ROWS:POOL_ROWS + tb, :] = u
        ext = ext_ref[...]
    else:
        ext = jnp.concatenate(
            [blk for c in range(n_chunks) for blk in (pool0_ref[c], u[c * CHUNK:(c + 1) * CHUNK])], axis=0)
    pooled = []
    for g, w in enumerate(POOL_WINDOWS):
        sl = slice(g * POOL_GROUP_DIM, (g + 1) * POOL_GROUP_DIM)
        acc = ext[:, sl]
        for d in range(g + 1):
            acc = acc + pltpu.roll(acc, 1 << d, axis=0)
        if chained:
            win = acc[POOL_ROWS:, :]
        else:
            win = jnp.concatenate([acc[c * pseg + POOL_ROWS:(c + 1) * pseg] for c in range(n_chunks)], axis=0)
        if lead_pad:
            cnt = jnp.clip(row - lead_pad + 1, 1, w).astype(F32)
            pooled.append(win / cnt - u[:, sl])
        else:
            pooled.append(win * (1.0 / w) - u[:, sl])
    pys = [_dot(jnp.concatenate(pooled[2 * i:2 * i + 2], axis=-1).astype(BF16), wpool_ref[i])
           for i in range(len(POOL_WINDOWS) // 2)]
    pool_y = jnp.concatenate(pys, axis=-1) * pscale_ref[...]
    if chained:
        ext_ref[0:POOL_ROWS, :] = ext_ref[tb:tb + POOL_ROWS, :]
    else:
        for c in range(n_chunks):
            pool_out_ref[c] = u[(c + 1) * CHUNK - POOL_ROWS:(c + 1) * CHUNK]

    gpre = _dot_tn(z.astype(BF16), wgu_ref[...]) + bgate_ref[...]
    log_a = (jnp.minimum(gpre, 0.0) - jnp.log(1.0 + jnp.exp(-jnp.abs(gpre)))) * (1.0 / GATE_TAU)
    if lead_pad:
        log_a = jnp.where(row >= lead_pad, log_a, 0.0)
    a_hi = log_a.astype(BF16)
    a_lo = (log_a - a_hi.astype(F32)).astype(BF16)
    tril = tril_ref[...]
    seg = tril.shape[0]
    bcum = jnp.concatenate(
        [_dot(tril, a_hi[r0:r0 + seg]) + _dot(tril, a_lo[r0:r0 + seg]) for r0 in range(0, tb, seg)], axis=0)
    front_tiles(1)
    eb = jnp.exp(bcum)
    qi = q * (GLA_DK ** -0.5) * eb
    ki = k * jnp.exp(-bcum)

    rr = lax.broadcasted_iota(I32, (CHUNK, HEAD_GROUP * CHUNK), 0)
    cc = lax.broadcasted_iota(I32, (CHUNK, HEAD_GROUP * CHUNK), 1)
    causal = (cc % CHUNK) <= rr

    lasts = [eb[(c + 1) * CHUNK - 1:(c + 1) * CHUNK, :] for c in range(n_chunks)]
    dcol = jnp.concatenate(lasts + [jnp.zeros((LANES - n_chunks, D_QK), F32)], axis=0).T

    n_grp = GLA_HEADS // HEAD_GROUP
    chunk_rows = [slice(c * CHUNK, (c + 1) * CHUNK) for c in range(n_chunks)]
    grp_k = [slice(g * HEAD_GROUP * GLA_DK, (g + 1) * HEAD_GROUP * GLA_DK) for g in range(n_grp)]
    grp_v = [slice(g * HEAD_GROUP * GLA_DV, (g + 1) * HEAD_GROUP * GLA_DV) for g in range(n_grp)]
    qi_b = qi.astype(BF16)
    ki_b = ki.astype(BF16)
    v_b = v.astype(BF16)

    scores = {}
    for c in range(n_chunks):
        for hh in range(GLA_HEADS):
            gg, hp = divmod(hh, HEAD_GROUP)
            kbd_ref[c, gg, hp * CHUNK:(hp + 1) * CHUNK, hp * GLA_DK:(hp + 1) * GLA_DK] = (
                ki_b[chunk_rows[c], hh * GLA_DK:(hh + 1) * GLA_DK])
            vbd_ref[c, gg, hp * CHUNK:(hp + 1) * CHUNK, hp * GLA_DV:(hp + 1) * GLA_DV] = (
                v_b[chunk_rows[c], hh * GLA_DV:(hh + 1) * GLA_DV])
        for gg in range(n_grp):
            scores[c, gg] = _dot_nt(qi_b[chunk_rows[c], grp_k[gg]], kbd_ref[c, gg])

    kvs = {}
    for c in range(n_chunks):
        for hh in range(GLA_HEADS):
            kvs[c, hh] = _dot_tn(ki_b[chunk_rows[c], hh * GLA_DK:(hh + 1) * GLA_DK],
                                 v_b[chunk_rows[c], hh * GLA_DV:(hh + 1) * GLA_DV])
    front_tiles(2)

    for hh in range(GLA_HEADS):
        st = st_ref[hh] if chained else None
        for c in range(n_chunks):
            s_old = st if chained else st0_ref[c, hh]
            s_new = (s_old + kvs[c, hh]) * dcol[hh * GLA_DK:(hh + 1) * GLA_DK, c:c + 1]
            if not chained:
                st_out_ref[c, hh] = s_new
            else:
                st = s_new
                if c + 1 < n_chunks:
                    put_state(c + 1, hh, s_new)
        if chained:
            st_ref[hh] = st

    for c in range(n_chunks):
        for gg in range(n_grp):
            p = jnp.where(causal, scores[c, gg], 0.0).astype(BF16)
            o_ref[chunk_rows[c], grp_v[gg]] = (
                _dot(p, vbd_ref[c, gg]) + _dot(qi_b[chunk_rows[c], grp_k[gg]], sbd_ref[c, gg]))

    o = o_ref[...]
    ons = []
    for hh in range(GLA_HEADS):
        oh = o[:, hh * GLA_DV:(hh + 1) * GLA_DV]
        ons.append(oh * lax.rsqrt(jnp.mean(oh * oh, axis=-1, keepdims=True) + EPS))
    og = jnp.concatenate(ons, axis=-1) * gnorm_ref[...] * (r * jax.nn.sigmoid(r))
    mix = _dot(jnp.concatenate([pool_y, og], axis=-1).astype(BF16), wout_ref[...])
    front_tiles(3)

    h = x + mix
    h_ref[...] = h
    hn = _rms(h, nffn_ref[...]).astype(BF16)
    _pack_planes(hn, hn_ref)

    logits = _dot_nt(wr_ref[...], hn) + br_ref[...]
    front_tiles(4)
    sub = lax.broadcasted_iota(I32, (EXPERTS_PER_GROUP, tb), 0).astype(F32)
    neg = jnp.float32(-jnp.inf)
    big = jnp.float32(EXPERTS_PER_GROUP)
    tile0 = logits[0:EXPERTS_PER_GROUP]
    is_g = sub < N_GROUPS
    gmax = jnp.max(jnp.where(is_g, tile0, neg), axis=0, keepdims=True)
    gsum = jnp.sum(jnp.where(is_g, jnp.exp(tile0 - gmax), 0.0), axis=0, keepdims=True)
    p_g = 1.0 / gsum
    gidx = jnp.min(jnp.where(is_g & (tile0 == gmax), sub, big), axis=0, keepdims=True)
    el = logits[N_GROUPS * EXPERTS_PER_GROUP:(N_GROUPS + 1) * EXPERTS_PER_GROUP]
    for g in range(N_GROUPS - 2, -1, -1):
        el = jnp.where(gidx == g, logits[(g + 1) * EXPERTS_PER_GROUP:(g + 2) * EXPERTS_PER_GROUP], el)
    m1 = jnp.max(el, axis=0, keepdims=True)
    i1 = jnp.min(jnp.where(el == m1, sub, big), axis=0, keepdims=True)
    rest = sub != i1
    m2 = jnp.max(jnp.where(rest, el, neg), axis=0, keepdims=True)
    i2 = jnp.min(jnp.where(rest & (el == m2), sub, big), axis=0, keepdims=True)
    t2 = jnp.exp(m2 - m1)
    den = 1.0 + t2
    g1 = p_g / den
    g2 = p_g * t2 / den
    e1 = gidx * EXPERTS_PER_GROUP + i1
    e2 = gidx * EXPERTS_PER_GROUP + i2

    eid = lax.broadcasted_iota(I32, (N_EXPERTS, tb), 0).astype(F32)
    oh1 = eid == e1
    oh2 = eid == e2
    both = jnp.where(oh1 | oh2, 1.0, 0.0)
    cnt = cnt_ref[...]
    before = _dot(both.astype(BF16), sup_ref[...]) + cnt
    front_tiles(5)
    assert tiles_done[0] * FRONT_TILE == D_MAIN
    pos1 = jnp.sum(jnp.where(oh1, before, 0.0), axis=0, keepdims=True)
    pos2 = jnp.sum(jnp.where(oh2, before, 0.0), axis=0, keepdims=True)
    cnt_new = cnt + jnp.sum(both, axis=1, keepdims=True)
    cnt_ref[...] = cnt_new
    cnt_out_ref[...] = cnt_new

    zero = jnp.zeros_like(e1)
    route_t = jnp.concatenate([e1, e2, g1, g2, pos1, pos2, zero, zero], axis=0)
    route_t_ref[0] = route_t
    route_ref[...] = jnp.concatenate([route_t, jnp.zeros((LANES - ROUTE_ROWS, tb), F32)], axis=0).T

    if chained:
        @pl.when(j == nj - 1)
        def _():
            pool_out_ref[0] = ext_ref[0:POOL_ROWS, :]
            st_out_ref[0] = st_ref[...]


def _mixer(x2d, pool0, st0, cnt0, weights, *, batch, seq, tb, lead_pad):
    chained = tb <= seq
    total_rows = batch * seq
    n_blk = total_rows // tb
    nj = seq // tb if chained else 1
    per_blk = 1 if chained else tb // seq
    assert chained or (seq == CHUNK and batch % per_blk == 0)
    shared = pool0.shape[0] == 1
    front = lambda s: jnp.minimum(s, n_blk - 1)
    back = lambda s: jnp.maximum(s - 1, 0)
    stream = lambda s: back(s) // nj
    st_idx = (lambda s: (0, 0, 0)) if shared else (lambda s: (stream(s), 0, 0))
    gla_idx = (lambda s: (0, 0, 0, 0)) if shared else (lambda s: (stream(s), 0, 0, 0))
    const2 = lambda s: (0, 0)
    tok_out = lambda s: (back(s), 0)

    seg = min(tb, MXU_DIM)
    ii = jnp.arange(seg)
    tril = ((ii[:, None] >= ii[None, :]) & (ii[:, None] // CHUNK == ii[None, :] // CHUNK)).astype(BF16)
    ii = jnp.arange(tb)
    sup = (ii[:, None] < ii[None, :]).astype(BF16)

    in_specs = [
        pl.BlockSpec((tb, D_MODEL), lambda s: (front(s), 0)),
        pl.BlockSpec((per_blk, POOL_ROWS, D_POOL), st_idx),
        pl.BlockSpec((per_blk, GLA_HEADS, GLA_DK, GLA_DV), gla_idx),
        pl.BlockSpec((N_EXPERTS, 1), const2),
        pl.BlockSpec((seg, seg), const2),
        pl.BlockSpec((tb, tb), const2),
    ]
    for wgt in weights:
        in_specs.append(pl.BlockSpec(wgt.shape, (lambda s, n=wgt.ndim: (0,) * n)))
    args = [x2d, pool0, st0, cnt0, tril, sup, *weights]

    out_shape = [
        jax.ShapeDtypeStruct((total_rows, D_MODEL), F32),
        jax.ShapeDtypeStruct((HN_PLANES, total_rows, LANES), I32),
        jax.ShapeDtypeStruct((total_rows, LANES), F32),
        jax.ShapeDtypeStruct((n_blk, ROUTE_ROWS, tb), F32),
        jax.ShapeDtypeStruct((batch, POOL_ROWS, D_POOL), F32),
        jax.ShapeDtypeStruct((batch, GLA_HEADS, GLA_DK, GLA_DV), F32),
        jax.ShapeDtypeStruct((N_EXPERTS, 1), F32),
    ]
    out_specs = [
        pl.BlockSpec((tb, D_MODEL), tok_out),
        pl.BlockSpec((HN_PLANES, tb, LANES), lambda s: (0, back(s), 0)),
        pl.BlockSpec((tb, LANES), tok_out),
        pl.BlockSpec((1, ROUTE_ROWS, tb), lambda s: (back(s), 0, 0)),
        pl.BlockSpec((per_blk, POOL_ROWS, D_POOL), lambda s: (stream(s), 0, 0)),
        pl.BlockSpec((per_blk, GLA_HEADS, GLA_DK, GLA_DV), lambda s: (stream(s), 0, 0, 0)),
        pl.BlockSpec((N_EXPERTS, 1), const2),
    ]
    n_grp = GLA_HEADS // HEAD_GROUP
    scratch = [
        pltpu.VMEM((POOL_ROWS + tb, D_POOL), F32),
        pltpu.VMEM((GLA_HEADS, GLA_DK, GLA_DV), F32),
        pltpu.VMEM((tb // CHUNK, n_grp, HEAD_GROUP * CHUNK, HEAD_GROUP * GLA_DK), BF16),
        pltpu.VMEM((tb // CHUNK, n_grp, HEAD_GROUP * CHUNK, HEAD_GROUP * GLA_DV), BF16),
        pltpu.VMEM((tb // CHUNK, n_grp, HEAD_GROUP * GLA_DK, HEAD_GROUP * GLA_DV), BF16),
        pltpu.VMEM((tb, D_V), F32),
        pltpu.VMEM((N_EXPERTS, 1), F32),
        pltpu.VMEM((2, tb, D_MODEL), F32),
        pltpu.VMEM((2, tb, D_MAIN), F32),
        pltpu.VMEM((2, GATE_RANK, tb), F32),
    ]
    return pl.pallas_call(
        functools.partial(_mixer_kernel, tb=tb, nj=nj, lead_pad=lead_pad, chained=chained),
        grid=(n_blk + 1,),
        in_specs=in_specs,
        out_specs=out_specs,
        out_shape=out_shape,
        scratch_shapes=scratch,
        compiler_params=pltpu.CompilerParams(
            dimension_semantics=("arbitrary",), vmem_limit_bytes=VMEM_LIMIT),
        name=f"mixer_tb{tb}_pad{lead_pad}",
    )(*args)


def _sc_mesh():
    return plsc.VectorSubcoreMesh(core_axis_name="c", subcore_axis_name="s",
                                  num_cores=SC_CORES, num_subcores=SC_SUBCORES)


def _sc_worker():
    return lax.axis_index("s") * SC_CORES + lax.axis_index("c")


def _plane_rows(dest, planes, rows_per_plane):
    offs = (jnp.arange(planes, dtype=I32) * rows_per_plane)[None, :, None]
    return dest[:, None, :] + offs


def _dispatch(hn_p, hn_s, dest0, dest1, n_rows):
    planes, t_p, _ = hn_p.shape
    t_s = hn_s.shape[1]
    n_cp = t_p // SC_ROWS
    cp = n_cp // SC_WORKERS
    n_cs = t_s // SC_ROWS
    assert t_p == cp * SC_ROWS * SC_WORKERS and t_s == n_cs * SC_ROWS and n_cs <= SC_WORKERS
    idx0 = _plane_rows(dest0, planes, n_rows)
    idx1 = _plane_rows(dest1, planes, n_rows)

    def body(hnp_hbm, hns_hbm, d0_hbm, d1_hbm, xs_hbm, rows_v, i0_v, i1_v, is0_v, is1_v, sem_in, sem_out):
        wid = _sc_worker()
        pltpu.sync_copy(d0_hbm.at[pl.ds(wid * cp, cp)], i0_v)
        pltpu.sync_copy(d1_hbm.at[pl.ds(wid * cp, cp)], i1_v)

        def move(src_hbm, src_rows, row0, i0, i1, c):
            loads = [pltpu.async_copy(src_hbm.at[pl.ds(p * src_rows + row0, SC_ROWS)], rows_v.at[p], sem_in)
                     for p in range(planes)]
            for cpy in loads:
                cpy.wait()
            stores = []
            for p in range(planes):
                stores.append(pltpu.async_copy(rows_v.at[p], xs_hbm.at[i0.at[c, p]], sem_out))
                stores.append(pltpu.async_copy(rows_v.at[p], xs_hbm.at[i1.at[c, p]], sem_out))
            for cpy in stores:
                cpy.wait()

        @pl.loop(0, cp)
        def _(c):
            move(hnp_hbm, t_p, (wid * cp + c) * SC_ROWS, i0_v, i1_v, c)

        @pl.when(wid < n_cs)
        def _():
            pltpu.sync_copy(d0_hbm.at[pl.ds(n_cp + wid, 1)], is0_v)
            pltpu.sync_copy(d1_hbm.at[pl.ds(n_cp + wid, 1)], is1_v)
            move(hns_hbm, t_s, wid * SC_ROWS, is0_v, is1_v, 0)

    xs = pl.kernel(
        body,
        out_type=jax.ShapeDtypeStruct((planes * n_rows, LANES), I32),
        mesh=_sc_mesh(),
        scratch_types=[
            pltpu.VMEM((planes, SC_ROWS, LANES), I32),
            pltpu.VMEM((cp, planes, SC_ROWS), I32),
            pltpu.VMEM((cp, planes, SC_ROWS), I32),
            pltpu.VMEM((1, planes, SC_ROWS), I32),
            pltpu.VMEM((1, planes, SC_ROWS), I32),
            pltpu.SemaphoreType.DMA,
            pltpu.SemaphoreType.DMA,
        ],
        name="moe_dispatch_sc",
    )(hn_p.reshape(planes * t_p, LANES), hn_s.reshape(planes * t_s, LANES), idx0, idx1)
    return xs.reshape(planes, n_rows, LANES)


def _gather(ys, dest0, dest1):
    planes, n_rows, _ = ys.shape
    n_chunks = dest0.shape[0]
    n_tok = n_chunks * SC_ROWS
    cpw = max(n_chunks // SC_WORKERS, 1)
    assert n_chunks <= SC_WORKERS or n_chunks == cpw * SC_WORKERS
    idx = (_plane_rows(dest0, planes, n_rows), _plane_rows(dest1, planes, n_rows))

    def body(ys_hbm, d0_hbm, d1_hbm, out_hbm, rows_v, i0_v, i1_v, sem_in, sem_out):
        wid = _sc_worker()

        def work():
            pltpu.sync_copy(d0_hbm.at[pl.ds(wid * cpw, cpw)], i0_v)
            pltpu.sync_copy(d1_hbm.at[pl.ds(wid * cpw, cpw)], i1_v)

            @pl.loop(0, cpw)
            def _(c):
                row0 = (wid * cpw + c) * SC_ROWS
                for j, i_v in enumerate((i0_v, i1_v)):
                    loads = [pltpu.async_copy(ys_hbm.at[i_v.at[c, p]], rows_v.at[p], sem_in)
                             for p in range(planes)]
                    for cpy in loads:
                        cpy.wait()
                    stores = [
                        pltpu.async_copy(
                            rows_v.at[p], out_hbm.at[pl.ds((j * planes + p) * n_tok + row0, SC_ROWS)], sem_out)
                        for p in range(planes)]
                    for cpy in stores:
                        cpy.wait()

        if n_chunks < SC_WORKERS:
            pl.when(wid < n_chunks)(work)
        else:
            work()

    out = pl.kernel(
        body,
        out_type=jax.ShapeDtypeStruct((2 * planes * n_tok, LANES), ys.dtype),
        mesh=_sc_mesh(),
        scratch_types=[
            pltpu.VMEM((planes, SC_ROWS, LANES), ys.dtype),
            pltpu.VMEM((cpw, planes, SC_ROWS), I32),
            pltpu.VMEM((cpw, planes, SC_ROWS), I32),
            pltpu.SemaphoreType.DMA,
            pltpu.SemaphoreType.DMA,
        ],
        name="moe_gather_sc",
    )(ys.reshape(planes * n_rows, LANES), *idx)
    return out.reshape(2, planes, n_tok, LANES)


def _expert_kernel(be_ref, bv_ref, nb_ref, xs_ref, wg_ref, wu_ref, wd_ref, ys_ref, wgu_s, wd_s):
    i = pl.program_id(0)
    live = i < nb_ref[0]
    prev = be_ref[jnp.maximum(i - 1, 0)]
    fresh = (i == 0) | (be_ref[i] != prev)

    @pl.when(live & fresh)
    def _():
        wgu_s[:, 0:D_EXPERT] = wg_ref[0].astype(BF16)
        wgu_s[:, D_EXPERT:2 * D_EXPERT] = wu_ref[0].astype(BF16)
        wd_s[...] = wd_ref[0].astype(BF16)

    @pl.when(live)
    def _():
        valid = lax.broadcasted_iota(I32, (MOE_BLK, LANES), 0) < bv_ref[i]
        xb = _unpack_planes([jnp.where(valid, xs_ref[p], 0) for p in range(HN_PLANES)]).astype(BF16)
        gu = _dot(xb, wgu_s[...])
        gate = gu[:, 0:D_EXPERT]
        hmid = gate * jax.nn.sigmoid(gate) * gu[:, D_EXPERT:]
        _pack_planes(_dot(hmid.astype(BF16), wd_s[...]), ys_ref)

    @pl.when(jnp.logical_not(live))
    def _():
        ys_ref[...] = jnp.zeros_like(ys_ref)


def _experts(block_e, block_valid, nb, xs, w_eg, w_eu, w_ed):
    n_rows = xs.shape[1]
    n_blocks = n_rows // MOE_BLK
    row_idx = lambda i, be, bv, nb: (0, jnp.minimum(i, nb[0] - 1), 0)
    w_idx = lambda i, be, bv, nb: (be[i], 0, 0)
    grid_spec = pltpu.PrefetchScalarGridSpec(
        num_scalar_prefetch=3,
        grid=(n_blocks,),
        in_specs=[
            pl.BlockSpec((HN_PLANES, MOE_BLK, LANES), row_idx),
            pl.BlockSpec((1, D_MODEL, D_EXPERT), w_idx),
            pl.BlockSpec((1, D_MODEL, D_EXPERT), w_idx),
            pl.BlockSpec((1, D_EXPERT, D_MODEL), w_idx),
        ],
        out_specs=pl.BlockSpec((HN_PLANES, MOE_BLK, LANES), lambda i, be, bv, nb: (0, i, 0)),
        scratch_shapes=[
            pltpu.VMEM((D_MODEL, 2 * D_EXPERT), BF16),
            pltpu.VMEM((D_EXPERT, D_MODEL), BF16),
        ],
    )
    return pl.pallas_call(
        _expert_kernel,
        grid_spec=grid_spec,
        out_shape=jax.ShapeDtypeStruct((HN_PLANES, n_rows, LANES), I32),
        compiler_params=pltpu.CompilerParams(
            dimension_semantics=("arbitrary",), vmem_limit_bytes=VMEM_LIMIT),
        name="moe_experts",
    )(block_e, block_valid, nb, xs, w_eg, w_eu, w_ed)


def _combine_kernel(h_ref, route_ref, nfin_ref, rows_ref, *rest):
    y_ref = rest[-1]
    route = route_ref[...]
    ys1 = _unpack_planes([rows_ref[0, p] for p in range(HN_PLANES)])
    ys2 = _unpack_planes([rows_ref[1, p] for p in range(HN_PLANES)])
    out = h_ref[...] + (ys1 * route[:, 2:3] + ys2 * route[:, 3:4])
    y_ref[...] = _rms(out, nfin_ref[...])


def _combine(h, route, norm_final, rows, *, row0=0, y_prev=None):
    total = h.shape[0]
    tb = COMBINE_TB
    blk0 = row0 // tb
    args = [h, route, norm_final, rows]
    in_specs = [
        pl.BlockSpec((tb, D_MODEL), lambda i: (blk0 + i, 0)),
        pl.BlockSpec((tb, LANES), lambda i: (blk0 + i, 0)),
        pl.BlockSpec((1, D_MODEL), lambda i: (0, 0)),
        pl.BlockSpec((2, HN_PLANES, tb, LANES), lambda i: (0, 0, i, 0)),
    ]
    aliases = {}
    if y_prev is not None:
        aliases[len(args)] = 0
        args.append(y_prev)
        in_specs.append(pl.BlockSpec(memory_space=pl.ANY))
    return pl.pallas_call(
        _combine_kernel,
        grid=(rows.shape[2] // tb,),
        in_specs=in_specs,
        out_specs=pl.BlockSpec((tb, D_MODEL), lambda i: (blk0 + i, 0)),
        out_shape=jax.ShapeDtypeStruct((total, D_MODEL), F32),
        input_output_aliases=aliases,
        compiler_params=pltpu.CompilerParams(
            dimension_semantics=("arbitrary",), vmem_limit_bytes=VMEM_LIMIT),
        name="moe_combine",
    )(*args)


def kernel(x_prompt, x_sample, state_pool, state_gla, meta_tokens, norm_mix, w_in, w_gate_up, b_gate, w_pool, pool_scale, gla_norm, w_out, norm_ffn, w_router_group, b_router_group, w_router_expert, b_router_expert, w_expert_gate, w_expert_up, w_expert_down, norm_final):
    assert w_in.shape[0] == 1, "one encoder layer"
    batch, seq, _ = x_prompt.shape
    dec_batch, dec_seq, _ = x_sample.shape
    assert seq % MIX_TB == 0 and dec_seq == CHUNK and N_META <= CHUNK
    t_prompt = batch * seq
    t_sample = dec_batch * dec_seq
    t_all = t_prompt + t_sample
    assert t_prompt % COMBINE_TB == 0 and t_sample % COMBINE_TB == 0

    w_in0 = w_in[0]
    gpad = EXPERTS_PER_GROUP - N_GROUPS
    rpad = ROUTER_ROWS - EXPERTS_PER_GROUP - N_EXPERTS
    w_router = jnp.concatenate([
        w_router_group[0].T, jnp.zeros((gpad, D_MODEL), F32),
        w_router_expert[0].T, jnp.zeros((rpad, D_MODEL), F32)], axis=0)
    b_router = jnp.concatenate([
        b_router_group[0], jnp.zeros((gpad,), F32), b_router_expert[0], jnp.zeros((rpad,), F32)])
    zg = jnp.zeros((POOL_GROUP_DIM, POOL_GROUP_DIM), F32)
    w_pool_pairs = jnp.stack([
        jnp.block([[w_pool[0, 2 * i], zg], [zg, w_pool[0, 2 * i + 1]]]) for i in range(len(POOL_WINDOWS) // 2)])
    weights = (
        norm_mix[0][None, :],
        w_in0[:, :D_MAIN].astype(BF16),
        w_in0[:, D_MAIN:].T.astype(BF16),
        w_gate_up[0].astype(BF16),
        b_gate[0][None, :],
        w_pool_pairs.astype(BF16),
        pool_scale[0][None, :],
        gla_norm[0][None, :],
        w_out[0].astype(BF16),
        norm_ffn[0][None, :],
        w_router.astype(BF16),
        b_router[:, None],
    )

    zero_cnt = jnp.zeros((N_EXPERTS, 1), F32)
    x_meta = jnp.pad(meta_tokens.astype(F32), ((CHUNK - N_META, 0), (0, 0)))
    meta = _mixer(x_meta, jnp.zeros((1, POOL_ROWS, D_POOL), F32),
                  jnp.zeros((1, GLA_HEADS, GLA_DK, GLA_DV), F32), zero_cnt,
                  weights, batch=1, seq=CHUNK, tb=CHUNK, lead_pad=CHUNK - N_META)
    h_m, hn_m, route_m, rt_m, pool_m, st_m, cnt_m = meta
    del h_m, hn_m, route_m, rt_m, cnt_m
    h_p, hn_p, route_p, rt_p, pool_p, st_p, cnt_p = _mixer(
        x_prompt.reshape(t_prompt, D_MODEL), pool_m, st_m, zero_cnt, weights,
        batch=batch, seq=seq, tb=MIX_TB, lead_pad=0)
    pool_s0 = jnp.pad(state_pool[0], ((0, 0), (POOL_ROWS - POOL_PAD, 0), (0, 0)))
    h_s, hn_s, route_s, rt_s, pool_s, st_s, cnt_s = _mixer(
        x_sample.reshape(t_sample, D_MODEL), pool_s0, state_gla[0].astype(F32),
        cnt_p, weights, batch=dec_batch, seq=dec_seq, tb=SAMPLE_TB, lead_pad=0)

    counts = cnt_s[:, 0].astype(I32)
    padded = (counts + MOE_BLK - 1) // MOE_BLK * MOE_BLK
    ends = jnp.cumsum(padded)
    pstart = ends - padded
    n_blocks = (2 * t_all + N_EXPERTS * (MOE_BLK - 1) + MOE_BLK - 1) // MOE_BLK
    nb = (ends[-1] // MOE_BLK).astype(I32)
    blk_ids = jnp.minimum(jnp.arange(n_blocks, dtype=I32), nb - 1)
    block_e = jnp.sum((ends[None, :] <= (blk_ids * MOE_BLK)[:, None]).astype(I32), axis=1)
    block_e = jnp.minimum(block_e, N_EXPERTS - 1)
    owner = block_e[:, None] == jnp.arange(N_EXPERTS, dtype=I32)
    row_end = jnp.sum(jnp.where(owner, pstart + counts, 0), axis=1)
    block_valid = jnp.clip(row_end - blk_ids * MOE_BLK, 0, MOE_BLK)

    def dest_rows(rt):
        rt = rt.transpose(1, 0, 2).reshape(ROUTE_ROWS, -1)
        onehot = rt[0:2].astype(I32)[..., None] == jnp.arange(N_EXPERTS, dtype=I32)
        return jnp.sum(jnp.where(onehot, pstart, 0), axis=-1) + rt[4:6].astype(I32)

    dest = jnp.concatenate([dest_rows(rt_p), dest_rows(rt_s)], axis=1)
    dest0 = dest[0].reshape(t_all // SC_ROWS, SC_ROWS)
    dest1 = dest[1].reshape(t_all // SC_ROWS, SC_ROWS)
    xs = _dispatch(hn_p, hn_s, dest0, dest1, n_blocks * MOE_BLK)
    ys = _experts(block_e, block_valid.astype(I32), nb[None], xs,
                  w_expert_gate[0], w_expert_up[0], w_expert_down[0])
    nfin = norm_final[None, :]
    cp_chunks = t_prompt // SC_ROWS
    part = cp_chunks // COMBINE_PARTS
    y_prompt = None
    for i in range(COMBINE_PARTS):
        ch = slice(i * part, (i + 1) * part)
        rows_i = _gather(ys, dest0[ch], dest1[ch])
        y_prompt = _combine(h_p, route_p, nfin, rows_i, row0=i * part * SC_ROWS, y_prev=y_prompt)
    y_prompt = y_prompt.reshape(batch, seq, D_MODEL)
    rows_s = _gather(ys, dest0[cp_chunks:], dest1[cp_chunks:])
    y_sample = _combine(h_s, route_s, nfin, rows_s).reshape(dec_batch, dec_seq, D_MODEL)
    new_pool_prompt = pool_p[:, POOL_ROWS - POOL_PAD:][None]
    new_gla_prompt = st_p[None]
    new_pool_sample = pool_s[:, POOL_ROWS - POOL_PAD:][None]
    new_gla_sample = st_s[None]
    return (y_prompt, y_sample, new_pool_prompt, new_gla_prompt, new_pool_sample, new_gla_sample)
```

```python
import functools

import jax
import jax.numpy as jnp
from jax import lax
from jax.experimental import pallas as pl
from jax.experimental.pallas import tpu as pltpu
from jax.experimental.pallas import tpu_sc as plsc

F32 = jnp.float32
BF16 = jnp.bfloat16
U32 = jnp.uint32
I32 = jnp.int32

D_MODEL = 1024
N_META = 16
CHUNK = 64
EPS = 1e-6
D_POOL = 512
POOL_WINDOWS = (2, 4, 8, 16)
POOL_GROUP_DIM = 128
POOL_PAD = 15
POOL_ROWS = 16
GLA_HEADS = 4
GLA_DK = 64
GLA_DV = 128
D_QK = 256
D_V = 512
HEAD_GROUP = 2
GATE_RANK = 16
GATE_TAU = 16.0
D_MAIN = D_POOL + 2 * D_QK + 2 * D_V
N_GROUPS = 4
EXPERTS_PER_GROUP = 8
N_EXPERTS = 32
D_EXPERT = 512

LANES = 128
MXU_DIM = 256
HALF = D_MODEL // 2
HN_PLANES = HALF // LANES
ROUTE_ROWS = 8
ROUTER_ROWS = 64
MIX_TB = 512
SAMPLE_TB = 256
FRONT_TILE = 256
FRONT_PLAN = (3, 0, 1, 2, 2, 0)
MOE_BLK = 768
COMBINE_TB = 512
COMBINE_PARTS = 8
VMEM_LIMIT = 48 * 1024 * 1024
SC_CORES = 2
SC_SUBCORES = 16
SC_WORKERS = SC_CORES * SC_SUBCORES
SC_ROWS = 128


def _rms(x, g):
    return x * lax.rsqrt(jnp.mean(x * x, axis=-1, keepdims=True) + EPS) * g


def _dot(a, b):
    return jnp.dot(a, b, preferred_element_type=F32)


def _dot_nt(a, b):
    return lax.dot_general(a, b, (((1,), (1,)), ((), ())), preferred_element_type=F32)


def _pack_planes(x, ref):
    xb = x.astype(BF16)
    lo = lax.bitcast_convert_type(xb[:, :HALF].astype(F32), U32) >> 16
    hi = lax.bitcast_convert_type(xb[:, HALF:].astype(F32), U32) & jnp.uint32(0xFFFF0000)
    packed = lax.bitcast_convert_type(lo | hi, I32)
    for p in range(HN_PLANES):
        ref[p] = packed[:, p * LANES:(p + 1) * LANES]


def _unpack_planes(planes):
    words = [lax.bitcast_convert_type(p, U32) for p in planes]
    los = [lax.bitcast_convert_type(w << 16, F32) for w in words]
    his = [lax.bitcast_convert_type(w & jnp.uint32(0xFFFF0000), F32) for w in words]
    return jnp.concatenate(los + his, axis=-1)


def _dot_tn(a, b):
    return lax.dot_general(a, b, (((0,), (0,)), ((), ())), preferred_element_type=F32)


def _mixer_kernel(x_ref, pool0_ref, st0_ref, cnt0_ref, tril_ref, sup_ref,
                  nmix_ref, wmain_ref, wz_ref, wgu_ref, bgate_ref, wpool_ref, pscale_ref,
                  gnorm_ref, wout_ref, nffn_ref, wr_ref, br_ref,
                  h_ref, hn_ref, route_ref, route_t_ref, pool_out_ref, st_out_ref, cnt_out_ref,
                  ext_ref, st_ref, kbd_ref, vbd_ref, sbd_ref, o_ref, cnt_ref,
                  xs_ref, proj_ref, z_ref, *, tb, nj, lead_pad, chained):
    s = pl.program_id(0)
    back = jnp.maximum(s - 1, 0)
    j = lax.rem(back, nj)
    n_chunks = tb // CHUNK
    wr_slot = lax.rem(s, 2)
    rd_slot = 1 - wr_slot

    @pl.when(s == 0)
    def _():
        kbd_ref[...] = jnp.zeros_like(kbd_ref)
        vbd_ref[...] = jnp.zeros_like(vbd_ref)
        sbd_ref[...] = jnp.zeros_like(sbd_ref)
        xs_ref[1] = jnp.zeros((tb, D_MODEL), F32)
        proj_ref[1] = jnp.zeros((tb, D_MAIN), F32)
        z_ref[1] = jnp.zeros((GATE_RANK, tb), F32)

    @pl.when(s <= 1)
    def _():
        cnt_ref[...] = cnt0_ref[...]

    def put_state(c, hh, st):
        gg, hp = divmod(hh, HEAD_GROUP)
        sbd_ref[c, gg, hp * GLA_DK:(hp + 1) * GLA_DK, hp * GLA_DV:(hp + 1) * GLA_DV] = st.astype(BF16)

    if chained:
        @pl.when(j == 0)
        def _():
            ext_ref[0:POOL_ROWS, :] = pool0_ref[0]
            st_ref[...] = st0_ref[0]

        for hh in range(GLA_HEADS):
            put_state(0, hh, st_ref[hh])
    else:
        for c in range(n_chunks):
            for hh in range(GLA_HEADS):
                put_state(c, hh, st0_ref[c, hh])

    x_new = x_ref[...]
    xn = _rms(x_new, nmix_ref[...]).astype(BF16)
    xs_ref[wr_slot] = x_new

    tiles_done = [0]

    def front_tiles(stage):
        for t in range(tiles_done[0], tiles_done[0] + FRONT_PLAN[stage]):
            cols = slice(t * FRONT_TILE, (t + 1) * FRONT_TILE)
            proj_ref[wr_slot, :, cols] = _dot(xn, wmain_ref[:, cols])
        tiles_done[0] += FRONT_PLAN[stage]

    front_tiles(0)
    z_ref[wr_slot] = _dot_nt(wz_ref[...], xn)

    x = xs_ref[rd_slot]
    z = z_ref[rd_slot]
    u = proj_ref[rd_slot, :, 0:D_POOL]
    q = proj_ref[rd_slot, :, D_POOL:D_POOL + D_QK]
    k = proj_ref[rd_slot, :, D_POOL + D_QK:D_POOL + 2 * D_QK]
    v = proj_ref[rd_slot, :, D_POOL + 2 * D_QK:D_POOL + 2 * D_QK + D_V]
    r = proj_ref[rd_slot, :, D_POOL + 2 * D_QK + D_V:D_MAIN]

    row = lax.broadcasted_iota(I32, (tb, 1), 0)

    pseg = POOL_ROWS + CHUNK
    if chained:
        ext_ref[POOL_ROWS:POOL_ROWS + tb, :] = u
        ext = ext_ref[...]
    else:
        ext = jnp.concatenate(
            [blk for c in range(n_chunks) for blk in (pool0_ref[c], u[c * CHUNK:(c + 1) * CHUNK])], axis=0)
    pooled = []
    for g, w in enumerate(POOL_WINDOWS):
        sl = slice(g * POOL_GROUP_DIM, (g + 1) * POOL_GROUP_DIM)
        acc = ext[:, sl]
        for d in range(g + 1):
            acc = acc + pltpu.roll(acc, 1 << d, axis=0)
        if chained:
            win = acc[POOL_ROWS:, :]
        else:
            win = jnp.concatenate([acc[c * pseg + POOL_ROWS:(c + 1) * pseg] for c in range(n_chunks)], axis=0)
        if lead_pad:
            cnt = jnp.clip(row - lead_pad + 1, 1, w).astype(F32)
            pooled.append(win / cnt - u[:, sl])
        else:
            pooled.append(win * (1.0 / w) - u[:, sl])
    pys = [_dot(jnp.concatenate(pooled[2 * i:2 * i + 2], axis=-1).astype(BF16), wpool_ref[i])
           for i in range(len(POOL_WINDOWS) // 2)]
    pool_y = jnp.concatenate(pys, axis=-1) * pscale_ref[...]
    if chained:
        ext_ref[0:POOL_ROWS, :] = ext_ref[tb:tb + POOL_ROWS, :]
    else:
        for c in range(n_chunks):
            pool_out_ref[c] = u[(c + 1) * CHUNK - POOL_ROWS:(c + 1) * CHUNK]

    gpre = _dot_tn(z.astype(BF16), wgu_ref[...]) + bgate_ref[...]
    log_a = (jnp.minimum(gpre, 0.0) - jnp.log(1.0 + jnp.exp(-jnp.abs(gpre)))) * (1.0 / GATE_TAU)
    if lead_pad:
        log_a = jnp.where(row >= lead_pad, log_a, 0.0)
    a_hi = log_a.astype(BF16)
    a_lo = (log_a - a_hi.astype(F32)).astype(BF16)
    tril = tril_ref[...]
    seg = tril.shape[0]
    bcum = jnp.concatenate(
        [_dot(tril, a_hi[r0:r0 + seg]) + _dot(tril, a_lo[r0:r0 + seg]) for r0 in range(0, tb, seg)], axis=0)
    front_tiles(1)
    eb = jnp.exp(bcum)
    qi = q * (GLA_DK ** -0.5) * eb
    ki = k * jnp.exp(-bcum)

    rr = lax.broadcasted_iota(I32, (CHUNK, HEAD_GROUP * CHUNK), 0)
    cc = lax.broadcasted_iota(I32, (CHUNK, HEAD_GROUP * CHUNK), 1)
    causal = (cc % CHUNK) <= rr

    lasts = [eb[(c + 1) * CHUNK - 1:(c + 1) * CHUNK, :] for c in range(n_chunks)]
    dcol = jnp.concatenate(lasts + [jnp.zeros((LANES - n_chunks, D_QK), F32)], axis=0).T

    n_grp = GLA_HEADS // HEAD_GROUP
    chunk_rows = [slice(c * CHUNK, (c + 1) * CHUNK) for c in range(n_chunks)]
    grp_k = [slice(g * HEAD_GROUP * GLA_DK, (g + 1) * HEAD_GROUP * GLA_DK) for g in range(n_grp)]
    grp_v = [slice(g * HEAD_GROUP * GLA_DV, (g + 1) * HEAD_GROUP * GLA_DV) for g in range(n_grp)]
    qi_b = qi.astype(BF16)
    ki_b = ki.astype(BF16)
    v_b = v.astype(BF16)

    scores = {}
    for c in range(n_chunks):
        for hh in range(GLA_HEADS):
            gg, hp = divmod(hh, HEAD_GROUP)
            kbd_ref[c, gg, hp * CHUNK:(hp + 1) * CHUNK, hp * GLA_DK:(hp + 1) * GLA_DK] = (
                ki_b[chunk_rows[c], hh * GLA_DK:(hh + 1) * GLA_DK])
            vbd_ref[c, gg, hp * CHUNK:(hp + 1) * CHUNK, hp * GLA_DV:(hp + 1) * GLA_DV] = (
                v_b[chunk_rows[c], hh * GLA_DV:(hh + 1) * GLA_DV])
        for gg in range(n_grp):
            scores[c, gg] = _dot_nt(qi_b[chunk_rows[c], grp_k[gg]], kbd_ref[c, gg])

    kvs = {}
    for c in range(n_chunks):
        for hh in range(GLA_HEADS):
            kvs[c, hh] = _dot_tn(ki_b[chunk_rows[c], hh * GLA_DK:(hh + 1) * GLA_DK],
                                 v_b[chunk_rows[c], hh * GLA_DV:(hh + 1) * GLA_DV])
    front_tiles(2)

    for hh in range(GLA_HEADS):
        st = st_ref[hh] if chained else None
        for c in range(n_chunks):
            s_old = st if chained else st0_ref[c, hh]
            s_new = (s_old + kvs[c, hh]) * dcol[hh * GLA_DK:(hh + 1) * GLA_DK, c:c + 1]
            if not chained:
                st_out_ref[c, hh] = s_new
            else:
                st = s_new
                if c + 1 < n_chunks:
                    put_state(c + 1, hh, s_new)
        if chained:
            st_ref[hh] = st

    for c in range(n_chunks):
        for gg in range(n_grp):
            p = jnp.where(causal, scores[c, gg], 0.0).astype(BF16)
            o_ref[chunk_rows[c], grp_v[gg]] = (
                _dot(p, vbd_ref[c, gg]) + _dot(qi_b[chunk_rows[c], grp_k[gg]], sbd_ref[c, gg]))

    o = o_ref[...]
    ons = []
    for hh in range(GLA_HEADS):
        oh = o[:, hh * GLA_DV:(hh + 1) * GLA_DV]
        ons.append(oh * lax.rsqrt(jnp.mean(oh * oh, axis=-1, keepdims=True) + EPS))
    og = jnp.concatenate(ons, axis=-1) * gnorm_ref[...] * (r * jax.nn.sigmoid(r))
    mix = _dot(jnp.concatenate([pool_y, og], axis=-1).astype(BF16), wout_ref[...])
    front_tiles(3)

    h = x + mix
    h_ref[...] = h
    hn = _rms(h, nffn_ref[...]).astype(BF16)
    _pack_planes(hn, hn_ref)

    logits = _dot_nt(wr_ref[...], hn) + br_ref[...]
    front_tiles(4)
    sub = lax.broadcasted_iota(I32, (EXPERTS_PER_GROUP, tb), 0).astype(F32)
    neg = jnp.float32(-jnp.inf)
    big = jnp.float32(EXPERTS_PER_GROUP)
    tile0 = logits[0:EXPERTS_PER_GROUP]
    is_g = sub < N_GROUPS
    gmax = jnp.max(jnp.where(is_g, tile0, neg), axis=0, keepdims=True)
    gsum = jnp.sum(jnp.where(is_g, jnp.exp(tile0 - gmax), 0.0), axis=0, keepdims=True)
    p_g = 1.0 / gsum
    gidx = jnp.min(jnp.where(is_g & (tile0 == gmax), sub, big), axis=0, keepdims=True)
    el = logits[N_GROUPS * EXPERTS_PER_GROUP:(N_GROUPS + 1) * EXPERTS_PER_GROUP]
    for g in range(N_GROUPS - 2, -1, -1):
        el = jnp.where(gidx == g, logits[(g + 1) * EXPERTS_PER_GROUP:(g + 2) * EXPERTS_PER_GROUP], el)
    m1 = jnp.max(el, axis=0, keepdims=True)
    i1 = jnp.min(jnp.where(el == m1, sub, big), axis=0, keepdims=True)
    rest = sub != i1
    m2 = jnp.max(jnp.where(rest, el, neg), axis=0, keepdims=True)
    i2 = jnp.min(jnp.where(rest & (el == m2), sub, big), axis=0, keepdims=True)
    t2 = jnp.exp(m2 - m1)
    den = 1.0 + t2
    g1 = p_g / den
    g2 = p_g * t2 / den
    e1 = gidx * EXPERTS_PER_GROUP + i1
    e2 = gidx * EXPERTS_PER_GROUP + i2

    eid = lax.broadcasted_iota(I32, (N_EXPERTS, tb), 0).astype(F32)
    oh1 = eid == e1
    oh2 = eid == e2
    both = jnp.where(oh1 | oh2, 1.0, 0.0)
    cnt = cnt_ref[...]
    before = _dot(both.astype(BF16), sup_ref[...]) + cnt
    front_tiles(5)
    assert tiles_done[0] * FRONT_TILE == D_MAIN
    pos1 = jnp.sum(jnp.where(oh1, before, 0.0), axis=0, keepdims=True)
    pos2 = jnp.sum(jnp.where(oh2, before, 0.0), axis=0, keepdims=True)
    cnt_new = cnt + jnp.sum(both, axis=1, keepdims=True)
    cnt_ref[...] = cnt_new
    cnt_out_ref[...] = cnt_new

    zero = jnp.zeros_like(e1)
    route_t = jnp.concatenate([e1, e2, g1, g2, pos1, pos2, zero, zero], axis=0)
    route_t_ref[0] = route_t
    route_ref[...] = jnp.concatenate([route_t, jnp.zeros((LANES - ROUTE_ROWS, tb), F32)], axis=0).T

    if chained:
        @pl.when(j == nj - 1)
        def _():
            pool_out_ref[0] = ext_ref[0:POOL_ROWS, :]
            st_out_ref[0] = st_ref[...]


def _mixer(x2d, pool0, st0, cnt0, weights, *, batch, seq, tb, lead_pad):
    chained = tb <= seq
    total_rows = batch * seq
    n_blk = total_rows // tb
    nj = seq // tb if chained else 1
    per_blk = 1 if chained else tb // seq
    assert chained or (seq == CHUNK and batch % per_blk == 0)
    shared = pool0.shape[0] == 1
    front = lambda s: jnp.minimum(s, n_blk - 1)
    back = lambda s: jnp.maximum(s - 1, 0)
    stream = lambda s: back(s) // nj
    st_idx = (lambda s: (0, 0, 0)) if shared else (lambda s: (stream(s), 0, 0))
    gla_idx = (lambda s: (0, 0, 0, 0)) if shared else (lambda s: (stream(s), 0, 0, 0))
    const2 = lambda s: (0, 0)
    tok_out = lambda s: (back(s), 0)

    seg = min(tb, MXU_DIM)
    ii = jnp.arange(seg)
    tril = ((ii[:, None] >= ii[None, :]) & (ii[:, None] // CHUNK == ii[None, :] // CHUNK)).astype(BF16)
    ii = jnp.arange(tb)
    sup = (ii[:, None] < ii[None, :]).astype(BF16)

    in_specs = [
        pl.BlockSpec((tb, D_MODEL), lambda s: (front(s), 0)),
        pl.BlockSpec((per_blk, POOL_ROWS, D_POOL), st_idx),
        pl.BlockSpec((per_blk, GLA_HEADS, GLA_DK, GLA_DV), gla_idx),
        pl.BlockSpec((N_EXPERTS, 1), const2),
        pl.BlockSpec((seg, seg), const2),
        pl.BlockSpec((tb, tb), const2),
    ]
    for wgt in weights:
        in_specs.append(pl.BlockSpec(wgt.shape, (lambda s, n=wgt.ndim: (0,) * n)))
    args = [x2d, pool0, st0, cnt0, tril, sup, *weights]

    out_shape = [
        jax.ShapeDtypeStruct((total_rows, D_MODEL), F32),
        jax.ShapeDtypeStruct((HN_PLANES, total_rows, LANES), I32),
        jax.ShapeDtypeStruct((total_rows, LANES), F32),
        jax.ShapeDtypeStruct((n_blk, ROUTE_ROWS, tb), F32),
        jax.ShapeDtypeStruct((batch, POOL_ROWS, D_POOL), F32),
        jax.ShapeDtypeStruct((batch, GLA_HEADS, GLA_DK, GLA_DV), F32),
        jax.ShapeDtypeStruct((N_EXPERTS, 1), F32),
    ]
    out_specs = [
        pl.BlockSpec((tb, D_MODEL), tok_out),
        pl.BlockSpec((HN_PLANES, tb, LANES), lambda s: (0, back(s), 0)),
        pl.BlockSpec((tb, LANES), tok_out),
        pl.BlockSpec((1, ROUTE_ROWS, tb), lambda s: (back(s), 0, 0)),
        pl.BlockSpec((per_blk, POOL_ROWS, D_POOL), lambda s: (stream(s), 0, 0)),
        pl.BlockSpec((per_blk, GLA_HEADS, GLA_DK, GLA_DV), lambda s: (stream(s), 0, 0, 0)),
        pl.BlockSpec((N_EXPERTS, 1), const2),
    ]
    n_grp = GLA_HEADS // HEAD_GROUP
    scratch = [
        pltpu.VMEM((POOL_ROWS + tb, D_POOL), F32),
        pltpu.VMEM((GLA_HEADS, GLA_DK, GLA_DV), F32),
        pltpu.VMEM((tb // CHUNK, n_grp, HEAD_GROUP * CHUNK, HEAD_GROUP * GLA_DK), BF16),
        pltpu.VMEM((tb // CHUNK, n_grp, HEAD_GROUP * CHUNK, HEAD_GROUP * GLA_DV), BF16),
        pltpu.VMEM((tb // CHUNK, n_grp, HEAD_GROUP * GLA_DK, HEAD_GROUP * GLA_DV), BF16),
        pltpu.VMEM((tb, D_V), F32),
        pltpu.VMEM((N_EXPERTS, 1), F32),
        pltpu.VMEM((2, tb, D_MODEL), F32),
        pltpu.VMEM((2, tb, D_MAIN), F32),
        pltpu.VMEM((2, GATE_RANK, tb), F32),
    ]
    return pl.pallas_call(
        functools.partial(_mixer_kernel, tb=tb, nj=nj, lead_pad=lead_pad, chained=chained),
        grid=(n_blk + 1,),
        in_specs=in_specs,
        out_specs=out_specs,
        out_shape=out_shape,
        scratch_shapes=scratch,
        compiler_params=pltpu.CompilerParams(
            dimension_semantics=("arbitrary",), vmem_limit_bytes=VMEM_LIMIT),
        name=f"mixer_tb{tb}_pad{lead_pad}",
    )(*args)


def _sc_mesh():
    return plsc.VectorSubcoreMesh(core_axis_name="c", subcore_axis_name="s",
                                  num_cores=SC_CORES, num_subcores=SC_SUBCORES)


def _sc_worker():
    return lax.axis_index("s") * SC_CORES + lax.axis_index("c")


def _plane_rows(dest, planes, rows_per_plane):
    offs = (jnp.arange(planes, dtype=I32) * rows_per_plane)[None, :, None]
    return dest[:, None, :] + offs


def _dispatch(hn_p, hn_s, dest0, dest1, n_rows):
    planes, t_p, _ = hn_p.shape
    t_s = hn_s.shape[1]
    n_cp = t_p // SC_ROWS
    cp = n_cp // SC_WORKERS
    n_cs = t_s // SC_ROWS
    assert t_p == cp * SC_ROWS * SC_WORKERS and t_s == n_cs * SC_ROWS and n_cs <= SC_WORKERS
    idx0 = _plane_rows(dest0, planes, n_rows)
    idx1 = _plane_rows(dest1, planes, n_rows)

    def body(hnp_hbm, hns_hbm, d0_hbm, d1_hbm, xs_hbm, rows_v, i0_v, i1_v, is0_v, is1_v, sem_in, sem_out):
        wid = _sc_worker()
        pltpu.sync_copy(d0_hbm.at[pl.ds(wid * cp, cp)], i0_v)
        pltpu.sync_copy(d1_hbm.at[pl.ds(wid * cp, cp)], i1_v)

        def move(src_hbm, src_rows, row0, i0, i1, c):
            loads = [pltpu.async_copy(src_hbm.at[pl.ds(p * src_rows + row0, SC_ROWS)], rows_v.at[p], sem_in)
                     for p in range(planes)]
            for cpy in loads:
                cpy.wait()
            stores = []
            for p in range(planes):
                stores.append(pltpu.async_copy(rows_v.at[p], xs_hbm.at[i0.at[c, p]], sem_out))
                stores.append(pltpu.async_copy(rows_v.at[p], xs_hbm.at[i1.at[c, p]], sem_out))
            for cpy in stores:
                cpy.wait()

        @pl.loop(0, cp)
        def _(c):
            move(hnp_hbm, t_p, (wid * cp + c) * SC_ROWS, i0_v, i1_v, c)

        @pl.when(wid < n_cs)
        def _():
            pltpu.sync_copy(d0_hbm.at[pl.ds(n_cp + wid, 1)], is0_v)
            pltpu.sync_copy(d1_hbm.at[pl.ds(n_cp + wid, 1)], is1_v)
            move(hns_hbm, t_s, wid * SC_ROWS, is0_v, is1_v, 0)

    xs = pl.kernel(
        body,
        out_type=jax.ShapeDtypeStruct((planes * n_rows, LANES), I32),
        mesh=_sc_mesh(),
        scratch_types=[
            pltpu.VMEM((planes, SC_ROWS, LANES), I32),
            pltpu.VMEM((cp, planes, SC_ROWS), I32),
            pltpu.VMEM((cp, planes, SC_ROWS), I32),
            pltpu.VMEM((1, planes, SC_ROWS), I32),
            pltpu.VMEM((1, planes, SC_ROWS), I32),
            pltpu.SemaphoreType.DMA,
            pltpu.SemaphoreType.DMA,
        ],
        name="moe_dispatch_sc",
    )(hn_p.reshape(planes * t_p, LANES), hn_s.reshape(planes * t_s, LANES), idx0, idx1)
    return xs.reshape(planes, n_rows, LANES)


def _gather(ys, dest0, dest1):
    planes, n_rows, _ = ys.shape
    n_chunks = dest0.shape[0]
    n_tok = n_chunks * SC_ROWS
    cpw = max(n_chunks // SC_WORKERS, 1)
    assert n_chunks <= SC_WORKERS or n_chunks == cpw * SC_WORKERS
    idx = (_plane_rows(dest0, planes, n_rows), _plane_rows(dest1, planes, n_rows))

    def body(ys_hbm, d0_hbm, d1_hbm, out_hbm, rows_v, i0_v, i1_v, sem_in, sem_out):
        wid = _sc_worker()

        def work():
            pltpu.sync_copy(d0_hbm.at[pl.ds(wid * cpw, cpw)], i0_v)
            pltpu.sync_copy(d1_hbm.at[pl.ds(wid * cpw, cpw)], i1_v)

            @pl.loop(0, cpw)
            def _(c):
                row0 = (wid * cpw + c) * SC_ROWS
                for j, i_v in enumerate((i0_v, i1_v)):
                    loads = [pltpu.async_copy(ys_hbm.at[i_v.at[c, p]], rows_v.at[p], sem_in)
                             for p in range(planes)]
                    for cpy in loads:
                        cpy.wait()
                    stores = [
                        pltpu.async_copy(
                            rows_v.at[p], out_hbm.at[pl.ds((j * planes + p) * n_tok + row0, SC_ROWS)], sem_out)
                        for p in range(planes)]
                    for cpy in stores:
                        cpy.wait()

        if n_chunks < SC_WORKERS:
            pl.when(wid < n_chunks)(work)
        else:
            work()

    out = pl.kernel(
        body,
        out_type=jax.ShapeDtypeStruct((2 * planes * n_tok, LANES), ys.dtype),
        mesh=_sc_mesh(),
        scratch_types=[
            pltpu.VMEM((planes, SC_ROWS, LANES), ys.dtype),
            pltpu.VMEM((cpw, planes, SC_ROWS), I32),
            pltpu.VMEM((cpw, planes, SC_ROWS), I32),
            pltpu.SemaphoreType.DMA,
            pltpu.SemaphoreType.DMA,
        ],
        name="moe_gather_sc",
    )(ys.reshape(planes * n_rows, LANES), *idx)
    return out.reshape(2, planes, n_tok, LANES)


def _expert_kernel(be_ref, bv_ref, nb_ref, xs_ref, wg_ref, wu_ref, wd_ref, ys_ref, wgu_s, wd_s):
    i = pl.program_id(0)
    live = i < nb_ref[0]
    prev = be_ref[jnp.maximum(i - 1, 0)]
    fresh = (i == 0) | (be_ref[i] != prev)

    @pl.when(live & fresh)
    def _():
        wgu_s[:, 0:D_EXPERT] = wg_ref[0].astype(BF16)
        wgu_s[:, D_EXPERT:2 * D_EXPERT] = wu_ref[0].astype(BF16)
        wd_s[...] = wd_ref[0].astype(BF16)

    @pl.when(live)
    def _():
        valid = lax.broadcasted_iota(I32, (MOE_BLK, LANES), 0) < bv_ref[i]
        xb = _unpack_planes([jnp.where(valid, xs_ref[p], 0) for p in range(HN_PLANES)]).astype(BF16)
        gu = _dot(xb, wgu_s[...])
        gate = gu[:, 0:D_EXPERT]
        hmid = gate * jax.nn.sigmoid(gate) * gu[:, D_EXPERT:]
        _pack_planes(_dot(hmid.astype(BF16), wd_s[...]), ys_ref)

    @pl.when(jnp.logical_not(live))
    def _():
        ys_ref[...] = jnp.zeros_like(ys_ref)


def _experts(block_e, block_valid, nb, xs, w_eg, w_eu, w_ed):
    n_rows = xs.shape[1]
    n_blocks = n_rows // MOE_BLK
    row_idx = lambda i, be, bv, nb: (0, jnp.minimum(i, nb[0] - 1), 0)
    w_idx = lambda i, be, bv, nb: (be[i], 0, 0)
    grid_spec = pltpu.PrefetchScalarGridSpec(
        num_scalar_prefetch=3,
        grid=(n_blocks,),
        in_specs=[
            pl.BlockSpec((HN_PLANES, MOE_BLK, LANES), row_idx),
            pl.BlockSpec((1, D_MODEL, D_EXPERT), w_idx),
            pl.BlockSpec((1, D_MODEL, D_EXPERT), w_idx),
            pl.BlockSpec((1, D_EXPERT, D_MODEL), w_idx),
        ],
        out_specs=pl.BlockSpec((HN_PLANES, MOE_BLK, LANES), lambda i, be, bv, nb: (0, i, 0)),
        scratch_shapes=[
            pltpu.VMEM((D_MODEL, 2 * D_EXPERT), BF16),
            pltpu.VMEM((D_EXPERT, D_MODEL), BF16),
        ],
    )
    return pl.pallas_call(
        _expert_kernel,
        grid_spec=grid_spec,
        out_shape=jax.ShapeDtypeStruct((HN_PLANES, n_rows, LANES), I32),
        compiler_params=pltpu.CompilerParams(
            dimension_semantics=("arbitrary",), vmem_limit_bytes=VMEM_LIMIT),
        name="moe_experts",
    )(block_e, block_valid, nb, xs, w_eg, w_eu, w_ed)


def _combine_kernel(h_ref, route_ref, nfin_ref, rows_ref, *rest):
    y_ref = rest[-1]
    route = route_ref[...]
    ys1 = _unpack_planes([rows_ref[0, p] for p in range(HN_PLANES)])
    ys2 = _unpack_planes([rows_ref[1, p] for p in range(HN_PLANES)])
    out = h_ref[...] + (ys1 * route[:, 2:3] + ys2 * route[:, 3:4])
    y_ref[...] = _rms(out, nfin_ref[...])


def _combine(h, route, norm_final, rows, *, row0=0, y_prev=None):
    total = h.shape[0]
    tb = COMBINE_TB
    blk0 = row0 // tb
    args = [h, route, norm_final, rows]
    in_specs = [
        pl.BlockSpec((tb, D_MODEL), lambda i: (blk0 + i, 0)),
        pl.BlockSpec((tb, LANES), lambda i: (blk0 + i, 0)),
        pl.BlockSpec((1, D_MODEL), lambda i: (0, 0)),
        pl.BlockSpec((2, HN_PLANES, tb, LANES), lambda i: (0, 0, i, 0)),
    ]
    aliases = {}
    if y_prev is not None:
        aliases[len(args)] = 0
        args.append(y_prev)
        in_specs.append(pl.BlockSpec(memory_space=pl.ANY))
    return pl.pallas_call(
        _combine_kernel,
        grid=(rows.shape[2] // tb,),
        in_specs=in_specs,
        out_specs=pl.BlockSpec((tb, D_MODEL), lambda i: (blk0 + i, 0)),
        out_shape=jax.ShapeDtypeStruct((total, D_MODEL), F32),
        input_output_aliases=aliases,
        compiler_params=pltpu.CompilerParams(
            dimension_semantics=("arbitrary",), vmem_limit_bytes=VMEM_LIMIT),
        name="moe_combine",
    )(*args)


def kernel(x_prompt, x_sample, state_pool, state_gla, meta_tokens, norm_mix, w_in, w_gate_up, b_gate, w_pool, pool_scale, gla_norm, w_out, norm_ffn, w_router_group, b_router_group, w_router_expert, b_router_expert, w_expert_gate, w_expert_up, w_expert_down, norm_final):
    assert w_in.shape[0] == 1, "one encoder layer"
    batch, seq, _ = x_prompt.shape
    dec_batch, dec_seq, _ = x_sample.shape
    assert seq % MIX_TB == 0 and dec_seq == CHUNK and N_META <= CHUNK
    t_prompt = batch * seq
    t_sample = dec_batch * dec_seq
    t_all = t_prompt + t_sample
    assert t_prompt % COMBINE_TB == 0 and t_sample % COMBINE_TB == 0

    w_in0 = w_in[0]
    gpad = EXPERTS_PER_GROUP - N_GROUPS
    rpad = ROUTER_ROWS - EXPERTS_PER_GROUP - N_EXPERTS
    w_router = jnp.concatenate([
        w_router_group[0].T, jnp.zeros((gpad, D_MODEL), F32),
        w_router_expert[0].T, jnp.zeros((rpad, D_MODEL), F32)], axis=0)
    b_router = jnp.concatenate([
        b_router_group[0], jnp.zeros((gpad,), F32), b_router_expert[0], jnp.zeros((rpad,), F32)])
    zg = jnp.zeros((POOL_GROUP_DIM, POOL_GROUP_DIM), F32)
    w_pool_pairs = jnp.stack([
        jnp.block([[w_pool[0, 2 * i], zg], [zg, w_pool[0, 2 * i + 1]]]) for i in range(len(POOL_WINDOWS) // 2)])
    weights = (
        norm_mix[0][None, :],
        w_in0[:, :D_MAIN].astype(BF16),
        w_in0[:, D_MAIN:].T.astype(BF16),
        w_gate_up[0].astype(BF16),
        b_gate[0][None, :],
        w_pool_pairs.astype(BF16),
        pool_scale[0][None, :],
        gla_norm[0][None, :],
        w_out[0].astype(BF16),
        norm_ffn[0][None, :],
        w_router.astype(BF16),
        b_router[:, None],
    )

    zero_cnt = jnp.zeros((N_EXPERTS, 1), F32)
    x_meta = jnp.pad(meta_tokens.astype(F32), ((CHUNK - N_META, 0), (0, 0)))
    meta = _mixer(x_meta, jnp.zeros((1, POOL_ROWS, D_POOL), F32),
                  jnp.zeros((1, GLA_HEADS, GLA_DK, GLA_DV), F32), zero_cnt,
                  weights, batch=1, seq=CHUNK, tb=CHUNK, lead_pad=CHUNK - N_META)
    h_m, hn_m, route_m, rt_m, pool_m, st_m, cnt_m = meta
    del h_m, hn_m, route_m, rt_m, cnt_m
    h_p, hn_p, route_p, rt_p, pool_p, st_p, cnt_p = _mixer(
        x_prompt.reshape(t_prompt, D_MODEL), pool_m, st_m, zero_cnt, weights,
        batch=batch, seq=seq, tb=MIX_TB, lead_pad=0)
    pool_s0 = jnp.pad(state_pool[0], ((0, 0), (POOL_ROWS - POOL_PAD, 0), (0, 0)))
    h_s, hn_s, route_s, rt_s, pool_s, st_s, cnt_s = _mixer(
        x_sample.reshape(t_sample, D_MODEL), pool_s0, state_gla[0].astype(F32),
        cnt_p, weights, batch=dec_batch, seq=dec_seq, tb=SAMPLE_TB, lead_pad=0)

    counts = cnt_s[:, 0].astype(I32)
    padded = (counts + MOE_BLK - 1) // MOE_BLK * MOE_BLK
    ends = jnp.cumsum(padded)
    pstart = ends - padded
    n_blocks = (2 * t_all + N_EXPERTS * (MOE_BLK - 1) + MOE_BLK - 1) // MOE_BLK
    nb = (ends[-1] // MOE_BLK).astype(I32)
    blk_ids = jnp.minimum(jnp.arange(n_blocks, dtype=I32), nb - 1)
    block_e = jnp.sum((ends[None, :] <= (blk_ids * MOE_BLK)[:, None]).astype(I32), axis=1)
    block_e = jnp.minimum(block_e, N_EXPERTS - 1)
    owner = block_e[:, None] == jnp.arange(N_EXPERTS, dtype=I32)
    row_end = jnp.sum(jnp.where(owner, pstart + counts, 0), axis=1)
    block_valid = jnp.clip(row_end - blk_ids * MOE_BLK, 0, MOE_BLK)

    def dest_rows(rt):
        rt = rt.transpose(1, 0, 2).reshape(ROUTE_ROWS, -1)
        onehot = rt[0:2].astype(I32)[..., None] == jnp.arange(N_EXPERTS, dtype=I32)
        return jnp.sum(jnp.where(onehot, pstart, 0), axis=-1) + rt[4:6].astype(I32)

    dest = jnp.concatenate([dest_rows(rt_p), dest_rows(rt_s)], axis=1)
    dest0 = dest[0].reshape(t_all // SC_ROWS, SC_ROWS)
    dest1 = dest[1].reshape(t_all // SC_ROWS, SC_ROWS)
    xs = _dispatch(hn_p, hn_s, dest0, dest1, n_blocks * MOE_BLK)
    ys = _experts(block_e, block_valid.astype(I32), nb[None], xs,
                  w_expert_gate[0], w_expert_up[0], w_expert_down[0])
    nfin = norm_final[None, :]
    cp_chunks = t_prompt // SC_ROWS
    part = cp_chunks // COMBINE_PARTS
    y_prompt = None
    for i in range(COMBINE_PARTS):
        ch = slice(i * part, (i + 1) * part)
        rows_i = _gather(ys, dest0[ch], dest1[ch])
        y_prompt = _combine(h_p, route_p, nfin, rows_i, row0=i * part * SC_ROWS, y_prev=y_prompt)
    y_prompt = y_prompt.reshape(batch, seq, D_MODEL)
    rows_s = _gather(ys, dest0[cp_chunks:], dest1[cp_chunks:])
    y_sample = _combine(h_s, route_s, nfin, rows_s).reshape(dec_batch, dec_seq, D_MODEL)
    new_pool_prompt = pool_p[:, POOL_ROWS - POOL_PAD:][None]
    new_gla_prompt = st_p[None]
    new_pool_sample = pool_s[:, POOL_ROWS - POOL_PAD:][None]
    new_gla_sample = st_s[None]
    return (y_prompt, y_sample, new_pool_prompt, new_gla_prompt, new_pool_sample, new_gla_sample)
```

```python
import functools

import jax
import jax.numpy as jnp
from jax import lax
from jax.experimental import pallas as pl
from jax.experimental.pallas import tpu as pltpu
from jax.experimental.pallas import tpu_sc as plsc

F32 = jnp.float32
BF16 = jnp.bfloat16
U32 = jnp.uint32
I32 = jnp.int32

D_MODEL = 1024
N_META = 16
CHUNK = 64
EPS = 1e-6
D_POOL = 512
POOL_WINDOWS = (2, 4, 8, 16)
POOL_GROUP_DIM = 128
POOL_PAD = 15
POOL_ROWS = 16
GLA_HEADS = 4
GLA_DK = 64
GLA_DV = 128
D_QK = 256
D_V = 512
HEAD_GROUP = 2
GATE_RANK = 16
GATE_TAU = 16.0
D_MAIN = D_POOL + 2 * D_QK + 2 * D_V
N_GROUPS = 4
EXPERTS_PER_GROUP = 8
N_EXPERTS = 32
D_EXPERT = 512

LANES = 128
MXU_DIM = 256
HALF = D_MODEL // 2
HN_PLANES = HALF // LANES
ROUTE_ROWS = 8
ROUTER_ROWS = 64
MIX_TB = 512
SAMPLE_TB = 256
FRONT_TILE = 256
FRONT_PLAN = (3, 0, 1, 2, 2, 0)
MOE_BLK = 768
COMBINE_TB = 512
COMBINE_PARTS = 4
VMEM_LIMIT = 48 * 1024 * 1024
SC_CORES = 2
SC_SUBCORES = 16
SC_WORKERS = SC_CORES * SC_SUBCORES
SC_ROWS = 128


def _rms(x, g):
    return x * lax.rsqrt(jnp.mean(x * x, axis=-1, keepdims=True) + EPS) * g


def _dot(a, b):
    return jnp.dot(a, b, preferred_element_type=F32)


def _dot_nt(a, b):
    return lax.dot_general(a, b, (((1,), (1,)), ((), ())), preferred_element_type=F32)


def _pack_planes(x, ref):
    xb = x.astype(BF16)
    lo = lax.bitcast_convert_type(xb[:, :HALF].astype(F32), U32) >> 16
    hi = lax.bitcast_convert_type(xb[:, HALF:].astype(F32), U32) & jnp.uint32(0xFFFF0000)
    packed = lax.bitcast_convert_type(lo | hi, I32)
    for p in range(HN_PLANES):
        ref[p] = packed[:, p * LANES:(p + 1) * LANES]


def _unpack_planes(planes):
    words = [lax.bitcast_convert_type(p, U32) for p in planes]
    los = [lax.bitcast_convert_type(w << 16, F32) for w in words]
    his = [lax.bitcast_convert_type(w & jnp.uint32(0xFFFF0000), F32) for w in words]
    return jnp.concatenate(los + his, axis=-1)


def _dot_tn(a, b):
    return lax.dot_general(a, b, (((0,), (0,)), ((), ())), preferred_element_type=F32)


def _mixer_kernel(x_ref, pool0_ref, st0_ref, cnt0_ref, tril_ref, sup_ref,
                  nmix_ref, wmain_ref, wz_ref, wgu_ref, bgate_ref, wpool_ref, pscale_ref,
                  gnorm_ref, wout_ref, nffn_ref, wr_ref, br_ref,
                  h_ref, hn_ref, route_ref, route_t_ref, pool_out_ref, st_out_ref, cnt_out_ref,
                  ext_ref, st_ref, kbd_ref, vbd_ref, sbd_ref, o_ref, cnt_ref,
                  xs_ref, proj_ref, z_ref, *, tb, nj, lead_pad, chained):
    s = pl.program_id(0)
    back = jnp.maximum(s - 1, 0)
    j = lax.rem(back, nj)
    n_chunks = tb // CHUNK
    wr_slot = lax.rem(s, 2)
    rd_slot = 1 - wr_slot

    @pl.when(s == 0)
    def _():
        kbd_ref[...] = jnp.zeros_like(kbd_ref)
        vbd_ref[...] = jnp.zeros_like(vbd_ref)
        sbd_ref[...] = jnp.zeros_like(sbd_ref)
        xs_ref[1] = jnp.zeros((tb, D_MODEL), F32)
        proj_ref[1] = jnp.zeros((tb, D_MAIN), F32)
        z_ref[1] = jnp.zeros((GATE_RANK, tb), F32)

    @pl.when(s <= 1)
    def _():
        cnt_ref[...] = cnt0_ref[...]

    def put_state(c, hh, st):
        gg, hp = divmod(hh, HEAD_GROUP)
        sbd_ref[c, gg, hp * GLA_DK:(hp + 1) * GLA_DK, hp * GLA_DV:(hp + 1) * GLA_DV] = st.astype(BF16)

    if chained:
        @pl.when(j == 0)
        def _():
            ext_ref[0:POOL_ROWS, :] = pool0_ref[0]
            st_ref[...] = st0_ref[0]

        for hh in range(GLA_HEADS):
            put_state(0, hh, st_ref[hh])
    else:
        for c in range(n_chunks):
            for hh in range(GLA_HEADS):
                put_state(c, hh, st0_ref[c, hh])

    x_new = x_ref[...]
    xn = _rms(x_new, nmix_ref[...]).astype(BF16)
    xs_ref[wr_slot] = x_new

    tiles_done = [0]

    def front_tiles(stage):
        for t in range(tiles_done[0], tiles_done[0] + FRONT_PLAN[stage]):
            cols = slice(t * FRONT_TILE, (t + 1) * FRONT_TILE)
            proj_ref[wr_slot, :, cols] = _dot(xn, wmain_ref[:, cols])
        tiles_done[0] += FRONT_PLAN[stage]

    front_tiles(0)
    z_ref[wr_slot] = _dot_nt(wz_ref[...], xn)

    x = xs_ref[rd_slot]
    z = z_ref[rd_slot]
    u = proj_ref[rd_slot, :, 0:D_POOL]
    q = proj_ref[rd_slot, :, D_POOL:D_POOL + D_QK]
    k = proj_ref[rd_slot, :, D_POOL + D_QK:D_POOL + 2 * D_QK]
    v = proj_ref[rd_slot, :, D_POOL + 2 * D_QK:D_POOL + 2 * D_QK + D_V]
    r = proj_ref[rd_slot, :, D_POOL + 2 * D_QK + D_V:D_MAIN]

    row = lax.broadcasted_iota(I32, (tb, 1), 0)

    pseg = POOL_ROWS + CHUNK
    if chained:
        ext_ref[POOL_ROWS:POOL_ROWS + tb, :] = u
        ext = ext_ref[...]
    else:
        ext = jnp.concatenate(
            [blk for c in range(n_chunks) for blk in (pool0_ref[c], u[c * CHUNK:(c + 1) * CHUNK])], axis=0)
    pooled = []
    for g, w in enumerate(POOL_WINDOWS):
        sl = slice(g * POOL_GROUP_DIM, (g + 1) * POOL_GROUP_DIM)
        acc = ext[:, sl]
        for d in range(g + 1):
            acc = acc + pltpu.roll(acc, 1 << d, axis=0)
        if chained:
            win = acc[POOL_ROWS:, :]
        else:
            win = jnp.concatenate([acc[c * pseg + POOL_ROWS:(c + 1) * pseg] for c in range(n_chunks)], axis=0)
        if lead_pad:
            cnt = jnp.clip(row - lead_pad + 1, 1, w).astype(F32)
            pooled.append(win / cnt - u[:, sl])
        else:
            pooled.append(win * (1.0 / w) - u[:, sl])
    pys = [_dot(jnp.concatenate(pooled[2 * i:2 * i + 2], axis=-1).astype(BF16), wpool_ref[i])
           for i in range(len(POOL_WINDOWS) // 2)]
    pool_y = jnp.concatenate(pys, axis=-1) * pscale_ref[...]
    if chained:
        ext_ref[0:POOL_ROWS, :] = ext_ref[tb:tb + POOL_ROWS, :]
    else:
        for c in range(n_chunks):
            pool_out_ref[c] = u[(c + 1) * CHUNK - POOL_ROWS:(c + 1) * CHUNK]

    gpre = _dot_tn(z.astype(BF16), wgu_ref[...]) + bgate_ref[...]
    log_a = (jnp.minimum(gpre, 0.0) - jnp.log(1.0 + jnp.exp(-jnp.abs(gpre)))) * (1.0 / GATE_TAU)
    if lead_pad:
        log_a = jnp.where(row >= lead_pad, log_a, 0.0)
    a_hi = log_a.astype(BF16)
    a_lo = (log_a - a_hi.astype(F32)).astype(BF16)
    tril = tril_ref[...]
    seg = tril.shape[0]
    bcum = jnp.concatenate(
        [_dot(tril, a_hi[r0:r0 + seg]) + _dot(tril, a_lo[r0:r0 + seg]) for r0 in range(0, tb, seg)], axis=0)
    front_tiles(1)
    eb = jnp.exp(bcum)
    qi = q * (GLA_DK ** -0.5) * eb
    ki = k * jnp.exp(-bcum)

    rr = lax.broadcasted_iota(I32, (CHUNK, HEAD_GROUP * CHUNK), 0)
    cc = lax.broadcasted_iota(I32, (CHUNK, HEAD_GROUP * CHUNK), 1)
    causal = (cc % CHUNK) <= rr

    lasts = [eb[(c + 1) * CHUNK - 1:(c + 1) * CHUNK, :] for c in range(n_chunks)]
    dcol = jnp.concatenate(lasts + [jnp.zeros((LANES - n_chunks, D_QK), F32)], axis=0).T

    n_grp = GLA_HEADS // HEAD_GROUP
    chunk_rows = [slice(c * CHUNK, (c + 1) * CHUNK) for c in range(n_chunks)]
    grp_k = [slice(g * HEAD_GROUP * GLA_DK, (g + 1) * HEAD_GROUP * GLA_DK) for g in range(n_grp)]
    grp_v = [slice(g * HEAD_GROUP * GLA_DV, (g + 1) * HEAD_GROUP * GLA_DV) for g in range(n_grp)]
    qi_b = qi.astype(BF16)
    ki_b = ki.astype(BF16)
    v_b = v.astype(BF16)

    scores = {}
    for c in range(n_chunks):
        for hh in range(GLA_HEADS):
            gg, hp = divmod(hh, HEAD_GROUP)
            kbd_ref[c, gg, hp * CHUNK:(hp + 1) * CHUNK, hp * GLA_DK:(hp + 1) * GLA_DK] = (
                ki_b[chunk_rows[c], hh * GLA_DK:(hh + 1) * GLA_DK])
            vbd_ref[c, gg, hp * CHUNK:(hp + 1) * CHUNK, hp * GLA_DV:(hp + 1) * GLA_DV] = (
                v_b[chunk_rows[c], hh * GLA_DV:(hh + 1) * GLA_DV])
        for gg in range(n_grp):
            scores[c, gg] = _dot_nt(qi_b[chunk_rows[c], grp_k[gg]], kbd_ref[c, gg])

    kvs = {}
    for c in range(n_chunks):
        for hh in range(GLA_HEADS):
            kvs[c, hh] = _dot_tn(ki_b[chunk_rows[c], hh * GLA_DK:(hh + 1) * GLA_DK],
                                 v_b[chunk_rows[c], hh * GLA_DV:(hh + 1) * GLA_DV])
    front_tiles(2)

    for hh in range(GLA_HEADS):
        st = st_ref[hh] if chained else None
        for c in range(n_chunks):
            s_old = st if chained else st0_ref[c, hh]
            s_new = (s_old + kvs[c, hh]) * dcol[hh * GLA_DK:(hh + 1) * GLA_DK, c:c + 1]
            if not chained:
                st_out_ref[c, hh] = s_new
            else:
                st = s_new
                if c + 1 < n_chunks:
                    put_state(c + 1, hh, s_new)
        if chained:
            st_ref[hh] = st

    for c in range(n_chunks):
        for gg in range(n_grp):
            p = jnp.where(causal, scores[c, gg], 0.0).astype(BF16)
            o_ref[chunk_rows[c], grp_v[gg]] = (
                _dot(p, vbd_ref[c, gg]) + _dot(qi_b[chunk_rows[c], grp_k[gg]], sbd_ref[c, gg]))

    o = o_ref[...]
    ons = []
    for hh in range(GLA_HEADS):
        oh = o[:, hh * GLA_DV:(hh + 1) * GLA_DV]
        ons.append(oh * lax.rsqrt(jnp.mean(oh * oh, axis=-1, keepdims=True) + EPS))
    og = jnp.concatenate(ons, axis=-1) * gnorm_ref[...] * (r * jax.nn.sigmoid(r))
    mix = _dot(jnp.concatenate([pool_y, og], axis=-1).astype(BF16), wout_ref[...])
    front_tiles(3)

    h = x + mix
    h_ref[...] = h
    hn = _rms(h, nffn_ref[...]).astype(BF16)
    _pack_planes(hn, hn_ref)

    logits = _dot_nt(wr_ref[...], hn) + br_ref[...]
    front_tiles(4)
    sub = lax.broadcasted_iota(I32, (EXPERTS_PER_GROUP, tb), 0).astype(F32)
    neg = jnp.float32(-jnp.inf)
    big = jnp.float32(EXPERTS_PER_GROUP)
    tile0 = logits[0:EXPERTS_PER_GROUP]
    is_g = sub < N_GROUPS
    gmax = jnp.max(jnp.where(is_g, tile0, neg), axis=0, keepdims=True)
    gsum = jnp.sum(jnp.where(is_g, jnp.exp(tile0 - gmax), 0.0), axis=0, keepdims=True)
    p_g = 1.0 / gsum
    gidx = jnp.min(jnp.where(is_g & (tile0 == gmax), sub, big), axis=0, keepdims=True)
    el = logits[N_GROUPS * EXPERTS_PER_GROUP:(N_GROUPS + 1) * EXPERTS_PER_GROUP]
    for g in range(N_GROUPS - 2, -1, -1):
        el = jnp.where(gidx == g, logits[(g + 1) * EXPERTS_PER_GROUP:(g + 2) * EXPERTS_PER_GROUP], el)
    m1 = jnp.max(el, axis=0, keepdims=True)
    i1 = jnp.min(jnp.where(el == m1, sub, big), axis=0, keepdims=True)
    rest = sub != i1
    m2 = jnp.max(jnp.where(rest, el, neg), axis=0, keepdims=True)
    i2 = jnp.min(jnp.where(rest & (el == m2), sub, big), axis=0, keepdims=True)
    t2 = jnp.exp(m2 - m1)
    den = 1.0 + t2
    g1 = p_g / den
    g2 = p_g * t2 / den
    e1 = gidx * EXPERTS_PER_GROUP + i1
    e2 = gidx * EXPERTS_PER_GROUP + i2

    eid = lax.broadcasted_iota(I32, (N_EXPERTS, tb), 0).astype(F32)
    oh1 = eid == e1
    oh2 = eid == e2
    both = jnp.where(oh1 | oh2, 1.0, 0.0)
    cnt = cnt_ref[...]
    before = _dot(both.astype(BF16), sup_ref[...]) + cnt
    front_tiles(5)
    assert tiles_done[0] * FRONT_TILE == D_MAIN
    pos1 = jnp.sum(jnp.where(oh1, before, 0.0), axis=0, keepdims=True)
    pos2 = jnp.sum(jnp.where(oh2, before, 0.0), axis=0, keepdims=True)
    cnt_new = cnt + jnp.sum(both, axis=1, keepdims=True)
    cnt_ref[...] = cnt_new
    cnt_out_ref[...] = cnt_new

    zero = jnp.zeros_like(e1)
    route_t = jnp.concatenate([e1, e2, g1, g2, pos1, pos2, zero, zero], axis=0)
    route_t_ref[0] = route_t
    route_ref[...] = jnp.concatenate([route_t, jnp.zeros((LANES - ROUTE_ROWS, tb), F32)], axis=0).T

    if chained:
        @pl.when(j == nj - 1)
        def _():
            pool_out_ref[0] = ext_ref[0:POOL_ROWS, :]
            st_out_ref[0] = st_ref[...]


def _mixer(x2d, pool0, st0, cnt0, weights, *, batch, seq, tb, lead_pad):
    chained = tb <= seq
    total_rows = batch * seq
    n_blk = total_rows // tb
    nj = seq // tb if chained else 1
    per_blk = 1 if chained else tb // seq
    assert chained or (seq == CHUNK and batch % per_blk == 0)
    shared = pool0.shape[0] == 1
    front = lambda s: jnp.minimum(s, n_blk - 1)
    back = lambda s: jnp.maximum(s - 1, 0)
    stream = lambda s: back(s) // nj
    st_idx = (lambda s: (0, 0, 0)) if shared else (lambda s: (stream(s), 0, 0))
    gla_idx = (lambda s: (0, 0, 0, 0)) if shared else (lambda s: (stream(s), 0, 0, 0))
    const2 = lambda s: (0, 0)
    tok_out = lambda s: (back(s), 0)

    seg = min(tb, MXU_DIM)
    ii = jnp.arange(seg)
    tril = ((ii[:, None] >= ii[None, :]) & (ii[:, None] // CHUNK == ii[None, :] // CHUNK)).astype(BF16)
    ii = jnp.arange(tb)
    sup = (ii[:, None] < ii[None, :]).astype(BF16)

    in_specs = [
        pl.BlockSpec((tb, D_MODEL), lambda s: (front(s), 0)),
        pl.BlockSpec((per_blk, POOL_ROWS, D_POOL), st_idx),
        pl.BlockSpec((per_blk, GLA_HEADS, GLA_DK, GLA_DV), gla_idx),
        pl.BlockSpec((N_EXPERTS, 1), const2),
        pl.BlockSpec((seg, seg), const2),
        pl.BlockSpec((tb, tb), const2),
    ]
    for wgt in weights:
        in_specs.append(pl.BlockSpec(wgt.shape, (lambda s, n=wgt.ndim: (0,) * n)))
    args = [x2d, pool0, st0, cnt0, tril, sup, *weights]

    out_shape = [
        jax.ShapeDtypeStruct((total_rows, D_MODEL), F32),
        jax.ShapeDtypeStruct((HN_PLANES, total_rows, LANES), I32),
        jax.ShapeDtypeStruct((total_rows, LANES), F32),
        jax.ShapeDtypeStruct((n_blk, ROUTE_ROWS, tb), F32),
        jax.ShapeDtypeStruct((batch, POOL_ROWS, D_POOL), F32),
        jax.ShapeDtypeStruct((batch, GLA_HEADS, GLA_DK, GLA_DV), F32),
        jax.ShapeDtypeStruct((N_EXPERTS, 1), F32),
    ]
    out_specs = [
        pl.BlockSpec((tb, D_MODEL), tok_out),
        pl.BlockSpec((HN_PLANES, tb, LANES), lambda s: (0, back(s), 0)),
        pl.BlockSpec((tb, LANES), tok_out),
        pl.BlockSpec((1, ROUTE_ROWS, tb), lambda s: (back(s), 0, 0)),
        pl.BlockSpec((per_blk, POOL_ROWS, D_POOL), lambda s: (stream(s), 0, 0)),
        pl.BlockSpec((per_blk, GLA_HEADS, GLA_DK, GLA_DV), lambda s: (stream(s), 0, 0, 0)),
        pl.BlockSpec((N_EXPERTS, 1), const2),
    ]
    n_grp = GLA_HEADS // HEAD_GROUP
    scratch = [
        pltpu.VMEM((POOL_ROWS + tb, D_POOL), F32),
        pltpu.VMEM((GLA_HEADS, GLA_DK, GLA_DV), F32),
        pltpu.VMEM((tb // CHUNK, n_grp, HEAD_GROUP * CHUNK, HEAD_GROUP * GLA_DK), BF16),
        pltpu.VMEM((tb // CHUNK, n_grp, HEAD_GROUP * CHUNK, HEAD_GROUP * GLA_DV), BF16),
        pltpu.VMEM((tb // CHUNK, n_grp, HEAD_GROUP * GLA_DK, HEAD_GROUP * GLA_DV), BF16),
        pltpu.VMEM((tb, D_V), F32),
        pltpu.VMEM((N_EXPERTS, 1), F32),
        pltpu.VMEM((2, tb, D_MODEL), F32),
        pltpu.VMEM((2, tb, D_MAIN), F32),
        pltpu.VMEM((2, GATE_RANK, tb), F32),
    ]
    return pl.pallas_call(
        functools.partial(_mixer_kernel, tb=tb, nj=nj, lead_pad=lead_pad, chained=chained),
        grid=(n_blk + 1,),
        in_specs=in_specs,
        out_specs=out_specs,
        out_shape=out_shape,
        scratch_shapes=scratch,
        compiler_params=pltpu.CompilerParams(
            dimension_semantics=("arbitrary",), vmem_limit_bytes=VMEM_LIMIT),
        name=f"mixer_tb{tb}_pad{lead_pad}",
    )(*args)


def _sc_mesh():
    return plsc.VectorSubcoreMesh(core_axis_name="c", subcore_axis_name="s",
                                  num_cores=SC_CORES, num_subcores=SC_SUBCORES)


def _sc_worker():
    return lax.axis_index("s") * SC_CORES + lax.axis_index("c")


def _plane_rows(dest, planes, rows_per_plane):
    offs = (jnp.arange(planes, dtype=I32) * rows_per_plane)[None, :, None]
    return dest[:, None, :] + offs


def _dispatch(hn_p, hn_s, dest0, dest1, n_rows):
    planes, t_p, _ = hn_p.shape
    t_s = hn_s.shape[1]
    n_cp = t_p // SC_ROWS
    cp = n_cp // SC_WORKERS
    n_cs = t_s // SC_ROWS
    assert t_p == cp * SC_ROWS * SC_WORKERS and t_s == n_cs * SC_ROWS and n_cs <= SC_WORKERS
    idx0 = _plane_rows(dest0, planes, n_rows)
    idx1 = _plane_rows(dest1, planes, n_rows)

    def body(hnp_hbm, hns_hbm, d0_hbm, d1_hbm, xs_hbm, rows_v, i0_v, i1_v, is0_v, is1_v, sem_in, sem_out):
        wid = _sc_worker()
        pltpu.sync_copy(d0_hbm.at[pl.ds(wid * cp, cp)], i0_v)
        pltpu.sync_copy(d1_hbm.at[pl.ds(wid * cp, cp)], i1_v)

        def move(src_hbm, src_rows, row0, i0, i1, c):
            loads = [pltpu.async_copy(src_hbm.at[pl.ds(p * src_rows + row0, SC_ROWS)], rows_v.at[p], sem_in)
                     for p in range(planes)]
            for cpy in loads:
                cpy.wait()
            stores = []
            for p in range(planes):
                stores.append(pltpu.async_copy(rows_v.at[p], xs_hbm.at[i0.at[c, p]], sem_out))
                stores.append(pltpu.async_copy(rows_v.at[p], xs_hbm.at[i1.at[c, p]], sem_out))
            for cpy in stores:
                cpy.wait()

        @pl.loop(0, cp)
        def _(c):
            move(hnp_hbm, t_p, (wid * cp + c) * SC_ROWS, i0_v, i1_v, c)

        @pl.when(wid < n_cs)
        def _():
            pltpu.sync_copy(d0_hbm.at[pl.ds(n_cp + wid, 1)], is0_v)
            pltpu.sync_copy(d1_hbm.at[pl.ds(n_cp + wid, 1)], is1_v)
            move(hns_hbm, t_s, wid * SC_ROWS, is0_v, is1_v, 0)

    xs = pl.kernel(
        body,
        out_type=jax.ShapeDtypeStruct((planes * n_rows, LANES), I32),
        mesh=_sc_mesh(),
        scratch_types=[
            pltpu.VMEM((planes, SC_ROWS, LANES), I32),
            pltpu.VMEM((cp, planes, SC_ROWS), I32),
            pltpu.VMEM((cp, planes, SC_ROWS), I32),
            pltpu.VMEM((1, planes, SC_ROWS), I32),
            pltpu.VMEM((1, planes, SC_ROWS), I32),
            pltpu.SemaphoreType.DMA,
            pltpu.SemaphoreType.DMA,
        ],
        name="moe_dispatch_sc",
    )(hn_p.reshape(planes * t_p, LANES), hn_s.reshape(planes * t_s, LANES), idx0, idx1)
    return xs.reshape(planes, n_rows, LANES)


def _gather(ys, dest0, dest1):
    planes, n_rows, _ = ys.shape
    n_chunks = dest0.shape[0]
    n_tok = n_chunks * SC_ROWS
    cpw = max(n_chunks // SC_WORKERS, 1)
    assert n_chunks <= SC_WORKERS or n_chunks == cpw * SC_WORKERS
    idx = (_plane_rows(dest0, planes, n_rows), _plane_rows(dest1, planes, n_rows))

    def body(ys_hbm, d0_hbm, d1_hbm, out_hbm, rows_v, i0_v, i1_v, sem_in, sem_out):
        wid = _sc_worker()

        def work():
            pltpu.sync_copy(d0_hbm.at[pl.ds(wid * cpw, cpw)], i0_v)
            pltpu.sync_copy(d1_hbm.at[pl.ds(wid * cpw, cpw)], i1_v)

            @pl.loop(0, cpw)
            def _(c):
                row0 = (wid * cpw + c) * SC_ROWS
                for j, i_v in enumerate((i0_v, i1_v)):
                    loads = [pltpu.async_copy(ys_hbm.at[i_v.at[c, p]], rows_v.at[p], sem_in)
                             for p in range(planes)]
                    for cpy in loads:
                        cpy.wait()
                    stores = [
                        pltpu.async_copy(
                            rows_v.at[p], out_hbm.at[pl.ds((j * planes + p) * n_tok + row0, SC_ROWS)], sem_out)
                        for p in range(planes)]
                    for cpy in stores:
                        cpy.wait()

        if n_chunks < SC_WORKERS:
            pl.when(wid < n_chunks)(work)
        else:
            work()

    out = pl.kernel(
        body,
        out_type=jax.ShapeDtypeStruct((2 * planes * n_tok, LANES), ys.dtype),
        mesh=_sc_mesh(),
        scratch_types=[
            pltpu.VMEM((planes, SC_ROWS, LANES), ys.dtype),
            pltpu.VMEM((cpw, planes, SC_ROWS), I32),
            pltpu.VMEM((cpw, planes, SC_ROWS), I32),
            pltpu.SemaphoreType.DMA,
            pltpu.SemaphoreType.DMA,
        ],
        name="moe_gather_sc",
    )(ys.reshape(planes * n_rows, LANES), *idx)
    return out.reshape(2, planes, n_tok, LANES)


def _expert_kernel(be_ref, bv_ref, nb_ref, xs_ref, wg_ref, wu_ref, wd_ref, ys_ref, wgu_s, wd_s):
    i = pl.program_id(0)
    live = i < nb_ref[0]
    prev = be_ref[jnp.maximum(i - 1, 0)]
    fresh = (i == 0) | (be_ref[i] != prev)

    @pl.when(live & fresh)
    def _():
        wgu_s[:, 0:D_EXPERT] = wg_ref[0].astype(BF16)
        wgu_s[:, D_EXPERT:2 * D_EXPERT] = wu_ref[0].astype(BF16)
        wd_s[...] = wd_ref[0].astype(BF16)

    @pl.when(live)
    def _():
        valid = lax.broadcasted_iota(I32, (MOE_BLK, LANES), 0) < bv_ref[i]
        xb = _unpack_planes([jnp.where(valid, xs_ref[p], 0) for p in range(HN_PLANES)]).astype(BF16)
        gu = _dot(xb, wgu_s[...])
        gate = gu[:, 0:D_EXPERT]
        hmid = gate * jax.nn.sigmoid(gate) * gu[:, D_EXPERT:]
        _pack_planes(_dot(hmid.astype(BF16), wd_s[...]), ys_ref)

    @pl.when(jnp.logical_not(live))
    def _():
        ys_ref[...] = jnp.zeros_like(ys_ref)


def _experts(block_e, block_valid, nb, xs, w_eg, w_eu, w_ed):
    n_rows = xs.shape[1]
    n_blocks = n_rows // MOE_BLK
    row_idx = lambda i, be, bv, nb: (0, jnp.minimum(i, nb[0] - 1), 0)
    w_idx = lambda i, be, bv, nb: (be[i], 0, 0)
    grid_spec = pltpu.PrefetchScalarGridSpec(
        num_scalar_prefetch=3,
        grid=(n_blocks,),
        in_specs=[
            pl.BlockSpec((HN_PLANES, MOE_BLK, LANES), row_idx),
            pl.BlockSpec((1, D_MODEL, D_EXPERT), w_idx),
            pl.BlockSpec((1, D_MODEL, D_EXPERT), w_idx),
            pl.BlockSpec((1, D_EXPERT, D_MODEL), w_idx),
        ],
        out_specs=pl.BlockSpec((HN_PLANES, MOE_BLK, LANES), lambda i, be, bv, nb: (0, i, 0)),
        scratch_shapes=[
            pltpu.VMEM((D_MODEL, 2 * D_EXPERT), BF16),
            pltpu.VMEM((D_EXPERT, D_MODEL), BF16),
        ],
    )
    return pl.pallas_call(
        _expert_kernel,
        grid_spec=grid_spec,
        out_shape=jax.ShapeDtypeStruct((HN_PLANES, n_rows, LANES), I32),
        compiler_params=pltpu.CompilerParams(
            dimension_semantics=("arbitrary",), vmem_limit_bytes=VMEM_LIMIT),
        name="moe_experts",
    )(block_e, block_valid, nb, xs, w_eg, w_eu, w_ed)


def _combine_kernel(h_ref, route_ref, nfin_ref, rows_ref, *rest):
    y_ref = rest[-1]
    route = route_ref[...]
    ys1 = _unpack_planes([rows_ref[0, p] for p in range(HN_PLANES)])
    ys2 = _unpack_planes([rows_ref[1, p] for p in range(HN_PLANES)])
    out = h_ref[...] + (ys1 * route[:, 2:3] + ys2 * route[:, 3:4])
    y_ref[...] = _rms(out, nfin_ref[...])


def _combine(h, route, norm_final, rows, *, row0=0, y_prev=None):
    total = h.shape[0]
    tb = COMBINE_TB
    blk0 = row0 // tb
    args = [h, route, norm_final, rows]
    in_specs = [
        pl.BlockSpec((tb, D_MODEL), lambda i: (blk0 + i, 0)),
        pl.BlockSpec((tb, LANES), lambda i: (blk0 + i, 0)),
        pl.BlockSpec((1, D_MODEL), lambda i: (0, 0)),
        pl.BlockSpec((2, HN_PLANES, tb, LANES), lambda i: (0, 0, i, 0)),
    ]
    aliases = {}
    if y_prev is not None:
        aliases[len(args)] = 0
        args.append(y_prev)
        in_specs.append(pl.BlockSpec(memory_space=pl.ANY))
    return pl.pallas_call(
        _combine_kernel,
        grid=(rows.shape[2] // tb,),
        in_specs=in_specs,
        out_specs=pl.BlockSpec((tb, D_MODEL), lambda i: (blk0 + i, 0)),
        out_shape=jax.ShapeDtypeStruct((total, D_MODEL), F32),
        input_output_aliases=aliases,
        compiler_params=pltpu.CompilerParams(
            dimension_semantics=("arbitrary",), vmem_limit_bytes=VMEM_LIMIT),
        name="moe_combine",
    )(*args)


def kernel(x_prompt, x_sample, state_pool, state_gla, meta_tokens, norm_mix, w_in, w_gate_up, b_gate, w_pool, pool_scale, gla_norm, w_out, norm_ffn, w_router_group, b_router_group, w_router_expert, b_router_expert, w_expert_gate, w_expert_up, w_expert_down, norm_final):
    assert w_in.shape[0] == 1, "one encoder layer"
    batch, seq, _ = x_prompt.shape
    dec_batch, dec_seq, _ = x_sample.shape
    assert seq % MIX_TB == 0 and dec_seq == CHUNK and N_META <= CHUNK
    t_prompt = batch * seq
    t_sample = dec_batch * dec_seq
    t_all = t_prompt + t_sample
    assert t_prompt % COMBINE_TB == 0 and t_sample % COMBINE_TB == 0

    w_in0 = w_in[0]
    gpad = EXPERTS_PER_GROUP - N_GROUPS
    rpad = ROUTER_ROWS - EXPERTS_PER_GROUP - N_EXPERTS
    w_router = jnp.concatenate([
        w_router_group[0].T, jnp.zeros((gpad, D_MODEL), F32),
        w_router_expert[0].T, jnp.zeros((rpad, D_MODEL), F32)], axis=0)
    b_router = jnp.concatenate([
        b_router_group[0], jnp.zeros((gpad,), F32), b_router_expert[0], jnp.zeros((rpad,), F32)])
    zg = jnp.zeros((POOL_GROUP_DIM, POOL_GROUP_DIM), F32)
    w_pool_pairs = jnp.stack([
        jnp.block([[w_pool[0, 2 * i], zg], [zg, w_pool[0, 2 * i + 1]]]) for i in range(len(POOL_WINDOWS) // 2)])
    weights = (
        norm_mix[0][None, :],
        w_in0[:, :D_MAIN].astype(BF16),
        w_in0[:, D_MAIN:].T.astype(BF16),
        w_gate_up[0].astype(BF16),
        b_gate[0][None, :],
        w_pool_pairs.astype(BF16),
        pool_scale[0][None, :],
        gla_norm[0][None, :],
        w_out[0].astype(BF16),
        norm_ffn[0][None, :],
        w_router.astype(BF16),
        b_router[:, None],
    )

    zero_cnt = jnp.zeros((N_EXPERTS, 1), F32)
    x_meta = jnp.pad(meta_tokens.astype(F32), ((CHUNK - N_META, 0), (0, 0)))
    meta = _mixer(x_meta, jnp.zeros((1, POOL_ROWS, D_POOL), F32),
                  jnp.zeros((1, GLA_HEADS, GLA_DK, GLA_DV), F32), zero_cnt,
                  weights, batch=1, seq=CHUNK, tb=CHUNK, lead_pad=CHUNK - N_META)
    h_m, hn_m, route_m, rt_m, pool_m, st_m, cnt_m = meta
    del h_m, hn_m, route_m, rt_m, cnt_m
    h_p, hn_p, route_p, rt_p, pool_p, st_p, cnt_p = _mixer(
        x_prompt.reshape(t_prompt, D_MODEL), pool_m, st_m, zero_cnt, weights,
        batch=batch, seq=seq, tb=MIX_TB, lead_pad=0)
    pool_s0 = jnp.pad(state_pool[0], ((0, 0), (POOL_ROWS - POOL_PAD, 0), (0, 0)))
    h_s, hn_s, route_s, rt_s, pool_s, st_s, cnt_s = _mixer(
        x_sample.reshape(t_sample, D_MODEL), pool_s0, state_gla[0].astype(F32),
        cnt_p, weights, batch=dec_batch, seq=dec_seq, tb=SAMPLE_TB, lead_pad=0)

    counts = cnt_s[:, 0].astype(I32)
    padded = (counts + MOE_BLK - 1) // MOE_BLK * MOE_BLK
    ends = jnp.cumsum(padded)
    pstart = ends - padded
    n_blocks = (2 * t_all + N_EXPERTS * (MOE_BLK - 1) + MOE_BLK - 1) // MOE_BLK
    nb = (ends[-1] // MOE_BLK).astype(I32)
    blk_ids = jnp.minimum(jnp.arange(n_blocks, dtype=I32), nb - 1)
    block_e = jnp.sum((ends[None, :] <= (blk_ids * MOE_BLK)[:, None]).astype(I32), axis=1)
    block_e = jnp.minimum(block_e, N_EXPERTS - 1)
    owner = block_e[:, None] == jnp.arange(N_EXPERTS, dtype=I32)
    row_end = jnp.sum(jnp.where(owner, pstart + counts, 0), axis=1)
    block_valid = jnp.clip(row_end - blk_ids * MOE_BLK, 0, MOE_BLK)

    def dest_rows(rt):
        rt = rt.transpose(1, 0, 2).reshape(ROUTE_ROWS, -1)
        onehot = rt[0:2].astype(I32)[..., None] == jnp.arange(N_EXPERTS, dtype=I32)
        return jnp.sum(jnp.where(onehot, pstart, 0), axis=-1) + rt[4:6].astype(I32)

    dest = jnp.concatenate([dest_rows(rt_p), dest_rows(rt_s)], axis=1)
    dest0 = dest[0].reshape(t_all // SC_ROWS, SC_ROWS)
    dest1 = dest[1].reshape(t_all // SC_ROWS, SC_ROWS)
    xs = _dispatch(hn_p, hn_s, dest0, dest1, n_blocks * MOE_BLK)
    ys = _experts(block_e, block_valid.astype(I32), nb[None], xs,
                  w_expert_gate[0], w_expert_up[0], w_expert_down[0])
    nfin = norm_final[None, :]
    cp_chunks = t_prompt // SC_ROWS
    part = cp_chunks // COMBINE_PARTS
    y_prompt = None
    for i in range(COMBINE_PARTS):
        ch = slice(i * part, (i + 1) * part)
        rows_i = _gather(ys, dest0[ch], dest1[ch])
        y_prompt = _combine(h_p, route_p, nfin, rows_i, row0=i * part * SC_ROWS, y_prev=y_prompt)
    y_prompt = y_prompt.reshape(batch, seq, D_MODEL)
    rows_s = _gather(ys, dest0[cp_chunks:], dest1[cp_chunks:])
    y_sample = _combine(h_s, route_s, nfin, rows_s).reshape(dec_batch, dec_seq, D_MODEL)
    new_pool_prompt = pool_p[:, POOL_ROWS - POOL_PAD:][None]
    new_gla_prompt = st_p[None]
    new_pool_sample = pool_s[:, POOL_ROWS - POOL_PAD:][None]
    new_gla_sample = st_s[None]
    return (y_prompt, y_sample, new_pool_prompt, new_gla_prompt, new_pool_sample, new_gla_sample)
```

```python
import functools

import jax
import jax.numpy as jnp
from jax import lax
from jax.experimental import pallas as pl
from jax.experimental.pallas import tpu as pltpu
from jax.experimental.pallas import tpu_sc as plsc

F32 = jnp.float32
BF16 = jnp.bfloat16
U32 = jnp.uint32
I32 = jnp.int32

D_MODEL = 1024
N_META = 16
CHUNK = 64
EPS = 1e-6
D_POOL = 512
POOL_WINDOWS = (2, 4, 8, 16)
POOL_GROUP_DIM = 128
POOL_PAD = 15
POOL_ROWS = 16
GLA_HEADS = 4
GLA_DK = 64
GLA_DV = 128
D_QK = 256
D_V = 512
HEAD_GROUP = 2
GATE_RANK = 16
GATE_TAU = 16.0
D_MAIN = D_POOL + 2 * D_QK + 2 * D_V
N_GROUPS = 4
EXPERTS_PER_GROUP = 8
N_EXPERTS = 32
D_EXPERT = 512

LANES = 128
MXU_DIM = 256
HALF = D_MODEL // 2
HN_PLANES = HALF // LANES
ROUTE_ROWS = 8
ROUTER_ROWS = 64
MIX_TB = 512
SAMPLE_TB = 256
FRONT_TILE = 256
FRONT_PLAN = (3, 0, 1, 2, 2, 0)
MOE_BLK = 768
COMBINE_TB = 1024
COMBINE_PARTS = 4
VMEM_LIMIT = 48 * 1024 * 1024
SC_CORES = 2
SC_SUBCORES = 16
SC_WORKERS = SC_CORES * SC_SUBCORES
SC_ROWS = 128


def _rms(x, g):
    return x * lax.rsqrt(jnp.mean(x * x, axis=-1, keepdims=True) + EPS) * g


def _dot(a, b):
    return jnp.dot(a, b, preferred_element_type=F32)


def _dot_nt(a, b):
    return lax.dot_general(a, b, (((1,), (1,)), ((), ())), preferred_element_type=F32)


def _pack_planes(x, ref):
    xb = x.astype(BF16)
    lo = lax.bitcast_convert_type(xb[:, :HALF].astype(F32), U32) >> 16
    hi = lax.bitcast_convert_type(xb[:, HALF:].astype(F32), U32) & jnp.uint32(0xFFFF0000)
    packed = lax.bitcast_convert_type(lo | hi, I32)
    for p in range(HN_PLANES):
        ref[p] = packed[:, p * LANES:(p + 1) * LANES]


def _unpack_planes(planes):
    words = [lax.bitcast_convert_type(p, U32) for p in planes]
    los = [lax.bitcast_convert_type(w << 16, F32) for w in words]
    his = [lax.bitcast_convert_type(w & jnp.uint32(0xFFFF0000), F32) for w in words]
    return jnp.concatenate(los + his, axis=-1)


def _dot_tn(a, b):
    return lax.dot_general(a, b, (((0,), (0,)), ((), ())), preferred_element_type=F32)


def _mixer_kernel(x_ref, pool0_ref, st0_ref, cnt0_ref, tril_ref, sup_ref,
                  nmix_ref, wmain_ref, wz_ref, wgu_ref, bgate_ref, wpool_ref, pscale_ref,
                  gnorm_ref, wout_ref, nffn_ref, wr_ref, br_ref,
                  h_ref, hn_ref, route_ref, route_t_ref, pool_out_ref, st_out_ref, cnt_out_ref,
                  ext_ref, st_ref, kbd_ref, vbd_ref, sbd_ref, o_ref, cnt_ref,
                  xs_ref, proj_ref, z_ref, *, tb, nj, lead_pad, chained):
    s = pl.program_id(0)
    back = jnp.maximum(s - 1, 0)
    j = lax.rem(back, nj)
    n_chunks = tb // CHUNK
    wr_slot = lax.rem(s, 2)
    rd_slot = 1 - wr_slot

    @pl.when(s == 0)
    def _():
        kbd_ref[...] = jnp.zeros_like(kbd_ref)
        vbd_ref[...] = jnp.zeros_like(vbd_ref)
        sbd_ref[...] = jnp.zeros_like(sbd_ref)
        xs_ref[1] = jnp.zeros((tb, D_MODEL), F32)
        proj_ref[1] = jnp.zeros((tb, D_MAIN), F32)
        z_ref[1] = jnp.zeros((GATE_RANK, tb), F32)

    @pl.when(s <= 1)
    def _():
        cnt_ref[...] = cnt0_ref[...]

    def put_state(c, hh, st):
        gg, hp = divmod(hh, HEAD_GROUP)
        sbd_ref[c, gg, hp * GLA_DK:(hp + 1) * GLA_DK, hp * GLA_DV:(hp + 1) * GLA_DV] = st.astype(BF16)

    if chained:
        @pl.when(j == 0)
        def _():
            ext_ref[0:POOL_ROWS, :] = pool0_ref[0]
            st_ref[...] = st0_ref[0]

        for hh in range(GLA_HEADS):
            put_state(0, hh, st_ref[hh])
    else:
        for c in range(n_chunks):
            for hh in range(GLA_HEADS):
                put_state(c, hh, st0_ref[c, hh])

    x_new = x_ref[...]
    xn = _rms(x_new, nmix_ref[...]).astype(BF16)
    xs_ref[wr_slot] = x_new

    tiles_done = [0]

    def front_tiles(stage):
        for t in range(tiles_done[0], tiles_done[0] + FRONT_PLAN[stage]):
            cols = slice(t * FRONT_TILE, (t + 1) * FRONT_TILE)
            proj_ref[wr_slot, :, cols] = _dot(xn, wmain_ref[:, cols])
        tiles_done[0] += FRONT_PLAN[stage]

    front_tiles(0)
    z_ref[wr_slot] = _dot_nt(wz_ref[...], xn)

    x = xs_ref[rd_slot]
    z = z_ref[rd_slot]
    u = proj_ref[rd_slot, :, 0:D_POOL]
    q = proj_ref[rd_slot, :, D_POOL:D_POOL + D_QK]
    k = proj_ref[rd_slot, :, D_POOL + D_QK:D_POOL + 2 * D_QK]
    v = proj_ref[rd_slot, :, D_POOL + 2 * D_QK:D_POOL + 2 * D_QK + D_V]
    r = proj_ref[rd_slot, :, D_POOL + 2 * D_QK + D_V:D_MAIN]

    row = lax.broadcasted_iota(I32, (tb, 1), 0)

    pseg = POOL_ROWS + CHUNK
    if chained:
        ext_ref[POOL_ROWS:POOL_ROWS + tb, :] = u
        ext = ext_ref[...]
    else:
        ext = jnp.concatenate(
            [blk for c in range(n_chunks) for blk in (pool0_ref[c], u[c * CHUNK:(c + 1) * CHUNK])], axis=0)
    pooled = []
    for g, w in enumerate(POOL_WINDOWS):
        sl = slice(g * POOL_GROUP_DIM, (g + 1) * POOL_GROUP_DIM)
        acc = ext[:, sl]
        for d in range(g + 1):
            acc = acc + pltpu.roll(acc, 1 << d, axis=0)
        if chained:
            win = acc[POOL_ROWS:, :]
        else:
            win = jnp.concatenate([acc[c * pseg + POOL_ROWS:(c + 1) * pseg] for c in range(n_chunks)], axis=0)
        if lead_pad:
            cnt = jnp.clip(row - lead_pad + 1, 1, w).astype(F32)
            pooled.append(win / cnt - u[:, sl])
        else:
            pooled.append(win * (1.0 / w) - u[:, sl])
    pys = [_dot(jnp.concatenate(pooled[2 * i:2 * i + 2], axis=-1).astype(BF16), wpool_ref[i])
           for i in range(len(POOL_WINDOWS) // 2)]
    pool_y = jnp.concatenate(pys, axis=-1) * pscale_ref[...]
    if chained:
        ext_ref[0:POOL_ROWS, :] = ext_ref[tb:tb + POOL_ROWS, :]
    else:
        for c in range(n_chunks):
            pool_out_ref[c] = u[(c + 1) * CHUNK - POOL_ROWS:(c + 1) * CHUNK]

    gpre = _dot_tn(z.astype(BF16), wgu_ref[...]) + bgate_ref[...]
    log_a = (jnp.minimum(gpre, 0.0) - jnp.log(1.0 + jnp.exp(-jnp.abs(gpre)))) * (1.0 / GATE_TAU)
    if lead_pad:
        log_a = jnp.where(row >= lead_pad, log_a, 0.0)
    a_hi = log_a.astype(BF16)
    a_lo = (log_a - a_hi.astype(F32)).astype(BF16)
    tril = tril_ref[...]
    seg = tril.shape[0]
    bcum = jnp.concatenate(
        [_dot(tril, a_hi[r0:r0 + seg]) + _dot(tril, a_lo[r0:r0 + seg]) for r0 in range(0, tb, seg)], axis=0)
    front_tiles(1)
    eb = jnp.exp(bcum)
    qi = q * (GLA_DK ** -0.5) * eb
    ki = k * jnp.exp(-bcum)

    rr = lax.broadcasted_iota(I32, (CHUNK, HEAD_GROUP * CHUNK), 0)
    cc = lax.broadcasted_iota(I32, (CHUNK, HEAD_GROUP * CHUNK), 1)
    causal = (cc % CHUNK) <= rr

    lasts = [eb[(c + 1) * CHUNK - 1:(c + 1) * CHUNK, :] for c in range(n_chunks)]
    dcol = jnp.concatenate(lasts + [jnp.zeros((LANES - n_chunks, D_QK), F32)], axis=0).T

    n_grp = GLA_HEADS // HEAD_GROUP
    chunk_rows = [slice(c * CHUNK, (c + 1) * CHUNK) for c in range(n_chunks)]
    grp_k = [slice(g * HEAD_GROUP * GLA_DK, (g + 1) * HEAD_GROUP * GLA_DK) for g in range(n_grp)]
    grp_v = [slice(g * HEAD_GROUP * GLA_DV, (g + 1) * HEAD_GROUP * GLA_DV) for g in range(n_grp)]
    qi_b = qi.astype(BF16)
    ki_b = ki.astype(BF16)
    v_b = v.astype(BF16)

    scores = {}
    for c in range(n_chunks):
        for hh in range(GLA_HEADS):
            gg, hp = divmod(hh, HEAD_GROUP)
            kbd_ref[c, gg, hp * CHUNK:(hp + 1) * CHUNK, hp * GLA_DK:(hp + 1) * GLA_DK] = (
                ki_b[chunk_rows[c], hh * GLA_DK:(hh + 1) * GLA_DK])
            vbd_ref[c, gg, hp * CHUNK:(hp + 1) * CHUNK, hp * GLA_DV:(hp + 1) * GLA_DV] = (
                v_b[chunk_rows[c], hh * GLA_DV:(hh + 1) * GLA_DV])
        for gg in range(n_grp):
            scores[c, gg] = _dot_nt(qi_b[chunk_rows[c], grp_k[gg]], kbd_ref[c, gg])

    kvs = {}
    for c in range(n_chunks):
        for hh in range(GLA_HEADS):
            kvs[c, hh] = _dot_tn(ki_b[chunk_rows[c], hh * GLA_DK:(hh + 1) * GLA_DK],
                                 v_b[chunk_rows[c], hh * GLA_DV:(hh + 1) * GLA_DV])
    front_tiles(2)

    for hh in range(GLA_HEADS):
        st = st_ref[hh] if chained else None
        for c in range(n_chunks):
            s_old = st if chained else st0_ref[c, hh]
            s_new = (s_old + kvs[c, hh]) * dcol[hh * GLA_DK:(hh + 1) * GLA_DK, c:c + 1]
            if not chained:
                st_out_ref[c, hh] = s_new
            else:
                st = s_new
                if c + 1 < n_chunks:
                    put_state(c + 1, hh, s_new)
        if chained:
            st_ref[hh] = st

    for c in range(n_chunks):
        for gg in range(n_grp):
            p = jnp.where(causal, scores[c, gg], 0.0).astype(BF16)
            o_ref[chunk_rows[c], grp_v[gg]] = (
                _dot(p, vbd_ref[c, gg]) + _dot(qi_b[chunk_rows[c], grp_k[gg]], sbd_ref[c, gg]))

    o = o_ref[...]
    ons = []
    for hh in range(GLA_HEADS):
        oh = o[:, hh * GLA_DV:(hh + 1) * GLA_DV]
        ons.append(oh * lax.rsqrt(jnp.mean(oh * oh, axis=-1, keepdims=True) + EPS))
    og = jnp.concatenate(ons, axis=-1) * gnorm_ref[...] * (r * jax.nn.sigmoid(r))
    mix = _dot(jnp.concatenate([pool_y, og], axis=-1).astype(BF16), wout_ref[...])
    front_tiles(3)

    h = x + mix
    h_ref[...] = h
    hn = _rms(h, nffn_ref[...]).astype(BF16)
    _pack_planes(hn, hn_ref)

    logits = _dot_nt(wr_ref[...], hn) + br_ref[...]
    front_tiles(4)
    sub = lax.broadcasted_iota(I32, (EXPERTS_PER_GROUP, tb), 0).astype(F32)
    neg = jnp.float32(-jnp.inf)
    big = jnp.float32(EXPERTS_PER_GROUP)
    tile0 = logits[0:EXPERTS_PER_GROUP]
    is_g = sub < N_GROUPS
    gmax = jnp.max(jnp.where(is_g, tile0, neg), axis=0, keepdims=True)
    gsum = jnp.sum(jnp.where(is_g, jnp.exp(tile0 - gmax), 0.0), axis=0, keepdims=True)
    p_g = 1.0 / gsum
    gidx = jnp.min(jnp.where(is_g & (tile0 == gmax), sub, big), axis=0, keepdims=True)
    el = logits[N_GROUPS * EXPERTS_PER_GROUP:(N_GROUPS + 1) * EXPERTS_PER_GROUP]
    for g in range(N_GROUPS - 2, -1, -1):
        el = jnp.where(gidx == g, logits[(g + 1) * EXPERTS_PER_GROUP:(g + 2) * EXPERTS_PER_GROUP], el)
    m1 = jnp.max(el, axis=0, keepdims=True)
    i1 = jnp.min(jnp.where(el == m1, sub, big), axis=0, keepdims=True)
    rest = sub != i1
    m2 = jnp.max(jnp.where(rest, el, neg), axis=0, keepdims=True)
    i2 = jnp.min(jnp.where(rest & (el == m2), sub, big), axis=0, keepdims=True)
    t2 = jnp.exp(m2 - m1)
    den = 1.0 + t2
    g1 = p_g / den
    g2 = p_g * t2 / den
    e1 = gidx * EXPERTS_PER_GROUP + i1
    e2 = gidx * EXPERTS_PER_GROUP + i2

    eid = lax.broadcasted_iota(I32, (N_EXPERTS, tb), 0).astype(F32)
    oh1 = eid == e1
    oh2 = eid == e2
    both = jnp.where(oh1 | oh2, 1.0, 0.0)
    cnt = cnt_ref[...]
    before = _dot(both.astype(BF16), sup_ref[...]) + cnt
    front_tiles(5)
    assert tiles_done[0] * FRONT_TILE == D_MAIN
    pos1 = jnp.sum(jnp.where(oh1, before, 0.0), axis=0, keepdims=True)
    pos2 = jnp.sum(jnp.where(oh2, before, 0.0), axis=0, keepdims=True)
    cnt_new = cnt + jnp.sum(both, axis=1, keepdims=True)
    cnt_ref[...] = cnt_new
    cnt_out_ref[...] = cnt_new

    zero = jnp.zeros_like(e1)
    route_t = jnp.concatenate([e1, e2, g1, g2, pos1, pos2, zero, zero], axis=0)
    route_t_ref[0] = route_t
    route_ref[...] = jnp.concatenate([route_t, jnp.zeros((LANES - ROUTE_ROWS, tb), F32)], axis=0).T

    if chained:
        @pl.when(j == nj - 1)
        def _():
            pool_out_ref[0] = ext_ref[0:POOL_ROWS, :]
            st_out_ref[0] = st_ref[...]


def _mixer(x2d, pool0, st0, cnt0, weights, *, batch, seq, tb, lead_pad):
    chained = tb <= seq
    total_rows = batch * seq
    n_blk = total_rows // tb
    nj = seq // tb if chained else 1
    per_blk = 1 if chained else tb // seq
    assert chained or (seq == CHUNK and batch % per_blk == 0)
    shared = pool0.shape[0] == 1
    front = lambda s: jnp.minimum(s, n_blk - 1)
    back = lambda s: jnp.maximum(s - 1, 0)
    stream = lambda s: back(s) // nj
    st_idx = (lambda s: (0, 0, 0)) if shared else (lambda s: (stream(s), 0, 0))
    gla_idx = (lambda s: (0, 0, 0, 0)) if shared else (lambda s: (stream(s), 0, 0, 0))
    const2 = lambda s: (0, 0)
    tok_out = lambda s: (back(s), 0)

    seg = min(tb, MXU_DIM)
    ii = jnp.arange(seg)
    tril = ((ii[:, None] >= ii[None, :]) & (ii[:, None] // CHUNK == ii[None, :] // CHUNK)).astype(BF16)
    ii = jnp.arange(tb)
    sup = (ii[:, None] < ii[None, :]).astype(BF16)

    in_specs = [
        pl.BlockSpec((tb, D_MODEL), lambda s: (front(s), 0)),
        pl.BlockSpec((per_blk, POOL_ROWS, D_POOL), st_idx),
        pl.BlockSpec((per_blk, GLA_HEADS, GLA_DK, GLA_DV), gla_idx),
        pl.BlockSpec((N_EXPERTS, 1), const2),
        pl.BlockSpec((seg, seg), const2),
        pl.BlockSpec((tb, tb), const2),
    ]
    for wgt in weights:
        in_specs.append(pl.BlockSpec(wgt.shape, (lambda s, n=wgt.ndim: (0,) * n)))
    args = [x2d, pool0, st0, cnt0, tril, sup, *weights]

    out_shape = [
        jax.ShapeDtypeStruct((total_rows, D_MODEL), F32),
        jax.ShapeDtypeStruct((HN_PLANES, total_rows, LANES), I32),
        jax.ShapeDtypeStruct((total_rows, LANES), F32),
        jax.ShapeDtypeStruct((n_blk, ROUTE_ROWS, tb), F32),
        jax.ShapeDtypeStruct((batch, POOL_ROWS, D_POOL), F32),
        jax.ShapeDtypeStruct((batch, GLA_HEADS, GLA_DK, GLA_DV), F32),
        jax.ShapeDtypeStruct((N_EXPERTS, 1), F32),
    ]
    out_specs = [
        pl.BlockSpec((tb, D_MODEL), tok_out),
        pl.BlockSpec((HN_PLANES, tb, LANES), lambda s: (0, back(s), 0)),
        pl.BlockSpec((tb, LANES), tok_out),
        pl.BlockSpec((1, ROUTE_ROWS, tb), lambda s: (back(s), 0, 0)),
        pl.BlockSpec((per_blk, POOL_ROWS, D_POOL), lambda s: (stream(s), 0, 0)),
        pl.BlockSpec((per_blk, GLA_HEADS, GLA_DK, GLA_DV), lambda s: (stream(s), 0, 0, 0)),
        pl.BlockSpec((N_EXPERTS, 1), const2),
    ]
    n_grp = GLA_HEADS // HEAD_GROUP
    scratch = [
        pltpu.VMEM((POOL_ROWS + tb, D_POOL), F32),
        pltpu.VMEM((GLA_HEADS, GLA_DK, GLA_DV), F32),
        pltpu.VMEM((tb // CHUNK, n_grp, HEAD_GROUP * CHUNK, HEAD_GROUP * GLA_DK), BF16),
        pltpu.VMEM((tb // CHUNK, n_grp, HEAD_GROUP * CHUNK, HEAD_GROUP * GLA_DV), BF16),
        pltpu.VMEM((tb // CHUNK, n_grp, HEAD_GROUP * GLA_DK, HEAD_GROUP * GLA_DV), BF16),
        pltpu.VMEM((tb, D_V), F32),
        pltpu.VMEM((N_EXPERTS, 1), F32),
        pltpu.VMEM((2, tb, D_MODEL), F32),
        pltpu.VMEM((2, tb, D_MAIN), F32),
        pltpu.VMEM((2, GATE_RANK, tb), F32),
    ]
    return pl.pallas_call(
        functools.partial(_mixer_kernel, tb=tb, nj=nj, lead_pad=lead_pad, chained=chained),
        grid=(n_blk + 1,),
        in_specs=in_specs,
        out_specs=out_specs,
        out_shape=out_shape,
        scratch_shapes=scratch,
        compiler_params=pltpu.CompilerParams(
            dimension_semantics=("arbitrary",), vmem_limit_bytes=VMEM_LIMIT),
        name=f"mixer_tb{tb}_pad{lead_pad}",
    )(*args)


def _sc_mesh():
    return plsc.VectorSubcoreMesh(core_axis_name="c", subcore_axis_name="s",
                                  num_cores=SC_CORES, num_subcores=SC_SUBCORES)


def _sc_worker():
    return lax.axis_index("s") * SC_CORES + lax.axis_index("c")


def _plane_rows(dest, planes, rows_per_plane):
    offs = (jnp.arange(planes, dtype=I32) * rows_per_plane)[None, :, None]
    return dest[:, None, :] + offs


def _dispatch(hn_p, hn_s, dest0, dest1, n_rows):
    planes, t_p, _ = hn_p.shape
    t_s = hn_s.shape[1]
    n_cp = t_p // SC_ROWS
    cp = n_cp // SC_WORKERS
    n_cs = t_s // SC_ROWS
    assert t_p == cp * SC_ROWS * SC_WORKERS and t_s == n_cs * SC_ROWS and n_cs <= SC_WORKERS
    idx0 = _plane_rows(dest0, planes, n_rows)
    idx1 = _plane_rows(dest1, planes, n_rows)

    def body(hnp_hbm, hns_hbm, d0_hbm, d1_hbm, xs_hbm, rows_v, i0_v, i1_v, is0_v, is1_v, sem_in, sem_out):
        wid = _sc_worker()
        pltpu.sync_copy(d0_hbm.at[pl.ds(wid * cp, cp)], i0_v)
        pltpu.sync_copy(d1_hbm.at[pl.ds(wid * cp, cp)], i1_v)

        def move(src_hbm, src_rows, row0, i0, i1, c):
            loads = [pltpu.async_copy(src_hbm.at[pl.ds(p * src_rows + row0, SC_ROWS)], rows_v.at[p], sem_in)
                     for p in range(planes)]
            for cpy in loads:
                cpy.wait()
            stores = []
            for p in range(planes):
                stores.append(pltpu.async_copy(rows_v.at[p], xs_hbm.at[i0.at[c, p]], sem_out))
                stores.append(pltpu.async_copy(rows_v.at[p], xs_hbm.at[i1.at[c, p]], sem_out))
            for cpy in stores:
                cpy.wait()

        @pl.loop(0, cp)
        def _(c):
            move(hnp_hbm, t_p, (wid * cp + c) * SC_ROWS, i0_v, i1_v, c)

        @pl.when(wid < n_cs)
        def _():
            pltpu.sync_copy(d0_hbm.at[pl.ds(n_cp + wid, 1)], is0_v)
            pltpu.sync_copy(d1_hbm.at[pl.ds(n_cp + wid, 1)], is1_v)
            move(hns_hbm, t_s, wid * SC_ROWS, is0_v, is1_v, 0)

    xs = pl.kernel(
        body,
        out_type=jax.ShapeDtypeStruct((planes * n_rows, LANES), I32),
        mesh=_sc_mesh(),
        scratch_types=[
            pltpu.VMEM((planes, SC_ROWS, LANES), I32),
            pltpu.VMEM((cp, planes, SC_ROWS), I32),
            pltpu.VMEM((cp, planes, SC_ROWS), I32),
            pltpu.VMEM((1, planes, SC_ROWS), I32),
            pltpu.VMEM((1, planes, SC_ROWS), I32),
            pltpu.SemaphoreType.DMA,
            pltpu.SemaphoreType.DMA,
        ],
        name="moe_dispatch_sc",
    )(hn_p.reshape(planes * t_p, LANES), hn_s.reshape(planes * t_s, LANES), idx0, idx1)
    return xs.reshape(planes, n_rows, LANES)


def _gather(ys, dest0, dest1):
    planes, n_rows, _ = ys.shape
    n_chunks = dest0.shape[0]
    n_tok = n_chunks * SC_ROWS
    cpw = max(n_chunks // SC_WORKERS, 1)
    assert n_chunks <= SC_WORKERS or n_chunks == cpw * SC_WORKERS
    idx = (_plane_rows(dest0, planes, n_rows), _plane_rows(dest1, planes, n_rows))

    def body(ys_hbm, d0_hbm, d1_hbm, out_hbm, rows_v, i0_v, i1_v, sem_in, sem_out):
        wid = _sc_worker()

        def work():
            pltpu.sync_copy(d0_hbm.at[pl.ds(wid * cpw, cpw)], i0_v)
            pltpu.sync_copy(d1_hbm.at[pl.ds(wid * cpw, cpw)], i1_v)

            @pl.loop(0, cpw)
            def _(c):
                row0 = (wid * cpw + c) * SC_ROWS
                for j, i_v in enumerate((i0_v, i1_v)):
                    loads = [pltpu.async_copy(ys_hbm.at[i_v.at[c, p]], rows_v.at[p], sem_in)
                             for p in range(planes)]
                    for cpy in loads:
                        cpy.wait()
                    stores = [
                        pltpu.async_copy(
                            rows_v.at[p], out_hbm.at[pl.ds((j * planes + p) * n_tok + row0, SC_ROWS)], sem_out)
                        for p in range(planes)]
                    for cpy in stores:
                        cpy.wait()

        if n_chunks < SC_WORKERS:
            pl.when(wid < n_chunks)(work)
        else:
            work()

    out = pl.kernel(
        body,
        out_type=jax.ShapeDtypeStruct((2 * planes * n_tok, LANES), ys.dtype),
        mesh=_sc_mesh(),
        scratch_types=[
            pltpu.VMEM((planes, SC_ROWS, LANES), ys.dtype),
            pltpu.VMEM((cpw, planes, SC_ROWS), I32),
            pltpu.VMEM((cpw, planes, SC_ROWS), I32),
            pltpu.SemaphoreType.DMA,
            pltpu.SemaphoreType.DMA,
        ],
        name="moe_gather_sc",
    )(ys.reshape(planes * n_rows, LANES), *idx)
    return out.reshape(2, planes, n_tok, LANES)


def _expert_kernel(be_ref, bv_ref, nb_ref, xs_ref, wg_ref, wu_ref, wd_ref, ys_ref, wgu_s, wd_s):
    i = pl.program_id(0)
    live = i < nb_ref[0]
    prev = be_ref[jnp.maximum(i - 1, 0)]
    fresh = (i == 0) | (be_ref[i] != prev)

    @pl.when(live & fresh)
    def _():
        wgu_s[:, 0:D_EXPERT] = wg_ref[0].astype(BF16)
        wgu_s[:, D_EXPERT:2 * D_EXPERT] = wu_ref[0].astype(BF16)
        wd_s[...] = wd_ref[0].astype(BF16)

    @pl.when(live)
    def _():
        valid = lax.broadcasted_iota(I32, (MOE_BLK, LANES), 0) < bv_ref[i]
        xb = _unpack_planes([jnp.where(valid, xs_ref[p], 0) for p in range(HN_PLANES)]).astype(BF16)
        gu = _dot(xb, wgu_s[...])
        gate = gu[:, 0:D_EXPERT]
        hmid = gate * jax.nn.sigmoid(gate) * gu[:, D_EXPERT:]
        _pack_planes(_dot(hmid.astype(BF16), wd_s[...]), ys_ref)

    @pl.when(jnp.logical_not(live))
    def _():
        ys_ref[...] = jnp.zeros_like(ys_ref)


def _experts(block_e, block_valid, nb, xs, w_eg, w_eu, w_ed):
    n_rows = xs.shape[1]
    n_blocks = n_rows // MOE_BLK
    row_idx = lambda i, be, bv, nb: (0, jnp.minimum(i, nb[0] - 1), 0)
    w_idx = lambda i, be, bv, nb: (be[i], 0, 0)
    grid_spec = pltpu.PrefetchScalarGridSpec(
        num_scalar_prefetch=3,
        grid=(n_blocks,),
        in_specs=[
            pl.BlockSpec((HN_PLANES, MOE_BLK, LANES), row_idx),
            pl.BlockSpec((1, D_MODEL, D_EXPERT), w_idx),
            pl.BlockSpec((1, D_MODEL, D_EXPERT), w_idx),
            pl.BlockSpec((1, D_EXPERT, D_MODEL), w_idx),
        ],
        out_specs=pl.BlockSpec((HN_PLANES, MOE_BLK, LANES), lambda i, be, bv, nb: (0, i, 0)),
        scratch_shapes=[
            pltpu.VMEM((D_MODEL, 2 * D_EXPERT), BF16),
            pltpu.VMEM((D_EXPERT, D_MODEL), BF16),
        ],
    )
    return pl.pallas_call(
        _expert_kernel,
        grid_spec=grid_spec,
        out_shape=jax.ShapeDtypeStruct((HN_PLANES, n_rows, LANES), I32),
        compiler_params=pltpu.CompilerParams(
            dimension_semantics=("arbitrary",), vmem_limit_bytes=VMEM_LIMIT),
        name="moe_experts",
    )(block_e, block_valid, nb, xs, w_eg, w_eu, w_ed)


def _combine_kernel(h_ref, route_ref, nfin_ref, rows_ref, *rest):
    y_ref = rest[-1]
    route = route_ref[...]
    ys1 = _unpack_planes([rows_ref[0, p] for p in range(HN_PLANES)])
    ys2 = _unpack_planes([rows_ref[1, p] for p in range(HN_PLANES)])
    out = h_ref[...] + (ys1 * route[:, 2:3] + ys2 * route[:, 3:4])
    y_ref[...] = _rms(out, nfin_ref[...])


def _combine(h, route, norm_final, rows, *, row0=0, y_prev=None):
    total = h.shape[0]
    tb = COMBINE_TB
    blk0 = row0 // tb
    args = [h, route, norm_final, rows]
    in_specs = [
        pl.BlockSpec((tb, D_MODEL), lambda i: (blk0 + i, 0)),
        pl.BlockSpec((tb, LANES), lambda i: (blk0 + i, 0)),
        pl.BlockSpec((1, D_MODEL), lambda i: (0, 0)),
        pl.BlockSpec((2, HN_PLANES, tb, LANES), lambda i: (0, 0, i, 0)),
    ]
    aliases = {}
    if y_prev is not None:
        aliases[len(args)] = 0
        args.append(y_prev)
        in_specs.append(pl.BlockSpec(memory_space=pl.ANY))
    return pl.pallas_call(
        _combine_kernel,
        grid=(rows.shape[2] // tb,),
        in_specs=in_specs,
        out_specs=pl.BlockSpec((tb, D_MODEL), lambda i: (blk0 + i, 0)),
        out_shape=jax.ShapeDtypeStruct((total, D_MODEL), F32),
        input_output_aliases=aliases,
        compiler_params=pltpu.CompilerParams(
            dimension_semantics=("arbitrary",), vmem_limit_bytes=VMEM_LIMIT),
        name="moe_combine",
    )(*args)


def kernel(x_prompt, x_sample, state_pool, state_gla, meta_tokens, norm_mix, w_in, w_gate_up, b_gate, w_pool, pool_scale, gla_norm, w_out, norm_ffn, w_router_group, b_router_group, w_router_expert, b_router_expert, w_expert_gate, w_expert_up, w_expert_down, norm_final):
    assert w_in.shape[0] == 1, "one encoder layer"
    batch, seq, _ = x_prompt.shape
    dec_batch, dec_seq, _ = x_sample.shape
    assert seq % MIX_TB == 0 and dec_seq == CHUNK and N_META <= CHUNK
    t_prompt = batch * seq
    t_sample = dec_batch * dec_seq
    t_all = t_prompt + t_sample
    assert t_prompt % COMBINE_TB == 0 and t_sample % COMBINE_TB == 0

    w_in0 = w_in[0]
    gpad = EXPERTS_PER_GROUP - N_GROUPS
    rpad = ROUTER_ROWS - EXPERTS_PER_GROUP - N_EXPERTS
    w_router = jnp.concatenate([
        w_router_group[0].T, jnp.zeros((gpad, D_MODEL), F32),
        w_router_expert[0].T, jnp.zeros((rpad, D_MODEL), F32)], axis=0)
    b_router = jnp.concatenate([
        b_router_group[0], jnp.zeros((gpad,), F32), b_router_expert[0], jnp.zeros((rpad,), F32)])
    zg = jnp.zeros((POOL_GROUP_DIM, POOL_GROUP_DIM), F32)
    w_pool_pairs = jnp.stack([
        jnp.block([[w_pool[0, 2 * i], zg], [zg, w_pool[0, 2 * i + 1]]]) for i in range(len(POOL_WINDOWS) // 2)])
    weights = (
        norm_mix[0][None, :],
        w_in0[:, :D_MAIN].astype(BF16),
        w_in0[:, D_MAIN:].T.astype(BF16),
        w_gate_up[0].astype(BF16),
        b_gate[0][None, :],
        w_pool_pairs.astype(BF16),
        pool_scale[0][None, :],
        gla_norm[0][None, :],
        w_out[0].astype(BF16),
        norm_ffn[0][None, :],
        w_router.astype(BF16),
        b_router[:, None],
    )

    zero_cnt = jnp.zeros((N_EXPERTS, 1), F32)
    x_meta = jnp.pad(meta_tokens.astype(F32), ((CHUNK - N_META, 0), (0, 0)))
    meta = _mixer(x_meta, jnp.zeros((1, POOL_ROWS, D_POOL), F32),
                  jnp.zeros((1, GLA_HEADS, GLA_DK, GLA_DV), F32), zero_cnt,
                  weights, batch=1, seq=CHUNK, tb=CHUNK, lead_pad=CHUNK - N_META)
    h_m, hn_m, route_m, rt_m, pool_m, st_m, cnt_m = meta
    del h_m, hn_m, route_m, rt_m, cnt_m
    h_p, hn_p, route_p, rt_p, pool_p, st_p, cnt_p = _mixer(
        x_prompt.reshape(t_prompt, D_MODEL), pool_m, st_m, zero_cnt, weights,
        batch=batch, seq=seq, tb=MIX_TB, lead_pad=0)
    pool_s0 = jnp.pad(state_pool[0], ((0, 0), (POOL_ROWS - POOL_PAD, 0), (0, 0)))
    h_s, hn_s, route_s, rt_s, pool_s, st_s, cnt_s = _mixer(
        x_sample.reshape(t_sample, D_MODEL), pool_s0, state_gla[0].astype(F32),
        cnt_p, weights, batch=dec_batch, seq=dec_seq, tb=SAMPLE_TB, lead_pad=0)

    counts = cnt_s[:, 0].astype(I32)
    padded = (counts + MOE_BLK - 1) // MOE_BLK * MOE_BLK
    ends = jnp.cumsum(padded)
    pstart = ends - padded
    n_blocks = (2 * t_all + N_EXPERTS * (MOE_BLK - 1) + MOE_BLK - 1) // MOE_BLK
    nb = (ends[-1] // MOE_BLK).astype(I32)
    blk_ids = jnp.minimum(jnp.arange(n_blocks, dtype=I32), nb - 1)
    block_e = jnp.sum((ends[None, :] <= (blk_ids * MOE_BLK)[:, None]).astype(I32), axis=1)
    block_e = jnp.minimum(block_e, N_EXPERTS - 1)
    owner = block_e[:, None] == jnp.arange(N_EXPERTS, dtype=I32)
    row_end = jnp.sum(jnp.where(owner, pstart + counts, 0), axis=1)
    block_valid = jnp.clip(row_end - blk_ids * MOE_BLK, 0, MOE_BLK)

    def dest_rows(rt):
        rt = rt.transpose(1, 0, 2).reshape(ROUTE_ROWS, -1)
        onehot = rt[0:2].astype(I32)[..., None] == jnp.arange(N_EXPERTS, dtype=I32)
        return jnp.sum(jnp.where(onehot, pstart, 0), axis=-1) + rt[4:6].astype(I32)

    dest = jnp.concatenate([dest_rows(rt_p), dest_rows(rt_s)], axis=1)
    dest0 = dest[0].reshape(t_all // SC_ROWS, SC_ROWS)
    dest1 = dest[1].reshape(t_all // SC_ROWS, SC_ROWS)
    xs = _dispatch(hn_p, hn_s, dest0, dest1, n_blocks * MOE_BLK)
    ys = _experts(block_e, block_valid.astype(I32), nb[None], xs,
                  w_expert_gate[0], w_expert_up[0], w_expert_down[0])
    nfin = norm_final[None, :]
    cp_chunks = t_prompt // SC_ROWS
    part = cp_chunks // COMBINE_PARTS
    y_prompt = None
    for i in range(COMBINE_PARTS):
        ch = slice(i * part, (i + 1) * part)
        rows_i = _gather(ys, dest0[ch], dest1[ch])
        y_prompt = _combine(h_p, route_p, nfin, rows_i, row0=i * part * SC_ROWS, y_prev=y_prompt)
    y_prompt = y_prompt.reshape(batch, seq, D_MODEL)
    rows_s = _gather(ys, dest0[cp_chunks:], dest1[cp_chunks:])
    y_sample = _combine(h_s, route_s, nfin, rows_s).reshape(dec_batch, dec_seq, D_MODEL)
    new_pool_prompt = pool_p[:, POOL_ROWS - POOL_PAD:][None]
    new_gla_prompt = st_p[None]
    new_pool_sample = pool_s[:, POOL_ROWS - POOL_PAD:][None]
    new_gla_sample = st_s[None]
    return (y_prompt, y_sample, new_pool_prompt, new_gla_prompt, new_pool_sample, new_gla_sample)
```

```python
import functools

import jax
import jax.numpy as jnp
from jax import lax
from jax.experimental import pallas as pl
from jax.experimental.pallas import tpu as pltpu
from jax.experimental.pallas import tpu_sc as plsc

F32 = jnp.float32
BF16 = jnp.bfloat16
U32 = jnp.uint32
I32 = jnp.int32

D_MODEL = 1024
N_META = 16
CHUNK = 64
EPS = 1e-6
D_POOL = 512
POOL_WINDOWS = (2, 4, 8, 16)
POOL_GROUP_DIM = 128
POOL_PAD = 15
POOL_ROWS = 16
GLA_HEADS = 4
GLA_DK = 64
GLA_DV = 128
D_QK = 256
D_V = 512
HEAD_GROUP = 2
GATE_RANK = 16
GATE_TAU = 16.0
D_MAIN = D_POOL + 2 * D_QK + 2 * D_V
N_GROUPS = 4
EXPERTS_PER_GROUP = 8
N_EXPERTS = 32
D_EXPERT = 512

LANES = 128
MXU_DIM = 256
HALF = D_MODEL // 2
HN_PLANES = HALF // LANES
ROUTE_ROWS = 8
ROUTER_ROWS = 64
MIX_TB = 512
SAMPLE_TB = 512
FRONT_TILE = 256
FRONT_PLAN = (3, 0, 1, 2, 2, 0)
MOE_BLK = 768
COMBINE_TB = 1024
COMBINE_PARTS = 4
VMEM_LIMIT = 48 * 1024 * 1024
SC_CORES = 2
SC_SUBCORES = 16
SC_WORKERS = SC_CORES * SC_SUBCORES
SC_ROWS = 128


def _rms(x, g):
    return x * lax.rsqrt(jnp.mean(x * x, axis=-1, keepdims=True) + EPS) * g


def _dot(a, b):
    return jnp.dot(a, b, preferred_element_type=F32)


def _dot_nt(a, b):
    return lax.dot_general(a, b, (((1,), (1,)), ((), ())), preferred_element_type=F32)


def _pack_planes(x, ref):
    xb = x.astype(BF16)
    lo = lax.bitcast_convert_type(xb[:, :HALF].astype(F32), U32) >> 16
    hi = lax.bitcast_convert_type(xb[:, HALF:].astype(F32), U32) & jnp.uint32(0xFFFF0000)
    packed = lax.bitcast_convert_type(lo | hi, I32)
    for p in range(HN_PLANES):
        ref[p] = packed[:, p * LANES:(p + 1) * LANES]


def _unpack_planes(planes):
    words = [lax.bitcast_convert_type(p, U32) for p in planes]
    los = [lax.bitcast_convert_type(w << 16, F32) for w in words]
    his = [lax.bitcast_convert_type(w & jnp.uint32(0xFFFF0000), F32) for w in words]
    return jnp.concatenate(los + his, axis=-1)


def _dot_tn(a, b):
    return lax.dot_general(a, b, (((0,), (0,)), ((), ())), preferred_element_type=F32)


def _mixer_kernel(x_ref, pool0_ref, st0_ref, cnt0_ref, tril_ref, sup_ref,
                  nmix_ref, wmain_ref, wz_ref, wgu_ref, bgate_ref, wpool_ref, pscale_ref,
                  gnorm_ref, wout_ref, nffn_ref, wr_ref, br_ref,
                  h_ref, hn_ref, route_ref, route_t_ref, pool_out_ref, st_out_ref, cnt_out_ref,
                  ext_ref, st_ref, kbd_ref, vbd_ref, sbd_ref, o_ref, cnt_ref,
                  xs_ref, proj_ref, z_ref, *, tb, nj, lead_pad, chained):
    s = pl.program_id(0)
    back = jnp.maximum(s - 1, 0)
    j = lax.rem(back, nj)
    n_chunks = tb // CHUNK
    wr_slot = lax.rem(s, 2)
    rd_slot = 1 - wr_slot

    @pl.when(s == 0)
    def _():
        kbd_ref[...] = jnp.zeros_like(kbd_ref)
        vbd_ref[...] = jnp.zeros_like(vbd_ref)
        sbd_ref[...] = jnp.zeros_like(sbd_ref)
        xs_ref[1] = jnp.zeros((tb, D_MODEL), F32)
        proj_ref[1] = jnp.zeros((tb, D_MAIN), F32)
        z_ref[1] = jnp.zeros((GATE_RANK, tb), F32)

    @pl.when(s <= 1)
    def _():
        cnt_ref[...] = cnt0_ref[...]

    def put_state(c, hh, st):
        gg, hp = divmod(hh, HEAD_GROUP)
        sbd_ref[c, gg, hp * GLA_DK:(hp + 1) * GLA_DK, hp * GLA_DV:(hp + 1) * GLA_DV] = st.astype(BF16)

    if chained:
        @pl.when(j == 0)
        def _():
            ext_ref[0:POOL_ROWS, :] = pool0_ref[0]
            st_ref[...] = st0_ref[0]

        for hh in range(GLA_HEADS):
            put_state(0, hh, st_ref[hh])
    else:
        for c in range(n_chunks):
            for hh in range(GLA_HEADS):
                put_state(c, hh, st0_ref[c, hh])

    x_new = x_ref[...]
    xn = _rms(x_new, nmix_ref[...]).astype(BF16)
    xs_ref[wr_slot] = x_new

    tiles_done = [0]

    def front_tiles(stage):
        for t in range(tiles_done[0], tiles_done[0] + FRONT_PLAN[stage]):
            cols = slice(t * FRONT_TILE, (t + 1) * FRONT_TILE)
            proj_ref[wr_slot, :, cols] = _dot(xn, wmain_ref[:, cols])
        tiles_done[0] += FRONT_PLAN[stage]

    front_tiles(0)
    z_ref[wr_slot] = _dot_nt(wz_ref[...], xn)

    x = xs_ref[rd_slot]
    z = z_ref[rd_slot]
    u = proj_ref[rd_slot, :, 0:D_POOL]
    q = proj_ref[rd_slot, :, D_POOL:D_POOL + D_QK]
    k = proj_ref[rd_slot, :, D_POOL + D_QK:D_POOL + 2 * D_QK]
    v = proj_ref[rd_slot, :, D_POOL + 2 * D_QK:D_POOL + 2 * D_QK + D_V]
    r = proj_ref[rd_slot, :, D_POOL + 2 * D_QK + D_V:D_MAIN]

    row = lax.broadcasted_iota(I32, (tb, 1), 0)

    pseg = POOL_ROWS + CHUNK
    if chained:
        ext_ref[POOL_ROWS:POOL_ROWS + tb, :] = u
        ext = ext_ref[...]
    else:
        ext = jnp.concatenate(
            [blk for c in range(n_chunks) for blk in (pool0_ref[c], u[c * CHUNK:(c + 1) * CHUNK])], axis=0)
    pooled = []
    for g, w in enumerate(POOL_WINDOWS):
        sl = slice(g * POOL_GROUP_DIM, (g + 1) * POOL_GROUP_DIM)
        acc = ext[:, sl]
        for d in range(g + 1):
            acc = acc + pltpu.roll(acc, 1 << d, axis=0)
        if chained:
            win = acc[POOL_ROWS:, :]
        else:
            win = jnp.concatenate([acc[c * pseg + POOL_ROWS:(c + 1) * pseg] for c in range(n_chunks)], axis=0)
        if lead_pad:
            cnt = jnp.clip(row - lead_pad + 1, 1, w).astype(F32)
            pooled.append(win / cnt - u[:, sl])
        else:
            pooled.append(win * (1.0 / w) - u[:, sl])
    pys = [_dot(jnp.concatenate(pooled[2 * i:2 * i + 2], axis=-1).astype(BF16), wpool_ref[i])
           for i in range(len(POOL_WINDOWS) // 2)]
    pool_y = jnp.concatenate(pys, axis=-1) * pscale_ref[...]
    if chained:
        ext_ref[0:POOL_ROWS, :] = ext_ref[tb:tb + POOL_ROWS, :]
    else:
        for c in range(n_chunks):
            pool_out_ref[c] = u[(c + 1) * CHUNK - POOL_ROWS:(c + 1) * CHUNK]

    gpre = _dot_tn(z.astype(BF16), wgu_ref[...]) + bgate_ref[...]
    log_a = (jnp.minimum(gpre, 0.0) - jnp.log(1.0 + jnp.exp(-jnp.abs(gpre)))) * (1.0 / GATE_TAU)
    if lead_pad:
        log_a = jnp.where(row >= lead_pad, log_a, 0.0)
    a_hi = log_a.astype(BF16)
    a_lo = (log_a - a_hi.astype(F32)).astype(BF16)
    tril = tril_ref[...]
    seg = tril.shape[0]
    bcum = jnp.concatenate(
        [_dot(tril, a_hi[r0:r0 + seg]) + _dot(tril, a_lo[r0:r0 + seg]) for r0 in range(0, tb, seg)], axis=0)
    front_tiles(1)
    eb = jnp.exp(bcum)
    qi = q * (GLA_DK ** -0.5) * eb
    ki = k * jnp.exp(-bcum)

    rr = lax.broadcasted_iota(I32, (CHUNK, HEAD_GROUP * CHUNK), 0)
    cc = lax.broadcasted_iota(I32, (CHUNK, HEAD_GROUP * CHUNK), 1)
    causal = (cc % CHUNK) <= rr

    lasts = [eb[(c + 1) * CHUNK - 1:(c + 1) * CHUNK, :] for c in range(n_chunks)]
    dcol = jnp.concatenate(lasts + [jnp.zeros((LANES - n_chunks, D_QK), F32)], axis=0).T

    n_grp = GLA_HEADS // HEAD_GROUP
    chunk_rows = [slice(c * CHUNK, (c + 1) * CHUNK) for c in range(n_chunks)]
    grp_k = [slice(g * HEAD_GROUP * GLA_DK, (g + 1) * HEAD_GROUP * GLA_DK) for g in range(n_grp)]
    grp_v = [slice(g * HEAD_GROUP * GLA_DV, (g + 1) * HEAD_GROUP * GLA_DV) for g in range(n_grp)]
    qi_b = qi.astype(BF16)
    ki_b = ki.astype(BF16)
    v_b = v.astype(BF16)

    scores = {}
    for c in range(n_chunks):
        for hh in range(GLA_HEADS):
            gg, hp = divmod(hh, HEAD_GROUP)
            kbd_ref[c, gg, hp * CHUNK:(hp + 1) * CHUNK, hp * GLA_DK:(hp + 1) * GLA_DK] = (
                ki_b[chunk_rows[c], hh * GLA_DK:(hh + 1) * GLA_DK])
            vbd_ref[c, gg, hp * CHUNK:(hp + 1) * CHUNK, hp * GLA_DV:(hp + 1) * GLA_DV] = (
                v_b[chunk_rows[c], hh * GLA_DV:(hh + 1) * GLA_DV])
        for gg in range(n_grp):
            scores[c, gg] = _dot_nt(qi_b[chunk_rows[c], grp_k[gg]], kbd_ref[c, gg])

    kvs = {}
    for c in range(n_chunks):
        for hh in range(GLA_HEADS):
            kvs[c, hh] = _dot_tn(ki_b[chunk_rows[c], hh * GLA_DK:(hh + 1) * GLA_DK],
                                 v_b[chunk_rows[c], hh * GLA_DV:(hh + 1) * GLA_DV])
    front_tiles(2)

    for hh in range(GLA_HEADS):
        st = st_ref[hh] if chained else None
        for c in range(n_chunks):
            s_old = st if chained else st0_ref[c, hh]
            s_new = (s_old + kvs[c, hh]) * dcol[hh * GLA_DK:(hh + 1) * GLA_DK, c:c + 1]
            if not chained:
                st_out_ref[c, hh] = s_new
            else:
                st = s_new
                if c + 1 < n_chunks:
                    put_state(c + 1, hh, s_new)
        if chained:
            st_ref[hh] = st

    for c in range(n_chunks):
        for gg in range(n_grp):
            p = jnp.where(causal, scores[c, gg], 0.0).astype(BF16)
            o_ref[chunk_rows[c], grp_v[gg]] = (
                _dot(p, vbd_ref[c, gg]) + _dot(qi_b[chunk_rows[c], grp_k[gg]], sbd_ref[c, gg]))

    o = o_ref[...]
    ons = []
    for hh in range(GLA_HEADS):
        oh = o[:, hh * GLA_DV:(hh + 1) * GLA_DV]
        ons.append(oh * lax.rsqrt(jnp.mean(oh * oh, axis=-1, keepdims=True) + EPS))
    og = jnp.concatenate(ons, axis=-1) * gnorm_ref[...] * (r * jax.nn.sigmoid(r))
    mix = _dot(jnp.concatenate([pool_y, og], axis=-1).astype(BF16), wout_ref[...])
    front_tiles(3)

    h = x + mix
    h_ref[...] = h
    hn = _rms(h, nffn_ref[...]).astype(BF16)
    _pack_planes(hn, hn_ref)

    logits = _dot_nt(wr_ref[...], hn) + br_ref[...]
    front_tiles(4)
    sub = lax.broadcasted_iota(I32, (EXPERTS_PER_GROUP, tb), 0).astype(F32)
    neg = jnp.float32(-jnp.inf)
    big = jnp.float32(EXPERTS_PER_GROUP)
    tile0 = logits[0:EXPERTS_PER_GROUP]
    is_g = sub < N_GROUPS
    gmax = jnp.max(jnp.where(is_g, tile0, neg), axis=0, keepdims=True)
    gsum = jnp.sum(jnp.where(is_g, jnp.exp(tile0 - gmax), 0.0), axis=0, keepdims=True)
    p_g = 1.0 / gsum
    gidx = jnp.min(jnp.where(is_g & (tile0 == gmax), sub, big), axis=0, keepdims=True)
    el = logits[N_GROUPS * EXPERTS_PER_GROUP:(N_GROUPS + 1) * EXPERTS_PER_GROUP]
    for g in range(N_GROUPS - 2, -1, -1):
        el = jnp.where(gidx == g, logits[(g + 1) * EXPERTS_PER_GROUP:(g + 2) * EXPERTS_PER_GROUP], el)
    m1 = jnp.max(el, axis=0, keepdims=True)
    i1 = jnp.min(jnp.where(el == m1, sub, big), axis=0, keepdims=True)
    rest = sub != i1
    m2 = jnp.max(jnp.where(rest, el, neg), axis=0, keepdims=True)
    i2 = jnp.min(jnp.where(rest & (el == m2), sub, big), axis=0, keepdims=True)
    t2 = jnp.exp(m2 - m1)
    den = 1.0 + t2
    g1 = p_g / den
    g2 = p_g * t2 / den
    e1 = gidx * EXPERTS_PER_GROUP + i1
    e2 = gidx * EXPERTS_PER_GROUP + i2

    eid = lax.broadcasted_iota(I32, (N_EXPERTS, tb), 0).astype(F32)
    oh1 = eid == e1
    oh2 = eid == e2
    both = jnp.where(oh1 | oh2, 1.0, 0.0)
    cnt = cnt_ref[...]
    before = _dot(both.astype(BF16), sup_ref[...]) + cnt
    front_tiles(5)
    assert tiles_done[0] * FRONT_TILE == D_MAIN
    pos1 = jnp.sum(jnp.where(oh1, before, 0.0), axis=0, keepdims=True)
    pos2 = jnp.sum(jnp.where(oh2, before, 0.0), axis=0, keepdims=True)
    cnt_new = cnt + jnp.sum(both, axis=1, keepdims=True)
    cnt_ref[...] = cnt_new
    cnt_out_ref[...] = cnt_new

    zero = jnp.zeros_like(e1)
    route_t = jnp.concatenate([e1, e2, g1, g2, pos1, pos2, zero, zero], axis=0)
    route_t_ref[0] = route_t
    route_ref[...] = jnp.concatenate([route_t, jnp.zeros((LANES - ROUTE_ROWS, tb), F32)], axis=0).T

    if chained:
        @pl.when(j == nj - 1)
        def _():
            pool_out_ref[0] = ext_ref[0:POOL_ROWS, :]
            st_out_ref[0] = st_ref[...]


def _mixer(x2d, pool0, st0, cnt0, weights, *, batch, seq, tb, lead_pad):
    chained = tb <= seq
    total_rows = batch * seq
    n_blk = total_rows // tb
    nj = seq // tb if chained else 1
    per_blk = 1 if chained else tb // seq
    assert chained or (seq == CHUNK and batch % per_blk == 0)
    shared = pool0.shape[0] == 1
    front = lambda s: jnp.minimum(s, n_blk - 1)
    back = lambda s: jnp.maximum(s - 1, 0)
    stream = lambda s: back(s) // nj
    st_idx = (lambda s: (0, 0, 0)) if shared else (lambda s: (stream(s), 0, 0))
    gla_idx = (lambda s: (0, 0, 0, 0)) if shared else (lambda s: (stream(s), 0, 0, 0))
    const2 = lambda s: (0, 0)
    tok_out = lambda s: (back(s), 0)

    seg = min(tb, MXU_DIM)
    ii = jnp.arange(seg)
    tril = ((ii[:, None] >= ii[None, :]) & (ii[:, None] // CHUNK == ii[None, :] // CHUNK)).astype(BF16)
    ii = jnp.arange(tb)
    sup = (ii[:, None] < ii[None, :]).astype(BF16)

    in_specs = [
        pl.BlockSpec((tb, D_MODEL), lambda s: (front(s), 0)),
        pl.BlockSpec((per_blk, POOL_ROWS, D_POOL), st_idx),
        pl.BlockSpec((per_blk, GLA_HEADS, GLA_DK, GLA_DV), gla_idx),
        pl.BlockSpec((N_EXPERTS, 1), const2),
        pl.BlockSpec((seg, seg), const2),
        pl.BlockSpec((tb, tb), const2),
    ]
    for wgt in weights:
        in_specs.append(pl.BlockSpec(wgt.shape, (lambda s, n=wgt.ndim: (0,) * n)))
    args = [x2d, pool0, st0, cnt0, tril, sup, *weights]

    out_shape = [
        jax.ShapeDtypeStruct((total_rows, D_MODEL), F32),
        jax.ShapeDtypeStruct((HN_PLANES, total_rows, LANES), I32),
        jax.ShapeDtypeStruct((total_rows, LANES), F32),
        jax.ShapeDtypeStruct((n_blk, ROUTE_ROWS, tb), F32),
        jax.ShapeDtypeStruct((batch, POOL_ROWS, D_POOL), F32),
        jax.ShapeDtypeStruct((batch, GLA_HEADS, GLA_DK, GLA_DV), F32),
        jax.ShapeDtypeStruct((N_EXPERTS, 1), F32),
    ]
    out_specs = [
        pl.BlockSpec((tb, D_MODEL), tok_out),
        pl.BlockSpec((HN_PLANES, tb, LANES), lambda s: (0, back(s), 0)),
        pl.BlockSpec((tb, LANES), tok_out),
        pl.BlockSpec((1, ROUTE_ROWS, tb), lambda s: (back(s), 0, 0)),
        pl.BlockSpec((per_blk, POOL_ROWS, D_POOL), lambda s: (stream(s), 0, 0)),
        pl.BlockSpec((per_blk, GLA_HEADS, GLA_DK, GLA_DV), lambda s: (stream(s), 0, 0, 0)),
        pl.BlockSpec((N_EXPERTS, 1), const2),
    ]
    n_grp = GLA_HEADS // HEAD_GROUP
    scratch = [
        pltpu.VMEM((POOL_ROWS + tb, D_POOL), F32),
        pltpu.VMEM((GLA_HEADS, GLA_DK, GLA_DV), F32),
        pltpu.VMEM((tb // CHUNK, n_grp, HEAD_GROUP * CHUNK, HEAD_GROUP * GLA_DK), BF16),
        pltpu.VMEM((tb // CHUNK, n_grp, HEAD_GROUP * CHUNK, HEAD_GROUP * GLA_DV), BF16),
        pltpu.VMEM((tb // CHUNK, n_grp, HEAD_GROUP * GLA_DK, HEAD_GROUP * GLA_DV), BF16),
        pltpu.VMEM((tb, D_V), F32),
        pltpu.VMEM((N_EXPERTS, 1), F32),
        pltpu.VMEM((2, tb, D_MODEL), F32),
        pltpu.VMEM((2, tb, D_MAIN), F32),
        pltpu.VMEM((2, GATE_RANK, tb), F32),
    ]
    return pl.pallas_call(
        functools.partial(_mixer_kernel, tb=tb, nj=nj, lead_pad=lead_pad, chained=chained),
        grid=(n_blk + 1,),
        in_specs=in_specs,
        out_specs=out_specs,
        out_shape=out_shape,
        scratch_shapes=scratch,
        compiler_params=pltpu.CompilerParams(
            dimension_semantics=("arbitrary",), vmem_limit_bytes=VMEM_LIMIT),
        name=f"mixer_tb{tb}_pad{lead_pad}_{'chained' if chained else 'streams'}",
    )(*args)


def _sc_mesh():
    return plsc.VectorSubcoreMesh(core_axis_name="c", subcore_axis_name="s",
                                  num_cores=SC_CORES, num_subcores=SC_SUBCORES)


def _sc_worker():
    return lax.axis_index("s") * SC_CORES + lax.axis_index("c")


def _plane_rows(dest, planes, rows_per_plane):
    offs = (jnp.arange(planes, dtype=I32) * rows_per_plane)[None, :, None]
    return dest[:, None, :] + offs


def _dispatch(hn_p, hn_s, dest0, dest1, n_rows):
    planes, t_p, _ = hn_p.shape
    t_s = hn_s.shape[1]
    n_cp = t_p // SC_ROWS
    cp = n_cp // SC_WORKERS
    n_cs = t_s // SC_ROWS
    assert t_p == cp * SC_ROWS * SC_WORKERS and t_s == n_cs * SC_ROWS and n_cs <= SC_WORKERS
    idx0 = _plane_rows(dest0, planes, n_rows)
    idx1 = _plane_rows(dest1, planes, n_rows)

    def body(hnp_hbm, hns_hbm, d0_hbm, d1_hbm, xs_hbm, rows_v, i0_v, i1_v, is0_v, is1_v, sem_in, sem_out):
        wid = _sc_worker()
        pltpu.sync_copy(d0_hbm.at[pl.ds(wid * cp, cp)], i0_v)
        pltpu.sync_copy(d1_hbm.at[pl.ds(wid * cp, cp)], i1_v)

        def move(src_hbm, src_rows, row0, i0, i1, c):
            loads = [pltpu.async_copy(src_hbm.at[pl.ds(p * src_rows + row0, SC_ROWS)], rows_v.at[p], sem_in)
                     for p in range(planes)]
            for cpy in loads:
                cpy.wait()
            stores = []
            for p in range(planes):
                stores.append(pltpu.async_copy(rows_v.at[p], xs_hbm.at[i0.at[c, p]], sem_out))
                stores.append(pltpu.async_copy(rows_v.at[p], xs_hbm.at[i1.at[c, p]], sem_out))
            for cpy in stores:
                cpy.wait()

        @pl.loop(0, cp)
        def _(c):
            move(hnp_hbm, t_p, (wid * cp + c) * SC_ROWS, i0_v, i1_v, c)

        @pl.when(wid < n_cs)
        def _():
            pltpu.sync_copy(d0_hbm.at[pl.ds(n_cp + wid, 1)], is0_v)
            pltpu.sync_copy(d1_hbm.at[pl.ds(n_cp + wid, 1)], is1_v)
            move(hns_hbm, t_s, wid * SC_ROWS, is0_v, is1_v, 0)

    xs = pl.kernel(
        body,
        out_type=jax.ShapeDtypeStruct((planes * n_rows, LANES), I32),
        mesh=_sc_mesh(),
        scratch_types=[
            pltpu.VMEM((planes, SC_ROWS, LANES), I32),
            pltpu.VMEM((cp, planes, SC_ROWS), I32),
            pltpu.VMEM((cp, planes, SC_ROWS), I32),
            pltpu.VMEM((1, planes, SC_ROWS), I32),
            pltpu.VMEM((1, planes, SC_ROWS), I32),
            pltpu.SemaphoreType.DMA,
            pltpu.SemaphoreType.DMA,
        ],
        name="moe_dispatch_sc",
    )(hn_p.reshape(planes * t_p, LANES), hn_s.reshape(planes * t_s, LANES), idx0, idx1)
    return xs.reshape(planes, n_rows, LANES)


def _gather(ys, dest0, dest1):
    planes, n_rows, _ = ys.shape
    n_chunks = dest0.shape[0]
    n_tok = n_chunks * SC_ROWS
    cpw = max(n_chunks // SC_WORKERS, 1)
    assert n_chunks <= SC_WORKERS or n_chunks == cpw * SC_WORKERS
    idx = (_plane_rows(dest0, planes, n_rows), _plane_rows(dest1, planes, n_rows))

    def body(ys_hbm, d0_hbm, d1_hbm, out_hbm, rows_v, i0_v, i1_v, sem_in, sem_out):
        wid = _sc_worker()

        def work():
            pltpu.sync_copy(d0_hbm.at[pl.ds(wid * cpw, cpw)], i0_v)
            pltpu.sync_copy(d1_hbm.at[pl.ds(wid * cpw, cpw)], i1_v)

            @pl.loop(0, cpw)
            def _(c):
                row0 = (wid * cpw + c) * SC_ROWS
                for j, i_v in enumerate((i0_v, i1_v)):
                    loads = [pltpu.async_copy(ys_hbm.at[i_v.at[c, p]], rows_v.at[p], sem_in)
                             for p in range(planes)]
                    for cpy in loads:
                        cpy.wait()
                    stores = [
                        pltpu.async_copy(
                            rows_v.at[p], out_hbm.at[pl.ds((j * planes + p) * n_tok + row0, SC_ROWS)], sem_out)
                        for p in range(planes)]
                    for cpy in stores:
                        cpy.wait()

        if n_chunks < SC_WORKERS:
            pl.when(wid < n_chunks)(work)
        else:
            work()

    out = pl.kernel(
        body,
        out_type=jax.ShapeDtypeStruct((2 * planes * n_tok, LANES), ys.dtype),
        mesh=_sc_mesh(),
        scratch_types=[
            pltpu.VMEM((planes, SC_ROWS, LANES), ys.dtype),
            pltpu.VMEM((cpw, planes, SC_ROWS), I32),
            pltpu.VMEM((cpw, planes, SC_ROWS), I32),
            pltpu.SemaphoreType.DMA,
            pltpu.SemaphoreType.DMA,
        ],
        name="moe_gather_sc",
    )(ys.reshape(planes * n_rows, LANES), *idx)
    return out.reshape(2, planes, n_tok, LANES)


def _expert_kernel(be_ref, bv_ref, nb_ref, xs_ref, wg_ref, wu_ref, wd_ref, ys_ref, wgu_s, wd_s):
    i = pl.program_id(0)
    live = i < nb_ref[0]
    prev = be_ref[jnp.maximum(i - 1, 0)]
    fresh = (i == 0) | (be_ref[i] != prev)

    @pl.when(live & fresh)
    def _():
        wgu_s[:, 0:D_EXPERT] = wg_ref[0].astype(BF16)
        wgu_s[:, D_EXPERT:2 * D_EXPERT] = wu_ref[0].astype(BF16)
        wd_s[...] = wd_ref[0].astype(BF16)

    @pl.when(live)
    def _():
        valid = lax.broadcasted_iota(I32, (MOE_BLK, LANES), 0) < bv_ref[i]
        xb = _unpack_planes([jnp.where(valid, xs_ref[p], 0) for p in range(HN_PLANES)]).astype(BF16)
        gu = _dot(xb, wgu_s[...])
        gate = gu[:, 0:D_EXPERT]
        hmid = gate * jax.nn.sigmoid(gate) * gu[:, D_EXPERT:]
        _pack_planes(_dot(hmid.astype(BF16), wd_s[...]), ys_ref)

    @pl.when(jnp.logical_not(live))
    def _():
        ys_ref[...] = jnp.zeros_like(ys_ref)


def _experts(block_e, block_valid, nb, xs, w_eg, w_eu, w_ed):
    n_rows = xs.shape[1]
    n_blocks = n_rows // MOE_BLK
    row_idx = lambda i, be, bv, nb: (0, jnp.minimum(i, nb[0] - 1), 0)
    w_idx = lambda i, be, bv, nb: (be[i], 0, 0)
    grid_spec = pltpu.PrefetchScalarGridSpec(
        num_scalar_prefetch=3,
        grid=(n_blocks,),
        in_specs=[
            pl.BlockSpec((HN_PLANES, MOE_BLK, LANES), row_idx),
            pl.BlockSpec((1, D_MODEL, D_EXPERT), w_idx),
            pl.BlockSpec((1, D_MODEL, D_EXPERT), w_idx),
            pl.BlockSpec((1, D_EXPERT, D_MODEL), w_idx),
        ],
        out_specs=pl.BlockSpec((HN_PLANES, MOE_BLK, LANES), lambda i, be, bv, nb: (0, i, 0)),
        scratch_shapes=[
            pltpu.VMEM((D_MODEL, 2 * D_EXPERT), BF16),
            pltpu.VMEM((D_EXPERT, D_MODEL), BF16),
        ],
    )
    return pl.pallas_call(
        _expert_kernel,
        grid_spec=grid_spec,
        out_shape=jax.ShapeDtypeStruct((HN_PLANES, n_rows, LANES), I32),
        compiler_params=pltpu.CompilerParams(
            dimension_semantics=("arbitrary",), vmem_limit_bytes=VMEM_LIMIT),
        name="moe_experts",
    )(block_e, block_valid, nb, xs, w_eg, w_eu, w_ed)


def _combine_kernel(h_ref, route_ref, nfin_ref, rows_ref, *rest):
    y_ref = rest[-1]
    route = route_ref[...]
    ys1 = _unpack_planes([rows_ref[0, p] for p in range(HN_PLANES)])
    ys2 = _unpack_planes([rows_ref[1, p] for p in range(HN_PLANES)])
    out = h_ref[...] + (ys1 * route[:, 2:3] + ys2 * route[:, 3:4])
    y_ref[...] = _rms(out, nfin_ref[...])


def _combine(h, route, norm_final, rows, *, row0=0, y_prev=None):
    total = h.shape[0]
    tb = COMBINE_TB
    blk0 = row0 // tb
    args = [h, route, norm_final, rows]
    in_specs = [
        pl.BlockSpec((tb, D_MODEL), lambda i: (blk0 + i, 0)),
        pl.BlockSpec((tb, LANES), lambda i: (blk0 + i, 0)),
        pl.BlockSpec((1, D_MODEL), lambda i: (0, 0)),
        pl.BlockSpec((2, HN_PLANES, tb, LANES), lambda i: (0, 0, i, 0)),
    ]
    aliases = {}
    if y_prev is not None:
        aliases[len(args)] = 0
        args.append(y_prev)
        in_specs.append(pl.BlockSpec(memory_space=pl.ANY))
    return pl.pallas_call(
        _combine_kernel,
        grid=(rows.shape[2] // tb,),
        in_specs=in_specs,
        out_specs=pl.BlockSpec((tb, D_MODEL), lambda i: (blk0 + i, 0)),
        out_shape=jax.ShapeDtypeStruct((total, D_MODEL), F32),
        input_output_aliases=aliases,
        compiler_params=pltpu.CompilerParams(
            dimension_semantics=("arbitrary",), vmem_limit_bytes=VMEM_LIMIT),
        name="moe_combine",
    )(*args)


def kernel(x_prompt, x_sample, state_pool, state_gla, meta_tokens, norm_mix, w_in, w_gate_up, b_gate, w_pool, pool_scale, gla_norm, w_out, norm_ffn, w_router_group, b_router_group, w_router_expert, b_router_expert, w_expert_gate, w_expert_up, w_expert_down, norm_final):
    assert w_in.shape[0] == 1, "one encoder layer"
    batch, seq, _ = x_prompt.shape
    dec_batch, dec_seq, _ = x_sample.shape
    assert seq % MIX_TB == 0 and dec_seq == CHUNK and N_META <= CHUNK
    t_prompt = batch * seq
    t_sample = dec_batch * dec_seq
    t_all = t_prompt + t_sample
    assert t_prompt % COMBINE_TB == 0 and t_sample % COMBINE_TB == 0

    w_in0 = w_in[0]
    gpad = EXPERTS_PER_GROUP - N_GROUPS
    rpad = ROUTER_ROWS - EXPERTS_PER_GROUP - N_EXPERTS
    w_router = jnp.concatenate([
        w_router_group[0].T, jnp.zeros((gpad, D_MODEL), F32),
        w_router_expert[0].T, jnp.zeros((rpad, D_MODEL), F32)], axis=0)
    b_router = jnp.concatenate([
        b_router_group[0], jnp.zeros((gpad,), F32), b_router_expert[0], jnp.zeros((rpad,), F32)])
    zg = jnp.zeros((POOL_GROUP_DIM, POOL_GROUP_DIM), F32)
    w_pool_pairs = jnp.stack([
        jnp.block([[w_pool[0, 2 * i], zg], [zg, w_pool[0, 2 * i + 1]]]) for i in range(len(POOL_WINDOWS) // 2)])
    weights = (
        norm_mix[0][None, :],
        w_in0[:, :D_MAIN].astype(BF16),
        w_in0[:, D_MAIN:].T.astype(BF16),
        w_gate_up[0].astype(BF16),
        b_gate[0][None, :],
        w_pool_pairs.astype(BF16),
        pool_scale[0][None, :],
        gla_norm[0][None, :],
        w_out[0].astype(BF16),
        norm_ffn[0][None, :],
        w_router.astype(BF16),
        b_router[:, None],
    )

    zero_cnt = jnp.zeros((N_EXPERTS, 1), F32)
    x_meta = jnp.pad(meta_tokens.astype(F32), ((CHUNK - N_META, 0), (0, 0)))
    meta = _mixer(x_meta, jnp.zeros((1, POOL_ROWS, D_POOL), F32),
                  jnp.zeros((1, GLA_HEADS, GLA_DK, GLA_DV), F32), zero_cnt,
                  weights, batch=1, seq=CHUNK, tb=CHUNK, lead_pad=CHUNK - N_META)
    h_m, hn_m, route_m, rt_m, pool_m, st_m, cnt_m = meta
    del h_m, hn_m, route_m, rt_m, cnt_m
    h_p, hn_p, route_p, rt_p, pool_p, st_p, cnt_p = _mixer(
        x_prompt.reshape(t_prompt, D_MODEL), pool_m, st_m, zero_cnt, weights,
        batch=batch, seq=seq, tb=MIX_TB, lead_pad=0)
    pool_s0 = jnp.pad(state_pool[0], ((0, 0), (POOL_ROWS - POOL_PAD, 0), (0, 0)))
    h_s, hn_s, route_s, rt_s, pool_s, st_s, cnt_s = _mixer(
        x_sample.reshape(t_sample, D_MODEL), pool_s0, state_gla[0].astype(F32),
        cnt_p, weights, batch=dec_batch, seq=dec_seq, tb=SAMPLE_TB, lead_pad=0)

    counts = cnt_s[:, 0].astype(I32)
    padded = (counts + MOE_BLK - 1) // MOE_BLK * MOE_BLK
    ends = jnp.cumsum(padded)
    pstart = ends - padded
    n_blocks = (2 * t_all + N_EXPERTS * (MOE_BLK - 1) + MOE_BLK - 1) // MOE_BLK
    nb = (ends[-1] // MOE_BLK).astype(I32)
    blk_ids = jnp.minimum(jnp.arange(n_blocks, dtype=I32), nb - 1)
    block_e = jnp.sum((ends[None, :] <= (blk_ids * MOE_BLK)[:, None]).astype(I32), axis=1)
    block_e = jnp.minimum(block_e, N_EXPERTS - 1)
    owner = block_e[:, None] == jnp.arange(N_EXPERTS, dtype=I32)
    row_end = jnp.sum(jnp.where(owner, pstart + counts, 0), axis=1)
    block_valid = jnp.clip(row_end - blk_ids * MOE_BLK, 0, MOE_BLK)

    def dest_rows(rt):
        rt = rt.transpose(1, 0, 2).reshape(ROUTE_ROWS, -1)
        onehot = rt[0:2].astype(I32)[..., None] == jnp.arange(N_EXPERTS, dtype=I32)
        return jnp.sum(jnp.where(onehot, pstart, 0), axis=-1) + rt[4:6].astype(I32)

    dest = jnp.concatenate([dest_rows(rt_p), dest_rows(rt_s)], axis=1)
    dest0 = dest[0].reshape(t_all // SC_ROWS, SC_ROWS)
    dest1 = dest[1].reshape(t_all // SC_ROWS, SC_ROWS)
    xs = _dispatch(hn_p, hn_s, dest0, dest1, n_blocks * MOE_BLK)
    ys = _experts(block_e, block_valid.astype(I32), nb[None], xs,
                  w_expert_gate[0], w_expert_up[0], w_expert_down[0])
    nfin = norm_final[None, :]
    cp_chunks = t_prompt // SC_ROWS
    part = cp_chunks // COMBINE_PARTS
    y_prompt = None
    for i in range(COMBINE_PARTS):
        ch = slice(i * part, (i + 1) * part)
        rows_i = _gather(ys, dest0[ch], dest1[ch])
        y_prompt = _combine(h_p, route_p, nfin, rows_i, row0=i * part * SC_ROWS, y_prev=y_prompt)
    y_prompt = y_prompt.reshape(batch, seq, D_MODEL)
    rows_s = _gather(ys, dest0[cp_chunks:], dest1[cp_chunks:])
    y_sample = _combine(h_s, route_s, nfin, rows_s).reshape(dec_batch, dec_seq, D_MODEL)
    new_pool_prompt = pool_p[:, POOL_ROWS - POOL_PAD:][None]
    new_gla_prompt = st_p[None]
    new_pool_sample = pool_s[:, POOL_ROWS - POOL_PAD:][None]
    new_gla_sample = st_s[None]
    return (y_prompt, y_sample, new_pool_prompt, new_gla_prompt, new_pool_sample, new_gla_sample)
```

```python
import functools

import jax
import jax.numpy as jnp
from jax import lax
from jax.experimental import pallas as pl
from jax.experimental.pallas import tpu as pltpu
from jax.experimental.pallas import tpu_sc as plsc

F32 = jnp.float32
BF16 = jnp.bfloat16
U32 = jnp.uint32
I32 = jnp.int32

D_MODEL = 1024
N_META = 16
CHUNK = 64
EPS = 1e-6
D_POOL = 512
POOL_WINDOWS = (2, 4, 8, 16)
POOL_GROUP_DIM = 128
POOL_PAD = 15
POOL_ROWS = 16
GLA_HEADS = 4
GLA_DK = 64
GLA_DV = 128
D_QK = 256
D_V = 512
HEAD_GROUP = 2
GATE_RANK = 16
GATE_TAU = 16.0
D_MAIN = D_POOL + 2 * D_QK + 2 * D_V
N_GROUPS = 4
EXPERTS_PER_GROUP = 8
N_EXPERTS = 32
D_EXPERT = 512

LANES = 128
MXU_DIM = 256
HALF = D_MODEL // 2
HN_PLANES = HALF // LANES
ROUTE_ROWS = 8
ROUTER_ROWS = 64
MIX_TB = 512
SAMPLE_TB = 256
FRONT_TILE = 256
FRONT_PLAN = (3, 0, 1, 2, 2, 0)
MOE_BLK = 768
COMBINE_TB = 1024
COMBINE_PARTS = 4
VMEM_LIMIT = 48 * 1024 * 1024
SC_CORES = 2
SC_SUBCORES = 16
SC_WORKERS = SC_CORES * SC_SUBCORES
SC_ROWS = 128


def _rms(x, g):
    return x * lax.rsqrt(jnp.mean(x * x, axis=-1, keepdims=True) + EPS) * g


def _dot(a, b):
    return jnp.dot(a, b, preferred_element_type=F32)


def _dot_nt(a, b):
    return lax.dot_general(a, b, (((1,), (1,)), ((), ())), preferred_element_type=F32)


def _pack_planes(x, ref):
    xb = x.astype(BF16)
    lo = lax.bitcast_convert_type(xb[:, :HALF].astype(F32), U32) >> 16
    hi = lax.bitcast_convert_type(xb[:, HALF:].astype(F32), U32) & jnp.uint32(0xFFFF0000)
    packed = lax.bitcast_convert_type(lo | hi, I32)
    for p in range(HN_PLANES):
        ref[p] = packed[:, p * LANES:(p + 1) * LANES]


def _unpack_planes(planes):
    words = [lax.bitcast_convert_type(p, U32) for p in planes]
    los = [lax.bitcast_convert_type(w << 16, F32) for w in words]
    his = [lax.bitcast_convert_type(w & jnp.uint32(0xFFFF0000), F32) for w in words]
    return jnp.concatenate(los + his, axis=-1)


def _dot_tn(a, b):
    return lax.dot_general(a, b, (((0,), (0,)), ((), ())), preferred_element_type=F32)


def _mixer_kernel(x_ref, pool0_ref, st0_ref, cnt0_ref, tril_ref, sup_ref,
                  nmix_ref, wmain_ref, wz_ref, wgu_ref, bgate_ref, wpool_ref, pscale_ref,
                  gnorm_ref, wout_ref, nffn_ref, wr_ref, br_ref, eg_ref, eu_ref, ed_ref,
                  h_ref, hn_ref, route_ref, route_t_ref, pool_out_ref, st_out_ref, cnt_out_ref,
                  eg_out_ref, eu_out_ref, ed_out_ref,
                  ext_ref, st_ref, kbd_ref, vbd_ref, sbd_ref, o_ref, cnt_ref,
                  xs_ref, proj_ref, z_ref, *, tb, nj, lead_pad, chained):
    s = pl.program_id(0)
    back = jnp.maximum(s - 1, 0)
    j = lax.rem(back, nj)
    n_chunks = tb // CHUNK
    wr_slot = lax.rem(s, 2)
    rd_slot = 1 - wr_slot

    @pl.when(s == 0)
    def _():
        kbd_ref[...] = jnp.zeros_like(kbd_ref)
        vbd_ref[...] = jnp.zeros_like(vbd_ref)
        sbd_ref[...] = jnp.zeros_like(sbd_ref)
        xs_ref[1] = jnp.zeros((tb, D_MODEL), F32)
        proj_ref[1] = jnp.zeros((tb, D_MAIN), F32)
        z_ref[1] = jnp.zeros((GATE_RANK, tb), F32)

    @pl.when(s <= 1)
    def _():
        cnt_ref[...] = cnt0_ref[...]

    def put_state(c, hh, st):
        gg, hp = divmod(hh, HEAD_GROUP)
        sbd_ref[c, gg, hp * GLA_DK:(hp + 1) * GLA_DK, hp * GLA_DV:(hp + 1) * GLA_DV] = st.astype(BF16)

    if chained:
        @pl.when(j == 0)
        def _():
            ext_ref[0:POOL_ROWS, :] = pool0_ref[0]
            st_ref[...] = st0_ref[0]

        for hh in range(GLA_HEADS):
            put_state(0, hh, st_ref[hh])
    else:
        for c in range(n_chunks):
            for hh in range(GLA_HEADS):
                put_state(c, hh, st0_ref[c, hh])

    x_new = x_ref[...]
    xn = _rms(x_new, nmix_ref[...]).astype(BF16)
    xs_ref[wr_slot] = x_new

    tiles_done = [0]

    def front_tiles(stage):
        for t in range(tiles_done[0], tiles_done[0] + FRONT_PLAN[stage]):
            cols = slice(t * FRONT_TILE, (t + 1) * FRONT_TILE)
            proj_ref[wr_slot, :, cols] = _dot(xn, wmain_ref[:, cols])
        tiles_done[0] += FRONT_PLAN[stage]

    front_tiles(0)
    z_ref[wr_slot] = _dot_nt(wz_ref[...], xn)

    x = xs_ref[rd_slot]
    z = z_ref[rd_slot]
    u = proj_ref[rd_slot, :, 0:D_POOL]
    q = proj_ref[rd_slot, :, D_POOL:D_POOL + D_QK]
    k = proj_ref[rd_slot, :, D_POOL + D_QK:D_POOL + 2 * D_QK]
    v = proj_ref[rd_slot, :, D_POOL + 2 * D_QK:D_POOL + 2 * D_QK + D_V]
    r = proj_ref[rd_slot, :, D_POOL + 2 * D_QK + D_V:D_MAIN]

    row = lax.broadcasted_iota(I32, (tb, 1), 0)

    pseg = POOL_ROWS + CHUNK
    if chained:
        ext_ref[POOL_ROWS:POOL_ROWS + tb, :] = u
        ext = ext_ref[...]
    else:
        ext = jnp.concatenate(
            [blk for c in range(n_chunks) for blk in (pool0_ref[c], u[c * CHUNK:(c + 1) * CHUNK])], axis=0)
    pooled = []
    for g, w in enumerate(POOL_WINDOWS):
        sl = slice(g * POOL_GROUP_DIM, (g + 1) * POOL_GROUP_DIM)
        acc = ext[:, sl]
        for d in range(g + 1):
            acc = acc + pltpu.roll(acc, 1 << d, axis=0)
        if chained:
            win = acc[POOL_ROWS:, :]
        else:
            win = jnp.concatenate([acc[c * pseg + POOL_ROWS:(c + 1) * pseg] for c in range(n_chunks)], axis=0)
        if lead_pad:
            cnt = jnp.clip(row - lead_pad + 1, 1, w).astype(F32)
            pooled.append(win / cnt - u[:, sl])
        else:
            pooled.append(win * (1.0 / w) - u[:, sl])
    pys = [_dot(jnp.concatenate(pooled[2 * i:2 * i + 2], axis=-1).astype(BF16), wpool_ref[i])
           for i in range(len(POOL_WINDOWS) // 2)]
    pool_y = jnp.concatenate(pys, axis=-1) * pscale_ref[...]
    if chained:
        ext_ref[0:POOL_ROWS, :] = ext_ref[tb:tb + POOL_ROWS, :]
    else:
        for c in range(n_chunks):
            pool_out_ref[c] = u[(c + 1) * CHUNK - POOL_ROWS:(c + 1) * CHUNK]

    gpre = _dot_tn(z.astype(BF16), wgu_ref[...]) + bgate_ref[...]
    log_a = (jnp.minimum(gpre, 0.0) - jnp.log(1.0 + jnp.exp(-jnp.abs(gpre)))) * (1.0 / GATE_TAU)
    if lead_pad:
        log_a = jnp.where(row >= lead_pad, log_a, 0.0)
    a_hi = log_a.astype(BF16)
    a_lo = (log_a - a_hi.astype(F32)).astype(BF16)
    tril = tril_ref[...]
    seg = tril.shape[0]
    bcum = jnp.concatenate(
        [_dot(tril, a_hi[r0:r0 + seg]) + _dot(tril, a_lo[r0:r0 + seg]) for r0 in range(0, tb, seg)], axis=0)
    front_tiles(1)
    eb = jnp.exp(bcum)
    qi = q * (GLA_DK ** -0.5) * eb
    ki = k * jnp.exp(-bcum)

    rr = lax.broadcasted_iota(I32, (CHUNK, HEAD_GROUP * CHUNK), 0)
    cc = lax.broadcasted_iota(I32, (CHUNK, HEAD_GROUP * CHUNK), 1)
    causal = (cc % CHUNK) <= rr

    lasts = [eb[(c + 1) * CHUNK - 1:(c + 1) * CHUNK, :] for c in range(n_chunks)]
    dcol = jnp.concatenate(lasts + [jnp.zeros((LANES - n_chunks, D_QK), F32)], axis=0).T

    n_grp = GLA_HEADS // HEAD_GROUP
    chunk_rows = [slice(c * CHUNK, (c + 1) * CHUNK) for c in range(n_chunks)]
    grp_k = [slice(g * HEAD_GROUP * GLA_DK, (g + 1) * HEAD_GROUP * GLA_DK) for g in range(n_grp)]
    grp_v = [slice(g * HEAD_GROUP * GLA_DV, (g + 1) * HEAD_GROUP * GLA_DV) for g in range(n_grp)]
    qi_b = qi.astype(BF16)
    ki_b = ki.astype(BF16)
    v_b = v.astype(BF16)

    scores = {}
    for c in range(n_chunks):
        for hh in range(GLA_HEADS):
            gg, hp = divmod(hh, HEAD_GROUP)
            kbd_ref[c, gg, hp * CHUNK:(hp + 1) * CHUNK, hp * GLA_DK:(hp + 1) * GLA_DK] = (
                ki_b[chunk_rows[c], hh * GLA_DK:(hh + 1) * GLA_DK])
            vbd_ref[c, gg, hp * CHUNK:(hp + 1) * CHUNK, hp * GLA_DV:(hp + 1) * GLA_DV] = (
                v_b[chunk_rows[c], hh * GLA_DV:(hh + 1) * GLA_DV])
        for gg in range(n_grp):
            scores[c, gg] = _dot_nt(qi_b[chunk_rows[c], grp_k[gg]], kbd_ref[c, gg])

    kvs = {}
    for c in range(n_chunks):
        for hh in range(GLA_HEADS):
            kvs[c, hh] = _dot_tn(ki_b[chunk_rows[c], hh * GLA_DK:(hh + 1) * GLA_DK],
                                 v_b[chunk_rows[c], hh * GLA_DV:(hh + 1) * GLA_DV])
    front_tiles(2)

    for hh in range(GLA_HEADS):
        st = st_ref[hh] if chained else None
        for c in range(n_chunks):
            s_old = st if chained else st0_ref[c, hh]
            s_new = (s_old + kvs[c, hh]) * dcol[hh * GLA_DK:(hh + 1) * GLA_DK, c:c + 1]
            if not chained:
                st_out_ref[c, hh] = s_new
            else:
                st = s_new
                if c + 1 < n_chunks:
                    put_state(c + 1, hh, s_new)
        if chained:
            st_ref[hh] = st

    for c in range(n_chunks):
        for gg in range(n_grp):
            p = jnp.where(causal, scores[c, gg], 0.0).astype(BF16)
            o_ref[chunk_rows[c], grp_v[gg]] = (
                _dot(p, vbd_ref[c, gg]) + _dot(qi_b[chunk_rows[c], grp_k[gg]], sbd_ref[c, gg]))

    o = o_ref[...]
    ons = []
    for hh in range(GLA_HEADS):
        oh = o[:, hh * GLA_DV:(hh + 1) * GLA_DV]
        ons.append(oh * lax.rsqrt(jnp.mean(oh * oh, axis=-1, keepdims=True) + EPS))
    og = jnp.concatenate(ons, axis=-1) * gnorm_ref[...] * (r * jax.nn.sigmoid(r))
    mix = _dot(jnp.concatenate([pool_y, og], axis=-1).astype(BF16), wout_ref[...])
    front_tiles(3)

    h = x + mix
    h_ref[...] = h
    hn = _rms(h, nffn_ref[...]).astype(BF16)
    _pack_planes(hn, hn_ref)

    logits = _dot_nt(wr_ref[...], hn) + br_ref[...]
    front_tiles(4)
    sub = lax.broadcasted_iota(I32, (EXPERTS_PER_GROUP, tb), 0).astype(F32)
    neg = jnp.float32(-jnp.inf)
    big = jnp.float32(EXPERTS_PER_GROUP)
    tile0 = logits[0:EXPERTS_PER_GROUP]
    is_g = sub < N_GROUPS
    gmax = jnp.max(jnp.where(is_g, tile0, neg), axis=0, keepdims=True)
    gsum = jnp.sum(jnp.where(is_g, jnp.exp(tile0 - gmax), 0.0), axis=0, keepdims=True)
    p_g = 1.0 / gsum
    gidx = jnp.min(jnp.where(is_g & (tile0 == gmax), sub, big), axis=0, keepdims=True)
    el = logits[N_GROUPS * EXPERTS_PER_GROUP:(N_GROUPS + 1) * EXPERTS_PER_GROUP]
    for g in range(N_GROUPS - 2, -1, -1):
        el = jnp.where(gidx == g, logits[(g + 1) * EXPERTS_PER_GROUP:(g + 2) * EXPERTS_PER_GROUP], el)
    m1 = jnp.max(el, axis=0, keepdims=True)
    i1 = jnp.min(jnp.where(el == m1, sub, big), axis=0, keepdims=True)
    rest = sub != i1
    m2 = jnp.max(jnp.where(rest, el, neg), axis=0, keepdims=True)
    i2 = jnp.min(jnp.where(rest & (el == m2), sub, big), axis=0, keepdims=True)
    t2 = jnp.exp(m2 - m1)
    den = 1.0 + t2
    g1 = p_g / den
    g2 = p_g * t2 / den
    e1 = gidx * EXPERTS_PER_GROUP + i1
    e2 = gidx * EXPERTS_PER_GROUP + i2

    eid = lax.broadcasted_iota(I32, (N_EXPERTS, tb), 0).astype(F32)
    oh1 = eid == e1
    oh2 = eid == e2
    both = jnp.where(oh1 | oh2, 1.0, 0.0)
    cnt = cnt_ref[...]
    before = _dot(both.astype(BF16), sup_ref[...]) + cnt
    front_tiles(5)
    assert tiles_done[0] * FRONT_TILE == D_MAIN
    pos1 = jnp.sum(jnp.where(oh1, before, 0.0), axis=0, keepdims=True)
    pos2 = jnp.sum(jnp.where(oh2, before, 0.0), axis=0, keepdims=True)
    cnt_new = cnt + jnp.sum(both, axis=1, keepdims=True)
    cnt_ref[...] = cnt_new
    cnt_out_ref[...] = cnt_new

    zero = jnp.zeros_like(e1)
    route_t = jnp.concatenate([e1, e2, g1, g2, pos1, pos2, zero, zero], axis=0)
    route_t_ref[0] = route_t
    route_ref[...] = jnp.concatenate([route_t, jnp.zeros((LANES - ROUTE_ROWS, tb), F32)], axis=0).T

    eg_out_ref[...] = eg_ref[...].astype(BF16)
    eu_out_ref[...] = eu_ref[...].astype(BF16)
    ed_out_ref[...] = ed_ref[...].astype(BF16)

    if chained:
        @pl.when(j == nj - 1)
        def _():
            pool_out_ref[0] = ext_ref[0:POOL_ROWS, :]
            st_out_ref[0] = st_ref[...]


def _mixer(x2d, pool0, st0, cnt0, weights, *, batch, seq, tb, lead_pad, experts=None):
    chained = tb <= seq
    total_rows = batch * seq
    n_blk = total_rows // tb
    nj = seq // tb if chained else 1
    per_blk = 1 if chained else tb // seq
    assert chained or (seq == CHUNK and batch % per_blk == 0)
    shared = pool0.shape[0] == 1
    front = lambda s: jnp.minimum(s, n_blk - 1)
    back = lambda s: jnp.maximum(s - 1, 0)
    stream = lambda s: back(s) // nj
    st_idx = (lambda s: (0, 0, 0)) if shared else (lambda s: (stream(s), 0, 0))
    gla_idx = (lambda s: (0, 0, 0, 0)) if shared else (lambda s: (stream(s), 0, 0, 0))
    const2 = lambda s: (0, 0)
    tok_out = lambda s: (back(s), 0)

    seg = min(tb, MXU_DIM)
    ii = jnp.arange(seg)
    tril = ((ii[:, None] >= ii[None, :]) & (ii[:, None] // CHUNK == ii[None, :] // CHUNK)).astype(BF16)
    ii = jnp.arange(tb)
    sup = (ii[:, None] < ii[None, :]).astype(BF16)

    in_specs = [
        pl.BlockSpec((tb, D_MODEL), lambda s: (front(s), 0)),
        pl.BlockSpec((per_blk, POOL_ROWS, D_POOL), st_idx),
        pl.BlockSpec((per_blk, GLA_HEADS, GLA_DK, GLA_DV), gla_idx),
        pl.BlockSpec((N_EXPERTS, 1), const2),
        pl.BlockSpec((seg, seg), const2),
        pl.BlockSpec((tb, tb), const2),
    ]
    for wgt in weights:
        in_specs.append(pl.BlockSpec(wgt.shape, (lambda s, n=wgt.ndim: (0,) * n)))
    if experts is None:
        experts = (jnp.zeros((1, 8, LANES), F32),) * 3
        per = 1
        cast_idx = lambda s: (0, 0, 0)
    else:
        per = n_blk // N_EXPERTS
        assert n_blk == per * N_EXPERTS and all(w.shape[1] % (8 * per) == 0 for w in experts)
        cast_idx = lambda s: (front(s) // per, front(s) % per, 0)
    cast_specs = [pl.BlockSpec((1, w.shape[1] // per, w.shape[2]), cast_idx) for w in experts]
    in_specs += cast_specs
    args = [x2d, pool0, st0, cnt0, tril, sup, *weights, *experts]

    out_shape = [
        jax.ShapeDtypeStruct((total_rows, D_MODEL), F32),
        jax.ShapeDtypeStruct((HN_PLANES, total_rows, LANES), I32),
        jax.ShapeDtypeStruct((total_rows, LANES), F32),
        jax.ShapeDtypeStruct((n_blk, ROUTE_ROWS, tb), F32),
        jax.ShapeDtypeStruct((batch, POOL_ROWS, D_POOL), F32),
        jax.ShapeDtypeStruct((batch, GLA_HEADS, GLA_DK, GLA_DV), F32),
        jax.ShapeDtypeStruct((N_EXPERTS, 1), F32),
        *[jax.ShapeDtypeStruct(w.shape, BF16) for w in experts],
    ]
    out_specs = [
        pl.BlockSpec((tb, D_MODEL), tok_out),
        pl.BlockSpec((HN_PLANES, tb, LANES), lambda s: (0, back(s), 0)),
        pl.BlockSpec((tb, LANES), tok_out),
        pl.BlockSpec((1, ROUTE_ROWS, tb), lambda s: (back(s), 0, 0)),
        pl.BlockSpec((per_blk, POOL_ROWS, D_POOL), lambda s: (stream(s), 0, 0)),
        pl.BlockSpec((per_blk, GLA_HEADS, GLA_DK, GLA_DV), lambda s: (stream(s), 0, 0, 0)),
        pl.BlockSpec((N_EXPERTS, 1), const2),
        *cast_specs,
    ]
    n_grp = GLA_HEADS // HEAD_GROUP
    scratch = [
        pltpu.VMEM((POOL_ROWS + tb, D_POOL), F32),
        pltpu.VMEM((GLA_HEADS, GLA_DK, GLA_DV), F32),
        pltpu.VMEM((tb // CHUNK, n_grp, HEAD_GROUP * CHUNK, HEAD_GROUP * GLA_DK), BF16),
        pltpu.VMEM((tb // CHUNK, n_grp, HEAD_GROUP * CHUNK, HEAD_GROUP * GLA_DV), BF16),
        pltpu.VMEM((tb // CHUNK, n_grp, HEAD_GROUP * GLA_DK, HEAD_GROUP * GLA_DV), BF16),
        pltpu.VMEM((tb, D_V), F32),
        pltpu.VMEM((N_EXPERTS, 1), F32),
        pltpu.VMEM((2, tb, D_MODEL), F32),
        pltpu.VMEM((2, tb, D_MAIN), F32),
        pltpu.VMEM((2, GATE_RANK, tb), F32),
    ]
    return pl.pallas_call(
        functools.partial(_mixer_kernel, tb=tb, nj=nj, lead_pad=lead_pad, chained=chained),
        grid=(n_blk + 1,),
        in_specs=in_specs,
        out_specs=out_specs,
        out_shape=out_shape,
        scratch_shapes=scratch,
        compiler_params=pltpu.CompilerParams(
            dimension_semantics=("arbitrary",), vmem_limit_bytes=VMEM_LIMIT),
        name=f"mixer_tb{tb}_pad{lead_pad}",
    )(*args)


def _sc_mesh():
    return plsc.VectorSubcoreMesh(core_axis_name="c", subcore_axis_name="s",
                                  num_cores=SC_CORES, num_subcores=SC_SUBCORES)


def _sc_worker():
    return lax.axis_index("s") * SC_CORES + lax.axis_index("c")


def _plane_rows(dest, planes, rows_per_plane):
    offs = (jnp.arange(planes, dtype=I32) * rows_per_plane)[None, :, None]
    return dest[:, None, :] + offs


def _dispatch(hn_p, hn_s, dest0, dest1, n_rows):
    planes, t_p, _ = hn_p.shape
    t_s = hn_s.shape[1]
    n_cp = t_p // SC_ROWS
    cp = n_cp // SC_WORKERS
    n_cs = t_s // SC_ROWS
    assert t_p == cp * SC_ROWS * SC_WORKERS and t_s == n_cs * SC_ROWS and n_cs <= SC_WORKERS
    idx0 = _plane_rows(dest0, planes, n_rows)
    idx1 = _plane_rows(dest1, planes, n_rows)

    def body(hnp_hbm, hns_hbm, d0_hbm, d1_hbm, xs_hbm, rows_v, i0_v, i1_v, is0_v, is1_v, sem_in, sem_out):
        wid = _sc_worker()
        pltpu.sync_copy(d0_hbm.at[pl.ds(wid * cp, cp)], i0_v)
        pltpu.sync_copy(d1_hbm.at[pl.ds(wid * cp, cp)], i1_v)

        def move(src_hbm, src_rows, row0, i0, i1, c):
            loads = [pltpu.async_copy(src_hbm.at[pl.ds(p * src_rows + row0, SC_ROWS)], rows_v.at[p], sem_in)
                     for p in range(planes)]
            for cpy in loads:
                cpy.wait()
            stores = []
            for p in range(planes):
                stores.append(pltpu.async_copy(rows_v.at[p], xs_hbm.at[i0.at[c, p]], sem_out))
                stores.append(pltpu.async_copy(rows_v.at[p], xs_hbm.at[i1.at[c, p]], sem_out))
            for cpy in stores:
                cpy.wait()

        @pl.loop(0, cp)
        def _(c):
            move(hnp_hbm, t_p, (wid * cp + c) * SC_ROWS, i0_v, i1_v, c)

        @pl.when(wid < n_cs)
        def _():
            pltpu.sync_copy(d0_hbm.at[pl.ds(n_cp + wid, 1)], is0_v)
            pltpu.sync_copy(d1_hbm.at[pl.ds(n_cp + wid, 1)], is1_v)
            move(hns_hbm, t_s, wid * SC_ROWS, is0_v, is1_v, 0)

    xs = pl.kernel(
        body,
        out_type=jax.ShapeDtypeStruct((planes * n_rows, LANES), I32),
        mesh=_sc_mesh(),
        scratch_types=[
            pltpu.VMEM((planes, SC_ROWS, LANES), I32),
            pltpu.VMEM((cp, planes, SC_ROWS), I32),
            pltpu.VMEM((cp, planes, SC_ROWS), I32),
            pltpu.VMEM((1, planes, SC_ROWS), I32),
            pltpu.VMEM((1, planes, SC_ROWS), I32),
            pltpu.SemaphoreType.DMA,
            pltpu.SemaphoreType.DMA,
        ],
        name="moe_dispatch_sc",
    )(hn_p.reshape(planes * t_p, LANES), hn_s.reshape(planes * t_s, LANES), idx0, idx1)
    return xs.reshape(planes, n_rows, LANES)


def _gather(ys, dest0, dest1):
    planes, n_rows, _ = ys.shape
    n_chunks = dest0.shape[0]
    n_tok = n_chunks * SC_ROWS
    cpw = max(n_chunks // SC_WORKERS, 1)
    assert n_chunks <= SC_WORKERS or n_chunks == cpw * SC_WORKERS
    idx = (_plane_rows(dest0, planes, n_rows), _plane_rows(dest1, planes, n_rows))

    def body(ys_hbm, d0_hbm, d1_hbm, out_hbm, rows_v, i0_v, i1_v, sem_in, sem_out):
        wid = _sc_worker()

        def work():
            pltpu.sync_copy(d0_hbm.at[pl.ds(wid * cpw, cpw)], i0_v)
            pltpu.sync_copy(d1_hbm.at[pl.ds(wid * cpw, cpw)], i1_v)

            @pl.loop(0, cpw)
            def _(c):
                row0 = (wid * cpw + c) * SC_ROWS
                for j, i_v in enumerate((i0_v, i1_v)):
                    loads = [pltpu.async_copy(ys_hbm.at[i_v.at[c, p]], rows_v.at[p], sem_in)
                             for p in range(planes)]
                    for cpy in loads:
                        cpy.wait()
                    stores = [
                        pltpu.async_copy(
                            rows_v.at[p], out_hbm.at[pl.ds((j * planes + p) * n_tok + row0, SC_ROWS)], sem_out)
                        for p in range(planes)]
                    for cpy in stores:
                        cpy.wait()

        if n_chunks < SC_WORKERS:
            pl.when(wid < n_chunks)(work)
        else:
            work()

    out = pl.kernel(
        body,
        out_type=jax.ShapeDtypeStruct((2 * planes * n_tok, LANES), ys.dtype),
        mesh=_sc_mesh(),
        scratch_types=[
            pltpu.VMEM((planes, SC_ROWS, LANES), ys.dtype),
            pltpu.VMEM((cpw, planes, SC_ROWS), I32),
            pltpu.VMEM((cpw, planes, SC_ROWS), I32),
            pltpu.SemaphoreType.DMA,
            pltpu.SemaphoreType.DMA,
        ],
        name="moe_gather_sc",
    )(ys.reshape(planes * n_rows, LANES), *idx)
    return out.reshape(2, planes, n_tok, LANES)


def _expert_kernel(be_ref, bv_ref, nb_ref, xs_ref, wg_ref, wu_ref, wd_ref, ys_ref):
    del be_ref
    i = pl.program_id(0)
    live = i < nb_ref[0]

    @pl.when(live)
    def _():
        valid = lax.broadcasted_iota(I32, (MOE_BLK, LANES), 0) < bv_ref[i]
        xb = _unpack_planes([jnp.where(valid, xs_ref[p], 0) for p in range(HN_PLANES)]).astype(BF16)
        gate = _dot(xb, wg_ref[0])
        hmid = gate * jax.nn.sigmoid(gate) * _dot(xb, wu_ref[0])
        _pack_planes(_dot(hmid.astype(BF16), wd_ref[0]), ys_ref)

    @pl.when(jnp.logical_not(live))
    def _():
        ys_ref[...] = jnp.zeros_like(ys_ref)


def _experts(block_e, block_valid, nb, xs, w_eg, w_eu, w_ed):
    n_rows = xs.shape[1]
    n_blocks = n_rows // MOE_BLK
    row_idx = lambda i, be, bv, nb: (0, jnp.minimum(i, nb[0] - 1), 0)
    w_idx = lambda i, be, bv, nb: (be[i], 0, 0)
    grid_spec = pltpu.PrefetchScalarGridSpec(
        num_scalar_prefetch=3,
        grid=(n_blocks,),
        in_specs=[
            pl.BlockSpec((HN_PLANES, MOE_BLK, LANES), row_idx),
            pl.BlockSpec((1, D_MODEL, D_EXPERT), w_idx),
            pl.BlockSpec((1, D_MODEL, D_EXPERT), w_idx),
            pl.BlockSpec((1, D_EXPERT, D_MODEL), w_idx),
        ],
        out_specs=pl.BlockSpec((HN_PLANES, MOE_BLK, LANES), lambda i, be, bv, nb: (0, i, 0)),
    )
    return pl.pallas_call(
        _expert_kernel,
        grid_spec=grid_spec,
        out_shape=jax.ShapeDtypeStruct((HN_PLANES, n_rows, LANES), I32),
        compiler_params=pltpu.CompilerParams(
            dimension_semantics=("arbitrary",), vmem_limit_bytes=VMEM_LIMIT),
        name="moe_experts",
    )(block_e, block_valid, nb, xs, w_eg, w_eu, w_ed)


def _combine_kernel(h_ref, route_ref, nfin_ref, rows_ref, *rest):
    y_ref = rest[-1]
    route = route_ref[...]
    ys1 = _unpack_planes([rows_ref[0, p] for p in range(HN_PLANES)])
    ys2 = _unpack_planes([rows_ref[1, p] for p in range(HN_PLANES)])
    out = h_ref[...] + (ys1 * route[:, 2:3] + ys2 * route[:, 3:4])
    y_ref[...] = _rms(out, nfin_ref[...])


def _combine(h, route, norm_final, rows, *, row0=0, y_prev=None):
    total = h.shape[0]
    tb = COMBINE_TB
    blk0 = row0 // tb
    args = [h, route, norm_final, rows]
    in_specs = [
        pl.BlockSpec((tb, D_MODEL), lambda i: (blk0 + i, 0)),
        pl.BlockSpec((tb, LANES), lambda i: (blk0 + i, 0)),
        pl.BlockSpec((1, D_MODEL), lambda i: (0, 0)),
        pl.BlockSpec((2, HN_PLANES, tb, LANES), lambda i: (0, 0, i, 0)),
    ]
    aliases = {}
    if y_prev is not None:
        aliases[len(args)] = 0
        args.append(y_prev)
        in_specs.append(pl.BlockSpec(memory_space=pl.ANY))
    return pl.pallas_call(
        _combine_kernel,
        grid=(rows.shape[2] // tb,),
        in_specs=in_specs,
        out_specs=pl.BlockSpec((tb, D_MODEL), lambda i: (blk0 + i, 0)),
        out_shape=jax.ShapeDtypeStruct((total, D_MODEL), F32),
        input_output_aliases=aliases,
        compiler_params=pltpu.CompilerParams(
            dimension_semantics=("arbitrary",), vmem_limit_bytes=VMEM_LIMIT),
        name="moe_combine",
    )(*args)


def kernel(x_prompt, x_sample, state_pool, state_gla, meta_tokens, norm_mix, w_in, w_gate_up, b_gate, w_pool, pool_scale, gla_norm, w_out, norm_ffn, w_router_group, b_router_group, w_router_expert, b_router_expert, w_expert_gate, w_expert_up, w_expert_down, norm_final):
    assert w_in.shape[0] == 1, "one encoder layer"
    batch, seq, _ = x_prompt.shape
    dec_batch, dec_seq, _ = x_sample.shape
    assert seq % MIX_TB == 0 and dec_seq == CHUNK and N_META <= CHUNK
    t_prompt = batch * seq
    t_sample = dec_batch * dec_seq
    t_all = t_prompt + t_sample
    assert t_prompt % COMBINE_TB == 0 and t_sample % COMBINE_TB == 0

    w_in0 = w_in[0]
    gpad = EXPERTS_PER_GROUP - N_GROUPS
    rpad = ROUTER_ROWS - EXPERTS_PER_GROUP - N_EXPERTS
    w_router = jnp.concatenate([
        w_router_group[0].T, jnp.zeros((gpad, D_MODEL), F32),
        w_router_expert[0].T, jnp.zeros((rpad, D_MODEL), F32)], axis=0)
    b_router = jnp.concatenate([
        b_router_group[0], jnp.zeros((gpad,), F32), b_router_expert[0], jnp.zeros((rpad,), F32)])
    zg = jnp.zeros((POOL_GROUP_DIM, POOL_GROUP_DIM), F32)
    w_pool_pairs = jnp.stack([
        jnp.block([[w_pool[0, 2 * i], zg], [zg, w_pool[0, 2 * i + 1]]]) for i in range(len(POOL_WINDOWS) // 2)])
    weights = (
        norm_mix[0][None, :],
        w_in0[:, :D_MAIN].astype(BF16),
        w_in0[:, D_MAIN:].T.astype(BF16),
        w_gate_up[0].astype(BF16),
        b_gate[0][None, :],
        w_pool_pairs.astype(BF16),
        pool_scale[0][None, :],
        gla_norm[0][None, :],
        w_out[0].astype(BF16),
        norm_ffn[0][None, :],
        w_router.astype(BF16),
        b_router[:, None],
    )

    zero_cnt = jnp.zeros((N_EXPERTS, 1), F32)
    x_meta = jnp.pad(meta_tokens.astype(F32), ((CHUNK - N_META, 0), (0, 0)))
    meta = _mixer(x_meta, jnp.zeros((1, POOL_ROWS, D_POOL), F32),
                  jnp.zeros((1, GLA_HEADS, GLA_DK, GLA_DV), F32), zero_cnt,
                  weights, batch=1, seq=CHUNK, tb=CHUNK, lead_pad=CHUNK - N_META)
    pool_m, st_m = meta[4], meta[5]
    h_p, hn_p, route_p, rt_p, pool_p, st_p, cnt_p, w_eg, w_eu, w_ed = _mixer(
        x_prompt.reshape(t_prompt, D_MODEL), pool_m, st_m, zero_cnt, weights,
        batch=batch, seq=seq, tb=MIX_TB, lead_pad=0,
        experts=(w_expert_gate[0], w_expert_up[0], w_expert_down[0]))
    pool_s0 = jnp.pad(state_pool[0], ((0, 0), (POOL_ROWS - POOL_PAD, 0), (0, 0)))
    h_s, hn_s, route_s, rt_s, pool_s, st_s, cnt_s = _mixer(
        x_sample.reshape(t_sample, D_MODEL), pool_s0, state_gla[0].astype(F32),
        cnt_p, weights, batch=dec_batch, seq=dec_seq, tb=SAMPLE_TB, lead_pad=0)[:7]

    counts = cnt_s[:, 0].astype(I32)
    padded = (counts + MOE_BLK - 1) // MOE_BLK * MOE_BLK
    ends = jnp.cumsum(padded)
    pstart = ends - padded
    n_blocks = (2 * t_all + N_EXPERTS * (MOE_BLK - 1) + MOE_BLK - 1) // MOE_BLK
    nb = (ends[-1] // MOE_BLK).astype(I32)
    blk_ids = jnp.minimum(jnp.arange(n_blocks, dtype=I32), nb - 1)
    block_e = jnp.sum((ends[None, :] <= (blk_ids * MOE_BLK)[:, None]).astype(I32), axis=1)
    block_e = jnp.minimum(block_e, N_EXPERTS - 1)
    owner = block_e[:, None] == jnp.arange(N_EXPERTS, dtype=I32)
    row_end = jnp.sum(jnp.where(owner, pstart + counts, 0), axis=1)
    block_valid = jnp.clip(row_end - blk_ids * MOE_BLK, 0, MOE_BLK)

    def dest_rows(rt):
        rt = rt.transpose(1, 0, 2).reshape(ROUTE_ROWS, -1)
        onehot = rt[0:2].astype(I32)[..., None] == jnp.arange(N_EXPERTS, dtype=I32)
        return jnp.sum(jnp.where(onehot, pstart, 0), axis=-1) + rt[4:6].astype(I32)

    dest = jnp.concatenate([dest_rows(rt_p), dest_rows(rt_s)], axis=1)
    dest0 = dest[0].reshape(t_all // SC_ROWS, SC_ROWS)
    dest1 = dest[1].reshape(t_all // SC_ROWS, SC_ROWS)
    xs = _dispatch(hn_p, hn_s, dest0, dest1, n_blocks * MOE_BLK)
    ys = _experts(block_e, block_valid.astype(I32), nb[None], xs, w_eg, w_eu, w_ed)
    nfin = norm_final[None, :]
    cp_chunks = t_prompt // SC_ROWS
    part = cp_chunks // COMBINE_PARTS
    y_prompt = None
    for i in range(COMBINE_PARTS):
        ch = slice(i * part, (i + 1) * part)
        rows_i = _gather(ys, dest0[ch], dest1[ch])
        y_prompt = _combine(h_p, route_p, nfin, rows_i, row0=i * part * SC_ROWS, y_prev=y_prompt)
    y_prompt = y_prompt.reshape(batch, seq, D_MODEL)
    rows_s = _gather(ys, dest0[cp_chunks:], dest1[cp_chunks:])
    y_sample = _combine(h_s, route_s, nfin, rows_s).reshape(dec_batch, dec_seq, D_MODEL)
    new_pool_prompt = pool_p[:, POOL_ROWS - POOL_PAD:][None]
    new_gla_prompt = st_p[None]
    new_pool_sample = pool_s[:, POOL_ROWS - POOL_PAD:][None]
    new_gla_sample = st_s[None]
    return (y_prompt, y_sample, new_pool_prompt, new_gla_prompt, new_pool_sample, new_gla_sample)
```

```python
import functools

import jax
import jax.numpy as jnp
from jax import lax
from jax.experimental import pallas as pl
from jax.experimental.pallas import tpu as pltpu
from jax.experimental.pallas import tpu_sc as plsc

F32 = jnp.float32
BF16 = jnp.bfloat16
U32 = jnp.uint32
I32 = jnp.int32

D_MODEL = 1024
N_META = 16
CHUNK = 64
EPS = 1e-6
D_POOL = 512
POOL_WINDOWS = (2, 4, 8, 16)
POOL_GROUP_DIM = 128
POOL_PAD = 15
POOL_ROWS = 16
GLA_HEADS = 4
GLA_DK = 64
GLA_DV = 128
D_QK = 256
D_V = 512
HEAD_GROUP = 2
GATE_RANK = 16
GATE_TAU = 16.0
D_MAIN = D_POOL + 2 * D_QK + 2 * D_V
N_GROUPS = 4
EXPERTS_PER_GROUP = 8
N_EXPERTS = 32
D_EXPERT = 512

LANES = 128
MXU_DIM = 256
HALF = D_MODEL // 2
HN_PLANES = HALF // LANES
ROUTE_ROWS = 8
ROUTER_ROWS = 64
MIX_TB = 512
SAMPLE_TB = 256
FRONT_TILE = 256
FRONT_PLAN = (3, 0, 1, 2, 2, 0)
MOE_BLK = 768
COMBINE_TB = 1024
COMBINE_PARTS = 4
VMEM_LIMIT = 48 * 1024 * 1024
SC_CORES = 2
SC_SUBCORES = 16
SC_WORKERS = SC_CORES * SC_SUBCORES
SC_ROWS = 128


def _rms(x, g):
    return x * lax.rsqrt(jnp.mean(x * x, axis=-1, keepdims=True) + EPS) * g


def _dot(a, b):
    return jnp.dot(a, b, preferred_element_type=F32)


def _dot_nt(a, b):
    return lax.dot_general(a, b, (((1,), (1,)), ((), ())), preferred_element_type=F32)


def _pack_planes(x, ref):
    xb = x.astype(BF16)
    lo = lax.bitcast_convert_type(xb[:, :HALF].astype(F32), U32) >> 16
    hi = lax.bitcast_convert_type(xb[:, HALF:].astype(F32), U32) & jnp.uint32(0xFFFF0000)
    packed = lax.bitcast_convert_type(lo | hi, I32)
    for p in range(HN_PLANES):
        ref[p] = packed[:, p * LANES:(p + 1) * LANES]


def _unpack_planes(planes):
    words = [lax.bitcast_convert_type(p, U32) for p in planes]
    los = [lax.bitcast_convert_type(w << 16, F32) for w in words]
    his = [lax.bitcast_convert_type(w & jnp.uint32(0xFFFF0000), F32) for w in words]
    return jnp.concatenate(los + his, axis=-1)


def _dot_tn(a, b):
    return lax.dot_general(a, b, (((0,), (0,)), ((), ())), preferred_element_type=F32)


def _mixer_kernel(x_ref, pool0_ref, st0_ref, cnt0_ref, tril_ref, sup_ref,
                  nmix_ref, win_ref, wgu_ref, bgate_ref, wpool_ref, pscale_ref,
                  gnorm_ref, wout_ref, nffn_ref, wr_ref, br_ref, eg_ref, eu_ref, ed_ref,
                  h_ref, hn_ref, route_ref, route_t_ref, pool_out_ref, st_out_ref, cnt_out_ref,
                  eg_out_ref, eu_out_ref, ed_out_ref,
                  ext_ref, st_ref, kbd_ref, vbd_ref, sbd_ref, o_ref, cnt_ref,
                  xs_ref, proj_ref, z_ref, *, tb, nj, lead_pad, chained):
    s = pl.program_id(0)
    back = jnp.maximum(s - 1, 0)
    j = lax.rem(back, nj)
    n_chunks = tb // CHUNK
    wr_slot = lax.rem(s, 2)
    rd_slot = 1 - wr_slot

    @pl.when(s == 0)
    def _():
        kbd_ref[...] = jnp.zeros_like(kbd_ref)
        vbd_ref[...] = jnp.zeros_like(vbd_ref)
        sbd_ref[...] = jnp.zeros_like(sbd_ref)
        xs_ref[1] = jnp.zeros((tb, D_MODEL), F32)
        proj_ref[1] = jnp.zeros((tb, D_MAIN), F32)
        z_ref[1] = jnp.zeros((GATE_RANK, tb), F32)

    @pl.when(s <= 1)
    def _():
        cnt_ref[...] = cnt0_ref[...]

    def put_state(c, hh, st):
        gg, hp = divmod(hh, HEAD_GROUP)
        sbd_ref[c, gg, hp * GLA_DK:(hp + 1) * GLA_DK, hp * GLA_DV:(hp + 1) * GLA_DV] = st.astype(BF16)

    if chained:
        @pl.when(j == 0)
        def _():
            ext_ref[0:POOL_ROWS, :] = pool0_ref[0]
            st_ref[...] = st0_ref[0]

        for hh in range(GLA_HEADS):
            put_state(0, hh, st_ref[hh])
    else:
        for c in range(n_chunks):
            for hh in range(GLA_HEADS):
                put_state(c, hh, st0_ref[c, hh])

    x_new = x_ref[...]
    xn = _rms(x_new, nmix_ref[...]).astype(BF16)
    xs_ref[wr_slot] = x_new

    tiles_done = [0]

    def front_tiles(stage):
        for t in range(tiles_done[0], tiles_done[0] + FRONT_PLAN[stage]):
            cols = slice(t * FRONT_TILE, (t + 1) * FRONT_TILE)
            proj_ref[wr_slot, :, cols] = _dot_nt(xn, win_ref[cols, :])
        tiles_done[0] += FRONT_PLAN[stage]

    front_tiles(0)
    z_ref[wr_slot] = _dot_nt(win_ref[D_MAIN:, :], xn)

    x = xs_ref[rd_slot]
    z = z_ref[rd_slot]
    u = proj_ref[rd_slot, :, 0:D_POOL]
    q = proj_ref[rd_slot, :, D_POOL:D_POOL + D_QK]
    k = proj_ref[rd_slot, :, D_POOL + D_QK:D_POOL + 2 * D_QK]
    v = proj_ref[rd_slot, :, D_POOL + 2 * D_QK:D_POOL + 2 * D_QK + D_V]
    r = proj_ref[rd_slot, :, D_POOL + 2 * D_QK + D_V:D_MAIN]

    row = lax.broadcasted_iota(I32, (tb, 1), 0)

    pseg = POOL_ROWS + CHUNK
    if chained:
        ext_ref[POOL_ROWS:POOL_ROWS + tb, :] = u
        ext = ext_ref[...]
    else:
        ext = jnp.concatenate(
            [blk for c in range(n_chunks) for blk in (pool0_ref[c], u[c * CHUNK:(c + 1) * CHUNK])], axis=0)
    pooled = []
    for g, w in enumerate(POOL_WINDOWS):
        sl = slice(g * POOL_GROUP_DIM, (g + 1) * POOL_GROUP_DIM)
        acc = ext[:, sl]
        for d in range(g + 1):
            acc = acc + pltpu.roll(acc, 1 << d, axis=0)
        if chained:
            win = acc[POOL_ROWS:, :]
        else:
            win = jnp.concatenate([acc[c * pseg + POOL_ROWS:(c + 1) * pseg] for c in range(n_chunks)], axis=0)
        if lead_pad:
            cnt = jnp.clip(row - lead_pad + 1, 1, w).astype(F32)
            pooled.append(win / cnt - u[:, sl])
        else:
            pooled.append(win * (1.0 / w) - u[:, sl])
    pys = [_dot(jnp.concatenate(pooled[2 * i:2 * i + 2], axis=-1).astype(BF16), wpool_ref[i])
           for i in range(len(POOL_WINDOWS) // 2)]
    pool_y = jnp.concatenate(pys, axis=-1) * pscale_ref[...]
    if chained:
        ext_ref[0:POOL_ROWS, :] = ext_ref[tb:tb + POOL_ROWS, :]
    else:
        for c in range(n_chunks):
            pool_out_ref[c] = u[(c + 1) * CHUNK - POOL_ROWS:(c + 1) * CHUNK]

    gpre = _dot_tn(z.astype(BF16), wgu_ref[...]) + bgate_ref[...]
    log_a = (jnp.minimum(gpre, 0.0) - jnp.log(1.0 + jnp.exp(-jnp.abs(gpre)))) * (1.0 / GATE_TAU)
    if lead_pad:
        log_a = jnp.where(row >= lead_pad, log_a, 0.0)
    a_hi = log_a.astype(BF16)
    a_lo = (log_a - a_hi.astype(F32)).astype(BF16)
    tril = tril_ref[...]
    seg = tril.shape[0]
    bcum = jnp.concatenate(
        [_dot(tril, a_hi[r0:r0 + seg]) + _dot(tril, a_lo[r0:r0 + seg]) for r0 in range(0, tb, seg)], axis=0)
    front_tiles(1)
    eb = jnp.exp(bcum)
    qi = q * (GLA_DK ** -0.5) * eb
    ki = k * jnp.exp(-bcum)

    rr = lax.broadcasted_iota(I32, (CHUNK, HEAD_GROUP * CHUNK), 0)
    cc = lax.broadcasted_iota(I32, (CHUNK, HEAD_GROUP * CHUNK), 1)
    causal = (cc % CHUNK) <= rr

    lasts = [eb[(c + 1) * CHUNK - 1:(c + 1) * CHUNK, :] for c in range(n_chunks)]
    dcol = jnp.concatenate(lasts + [jnp.zeros((LANES - n_chunks, D_QK), F32)], axis=0).T

    n_grp = GLA_HEADS // HEAD_GROUP
    chunk_rows = [slice(c * CHUNK, (c + 1) * CHUNK) for c in range(n_chunks)]
    grp_k = [slice(g * HEAD_GROUP * GLA_DK, (g + 1) * HEAD_GROUP * GLA_DK) for g in range(n_grp)]
    grp_v = [slice(g * HEAD_GROUP * GLA_DV, (g + 1) * HEAD_GROUP * GLA_DV) for g in range(n_grp)]
    qi_b = qi.astype(BF16)
    ki_b = ki.astype(BF16)
    v_b = v.astype(BF16)

    scores = {}
    for c in range(n_chunks):
        for hh in range(GLA_HEADS):
            gg, hp = divmod(hh, HEAD_GROUP)
            kbd_ref[c, gg, hp * CHUNK:(hp + 1) * CHUNK, hp * GLA_DK:(hp + 1) * GLA_DK] = (
                ki_b[chunk_rows[c], hh * GLA_DK:(hh + 1) * GLA_DK])
            vbd_ref[c, gg, hp * CHUNK:(hp + 1) * CHUNK, hp * GLA_DV:(hp + 1) * GLA_DV] = (
                v_b[chunk_rows[c], hh * GLA_DV:(hh + 1) * GLA_DV])
        for gg in range(n_grp):
            scores[c, gg] = _dot_nt(qi_b[chunk_rows[c], grp_k[gg]], kbd_ref[c, gg])

    kvs = {}
    for c in range(n_chunks):
        for hh in range(GLA_HEADS):
            kvs[c, hh] = _dot_tn(ki_b[chunk_rows[c], hh * GLA_DK:(hh + 1) * GLA_DK],
                                 v_b[chunk_rows[c], hh * GLA_DV:(hh + 1) * GLA_DV])
    front_tiles(2)

    for hh in range(GLA_HEADS):
        st = st_ref[hh] if chained else None
        for c in range(n_chunks):
            s_old = st if chained else st0_ref[c, hh]
            s_new = (s_old + kvs[c, hh]) * dcol[hh * GLA_DK:(hh + 1) * GLA_DK, c:c + 1]
            if not chained:
                st_out_ref[c, hh] = s_new
            else:
                st = s_new
                if c + 1 < n_chunks:
                    put_state(c + 1, hh, s_new)
        if chained:
            st_ref[hh] = st

    for c in range(n_chunks):
        for gg in range(n_grp):
            p = jnp.where(causal, scores[c, gg], 0.0).astype(BF16)
            o_ref[chunk_rows[c], grp_v[gg]] = (
                _dot(p, vbd_ref[c, gg]) + _dot(qi_b[chunk_rows[c], grp_k[gg]], sbd_ref[c, gg]))

    o = o_ref[...]
    ons = []
    for hh in range(GLA_HEADS):
        oh = o[:, hh * GLA_DV:(hh + 1) * GLA_DV]
        ons.append(oh * lax.rsqrt(jnp.mean(oh * oh, axis=-1, keepdims=True) + EPS))
    og = jnp.concatenate(ons, axis=-1) * gnorm_ref[...] * (r * jax.nn.sigmoid(r))
    mix = _dot(jnp.concatenate([pool_y, og], axis=-1).astype(BF16), wout_ref[...])
    front_tiles(3)

    h = x + mix
    h_ref[...] = h
    hn = _rms(h, nffn_ref[...]).astype(BF16)
    _pack_planes(hn, hn_ref)

    logits = _dot_nt(wr_ref[...], hn) + br_ref[...]
    front_tiles(4)
    sub = lax.broadcasted_iota(I32, (EXPERTS_PER_GROUP, tb), 0).astype(F32)
    neg = jnp.float32(-jnp.inf)
    big = jnp.float32(EXPERTS_PER_GROUP)
    tile0 = logits[0:EXPERTS_PER_GROUP]
    is_g = sub < N_GROUPS
    gmax = jnp.max(jnp.where(is_g, tile0, neg), axis=0, keepdims=True)
    gsum = jnp.sum(jnp.where(is_g, jnp.exp(tile0 - gmax), 0.0), axis=0, keepdims=True)
    p_g = 1.0 / gsum
    gidx = jnp.min(jnp.where(is_g & (tile0 == gmax), sub, big), axis=0, keepdims=True)
    el = logits[N_GROUPS * EXPERTS_PER_GROUP:(N_GROUPS + 1) * EXPERTS_PER_GROUP]
    for g in range(N_GROUPS - 2, -1, -1):
        el = jnp.where(gidx == g, logits[(g + 1) * EXPERTS_PER_GROUP:(g + 2) * EXPERTS_PER_GROUP], el)
    m1 = jnp.max(el, axis=0, keepdims=True)
    i1 = jnp.min(jnp.where(el == m1, sub, big), axis=0, keepdims=True)
    rest = sub != i1
    m2 = jnp.max(jnp.where(rest, el, neg), axis=0, keepdims=True)
    i2 = jnp.min(jnp.where(rest & (el == m2), sub, big), axis=0, keepdims=True)
    t2 = jnp.exp(m2 - m1)
    den = 1.0 + t2
    g1 = p_g / den
    g2 = p_g * t2 / den
    e1 = gidx * EXPERTS_PER_GROUP + i1
    e2 = gidx * EXPERTS_PER_GROUP + i2

    eid = lax.broadcasted_iota(I32, (N_EXPERTS, tb), 0).astype(F32)
    oh1 = eid == e1
    oh2 = eid == e2
    both = jnp.where(oh1 | oh2, 1.0, 0.0)
    cnt = cnt_ref[...]
    before = _dot(both.astype(BF16), sup_ref[...]) + cnt
    front_tiles(5)
    assert tiles_done[0] * FRONT_TILE == D_MAIN
    pos1 = jnp.sum(jnp.where(oh1, before, 0.0), axis=0, keepdims=True)
    pos2 = jnp.sum(jnp.where(oh2, before, 0.0), axis=0, keepdims=True)
    cnt_new = cnt + jnp.sum(both, axis=1, keepdims=True)
    cnt_ref[...] = cnt_new
    cnt_out_ref[...] = cnt_new

    zero = jnp.zeros_like(e1)
    route_t = jnp.concatenate([e1, e2, g1, g2, pos1, pos2, zero, zero], axis=0)
    route_t_ref[0] = route_t
    route_ref[...] = jnp.concatenate([route_t, jnp.zeros((LANES - ROUTE_ROWS, tb), F32)], axis=0).T

    eg_out_ref[...] = eg_ref[...].astype(BF16)
    eu_out_ref[...] = eu_ref[...].astype(BF16)
    ed_out_ref[...] = ed_ref[...].astype(BF16)

    if chained:
        @pl.when(j == nj - 1)
        def _():
            pool_out_ref[0] = ext_ref[0:POOL_ROWS, :]
            st_out_ref[0] = st_ref[...]


def _mixer(x2d, pool0, st0, cnt0, weights, *, batch, seq, tb, lead_pad, experts=None):
    chained = tb <= seq
    total_rows = batch * seq
    n_blk = total_rows // tb
    nj = seq // tb if chained else 1
    per_blk = 1 if chained else tb // seq
    assert chained or (seq == CHUNK and batch % per_blk == 0)
    shared = pool0.shape[0] == 1
    front = lambda s: jnp.minimum(s, n_blk - 1)
    back = lambda s: jnp.maximum(s - 1, 0)
    stream = lambda s: back(s) // nj
    st_idx = (lambda s: (0, 0, 0)) if shared else (lambda s: (stream(s), 0, 0))
    gla_idx = (lambda s: (0, 0, 0, 0)) if shared else (lambda s: (stream(s), 0, 0, 0))
    const2 = lambda s: (0, 0)
    tok_out = lambda s: (back(s), 0)

    seg = min(tb, MXU_DIM)
    ii = jnp.arange(seg)
    tril = ((ii[:, None] >= ii[None, :]) & (ii[:, None] // CHUNK == ii[None, :] // CHUNK)).astype(BF16)
    ii = jnp.arange(tb)
    sup = (ii[:, None] < ii[None, :]).astype(BF16)

    in_specs = [
        pl.BlockSpec((tb, D_MODEL), lambda s: (front(s), 0)),
        pl.BlockSpec((per_blk, POOL_ROWS, D_POOL), st_idx),
        pl.BlockSpec((per_blk, GLA_HEADS, GLA_DK, GLA_DV), gla_idx),
        pl.BlockSpec((N_EXPERTS, 1), const2),
        pl.BlockSpec((seg, seg), const2),
        pl.BlockSpec((tb, tb), const2),
    ]
    for wgt in weights:
        in_specs.append(pl.BlockSpec(wgt.shape, (lambda s, n=wgt.ndim: (0,) * n)))
    if experts is None:
        experts = (jnp.zeros((1, 8, LANES), F32),) * 3
        per = 1
        cast_idx = lambda s: (0, 0, 0)
    else:
        per = n_blk // N_EXPERTS
        assert n_blk == per * N_EXPERTS and all(w.shape[1] % (8 * per) == 0 for w in experts)
        cast_idx = lambda s: (front(s) // per, front(s) % per, 0)
    cast_specs = [pl.BlockSpec((1, w.shape[1] // per, w.shape[2]), cast_idx) for w in experts]
    in_specs += cast_specs
    args = [x2d, pool0, st0, cnt0, tril, sup, *weights, *experts]

    out_shape = [
        jax.ShapeDtypeStruct((total_rows, D_MODEL), F32),
        jax.ShapeDtypeStruct((HN_PLANES, total_rows, LANES), I32),
        jax.ShapeDtypeStruct((total_rows, LANES), F32),
        jax.ShapeDtypeStruct((n_blk, ROUTE_ROWS, tb), F32),
        jax.ShapeDtypeStruct((batch, POOL_ROWS, D_POOL), F32),
        jax.ShapeDtypeStruct((batch, GLA_HEADS, GLA_DK, GLA_DV), F32),
        jax.ShapeDtypeStruct((N_EXPERTS, 1), F32),
        *[jax.ShapeDtypeStruct(w.shape, BF16) for w in experts],
    ]
    out_specs = [
        pl.BlockSpec((tb, D_MODEL), tok_out),
        pl.BlockSpec((HN_PLANES, tb, LANES), lambda s: (0, back(s), 0)),
        pl.BlockSpec((tb, LANES), tok_out),
        pl.BlockSpec((1, ROUTE_ROWS, tb), lambda s: (back(s), 0, 0)),
        pl.BlockSpec((per_blk, POOL_ROWS, D_POOL), lambda s: (stream(s), 0, 0)),
        pl.BlockSpec((per_blk, GLA_HEADS, GLA_DK, GLA_DV), lambda s: (stream(s), 0, 0, 0)),
        pl.BlockSpec((N_EXPERTS, 1), const2),
        *cast_specs,
    ]
    n_grp = GLA_HEADS // HEAD_GROUP
    scratch = [
        pltpu.VMEM((POOL_ROWS + tb, D_POOL), F32),
        pltpu.VMEM((GLA_HEADS, GLA_DK, GLA_DV), F32),
        pltpu.VMEM((tb // CHUNK, n_grp, HEAD_GROUP * CHUNK, HEAD_GROUP * GLA_DK), BF16),
        pltpu.VMEM((tb // CHUNK, n_grp, HEAD_GROUP * CHUNK, HEAD_GROUP * GLA_DV), BF16),
        pltpu.VMEM((tb // CHUNK, n_grp, HEAD_GROUP * GLA_DK, HEAD_GROUP * GLA_DV), BF16),
        pltpu.VMEM((tb, D_V), F32),
        pltpu.VMEM((N_EXPERTS, 1), F32),
        pltpu.VMEM((2, tb, D_MODEL), F32),
        pltpu.VMEM((2, tb, D_MAIN), F32),
        pltpu.VMEM((2, GATE_RANK, tb), F32),
    ]
    return pl.pallas_call(
        functools.partial(_mixer_kernel, tb=tb, nj=nj, lead_pad=lead_pad, chained=chained),
        grid=(n_blk + 1,),
        in_specs=in_specs,
        out_specs=out_specs,
        out_shape=out_shape,
        scratch_shapes=scratch,
        compiler_params=pltpu.CompilerParams(
            dimension_semantics=("arbitrary",), vmem_limit_bytes=VMEM_LIMIT),
        name=f"mixer_tb{tb}_pad{lead_pad}",
    )(*args)


def _sc_mesh():
    return plsc.VectorSubcoreMesh(core_axis_name="c", subcore_axis_name="s",
                                  num_cores=SC_CORES, num_subcores=SC_SUBCORES)


def _sc_worker():
    return lax.axis_index("s") * SC_CORES + lax.axis_index("c")


def _plane_rows(dest, planes, rows_per_plane):
    offs = (jnp.arange(planes, dtype=I32) * rows_per_plane)[None, :, None]
    return dest[:, None, :] + offs


def _dispatch(hn_p, hn_s, dest0, dest1, n_rows):
    planes, t_p, _ = hn_p.shape
    t_s = hn_s.shape[1]
    n_cp = t_p // SC_ROWS
    cp = n_cp // SC_WORKERS
    n_cs = t_s // SC_ROWS
    assert t_p == cp * SC_ROWS * SC_WORKERS and t_s == n_cs * SC_ROWS and n_cs <= SC_WORKERS
    idx0 = _plane_rows(dest0, planes, n_rows)
    idx1 = _plane_rows(dest1, planes, n_rows)

    def body(hnp_hbm, hns_hbm, d0_hbm, d1_hbm, xs_hbm, rows_v, i0_v, i1_v, is0_v, is1_v, sem_in, sem_out):
        wid = _sc_worker()
        pltpu.sync_copy(d0_hbm.at[pl.ds(wid * cp, cp)], i0_v)
        pltpu.sync_copy(d1_hbm.at[pl.ds(wid * cp, cp)], i1_v)

        def move(src_hbm, src_rows, row0, i0, i1, c):
            loads = [pltpu.async_copy(src_hbm.at[pl.ds(p * src_rows + row0, SC_ROWS)], rows_v.at[p], sem_in)
                     for p in range(planes)]
            for cpy in loads:
                cpy.wait()
            stores = []
            for p in range(planes):
                stores.append(pltpu.async_copy(rows_v.at[p], xs_hbm.at[i0.at[c, p]], sem_out))
                stores.append(pltpu.async_copy(rows_v.at[p], xs_hbm.at[i1.at[c, p]], sem_out))
            for cpy in stores:
                cpy.wait()

        @pl.loop(0, cp)
        def _(c):
            move(hnp_hbm, t_p, (wid * cp + c) * SC_ROWS, i0_v, i1_v, c)

        @pl.when(wid < n_cs)
        def _():
            pltpu.sync_copy(d0_hbm.at[pl.ds(n_cp + wid, 1)], is0_v)
            pltpu.sync_copy(d1_hbm.at[pl.ds(n_cp + wid, 1)], is1_v)
            move(hns_hbm, t_s, wid * SC_ROWS, is0_v, is1_v, 0)

    xs = pl.kernel(
        body,
        out_type=jax.ShapeDtypeStruct((planes * n_rows, LANES), I32),
        mesh=_sc_mesh(),
        scratch_types=[
            pltpu.VMEM((planes, SC_ROWS, LANES), I32),
            pltpu.VMEM((cp, planes, SC_ROWS), I32),
            pltpu.VMEM((cp, planes, SC_ROWS), I32),
            pltpu.VMEM((1, planes, SC_ROWS), I32),
            pltpu.VMEM((1, planes, SC_ROWS), I32),
            pltpu.SemaphoreType.DMA,
            pltpu.SemaphoreType.DMA,
        ],
        name="moe_dispatch_sc",
    )(hn_p.reshape(planes * t_p, LANES), hn_s.reshape(planes * t_s, LANES), idx0, idx1)
    return xs.reshape(planes, n_rows, LANES)


def _gather(ys, dest0, dest1):
    planes, n_rows, _ = ys.shape
    n_chunks = dest0.shape[0]
    n_tok = n_chunks * SC_ROWS
    cpw = max(n_chunks // SC_WORKERS, 1)
    assert n_chunks <= SC_WORKERS or n_chunks == cpw * SC_WORKERS
    idx = (_plane_rows(dest0, planes, n_rows), _plane_rows(dest1, planes, n_rows))

    def body(ys_hbm, d0_hbm, d1_hbm, out_hbm, rows_v, i0_v, i1_v, sem_in, sem_out):
        wid = _sc_worker()

        def work():
            pltpu.sync_copy(d0_hbm.at[pl.ds(wid * cpw, cpw)], i0_v)
            pltpu.sync_copy(d1_hbm.at[pl.ds(wid * cpw, cpw)], i1_v)

            @pl.loop(0, cpw)
            def _(c):
                row0 = (wid * cpw + c) * SC_ROWS
                for j, i_v in enumerate((i0_v, i1_v)):
                    loads = [pltpu.async_copy(ys_hbm.at[i_v.at[c, p]], rows_v.at[p], sem_in)
                             for p in range(planes)]
                    for cpy in loads:
                        cpy.wait()
                    stores = [
                        pltpu.async_copy(
                            rows_v.at[p], out_hbm.at[pl.ds((j * planes + p) * n_tok + row0, SC_ROWS)], sem_out)
                        for p in range(planes)]
                    for cpy in stores:
                        cpy.wait()

        if n_chunks < SC_WORKERS:
            pl.when(wid < n_chunks)(work)
        else:
            work()

    out = pl.kernel(
        body,
        out_type=jax.ShapeDtypeStruct((2 * planes * n_tok, LANES), ys.dtype),
        mesh=_sc_mesh(),
        scratch_types=[
            pltpu.VMEM((planes, SC_ROWS, LANES), ys.dtype),
            pltpu.VMEM((cpw, planes, SC_ROWS), I32),
            pltpu.VMEM((cpw, planes, SC_ROWS), I32),
            pltpu.SemaphoreType.DMA,
            pltpu.SemaphoreType.DMA,
        ],
        name="moe_gather_sc",
    )(ys.reshape(planes * n_rows, LANES), *idx)
    return out.reshape(2, planes, n_tok, LANES)


def _expert_kernel(be_ref, bv_ref, nb_ref, xs_ref, wg_ref, wu_ref, wd_ref, ys_ref):
    del be_ref
    i = pl.program_id(0)
    live = i < nb_ref[0]

    @pl.when(live)
    def _():
        valid = lax.broadcasted_iota(I32, (MOE_BLK, LANES), 0) < bv_ref[i]
        xb = _unpack_planes([jnp.where(valid, xs_ref[p], 0) for p in range(HN_PLANES)]).astype(BF16)
        gate = _dot(xb, wg_ref[0])
        hmid = gate * jax.nn.sigmoid(gate) * _dot(xb, wu_ref[0])
        _pack_planes(_dot(hmid.astype(BF16), wd_ref[0]), ys_ref)

    @pl.when(jnp.logical_not(live))
    def _():
        ys_ref[...] = jnp.zeros_like(ys_ref)


def _experts(block_e, block_valid, nb, xs, w_eg, w_eu, w_ed):
    n_rows = xs.shape[1]
    n_blocks = n_rows // MOE_BLK
    row_idx = lambda i, be, bv, nb: (0, jnp.minimum(i, nb[0] - 1), 0)
    w_idx = lambda i, be, bv, nb: (be[i], 0, 0)
    grid_spec = pltpu.PrefetchScalarGridSpec(
        num_scalar_prefetch=3,
        grid=(n_blocks,),
        in_specs=[
            pl.BlockSpec((HN_PLANES, MOE_BLK, LANES), row_idx),
            pl.BlockSpec((1, D_MODEL, D_EXPERT), w_idx),
            pl.BlockSpec((1, D_MODEL, D_EXPERT), w_idx),
            pl.BlockSpec((1, D_EXPERT, D_MODEL), w_idx),
        ],
        out_specs=pl.BlockSpec((HN_PLANES, MOE_BLK, LANES), lambda i, be, bv, nb: (0, i, 0)),
    )
    return pl.pallas_call(
        _expert_kernel,
        grid_spec=grid_spec,
        out_shape=jax.ShapeDtypeStruct((HN_PLANES, n_rows, LANES), I32),
        compiler_params=pltpu.CompilerParams(
            dimension_semantics=("arbitrary",), vmem_limit_bytes=VMEM_LIMIT),
        name="moe_experts",
    )(block_e, block_valid, nb, xs, w_eg, w_eu, w_ed)


def _combine_kernel(h_ref, route_ref, nfin_ref, rows_ref, *rest):
    y_ref = rest[-1]
    route = route_ref[...]
    ys1 = _unpack_planes([rows_ref[0, p] for p in range(HN_PLANES)])
    ys2 = _unpack_planes([rows_ref[1, p] for p in range(HN_PLANES)])
    out = h_ref[...] + (ys1 * route[:, 2:3] + ys2 * route[:, 3:4])
    y_ref[...] = _rms(out, nfin_ref[...])


def _combine(h, route, norm_final, rows, *, row0=0, y_prev=None):
    total = h.shape[0]
    tb = COMBINE_TB
    blk0 = row0 // tb
    args = [h, route, norm_final, rows]
    in_specs = [
        pl.BlockSpec((tb, D_MODEL), lambda i: (blk0 + i, 0)),
        pl.BlockSpec((tb, LANES), lambda i: (blk0 + i, 0)),
        pl.BlockSpec((1, D_MODEL), lambda i: (0, 0)),
        pl.BlockSpec((2, HN_PLANES, tb, LANES), lambda i: (0, 0, i, 0)),
    ]
    aliases = {}
    if y_prev is not None:
        aliases[len(args)] = 0
        args.append(y_prev)
        in_specs.append(pl.BlockSpec(memory_space=pl.ANY))
    return pl.pallas_call(
        _combine_kernel,
        grid=(rows.shape[2] // tb,),
        in_specs=in_specs,
        out_specs=pl.BlockSpec((tb, D_MODEL), lambda i: (blk0 + i, 0)),
        out_shape=jax.ShapeDtypeStruct((total, D_MODEL), F32),
        input_output_aliases=aliases,
        compiler_params=pltpu.CompilerParams(
            dimension_semantics=("arbitrary",), vmem_limit_bytes=VMEM_LIMIT),
        name="moe_combine",
    )(*args)


def kernel(x_prompt, x_sample, state_pool, state_gla, meta_tokens, norm_mix, w_in, w_gate_up, b_gate, w_pool, pool_scale, gla_norm, w_out, norm_ffn, w_router_group, b_router_group, w_router_expert, b_router_expert, w_expert_gate, w_expert_up, w_expert_down, norm_final):
    assert w_in.shape[0] == 1, "one encoder layer"
    batch, seq, _ = x_prompt.shape
    dec_batch, dec_seq, _ = x_sample.shape
    assert seq % MIX_TB == 0 and dec_seq == CHUNK and N_META <= CHUNK
    t_prompt = batch * seq
    t_sample = dec_batch * dec_seq
    t_all = t_prompt + t_sample
    assert t_prompt % COMBINE_TB == 0 and t_sample % COMBINE_TB == 0

    gpad = EXPERTS_PER_GROUP - N_GROUPS
    rpad = ROUTER_ROWS - EXPERTS_PER_GROUP - N_EXPERTS
    w_router = jnp.concatenate([
        w_router_group[0].T, jnp.zeros((gpad, D_MODEL), F32),
        w_router_expert[0].T, jnp.zeros((rpad, D_MODEL), F32)], axis=0)
    b_router = jnp.concatenate([
        b_router_group[0], jnp.zeros((gpad,), F32), b_router_expert[0], jnp.zeros((rpad,), F32)])
    zg = jnp.zeros((POOL_GROUP_DIM, POOL_GROUP_DIM), F32)
    w_pool_pairs = jnp.stack([
        jnp.block([[w_pool[0, 2 * i], zg], [zg, w_pool[0, 2 * i + 1]]]) for i in range(len(POOL_WINDOWS) // 2)])
    weights = (
        norm_mix[0][None, :],
        w_in[0].T.astype(BF16),
        w_gate_up[0].astype(BF16),
        b_gate[0][None, :],
        w_pool_pairs.astype(BF16),
        pool_scale[0][None, :],
        gla_norm[0][None, :],
        w_out[0].astype(BF16),
        norm_ffn[0][None, :],
        w_router.astype(BF16),
        b_router[:, None],
    )

    zero_cnt = jnp.zeros((N_EXPERTS, 1), F32)
    x_meta = jnp.pad(meta_tokens.astype(F32), ((CHUNK - N_META, 0), (0, 0)))
    meta = _mixer(x_meta, jnp.zeros((1, POOL_ROWS, D_POOL), F32),
                  jnp.zeros((1, GLA_HEADS, GLA_DK, GLA_DV), F32), zero_cnt,
                  weights, batch=1, seq=CHUNK, tb=CHUNK, lead_pad=CHUNK - N_META)
    pool_m, st_m = meta[4], meta[5]
    h_p, hn_p, route_p, rt_p, pool_p, st_p, cnt_p, w_eg, w_eu, w_ed = _mixer(
        x_prompt.reshape(t_prompt, D_MODEL), pool_m, st_m, zero_cnt, weights,
        batch=batch, seq=seq, tb=MIX_TB, lead_pad=0,
        experts=(w_expert_gate[0], w_expert_up[0], w_expert_down[0]))
    pool_s0 = jnp.pad(state_pool[0], ((0, 0), (POOL_ROWS - POOL_PAD, 0), (0, 0)))
    h_s, hn_s, route_s, rt_s, pool_s, st_s, cnt_s = _mixer(
        x_sample.reshape(t_sample, D_MODEL), pool_s0, state_gla[0].astype(F32),
        cnt_p, weights, batch=dec_batch, seq=dec_seq, tb=SAMPLE_TB, lead_pad=0)[:7]

    counts = cnt_s[:, 0].astype(I32)
    padded = (counts + MOE_BLK - 1) // MOE_BLK * MOE_BLK
    ends = jnp.cumsum(padded)
    pstart = ends - padded
    n_blocks = (2 * t_all + N_EXPERTS * (MOE_BLK - 1) + MOE_BLK - 1) // MOE_BLK
    nb = (ends[-1] // MOE_BLK).astype(I32)
    blk_ids = jnp.minimum(jnp.arange(n_blocks, dtype=I32), nb - 1)
    block_e = jnp.sum((ends[None, :] <= (blk_ids * MOE_BLK)[:, None]).astype(I32), axis=1)
    block_e = jnp.minimum(block_e, N_EXPERTS - 1)
    owner = block_e[:, None] == jnp.arange(N_EXPERTS, dtype=I32)
    row_end = jnp.sum(jnp.where(owner, pstart + counts, 0), axis=1)
    block_valid = jnp.clip(row_end - blk_ids * MOE_BLK, 0, MOE_BLK)

    def dest_rows(rt):
        rt = rt.transpose(1, 0, 2).reshape(ROUTE_ROWS, -1)
        onehot = rt[0:2].astype(I32)[..., None] == jnp.arange(N_EXPERTS, dtype=I32)
        return jnp.sum(jnp.where(onehot, pstart, 0), axis=-1) + rt[4:6].astype(I32)

    dest = jnp.concatenate([dest_rows(rt_p), dest_rows(rt_s)], axis=1)
    dest0 = dest[0].reshape(t_all // SC_ROWS, SC_ROWS)
    dest1 = dest[1].reshape(t_all // SC_ROWS, SC_ROWS)
    xs = _dispatch(hn_p, hn_s, dest0, dest1, n_blocks * MOE_BLK)
    ys = _experts(block_e, block_valid.astype(I32), nb[None], xs, w_eg, w_eu, w_ed)
    nfin = norm_final[None, :]
    cp_chunks = t_prompt // SC_ROWS
    part = cp_chunks // COMBINE_PARTS
    y_prompt = None
    for i in range(COMBINE_PARTS):
        ch = slice(i * part, (i + 1) * part)
        rows_i = _gather(ys, dest0[ch], dest1[ch])
        y_prompt = _combine(h_p, route_p, nfin, rows_i, row0=i * part * SC_ROWS, y_prev=y_prompt)
    y_prompt = y_prompt.reshape(batch, seq, D_MODEL)
    rows_s = _gather(ys, dest0[cp_chunks:], dest1[cp_chunks:])
    y_sample = _combine(h_s, route_s, nfin, rows_s).reshape(dec_batch, dec_seq, D_MODEL)
    new_pool_prompt = pool_p[:, POOL_ROWS - POOL_PAD:][None]
    new_gla_prompt = st_p[None]
    new_pool_sample = pool_s[:, POOL_ROWS - POOL_PAD:][None]
    new_gla_sample = st_s[None]
    return (y_prompt, y_sample, new_pool_prompt, new_gla_prompt, new_pool_sample, new_gla_sample)
```

```python
import functools

import jax
import jax.numpy as jnp
from jax import lax
from jax.experimental import pallas as pl
from jax.experimental.pallas import tpu as pltpu
from jax.experimental.pallas import tpu_sc as plsc

F32 = jnp.float32
BF16 = jnp.bfloat16
U32 = jnp.uint32
I32 = jnp.int32

D_MODEL = 1024
N_META = 16
CHUNK = 64
EPS = 1e-6
D_POOL = 512
POOL_WINDOWS = (2, 4, 8, 16)
POOL_GROUP_DIM = 128
POOL_PAD = 15
POOL_ROWS = 16
GLA_HEADS = 4
GLA_DK = 64
GLA_DV = 128
D_QK = 256
D_V = 512
HEAD_GROUP = 2
GATE_RANK = 16
GATE_TAU = 16.0
D_MAIN = D_POOL + 2 * D_QK + 2 * D_V
N_GROUPS = 4
EXPERTS_PER_GROUP = 8
N_EXPERTS = 32
D_EXPERT = 512

LANES = 128
MXU_DIM = 256
HALF = D_MODEL // 2
HN_PLANES = HALF // LANES
ROUTE_ROWS = 8
ROUTER_ROWS = 64
MIX_TB = 512
SAMPLE_TB = 256
FRONT_TILE = 256
FRONT_PLAN = (3, 0, 1, 2, 2, 0)
WEIGHT_TILE = 512
MOE_BLK = 768
COMBINE_TB = 1024
COMBINE_PARTS = 4
VMEM_LIMIT = 48 * 1024 * 1024
SC_CORES = 2
SC_SUBCORES = 16
SC_WORKERS = SC_CORES * SC_SUBCORES
SC_ROWS = 128


def _rms(x, g):
    return x * lax.rsqrt(jnp.mean(x * x, axis=-1, keepdims=True) + EPS) * g


def _dot(a, b):
    return jnp.dot(a, b, preferred_element_type=F32)


def _dot_nt(a, b):
    return lax.dot_general(a, b, (((1,), (1,)), ((), ())), preferred_element_type=F32)


def _pack_planes(x, ref):
    xb = x.astype(BF16)
    lo = lax.bitcast_convert_type(xb[:, :HALF].astype(F32), U32) >> 16
    hi = lax.bitcast_convert_type(xb[:, HALF:].astype(F32), U32) & jnp.uint32(0xFFFF0000)
    packed = lax.bitcast_convert_type(lo | hi, I32)
    for p in range(HN_PLANES):
        ref[p] = packed[:, p * LANES:(p + 1) * LANES]


def _unpack_planes(planes):
    words = [lax.bitcast_convert_type(p, U32) for p in planes]
    los = [lax.bitcast_convert_type(w << 16, F32) for w in words]
    his = [lax.bitcast_convert_type(w & jnp.uint32(0xFFFF0000), F32) for w in words]
    return jnp.concatenate(los + his, axis=-1)


def _dot_tn(a, b):
    return lax.dot_general(a, b, (((0,), (0,)), ((), ())), preferred_element_type=F32)


def _mixer_kernel(*refs, **static):
    s = pl.program_id(0)
    pl.when(s == 0)(lambda: _mixer_first_step(*refs, **static))
    pl.when(s > 0)(lambda: _mixer_step(*refs, **static))


def _mixer_first_step(x_ref, pool0_ref, st0_ref, cnt0_ref, tril_ref, sup_ref,
                      nmix_ref, wmain_ref, wz_ref, wgu_ref, bgate_ref, wpool_ref, pscale_ref,
                      gnorm_ref, wout_ref, nffn_ref, wr_ref, br_ref, eg_ref, eu_ref, ed_ref,
                      h_ref, hn_ref, route_ref, route_t_ref, pool_out_ref, st_out_ref, cnt_out_ref,
                      eg_out_ref, eu_out_ref, ed_out_ref,
                      ext_ref, st_ref, kbd_ref, vbd_ref, sbd_ref, o_ref, cnt_ref,
                      xs_ref, proj_ref, z_ref, *, tb, nj, lead_pad, chained):
    x_new = x_ref[...]
    xn = _rms(x_new, nmix_ref[...]).astype(BF16)
    xs_ref[0] = x_new
    for t in range(D_MAIN // FRONT_TILE):
        cols = slice(t * FRONT_TILE, (t + 1) * FRONT_TILE)
        proj_ref[0, :, cols] = _dot(xn, wmain_ref[:, cols])
    z_ref[0] = _dot_nt(wz_ref[...], xn)
    eg_out_ref[...] = eg_ref[...].astype(BF16)
    eu_out_ref[...] = eu_ref[...].astype(BF16)
    ed_out_ref[...] = ed_ref[...].astype(BF16)


def _mixer_step(x_ref, pool0_ref, st0_ref, cnt0_ref, tril_ref, sup_ref,
                nmix_ref, wmain_ref, wz_ref, wgu_ref, bgate_ref, wpool_ref, pscale_ref,
                gnorm_ref, wout_ref, nffn_ref, wr_ref, br_ref, eg_ref, eu_ref, ed_ref,
                h_ref, hn_ref, route_ref, route_t_ref, pool_out_ref, st_out_ref, cnt_out_ref,
                eg_out_ref, eu_out_ref, ed_out_ref,
                ext_ref, st_ref, kbd_ref, vbd_ref, sbd_ref, o_ref, cnt_ref,
                xs_ref, proj_ref, z_ref, *, tb, nj, lead_pad, chained):
    s = pl.program_id(0)
    back = s - 1
    j = lax.rem(back, nj)
    n_chunks = tb // CHUNK
    wr_slot = lax.rem(s, 2)
    rd_slot = 1 - wr_slot

    @pl.when(back == 0)
    def _():
        kbd_ref[...] = jnp.zeros_like(kbd_ref)
        vbd_ref[...] = jnp.zeros_like(vbd_ref)
        sbd_ref[...] = jnp.zeros_like(sbd_ref)
        cnt_ref[...] = cnt0_ref[...]

    def put_state(c, hh, st):
        gg, hp = divmod(hh, HEAD_GROUP)
        sbd_ref[c, gg, hp * GLA_DK:(hp + 1) * GLA_DK, hp * GLA_DV:(hp + 1) * GLA_DV] = st.astype(BF16)

    if chained:
        @pl.when(j == 0)
        def _():
            ext_ref[0:POOL_ROWS, :] = pool0_ref[0]
            st_ref[...] = st0_ref[0]

        for hh in range(GLA_HEADS):
            put_state(0, hh, st_ref[hh])
    else:
        for c in range(n_chunks):
            for hh in range(GLA_HEADS):
                put_state(c, hh, st0_ref[c, hh])

    x_new = x_ref[...]
    xn = _rms(x_new, nmix_ref[...]).astype(BF16)
    xs_ref[wr_slot] = x_new

    tiles_done = [0]

    def front_tiles(stage):
        for t in range(tiles_done[0], tiles_done[0] + FRONT_PLAN[stage]):
            cols = slice(t * FRONT_TILE, (t + 1) * FRONT_TILE)
            proj_ref[wr_slot, :, cols] = _dot(xn, wmain_ref[:, cols])
        tiles_done[0] += FRONT_PLAN[stage]

    front_tiles(0)
    z_ref[wr_slot] = _dot_nt(wz_ref[...], xn)

    x = xs_ref[rd_slot]
    z = z_ref[rd_slot]
    u = proj_ref[rd_slot, :, 0:D_POOL]
    q = proj_ref[rd_slot, :, D_POOL:D_POOL + D_QK]
    k = proj_ref[rd_slot, :, D_POOL + D_QK:D_POOL + 2 * D_QK]
    v = proj_ref[rd_slot, :, D_POOL + 2 * D_QK:D_POOL + 2 * D_QK + D_V]
    r = proj_ref[rd_slot, :, D_POOL + 2 * D_QK + D_V:D_MAIN]

    row = lax.broadcasted_iota(I32, (tb, 1), 0)

    pseg = POOL_ROWS + CHUNK
    if chained:
        ext_ref[POOL_ROWS:POOL_ROWS + tb, :] = u
        ext = ext_ref[...]
    else:
        ext = jnp.concatenate(
            [blk for c in range(n_chunks) for blk in (pool0_ref[c], u[c * CHUNK:(c + 1) * CHUNK])], axis=0)
    pooled = []
    for g, w in enumerate(POOL_WINDOWS):
        sl = slice(g * POOL_GROUP_DIM, (g + 1) * POOL_GROUP_DIM)
        acc = ext[:, sl]
        for d in range(g + 1):
            acc = acc + pltpu.roll(acc, 1 << d, axis=0)
        if chained:
            win = acc[POOL_ROWS:, :]
        else:
            win = jnp.concatenate([acc[c * pseg + POOL_ROWS:(c + 1) * pseg] for c in range(n_chunks)], axis=0)
        if lead_pad:
            cnt = jnp.clip(row - lead_pad + 1, 1, w).astype(F32)
            pooled.append(win / cnt - u[:, sl])
        else:
            pooled.append(win * (1.0 / w) - u[:, sl])
    pys = [_dot(jnp.concatenate(pooled[2 * i:2 * i + 2], axis=-1).astype(BF16), wpool_ref[i])
           for i in range(len(POOL_WINDOWS) // 2)]
    pool_y = jnp.concatenate(pys, axis=-1) * pscale_ref[...]
    if chained:
        ext_ref[0:POOL_ROWS, :] = ext_ref[tb:tb + POOL_ROWS, :]
    else:
        for c in range(n_chunks):
            pool_out_ref[c] = u[(c + 1) * CHUNK - POOL_ROWS:(c + 1) * CHUNK]

    gpre = _dot_tn(z.astype(BF16), wgu_ref[...]) + bgate_ref[...]
    log_a = (jnp.minimum(gpre, 0.0) - jnp.log(1.0 + jnp.exp(-jnp.abs(gpre)))) * (1.0 / GATE_TAU)
    if lead_pad:
        log_a = jnp.where(row >= lead_pad, log_a, 0.0)
    a_hi = log_a.astype(BF16)
    a_lo = (log_a - a_hi.astype(F32)).astype(BF16)
    tril = tril_ref[...]
    seg = tril.shape[0]
    bcum = jnp.concatenate(
        [_dot(tril, a_hi[r0:r0 + seg]) + _dot(tril, a_lo[r0:r0 + seg]) for r0 in range(0, tb, seg)], axis=0)
    front_tiles(1)
    eb = jnp.exp(bcum)
    qi = q * (GLA_DK ** -0.5) * eb
    ki = k * jnp.exp(-bcum)

    rr = lax.broadcasted_iota(I32, (CHUNK, HEAD_GROUP * CHUNK), 0)
    cc = lax.broadcasted_iota(I32, (CHUNK, HEAD_GROUP * CHUNK), 1)
    causal = (cc % CHUNK) <= rr

    lasts = [eb[(c + 1) * CHUNK - 1:(c + 1) * CHUNK, :] for c in range(n_chunks)]
    dcol = jnp.concatenate(lasts + [jnp.zeros((LANES - n_chunks, D_QK), F32)], axis=0).T

    n_grp = GLA_HEADS // HEAD_GROUP
    chunk_rows = [slice(c * CHUNK, (c + 1) * CHUNK) for c in range(n_chunks)]
    grp_k = [slice(g * HEAD_GROUP * GLA_DK, (g + 1) * HEAD_GROUP * GLA_DK) for g in range(n_grp)]
    grp_v = [slice(g * HEAD_GROUP * GLA_DV, (g + 1) * HEAD_GROUP * GLA_DV) for g in range(n_grp)]
    qi_b = qi.astype(BF16)
    ki_b = ki.astype(BF16)
    v_b = v.astype(BF16)

    scores = {}
    for c in range(n_chunks):
        for hh in range(GLA_HEADS):
            gg, hp = divmod(hh, HEAD_GROUP)
            kbd_ref[c, gg, hp * CHUNK:(hp + 1) * CHUNK, hp * GLA_DK:(hp + 1) * GLA_DK] = (
                ki_b[chunk_rows[c], hh * GLA_DK:(hh + 1) * GLA_DK])
            vbd_ref[c, gg, hp * CHUNK:(hp + 1) * CHUNK, hp * GLA_DV:(hp + 1) * GLA_DV] = (
                v_b[chunk_rows[c], hh * GLA_DV:(hh + 1) * GLA_DV])
        for gg in range(n_grp):
            scores[c, gg] = _dot_nt(qi_b[chunk_rows[c], grp_k[gg]], kbd_ref[c, gg])

    kvs = {}
    for c in range(n_chunks):
        for hh in range(GLA_HEADS):
            kvs[c, hh] = _dot_tn(ki_b[chunk_rows[c], hh * GLA_DK:(hh + 1) * GLA_DK],
                                 v_b[chunk_rows[c], hh * GLA_DV:(hh + 1) * GLA_DV])
    front_tiles(2)

    for hh in range(GLA_HEADS):
        st = st_ref[hh] if chained else None
        for c in range(n_chunks):
            s_old = st if chained else st0_ref[c, hh]
            s_new = (s_old + kvs[c, hh]) * dcol[hh * GLA_DK:(hh + 1) * GLA_DK, c:c + 1]
            if not chained:
                st_out_ref[c, hh] = s_new
            else:
                st = s_new
                if c + 1 < n_chunks:
                    put_state(c + 1, hh, s_new)
        if chained:
            st_ref[hh] = st

    for c in range(n_chunks):
        for gg in range(n_grp):
            p = jnp.where(causal, scores[c, gg], 0.0).astype(BF16)
            o_ref[chunk_rows[c], grp_v[gg]] = (
                _dot(p, vbd_ref[c, gg]) + _dot(qi_b[chunk_rows[c], grp_k[gg]], sbd_ref[c, gg]))

    o = o_ref[...]
    ons = []
    for hh in range(GLA_HEADS):
        oh = o[:, hh * GLA_DV:(hh + 1) * GLA_DV]
        ons.append(oh * lax.rsqrt(jnp.mean(oh * oh, axis=-1, keepdims=True) + EPS))
    og = jnp.concatenate(ons, axis=-1) * gnorm_ref[...] * (r * jax.nn.sigmoid(r))
    mix = _dot(jnp.concatenate([pool_y, og], axis=-1).astype(BF16), wout_ref[...])
    front_tiles(3)

    eg_out_ref[...] = eg_ref[...].astype(BF16)
    eu_out_ref[...] = eu_ref[...].astype(BF16)
    ed_out_ref[...] = ed_ref[...].astype(BF16)

    h = x + mix
    h_ref[...] = h
    hn = _rms(h, nffn_ref[...]).astype(BF16)
    _pack_planes(hn, hn_ref)

    logits = _dot_nt(wr_ref[...], hn) + br_ref[...]
    front_tiles(4)
    sub = lax.broadcasted_iota(I32, (EXPERTS_PER_GROUP, tb), 0).astype(F32)
    neg = jnp.float32(-jnp.inf)
    big = jnp.float32(EXPERTS_PER_GROUP)
    tile0 = logits[0:EXPERTS_PER_GROUP]
    is_g = sub < N_GROUPS
    gmax = jnp.max(jnp.where(is_g, tile0, neg), axis=0, keepdims=True)
    gsum = jnp.sum(jnp.where(is_g, jnp.exp(tile0 - gmax), 0.0), axis=0, keepdims=True)
    p_g = 1.0 / gsum
    gidx = jnp.min(jnp.where(is_g & (tile0 == gmax), sub, big), axis=0, keepdims=True)
    el = logits[N_GROUPS * EXPERTS_PER_GROUP:(N_GROUPS + 1) * EXPERTS_PER_GROUP]
    for g in range(N_GROUPS - 2, -1, -1):
        el = jnp.where(gidx == g, logits[(g + 1) * EXPERTS_PER_GROUP:(g + 2) * EXPERTS_PER_GROUP], el)
    m1 = jnp.max(el, axis=0, keepdims=True)
    i1 = jnp.min(jnp.where(el == m1, sub, big), axis=0, keepdims=True)
    rest = sub != i1
    m2 = jnp.max(jnp.where(rest, el, neg), axis=0, keepdims=True)
    i2 = jnp.min(jnp.where(rest & (el == m2), sub, big), axis=0, keepdims=True)
    t2 = jnp.exp(m2 - m1)
    den = 1.0 + t2
    g1 = p_g / den
    g2 = p_g * t2 / den
    e1 = gidx * EXPERTS_PER_GROUP + i1
    e2 = gidx * EXPERTS_PER_GROUP + i2

    eid = lax.broadcasted_iota(I32, (N_EXPERTS, tb), 0).astype(F32)
    oh1 = eid == e1
    oh2 = eid == e2
    both = jnp.where(oh1 | oh2, 1.0, 0.0)
    cnt = cnt_ref[...]
    before = _dot(both.astype(BF16), sup_ref[...]) + cnt
    front_tiles(5)
    assert tiles_done[0] * FRONT_TILE == D_MAIN
    pos1 = jnp.sum(jnp.where(oh1, before, 0.0), axis=0, keepdims=True)
    pos2 = jnp.sum(jnp.where(oh2, before, 0.0), axis=0, keepdims=True)
    cnt_new = cnt + jnp.sum(both, axis=1, keepdims=True)
    cnt_ref[...] = cnt_new
    cnt_out_ref[...] = cnt_new

    zero = jnp.zeros_like(e1)
    route_t = jnp.concatenate([e1, e2, g1, g2, pos1, pos2, zero, zero], axis=0)
    route_t_ref[0] = route_t
    route_ref[...] = jnp.concatenate([route_t, jnp.zeros((LANES - ROUTE_ROWS, tb), F32)], axis=0).T

    if chained:
        @pl.when(j == nj - 1)
        def _():
            pool_out_ref[0] = ext_ref[0:POOL_ROWS, :]
            st_out_ref[0] = st_ref[...]


def _mixer(x2d, pool0, st0, cnt0, weights, *, batch, seq, tb, lead_pad, experts=None):
    chained = tb <= seq
    total_rows = batch * seq
    n_blk = total_rows // tb
    nj = seq // tb if chained else 1
    per_blk = 1 if chained else tb // seq
    assert chained or (seq == CHUNK and batch % per_blk == 0)
    shared = pool0.shape[0] == 1
    front = lambda s: jnp.minimum(s, n_blk - 1)
    back = lambda s: jnp.maximum(s - 1, 0)
    stream = lambda s: back(s) // nj
    st_idx = (lambda s: (0, 0, 0)) if shared else (lambda s: (stream(s), 0, 0))
    gla_idx = (lambda s: (0, 0, 0, 0)) if shared else (lambda s: (stream(s), 0, 0, 0))
    const2 = lambda s: (0, 0)
    tok_out = lambda s: (back(s), 0)

    seg = min(tb, MXU_DIM)
    ii = jnp.arange(seg)
    tril = ((ii[:, None] >= ii[None, :]) & (ii[:, None] // CHUNK == ii[None, :] // CHUNK)).astype(BF16)
    ii = jnp.arange(tb)
    sup = (ii[:, None] < ii[None, :]).astype(BF16)

    in_specs = [
        pl.BlockSpec((tb, D_MODEL), lambda s: (front(s), 0)),
        pl.BlockSpec((per_blk, POOL_ROWS, D_POOL), st_idx),
        pl.BlockSpec((per_blk, GLA_HEADS, GLA_DK, GLA_DV), gla_idx),
        pl.BlockSpec((N_EXPERTS, 1), const2),
        pl.BlockSpec((seg, seg), const2),
        pl.BlockSpec((tb, tb), const2),
    ]
    for wgt in weights:
        in_specs.append(pl.BlockSpec(wgt.shape, (lambda s, n=wgt.ndim: (0,) * n)))
    if experts is None:
        experts = (jnp.zeros((1, 8, LANES), F32),) * 3
        per = 1
        cast_idx = lambda s: (0, 0, 0)
    else:
        per = n_blk // N_EXPERTS
        assert n_blk == per * N_EXPERTS and all(w.shape[1] % (8 * per) == 0 for w in experts)
        cast_idx = lambda s: (front(s) // per, front(s) % per, 0)
    cast_specs = [pl.BlockSpec((1, w.shape[1] // per, w.shape[2]), cast_idx) for w in experts]
    in_specs += cast_specs
    args = [x2d, pool0, st0, cnt0, tril, sup, *weights, *experts]

    out_shape = [
        jax.ShapeDtypeStruct((total_rows, D_MODEL), F32),
        jax.ShapeDtypeStruct((HN_PLANES, total_rows, LANES), I32),
        jax.ShapeDtypeStruct((total_rows, LANES), F32),
        jax.ShapeDtypeStruct((n_blk, ROUTE_ROWS, tb), F32),
        jax.ShapeDtypeStruct((batch, POOL_ROWS, D_POOL), F32),
        jax.ShapeDtypeStruct((batch, GLA_HEADS, GLA_DK, GLA_DV), F32),
        jax.ShapeDtypeStruct((N_EXPERTS, 1), F32),
        *[jax.ShapeDtypeStruct(w.shape, BF16) for w in experts],
    ]
    out_specs = [
        pl.BlockSpec((tb, D_MODEL), tok_out),
        pl.BlockSpec((HN_PLANES, tb, LANES), lambda s: (0, back(s), 0)),
        pl.BlockSpec((tb, LANES), tok_out),
        pl.BlockSpec((1, ROUTE_ROWS, tb), lambda s: (back(s), 0, 0)),
        pl.BlockSpec((per_blk, POOL_ROWS, D_POOL), lambda s: (stream(s), 0, 0)),
        pl.BlockSpec((per_blk, GLA_HEADS, GLA_DK, GLA_DV), lambda s: (stream(s), 0, 0, 0)),
        pl.BlockSpec((N_EXPERTS, 1), const2),
        *cast_specs,
    ]
    n_grp = GLA_HEADS // HEAD_GROUP
    scratch = [
        pltpu.VMEM((POOL_ROWS + tb, D_POOL), F32),
        pltpu.VMEM((GLA_HEADS, GLA_DK, GLA_DV), F32),
        pltpu.VMEM((tb // CHUNK, n_grp, HEAD_GROUP * CHUNK, HEAD_GROUP * GLA_DK), BF16),
        pltpu.VMEM((tb // CHUNK, n_grp, HEAD_GROUP * CHUNK, HEAD_GROUP * GLA_DV), BF16),
        pltpu.VMEM((tb // CHUNK, n_grp, HEAD_GROUP * GLA_DK, HEAD_GROUP * GLA_DV), BF16),
        pltpu.VMEM((tb, D_V), F32),
        pltpu.VMEM((N_EXPERTS, 1), F32),
        pltpu.VMEM((2, tb, D_MODEL), F32),
        pltpu.VMEM((2, tb, D_MAIN), F32),
        pltpu.VMEM((2, GATE_RANK, tb), F32),
    ]
    return pl.pallas_call(
        functools.partial(_mixer_kernel, tb=tb, nj=nj, lead_pad=lead_pad, chained=chained),
        grid=(n_blk + 1,),
        in_specs=in_specs,
        out_specs=out_specs,
        out_shape=out_shape,
        scratch_shapes=scratch,
        compiler_params=pltpu.CompilerParams(
            dimension_semantics=("arbitrary",), vmem_limit_bytes=VMEM_LIMIT),
        name=f"mixer_tb{tb}_pad{lead_pad}",
    )(*args)


def _transpose_cast_kernel(wt_ref, w_ref):
    w_ref[...] = wt_ref[...].T.astype(BF16)


def _transpose_cast(w_t, n_cols):
    k = w_t.shape[1]
    assert n_cols % WEIGHT_TILE == 0 and n_cols <= w_t.shape[0]
    return pl.pallas_call(
        _transpose_cast_kernel,
        grid=(n_cols // WEIGHT_TILE,),
        in_specs=[pl.BlockSpec((WEIGHT_TILE, k), lambda i: (i, 0))],
        out_specs=pl.BlockSpec((k, WEIGHT_TILE), lambda i: (0, i)),
        out_shape=jax.ShapeDtypeStruct((k, n_cols), BF16),
        compiler_params=pltpu.CompilerParams(dimension_semantics=("arbitrary",)),
        name="weight_transpose_cast",
    )(w_t)


def _sc_mesh():
    return plsc.VectorSubcoreMesh(core_axis_name="c", subcore_axis_name="s",
                                  num_cores=SC_CORES, num_subcores=SC_SUBCORES)


def _sc_worker():
    return lax.axis_index("s") * SC_CORES + lax.axis_index("c")


def _plane_rows(dest, planes, rows_per_plane):
    offs = (jnp.arange(planes, dtype=I32) * rows_per_plane)[None, :, None]
    return dest[:, None, :] + offs


def _dispatch(hn_p, hn_s, dest0, dest1, n_rows):
    planes, t_p, _ = hn_p.shape
    t_s = hn_s.shape[1]
    n_cp = t_p // SC_ROWS
    cp = n_cp // SC_WORKERS
    n_cs = t_s // SC_ROWS
    assert t_p == cp * SC_ROWS * SC_WORKERS and t_s == n_cs * SC_ROWS and n_cs <= SC_WORKERS
    idx0 = _plane_rows(dest0, planes, n_rows)
    idx1 = _plane_rows(dest1, planes, n_rows)

    def body(hnp_hbm, hns_hbm, d0_hbm, d1_hbm, xs_hbm, rows_v, i0_v, i1_v, is0_v, is1_v, sem_in, sem_out):
        wid = _sc_worker()
        pltpu.sync_copy(d0_hbm.at[pl.ds(wid * cp, cp)], i0_v)
        pltpu.sync_copy(d1_hbm.at[pl.ds(wid * cp, cp)], i1_v)

        def move(src_hbm, src_rows, row0, i0, i1, c):
            loads = [pltpu.async_copy(src_hbm.at[pl.ds(p * src_rows + row0, SC_ROWS)], rows_v.at[p], sem_in)
                     for p in range(planes)]
            for cpy in loads:
                cpy.wait()
            stores = []
            for p in range(planes):
                stores.append(pltpu.async_copy(rows_v.at[p], xs_hbm.at[i0.at[c, p]], sem_out))
                stores.append(pltpu.async_copy(rows_v.at[p], xs_hbm.at[i1.at[c, p]], sem_out))
            for cpy in stores:
                cpy.wait()

        @pl.loop(0, cp)
        def _(c):
            move(hnp_hbm, t_p, (wid * cp + c) * SC_ROWS, i0_v, i1_v, c)

        @pl.when(wid < n_cs)
        def _():
            pltpu.sync_copy(d0_hbm.at[pl.ds(n_cp + wid, 1)], is0_v)
            pltpu.sync_copy(d1_hbm.at[pl.ds(n_cp + wid, 1)], is1_v)
            move(hns_hbm, t_s, wid * SC_ROWS, is0_v, is1_v, 0)

    xs = pl.kernel(
        body,
        out_type=jax.ShapeDtypeStruct((planes * n_rows, LANES), I32),
        mesh=_sc_mesh(),
        scratch_types=[
            pltpu.VMEM((planes, SC_ROWS, LANES), I32),
            pltpu.VMEM((cp, planes, SC_ROWS), I32),
            pltpu.VMEM((cp, planes, SC_ROWS), I32),
            pltpu.VMEM((1, planes, SC_ROWS), I32),
            pltpu.VMEM((1, planes, SC_ROWS), I32),
            pltpu.SemaphoreType.DMA,
            pltpu.SemaphoreType.DMA,
        ],
        name="moe_dispatch_sc",
    )(hn_p.reshape(planes * t_p, LANES), hn_s.reshape(planes * t_s, LANES), idx0, idx1)
    return xs.reshape(planes, n_rows, LANES)


def _gather(ys, dest0, dest1):
    planes, n_rows, _ = ys.shape
    n_chunks = dest0.shape[0]
    n_tok = n_chunks * SC_ROWS
    cpw = max(n_chunks // SC_WORKERS, 1)
    assert n_chunks <= SC_WORKERS or n_chunks == cpw * SC_WORKERS
    idx = (_plane_rows(dest0, planes, n_rows), _plane_rows(dest1, planes, n_rows))

    def body(ys_hbm, d0_hbm, d1_hbm, out_hbm, rows_v, i0_v, i1_v, sem_in, sem_out):
        wid = _sc_worker()

        def work():
            pltpu.sync_copy(d0_hbm.at[pl.ds(wid * cpw, cpw)], i0_v)
            pltpu.sync_copy(d1_hbm.at[pl.ds(wid * cpw, cpw)], i1_v)

            @pl.loop(0, cpw)
            def _(c):
                row0 = (wid * cpw + c) * SC_ROWS
                for j, i_v in enumerate((i0_v, i1_v)):
                    loads = [pltpu.async_copy(ys_hbm.at[i_v.at[c, p]], rows_v.at[p], sem_in)
                             for p in range(planes)]
                    for cpy in loads:
                        cpy.wait()
                    stores = [
                        pltpu.async_copy(
                            rows_v.at[p], out_hbm.at[pl.ds((j * planes + p) * n_tok + row0, SC_ROWS)], sem_out)
                        for p in range(planes)]
                    for cpy in stores:
                        cpy.wait()

        if n_chunks < SC_WORKERS:
            pl.when(wid < n_chunks)(work)
        else:
            work()

    out = pl.kernel(
        body,
        out_type=jax.ShapeDtypeStruct((2 * planes * n_tok, LANES), ys.dtype),
        mesh=_sc_mesh(),
        scratch_types=[
            pltpu.VMEM((planes, SC_ROWS, LANES), ys.dtype),
            pltpu.VMEM((cpw, planes, SC_ROWS), I32),
            pltpu.VMEM((cpw, planes, SC_ROWS), I32),
            pltpu.SemaphoreType.DMA,
            pltpu.SemaphoreType.DMA,
        ],
        name="moe_gather_sc",
    )(ys.reshape(planes * n_rows, LANES), *idx)
    return out.reshape(2, planes, n_tok, LANES)


def _expert_kernel(be_ref, bv_ref, nb_ref, xs_ref, wg_ref, wu_ref, wd_ref, ys_ref):
    del be_ref
    i = pl.program_id(0)
    live = i < nb_ref[0]

    @pl.when(live)
    def _():
        valid = lax.broadcasted_iota(I32, (MOE_BLK, LANES), 0) < bv_ref[i]
        xb = _unpack_planes([jnp.where(valid, xs_ref[p], 0) for p in range(HN_PLANES)]).astype(BF16)
        gate = _dot(xb, wg_ref[0])
        hmid = gate * jax.nn.sigmoid(gate) * _dot(xb, wu_ref[0])
        _pack_planes(_dot(hmid.astype(BF16), wd_ref[0]), ys_ref)

    @pl.when(jnp.logical_not(live))
    def _():
        ys_ref[...] = jnp.zeros_like(ys_ref)


def _experts(block_e, block_valid, nb, xs, w_eg, w_eu, w_ed):
    n_rows = xs.shape[1]
    n_blocks = n_rows // MOE_BLK
    row_idx = lambda i, be, bv, nb: (0, jnp.minimum(i, nb[0] - 1), 0)
    w_idx = lambda i, be, bv, nb: (be[i], 0, 0)
    grid_spec = pltpu.PrefetchScalarGridSpec(
        num_scalar_prefetch=3,
        grid=(n_blocks,),
        in_specs=[
            pl.BlockSpec((HN_PLANES, MOE_BLK, LANES), row_idx),
            pl.BlockSpec((1, D_MODEL, D_EXPERT), w_idx),
            pl.BlockSpec((1, D_MODEL, D_EXPERT), w_idx),
            pl.BlockSpec((1, D_EXPERT, D_MODEL), w_idx),
        ],
        out_specs=pl.BlockSpec((HN_PLANES, MOE_BLK, LANES), lambda i, be, bv, nb: (0, i, 0)),
    )
    return pl.pallas_call(
        _expert_kernel,
        grid_spec=grid_spec,
        out_shape=jax.ShapeDtypeStruct((HN_PLANES, n_rows, LANES), I32),
        compiler_params=pltpu.CompilerParams(
            dimension_semantics=("arbitrary",), vmem_limit_bytes=VMEM_LIMIT),
        name="moe_experts",
    )(block_e, block_valid, nb, xs, w_eg, w_eu, w_ed)


def _combine_kernel(h_ref, route_ref, nfin_ref, rows_ref, *rest):
    y_ref = rest[-1]
    route = route_ref[...]
    ys1 = _unpack_planes([rows_ref[0, p] for p in range(HN_PLANES)])
    ys2 = _unpack_planes([rows_ref[1, p] for p in range(HN_PLANES)])
    out = h_ref[...] + (ys1 * route[:, 2:3] + ys2 * route[:, 3:4])
    y_ref[...] = _rms(out, nfin_ref[...])


def _combine(h, route, norm_final, rows, *, row0=0, y_prev=None):
    total = h.shape[0]
    tb = COMBINE_TB
    blk0 = row0 // tb
    args = [h, route, norm_final, rows]
    in_specs = [
        pl.BlockSpec((tb, D_MODEL), lambda i: (blk0 + i, 0)),
        pl.BlockSpec((tb, LANES), lambda i: (blk0 + i, 0)),
        pl.BlockSpec((1, D_MODEL), lambda i: (0, 0)),
        pl.BlockSpec((2, HN_PLANES, tb, LANES), lambda i: (0, 0, i, 0)),
    ]
    aliases = {}
    if y_prev is not None:
        aliases[len(args)] = 0
        args.append(y_prev)
        in_specs.append(pl.BlockSpec(memory_space=pl.ANY))
    return pl.pallas_call(
        _combine_kernel,
        grid=(rows.shape[2] // tb,),
        in_specs=in_specs,
        out_specs=pl.BlockSpec((tb, D_MODEL), lambda i: (blk0 + i, 0)),
        out_shape=jax.ShapeDtypeStruct((total, D_MODEL), F32),
        input_output_aliases=aliases,
        compiler_params=pltpu.CompilerParams(
            dimension_semantics=("arbitrary",), vmem_limit_bytes=VMEM_LIMIT),
        name="moe_combine",
    )(*args)


def kernel(x_prompt, x_sample, state_pool, state_gla, meta_tokens, norm_mix, w_in, w_gate_up, b_gate, w_pool, pool_scale, gla_norm, w_out, norm_ffn, w_router_group, b_router_group, w_router_expert, b_router_expert, w_expert_gate, w_expert_up, w_expert_down, norm_final):
    assert w_in.shape[0] == 1, "one encoder layer"
    batch, seq, _ = x_prompt.shape
    dec_batch, dec_seq, _ = x_sample.shape
    assert seq % MIX_TB == 0 and dec_seq == CHUNK and N_META <= CHUNK
    t_prompt = batch * seq
    t_sample = dec_batch * dec_seq
    t_all = t_prompt + t_sample
    assert t_prompt % COMBINE_TB == 0 and t_sample % COMBINE_TB == 0

    w_in_t = w_in[0].T
    gpad = EXPERTS_PER_GROUP - N_GROUPS
    rpad = ROUTER_ROWS - EXPERTS_PER_GROUP - N_EXPERTS
    w_router = jnp.concatenate([
        w_router_group[0].T, jnp.zeros((gpad, D_MODEL), F32),
        w_router_expert[0].T, jnp.zeros((rpad, D_MODEL), F32)], axis=0)
    b_router = jnp.concatenate([
        b_router_group[0], jnp.zeros((gpad,), F32), b_router_expert[0], jnp.zeros((rpad,), F32)])
    zg = jnp.zeros((POOL_GROUP_DIM, POOL_GROUP_DIM), F32)
    w_pool_pairs = jnp.stack([
        jnp.block([[w_pool[0, 2 * i], zg], [zg, w_pool[0, 2 * i + 1]]]) for i in range(len(POOL_WINDOWS) // 2)])
    weights = (
        norm_mix[0][None, :],
        _transpose_cast(w_in_t, D_MAIN),
        w_in_t[D_MAIN:].astype(BF16),
        w_gate_up[0].astype(BF16),
        b_gate[0][None, :],
        w_pool_pairs.astype(BF16),
        pool_scale[0][None, :],
        gla_norm[0][None, :],
        w_out[0].astype(BF16),
        norm_ffn[0][None, :],
        w_router.astype(BF16),
        b_router[:, None],
    )

    zero_cnt = jnp.zeros((N_EXPERTS, 1), F32)
    x_meta = jnp.pad(meta_tokens.astype(F32), ((CHUNK - N_META, 0), (0, 0)))
    meta = _mixer(x_meta, jnp.zeros((1, POOL_ROWS, D_POOL), F32),
                  jnp.zeros((1, GLA_HEADS, GLA_DK, GLA_DV), F32), zero_cnt,
                  weights, batch=1, seq=CHUNK, tb=CHUNK, lead_pad=CHUNK - N_META)
    pool_m, st_m = meta[4], meta[5]
    h_p, hn_p, route_p, rt_p, pool_p, st_p, cnt_p, w_eg, w_eu, w_ed = _mixer(
        x_prompt.reshape(t_prompt, D_MODEL), pool_m, st_m, zero_cnt, weights,
        batch=batch, seq=seq, tb=MIX_TB, lead_pad=0,
        experts=(w_expert_gate[0], w_expert_up[0], w_expert_down[0]))
    pool_s0 = jnp.pad(state_pool[0], ((0, 0), (POOL_ROWS - POOL_PAD, 0), (0, 0)))
    h_s, hn_s, route_s, rt_s, pool_s, st_s, cnt_s = _mixer(
        x_sample.reshape(t_sample, D_MODEL), pool_s0, state_gla[0].astype(F32),
        cnt_p, weights, batch=dec_batch, seq=dec_seq, tb=SAMPLE_TB, lead_pad=0)[:7]

    counts = cnt_s[:, 0].astype(I32)
    padded = (counts + MOE_BLK - 1) // MOE_BLK * MOE_BLK
    ends = jnp.cumsum(padded)
    pstart = ends - padded
    n_blocks = (2 * t_all + N_EXPERTS * (MOE_BLK - 1) + MOE_BLK - 1) // MOE_BLK
    nb = (ends[-1] // MOE_BLK).astype(I32)
    blk_ids = jnp.minimum(jnp.arange(n_blocks, dtype=I32), nb - 1)
    block_e = jnp.sum((ends[None, :] <= (blk_ids * MOE_BLK)[:, None]).astype(I32), axis=1)
    block_e = jnp.minimum(block_e, N_EXPERTS - 1)
    owner = block_e[:, None] == jnp.arange(N_EXPERTS, dtype=I32)
    row_end = jnp.sum(jnp.where(owner, pstart + counts, 0), axis=1)
    block_valid = jnp.clip(row_end - blk_ids * MOE_BLK, 0, MOE_BLK)

    def dest_rows(rt):
        rt = rt.transpose(1, 0, 2).reshape(ROUTE_ROWS, -1)
        onehot = rt[0:2].astype(I32)[..., None] == jnp.arange(N_EXPERTS, dtype=I32)
        return jnp.sum(jnp.where(onehot, pstart, 0), axis=-1) + rt[4:6].astype(I32)

    dest = jnp.concatenate([dest_rows(rt_p), dest_rows(rt_s)], axis=1)
    dest0 = dest[0].reshape(t_all // SC_ROWS, SC_ROWS)
    dest1 = dest[1].reshape(t_all // SC_ROWS, SC_ROWS)
    xs = _dispatch(hn_p, hn_s, dest0, dest1, n_blocks * MOE_BLK)
    ys = _experts(block_e, block_valid.astype(I32), nb[None], xs, w_eg, w_eu, w_ed)
    nfin = norm_final[None, :]
    cp_chunks = t_prompt // SC_ROWS
    part = cp_chunks // COMBINE_PARTS
    y_prompt = None
    for i in range(COMBINE_PARTS):
        ch = slice(i * part, (i + 1) * part)
        rows_i = _gather(ys, dest0[ch], dest1[ch])
        y_prompt = _combine(h_p, route_p, nfin, rows_i, row0=i * part * SC_ROWS, y_prev=y_prompt)
    y_prompt = y_prompt.reshape(batch, seq, D_MODEL)
    rows_s = _gather(ys, dest0[cp_chunks:], dest1[cp_chunks:])
    y_sample = _combine(h_s, route_s, nfin, rows_s).reshape(dec_batch, dec_seq, D_MODEL)
    new_pool_prompt = pool_p[:, POOL_ROWS - POOL_PAD:][None]
    new_gla_prompt = st_p[None]
    new_pool_sample = pool_s[:, POOL_ROWS - POOL_PAD:][None]
    new_gla_sample = st_s[None]
    return (y_prompt, y_sample, new_pool_prompt, new_gla_prompt, new_pool_sample, new_gla_sample)
```

```python
import functools

import jax
import jax.numpy as jnp
from jax import lax
from jax.experimental import pallas as pl
from jax.experimental.pallas import tpu as pltpu
from jax.experimental.pallas import tpu_sc as plsc

F32 = jnp.float32
BF16 = jnp.bfloat16
U32 = jnp.uint32
I32 = jnp.int32

D_MODEL = 1024
N_META = 16
CHUNK = 64
EPS = 1e-6
D_POOL = 512
POOL_WINDOWS = (2, 4, 8, 16)
POOL_GROUP_DIM = 128
POOL_PAD = 15
POOL_ROWS = 16
GLA_HEADS = 4
GLA_DK = 64
GLA_DV = 128
D_QK = 256
D_V = 512
HEAD_GROUP = 2
GATE_RANK = 16
GATE_TAU = 16.0
D_MAIN = D_POOL + 2 * D_QK + 2 * D_V
N_GROUPS = 4
EXPERTS_PER_GROUP = 8
N_EXPERTS = 32
D_EXPERT = 512

LANES = 128
MXU_DIM = 256
HALF = D_MODEL // 2
HN_PLANES = HALF // LANES
ROUTE_ROWS = 8
ROUTER_ROWS = 64
MIX_TB = 512
SAMPLE_TB = 256
FRONT_TILE = 256
FRONT_PLAN = (3, 0, 1, 2, 2, 0)
WEIGHT_TILE = 512
MOE_BLK = 768
COMBINE_TB = 1024
COMBINE_PARTS = 4
VMEM_LIMIT = 48 * 1024 * 1024
SC_CORES = 2
SC_SUBCORES = 16
SC_WORKERS = SC_CORES * SC_SUBCORES
SC_ROWS = 128


def _rms(x, g):
    return x * lax.rsqrt(jnp.mean(x * x, axis=-1, keepdims=True) + EPS) * g


def _dot(a, b):
    return jnp.dot(a, b, preferred_element_type=F32)


def _dot_nt(a, b):
    return lax.dot_general(a, b, (((1,), (1,)), ((), ())), preferred_element_type=F32)


def _pack_planes(x, ref):
    xb = x.astype(BF16)
    lo = lax.bitcast_convert_type(xb[:, :HALF].astype(F32), U32) >> 16
    hi = lax.bitcast_convert_type(xb[:, HALF:].astype(F32), U32) & jnp.uint32(0xFFFF0000)
    packed = lax.bitcast_convert_type(lo | hi, I32)
    for p in range(HN_PLANES):
        ref[p] = packed[:, p * LANES:(p + 1) * LANES]


def _unpack_planes(planes):
    words = [lax.bitcast_convert_type(p, U32) for p in planes]
    los = [lax.bitcast_convert_type(w << 16, F32) for w in words]
    his = [lax.bitcast_convert_type(w & jnp.uint32(0xFFFF0000), F32) for w in words]
    return jnp.concatenate(los + his, axis=-1)


def _dot_tn(a, b):
    return lax.dot_general(a, b, (((0,), (0,)), ((), ())), preferred_element_type=F32)


def _mixer_kernel(*refs, **static):
    s = pl.program_id(0)
    pl.when(s == 0)(lambda: _mixer_first_step(*refs, **static))
    pl.when(s > 0)(lambda: _mixer_step(*refs, **static))


def _mixer_first_step(x_ref, pool0_ref, st0_ref, cnt0_ref, tril_ref, sup_ref,
                      nmix_ref, wmain_ref, wz_ref, wgu_ref, bgate_ref, wpool_ref, pscale_ref,
                      gnorm_ref, wout_ref, nffn_ref, wr_ref, br_ref, eg_ref, eu_ref, ed_ref,
                      h_ref, hn_ref, route_ref, route_t_ref, pool_out_ref, st_out_ref, cnt_out_ref,
                      eg_out_ref, eu_out_ref, ed_out_ref,
                      ext_ref, st_ref, kbd_ref, vbd_ref, sbd_ref, o_ref, cnt_ref,
                      xs_ref, proj_ref, z_ref, *, tb, nj, lead_pad, chained):
    x_new = x_ref[...]
    xn = _rms(x_new, nmix_ref[...]).astype(BF16)
    xs_ref[0] = x_new
    for t in range(D_MAIN // FRONT_TILE):
        cols = slice(t * FRONT_TILE, (t + 1) * FRONT_TILE)
        proj_ref[0, :, cols] = _dot(xn, wmain_ref[:, cols])
    z_ref[0] = _dot_nt(wz_ref[...], xn)
    eg_out_ref[...] = eg_ref[...].astype(BF16)
    eu_out_ref[...] = eu_ref[...].astype(BF16)
    ed_out_ref[...] = ed_ref[...].astype(BF16)


def _mixer_step(x_ref, pool0_ref, st0_ref, cnt0_ref, tril_ref, sup_ref,
                nmix_ref, wmain_ref, wz_ref, wgu_ref, bgate_ref, wpool_ref, pscale_ref,
                gnorm_ref, wout_ref, nffn_ref, wr_ref, br_ref, eg_ref, eu_ref, ed_ref,
                h_ref, hn_ref, route_ref, route_t_ref, pool_out_ref, st_out_ref, cnt_out_ref,
                eg_out_ref, eu_out_ref, ed_out_ref,
                ext_ref, st_ref, kbd_ref, vbd_ref, sbd_ref, o_ref, cnt_ref,
                xs_ref, proj_ref, z_ref, *, tb, nj, lead_pad, chained):
    s = pl.program_id(0)
    back = s - 1
    j = lax.rem(back, nj)
    n_chunks = tb // CHUNK
    wr_slot = lax.rem(s, 2)
    rd_slot = 1 - wr_slot

    @pl.when(back == 0)
    def _():
        kbd_ref[...] = jnp.zeros_like(kbd_ref)
        vbd_ref[...] = jnp.zeros_like(vbd_ref)
        sbd_ref[...] = jnp.zeros_like(sbd_ref)
        cnt_ref[...] = cnt0_ref[...]

    def put_state(c, hh, st):
        gg, hp = divmod(hh, HEAD_GROUP)
        sbd_ref[c, gg, hp * GLA_DK:(hp + 1) * GLA_DK, hp * GLA_DV:(hp + 1) * GLA_DV] = st.astype(BF16)

    if chained:
        @pl.when(j == 0)
        def _():
            ext_ref[0:POOL_ROWS, :] = pool0_ref[0]
            st_ref[...] = st0_ref[0]

        for hh in range(GLA_HEADS):
            put_state(0, hh, st_ref[hh])
    else:
        for c in range(n_chunks):
            for hh in range(GLA_HEADS):
                put_state(c, hh, st0_ref[c, hh])

    x_new = x_ref[...]
    xn = _rms(x_new, nmix_ref[...]).astype(BF16)
    xs_ref[wr_slot] = x_new

    tiles_done = [0]

    def front_tiles(stage):
        for t in range(tiles_done[0], tiles_done[0] + FRONT_PLAN[stage]):
            cols = slice(t * FRONT_TILE, (t + 1) * FRONT_TILE)
            proj_ref[wr_slot, :, cols] = _dot(xn, wmain_ref[:, cols])
        tiles_done[0] += FRONT_PLAN[stage]

    front_tiles(0)
    z_ref[wr_slot] = _dot_nt(wz_ref[...], xn)

    x = xs_ref[rd_slot]
    z = z_ref[rd_slot]
    u = proj_ref[rd_slot, :, 0:D_POOL]
    q = proj_ref[rd_slot, :, D_POOL:D_POOL + D_QK]
    k = proj_ref[rd_slot, :, D_POOL + D_QK:D_POOL + 2 * D_QK]
    v = proj_ref[rd_slot, :, D_POOL + 2 * D_QK:D_POOL + 2 * D_QK + D_V]
    r = proj_ref[rd_slot, :, D_POOL + 2 * D_QK + D_V:D_MAIN]

    row = lax.broadcasted_iota(I32, (tb, 1), 0)

    pseg = POOL_ROWS + CHUNK
    if chained:
        ext_ref[POOL_ROWS:POOL_ROWS + tb, :] = u
        ext = ext_ref[...]
    else:
        ext = jnp.concatenate(
            [blk for c in range(n_chunks) for blk in (pool0_ref[c], u[c * CHUNK:(c + 1) * CHUNK])], axis=0)
    pooled = []
    for g, w in enumerate(POOL_WINDOWS):
        sl = slice(g * POOL_GROUP_DIM, (g + 1) * POOL_GROUP_DIM)
        acc = ext[:, sl]
        for d in range(g + 1):
            acc = acc + pltpu.roll(acc, 1 << d, axis=0)
        if chained:
            win = acc[POOL_ROWS:, :]
        else:
            win = jnp.concatenate([acc[c * pseg + POOL_ROWS:(c + 1) * pseg] for c in range(n_chunks)], axis=0)
        if lead_pad:
            cnt = jnp.clip(row - lead_pad + 1, 1, w).astype(F32)
            pooled.append(win / cnt - u[:, sl])
        else:
            pooled.append(win * (1.0 / w) - u[:, sl])
    pys = [_dot(jnp.concatenate(pooled[2 * i:2 * i + 2], axis=-1).astype(BF16), wpool_ref[i])
           for i in range(len(POOL_WINDOWS) // 2)]
    pool_y = jnp.concatenate(pys, axis=-1) * pscale_ref[...]
    if chained:
        ext_ref[0:POOL_ROWS, :] = ext_ref[tb:tb + POOL_ROWS, :]
    else:
        for c in range(n_chunks):
            pool_out_ref[c] = u[(c + 1) * CHUNK - POOL_ROWS:(c + 1) * CHUNK]

    gpre = _dot_tn(z.astype(BF16), wgu_ref[...]) + bgate_ref[...]
    log_a = (jnp.minimum(gpre, 0.0) - jnp.log(1.0 + jnp.exp(-jnp.abs(gpre)))) * (1.0 / GATE_TAU)
    if lead_pad:
        log_a = jnp.where(row >= lead_pad, log_a, 0.0)
    a_hi = log_a.astype(BF16)
    a_lo = (log_a - a_hi.astype(F32)).astype(BF16)
    tril = tril_ref[...]
    seg = tril.shape[0]
    bcum = jnp.concatenate(
        [_dot(tril, a_hi[r0:r0 + seg]) + _dot(tril, a_lo[r0:r0 + seg]) for r0 in range(0, tb, seg)], axis=0)
    front_tiles(1)
    eb = jnp.exp(bcum)
    qi = q * (GLA_DK ** -0.5) * eb
    ki = k * jnp.exp(-bcum)

    rr = lax.broadcasted_iota(I32, (CHUNK, HEAD_GROUP * CHUNK), 0)
    cc = lax.broadcasted_iota(I32, (CHUNK, HEAD_GROUP * CHUNK), 1)
    causal = (cc % CHUNK) <= rr

    lasts = [eb[(c + 1) * CHUNK - 1:(c + 1) * CHUNK, :] for c in range(n_chunks)]
    dcol = jnp.concatenate(lasts + [jnp.zeros((LANES - n_chunks, D_QK), F32)], axis=0).T

    n_grp = GLA_HEADS // HEAD_GROUP
    chunk_rows = [slice(c * CHUNK, (c + 1) * CHUNK) for c in range(n_chunks)]
    grp_k = [slice(g * HEAD_GROUP * GLA_DK, (g + 1) * HEAD_GROUP * GLA_DK) for g in range(n_grp)]
    grp_v = [slice(g * HEAD_GROUP * GLA_DV, (g + 1) * HEAD_GROUP * GLA_DV) for g in range(n_grp)]
    qi_b = qi.astype(BF16)
    ki_b = ki.astype(BF16)
    v_b = v.astype(BF16)

    scores = {}
    for c in range(n_chunks):
        for hh in range(GLA_HEADS):
            gg, hp = divmod(hh, HEAD_GROUP)
            kbd_ref[c, gg, hp * CHUNK:(hp + 1) * CHUNK, hp * GLA_DK:(hp + 1) * GLA_DK] = (
                ki_b[chunk_rows[c], hh * GLA_DK:(hh + 1) * GLA_DK])
            vbd_ref[c, gg, hp * CHUNK:(hp + 1) * CHUNK, hp * GLA_DV:(hp + 1) * GLA_DV] = (
                v_b[chunk_rows[c], hh * GLA_DV:(hh + 1) * GLA_DV])
        for gg in range(n_grp):
            scores[c, gg] = _dot_nt(qi_b[chunk_rows[c], grp_k[gg]], kbd_ref[c, gg])

    kvs = {}
    for c in range(n_chunks):
        for hh in range(GLA_HEADS):
            kvs[c, hh] = _dot_tn(ki_b[chunk_rows[c], hh * GLA_DK:(hh + 1) * GLA_DK],
                                 v_b[chunk_rows[c], hh * GLA_DV:(hh + 1) * GLA_DV])
    front_tiles(2)

    for hh in range(GLA_HEADS):
        st = st_ref[hh] if chained else None
        for c in range(n_chunks):
            s_old = st if chained else st0_ref[c, hh]
            s_new = (s_old + kvs[c, hh]) * dcol[hh * GLA_DK:(hh + 1) * GLA_DK, c:c + 1]
            if not chained:
                st_out_ref[c, hh] = s_new
            else:
                st = s_new
                if c + 1 < n_chunks:
                    put_state(c + 1, hh, s_new)
        if chained:
            st_ref[hh] = st

    for c in range(n_chunks):
        for gg in range(n_grp):
            p = jnp.where(causal, scores[c, gg], 0.0).astype(BF16)
            o_ref[chunk_rows[c], grp_v[gg]] = (
                _dot(p, vbd_ref[c, gg]) + _dot(qi_b[chunk_rows[c], grp_k[gg]], sbd_ref[c, gg]))

    o = o_ref[...]
    ons = []
    for hh in range(GLA_HEADS):
        oh = o[:, hh * GLA_DV:(hh + 1) * GLA_DV]
        ons.append(oh * lax.rsqrt(jnp.mean(oh * oh, axis=-1, keepdims=True) + EPS))
    og = jnp.concatenate(ons, axis=-1) * gnorm_ref[...] * (r * jax.nn.sigmoid(r))
    mix = _dot(jnp.concatenate([pool_y, og], axis=-1).astype(BF16), wout_ref[...])
    front_tiles(3)

    eg_out_ref[...] = eg_ref[...].astype(BF16)
    eu_out_ref[...] = eu_ref[...].astype(BF16)
    ed_out_ref[...] = ed_ref[...].astype(BF16)

    h = x + mix
    h_ref[...] = h
    hn = _rms(h, nffn_ref[...]).astype(BF16)
    _pack_planes(hn, hn_ref)

    logits = _dot_nt(wr_ref[...], hn) + br_ref[...]
    front_tiles(4)
    sub = lax.broadcasted_iota(I32, (EXPERTS_PER_GROUP, tb), 0).astype(F32)
    neg = jnp.float32(-jnp.inf)
    big = jnp.float32(EXPERTS_PER_GROUP)
    tile0 = logits[0:EXPERTS_PER_GROUP]
    is_g = sub < N_GROUPS
    gmax = jnp.max(jnp.where(is_g, tile0, neg), axis=0, keepdims=True)
    gsum = jnp.sum(jnp.where(is_g, jnp.exp(tile0 - gmax), 0.0), axis=0, keepdims=True)
    p_g = 1.0 / gsum
    gidx = jnp.min(jnp.where(is_g & (tile0 == gmax), sub, big), axis=0, keepdims=True)
    el = logits[N_GROUPS * EXPERTS_PER_GROUP:(N_GROUPS + 1) * EXPERTS_PER_GROUP]
    for g in range(N_GROUPS - 2, -1, -1):
        el = jnp.where(gidx == g, logits[(g + 1) * EXPERTS_PER_GROUP:(g + 2) * EXPERTS_PER_GROUP], el)
    m1 = jnp.max(el, axis=0, keepdims=True)
    i1 = jnp.min(jnp.where(el == m1, sub, big), axis=0, keepdims=True)
    rest = sub != i1
    m2 = jnp.max(jnp.where(rest, el, neg), axis=0, keepdims=True)
    i2 = jnp.min(jnp.where(rest & (el == m2), sub, big), axis=0, keepdims=True)
    t2 = jnp.exp(m2 - m1)
    den = 1.0 + t2
    g1 = p_g / den
    g2 = p_g * t2 / den
    e1 = gidx * EXPERTS_PER_GROUP + i1
    e2 = gidx * EXPERTS_PER_GROUP + i2

    eid = lax.broadcasted_iota(I32, (N_EXPERTS, tb), 0).astype(F32)
    oh1 = eid == e1
    oh2 = eid == e2
    both = jnp.where(oh1 | oh2, 1.0, 0.0)
    cnt = cnt_ref[...]
    before = _dot(both.astype(BF16), sup_ref[...]) + cnt
    front_tiles(5)
    assert tiles_done[0] * FRONT_TILE == D_MAIN
    pos1 = jnp.sum(jnp.where(oh1, before, 0.0), axis=0, keepdims=True)
    pos2 = jnp.sum(jnp.where(oh2, before, 0.0), axis=0, keepdims=True)
    cnt_new = cnt + jnp.sum(both, axis=1, keepdims=True)
    cnt_ref[...] = cnt_new
    cnt_out_ref[...] = cnt_new

    zero = jnp.zeros_like(e1)
    route_t = jnp.concatenate([e1, e2, g1, g2, pos1, pos2, zero, zero], axis=0)
    route_t_ref[0] = route_t
    route_ref[...] = jnp.concatenate([route_t, jnp.zeros((LANES - ROUTE_ROWS, tb), F32)], axis=0).T

    if chained:
        @pl.when(j == nj - 1)
        def _():
            pool_out_ref[0] = ext_ref[0:POOL_ROWS, :]
            st_out_ref[0] = st_ref[...]


def _mixer(x2d, pool0, st0, cnt0, weights, *, batch, seq, tb, lead_pad, experts=None):
    chained = tb <= seq
    total_rows = batch * seq
    n_blk = total_rows // tb
    nj = seq // tb if chained else 1
    per_blk = 1 if chained else tb // seq
    assert chained or (seq == CHUNK and batch % per_blk == 0)
    shared = pool0.shape[0] == 1
    front = lambda s: jnp.minimum(s, n_blk - 1)
    back = lambda s: jnp.maximum(s - 1, 0)
    stream = lambda s: back(s) // nj
    st_idx = (lambda s: (0, 0, 0)) if shared else (lambda s: (stream(s), 0, 0))
    gla_idx = (lambda s: (0, 0, 0, 0)) if shared else (lambda s: (stream(s), 0, 0, 0))
    const2 = lambda s: (0, 0)
    tok_out = lambda s: (back(s), 0)

    seg = min(tb, MXU_DIM)
    ii = jnp.arange(seg)
    tril = ((ii[:, None] >= ii[None, :]) & (ii[:, None] // CHUNK == ii[None, :] // CHUNK)).astype(BF16)
    ii = jnp.arange(tb)
    sup = (ii[:, None] < ii[None, :]).astype(BF16)

    in_specs = [
        pl.BlockSpec((tb, D_MODEL), lambda s: (front(s), 0)),
        pl.BlockSpec((per_blk, POOL_ROWS, D_POOL), st_idx),
        pl.BlockSpec((per_blk, GLA_HEADS, GLA_DK, GLA_DV), gla_idx),
        pl.BlockSpec((N_EXPERTS, 1), const2),
        pl.BlockSpec((seg, seg), const2),
        pl.BlockSpec((tb, tb), const2),
    ]
    for wgt in weights:
        in_specs.append(pl.BlockSpec(wgt.shape, (lambda s, n=wgt.ndim: (0,) * n)))
    if experts is None:
        experts = (jnp.zeros((1, 8, LANES), F32),) * 3
        per = 1
        cast_idx = lambda s: (0, 0, 0)
    else:
        per = n_blk // N_EXPERTS
        assert n_blk == per * N_EXPERTS and all(w.shape[1] % (8 * per) == 0 for w in experts)
        cast_idx = lambda s: (front(s) // per, front(s) % per, 0)
    cast_specs = [pl.BlockSpec((1, w.shape[1] // per, w.shape[2]), cast_idx) for w in experts]
    in_specs += cast_specs
    args = [x2d, pool0, st0, cnt0, tril, sup, *weights, *experts]

    out_shape = [
        jax.ShapeDtypeStruct((total_rows, D_MODEL), F32),
        jax.ShapeDtypeStruct((HN_PLANES, total_rows, LANES), I32),
        jax.ShapeDtypeStruct((total_rows, LANES), F32),
        jax.ShapeDtypeStruct((n_blk, ROUTE_ROWS, tb), F32),
        jax.ShapeDtypeStruct((batch, POOL_ROWS, D_POOL), F32),
        jax.ShapeDtypeStruct((batch, GLA_HEADS, GLA_DK, GLA_DV), F32),
        jax.ShapeDtypeStruct((N_EXPERTS, 1), F32),
        *[jax.ShapeDtypeStruct(w.shape, BF16) for w in experts],
    ]
    out_specs = [
        pl.BlockSpec((tb, D_MODEL), tok_out),
        pl.BlockSpec((HN_PLANES, tb, LANES), lambda s: (0, back(s), 0)),
        pl.BlockSpec((tb, LANES), tok_out),
        pl.BlockSpec((1, ROUTE_ROWS, tb), lambda s: (back(s), 0, 0)),
        pl.BlockSpec((per_blk, POOL_ROWS, D_POOL), lambda s: (stream(s), 0, 0)),
        pl.BlockSpec((per_blk, GLA_HEADS, GLA_DK, GLA_DV), lambda s: (stream(s), 0, 0, 0)),
        pl.BlockSpec((N_EXPERTS, 1), const2),
        *cast_specs,
    ]
    n_grp = GLA_HEADS // HEAD_GROUP
    scratch = [
        pltpu.VMEM((POOL_ROWS + tb, D_POOL), F32),
        pltpu.VMEM((GLA_HEADS, GLA_DK, GLA_DV), F32),
        pltpu.VMEM((tb // CHUNK, n_grp, HEAD_GROUP * CHUNK, HEAD_GROUP * GLA_DK), BF16),
        pltpu.VMEM((tb // CHUNK, n_grp, HEAD_GROUP * CHUNK, HEAD_GROUP * GLA_DV), BF16),
        pltpu.VMEM((tb // CHUNK, n_grp, HEAD_GROUP * GLA_DK, HEAD_GROUP * GLA_DV), BF16),
        pltpu.VMEM((tb, D_V), F32),
        pltpu.VMEM((N_EXPERTS, 1), F32),
        pltpu.VMEM((2, tb, D_MODEL), F32),
        pltpu.VMEM((2, tb, D_MAIN), F32),
        pltpu.VMEM((2, GATE_RANK, tb), F32),
    ]
    return pl.pallas_call(
        functools.partial(_mixer_kernel, tb=tb, nj=nj, lead_pad=lead_pad, chained=chained),
        grid=(n_blk + 1,),
        in_specs=in_specs,
        out_specs=out_specs,
        out_shape=out_shape,
        scratch_shapes=scratch,
        compiler_params=pltpu.CompilerParams(
            dimension_semantics=("arbitrary",), vmem_limit_bytes=VMEM_LIMIT),
        name=f"mixer_tb{tb}_pad{lead_pad}",
    )(*args)


def _transpose_cast_kernel(wt_ref, w_ref):
    w_ref[...] = wt_ref[...].T.astype(BF16)


def _transpose_cast(w_t, n_cols):
    k = w_t.shape[1]
    assert n_cols % WEIGHT_TILE == 0 and n_cols <= w_t.shape[0]
    return pl.pallas_call(
        _transpose_cast_kernel,
        grid=(n_cols // WEIGHT_TILE,),
        in_specs=[pl.BlockSpec((WEIGHT_TILE, k), lambda i: (i, 0))],
        out_specs=pl.BlockSpec((k, WEIGHT_TILE), lambda i: (0, i)),
        out_shape=jax.ShapeDtypeStruct((k, n_cols), BF16),
        compiler_params=pltpu.CompilerParams(dimension_semantics=("arbitrary",)),
        name="weight_transpose_cast",
    )(w_t)


def _sc_mesh():
    return plsc.VectorSubcoreMesh(core_axis_name="c", subcore_axis_name="s",
                                  num_cores=SC_CORES, num_subcores=SC_SUBCORES)


def _sc_worker():
    return lax.axis_index("s") * SC_CORES + lax.axis_index("c")


def _plane_rows(dest, planes, rows_per_plane):
    offs = (jnp.arange(planes, dtype=I32) * rows_per_plane)[None, :, None]
    return dest[:, None, :] + offs


def _dispatch(hn_p, hn_s, dest0, dest1, n_rows):
    planes, t_p, _ = hn_p.shape
    t_s = hn_s.shape[1]
    n_cp = t_p // SC_ROWS
    cp = n_cp // SC_WORKERS
    n_cs = t_s // SC_ROWS
    assert t_p == cp * SC_ROWS * SC_WORKERS and t_s == n_cs * SC_ROWS and n_cs <= SC_WORKERS
    idx0 = _plane_rows(dest0, planes, n_rows)
    idx1 = _plane_rows(dest1, planes, n_rows)

    def body(hnp_hbm, hns_hbm, d0_hbm, d1_hbm, xs_hbm, rows_v, i0_v, i1_v, is0_v, is1_v, sem_in, sem_out):
        wid = _sc_worker()
        pltpu.sync_copy(d0_hbm.at[pl.ds(wid * cp, cp)], i0_v)
        pltpu.sync_copy(d1_hbm.at[pl.ds(wid * cp, cp)], i1_v)

        def move(src_hbm, src_rows, row0, i0, i1, c):
            loads = [pltpu.async_copy(src_hbm.at[pl.ds(p * src_rows + row0, SC_ROWS)], rows_v.at[p], sem_in)
                     for p in range(planes)]
            for cpy in loads:
                cpy.wait()
            stores = []
            for p in range(planes):
                stores.append(pltpu.async_copy(rows_v.at[p], xs_hbm.at[i0.at[c, p]], sem_out))
                stores.append(pltpu.async_copy(rows_v.at[p], xs_hbm.at[i1.at[c, p]], sem_out))
            for cpy in stores:
                cpy.wait()

        @pl.loop(0, cp)
        def _(c):
            move(hnp_hbm, t_p, (wid * cp + c) * SC_ROWS, i0_v, i1_v, c)

        @pl.when(wid < n_cs)
        def _():
            pltpu.sync_copy(d0_hbm.at[pl.ds(n_cp + wid, 1)], is0_v)
            pltpu.sync_copy(d1_hbm.at[pl.ds(n_cp + wid, 1)], is1_v)
            move(hns_hbm, t_s, wid * SC_ROWS, is0_v, is1_v, 0)

    xs = pl.kernel(
        body,
        out_type=jax.ShapeDtypeStruct((planes * n_rows, LANES), I32),
        mesh=_sc_mesh(),
        scratch_types=[
            pltpu.VMEM((planes, SC_ROWS, LANES), I32),
            pltpu.VMEM((cp, planes, SC_ROWS), I32),
            pltpu.VMEM((cp, planes, SC_ROWS), I32),
            pltpu.VMEM((1, planes, SC_ROWS), I32),
            pltpu.VMEM((1, planes, SC_ROWS), I32),
            pltpu.SemaphoreType.DMA,
            pltpu.SemaphoreType.DMA,
        ],
        name="moe_dispatch_sc",
    )(hn_p.reshape(planes * t_p, LANES), hn_s.reshape(planes * t_s, LANES), idx0, idx1)
    return xs.reshape(planes, n_rows, LANES)


def _gather(ys, dest0, dest1):
    planes, n_rows, _ = ys.shape
    n_chunks = dest0.shape[0]
    n_tok = n_chunks * SC_ROWS
    cpw = max(n_chunks // SC_WORKERS, 1)
    assert n_chunks <= SC_WORKERS or n_chunks == cpw * SC_WORKERS
    idx = (_plane_rows(dest0, planes, n_rows), _plane_rows(dest1, planes, n_rows))

    def body(ys_hbm, d0_hbm, d1_hbm, out_hbm, rows_v, i0_v, i1_v, sem_in, sem_out):
        wid = _sc_worker()

        def work():
            pltpu.sync_copy(d0_hbm.at[pl.ds(wid * cpw, cpw)], i0_v)
            pltpu.sync_copy(d1_hbm.at[pl.ds(wid * cpw, cpw)], i1_v)

            @pl.loop(0, cpw)
            def _(c):
                row0 = (wid * cpw + c) * SC_ROWS
                for j, i_v in enumerate((i0_v, i1_v)):
                    loads = [pltpu.async_copy(ys_hbm.at[i_v.at[c, p]], rows_v.at[p], sem_in)
                             for p in range(planes)]
                    for cpy in loads:
                        cpy.wait()
                    stores = [
                        pltpu.async_copy(
                            rows_v.at[p], out_hbm.at[pl.ds((j * planes + p) * n_tok + row0, SC_ROWS)], sem_out)
                        for p in range(planes)]
                    for cpy in stores:
                        cpy.wait()

        if n_chunks < SC_WORKERS:
            pl.when(wid < n_chunks)(work)
        else:
            work()

    out = pl.kernel(
        body,
        out_type=jax.ShapeDtypeStruct((2 * planes * n_tok, LANES), ys.dtype),
        mesh=_sc_mesh(),
        scratch_types=[
            pltpu.VMEM((planes, SC_ROWS, LANES), ys.dtype),
            pltpu.VMEM((cpw, planes, SC_ROWS), I32),
            pltpu.VMEM((cpw, planes, SC_ROWS), I32),
            pltpu.SemaphoreType.DMA,
            pltpu.SemaphoreType.DMA,
        ],
        name="moe_gather_sc",
    )(ys.reshape(planes * n_rows, LANES), *idx)
    return out.reshape(2, planes, n_tok, LANES)


def _expert_kernel(be_ref, bv_ref, nb_ref, xs_ref, wg_ref, wu_ref, wd_ref, ys_ref):
    del be_ref
    i = pl.program_id(0)
    live = i < nb_ref[0]

    @pl.when(live)
    def _():
        valid = lax.broadcasted_iota(I32, (MOE_BLK, LANES), 0) < bv_ref[i]
        xb = _unpack_planes([jnp.where(valid, xs_ref[p], 0) for p in range(HN_PLANES)]).astype(BF16)
        gate = _dot(xb, wg_ref[0])
        hmid = gate * jax.nn.sigmoid(gate) * _dot(xb, wu_ref[0])
        _pack_planes(_dot(hmid.astype(BF16), wd_ref[0]), ys_ref)


def _experts(block_e, block_valid, nb, xs, w_eg, w_eu, w_ed):
    n_rows = xs.shape[1]
    n_blocks = n_rows // MOE_BLK
    row_idx = lambda i, be, bv, nb: (0, jnp.minimum(i, nb[0] - 1), 0)
    w_idx = lambda i, be, bv, nb: (be[i], 0, 0)
    grid_spec = pltpu.PrefetchScalarGridSpec(
        num_scalar_prefetch=3,
        grid=(n_blocks,),
        in_specs=[
            pl.BlockSpec((HN_PLANES, MOE_BLK, LANES), row_idx),
            pl.BlockSpec((1, D_MODEL, D_EXPERT), w_idx),
            pl.BlockSpec((1, D_MODEL, D_EXPERT), w_idx),
            pl.BlockSpec((1, D_EXPERT, D_MODEL), w_idx),
        ],
        out_specs=pl.BlockSpec((HN_PLANES, MOE_BLK, LANES), row_idx),
    )
    return pl.pallas_call(
        _expert_kernel,
        grid_spec=grid_spec,
        out_shape=jax.ShapeDtypeStruct((HN_PLANES, n_rows, LANES), I32),
        compiler_params=pltpu.CompilerParams(
            dimension_semantics=("arbitrary",), vmem_limit_bytes=VMEM_LIMIT),
        name="moe_experts",
    )(block_e, block_valid, nb, xs, w_eg, w_eu, w_ed)


def _combine_kernel(h_ref, route_ref, nfin_ref, rows_ref, *rest):
    y_ref = rest[-1]
    route = route_ref[...]
    ys1 = _unpack_planes([rows_ref[0, p] for p in range(HN_PLANES)])
    ys2 = _unpack_planes([rows_ref[1, p] for p in range(HN_PLANES)])
    out = h_ref[...] + (ys1 * route[:, 2:3] + ys2 * route[:, 3:4])
    y_ref[...] = _rms(out, nfin_ref[...])


def _combine(h, route, norm_final, rows, *, row0=0, y_prev=None):
    total = h.shape[0]
    tb = COMBINE_TB
    blk0 = row0 // tb
    args = [h, route, norm_final, rows]
    in_specs = [
        pl.BlockSpec((tb, D_MODEL), lambda i: (blk0 + i, 0)),
        pl.BlockSpec((tb, LANES), lambda i: (blk0 + i, 0)),
        pl.BlockSpec((1, D_MODEL), lambda i: (0, 0)),
        pl.BlockSpec((2, HN_PLANES, tb, LANES), lambda i: (0, 0, i, 0)),
    ]
    aliases = {}
    if y_prev is not None:
        aliases[len(args)] = 0
        args.append(y_prev)
        in_specs.append(pl.BlockSpec(memory_space=pl.ANY))
    return pl.pallas_call(
        _combine_kernel,
        grid=(rows.shape[2] // tb,),
        in_specs=in_specs,
        out_specs=pl.BlockSpec((tb, D_MODEL), lambda i: (blk0 + i, 0)),
        out_shape=jax.ShapeDtypeStruct((total, D_MODEL), F32),
        input_output_aliases=aliases,
        compiler_params=pltpu.CompilerParams(
            dimension_semantics=("arbitrary",), vmem_limit_bytes=VMEM_LIMIT),
        name="moe_combine",
    )(*args)


def kernel(x_prompt, x_sample, state_pool, state_gla, meta_tokens, norm_mix, w_in, w_gate_up, b_gate, w_pool, pool_scale, gla_norm, w_out, norm_ffn, w_router_group, b_router_group, w_router_expert, b_router_expert, w_expert_gate, w_expert_up, w_expert_down, norm_final):
    assert w_in.shape[0] == 1, "one encoder layer"
    batch, seq, _ = x_prompt.shape
    dec_batch, dec_seq, _ = x_sample.shape
    assert seq % MIX_TB == 0 and dec_seq == CHUNK and N_META <= CHUNK
    t_prompt = batch * seq
    t_sample = dec_batch * dec_seq
    t_all = t_prompt + t_sample
    assert t_prompt % COMBINE_TB == 0 and t_sample % COMBINE_TB == 0

    w_in_t = w_in[0].T
    gpad = EXPERTS_PER_GROUP - N_GROUPS
    rpad = ROUTER_ROWS - EXPERTS_PER_GROUP - N_EXPERTS
    w_router = jnp.concatenate([
        w_router_group[0].T, jnp.zeros((gpad, D_MODEL), F32),
        w_router_expert[0].T, jnp.zeros((rpad, D_MODEL), F32)], axis=0)
    b_router = jnp.concatenate([
        b_router_group[0], jnp.zeros((gpad,), F32), b_router_expert[0], jnp.zeros((rpad,), F32)])
    zg = jnp.zeros((POOL_GROUP_DIM, POOL_GROUP_DIM), F32)
    w_pool_pairs = jnp.stack([
        jnp.block([[w_pool[0, 2 * i], zg], [zg, w_pool[0, 2 * i + 1]]]) for i in range(len(POOL_WINDOWS) // 2)])
    weights = (
        norm_mix[0][None, :],
        _transpose_cast(w_in_t, D_MAIN),
        w_in_t[D_MAIN:].astype(BF16),
        w_gate_up[0].astype(BF16),
        b_gate[0][None, :],
        w_pool_pairs.astype(BF16),
        pool_scale[0][None, :],
        gla_norm[0][None, :],
        w_out[0].astype(BF16),
        norm_ffn[0][None, :],
        w_router.astype(BF16),
        b_router[:, None],
    )

    zero_cnt = jnp.zeros((N_EXPERTS, 1), F32)
    x_meta = jnp.pad(meta_tokens.astype(F32), ((CHUNK - N_META, 0), (0, 0)))
    meta = _mixer(x_meta, jnp.zeros((1, POOL_ROWS, D_POOL), F32),
                  jnp.zeros((1, GLA_HEADS, GLA_DK, GLA_DV), F32), zero_cnt,
                  weights, batch=1, seq=CHUNK, tb=CHUNK, lead_pad=CHUNK - N_META)
    pool_m, st_m = meta[4], meta[5]
    h_p, hn_p, route_p, rt_p, pool_p, st_p, cnt_p, w_eg, w_eu, w_ed = _mixer(
        x_prompt.reshape(t_prompt, D_MODEL), pool_m, st_m, zero_cnt, weights,
        batch=batch, seq=seq, tb=MIX_TB, lead_pad=0,
        experts=(w_expert_gate[0], w_expert_up[0], w_expert_down[0]))
    pool_s0 = jnp.pad(state_pool[0], ((0, 0), (POOL_ROWS - POOL_PAD, 0), (0, 0)))
    h_s, hn_s, route_s, rt_s, pool_s, st_s, cnt_s = _mixer(
        x_sample.reshape(t_sample, D_MODEL), pool_s0, state_gla[0].astype(F32),
        cnt_p, weights, batch=dec_batch, seq=dec_seq, tb=SAMPLE_TB, lead_pad=0)[:7]

    counts = cnt_s[:, 0].astype(I32)
    padded = (counts + MOE_BLK - 1) // MOE_BLK * MOE_BLK
    ends = jnp.cumsum(padded)
    pstart = ends - padded
    n_blocks = (2 * t_all + N_EXPERTS * (MOE_BLK - 1) + MOE_BLK - 1) // MOE_BLK
    nb = (ends[-1] // MOE_BLK).astype(I32)
    blk_ids = jnp.minimum(jnp.arange(n_blocks, dtype=I32), nb - 1)
    block_e = jnp.sum((ends[None, :] <= (blk_ids * MOE_BLK)[:, None]).astype(I32), axis=1)
    block_e = jnp.minimum(block_e, N_EXPERTS - 1)
    owner = block_e[:, None] == jnp.arange(N_EXPERTS, dtype=I32)
    row_end = jnp.sum(jnp.where(owner, pstart + counts, 0), axis=1)
    block_valid = jnp.clip(row_end - blk_ids * MOE_BLK, 0, MOE_BLK)

    def dest_rows(rt):
        rt = rt.transpose(1, 0, 2).reshape(ROUTE_ROWS, -1)
        onehot = rt[0:2].astype(I32)[..., None] == jnp.arange(N_EXPERTS, dtype=I32)
        return jnp.sum(jnp.where(onehot, pstart, 0), axis=-1) + rt[4:6].astype(I32)

    dest = jnp.concatenate([dest_rows(rt_p), dest_rows(rt_s)], axis=1)
    dest0 = dest[0].reshape(t_all // SC_ROWS, SC_ROWS)
    dest1 = dest[1].reshape(t_all // SC_ROWS, SC_ROWS)
    xs = _dispatch(hn_p, hn_s, dest0, dest1, n_blocks * MOE_BLK)
    ys = _experts(block_e, block_valid.astype(I32), nb[None], xs, w_eg, w_eu, w_ed)
    nfin = norm_final[None, :]
    cp_chunks = t_prompt // SC_ROWS
    part = cp_chunks // COMBINE_PARTS
    y_prompt = None
    for i in range(COMBINE_PARTS):
        ch = slice(i * part, (i + 1) * part)
        rows_i = _gather(ys, dest0[ch], dest1[ch])
        y_prompt = _combine(h_p, route_p, nfin, rows_i, row0=i * part * SC_ROWS, y_prev=y_prompt)
    y_prompt = y_prompt.reshape(batch, seq, D_MODEL)
    rows_s = _gather(ys, dest0[cp_chunks:], dest1[cp_chunks:])
    y_sample = _combine(h_s, route_s, nfin, rows_s).reshape(dec_batch, dec_seq, D_MODEL)
    new_pool_prompt = pool_p[:, POOL_ROWS - POOL_PAD:][None]
    new_gla_prompt = st_p[None]
    new_pool_sample = pool_s[:, POOL_ROWS - POOL_PAD:][None]
    new_gla_sample = st_s[None]
    return (y_prompt, y_sample, new_pool_prompt, new_gla_prompt, new_pool_sample, new_gla_sample)
```

```python
import functools

import jax
import jax.numpy as jnp
from jax import lax
from jax.experimental import pallas as pl
from jax.experimental.pallas import tpu as pltpu
from jax.experimental.pallas import tpu_sc as plsc

F32 = jnp.float32
BF16 = jnp.bfloat16
U32 = jnp.uint32
I32 = jnp.int32

D_MODEL = 1024
N_META = 16
CHUNK = 64
EPS = 1e-6
D_POOL = 512
POOL_WINDOWS = (2, 4, 8, 16)
POOL_GROUP_DIM = 128
POOL_PAD = 15
POOL_ROWS = 16
GLA_HEADS = 4
GLA_DK = 64
GLA_DV = 128
D_QK = 256
D_V = 512
HEAD_GROUP = 2
GATE_RANK = 16
GATE_TAU = 16.0
D_MAIN = D_POOL + 2 * D_QK + 2 * D_V
N_GROUPS = 4
EXPERTS_PER_GROUP = 8
N_EXPERTS = 32
D_EXPERT = 512

LANES = 128
MXU_DIM = 256
HALF = D_MODEL // 2
HN_PLANES = HALF // LANES
ROUTE_ROWS = 8
ROUTER_ROWS = 64
MIX_TB = 512
SAMPLE_TB = 256
FRONT_TILE = 256
FRONT_PLAN = (3, 0, 1, 2, 2, 0)
WEIGHT_TILE = 512
MOE_BLK = 768
COMBINE_TB = 1024
COMBINE_PARTS = 4
VMEM_LIMIT = 48 * 1024 * 1024
SC_CORES = 2
SC_SUBCORES = 16
SC_WORKERS = SC_CORES * SC_SUBCORES
SC_ROWS = 128


def _rms(x, g):
    return x * lax.rsqrt(jnp.mean(x * x, axis=-1, keepdims=True) + EPS) * g


def _dot(a, b):
    return jnp.dot(a, b, preferred_element_type=F32)


def _dot_nt(a, b):
    return lax.dot_general(a, b, (((1,), (1,)), ((), ())), preferred_element_type=F32)


def _pack_planes(x, ref):
    xb = x.astype(BF16)
    lo = lax.bitcast_convert_type(xb[:, :HALF].astype(F32), U32) >> 16
    hi = lax.bitcast_convert_type(xb[:, HALF:].astype(F32), U32) & jnp.uint32(0xFFFF0000)
    packed = lax.bitcast_convert_type(lo | hi, I32)
    for p in range(HN_PLANES):
        ref[p] = packed[:, p * LANES:(p + 1) * LANES]


def _unpack_planes(planes):
    words = [lax.bitcast_convert_type(p, U32) for p in planes]
    los = [lax.bitcast_convert_type(w << 16, F32) for w in words]
    his = [lax.bitcast_convert_type(w & jnp.uint32(0xFFFF0000), F32) for w in words]
    return jnp.concatenate(los + his, axis=-1)


def _dot_tn(a, b):
    return lax.dot_general(a, b, (((0,), (0,)), ((), ())), preferred_element_type=F32)


def _mixer_kernel(*refs, **static):
    s = pl.program_id(0)
    pl.when(s == 0)(lambda: _mixer_first_step(*refs, **static))
    pl.when(s > 0)(lambda: _mixer_step(*refs, **static))


def _mixer_first_step(x_ref, pool0_ref, st0_ref, cnt0_ref, tril_ref, sup_ref,
                      nmix_ref, wmain_ref, wz_ref, wgu_ref, bgate_ref, wpool_ref, pscale_ref,
                      gnorm_ref, wout_ref, nffn_ref, wr_ref, br_ref, eg_ref, eu_ref, ed_ref,
                      h_ref, hn_ref, route_ref, route_t_ref, pool_out_ref, st_out_ref, cnt_out_ref,
                      eg_out_ref, eu_out_ref, ed_out_ref,
                      ext_ref, st_ref, kbd_ref, vbd_ref, sbd_ref, o_ref, cnt_ref,
                      xs_ref, proj_ref, z_ref, *, tb, nj, lead_pad, chained):
    x_new = x_ref[...]
    xn = _rms(x_new, nmix_ref[...]).astype(BF16)
    xs_ref[0] = x_new
    for t in range(D_MAIN // FRONT_TILE):
        cols = slice(t * FRONT_TILE, (t + 1) * FRONT_TILE)
        proj_ref[0, :, cols] = _dot(xn, wmain_ref[:, cols])
    z_ref[0] = _dot_nt(wz_ref[...], xn)
    eg_out_ref[...] = eg_ref[...].astype(BF16)
    eu_out_ref[...] = eu_ref[...].astype(BF16)
    ed_out_ref[...] = ed_ref[...].astype(BF16)


def _mixer_step(x_ref, pool0_ref, st0_ref, cnt0_ref, tril_ref, sup_ref,
                nmix_ref, wmain_ref, wz_ref, wgu_ref, bgate_ref, wpool_ref, pscale_ref,
                gnorm_ref, wout_ref, nffn_ref, wr_ref, br_ref, eg_ref, eu_ref, ed_ref,
                h_ref, hn_ref, route_ref, route_t_ref, pool_out_ref, st_out_ref, cnt_out_ref,
                eg_out_ref, eu_out_ref, ed_out_ref,
                ext_ref, st_ref, kbd_ref, vbd_ref, sbd_ref, o_ref, cnt_ref,
                xs_ref, proj_ref, z_ref, *, tb, nj, lead_pad, chained):
    s = pl.program_id(0)
    back = s - 1
    j = lax.rem(back, nj)
    n_chunks = tb // CHUNK
    wr_slot = lax.rem(s, 2)
    rd_slot = 1 - wr_slot

    @pl.when(back == 0)
    def _():
        kbd_ref[...] = jnp.zeros_like(kbd_ref)
        vbd_ref[...] = jnp.zeros_like(vbd_ref)
        sbd_ref[...] = jnp.zeros_like(sbd_ref)
        cnt_ref[...] = cnt0_ref[...]

    def put_state(c, hh, st):
        gg, hp = divmod(hh, HEAD_GROUP)
        sbd_ref[c, gg, hp * GLA_DK:(hp + 1) * GLA_DK, hp * GLA_DV:(hp + 1) * GLA_DV] = st.astype(BF16)

    if chained:
        @pl.when(j == 0)
        def _():
            ext_ref[0:POOL_ROWS, :] = pool0_ref[0]
            st_ref[...] = st0_ref[0]

        for hh in range(GLA_HEADS):
            put_state(0, hh, st_ref[hh])
    else:
        for c in range(n_chunks):
            for hh in range(GLA_HEADS):
                put_state(c, hh, st0_ref[c, hh])

    x_new = x_ref[...]
    xn = _rms(x_new, nmix_ref[...]).astype(BF16)
    xs_ref[wr_slot] = x_new

    tiles_done = [0]

    def front_tiles(stage):
        for t in range(tiles_done[0], tiles_done[0] + FRONT_PLAN[stage]):
            cols = slice(t * FRONT_TILE, (t + 1) * FRONT_TILE)
            proj_ref[wr_slot, :, cols] = _dot(xn, wmain_ref[:, cols])
        tiles_done[0] += FRONT_PLAN[stage]

    front_tiles(0)
    z_ref[wr_slot] = _dot_nt(wz_ref[...], xn)

    x = xs_ref[rd_slot]
    z = z_ref[rd_slot]
    u = proj_ref[rd_slot, :, 0:D_POOL]
    q = proj_ref[rd_slot, :, D_POOL:D_POOL + D_QK]
    k = proj_ref[rd_slot, :, D_POOL + D_QK:D_POOL + 2 * D_QK]
    v = proj_ref[rd_slot, :, D_POOL + 2 * D_QK:D_POOL + 2 * D_QK + D_V]
    r = proj_ref[rd_slot, :, D_POOL + 2 * D_QK + D_V:D_MAIN]

    row = lax.broadcasted_iota(I32, (tb, 1), 0)

    pseg = POOL_ROWS + CHUNK
    if chained:
        ext_ref[POOL_ROWS:POOL_ROWS + tb, :] = u
        ext = ext_ref[...]
    else:
        ext = jnp.concatenate(
            [blk for c in range(n_chunks) for blk in (pool0_ref[c], u[c * CHUNK:(c + 1) * CHUNK])], axis=0)
    pooled = []
    for g, w in enumerate(POOL_WINDOWS):
        sl = slice(g * POOL_GROUP_DIM, (g + 1) * POOL_GROUP_DIM)
        acc = ext[:, sl]
        for d in range(g + 1):
            acc = acc + pltpu.roll(acc, 1 << d, axis=0)
        if chained:
            win = acc[POOL_ROWS:, :]
        else:
            win = jnp.concatenate([acc[c * pseg + POOL_ROWS:(c + 1) * pseg] for c in range(n_chunks)], axis=0)
        if lead_pad:
            cnt = jnp.clip(row - lead_pad + 1, 1, w).astype(F32)
            pooled.append(win / cnt - u[:, sl])
        else:
            pooled.append(win * (1.0 / w) - u[:, sl])
    pys = [_dot(jnp.concatenate(pooled[2 * i:2 * i + 2], axis=-1).astype(BF16), wpool_ref[i])
           for i in range(len(POOL_WINDOWS) // 2)]
    pool_y = jnp.concatenate(pys, axis=-1) * pscale_ref[...]
    if chained:
        ext_ref[0:POOL_ROWS, :] = ext_ref[tb:tb + POOL_ROWS, :]
    else:
        for c in range(n_chunks):
            pool_out_ref[c] = u[(c + 1) * CHUNK - POOL_ROWS:(c + 1) * CHUNK]

    gpre = _dot_tn(z.astype(BF16), wgu_ref[...]) + bgate_ref[...]
    log_a = (jnp.minimum(gpre, 0.0) - jnp.log(1.0 + jnp.exp(-jnp.abs(gpre)))) * (1.0 / GATE_TAU)
    if lead_pad:
        log_a = jnp.where(row >= lead_pad, log_a, 0.0)
    a_hi = log_a.astype(BF16)
    a_lo = (log_a - a_hi.astype(F32)).astype(BF16)
    tril = tril_ref[...]
    seg = tril.shape[0]
    bcum = jnp.concatenate(
        [_dot(tril, a_hi[r0:r0 + seg]) + _dot(tril, a_lo[r0:r0 + seg]) for r0 in range(0, tb, seg)], axis=0)
    front_tiles(1)
    eb = jnp.exp(bcum)
    qi = q * (GLA_DK ** -0.5) * eb
    ki = k * jnp.exp(-bcum)

    rr = lax.broadcasted_iota(I32, (CHUNK, HEAD_GROUP * CHUNK), 0)
    cc = lax.broadcasted_iota(I32, (CHUNK, HEAD_GROUP * CHUNK), 1)
    causal = (cc % CHUNK) <= rr

    lasts = [eb[(c + 1) * CHUNK - 1:(c + 1) * CHUNK, :] for c in range(n_chunks)]
    dcol = jnp.concatenate(lasts + [jnp.zeros((LANES - n_chunks, D_QK), F32)], axis=0).T

    n_grp = GLA_HEADS // HEAD_GROUP
    chunk_rows = [slice(c * CHUNK, (c + 1) * CHUNK) for c in range(n_chunks)]
    grp_k = [slice(g * HEAD_GROUP * GLA_DK, (g + 1) * HEAD_GROUP * GLA_DK) for g in range(n_grp)]
    grp_v = [slice(g * HEAD_GROUP * GLA_DV, (g + 1) * HEAD_GROUP * GLA_DV) for g in range(n_grp)]
    qi_b = qi.astype(BF16)
    ki_b = ki.astype(BF16)
    v_b = v.astype(BF16)

    scores = {}
    for c in range(n_chunks):
        for hh in range(GLA_HEADS):
            gg, hp = divmod(hh, HEAD_GROUP)
            kbd_ref[c, gg, hp * CHUNK:(hp + 1) * CHUNK, hp * GLA_DK:(hp + 1) * GLA_DK] = (
                ki_b[chunk_rows[c], hh * GLA_DK:(hh + 1) * GLA_DK])
            vbd_ref[c, gg, hp * CHUNK:(hp + 1) * CHUNK, hp * GLA_DV:(hp + 1) * GLA_DV] = (
                v_b[chunk_rows[c], hh * GLA_DV:(hh + 1) * GLA_DV])
        for gg in range(n_grp):
            scores[c, gg] = _dot_nt(qi_b[chunk_rows[c], grp_k[gg]], kbd_ref[c, gg])

    kvs = {}
    for c in range(n_chunks):
        for hh in range(GLA_HEADS):
            kvs[c, hh] = _dot_tn(ki_b[chunk_rows[c], hh * GLA_DK:(hh + 1) * GLA_DK],
                                 v_b[chunk_rows[c], hh * GLA_DV:(hh + 1) * GLA_DV])
    front_tiles(2)

    for hh in range(GLA_HEADS):
        st = st_ref[hh] if chained else None
        for c in range(n_chunks):
            s_old = st if chained else st0_ref[c, hh]
            s_new = (s_old + kvs[c, hh]) * dcol[hh * GLA_DK:(hh + 1) * GLA_DK, c:c + 1]
            if not chained:
                st_out_ref[c, hh] = s_new
            else:
                st = s_new
                if c + 1 < n_chunks:
                    put_state(c + 1, hh, s_new)
        if chained:
            st_ref[hh] = st

    for c in range(n_chunks):
        for gg in range(n_grp):
            p = jnp.where(causal, scores[c, gg], 0.0).astype(BF16)
            o_ref[chunk_rows[c], grp_v[gg]] = (
                _dot(p, vbd_ref[c, gg]) + _dot(qi_b[chunk_rows[c], grp_k[gg]], sbd_ref[c, gg]))

    o = o_ref[...]
    ons = []
    for hh in range(GLA_HEADS):
        oh = o[:, hh * GLA_DV:(hh + 1) * GLA_DV]
        ons.append(oh * lax.rsqrt(jnp.mean(oh * oh, axis=-1, keepdims=True) + EPS))
    og = jnp.concatenate(ons, axis=-1) * gnorm_ref[...] * (r * jax.nn.sigmoid(r))
    mix = _dot(jnp.concatenate([pool_y, og], axis=-1).astype(BF16), wout_ref[...])
    front_tiles(3)

    eg_out_ref[...] = eg_ref[...].astype(BF16)
    eu_out_ref[...] = eu_ref[...].astype(BF16)
    ed_out_ref[...] = ed_ref[...].astype(BF16)

    h = x + mix
    h_ref[...] = h
    hn = _rms(h, nffn_ref[...]).astype(BF16)
    _pack_planes(hn, hn_ref)

    logits = _dot_nt(wr_ref[...], hn) + br_ref[...]
    front_tiles(4)
    sub = lax.broadcasted_iota(I32, (EXPERTS_PER_GROUP, tb), 0).astype(F32)
    neg = jnp.float32(-jnp.inf)
    big = jnp.float32(EXPERTS_PER_GROUP)
    tile0 = logits[0:EXPERTS_PER_GROUP]
    is_g = sub < N_GROUPS
    gmax = jnp.max(jnp.where(is_g, tile0, neg), axis=0, keepdims=True)
    gsum = jnp.sum(jnp.where(is_g, jnp.exp(tile0 - gmax), 0.0), axis=0, keepdims=True)
    p_g = 1.0 / gsum
    gidx = jnp.min(jnp.where(is_g & (tile0 == gmax), sub, big), axis=0, keepdims=True)
    el = logits[N_GROUPS * EXPERTS_PER_GROUP:(N_GROUPS + 1) * EXPERTS_PER_GROUP]
    for g in range(N_GROUPS - 2, -1, -1):
        el = jnp.where(gidx == g, logits[(g + 1) * EXPERTS_PER_GROUP:(g + 2) * EXPERTS_PER_GROUP], el)
    m1 = jnp.max(el, axis=0, keepdims=True)
    i1 = jnp.min(jnp.where(el == m1, sub, big), axis=0, keepdims=True)
    rest = sub != i1
    m2 = jnp.max(jnp.where(rest, el, neg), axis=0, keepdims=True)
    i2 = jnp.min(jnp.where(rest & (el == m2), sub, big), axis=0, keepdims=True)
    t2 = jnp.exp(m2 - m1)
    den = 1.0 + t2
    g1 = p_g / den
    g2 = p_g * t2 / den
    e1 = gidx * EXPERTS_PER_GROUP + i1
    e2 = gidx * EXPERTS_PER_GROUP + i2

    eid = lax.broadcasted_iota(I32, (N_EXPERTS, tb), 0).astype(F32)
    oh1 = eid == e1
    oh2 = eid == e2
    both = jnp.where(oh1 | oh2, 1.0, 0.0)
    cnt = cnt_ref[...]
    before = _dot(both.astype(BF16), sup_ref[...]) + cnt
    front_tiles(5)
    assert tiles_done[0] * FRONT_TILE == D_MAIN
    pos1 = jnp.sum(jnp.where(oh1, before, 0.0), axis=0, keepdims=True)
    pos2 = jnp.sum(jnp.where(oh2, before, 0.0), axis=0, keepdims=True)
    cnt_new = cnt + jnp.sum(both, axis=1, keepdims=True)
    cnt_ref[...] = cnt_new
    cnt_out_ref[...] = cnt_new

    zero = jnp.zeros_like(e1)
    route_t = jnp.concatenate([e1, e2, g1, g2, pos1, pos2, zero, zero], axis=0)
    route_t_ref[0] = route_t
    route_ref[...] = jnp.concatenate([route_t, jnp.zeros((LANES - ROUTE_ROWS, tb), F32)], axis=0).T

    if chained:
        @pl.when(j == nj - 1)
        def _():
            pool_out_ref[0] = ext_ref[0:POOL_ROWS, :]
            st_out_ref[0] = st_ref[...]


def _mixer(x2d, pool0, st0, cnt0, weights, *, batch, seq, tb, lead_pad, experts=None):
    chained = tb <= seq
    total_rows = batch * seq
    n_blk = total_rows // tb
    nj = seq // tb if chained else 1
    per_blk = 1 if chained else tb // seq
    assert chained or (seq == CHUNK and batch % per_blk == 0)
    shared = pool0.shape[0] == 1
    front = lambda s: jnp.minimum(s, n_blk - 1)
    back = lambda s: jnp.maximum(s - 1, 0)
    stream = lambda s: back(s) // nj
    st_idx = (lambda s: (0, 0, 0)) if shared else (lambda s: (stream(s), 0, 0))
    gla_idx = (lambda s: (0, 0, 0, 0)) if shared else (lambda s: (stream(s), 0, 0, 0))
    const2 = lambda s: (0, 0)
    tok_out = lambda s: (back(s), 0)

    seg = min(tb, MXU_DIM)
    ii = jnp.arange(seg)
    tril = ((ii[:, None] >= ii[None, :]) & (ii[:, None] // CHUNK == ii[None, :] // CHUNK)).astype(BF16)
    ii = jnp.arange(tb)
    sup = (ii[:, None] < ii[None, :]).astype(BF16)

    in_specs = [
        pl.BlockSpec((tb, D_MODEL), lambda s: (front(s), 0)),
        pl.BlockSpec((per_blk, POOL_ROWS, D_POOL), st_idx),
        pl.BlockSpec((per_blk, GLA_HEADS, GLA_DK, GLA_DV), gla_idx),
        pl.BlockSpec((N_EXPERTS, 1), const2),
        pl.BlockSpec((seg, seg), const2),
        pl.BlockSpec((tb, tb), const2),
    ]
    for wgt in weights:
        in_specs.append(pl.BlockSpec(wgt.shape, (lambda s, n=wgt.ndim: (0,) * n)))
    if experts is None:
        experts = (jnp.zeros((1, 8, LANES), F32),) * 3
        per = 1
        cast_idx = lambda s: (0, 0, 0)
    else:
        per = n_blk // N_EXPERTS
        assert n_blk == per * N_EXPERTS and all(w.shape[1] % (8 * per) == 0 for w in experts)
        cast_idx = lambda s: (front(s) // per, front(s) % per, 0)
    cast_specs = [pl.BlockSpec((1, w.shape[1] // per, w.shape[2]), cast_idx) for w in experts]
    in_specs += cast_specs
    args = [x2d, pool0, st0, cnt0, tril, sup, *weights, *experts]

    out_shape = [
        jax.ShapeDtypeStruct((total_rows, D_MODEL), F32),
        jax.ShapeDtypeStruct((HN_PLANES, total_rows, LANES), I32),
        jax.ShapeDtypeStruct((total_rows, LANES), F32),
        jax.ShapeDtypeStruct((n_blk, ROUTE_ROWS, tb), F32),
        jax.ShapeDtypeStruct((batch, POOL_ROWS, D_POOL), F32),
        jax.ShapeDtypeStruct((batch, GLA_HEADS, GLA_DK, GLA_DV), F32),
        jax.ShapeDtypeStruct((N_EXPERTS, 1), F32),
        *[jax.ShapeDtypeStruct(w.shape, BF16) for w in experts],
    ]
    out_specs = [
        pl.BlockSpec((tb, D_MODEL), tok_out),
        pl.BlockSpec((HN_PLANES, tb, LANES), lambda s: (0, back(s), 0)),
        pl.BlockSpec((tb, LANES), tok_out),
        pl.BlockSpec((1, ROUTE_ROWS, tb), lambda s: (back(s), 0, 0)),
        pl.BlockSpec((per_blk, POOL_ROWS, D_POOL), lambda s: (stream(s), 0, 0)),
        pl.BlockSpec((per_blk, GLA_HEADS, GLA_DK, GLA_DV), lambda s: (stream(s), 0, 0, 0)),
        pl.BlockSpec((N_EXPERTS, 1), const2),
        *cast_specs,
    ]
    n_grp = GLA_HEADS // HEAD_GROUP
    scratch = [
        pltpu.VMEM((POOL_ROWS + tb, D_POOL), F32),
        pltpu.VMEM((GLA_HEADS, GLA_DK, GLA_DV), F32),
        pltpu.VMEM((tb // CHUNK, n_grp, HEAD_GROUP * CHUNK, HEAD_GROUP * GLA_DK), BF16),
        pltpu.VMEM((tb // CHUNK, n_grp, HEAD_GROUP * CHUNK, HEAD_GROUP * GLA_DV), BF16),
        pltpu.VMEM((tb // CHUNK, n_grp, HEAD_GROUP * GLA_DK, HEAD_GROUP * GLA_DV), BF16),
        pltpu.VMEM((tb, D_V), F32),
        pltpu.VMEM((N_EXPERTS, 1), F32),
        pltpu.VMEM((2, tb, D_MODEL), F32),
        pltpu.VMEM((2, tb, D_MAIN), F32),
        pltpu.VMEM((2, GATE_RANK, tb), F32),
    ]
    return pl.pallas_call(
        functools.partial(_mixer_kernel, tb=tb, nj=nj, lead_pad=lead_pad, chained=chained),
        grid=(n_blk + 1,),
        in_specs=in_specs,
        out_specs=out_specs,
        out_shape=out_shape,
        scratch_shapes=scratch,
        compiler_params=pltpu.CompilerParams(
            dimension_semantics=("arbitrary",), vmem_limit_bytes=VMEM_LIMIT),
        name=f"mixer_tb{tb}_pad{lead_pad}",
    )(*args)


def _transpose_cast_kernel(wt_ref, w_ref):
    w_ref[...] = wt_ref[...].T.astype(BF16)


def _transpose_cast(w_t, n_cols):
    k = w_t.shape[1]
    assert n_cols % WEIGHT_TILE == 0 and n_cols <= w_t.shape[0]
    return pl.pallas_call(
        _transpose_cast_kernel,
        grid=(n_cols // WEIGHT_TILE,),
        in_specs=[pl.BlockSpec((WEIGHT_TILE, k), lambda i: (i, 0))],
        out_specs=pl.BlockSpec((k, WEIGHT_TILE), lambda i: (0, i)),
        out_shape=jax.ShapeDtypeStruct((k, n_cols), BF16),
        compiler_params=pltpu.CompilerParams(dimension_semantics=("arbitrary",)),
        name="weight_transpose_cast",
    )(w_t)


def _sc_mesh():
    return plsc.VectorSubcoreMesh(core_axis_name="c", subcore_axis_name="s",
                                  num_cores=SC_CORES, num_subcores=SC_SUBCORES)


def _sc_worker():
    return lax.axis_index("s") * SC_CORES + lax.axis_index("c")


def _plane_rows(dest, planes, rows_per_plane):
    offs = (jnp.arange(planes, dtype=I32) * rows_per_plane)[None, :, None]
    return dest[:, None, :] + offs


def _dispatch(hn_p, hn_s, dest0, dest1, n_rows):
    planes, t_p, _ = hn_p.shape
    t_s = hn_s.shape[1]
    n_cp = t_p // SC_ROWS
    cp = n_cp // SC_WORKERS
    n_cs = t_s // SC_ROWS
    assert t_p == cp * SC_ROWS * SC_WORKERS and t_s == n_cs * SC_ROWS and n_cs <= SC_WORKERS
    half = planes // 2
    assert planes == 2 * half
    idx0 = _plane_rows(dest0, planes, n_rows)
    idx1 = _plane_rows(dest1, planes, n_rows)

    def body(hnp_hbm, hns_hbm, d0_hbm, d1_hbm, xs_hbm, rows_v, i0_v, i1_v, is0_v, is1_v, sem_in, sem_out):
        wid = _sc_worker()
        pltpu.sync_copy(d0_hbm.at[pl.ds(wid * cp, cp)], i0_v)
        pltpu.sync_copy(d1_hbm.at[pl.ds(wid * cp, cp)], i1_v)

        def start_loads(src_hbm, src_rows, row0, h):
            return [pltpu.async_copy(src_hbm.at[pl.ds((h * half + q) * src_rows + row0, SC_ROWS)],
                                     rows_v.at[h, q], sem_in) for q in range(half)]

        def start_stores(h, i0, i1, c):
            stores = []
            for q in range(half):
                stores.append(pltpu.async_copy(rows_v.at[h, q], xs_hbm.at[i0.at[c, h * half + q]], sem_out))
                stores.append(pltpu.async_copy(rows_v.at[h, q], xs_hbm.at[i1.at[c, h * half + q]], sem_out))
            return stores

        def wait_all(copies):
            for cpy in copies:
                cpy.wait()

        def move(src_hbm, src_rows, row0_of, i0, i1, n):
            loads = start_loads(src_hbm, src_rows, row0_of(0), 0)
            stores = []
            for u in range(2 * n):
                c, h = divmod(u, 2)
                wait_all(loads)
                wait_all(stores)
                if u + 1 < 2 * n:
                    loads = start_loads(src_hbm, src_rows, row0_of((u + 1) // 2), (u + 1) % 2)
                stores = start_stores(h, i0, i1, c)
            wait_all(stores)

        move(hnp_hbm, t_p, lambda c: (wid * cp + c) * SC_ROWS, i0_v, i1_v, cp)

        @pl.when(wid < n_cs)
        def _():
            pltpu.sync_copy(d0_hbm.at[pl.ds(n_cp + wid, 1)], is0_v)
            pltpu.sync_copy(d1_hbm.at[pl.ds(n_cp + wid, 1)], is1_v)
            move(hns_hbm, t_s, lambda c: wid * SC_ROWS, is0_v, is1_v, 1)

    xs = pl.kernel(
        body,
        out_type=jax.ShapeDtypeStruct((planes * n_rows, LANES), I32),
        mesh=_sc_mesh(),
        scratch_types=[
            pltpu.VMEM((2, half, SC_ROWS, LANES), I32),
            pltpu.VMEM((cp, planes, SC_ROWS), I32),
            pltpu.VMEM((cp, planes, SC_ROWS), I32),
            pltpu.VMEM((1, planes, SC_ROWS), I32),
            pltpu.VMEM((1, planes, SC_ROWS), I32),
            pltpu.SemaphoreType.DMA,
            pltpu.SemaphoreType.DMA,
        ],
        name="moe_dispatch_sc",
    )(hn_p.reshape(planes * t_p, LANES), hn_s.reshape(planes * t_s, LANES), idx0, idx1)
    return xs.reshape(planes, n_rows, LANES)


def _gather(ys, dest0, dest1):
    planes, n_rows, _ = ys.shape
    n_chunks = dest0.shape[0]
    n_tok = n_chunks * SC_ROWS
    cpw = max(n_chunks // SC_WORKERS, 1)
    assert n_chunks <= SC_WORKERS or n_chunks == cpw * SC_WORKERS
    idx = (_plane_rows(dest0, planes, n_rows), _plane_rows(dest1, planes, n_rows))

    def body(ys_hbm, d0_hbm, d1_hbm, out_hbm, rows_v, i0_v, i1_v, sem_in, sem_out):
        wid = _sc_worker()

        def work():
            pltpu.sync_copy(d0_hbm.at[pl.ds(wid * cpw, cpw)], i0_v)
            pltpu.sync_copy(d1_hbm.at[pl.ds(wid * cpw, cpw)], i1_v)

            @pl.loop(0, cpw)
            def _(c):
                row0 = (wid * cpw + c) * SC_ROWS
                for j, i_v in enumerate((i0_v, i1_v)):
                    loads = [pltpu.async_copy(ys_hbm.at[i_v.at[c, p]], rows_v.at[p], sem_in)
                             for p in range(planes)]
                    for cpy in loads:
                        cpy.wait()
                    stores = [
                        pltpu.async_copy(
                            rows_v.at[p], out_hbm.at[pl.ds((j * planes + p) * n_tok + row0, SC_ROWS)], sem_out)
                        for p in range(planes)]
                    for cpy in stores:
                        cpy.wait()

        if n_chunks < SC_WORKERS:
            pl.when(wid < n_chunks)(work)
        else:
            work()

    out = pl.kernel(
        body,
        out_type=jax.ShapeDtypeStruct((2 * planes * n_tok, LANES), ys.dtype),
        mesh=_sc_mesh(),
        scratch_types=[
            pltpu.VMEM((planes, SC_ROWS, LANES), ys.dtype),
            pltpu.VMEM((cpw, planes, SC_ROWS), I32),
            pltpu.VMEM((cpw, planes, SC_ROWS), I32),
            pltpu.SemaphoreType.DMA,
            pltpu.SemaphoreType.DMA,
        ],
        name="moe_gather_sc",
    )(ys.reshape(planes * n_rows, LANES), *idx)
    return out.reshape(2, planes, n_tok, LANES)


def _expert_kernel(be_ref, bv_ref, nb_ref, xs_ref, wg_ref, wu_ref, wd_ref, ys_ref):
    del be_ref
    i = pl.program_id(0)
    live = i < nb_ref[0]

    @pl.when(live)
    def _():
        valid = lax.broadcasted_iota(I32, (MOE_BLK, LANES), 0) < bv_ref[i]
        xb = _unpack_planes([jnp.where(valid, xs_ref[p], 0) for p in range(HN_PLANES)]).astype(BF16)
        gate = _dot(xb, wg_ref[0])
        hmid = gate * jax.nn.sigmoid(gate) * _dot(xb, wu_ref[0])
        _pack_planes(_dot(hmid.astype(BF16), wd_ref[0]), ys_ref)


def _experts(block_e, block_valid, nb, xs, w_eg, w_eu, w_ed):
    n_rows = xs.shape[1]
    n_blocks = n_rows // MOE_BLK
    row_idx = lambda i, be, bv, nb: (0, jnp.minimum(i, nb[0] - 1), 0)
    w_idx = lambda i, be, bv, nb: (be[i], 0, 0)
    grid_spec = pltpu.PrefetchScalarGridSpec(
        num_scalar_prefetch=3,
        grid=(n_blocks,),
        in_specs=[
            pl.BlockSpec((HN_PLANES, MOE_BLK, LANES), row_idx),
            pl.BlockSpec((1, D_MODEL, D_EXPERT), w_idx),
            pl.BlockSpec((1, D_MODEL, D_EXPERT), w_idx),
            pl.BlockSpec((1, D_EXPERT, D_MODEL), w_idx),
        ],
        out_specs=pl.BlockSpec((HN_PLANES, MOE_BLK, LANES), row_idx),
    )
    return pl.pallas_call(
        _expert_kernel,
        grid_spec=grid_spec,
        out_shape=jax.ShapeDtypeStruct((HN_PLANES, n_rows, LANES), I32),
        compiler_params=pltpu.CompilerParams(
            dimension_semantics=("arbitrary",), vmem_limit_bytes=VMEM_LIMIT),
        name="moe_experts",
    )(block_e, block_valid, nb, xs, w_eg, w_eu, w_ed)


def _combine_kernel(h_ref, route_ref, nfin_ref, rows_ref, *rest):
    y_ref = rest[-1]
    route = route_ref[...]
    ys1 = _unpack_planes([rows_ref[0, p] for p in range(HN_PLANES)])
    ys2 = _unpack_planes([rows_ref[1, p] for p in range(HN_PLANES)])
    out = h_ref[...] + (ys1 * route[:, 2:3] + ys2 * route[:, 3:4])
    y_ref[...] = _rms(out, nfin_ref[...])


def _combine(h, route, norm_final, rows, *, row0=0, y_prev=None):
    total = h.shape[0]
    tb = COMBINE_TB
    blk0 = row0 // tb
    args = [h, route, norm_final, rows]
    in_specs = [
        pl.BlockSpec((tb, D_MODEL), lambda i: (blk0 + i, 0)),
        pl.BlockSpec((tb, LANES), lambda i: (blk0 + i, 0)),
        pl.BlockSpec((1, D_MODEL), lambda i: (0, 0)),
        pl.BlockSpec((2, HN_PLANES, tb, LANES), lambda i: (0, 0, i, 0)),
    ]
    aliases = {}
    if y_prev is not None:
        aliases[len(args)] = 0
        args.append(y_prev)
        in_specs.append(pl.BlockSpec(memory_space=pl.ANY))
    return pl.pallas_call(
        _combine_kernel,
        grid=(rows.shape[2] // tb,),
        in_specs=in_specs,
        out_specs=pl.BlockSpec((tb, D_MODEL), lambda i: (blk0 + i, 0)),
        out_shape=jax.ShapeDtypeStruct((total, D_MODEL), F32),
        input_output_aliases=aliases,
        compiler_params=pltpu.CompilerParams(
            dimension_semantics=("arbitrary",), vmem_limit_bytes=VMEM_LIMIT),
        name="moe_combine",
    )(*args)


def kernel(x_prompt, x_sample, state_pool, state_gla, meta_tokens, norm_mix, w_in, w_gate_up, b_gate, w_pool, pool_scale, gla_norm, w_out, norm_ffn, w_router_group, b_router_group, w_router_expert, b_router_expert, w_expert_gate, w_expert_up, w_expert_down, norm_final):
    assert w_in.shape[0] == 1, "one encoder layer"
    batch, seq, _ = x_prompt.shape
    dec_batch, dec_seq, _ = x_sample.shape
    assert seq % MIX_TB == 0 and dec_seq == CHUNK and N_META <= CHUNK
    t_prompt = batch * seq
    t_sample = dec_batch * dec_seq
    t_all = t_prompt + t_sample
    assert t_prompt % COMBINE_TB == 0 and t_sample % COMBINE_TB == 0

    w_in_t = w_in[0].T
    gpad = EXPERTS_PER_GROUP - N_GROUPS
    rpad = ROUTER_ROWS - EXPERTS_PER_GROUP - N_EXPERTS
    w_router = jnp.concatenate([
        w_router_group[0].T, jnp.zeros((gpad, D_MODEL), F32),
        w_router_expert[0].T, jnp.zeros((rpad, D_MODEL), F32)], axis=0)
    b_router = jnp.concatenate([
        b_router_group[0], jnp.zeros((gpad,), F32), b_router_expert[0], jnp.zeros((rpad,), F32)])
    zg = jnp.zeros((POOL_GROUP_DIM, POOL_GROUP_DIM), F32)
    w_pool_pairs = jnp.stack([
        jnp.block([[w_pool[0, 2 * i], zg], [zg, w_pool[0, 2 * i + 1]]]) for i in range(len(POOL_WINDOWS) // 2)])
    weights = (
        norm_mix[0][None, :],
        _transpose_cast(w_in_t, D_MAIN),
        w_in_t[D_MAIN:].astype(BF16),
        w_gate_up[0].astype(BF16),
        b_gate[0][None, :],
        w_pool_pairs.astype(BF16),
        pool_scale[0][None, :],
        gla_norm[0][None, :],
        w_out[0].astype(BF16),
        norm_ffn[0][None, :],
        w_router.astype(BF16),
        b_router[:, None],
    )

    zero_cnt = jnp.zeros((N_EXPERTS, 1), F32)
    x_meta = jnp.pad(meta_tokens.astype(F32), ((CHUNK - N_META, 0), (0, 0)))
    meta = _mixer(x_meta, jnp.zeros((1, POOL_ROWS, D_POOL), F32),
                  jnp.zeros((1, GLA_HEADS, GLA_DK, GLA_DV), F32), zero_cnt,
                  weights, batch=1, seq=CHUNK, tb=CHUNK, lead_pad=CHUNK - N_META)
    pool_m, st_m = meta[4], meta[5]
    h_p, hn_p, route_p, rt_p, pool_p, st_p, cnt_p, w_eg, w_eu, w_ed = _mixer(
        x_prompt.reshape(t_prompt, D_MODEL), pool_m, st_m, zero_cnt, weights,
        batch=batch, seq=seq, tb=MIX_TB, lead_pad=0,
        experts=(w_expert_gate[0], w_expert_up[0], w_expert_down[0]))
    pool_s0 = jnp.pad(state_pool[0], ((0, 0), (POOL_ROWS - POOL_PAD, 0), (0, 0)))
    h_s, hn_s, route_s, rt_s, pool_s, st_s, cnt_s = _mixer(
        x_sample.reshape(t_sample, D_MODEL), pool_s0, state_gla[0].astype(F32),
        cnt_p, weights, batch=dec_batch, seq=dec_seq, tb=SAMPLE_TB, lead_pad=0)[:7]

    counts = cnt_s[:, 0].astype(I32)
    padded = (counts + MOE_BLK - 1) // MOE_BLK * MOE_BLK
    ends = jnp.cumsum(padded)
    pstart = ends - padded
    n_blocks = (2 * t_all + N_EXPERTS * (MOE_BLK - 1) + MOE_BLK - 1) // MOE_BLK
    nb = (ends[-1] // MOE_BLK).astype(I32)
    blk_ids = jnp.minimum(jnp.arange(n_blocks, dtype=I32), nb - 1)
    block_e = jnp.sum((ends[None, :] <= (blk_ids * MOE_BLK)[:, None]).astype(I32), axis=1)
    block_e = jnp.minimum(block_e, N_EXPERTS - 1)
    owner = block_e[:, None] == jnp.arange(N_EXPERTS, dtype=I32)
    row_end = jnp.sum(jnp.where(owner, pstart + counts, 0), axis=1)
    block_valid = jnp.clip(row_end - blk_ids * MOE_BLK, 0, MOE_BLK)

    def dest_rows(rt):
        rt = rt.transpose(1, 0, 2).reshape(ROUTE_ROWS, -1)
        onehot = rt[0:2].astype(I32)[..., None] == jnp.arange(N_EXPERTS, dtype=I32)
        return jnp.sum(jnp.where(onehot, pstart, 0), axis=-1) + rt[4:6].astype(I32)

    dest = jnp.concatenate([dest_rows(rt_p), dest_rows(rt_s)], axis=1)
    dest0 = dest[0].reshape(t_all // SC_ROWS, SC_ROWS)
    dest1 = dest[1].reshape(t_all // SC_ROWS, SC_ROWS)
    xs = _dispatch(hn_p, hn_s, dest0, dest1, n_blocks * MOE_BLK)
    ys = _experts(block_e, block_valid.astype(I32), nb[None], xs, w_eg, w_eu, w_ed)
    nfin = norm_final[None, :]
    cp_chunks = t_prompt // SC_ROWS
    part = cp_chunks // COMBINE_PARTS
    y_prompt = None
    for i in range(COMBINE_PARTS):
        ch = slice(i * part, (i + 1) * part)
        rows_i = _gather(ys, dest0[ch], dest1[ch])
        y_prompt = _combine(h_p, route_p, nfin, rows_i, row0=i * part * SC_ROWS, y_prev=y_prompt)
    y_prompt = y_prompt.reshape(batch, seq, D_MODEL)
    rows_s = _gather(ys, dest0[cp_chunks:], dest1[cp_chunks:])
    y_sample = _combine(h_s, route_s, nfin, rows_s).reshape(dec_batch, dec_seq, D_MODEL)
    new_pool_prompt = pool_p[:, POOL_ROWS - POOL_PAD:][None]
    new_gla_prompt = st_p[None]
    new_pool_sample = pool_s[:, POOL_ROWS - POOL_PAD:][None]
    new_gla_sample = st_s[None]
    return (y_prompt, y_sample, new_pool_prompt, new_gla_prompt, new_pool_sample, new_gla_sample)
```

```python
import functools

import jax
import jax.numpy as jnp
from jax import lax
from jax.experimental import pallas as pl
from jax.experimental.pallas import tpu as pltpu
from jax.experimental.pallas import tpu_sc as plsc

F32 = jnp.float32
BF16 = jnp.bfloat16
U32 = jnp.uint32
I32 = jnp.int32

D_MODEL = 1024
N_META = 16
CHUNK = 64
EPS = 1e-6
D_POOL = 512
POOL_WINDOWS = (2, 4, 8, 16)
POOL_GROUP_DIM = 128
POOL_PAD = 15
POOL_ROWS = 16
GLA_HEADS = 4
GLA_DK = 64
GLA_DV = 128
D_QK = 256
D_V = 512
HEAD_GROUP = 2
GATE_RANK = 16
GATE_TAU = 16.0
D_MAIN = D_POOL + 2 * D_QK + 2 * D_V
N_GROUPS = 4
EXPERTS_PER_GROUP = 8
N_EXPERTS = 32
D_EXPERT = 512

LANES = 128
MXU_DIM = 256
HALF = D_MODEL // 2
HN_PLANES = HALF // LANES
ROUTE_ROWS = 8
ROUTER_ROWS = 64
MIX_TB = 512
SAMPLE_TB = 256
FRONT_TILE = 256
FRONT_PLAN = (3, 0, 1, 2, 2, 0)
WEIGHT_TILE = 512
MOE_BLK = 768
COMBINE_TB = 1024
COMBINE_PARTS = 4
VMEM_LIMIT = 48 * 1024 * 1024
SC_CORES = 2
SC_SUBCORES = 16
SC_WORKERS = SC_CORES * SC_SUBCORES
SC_ROWS = 128


def _rms(x, g):
    return x * lax.rsqrt(jnp.mean(x * x, axis=-1, keepdims=True) + EPS) * g


def _dot(a, b):
    return jnp.dot(a, b, preferred_element_type=F32)


def _dot_nt(a, b):
    return lax.dot_general(a, b, (((1,), (1,)), ((), ())), preferred_element_type=F32)


def _pack_planes(x, ref):
    xb = x.astype(BF16)
    lo = lax.bitcast_convert_type(xb[:, :HALF].astype(F32), U32) >> 16
    hi = lax.bitcast_convert_type(xb[:, HALF:].astype(F32), U32) & jnp.uint32(0xFFFF0000)
    packed = lax.bitcast_convert_type(lo | hi, I32)
    for p in range(HN_PLANES):
        ref[p] = packed[:, p * LANES:(p + 1) * LANES]


def _unpack_planes(planes):
    words = [lax.bitcast_convert_type(p, U32) for p in planes]
    los = [lax.bitcast_convert_type(w << 16, F32) for w in words]
    his = [lax.bitcast_convert_type(w & jnp.uint32(0xFFFF0000), F32) for w in words]
    return jnp.concatenate(los + his, axis=-1)


def _dot_tn(a, b):
    return lax.dot_general(a, b, (((0,), (0,)), ((), ())), preferred_element_type=F32)


def _mixer_kernel(*refs, **static):
    s = pl.program_id(0)
    pl.when(s == 0)(lambda: _mixer_first_step(*refs, **static))
    pl.when(s > 0)(lambda: _mixer_step(*refs, **static))


def _mixer_first_step(x_ref, pool0_ref, st0_ref, cnt0_ref, tril_ref, sup_ref,
                      nmix_ref, wmain_ref, wz_ref, wgu_ref, bgate_ref, wpool_ref, pscale_ref,
                      gnorm_ref, wout_ref, nffn_ref, wr_ref, br_ref, eg_ref, eu_ref, ed_ref,
                      h_ref, hn_ref, route_ref, route_t_ref, pool_out_ref, st_out_ref, cnt_out_ref,
                      eg_out_ref, eu_out_ref, ed_out_ref,
                      ext_ref, st_ref, kbd_ref, vbd_ref, sbd_ref, o_ref, cnt_ref,
                      xs_ref, proj_ref, z_ref, *, tb, nj, lead_pad, chained):
    x_new = x_ref[...]
    xn = _rms(x_new, nmix_ref[...]).astype(BF16)
    xs_ref[0] = x_new
    for t in range(D_MAIN // FRONT_TILE):
        cols = slice(t * FRONT_TILE, (t + 1) * FRONT_TILE)
        proj_ref[0, :, cols] = _dot(xn, wmain_ref[:, cols])
    z_ref[0] = _dot_nt(wz_ref[...], xn)
    eg_out_ref[...] = eg_ref[...].astype(BF16)
    eu_out_ref[...] = eu_ref[...].astype(BF16)
    ed_out_ref[...] = ed_ref[...].astype(BF16)


def _mixer_step(x_ref, pool0_ref, st0_ref, cnt0_ref, tril_ref, sup_ref,
                nmix_ref, wmain_ref, wz_ref, wgu_ref, bgate_ref, wpool_ref, pscale_ref,
                gnorm_ref, wout_ref, nffn_ref, wr_ref, br_ref, eg_ref, eu_ref, ed_ref,
                h_ref, hn_ref, route_ref, route_t_ref, pool_out_ref, st_out_ref, cnt_out_ref,
                eg_out_ref, eu_out_ref, ed_out_ref,
                ext_ref, st_ref, kbd_ref, vbd_ref, sbd_ref, o_ref, cnt_ref,
                xs_ref, proj_ref, z_ref, *, tb, nj, lead_pad, chained):
    s = pl.program_id(0)
    back = s - 1
    j = lax.rem(back, nj)
    n_chunks = tb // CHUNK
    wr_slot = lax.rem(s, 2)
    rd_slot = 1 - wr_slot

    @pl.when(back == 0)
    def _():
        kbd_ref[...] = jnp.zeros_like(kbd_ref)
        vbd_ref[...] = jnp.zeros_like(vbd_ref)
        sbd_ref[...] = jnp.zeros_like(sbd_ref)
        cnt_ref[...] = cnt0_ref[...]

    def put_state(c, hh, st):
        gg, hp = divmod(hh, HEAD_GROUP)
        sbd_ref[c, gg, hp * GLA_DK:(hp + 1) * GLA_DK, hp * GLA_DV:(hp + 1) * GLA_DV] = st.astype(BF16)

    if chained:
        @pl.when(j == 0)
        def _():
            ext_ref[0:POOL_ROWS, :] = pool0_ref[0]
            st_ref[...] = st0_ref[0]

        for hh in range(GLA_HEADS):
            put_state(0, hh, st_ref[hh])
    else:
        for c in range(n_chunks):
            for hh in range(GLA_HEADS):
                put_state(c, hh, st0_ref[c, hh])

    x_new = x_ref[...]
    xn = _rms(x_new, nmix_ref[...]).astype(BF16)
    xs_ref[wr_slot] = x_new

    tiles_done = [0]

    def front_tiles(stage):
        for t in range(tiles_done[0], tiles_done[0] + FRONT_PLAN[stage]):
            cols = slice(t * FRONT_TILE, (t + 1) * FRONT_TILE)
            proj_ref[wr_slot, :, cols] = _dot(xn, wmain_ref[:, cols])
        tiles_done[0] += FRONT_PLAN[stage]

    front_tiles(0)
    z_ref[wr_slot] = _dot_nt(wz_ref[...], xn)

    x = xs_ref[rd_slot]
    z = z_ref[rd_slot]
    u = proj_ref[rd_slot, :, 0:D_POOL]
    q = proj_ref[rd_slot, :, D_POOL:D_POOL + D_QK]
    k = proj_ref[rd_slot, :, D_POOL + D_QK:D_POOL + 2 * D_QK]
    v = proj_ref[rd_slot, :, D_POOL + 2 * D_QK:D_POOL + 2 * D_QK + D_V]
    r = proj_ref[rd_slot, :, D_POOL + 2 * D_QK + D_V:D_MAIN]

    row = lax.broadcasted_iota(I32, (tb, 1), 0)

    pseg = POOL_ROWS + CHUNK
    if chained:
        ext_ref[POOL_ROWS:POOL_ROWS + tb, :] = u
        ext = ext_ref[...]
    else:
        ext = jnp.concatenate(
            [blk for c in range(n_chunks) for blk in (pool0_ref[c], u[c * CHUNK:(c + 1) * CHUNK])], axis=0)
    pooled = []
    for g, w in enumerate(POOL_WINDOWS):
        sl = slice(g * POOL_GROUP_DIM, (g + 1) * POOL_GROUP_DIM)
        acc = ext[:, sl]
        for d in range(g + 1):
            acc = acc + pltpu.roll(acc, 1 << d, axis=0)
        if chained:
            win = acc[POOL_ROWS:, :]
        else:
            win = jnp.concatenate([acc[c * pseg + POOL_ROWS:(c + 1) * pseg] for c in range(n_chunks)], axis=0)
        if lead_pad:
            cnt = jnp.clip(row - lead_pad + 1, 1, w).astype(F32)
            pooled.append(win / cnt - u[:, sl])
        else:
            pooled.append(win * (1.0 / w) - u[:, sl])
    pys = [_dot(jnp.concatenate(pooled[2 * i:2 * i + 2], axis=-1).astype(BF16), wpool_ref[i])
           for i in range(len(POOL_WINDOWS) // 2)]
    pool_y = jnp.concatenate(pys, axis=-1) * pscale_ref[...]
    if chained:
        ext_ref[0:POOL_ROWS, :] = ext_ref[tb:tb + POOL_ROWS, :]
    else:
        for c in range(n_chunks):
            pool_out_ref[c] = u[(c + 1) * CHUNK - POOL_ROWS:(c + 1) * CHUNK]

    gpre = _dot_tn(z.astype(BF16), wgu_ref[...]) + bgate_ref[...]
    log_a = (jnp.minimum(gpre, 0.0) - jnp.log(1.0 + jnp.exp(-jnp.abs(gpre)))) * (1.0 / GATE_TAU)
    if lead_pad:
        log_a = jnp.where(row >= lead_pad, log_a, 0.0)
    a_hi = log_a.astype(BF16)
    a_lo = (log_a - a_hi.astype(F32)).astype(BF16)
    tril = tril_ref[...]
    seg = tril.shape[0]
    bcum = jnp.concatenate(
        [_dot(tril, a_hi[r0:r0 + seg]) + _dot(tril, a_lo[r0:r0 + seg]) for r0 in range(0, tb, seg)], axis=0)
    front_tiles(1)
    eb = jnp.exp(bcum)
    qi = q * (GLA_DK ** -0.5) * eb
    ki = k * jnp.exp(-bcum)

    rr = lax.broadcasted_iota(I32, (CHUNK, HEAD_GROUP * CHUNK), 0)
    cc = lax.broadcasted_iota(I32, (CHUNK, HEAD_GROUP * CHUNK), 1)
    causal = (cc % CHUNK) <= rr

    lasts = [eb[(c + 1) * CHUNK - 1:(c + 1) * CHUNK, :] for c in range(n_chunks)]
    dcol = jnp.concatenate(lasts + [jnp.zeros((LANES - n_chunks, D_QK), F32)], axis=0).T

    n_grp = GLA_HEADS // HEAD_GROUP
    chunk_rows = [slice(c * CHUNK, (c + 1) * CHUNK) for c in range(n_chunks)]
    grp_k = [slice(g * HEAD_GROUP * GLA_DK, (g + 1) * HEAD_GROUP * GLA_DK) for g in range(n_grp)]
    grp_v = [slice(g * HEAD_GROUP * GLA_DV, (g + 1) * HEAD_GROUP * GLA_DV) for g in range(n_grp)]
    qi_b = qi.astype(BF16)
    ki_b = ki.astype(BF16)
    v_b = v.astype(BF16)

    scores = {}
    for c in range(n_chunks):
        for hh in range(GLA_HEADS):
            gg, hp = divmod(hh, HEAD_GROUP)
            kbd_ref[c, gg, hp * CHUNK:(hp + 1) * CHUNK, hp * GLA_DK:(hp + 1) * GLA_DK] = (
                ki_b[chunk_rows[c], hh * GLA_DK:(hh + 1) * GLA_DK])
            vbd_ref[c, gg, hp * CHUNK:(hp + 1) * CHUNK, hp * GLA_DV:(hp + 1) * GLA_DV] = (
                v_b[chunk_rows[c], hh * GLA_DV:(hh + 1) * GLA_DV])
        for gg in range(n_grp):
            scores[c, gg] = _dot_nt(qi_b[chunk_rows[c], grp_k[gg]], kbd_ref[c, gg])

    kvs = {}
    for c in range(n_chunks):
        for hh in range(GLA_HEADS):
            kvs[c, hh] = _dot_tn(ki_b[chunk_rows[c], hh * GLA_DK:(hh + 1) * GLA_DK],
                                 v_b[chunk_rows[c], hh * GLA_DV:(hh + 1) * GLA_DV])
    front_tiles(2)

    for hh in range(GLA_HEADS):
        st = st_ref[hh] if chained else None
        for c in range(n_chunks):
            s_old = st if chained else st0_ref[c, hh]
            s_new = (s_old + kvs[c, hh]) * dcol[hh * GLA_DK:(hh + 1) * GLA_DK, c:c + 1]
            if not chained:
                st_out_ref[c, hh] = s_new
            else:
                st = s_new
                if c + 1 < n_chunks:
                    put_state(c + 1, hh, s_new)
        if chained:
            st_ref[hh] = st

    for c in range(n_chunks):
        for gg in range(n_grp):
            p = jnp.where(causal, scores[c, gg], 0.0).astype(BF16)
            o_ref[chunk_rows[c], grp_v[gg]] = (
                _dot(p, vbd_ref[c, gg]) + _dot(qi_b[chunk_rows[c], grp_k[gg]], sbd_ref[c, gg]))

    o = o_ref[...]
    ons = []
    for hh in range(GLA_HEADS):
        oh = o[:, hh * GLA_DV:(hh + 1) * GLA_DV]
        ons.append(oh * lax.rsqrt(jnp.mean(oh * oh, axis=-1, keepdims=True) + EPS))
    og = jnp.concatenate(ons, axis=-1) * gnorm_ref[...] * (r * jax.nn.sigmoid(r))
    mix = _dot(jnp.concatenate([pool_y, og], axis=-1).astype(BF16), wout_ref[...])
    front_tiles(3)

    eg_out_ref[...] = eg_ref[...].astype(BF16)
    eu_out_ref[...] = eu_ref[...].astype(BF16)
    ed_out_ref[...] = ed_ref[...].astype(BF16)

    h = x + mix
    h_ref[...] = h
    hn = _rms(h, nffn_ref[...]).astype(BF16)
    _pack_planes(hn, hn_ref)

    logits = _dot_nt(wr_ref[...], hn) + br_ref[...]
    front_tiles(4)
    sub = lax.broadcasted_iota(I32, (EXPERTS_PER_GROUP, tb), 0).astype(F32)
    neg = jnp.float32(-jnp.inf)
    big = jnp.float32(EXPERTS_PER_GROUP)
    tile0 = logits[0:EXPERTS_PER_GROUP]
    is_g = sub < N_GROUPS
    gmax = jnp.max(jnp.where(is_g, tile0, neg), axis=0, keepdims=True)
    gsum = jnp.sum(jnp.where(is_g, jnp.exp(tile0 - gmax), 0.0), axis=0, keepdims=True)
    p_g = 1.0 / gsum
    gidx = jnp.min(jnp.where(is_g & (tile0 == gmax), sub, big), axis=0, keepdims=True)
    el = logits[N_GROUPS * EXPERTS_PER_GROUP:(N_GROUPS + 1) * EXPERTS_PER_GROUP]
    for g in range(N_GROUPS - 2, -1, -1):
        el = jnp.where(gidx == g, logits[(g + 1) * EXPERTS_PER_GROUP:(g + 2) * EXPERTS_PER_GROUP], el)
    m1 = jnp.max(el, axis=0, keepdims=True)
    i1 = jnp.min(jnp.where(el == m1, sub, big), axis=0, keepdims=True)
    rest = sub != i1
    m2 = jnp.max(jnp.where(rest, el, neg), axis=0, keepdims=True)
    i2 = jnp.min(jnp.where(rest & (el == m2), sub, big), axis=0, keepdims=True)
    t2 = jnp.exp(m2 - m1)
    den = 1.0 + t2
    g1 = p_g / den
    g2 = p_g * t2 / den
    e1 = gidx * EXPERTS_PER_GROUP + i1
    e2 = gidx * EXPERTS_PER_GROUP + i2

    eid = lax.broadcasted_iota(I32, (N_EXPERTS, tb), 0).astype(F32)
    oh1 = eid == e1
    oh2 = eid == e2
    both = jnp.where(oh1 | oh2, 1.0, 0.0)
    cnt = cnt_ref[...]
    before = _dot(both.astype(BF16), sup_ref[...]) + cnt
    front_tiles(5)
    assert tiles_done[0] * FRONT_TILE == D_MAIN
    pos1 = jnp.sum(jnp.where(oh1, before, 0.0), axis=0, keepdims=True)
    pos2 = jnp.sum(jnp.where(oh2, before, 0.0), axis=0, keepdims=True)
    cnt_new = cnt + jnp.sum(both, axis=1, keepdims=True)
    cnt_ref[...] = cnt_new
    cnt_out_ref[...] = cnt_new

    zero = jnp.zeros_like(e1)
    route_t = jnp.concatenate([e1, e2, g1, g2, pos1, pos2, zero, zero], axis=0)
    route_t_ref[0] = route_t
    route_ref[...] = jnp.concatenate([route_t, jnp.zeros((LANES - ROUTE_ROWS, tb), F32)], axis=0).T

    if chained:
        @pl.when(j == nj - 1)
        def _():
            pool_out_ref[0] = ext_ref[0:POOL_ROWS, :]
            st_out_ref[0] = st_ref[...]


def _mixer(x2d, pool0, st0, cnt0, weights, *, batch, seq, tb, lead_pad, experts=None):
    chained = tb <= seq
    total_rows = batch * seq
    n_blk = total_rows // tb
    nj = seq // tb if chained else 1
    per_blk = 1 if chained else tb // seq
    assert chained or (seq == CHUNK and batch % per_blk == 0)
    shared = pool0.shape[0] == 1
    front = lambda s: jnp.minimum(s, n_blk - 1)
    back = lambda s: jnp.maximum(s - 1, 0)
    stream = lambda s: back(s) // nj
    st_idx = (lambda s: (0, 0, 0)) if shared else (lambda s: (stream(s), 0, 0))
    gla_idx = (lambda s: (0, 0, 0, 0)) if shared else (lambda s: (stream(s), 0, 0, 0))
    const2 = lambda s: (0, 0)
    tok_out = lambda s: (back(s), 0)

    seg = min(tb, MXU_DIM)
    ii = jnp.arange(seg)
    tril = ((ii[:, None] >= ii[None, :]) & (ii[:, None] // CHUNK == ii[None, :] // CHUNK)).astype(BF16)
    ii = jnp.arange(tb)
    sup = (ii[:, None] < ii[None, :]).astype(BF16)

    in_specs = [
        pl.BlockSpec((tb, D_MODEL), lambda s: (front(s), 0)),
        pl.BlockSpec((per_blk, POOL_ROWS, D_POOL), st_idx),
        pl.BlockSpec((per_blk, GLA_HEADS, GLA_DK, GLA_DV), gla_idx),
        pl.BlockSpec((N_EXPERTS, 1), const2),
        pl.BlockSpec((seg, seg), const2),
        pl.BlockSpec((tb, tb), const2),
    ]
    for wgt in weights:
        in_specs.append(pl.BlockSpec(wgt.shape, (lambda s, n=wgt.ndim: (0,) * n)))
    if experts is None:
        experts = (jnp.zeros((1, 8, LANES), F32),) * 3
        per = 1
        cast_idx = lambda s: (0, 0, 0)
    else:
        per = n_blk // N_EXPERTS
        assert n_blk == per * N_EXPERTS and all(w.shape[1] % (8 * per) == 0 for w in experts)
        cast_idx = lambda s: (front(s) // per, front(s) % per, 0)
    cast_specs = [pl.BlockSpec((1, w.shape[1] // per, w.shape[2]), cast_idx) for w in experts]
    in_specs += cast_specs
    args = [x2d, pool0, st0, cnt0, tril, sup, *weights, *experts]

    out_shape = [
        jax.ShapeDtypeStruct((total_rows, D_MODEL), F32),
        jax.ShapeDtypeStruct((HN_PLANES, total_rows, LANES), I32),
        jax.ShapeDtypeStruct((total_rows, LANES), F32),
        jax.ShapeDtypeStruct((n_blk, ROUTE_ROWS, tb), F32),
        jax.ShapeDtypeStruct((batch, POOL_ROWS, D_POOL), F32),
        jax.ShapeDtypeStruct((batch, GLA_HEADS, GLA_DK, GLA_DV), F32),
        jax.ShapeDtypeStruct((N_EXPERTS, 1), F32),
        *[jax.ShapeDtypeStruct(w.shape, BF16) for w in experts],
    ]
    out_specs = [
        pl.BlockSpec((tb, D_MODEL), tok_out),
        pl.BlockSpec((HN_PLANES, tb, LANES), lambda s: (0, back(s), 0)),
        pl.BlockSpec((tb, LANES), tok_out),
        pl.BlockSpec((1, ROUTE_ROWS, tb), lambda s: (back(s), 0, 0)),
        pl.BlockSpec((per_blk, POOL_ROWS, D_POOL), lambda s: (stream(s), 0, 0)),
        pl.BlockSpec((per_blk, GLA_HEADS, GLA_DK, GLA_DV), lambda s: (stream(s), 0, 0, 0)),
        pl.BlockSpec((N_EXPERTS, 1), const2),
        *cast_specs,
    ]
    n_grp = GLA_HEADS // HEAD_GROUP
    scratch = [
        pltpu.VMEM((POOL_ROWS + tb, D_POOL), F32),
        pltpu.VMEM((GLA_HEADS, GLA_DK, GLA_DV), F32),
        pltpu.VMEM((tb // CHUNK, n_grp, HEAD_GROUP * CHUNK, HEAD_GROUP * GLA_DK), BF16),
        pltpu.VMEM((tb // CHUNK, n_grp, HEAD_GROUP * CHUNK, HEAD_GROUP * GLA_DV), BF16),
        pltpu.VMEM((tb // CHUNK, n_grp, HEAD_GROUP * GLA_DK, HEAD_GROUP * GLA_DV), BF16),
        pltpu.VMEM((tb, D_V), F32),
        pltpu.VMEM((N_EXPERTS, 1), F32),
        pltpu.VMEM((2, tb, D_MODEL), F32),
        pltpu.VMEM((2, tb, D_MAIN), F32),
        pltpu.VMEM((2, GATE_RANK, tb), F32),
    ]
    return pl.pallas_call(
        functools.partial(_mixer_kernel, tb=tb, nj=nj, lead_pad=lead_pad, chained=chained),
        grid=(n_blk + 1,),
        in_specs=in_specs,
        out_specs=out_specs,
        out_shape=out_shape,
        scratch_shapes=scratch,
        compiler_params=pltpu.CompilerParams(
            dimension_semantics=("arbitrary",), vmem_limit_bytes=VMEM_LIMIT),
        name=f"mixer_tb{tb}_pad{lead_pad}",
    )(*args)


def _transpose_cast_kernel(wt_ref, w_ref):
    w_ref[...] = wt_ref[...].T.astype(BF16)


def _transpose_cast(w_t, n_cols):
    k = w_t.shape[1]
    assert n_cols % WEIGHT_TILE == 0 and n_cols <= w_t.shape[0]
    return pl.pallas_call(
        _transpose_cast_kernel,
        grid=(n_cols // WEIGHT_TILE,),
        in_specs=[pl.BlockSpec((WEIGHT_TILE, k), lambda i: (i, 0))],
        out_specs=pl.BlockSpec((k, WEIGHT_TILE), lambda i: (0, i)),
        out_shape=jax.ShapeDtypeStruct((k, n_cols), BF16),
        compiler_params=pltpu.CompilerParams(dimension_semantics=("arbitrary",)),
        name="weight_transpose_cast",
    )(w_t)


def _sc_mesh():
    return plsc.VectorSubcoreMesh(core_axis_name="c", subcore_axis_name="s",
                                  num_cores=SC_CORES, num_subcores=SC_SUBCORES)


def _sc_worker():
    return lax.axis_index("s") * SC_CORES + lax.axis_index("c")


def _plane_rows(dest, planes, rows_per_plane):
    offs = (jnp.arange(planes, dtype=I32) * rows_per_plane)[None, :, None]
    return dest[:, None, :] + offs


def _dispatch(hn_p, hn_s, dest0, dest1, n_rows):
    planes, t_p, _ = hn_p.shape
    t_s = hn_s.shape[1]
    n_cp = t_p // SC_ROWS
    cp = n_cp // SC_WORKERS
    n_cs = t_s // SC_ROWS
    assert t_p == cp * SC_ROWS * SC_WORKERS and t_s == n_cs * SC_ROWS and n_cs <= SC_WORKERS
    half = planes // 2
    assert planes == 2 * half
    idx0 = _plane_rows(dest0, planes, n_rows)
    idx1 = _plane_rows(dest1, planes, n_rows)

    def body(hnp_hbm, hns_hbm, d0_hbm, d1_hbm, xs_hbm, rows_v, i0_v, i1_v, is0_v, is1_v, sem_in, sem_out):
        wid = _sc_worker()
        pltpu.sync_copy(d0_hbm.at[pl.ds(wid * cp, cp)], i0_v)
        pltpu.sync_copy(d1_hbm.at[pl.ds(wid * cp, cp)], i1_v)

        def start_loads(src_hbm, src_rows, row0, h):
            return [pltpu.async_copy(src_hbm.at[pl.ds((h * half + q) * src_rows + row0, SC_ROWS)],
                                     rows_v.at[h, q], sem_in) for q in range(half)]

        def start_stores(h, i0, i1, c):
            stores = []
            for q in range(half):
                stores.append(pltpu.async_copy(rows_v.at[h, q], xs_hbm.at[i0.at[c, h * half + q]], sem_out))
                stores.append(pltpu.async_copy(rows_v.at[h, q], xs_hbm.at[i1.at[c, h * half + q]], sem_out))
            return stores

        def wait_all(copies):
            for cpy in copies:
                cpy.wait()

        def move(src_hbm, src_rows, row0_of, i0, i1, n):
            loads = start_loads(src_hbm, src_rows, row0_of(0), 0)
            stores = []
            for u in range(2 * n):
                c, h = divmod(u, 2)
                wait_all(loads)
                wait_all(stores)
                if u + 1 < 2 * n:
                    loads = start_loads(src_hbm, src_rows, row0_of((u + 1) // 2), (u + 1) % 2)
                stores = start_stores(h, i0, i1, c)
            wait_all(stores)

        move(hnp_hbm, t_p, lambda c: (wid * cp + c) * SC_ROWS, i0_v, i1_v, cp)

        @pl.when(wid < n_cs)
        def _():
            pltpu.sync_copy(d0_hbm.at[pl.ds(n_cp + wid, 1)], is0_v)
            pltpu.sync_copy(d1_hbm.at[pl.ds(n_cp + wid, 1)], is1_v)
            move(hns_hbm, t_s, lambda c: wid * SC_ROWS, is0_v, is1_v, 1)

    xs = pl.kernel(
        body,
        out_type=jax.ShapeDtypeStruct((planes * n_rows, LANES), I32),
        mesh=_sc_mesh(),
        scratch_types=[
            pltpu.VMEM((2, half, SC_ROWS, LANES), I32),
            pltpu.VMEM((cp, planes, SC_ROWS), I32),
            pltpu.VMEM((cp, planes, SC_ROWS), I32),
            pltpu.VMEM((1, planes, SC_ROWS), I32),
            pltpu.VMEM((1, planes, SC_ROWS), I32),
            pltpu.SemaphoreType.DMA,
            pltpu.SemaphoreType.DMA,
        ],
        name="moe_dispatch_sc",
    )(hn_p.reshape(planes * t_p, LANES), hn_s.reshape(planes * t_s, LANES), idx0, idx1)
    return xs.reshape(planes, n_rows, LANES)


def _gather(ys, dest0, dest1):
    planes, n_rows, _ = ys.shape
    n_chunks = dest0.shape[0]
    n_tok = n_chunks * SC_ROWS
    cpw = max(n_chunks // SC_WORKERS, 1)
    assert n_chunks <= SC_WORKERS or n_chunks == cpw * SC_WORKERS
    half = planes // 2
    assert planes == 2 * half
    idx = (_plane_rows(dest0, planes, n_rows), _plane_rows(dest1, planes, n_rows))

    def body(ys_hbm, d0_hbm, d1_hbm, out_hbm, rows_v, i0_v, i1_v, sem_in, sem_out):
        wid = _sc_worker()

        def work():
            pltpu.sync_copy(d0_hbm.at[pl.ds(wid * cpw, cpw)], i0_v)
            pltpu.sync_copy(d1_hbm.at[pl.ds(wid * cpw, cpw)], i1_v)

            units = [(c, j, h) for c in range(cpw) for j in range(2) for h in range(2)]

            def start_loads(c, j, h):
                i_v = (i0_v, i1_v)[j]
                return [pltpu.async_copy(ys_hbm.at[i_v.at[c, h * half + q]], rows_v.at[h, q], sem_in)
                        for q in range(half)]

            def start_stores(c, j, h):
                row0 = (wid * cpw + c) * SC_ROWS
                return [pltpu.async_copy(
                    rows_v.at[h, q],
                    out_hbm.at[pl.ds((j * planes + h * half + q) * n_tok + row0, SC_ROWS)], sem_out)
                    for q in range(half)]

            loads = start_loads(*units[0])
            stores = []
            for u, unit in enumerate(units):
                for cpy in loads:
                    cpy.wait()
                for cpy in stores:
                    cpy.wait()
                if u + 1 < len(units):
                    loads = start_loads(*units[u + 1])
                stores = start_stores(*unit)
            for cpy in stores:
                cpy.wait()

        if n_chunks < SC_WORKERS:
            pl.when(wid < n_chunks)(work)
        else:
            work()

    out = pl.kernel(
        body,
        out_type=jax.ShapeDtypeStruct((2 * planes * n_tok, LANES), ys.dtype),
        mesh=_sc_mesh(),
        scratch_types=[
            pltpu.VMEM((2, half, SC_ROWS, LANES), ys.dtype),
            pltpu.VMEM((cpw, planes, SC_ROWS), I32),
            pltpu.VMEM((cpw, planes, SC_ROWS), I32),
            pltpu.SemaphoreType.DMA,
            pltpu.SemaphoreType.DMA,
        ],
        name="moe_gather_sc",
    )(ys.reshape(planes * n_rows, LANES), *idx)
    return out.reshape(2, planes, n_tok, LANES)


def _expert_kernel(be_ref, bv_ref, nb_ref, xs_ref, wg_ref, wu_ref, wd_ref, ys_ref):
    del be_ref
    i = pl.program_id(0)
    live = i < nb_ref[0]

    @pl.when(live)
    def _():
        valid = lax.broadcasted_iota(I32, (MOE_BLK, LANES), 0) < bv_ref[i]
        xb = _unpack_planes([jnp.where(valid, xs_ref[p], 0) for p in range(HN_PLANES)]).astype(BF16)
        gate = _dot(xb, wg_ref[0])
        hmid = gate * jax.nn.sigmoid(gate) * _dot(xb, wu_ref[0])
        _pack_planes(_dot(hmid.astype(BF16), wd_ref[0]), ys_ref)


def _experts(block_e, block_valid, nb, xs, w_eg, w_eu, w_ed):
    n_rows = xs.shape[1]
    n_blocks = n_rows // MOE_BLK
    row_idx = lambda i, be, bv, nb: (0, jnp.minimum(i, nb[0] - 1), 0)
    w_idx = lambda i, be, bv, nb: (be[i], 0, 0)
    grid_spec = pltpu.PrefetchScalarGridSpec(
        num_scalar_prefetch=3,
        grid=(n_blocks,),
        in_specs=[
            pl.BlockSpec((HN_PLANES, MOE_BLK, LANES), row_idx),
            pl.BlockSpec((1, D_MODEL, D_EXPERT), w_idx),
            pl.BlockSpec((1, D_MODEL, D_EXPERT), w_idx),
            pl.BlockSpec((1, D_EXPERT, D_MODEL), w_idx),
        ],
        out_specs=pl.BlockSpec((HN_PLANES, MOE_BLK, LANES), row_idx),
    )
    return pl.pallas_call(
        _expert_kernel,
        grid_spec=grid_spec,
        out_shape=jax.ShapeDtypeStruct((HN_PLANES, n_rows, LANES), I32),
        compiler_params=pltpu.CompilerParams(
            dimension_semantics=("arbitrary",), vmem_limit_bytes=VMEM_LIMIT),
        name="moe_experts",
    )(block_e, block_valid, nb, xs, w_eg, w_eu, w_ed)


def _combine_kernel(h_ref, route_ref, nfin_ref, rows_ref, *rest):
    y_ref = rest[-1]
    route = route_ref[...]
    ys1 = _unpack_planes([rows_ref[0, p] for p in range(HN_PLANES)])
    ys2 = _unpack_planes([rows_ref[1, p] for p in range(HN_PLANES)])
    out = h_ref[...] + (ys1 * route[:, 2:3] + ys2 * route[:, 3:4])
    y_ref[...] = _rms(out, nfin_ref[...])


def _combine(h, route, norm_final, rows, *, row0=0, y_prev=None):
    total = h.shape[0]
    tb = COMBINE_TB
    blk0 = row0 // tb
    args = [h, route, norm_final, rows]
    in_specs = [
        pl.BlockSpec((tb, D_MODEL), lambda i: (blk0 + i, 0)),
        pl.BlockSpec((tb, LANES), lambda i: (blk0 + i, 0)),
        pl.BlockSpec((1, D_MODEL), lambda i: (0, 0)),
        pl.BlockSpec((2, HN_PLANES, tb, LANES), lambda i: (0, 0, i, 0)),
    ]
    aliases = {}
    if y_prev is not None:
        aliases[len(args)] = 0
        args.append(y_prev)
        in_specs.append(pl.BlockSpec(memory_space=pl.ANY))
    return pl.pallas_call(
        _combine_kernel,
        grid=(rows.shape[2] // tb,),
        in_specs=in_specs,
        out_specs=pl.BlockSpec((tb, D_MODEL), lambda i: (blk0 + i, 0)),
        out_shape=jax.ShapeDtypeStruct((total, D_MODEL), F32),
        input_output_aliases=aliases,
        compiler_params=pltpu.CompilerParams(
            dimension_semantics=("arbitrary",), vmem_limit_bytes=VMEM_LIMIT),
        name="moe_combine",
    )(*args)


def kernel(x_prompt, x_sample, state_pool, state_gla, meta_tokens, norm_mix, w_in, w_gate_up, b_gate, w_pool, pool_scale, gla_norm, w_out, norm_ffn, w_router_group, b_router_group, w_router_expert, b_router_expert, w_expert_gate, w_expert_up, w_expert_down, norm_final):
    assert w_in.shape[0] == 1, "one encoder layer"
    batch, seq, _ = x_prompt.shape
    dec_batch, dec_seq, _ = x_sample.shape
    assert seq % MIX_TB == 0 and dec_seq == CHUNK and N_META <= CHUNK
    t_prompt = batch * seq
    t_sample = dec_batch * dec_seq
    t_all = t_prompt + t_sample
    assert t_prompt % COMBINE_TB == 0 and t_sample % COMBINE_TB == 0

    w_in_t = w_in[0].T
    gpad = EXPERTS_PER_GROUP - N_GROUPS
    rpad = ROUTER_ROWS - EXPERTS_PER_GROUP - N_EXPERTS
    w_router = jnp.concatenate([
        w_router_group[0].T, jnp.zeros((gpad, D_MODEL), F32),
        w_router_expert[0].T, jnp.zeros((rpad, D_MODEL), F32)], axis=0)
    b_router = jnp.concatenate([
        b_router_group[0], jnp.zeros((gpad,), F32), b_router_expert[0], jnp.zeros((rpad,), F32)])
    zg = jnp.zeros((POOL_GROUP_DIM, POOL_GROUP_DIM), F32)
    w_pool_pairs = jnp.stack([
        jnp.block([[w_pool[0, 2 * i], zg], [zg, w_pool[0, 2 * i + 1]]]) for i in range(len(POOL_WINDOWS) // 2)])
    weights = (
        norm_mix[0][None, :],
        _transpose_cast(w_in_t, D_MAIN),
        w_in_t[D_MAIN:].astype(BF16),
        w_gate_up[0].astype(BF16),
        b_gate[0][None, :],
        w_pool_pairs.astype(BF16),
        pool_scale[0][None, :],
        gla_norm[0][None, :],
        w_out[0].astype(BF16),
        norm_ffn[0][None, :],
        w_router.astype(BF16),
        b_router[:, None],
    )

    zero_cnt = jnp.zeros((N_EXPERTS, 1), F32)
    x_meta = jnp.pad(meta_tokens.astype(F32), ((CHUNK - N_META, 0), (0, 0)))
    meta = _mixer(x_meta, jnp.zeros((1, POOL_ROWS, D_POOL), F32),
                  jnp.zeros((1, GLA_HEADS, GLA_DK, GLA_DV), F32), zero_cnt,
                  weights, batch=1, seq=CHUNK, tb=CHUNK, lead_pad=CHUNK - N_META)
    pool_m, st_m = meta[4], meta[5]
    h_p, hn_p, route_p, rt_p, pool_p, st_p, cnt_p, w_eg, w_eu, w_ed = _mixer(
        x_prompt.reshape(t_prompt, D_MODEL), pool_m, st_m, zero_cnt, weights,
        batch=batch, seq=seq, tb=MIX_TB, lead_pad=0,
        experts=(w_expert_gate[0], w_expert_up[0], w_expert_down[0]))
    pool_s0 = jnp.pad(state_pool[0], ((0, 0), (POOL_ROWS - POOL_PAD, 0), (0, 0)))
    h_s, hn_s, route_s, rt_s, pool_s, st_s, cnt_s = _mixer(
        x_sample.reshape(t_sample, D_MODEL), pool_s0, state_gla[0].astype(F32),
        cnt_p, weights, batch=dec_batch, seq=dec_seq, tb=SAMPLE_TB, lead_pad=0)[:7]

    counts = cnt_s[:, 0].astype(I32)
    padded = (counts + MOE_BLK - 1) // MOE_BLK * MOE_BLK
    ends = jnp.cumsum(padded)
    pstart = ends - padded
    n_blocks = (2 * t_all + N_EXPERTS * (MOE_BLK - 1) + MOE_BLK - 1) // MOE_BLK
    nb = (ends[-1] // MOE_BLK).astype(I32)
    blk_ids = jnp.minimum(jnp.arange(n_blocks, dtype=I32), nb - 1)
    block_e = jnp.sum((ends[None, :] <= (blk_ids * MOE_BLK)[:, None]).astype(I32), axis=1)
    block_e = jnp.minimum(block_e, N_EXPERTS - 1)
    owner = block_e[:, None] == jnp.arange(N_EXPERTS, dtype=I32)
    row_end = jnp.sum(jnp.where(owner, pstart + counts, 0), axis=1)
    block_valid = jnp.clip(row_end - blk_ids * MOE_BLK, 0, MOE_BLK)

    def dest_rows(rt):
        rt = rt.transpose(1, 0, 2).reshape(ROUTE_ROWS, -1)
        onehot = rt[0:2].astype(I32)[..., None] == jnp.arange(N_EXPERTS, dtype=I32)
        return jnp.sum(jnp.where(onehot, pstart, 0), axis=-1) + rt[4:6].astype(I32)

    dest = jnp.concatenate([dest_rows(rt_p), dest_rows(rt_s)], axis=1)
    dest0 = dest[0].reshape(t_all // SC_ROWS, SC_ROWS)
    dest1 = dest[1].reshape(t_all // SC_ROWS, SC_ROWS)
    xs = _dispatch(hn_p, hn_s, dest0, dest1, n_blocks * MOE_BLK)
    ys = _experts(block_e, block_valid.astype(I32), nb[None], xs, w_eg, w_eu, w_ed)
    nfin = norm_final[None, :]
    cp_chunks = t_prompt // SC_ROWS
    part = cp_chunks // COMBINE_PARTS
    y_prompt = None
    for i in range(COMBINE_PARTS):
        ch = slice(i * part, (i + 1) * part)
        rows_i = _gather(ys, dest0[ch], dest1[ch])
        y_prompt = _combine(h_p, route_p, nfin, rows_i, row0=i * part * SC_ROWS, y_prev=y_prompt)
    y_prompt = y_prompt.reshape(batch, seq, D_MODEL)
    rows_s = _gather(ys, dest0[cp_chunks:], dest1[cp_chunks:])
    y_sample = _combine(h_s, route_s, nfin, rows_s).reshape(dec_batch, dec_seq, D_MODEL)
    new_pool_prompt = pool_p[:, POOL_ROWS - POOL_PAD:][None]
    new_gla_prompt = st_p[None]
    new_pool_sample = pool_s[:, POOL_ROWS - POOL_PAD:][None]
    new_gla_sample = st_s[None]
    return (y_prompt, y_sample, new_pool_prompt, new_gla_prompt, new_pool_sample, new_gla_sample)
```
